```python
import math
import jax, jax.numpy as jnp
from jax import lax
import numpy as np

D_MODEL = 1024
BATCH = 16
SEQ = 2048
DEPTH = 4

CHUNK = 64
N_MIXERS = 2
N_HEADS_A = 16
HEAD_DIM_A = D_MODEL // N_HEADS_A
LEFT_CHUNKS = 8
BAND = (LEFT_CHUNKS + 1) * CHUNK
REL_CLIP = 128
N_HEADS_B = 16
HEAD_DIM_B = D_MODEL // N_HEADS_B
SB_BLOCK = 128
D_FF = 2816
ALPHA = (2.0 * DEPTH) ** 0.25
BETA = (8.0 * DEPTH) ** -0.25
LN_EPS = 1e-5
N_A_LAYERS = (DEPTH + N_MIXERS - 1) // N_MIXERS
N_B_LAYERS = DEPTH // N_MIXERS

kernel_name = "hybrid_chunked_relbias_stickbreaking_macaron_deepnorm"


def layer_norm(x, g, b):
    xf = x.astype(jnp.float32)
    mu = jnp.mean(xf, axis=-1, keepdims=True)
    var = jnp.mean(jnp.square(xf - mu), axis=-1, keepdims=True)
    y = (xf - mu) * lax.rsqrt(var + LN_EPS)
    return (y * g.astype(jnp.float32) + b.astype(jnp.float32)).astype(x.dtype)


def swiglu(x, w_gate, w_up, w_down):
    return (jax.nn.silu(x @ w_gate) * (x @ w_up)) @ w_down


def rel_bias_band(rel_table):
    i = np.arange(CHUNK)[:, None]
    j = np.arange(BAND)[None, :]
    rel = (np.clip(i - j + LEFT_CHUNKS * CHUNK, -REL_CLIP, REL_CLIP) + REL_CLIP).astype(np.int32)
    return jnp.transpose(rel_table[rel], (2, 0, 1))


def chunked_rel_attention(x, w_qkv, w_o, rel_table):
    B, S, _ = x.shape
    n_chunks = S // CHUNK
    qkv = (x @ w_qkv).reshape(B, S, 3, N_HEADS_A, HEAD_DIM_A)
    q, k, v = qkv[:, :, 0], qkv[:, :, 1], qkv[:, :, 2]
    pad = LEFT_CHUNKS * CHUNK
    k_pad = jnp.pad(k, ((0, 0), (pad, 0), (0, 0), (0, 0)))
    v_pad = jnp.pad(v, ((0, 0), (pad, 0), (0, 0), (0, 0)))
    bias = rel_bias_band(rel_table).astype(jnp.float32)
    key_offset = jnp.arange(BAND) - pad
    scale = HEAD_DIM_A ** -0.5

    def one_chunk(c):
        start = c * CHUNK
        q_c = lax.dynamic_slice_in_dim(q, start, CHUNK, axis=1)
        k_c = lax.dynamic_slice_in_dim(k_pad, start, BAND, axis=1)
        v_c = lax.dynamic_slice_in_dim(v_pad, start, BAND, axis=1)
        s = jnp.einsum('bqhd,bkhd->bhqk', q_c, k_c,
                       preferred_element_type=jnp.float32) * scale + bias
        valid = (start + key_offset) >= 0
        s = jnp.where(valid[None, None, None, :], s, -jnp.inf)
        p = jax.nn.softmax(s, axis=-1).astype(v_c.dtype)
        return jnp.einsum('bhqk,bkhd->bqhd', p, v_c)

    out = lax.map(one_chunk, jnp.arange(n_chunks))
    out = jnp.transpose(out, (1, 0, 2, 3, 4)).reshape(B, S, D_MODEL)
    return out @ w_o


def stick_breaking_attention(x, w_qkv, w_o):
    B, S, _ = x.shape
    qkv = (x @ w_qkv).reshape(B, S, 3, N_HEADS_B, HEAD_DIM_B)
    q, k, v = qkv[:, :, 0], qkv[:, :, 1], qkv[:, :, 2]
    scale = HEAD_DIM_B ** -0.5
    outs = []
    for blk in range(S // SB_BLOCK):
        q0 = blk * SB_BLOCK
        kv_len = q0 + SB_BLOCK
        q_b = q[:, q0:kv_len]
        k_b = k[:, :kv_len]
        v_b = v[:, :kv_len]
        z = jnp.einsum('bqhd,bkhd->bhqk', q_b, k_b,
                       preferred_element_type=jnp.float32) * scale
        t_pos = q0 + jnp.arange(SB_BLOCK)[:, None]
        s_pos = jnp.arange(kv_len)[None, :]
        causal = s_pos < t_pos
        log_beta = jnp.where(causal, jax.nn.log_sigmoid(z), -jnp.inf)
        log_1m_beta = jnp.where(causal, jax.nn.log_sigmoid(-z), 0.0)
        stick = lax.cumsum(log_1m_beta, axis=log_1m_beta.ndim - 1, reverse=True) - log_1m_beta
        a = jnp.exp(log_beta + stick).astype(v_b.dtype)
        outs.append(jnp.einsum('bhqk,bkhd->bqhd', a, v_b))
    out = jnp.concatenate(outs, axis=1).reshape(B, S, D_MODEL)
    return out @ w_o


def _fwd_setup_inputs(seed: int = 0) -> dict:
    key = jax.random.key(seed)
    ks = jax.random.split(key, 16)
    d_sc = D_MODEL ** -0.5
    f_sc = D_FF ** -0.5
    x = jax.random.normal(ks[0], (BATCH, SEQ, D_MODEL), jnp.float32)

    def qkv_weights(k, n):
        kq, kk, kv = jax.random.split(k, 3)
        wq = jax.random.normal(kq, (n, D_MODEL, D_MODEL), jnp.float32) * d_sc
        wk = jax.random.normal(kk, (n, D_MODEL, D_MODEL), jnp.float32) * d_sc
        wv = jax.random.normal(kv, (n, D_MODEL, D_MODEL), jnp.float32) * (d_sc * BETA)
        return jnp.concatenate([wq, wk, wv], axis=-1)

    w_qkv_a = qkv_weights(ks[1], N_A_LAYERS)
    w_o_a = jax.random.normal(ks[2], (N_A_LAYERS, D_MODEL, D_MODEL), jnp.float32) * (d_sc * BETA)
    rel_bias = jax.random.normal(ks[3], (2 * REL_CLIP + 1, N_HEADS_A), jnp.float32) * 0.5
    w_qkv_b = qkv_weights(ks[4], N_B_LAYERS)
    w_o_b = jax.random.normal(ks[5], (N_B_LAYERS, D_MODEL, D_MODEL), jnp.float32) * (d_sc * BETA)
    ffn_w_gate = jax.random.normal(ks[6], (DEPTH, 2, D_MODEL, D_FF), jnp.float32) * d_sc
    ffn_w_up = jax.random.normal(ks[7], (DEPTH, 2, D_MODEL, D_FF), jnp.float32) * d_sc
    ffn_w_down = jax.random.normal(ks[8], (DEPTH, 2, D_FF, D_MODEL), jnp.float32) * (f_sc * BETA)
    ln_g = 1.0 + 0.02 * jax.random.normal(ks[9], (DEPTH, 3, D_MODEL), jnp.float32)
    ln_b = 0.02 * jax.random.normal(ks[10], (DEPTH, 3, D_MODEL), jnp.float32)
    return {"x": x, "w_qkv_a": w_qkv_a, "w_o_a": w_o_a, "rel_bias": rel_bias,
            "w_qkv_b": w_qkv_b, "w_o_b": w_o_b, "ffn_w_gate": ffn_w_gate,
            "ffn_w_up": ffn_w_up, "ffn_w_down": ffn_w_down, "ln_g": ln_g, "ln_b": ln_b}


def _fwd_reference(x, w_qkv_a, w_o_a, rel_bias, w_qkv_b, w_o_b, ffn_w_gate, ffn_w_up,
              ffn_w_down, ln_g, ln_b):
    for i in range(DEPTH):
        h = swiglu(x, ffn_w_gate[i, 0], ffn_w_up[i, 0], ffn_w_down[i, 0])
        x = layer_norm(ALPHA * x + 0.5 * h, ln_g[i, 0], ln_b[i, 0])
        j = i // N_MIXERS
        if i % N_MIXERS == 0:
            y = chunked_rel_attention(x, w_qkv_a[j], w_o_a[j], rel_bias)
        else:
            y = stick_breaking_attention(x, w_qkv_b[j], w_o_b[j])
        x = layer_norm(ALPHA * x + y, ln_g[i, 1], ln_b[i, 1])
        h = swiglu(x, ffn_w_gate[i, 1], ffn_w_up[i, 1], ffn_w_down[i, 1])
        x = layer_norm(ALPHA * x + 0.5 * h, ln_g[i, 2], ln_b[i, 2])
    return x


import jax as _jax
import jax.numpy as _jnp

TWIN_FORMAT = 'train_step'
FWD_PARAMS = ['x', 'w_qkv_a', 'w_o_a', 'rel_bias', 'w_qkv_b', 'w_o_b', 'ffn_w_gate', 'ffn_w_up', 'ffn_w_down', 'ln_g', 'ln_b']
TWIN_WEIGHTS = ['w_qkv_a', 'w_o_a', 'rel_bias', 'w_qkv_b', 'w_o_b', 'ffn_w_gate', 'ffn_w_up', 'ffn_w_down', 'ln_g', 'ln_b']
TWIN_DIFF_INPUT = 'x'
TWIN_INPUTS = ['x', 'w_qkv_a', 'w_o_a', 'rel_bias', 'w_qkv_b', 'w_o_b', 'ffn_w_gate', 'ffn_w_up', 'ffn_w_down', 'ln_g', 'ln_b', 'loss_target', 'm_w_qkv_a', 'm_w_o_a', 'm_rel_bias', 'm_w_qkv_b', 'm_w_o_b', 'm_ffn_w_gate', 'm_ffn_w_up', 'm_ffn_w_down', 'm_ln_g', 'm_ln_b', 'v_w_qkv_a', 'v_w_o_a', 'v_rel_bias', 'v_w_qkv_b', 'v_w_o_b', 'v_ffn_w_gate', 'v_ffn_w_up', 'v_ffn_w_down', 'v_ln_g', 'v_ln_b']
TWIN_OUTPUTS = ['loss', 'grad_x', 'grad_w_qkv_a', 'grad_w_o_a', 'grad_rel_bias', 'grad_w_qkv_b', 'grad_w_o_b', 'grad_ffn_w_gate', 'grad_ffn_w_up', 'grad_ffn_w_down', 'grad_ln_g', 'grad_ln_b', 'delta_w_qkv_a', 'delta_w_o_a', 'delta_rel_bias', 'delta_w_qkv_b', 'delta_w_o_b', 'delta_ffn_w_gate', 'delta_ffn_w_up', 'delta_ffn_w_down', 'delta_ln_g', 'delta_ln_b', 'new_m_w_qkv_a', 'new_m_w_o_a', 'new_m_rel_bias', 'new_m_w_qkv_b', 'new_m_w_o_b', 'new_m_ffn_w_gate', 'new_m_ffn_w_up', 'new_m_ffn_w_down', 'new_m_ln_g', 'new_m_ln_b', 'new_v_w_qkv_a', 'new_v_w_o_a', 'new_v_rel_bias', 'new_v_w_qkv_b', 'new_v_w_o_b', 'new_v_ffn_w_gate', 'new_v_ffn_w_up', 'new_v_ffn_w_down', 'new_v_ln_g', 'new_v_ln_b']
TWIN_LEAF_KINDS = {'loss': 'loss', 'grad_x': 'grad_x', 'grad_w_qkv_a': 'grad_w', 'grad_w_o_a': 'grad_w', 'grad_rel_bias': 'grad_w', 'grad_w_qkv_b': 'grad_w', 'grad_w_o_b': 'grad_w', 'grad_ffn_w_gate': 'grad_w', 'grad_ffn_w_up': 'grad_w', 'grad_ffn_w_down': 'grad_w', 'grad_ln_g': 'grad_w', 'grad_ln_b': 'grad_w', 'delta_w_qkv_a': 'delta_w', 'delta_w_o_a': 'delta_w', 'delta_rel_bias': 'delta_w', 'delta_w_qkv_b': 'delta_w', 'delta_w_o_b': 'delta_w', 'delta_ffn_w_gate': 'delta_w', 'delta_ffn_w_up': 'delta_w', 'delta_ffn_w_down': 'delta_w', 'delta_ln_g': 'delta_w', 'delta_ln_b': 'delta_w', 'new_m_w_qkv_a': 'new_m', 'new_m_w_o_a': 'new_m', 'new_m_rel_bias': 'new_m', 'new_m_w_qkv_b': 'new_m', 'new_m_w_o_b': 'new_m', 'new_m_ffn_w_gate': 'new_m', 'new_m_ffn_w_up': 'new_m', 'new_m_ffn_w_down': 'new_m', 'new_m_ln_g': 'new_m', 'new_m_ln_b': 'new_m', 'new_v_w_qkv_a': 'new_v', 'new_v_w_o_a': 'new_v', 'new_v_rel_bias': 'new_v', 'new_v_w_qkv_b': 'new_v', 'new_v_w_o_b': 'new_v', 'new_v_ffn_w_gate': 'new_v', 'new_v_ffn_w_up': 'new_v', 'new_v_ffn_w_down': 'new_v', 'new_v_ln_g': 'new_v', 'new_v_ln_b': 'new_v'}


def _forward(args):
    return _fwd_reference(*[args[k] for k in FWD_PARAMS])


def _output_shape():
    out = _jax.eval_shape(lambda: _forward(_fwd_setup_inputs(0)))
    return out.shape, out.dtype

N_MICROBATCH = 1
ADAM_LR = 0.001
ADAM_B1 = 0.9
ADAM_B2 = 0.999
ADAM_EPS = 1e-08
ADAM_WD = 0.01
ADAM_STEP = 10
PER_EXAMPLE_BATCH_AXIS = {'x': 0, 'loss_target': 0}
SHARED_INPUTS = []
_WEIGHT_DTYPES = {'w_qkv_a': _jnp.float32, 'w_o_a': _jnp.float32, 'rel_bias': _jnp.float32, 'w_qkv_b': _jnp.float32, 'w_o_b': _jnp.float32, 'ffn_w_gate': _jnp.float32, 'ffn_w_up': _jnp.float32, 'ffn_w_down': _jnp.float32, 'ln_g': _jnp.float32, 'ln_b': _jnp.float32}
MOMENT_SCALE = {'w_qkv_a': 5.064904e-03, 'w_o_a': 7.742445e-03, 'rel_bias': 1.960704e-03, 'w_qkv_b': 1.744031e-02, 'w_o_b': 2.917245e-02, 'ffn_w_gate': 8.284503e-03, 'ffn_w_up': 8.025587e-03, 'ffn_w_down': 3.170916e-02, 'ln_g': 9.337687e+00, 'ln_b': 6.587186e-01}


def _to_microbatches(a, axis):
    t = _jnp.moveaxis(a, axis, 0)
    t = t.reshape((N_MICROBATCH, t.shape[0] // N_MICROBATCH) + t.shape[1:])
    return _jnp.moveaxis(t, 1, axis + 1)


def setup_inputs(seed: int = 0) -> dict:
    inp = _fwd_setup_inputs(seed)
    key = _jax.random.fold_in(_jax.random.key(seed), 7919)
    shape, _ = _output_shape()
    out = dict(inp)
    out["loss_target"] = _jax.random.normal(_jax.random.fold_in(key, 0), shape, _jnp.float32)
    for i, name in enumerate(TWIN_WEIGHTS):
        w = inp[name].astype(_jnp.float32)
        if MOMENT_SCALE is None:
            s = _jnp.sqrt(_jnp.mean(_jnp.square(w)) + 1e-30)
        else:
            s = MOMENT_SCALE[name]
        km, kv = _jax.random.split(_jax.random.fold_in(key, i + 1))
        out[name] = w
        out["m_" + name] = s * _jax.random.normal(km, w.shape, _jnp.float32)
        out["v_" + name] = (s * s) * _jax.random.uniform(kv, w.shape, _jnp.float32, 0.5, 1.5)
    if N_MICROBATCH > 1:
        for name, axis in PER_EXAMPLE_BATCH_AXIS.items():
            out[name] = _to_microbatches(out[name], axis)
    return {'x': out['x'], 'w_qkv_a': out['w_qkv_a'], 'w_o_a': out['w_o_a'], 'rel_bias': out['rel_bias'], 'w_qkv_b': out['w_qkv_b'], 'w_o_b': out['w_o_b'], 'ffn_w_gate': out['ffn_w_gate'], 'ffn_w_up': out['ffn_w_up'], 'ffn_w_down': out['ffn_w_down'], 'ln_g': out['ln_g'], 'ln_b': out['ln_b'], 'loss_target': out['loss_target'], 'm_w_qkv_a': out['m_w_qkv_a'], 'm_w_o_a': out['m_w_o_a'], 'm_rel_bias': out['m_rel_bias'], 'm_w_qkv_b': out['m_w_qkv_b'], 'm_w_o_b': out['m_w_o_b'], 'm_ffn_w_gate': out['m_ffn_w_gate'], 'm_ffn_w_up': out['m_ffn_w_up'], 'm_ffn_w_down': out['m_ffn_w_down'], 'm_ln_g': out['m_ln_g'], 'm_ln_b': out['m_ln_b'], 'v_w_qkv_a': out['v_w_qkv_a'], 'v_w_o_a': out['v_w_o_a'], 'v_rel_bias': out['v_rel_bias'], 'v_w_qkv_b': out['v_w_qkv_b'], 'v_w_o_b': out['v_w_o_b'], 'v_ffn_w_gate': out['v_ffn_w_gate'], 'v_ffn_w_up': out['v_ffn_w_up'], 'v_ffn_w_down': out['v_ffn_w_down'], 'v_ln_g': out['v_ln_g'], 'v_ln_b': out['v_ln_b']}


def _loss(weights, diff, rest, loss_target):
    with _jax.named_scope("forward"):
        args = {**rest, TWIN_DIFF_INPUT: diff, **{k: w.astype(_WEIGHT_DTYPES[k]) for k, w in weights.items()}}
        y = _forward(args)
    with _jax.named_scope("loss_head"):
        err = _jnp.square(y.astype(_jnp.float32) - loss_target)
        return 0.5 * _jnp.sum(_jnp.mean(err, axis=-1)) if err.ndim else 0.5 * err


def _adamw(w, g, m, v):
    m = ADAM_B1 * m + (1.0 - ADAM_B1) * g
    v = ADAM_B2 * v + (1.0 - ADAM_B2) * _jnp.square(g)
    m_hat = m / (1.0 - ADAM_B1 ** ADAM_STEP)
    v_hat = v / (1.0 - ADAM_B2 ** ADAM_STEP)
    delta = -ADAM_LR * (m_hat / (_jnp.sqrt(v_hat) + ADAM_EPS) + ADAM_WD * w)
    return delta, m, v


def reference(x, w_qkv_a, w_o_a, rel_bias, w_qkv_b, w_o_b, ffn_w_gate, ffn_w_up, ffn_w_down, ln_g, ln_b, loss_target, m_w_qkv_a, m_w_o_a, m_rel_bias, m_w_qkv_b, m_w_o_b, m_ffn_w_gate, m_ffn_w_up, m_ffn_w_down, m_ln_g, m_ln_b, v_w_qkv_a, v_w_o_a, v_rel_bias, v_w_qkv_b, v_w_o_b, v_ffn_w_gate, v_ffn_w_up, v_ffn_w_down, v_ln_g, v_ln_b):
    given = dict(x=x, w_qkv_a=w_qkv_a, w_o_a=w_o_a, rel_bias=rel_bias, w_qkv_b=w_qkv_b, w_o_b=w_o_b, ffn_w_gate=ffn_w_gate, ffn_w_up=ffn_w_up, ffn_w_down=ffn_w_down, ln_g=ln_g, ln_b=ln_b, loss_target=loss_target, m_w_qkv_a=m_w_qkv_a, m_w_o_a=m_w_o_a, m_rel_bias=m_rel_bias, m_w_qkv_b=m_w_qkv_b, m_w_o_b=m_w_o_b, m_ffn_w_gate=m_ffn_w_gate, m_ffn_w_up=m_ffn_w_up, m_ffn_w_down=m_ffn_w_down, m_ln_g=m_ln_g, m_ln_b=m_ln_b, v_w_qkv_a=v_w_qkv_a, v_w_o_a=v_w_o_a, v_rel_bias=v_rel_bias, v_w_qkv_b=v_w_qkv_b, v_w_o_b=v_w_o_b, v_ffn_w_gate=v_ffn_w_gate, v_ffn_w_up=v_ffn_w_up, v_ffn_w_down=v_ffn_w_down, v_ln_g=v_ln_g, v_ln_b=v_ln_b)
    weights = {n: given[n] for n in TWIN_WEIGHTS}
    shared = {n: given[n] for n in SHARED_INPUTS}
    per_example = {n: given[n] for n in ['x']}
    grad_fn = _jax.value_and_grad(_loss, argnums=(0, 1))

    def one_microbatch(ex, loss_target):
        ex = dict(ex)
        diff = ex.pop(TWIN_DIFF_INPUT)
        return grad_fn(weights, diff, {**shared, **ex}, loss_target)

    if N_MICROBATCH == 1:
        loss, (grad_w, grad_x) = one_microbatch(per_example, given["loss_target"])
    else:
        def body(carry, xs):
            loss_sum, grad_sum = carry
            l_k, (gw_k, gx_k) = one_microbatch(xs[0], xs[1])
            with _jax.named_scope("update"):
                return (loss_sum + l_k, _jax.tree.map(_jnp.add, grad_sum, gw_k)), gx_k

        init = (_jnp.zeros((), _jnp.float32), _jax.tree.map(_jnp.zeros_like, weights))
        (loss, grad_w), grad_x = _jax.lax.scan(body, init, (per_example, given["loss_target"]))
    with _jax.named_scope("update"):
        delta_w, new_m, new_v = {}, {}, {}
        for n in TWIN_WEIGHTS:
            delta_w[n], new_m[n], new_v[n] = _adamw(weights[n], grad_w[n], given["m_" + n], given["v_" + n])
    return (loss, grad_x, *[grad_w[n] for n in TWIN_WEIGHTS], *[delta_w[n] for n in TWIN_WEIGHTS],
            *[new_m[n] for n in TWIN_WEIGHTS], *[new_v[n] for n in TWIN_WEIGHTS])
```

```python
import functools
import math

import jax
import jax.numpy as jnp
from jax import lax
from jax.experimental import pallas as pl
from jax.experimental.pallas import tpu as pltpu

F32 = jnp.float32
BF16 = jnp.bfloat16
MESH = pl.DeviceIdType.MESH

N_CHIPS = 4
HEAD_DIM = 64
CHUNK = 64
LEFT_CHUNKS = 8
LOOKBACK = LEFT_CHUNKS * CHUNK
REL_CLIP = 128
N_REL = 2 * REL_CLIP + 1
REL_PAD = 384
SB_BLOCK = 128
QB_A = 256
KW_A = QB_A + LOOKBACK
VR_W = 1024
VR_C0 = KW_A - 1
LN_EPS = 1e-5
ADAM_LR, ADAM_B1, ADAM_B2, ADAM_EPS, ADAM_WD, ADAM_STEP = 0.001, 0.9, 0.999, 1e-08, 0.01, 10
NEG = -1e30
VMEM_LIMIT = 56 * 1024 * 1024

NT_DIMS = (((1,), (1,)), ((), ()))
TN_DIMS = (((0,), (0,)), ((), ()))
ANY = pl.BlockSpec(memory_space=pl.ANY)


def _params(sem=None):
    return pltpu.CompilerParams(dimension_semantics=sem, vmem_limit_bytes=VMEM_LIMIT)


def _tile(n, pref):
    t = min(n, pref)
    assert n % t == 0, (n, pref)
    return t


def _sigmoid(z):
    return 1.0 / (1.0 + jnp.exp(-z))


def _mm_call(name, operands, in_specs, out_shape, out_spec, grid, dims_list, acc_shape,
             add_coef=None, aliases=None):
    n_pairs = len(dims_list)
    nk = grid[-1]
    has_add = add_coef is not None
    n_alias = len(aliases) if aliases else 0

    def body(*refs):
        pair_refs = refs[:2 * n_pairs]
        pos = 2 * n_pairs
        add_ref = refs[pos] if has_add else None
        pos += (1 if has_add else 0) + n_alias
        o_ref = refs[pos]
        acc_ref = refs[pos + 1] if nk > 1 else None

        part = None
        for i, dims in enumerate(dims_list):
            d = lax.dot_general(pair_refs[2 * i][...], pair_refs[2 * i + 1][...], dims,
                                preferred_element_type=F32)
            part = d if part is None else part + d

        def finish(r):
            if has_add:
                r = r + add_coef * add_ref[...]
            o_ref[...] = r.astype(o_ref.dtype)

        if nk == 1:
            finish(part)
        else:
            k = pl.program_id(len(grid) - 1)

            @pl.when(k == 0)
            def _():
                acc_ref[...] = part

            @pl.when(k > 0)
            def _():
                acc_ref[...] += part

            @pl.when(k == nk - 1)
            def _():
                finish(acc_ref[...])

    sem = ("parallel",) * (len(grid) - 1) + ("arbitrary",)
    return pl.pallas_call(
        body, name=name, grid=grid, in_specs=in_specs, out_specs=out_spec, out_shape=out_shape,
        scratch_shapes=[pltpu.VMEM(acc_shape, F32)] if nk > 1 else [],
        input_output_aliases=aliases or {},
        compiler_params=_params(sem),
    )(*operands)


def qkv_proj(xb, w, l):
    M, D = xb.shape
    C = w.shape[-1]
    tm = _tile(M, 512)
    return _mm_call(
        "qkv_proj", (xb, w),
        [pl.BlockSpec((tm, D), lambda p, i, k: (i, 0)),
         pl.BlockSpec((None, None, D, C), lambda p, i, k: (p, l, 0, 0))],
        jax.ShapeDtypeStruct((M, N_CHIPS * C), BF16),
        pl.BlockSpec((tm, C), lambda p, i, k: (i, p)),
        (N_CHIPS, M // tm, 1), [(((1,), (0,)), ((), ()))], None)


def o_proj_bwd(dyb, w, l):
    M, D = dyb.shape
    R = w.shape[2]
    tm = _tile(M, 512)
    return _mm_call(
        "o_proj_bwd", (dyb, w),
        [pl.BlockSpec((tm, D), lambda p, i, k: (i, 0)),
         pl.BlockSpec((None, None, R, D), lambda p, i, k: (p, l, 0, 0))],
        jax.ShapeDtypeStruct((M, N_CHIPS * R), BF16),
        pl.BlockSpec((tm, R), lambda p, i, k: (i, p)),
        (N_CHIPS, M // tm, 1), [NT_DIMS], None)


def qkv_proj_bwd(dqkv, w, l, dpre, alpha):
    M = dqkv.shape[0]
    D, C = w.shape[2], w.shape[3]
    tm = _tile(M, 512)
    return _mm_call(
        "qkv_proj_bwd", (dqkv, w, dpre),
        [pl.BlockSpec((tm, C), lambda i, p: (i, p)),
         pl.BlockSpec((None, None, D, C), lambda i, p: (p, l, 0, 0)),
         pl.BlockSpec((tm, D), lambda i, p: (i, 0))],
        jax.ShapeDtypeStruct((M, D), F32),
        pl.BlockSpec((tm, D), lambda i, p: (i, 0)),
        (M // tm, N_CHIPS), [NT_DIMS], (tm, D), add_coef=alpha)


def ffn_dx(dg, du, wg, wu, l, dpre, alpha):
    _, M, Fs = dg.shape
    D = wg.shape[2]
    tm = _tile(M, 512)
    act = pl.BlockSpec((None, tm, Fs), lambda i, p: (p, i, 0))
    wsp = pl.BlockSpec((None, None, D, Fs), lambda i, p: (p, l, 0, 0))
    return _mm_call(
        "ffn_dx", (dg, wg, du, wu, dpre),
        [act, wsp, act, wsp, pl.BlockSpec((tm, D), lambda i, p: (i, 0))],
        jax.ShapeDtypeStruct((M, D), F32),
        pl.BlockSpec((tm, D), lambda i, p: (i, 0)),
        (M // tm, N_CHIPS), [NT_DIMS, NT_DIMS], (tm, D), add_coef=alpha)


def _dw_call(name, buf, l, a, b, a_spec, b_spec, M, tk):
    _, _, R, C = buf.shape
    return _mm_call(
        name, (a, b, buf),
        [a_spec, b_spec, ANY],
        jax.ShapeDtypeStruct(buf.shape, buf.dtype),
        pl.BlockSpec((None, None, R, C), lambda p, k: (l, p, 0, 0)),
        (N_CHIPS, M // tk), [TN_DIMS], (R, C), aliases={2: 0})


def dw_ffn_down(buf, l, h, dyb):
    _, M, Fs = h.shape
    D = dyb.shape[1]
    tk = _tile(M, 512)
    return _dw_call("dw_ffn_down", buf, l, h, dyb,
                    pl.BlockSpec((None, tk, Fs), lambda p, k: (p, k, 0)),
                    pl.BlockSpec((tk, D), lambda p, k: (k, 0)), M, tk)


def dw_ffn_up(buf, l, xb, dg):
    _, M, Fs = dg.shape
    D = xb.shape[1]
    tk = _tile(M, 512)
    return _dw_call("dw_ffn_up", buf, l, xb, dg,
                    pl.BlockSpec((tk, D), lambda p, k: (k, 0)),
                    pl.BlockSpec((None, tk, Fs), lambda p, k: (p, k, 0)), M, tk)


def dw_qkv(buf, l, xb, dqkv):
    M, D = xb.shape
    C = buf.shape[-1]
    tk = _tile(M, 512)
    return _dw_call("dw_qkv", buf, l, xb, dqkv,
                    pl.BlockSpec((tk, D), lambda p, k: (k, 0)),
                    pl.BlockSpec((tk, C), lambda p, k: (k, p)), M, tk)


def dw_o(buf, l, o, dyb):
    M, D = dyb.shape
    R = buf.shape[2]
    tk = _tile(M, 512)
    return _dw_call("dw_o", buf, l, o, dyb,
                    pl.BlockSpec((tk, R), lambda p, k: (k, p)),
                    pl.BlockSpec((tk, D), lambda p, k: (k, 0)), M, tk)


def ffn_up(xb, wg, wu, l):
    M, D = xb.shape
    Fs = wg.shape[-1]
    tm = _tile(M, 512)

    def body(x_ref, wg_ref, wu_ref, g_ref, u_ref, h_ref):
        x = x_ref[...]
        g = jnp.dot(x, wg_ref[...], preferred_element_type=F32)
        u = jnp.dot(x, wu_ref[...], preferred_element_type=F32)
        g_ref[...] = g.astype(BF16)
        u_ref[...] = u.astype(BF16)
        h_ref[...] = (g * _sigmoid(g) * u).astype(BF16)

    wsp = pl.BlockSpec((None, None, D, Fs), lambda p, i: (p, l, 0, 0))
    osp = pl.BlockSpec((None, tm, Fs), lambda p, i: (p, i, 0))
    osh = jax.ShapeDtypeStruct((N_CHIPS, M, Fs), BF16)
    return pl.pallas_call(
        body, name="ffn_up", grid=(N_CHIPS, M // tm),
        in_specs=[pl.BlockSpec((tm, D), lambda p, i: (i, 0)), wsp, wsp],
        out_specs=[osp, osp, osp], out_shape=[osh, osh, osh],
        compiler_params=_params(("parallel", "parallel")),
    )(xb, wg, wu)


def ffn_bwd_act(dyb, wd, l, g, u):
    M, D = dyb.shape
    Fs = wd.shape[2]
    tm = _tile(M, 512)

    def body(dy_ref, wd_ref, g_ref, u_ref, dg_ref, du_ref):
        dh = lax.dot_general(dy_ref[...], wd_ref[...], NT_DIMS, preferred_element_type=F32)
        gf = g_ref[...].astype(F32)
        uf = u_ref[...].astype(F32)
        sig = _sigmoid(gf)
        dg_ref[...] = (dh * uf * (sig * (1.0 + gf * (1.0 - sig)))).astype(BF16)
        du_ref[...] = (dh * (gf * sig)).astype(BF16)

    asp = pl.BlockSpec((None, tm, Fs), lambda p, i: (p, i, 0))
    osh = jax.ShapeDtypeStruct((N_CHIPS, M, Fs), BF16)
    return pl.pallas_call(
        body, name="ffn_bwd_act", grid=(N_CHIPS, M // tm),
        in_specs=[pl.BlockSpec((tm, D), lambda p, i: (i, 0)),
                  pl.BlockSpec((None, None, Fs, D), lambda p, i: (p, l, 0, 0)), asp, asp],
        out_specs=[asp, asp], out_shape=[osh, osh],
        compiler_params=_params(("parallel", "parallel")),
    )(dyb, wd, g, u)


def mm_ln(a, w, l, x, gam, bet, alpha, scale, a_piece_major):
    M, D = x.shape
    R = w.shape[2]
    tm = _tile(M, 512)
    if a_piece_major:
        a_spec = pl.BlockSpec((None, tm, R), lambda i, p: (p, i, 0))
    else:
        a_spec = pl.BlockSpec((tm, R), lambda i, p: (i, p))

    def body(a_ref, w_ref, x_ref, g_ref, b_ref, xo_ref, xb_ref, xh_ref, rs_ref, acc_ref):
        p = pl.program_id(1)
        part = jnp.dot(a_ref[...], w_ref[...], preferred_element_type=F32)

        @pl.when(p == 0)
        def _():
            acc_ref[...] = part

        @pl.when(p > 0)
        def _():
            acc_ref[...] += part

        @pl.when(p == N_CHIPS - 1)
        def _():
            pre = alpha * x_ref[...] + scale * acc_ref[...]
            mu = jnp.mean(pre, axis=-1, keepdims=True)
            cen = pre - mu
            var = jnp.mean(cen * cen, axis=-1, keepdims=True)
            rstd = lax.rsqrt(var + LN_EPS)
            xhat = cen * rstd
            out = xhat * g_ref[...] + b_ref[...]
            xo_ref[...] = out
            xb_ref[...] = out.astype(BF16)
            xh_ref[...] = xhat
            rs_ref[...] = rstd

    row = pl.BlockSpec((tm, D), lambda i, p: (i, 0))
    vec = pl.BlockSpec((1, D), lambda i, p: (0, 0))
    return pl.pallas_call(
        body, name="mm_ln", grid=(M // tm, N_CHIPS),
        in_specs=[a_spec, pl.BlockSpec((None, None, R, D), lambda i, p: (p, l, 0, 0)), row, vec, vec],
        out_specs=[row, row, row, pl.BlockSpec((tm, 1), lambda i, p: (i, 0))],
        out_shape=[jax.ShapeDtypeStruct((M, D), F32), jax.ShapeDtypeStruct((M, D), BF16),
                   jax.ShapeDtypeStruct((M, D), F32), jax.ShapeDtypeStruct((M, 1), F32)],
        scratch_shapes=[pltpu.VMEM((tm, D), F32)],
        compiler_params=_params(("parallel", "arbitrary")),
    )(a, w, x, gam, bet)


def ln_bwd(dy, xhat, rstd, gam, scale):
    M, D = dy.shape
    tm = _tile(M, 512)

    def body(dy_ref, xh_ref, rs_ref, g_ref, dp_ref, db16_ref, dg_ref, dbt_ref):
        i = pl.program_id(0)
        dy_v = dy_ref[...]
        xh = xh_ref[...]
        dxh = dy_v * g_ref[...]
        m1 = jnp.mean(dxh, axis=-1, keepdims=True)
        m2 = jnp.mean(dxh * xh, axis=-1, keepdims=True)
        dpre = rs_ref[...] * (dxh - m1 - xh * m2)
        dp_ref[...] = dpre
        db16_ref[...] = (scale * dpre).astype(BF16)
        dgp = jnp.sum(dy_v * xh, axis=0, keepdims=True)
        dbp = jnp.sum(dy_v, axis=0, keepdims=True)

        @pl.when(i == 0)
        def _():
            dg_ref[...] = dgp
            dbt_ref[...] = dbp

        @pl.when(i > 0)
        def _():
            dg_ref[...] += dgp
            dbt_ref[...] += dbp

    row = pl.BlockSpec((tm, D), lambda i: (i, 0))
    vec = pl.BlockSpec((1, D), lambda i: (0, 0))
    return pl.pallas_call(
        body, name="ln_bwd", grid=(M // tm,),
        in_specs=[row, row, pl.BlockSpec((tm, 1), lambda i: (i, 0)), vec],
        out_specs=[row, row, vec, vec],
        out_shape=[jax.ShapeDtypeStruct((M, D), F32), jax.ShapeDtypeStruct((M, D), BF16),
                   jax.ShapeDtypeStruct((1, D), F32), jax.ShapeDtypeStruct((1, D), F32)],
        compiler_params=_params(("arbitrary",)),
    )(dy, xhat, rstd, gam)


def loss_head(y, tgt):
    M, D = y.shape
    tm = _tile(M, 512)
    n = M // tm

    def body(y_ref, t_ref, dy_ref, l_ref, acc_ref):
        i = pl.program_id(0)
        e = y_ref[...] - t_ref[...]
        dy_ref[...] = e * (1.0 / D)
        part = jnp.sum(e * e, axis=0, keepdims=True)

        @pl.when(i == 0)
        def _():
            acc_ref[...] = part

        @pl.when(i > 0)
        def _():
            acc_ref[...] += part

        @pl.when(i == n - 1)
        def _():
            l_ref[...] = (0.5 / D) * jnp.sum(acc_ref[...], axis=1, keepdims=True)

    row = pl.BlockSpec((tm, D), lambda i: (i, 0))
    return pl.pallas_call(
        body, name="loss_head", grid=(n,),
        in_specs=[row, row],
        out_specs=[row, pl.BlockSpec((1, 1), lambda i: (0, 0))],
        out_shape=[jax.ShapeDtypeStruct((M, D), F32), jax.ShapeDtypeStruct((1, 1), F32)],
        scratch_shapes=[pltpu.VMEM((1, D), F32)],
        compiler_params=_params(("arbitrary",)),
    )(y, tgt)


def _rel_onehot_t():
    r = lax.broadcasted_iota(jnp.int32, (REL_PAD, VR_W), 0)
    n = lax.broadcasted_iota(jnp.int32, (REL_PAD, VR_W), 1)
    idx = jnp.clip(VR_C0 - n, -REL_CLIP, REL_CLIP) + REL_CLIP
    return (r == idx).astype(F32)


def bias_vec(tab_t):
    H = tab_t.shape[0]

    def body(t_ref, o_ref):
        o_ref[...] = jnp.dot(t_ref[...], _rel_onehot_t(), precision=lax.Precision.HIGHEST,
                             preferred_element_type=F32)

    return pl.pallas_call(
        body, name="bias_vec", out_shape=jax.ShapeDtypeStruct((H, VR_W), F32),
        compiler_params=_params(),
    )(tab_t)


def bias_vec_bwd(dvr):
    n, H, _ = dvr.shape

    def body(d_ref, o_ref):
        tot = d_ref[0]
        for i in range(1, n):
            tot = tot + d_ref[i]
        o_ref[...] = lax.dot_general(tot, _rel_onehot_t(), NT_DIMS, precision=lax.Precision.HIGHEST,
                                     preferred_element_type=F32)

    return pl.pallas_call(
        body, name="bias_vec_bwd", out_shape=jax.ShapeDtypeStruct((H, REL_PAD), F32),
        compiler_params=_params(),
    )(dvr)


def _a_bias_mask(vr_row):
    xb = jnp.broadcast_to(vr_row, (QB_A, VR_W))
    tile = pltpu.roll(xb, VR_W - (QB_A - 1), 1, stride=1, stride_axis=0)[:, :KW_A]
    qc = lax.broadcasted_iota(jnp.int32, (QB_A, KW_A), 0) // CHUNK
    kc = lax.broadcasted_iota(jnp.int32, (QB_A, KW_A), 1) // CHUNK
    valid = (kc >= qc) & (kc <= qc + LEFT_CHUNKS)
    return jnp.where(valid, tile, NEG)


def _a_diag_sums(db_acc, h):
    acc8 = None
    for a in range(QB_A // 8):
        grp = db_acc[h, 8 * a:8 * a + 8, :]
        shift = QB_A - 8 - 8 * a
        if shift:
            grp = pltpu.roll(grp, shift, 1)
        acc8 = grp if acc8 is None else acc8 + grp
    sub = lax.broadcasted_iota(jnp.int32, (8, VR_W), 0)
    tot = jnp.zeros((8, VR_W), F32)
    for b in range(8):
        moved = pltpu.roll(acc8, 7 - b, 1) if b < 7 else acc8
        tot = tot + jnp.where(sub == b, moved, 0.0)
    return jnp.sum(tot, axis=0, keepdims=True)


def _a_blocks(S):
    out = []
    for qi in range(S // QB_A):
        q0 = qi * QB_A
        ks = max(0, q0 - LOOKBACK)
        out.append((q0, ks, q0 + QB_A, ks - (q0 - LOOKBACK)))
    return out


def _head_specs(S, HP):
    q = pl.BlockSpec((S, 2 * HEAD_DIM), lambda b, hp: (b, hp))
    k = pl.BlockSpec((S, 2 * HEAD_DIM), lambda b, hp: (b, HP + hp))
    v = pl.BlockSpec((S, 2 * HEAD_DIM), lambda b, hp: (b, 2 * HP + hp))
    return q, k, v


def attn_a_fwd(qkv, vr, B, S):
    D = qkv.shape[1] // 3
    HP = D // (2 * HEAD_DIM)
    scale = HEAD_DIM ** -0.5
    blocks = _a_blocks(S)

    def body(q_ref, k_ref, v_ref, vr_ref, o_ref):
        for h in range(2):
            lo = h * HEAD_DIM
            bm = _a_bias_mask(vr_ref[h:h + 1, :])
            for (q0, ks, ke, joff) in blocks:
                q = q_ref[q0:q0 + QB_A, lo:lo + HEAD_DIM]
                k = k_ref[ks:ke, lo:lo + HEAD_DIM]
                v = v_ref[ks:ke, lo:lo + HEAD_DIM]
                s = lax.dot_general(q, k, NT_DIMS, preferred_element_type=F32) * scale + bm[:, joff:]
                m = jnp.max(s, axis=-1, keepdims=True)
                p = jnp.exp(s - m)
                den = jnp.sum(p, axis=-1, keepdims=True)
                o = jnp.dot(p.astype(BF16), v, preferred_element_type=F32) / den
                o_ref[q0:q0 + QB_A, lo:lo + HEAD_DIM] = o.astype(BF16)

    qs, ks_, vs = _head_specs(S, HP)
    return pl.pallas_call(
        body, name="attn_a_fwd", grid=(B, HP),
        in_specs=[qs, ks_, vs, pl.BlockSpec((None, 2, VR_W), lambda b, hp: (hp, 0, 0))],
        out_specs=pl.BlockSpec((S, 2 * HEAD_DIM), lambda b, hp: (b, hp)),
        out_shape=jax.ShapeDtypeStruct((B * S, D), BF16),
        compiler_params=_params(("parallel", "parallel")),
    )(qkv, qkv, qkv, vr)


def attn_a_bwd(qkv, vr, do, B, S):
    D = qkv.shape[1] // 3
    HP = D // (2 * HEAD_DIM)
    scale = HEAD_DIM ** -0.5
    blocks = _a_blocks(S)

    def body(q_ref, k_ref, v_ref, vr_ref, do_ref, dq_ref, dk_ref, dv_ref, dvr_ref,
             dk_acc, dv_acc, db_acc):
        dk_acc[...] = jnp.zeros_like(dk_acc)
        dv_acc[...] = jnp.zeros_like(dv_acc)
        db_acc[...] = jnp.zeros_like(db_acc)
        for h in range(2):
            lo = h * HEAD_DIM
            bm = _a_bias_mask(vr_ref[h:h + 1, :])
            for (q0, ks, ke, joff) in blocks:
                q = q_ref[q0:q0 + QB_A, lo:lo + HEAD_DIM]
                k = k_ref[ks:ke, lo:lo + HEAD_DIM]
                v = v_ref[ks:ke, lo:lo + HEAD_DIM]
                dob = do_ref[q0:q0 + QB_A, lo:lo + HEAD_DIM]
                s = lax.dot_general(q, k, NT_DIMS, preferred_element_type=F32) * scale + bm[:, joff:]
                m = jnp.max(s, axis=-1, keepdims=True)
                e = jnp.exp(s - m)
                p = e / jnp.sum(e, axis=-1, keepdims=True)
                dp = lax.dot_general(dob, v, NT_DIMS, preferred_element_type=F32)
                ds = p * (dp - jnp.sum(p * dp, axis=-1, keepdims=True))
                dsb = ds.astype(BF16)
                dq = jnp.dot(dsb, k, preferred_element_type=F32) * scale
                dq_ref[q0:q0 + QB_A, lo:lo + HEAD_DIM] = dq.astype(BF16)
                dk_acc[ks:ke, lo:lo + HEAD_DIM] += lax.dot_general(
                    dsb, q, TN_DIMS, preferred_element_type=F32) * scale
                dv_acc[ks:ke, lo:lo + HEAD_DIM] += lax.dot_general(
                    p.astype(BF16), dob, TN_DIMS, preferred_element_type=F32)
                db_acc[h, :, joff:KW_A] += ds
            dvr_ref[h:h + 1, :] = _a_diag_sums(db_acc, h)
        dk_ref[...] = dk_acc[...].astype(BF16)
        dv_ref[...] = dv_acc[...].astype(BF16)

    qs, ks_, vs = _head_specs(S, HP)
    hd = pl.BlockSpec((S, 2 * HEAD_DIM), lambda b, hp: (b, hp))
    osh = jax.ShapeDtypeStruct((B * S, D), BF16)
    return pl.pallas_call(
        body, name="attn_a_bwd", grid=(B, HP),
        in_specs=[qs, ks_, vs, pl.BlockSpec((None, 2, VR_W), lambda b, hp: (hp, 0, 0)), hd],
        out_specs=[hd, hd, hd, pl.BlockSpec((None, None, 2, VR_W), lambda b, hp: (b, hp, 0, 0))],
        out_shape=[osh, osh, osh, jax.ShapeDtypeStruct((B, HP, 2, VR_W), F32)],
        scratch_shapes=[pltpu.VMEM((S, 2 * HEAD_DIM), F32), pltpu.VMEM((S, 2 * HEAD_DIM), F32),
                        pltpu.VMEM((2, QB_A, VR_W), F32)],
        compiler_params=_params(("parallel", "parallel")),
    )(qkv, qkv, qkv, vr, do)


def _tri(cmp):
    j = lax.broadcasted_iota(jnp.int32, (SB_BLOCK, SB_BLOCK), 0)
    s = lax.broadcasted_iota(jnp.int32, (SB_BLOCK, SB_BLOCK), 1)
    return cmp(j, s).astype(BF16)


def _cumsum_mm(x, tri):
    hi = x.astype(BF16)
    r1 = x - hi.astype(F32)
    mid = r1.astype(BF16)
    low = (r1 - mid.astype(F32)).astype(BF16)
    return (jnp.dot(hi, tri, preferred_element_type=F32) + jnp.dot(mid, tri, preferred_element_type=F32)
            + jnp.dot(low, tri, preferred_element_type=F32))


def _sb_logs(q, k, scale, q0, k0):
    z = lax.dot_general(q, k, NT_DIMS, preferred_element_type=F32) * scale
    row = lax.broadcasted_iota(jnp.int32, (SB_BLOCK, SB_BLOCK), 0)
    col = lax.broadcasted_iota(jnp.int32, (SB_BLOCK, SB_BLOCK), 1)
    causal = (k0 + col) < (q0 + row)
    lg = jnp.log(1.0 + jnp.exp(-jnp.abs(z)))
    log_b = jnp.minimum(z, 0.0) - lg
    log_1mb = jnp.where(causal, jnp.minimum(-z, 0.0) - lg, 0.0)
    return log_b, log_1mb, causal


def attn_b_fwd(qkv, B, S):
    D = qkv.shape[1] // 3
    HP = D // (2 * HEAD_DIM)
    scale = HEAD_DIM ** -0.5
    nb = S // SB_BLOCK

    def body(q_ref, k_ref, v_ref, o_ref, nt_ref):
        tri = _tri(lambda j, s: j > s)
        for h in range(2):
            lo = h * HEAD_DIM

            def q_loop(qb, carry):
                q0 = pl.multiple_of(qb * SB_BLOCK, SB_BLOCK)
                q = q_ref[pl.ds(q0, SB_BLOCK), lo:lo + HEAD_DIM]

                def k_loop(t, kc):
                    right, acc = kc
                    k0 = pl.multiple_of((qb - t) * SB_BLOCK, SB_BLOCK)
                    k = k_ref[pl.ds(k0, SB_BLOCK), lo:lo + HEAD_DIM]
                    v = v_ref[pl.ds(k0, SB_BLOCK), lo:lo + HEAD_DIM]
                    log_b, log_1mb, causal = _sb_logs(q, k, scale, q0, k0)
                    stick = _cumsum_mm(log_1mb, tri) + right
                    a = jnp.where(causal, jnp.exp(log_b + stick), 0.0)
                    acc = acc + jnp.dot(a.astype(BF16), v, preferred_element_type=F32)
                    right = right + jnp.sum(log_1mb, axis=-1, keepdims=True)
                    return right, acc

                right, acc = lax.fori_loop(
                    0, qb + 1, k_loop,
                    (jnp.zeros((SB_BLOCK, 1), F32), jnp.zeros((SB_BLOCK, HEAD_DIM), F32)))
                o_ref[pl.ds(q0, SB_BLOCK), lo:lo + HEAD_DIM] = acc.astype(BF16)
                nt_ref[pl.ds(q0, SB_BLOCK), lo:lo + HEAD_DIM] = jnp.broadcast_to(
                    right, (SB_BLOCK, HEAD_DIM))
                return carry

            lax.fori_loop(0, nb, q_loop, 0)

    qs, ks_, vs = _head_specs(S, HP)
    hd = pl.BlockSpec((S, 2 * HEAD_DIM), lambda b, hp: (b, hp))
    return pl.pallas_call(
        body, name="attn_b_fwd", grid=(B, HP),
        in_specs=[qs, ks_, vs], out_specs=[hd, hd],
        out_shape=[jax.ShapeDtypeStruct((B * S, D), BF16), jax.ShapeDtypeStruct((B * S, D), F32)],
        compiler_params=_params(("parallel", "parallel")),
    )(qkv, qkv, qkv)


def attn_b_bwd(qkv, do, ntot, B, S):
    D = qkv.shape[1] // 3
    HP = D // (2 * HEAD_DIM)
    scale = HEAD_DIM ** -0.5
    nb = S // SB_BLOCK

    def body(q_ref, k_ref, v_ref, do_ref, nt_ref, dq_ref, dk_ref, dv_ref, dk_acc, dv_acc):
        tri_incl = _tri(lambda j, s: j <= s)
        tri_excl = _tri(lambda j, s: j < s)
        dk_acc[...] = jnp.zeros_like(dk_acc)
        dv_acc[...] = jnp.zeros_like(dv_acc)
        for h in range(2):
            lo = h * HEAD_DIM

            def q_loop(qb, carry):
                q0 = pl.multiple_of(qb * SB_BLOCK, SB_BLOCK)
                q = q_ref[pl.ds(q0, SB_BLOCK), lo:lo + HEAD_DIM]
                dob = do_ref[pl.ds(q0, SB_BLOCK), lo:lo + HEAD_DIM]
                nt = nt_ref[pl.ds(q0, SB_BLOCK), lo:lo + 1]

                def k_loop(kb, kc):
                    left_n, left_d, dq_acc = kc
                    k0 = pl.multiple_of(kb * SB_BLOCK, SB_BLOCK)
                    k = k_ref[pl.ds(k0, SB_BLOCK), lo:lo + HEAD_DIM]
                    v = v_ref[pl.ds(k0, SB_BLOCK), lo:lo + HEAD_DIM]
                    log_b, log_1mb, causal = _sb_logs(q, k, scale, q0, k0)
                    stick = nt - left_n - _cumsum_mm(log_1mb, tri_incl)
                    a = jnp.where(causal, jnp.exp(log_b + stick), 0.0)
                    da = lax.dot_general(dob, v, NT_DIMS, preferred_element_type=F32)
                    dl = a * da
                    dn = left_d + _cumsum_mm(dl, tri_excl)
                    dz = jnp.where(causal, dl * jnp.exp(log_1mb) - dn * jnp.exp(log_b), 0.0)
                    dzb = dz.astype(BF16)
                    dq_acc = dq_acc + jnp.dot(dzb, k, preferred_element_type=F32)
                    dk_acc[pl.ds(k0, SB_BLOCK), lo:lo + HEAD_DIM] += lax.dot_general(
                        dzb, q, TN_DIMS, preferred_element_type=F32)
                    dv_acc[pl.ds(k0, SB_BLOCK), lo:lo + HEAD_DIM] += lax.dot_general(
                        a.astype(BF16), dob, TN_DIMS, preferred_element_type=F32)
                    left_n = left_n + jnp.sum(log_1mb, axis=-1, keepdims=True)
                    left_d = left_d + jnp.sum(dl, axis=-1, keepdims=True)
                    return left_n, left_d, dq_acc

                zero1 = jnp.zeros((SB_BLOCK, 1), F32)
                _, _, dq_acc = lax.fori_loop(
                    0, qb + 1, k_loop, (zero1, zero1, jnp.zeros((SB_BLOCK, HEAD_DIM), F32)))
                dq_ref[pl.ds(q0, SB_BLOCK), lo:lo + HEAD_DIM] = (dq_acc * scale).astype(BF16)
                return carry

            lax.fori_loop(0, nb, q_loop, 0)
        dk_ref[...] = (dk_acc[...] * scale).astype(BF16)
        dv_ref[...] = dv_acc[...].astype(BF16)

    qs, ks_, vs = _head_specs(S, HP)
    hd = pl.BlockSpec((S, 2 * HEAD_DIM), lambda b, hp: (b, hp))
    osh = jax.ShapeDtypeStruct((B * S, D), BF16)
    return pl.pallas_call(
        body, name="attn_b_bwd", grid=(B, HP),
        in_specs=[qs, ks_, vs, hd, hd], out_specs=[hd, hd, hd], out_shape=[osh, osh, osh],
        scratch_shapes=[pltpu.VMEM((S, 2 * HEAD_DIM), F32), pltpu.VMEM((S, 2 * HEAD_DIM), F32)],
        compiler_params=_params(("parallel", "parallel")),
    )(qkv, qkv, qkv, do, ntot)


def _place():
    x, y, c = lax.axis_index("x"), lax.axis_index("y"), lax.axis_index("c")
    chips = [(1 - x, y), (x, 1 - y), (1 - x, 1 - y)]
    return x, y, c, 2 * x + y, chips


def _remote(src, dst, send_sem, recv_sem, dev):
    return pltpu.make_async_remote_copy(src_ref=src, dst_ref=dst, send_sem=send_sem, recv_sem=recv_sem,
                                        device_id=dev, device_id_type=MESH)


def gather_weights(shards):
    n = len(shards)

    def body(*refs):
        ins, outs = refs[:n], refs[n:2 * n]
        send1, recv1, send2, recv2, lsem = refs[2 * n:]
        x, y, c, me, chips = _place()
        local, first = [], []
        for f in range(n):
            hl = shards[f].shape[0] // 2
            cp = pltpu.make_async_copy(ins[f], outs[f].at[me], lsem.at[f])
            cp.start()
            local.append(cp)
            for j, (qx, qy) in enumerate(chips):
                half = pl.ds(c * hl, hl)
                cp = _remote(ins[f].at[half], outs[f].at[me, half],
                             send1.at[3 * f + j], recv1.at[3 * f + j], (qx, qy, c))
                cp.start()
                first.append(cp)
        passed = []
        for f in range(n):
            hl = shards[f].shape[0] // 2
            for j, (qx, qy) in enumerate(chips):
                slab = outs[f].at[2 * qx + qy, pl.ds(c * hl, hl)]
                _remote(slab, slab, send1.at[3 * f + j], recv1.at[3 * f + j], (x, y, c)).wait_recv()
                cp = _remote(slab, slab, send2.at[3 * f + j], recv2.at[3 * f + j], (x, y, 1 - c))
                cp.start()
                passed.append(cp)
        for f in range(n):
            hl = shards[f].shape[0] // 2
            for j, (qx, qy) in enumerate(chips):
                slab = outs[f].at[2 * qx + qy, pl.ds((1 - c) * hl, hl)]
                _remote(slab, slab, send2.at[3 * f + j], recv2.at[3 * f + j], (x, y, c)).wait_recv()
        for cp in first + passed:
            cp.wait_send()
        for cp in local:
            cp.wait()

    sems = pltpu.SemaphoreType.DMA((3 * n,))
    return pl.pallas_call(
        body, name="gather_weights",
        in_specs=[ANY] * n, out_specs=[ANY] * n,
        out_shape=[jax.ShapeDtypeStruct((N_CHIPS,) + s.shape, s.dtype) for s in shards],
        scratch_shapes=[sems, sems, sems, sems, pltpu.SemaphoreType.DMA((n,))],
        compiler_params=pltpu.CompilerParams(has_side_effects=True),
    )(*shards)


def pair_exchange(grads):
    n = len(grads)

    def body(*refs):
        ins, outs = refs[:n], refs[n:2 * n]
        send, recv = refs[2 * n:]
        x, y, c, _, _ = _place()
        cps = []
        for f in range(n):
            hl = grads[f].shape[0] // 2
            cp = _remote(ins[f].at[pl.ds((1 - c) * hl, hl)], outs[f], send.at[f], recv.at[f], (x, y, 1 - c))
            cp.start()
            cps.append(cp)
        for cp in cps:
            cp.wait()

    sems = pltpu.SemaphoreType.DMA((n,))
    return pl.pallas_call(
        body, name="pair_exchange",
        in_specs=[ANY] * n, out_specs=[ANY] * n,
        out_shape=[jax.ShapeDtypeStruct((g.shape[0] // 2,) + g.shape[1:], g.dtype) for g in grads],
        scratch_shapes=[sems, sems],
        compiler_params=pltpu.CompilerParams(has_side_effects=True),
    )(*grads)


def chip_exchange(parts):
    n = len(parts)

    def body(*refs):
        ins, outs = refs[:n], refs[n:2 * n]
        send, recv, lsem = refs[2 * n:]
        x, y, c, me, chips = _place()
        local, sent = [], []
        for f in range(n):
            hl = parts[f].shape[0]
            rows = pl.ds(0, hl)
            cp = pltpu.make_async_copy(ins[f].at[rows, me], outs[f].at[rows, me], lsem.at[f])
            cp.start()
            local.append(cp)
            for j, (qx, qy) in enumerate(chips):
                cp = _remote(ins[f].at[rows, 2 * qx + qy], outs[f].at[rows, me],
                             send.at[3 * f + j], recv.at[3 * f + j], (qx, qy, c))
                cp.start()
                sent.append(cp)
        for f in range(n):
            rows = pl.ds(0, parts[f].shape[0])
            for j, (qx, qy) in enumerate(chips):
                slab = outs[f].at[rows, 2 * qx + qy]
                _remote(slab, slab, send.at[3 * f + j], recv.at[3 * f + j], (x, y, c)).wait_recv()
        for cp in sent:
            cp.wait_send()
        for cp in local:
            cp.wait()

    sems = pltpu.SemaphoreType.DMA((3 * n,))
    return pl.pallas_call(
        body, name="chip_exchange",
        in_specs=[ANY] * n, out_specs=[ANY] * n,
        out_shape=[jax.ShapeDtypeStruct(s.shape, s.dtype) for s in parts],
        scratch_shapes=[sems, sems, pltpu.SemaphoreType.DMA((n,))],
        compiler_params=pltpu.CompilerParams(has_side_effects=True),
    )(*parts)


def half_swap(grads):
    n = len(grads)

    def body(*refs):
        ins, outs = refs[:n], refs[n:2 * n]
        send, recv = refs[2 * n:]
        x, y, c, _, _ = _place()
        cps = []
        for f in range(n):
            hl = grads[f].shape[0] // 2
            mine = pl.ds(c * hl, hl)
            cp = _remote(outs[f].at[mine], outs[f].at[mine], send.at[f], recv.at[f], (x, y, 1 - c))
            cp.start()
            cps.append(cp)
        for f in range(n):
            hl = grads[f].shape[0] // 2
            theirs = outs[f].at[pl.ds((1 - c) * hl, hl)]
            _remote(theirs, theirs, send.at[f], recv.at[f], (x, y, c)).wait_recv()
        for cp in cps:
            cp.wait_send()

    sems = pltpu.SemaphoreType.DMA((n,))
    return pl.pallas_call(
        body, name="half_swap",
        in_specs=[ANY] * n, out_specs=[ANY] * n,
        out_shape=[jax.ShapeDtypeStruct(g.shape, g.dtype) for g in grads],
        input_output_aliases={f: f for f in range(n)},
        scratch_shapes=[sems, sems],
        compiler_params=pltpu.CompilerParams(has_side_effects=True),
    )(*grads)


def all_sum_small(v):
    R = v.shape[0]

    def body(v_ref, o_ref, land, send, recv):
        x, y, c, _, _ = _place()
        me = 4 * x + 2 * y + c
        land[me] = v_ref[...]
        peers = [(px, py, pc) for px in range(2) for py in range(2) for pc in range(2)]
        cps = []
        for k in range(1, 8):
            dev = (x ^ (k >> 2), y ^ ((k >> 1) & 1), c ^ (k & 1))
            cp = _remote(v_ref, land.at[me], send.at[k - 1], recv.at[k - 1], dev)
            cp.start()
            cps.append(cp)
        for k in range(1, 8):
            src = 4 * (x ^ (k >> 2)) + 2 * (y ^ ((k >> 1) & 1)) + (c ^ (k & 1))
            _remote(v_ref, land.at[src], send.at[k - 1], recv.at[k - 1], (x, y, c)).wait_recv()
        for cp in cps:
            cp.wait_send()
        tot = land[0]
        for d in range(1, len(peers)):
            tot = tot + land[d]
        o_ref[...] = tot

    sems = pltpu.SemaphoreType.DMA((7,))
    vm = pl.BlockSpec(memory_space=pltpu.VMEM)
    return pl.pallas_call(
        body, name="all_sum_small", in_specs=[vm], out_specs=vm,
        out_shape=jax.ShapeDtypeStruct(v.shape, F32),
        scratch_shapes=[pltpu.VMEM((8, R, 128), F32), sems, sems],
        compiler_params=pltpu.CompilerParams(has_side_effects=True),
    )(v)


def _row_tile(R):
    for t in (512, 256, 128, 64, 32, 16):
        if R % t == 0:
            return t
    return R


def pair_add(cidx, grad, recv):
    hl, P, R, C = recv.shape
    tr = _row_tile(R)

    def body(c_ref, a_ref, b_ref, o_ref):
        o_ref[...] = (a_ref[...].astype(F32) + b_ref[...].astype(F32)).astype(o_ref.dtype)

    blk = (None, None, tr, C)
    return pl.pallas_call(
        body, name="pair_add",
        grid_spec=pltpu.PrefetchScalarGridSpec(
            num_scalar_prefetch=1, grid=(hl, P, R // tr),
            in_specs=[pl.BlockSpec(blk, lambda l, p, r, c: (c[0] * hl + l, p, r, 0)),
                      pl.BlockSpec(blk, lambda l, p, r, c: (l, p, r, 0))],
            out_specs=pl.BlockSpec(blk, lambda l, p, r, c: (l, p, r, 0))),
        out_shape=jax.ShapeDtypeStruct(recv.shape, recv.dtype),
        compiler_params=_params(("parallel", "parallel", "parallel")),
    )(cidx, grad, recv)


def chip_sum(cidx, land, L):
    hl, P, R, C = land.shape
    tr = _row_tile(R)

    def body(c_ref, a_ref, o_ref):
        tot = a_ref[0].astype(F32)
        for q in range(1, P):
            tot = tot + a_ref[q].astype(F32)
        o_ref[...] = tot

    return pl.pallas_call(
        body, name="chip_sum",
        grid_spec=pltpu.PrefetchScalarGridSpec(
            num_scalar_prefetch=1, grid=(hl, R // tr),
            in_specs=[pl.BlockSpec((None, P, tr, C), lambda l, r, c: (l, 0, r, 0))],
            out_specs=pl.BlockSpec((None, tr, C), lambda l, r, c: (c[0] * hl + l, r, 0))),
        out_shape=jax.ShapeDtypeStruct((L, R, C), F32),
        compiler_params=_params(("parallel", "parallel")),
    )(cidx, land)


def adamw(w, g, m, v):
    L, R, C = w.shape
    tr = _row_tile(R)
    c1 = 1.0 / (1.0 - ADAM_B1 ** ADAM_STEP)
    c2 = 1.0 / (1.0 - ADAM_B2 ** ADAM_STEP)

    def body(w_ref, g_ref, m_ref, v_ref, d_ref, nm_ref, nv_ref):
        gv = g_ref[...]
        nm = ADAM_B1 * m_ref[...] + (1.0 - ADAM_B1) * gv
        nv = ADAM_B2 * v_ref[...] + (1.0 - ADAM_B2) * (gv * gv)
        nm_ref[...] = nm
        nv_ref[...] = nv
        d_ref[...] = -ADAM_LR * ((nm * c1) / (jnp.sqrt(nv * c2) + ADAM_EPS) + ADAM_WD * w_ref[...])

    blk = pl.BlockSpec((None, tr, C), lambda l, r: (l, r, 0))
    osh = jax.ShapeDtypeStruct((L, R, C), F32)
    return pl.pallas_call(
        body, name="adamw", grid=(L, R // tr),
        in_specs=[blk, blk, blk, blk], out_specs=[blk, blk, blk], out_shape=[osh, osh, osh],
        compiler_params=_params(("parallel", "parallel")),
    )(w, g, m, v)


def kernel(x, w_qkv_a, w_o_a, rel_bias, w_qkv_b, w_o_b, ffn_w_gate, ffn_w_up, ffn_w_down, ln_g, ln_b, loss_target, m_w_qkv_a, m_w_o_a, m_rel_bias, m_w_qkv_b, m_w_o_b, m_ffn_w_gate, m_ffn_w_up, m_ffn_w_down, m_ln_g, m_ln_b, v_w_qkv_a, v_w_o_a, v_rel_bias, v_w_qkv_b, v_w_o_b, v_ffn_w_gate, v_ffn_w_up, v_ffn_w_down, v_ln_g, v_ln_b):
    B, S, D = x.shape
    M = B * S
    depth = ffn_w_gate.shape[0]
    n_ffn = 2 * depth
    H = D // HEAD_DIM
    HP = H // 2
    Fs = ffn_w_gate.shape[-1]
    alpha = (2.0 * depth) ** 0.25
    assert S % QB_A == 0 and S % SB_BLOCK == 0 and rel_bias.shape == (N_REL, H)

    def ffn3(a):
        return a.reshape((n_ffn,) + a.shape[2:])

    shards = [w_qkv_a.astype(BF16), w_o_a.astype(BF16), w_qkv_b.astype(BF16), w_o_b.astype(BF16),
              ffn3(ffn_w_gate).astype(BF16), ffn3(ffn_w_up).astype(BF16), ffn3(ffn_w_down).astype(BF16),
              ln_g, ln_b]
    wqa, woa, wqb, wob, wg, wu, wd, lng_p, lnb_p = gather_weights(shards)
    lng = jnp.moveaxis(lng_p, 0, 2).reshape(depth, 3, 1, D)
    lnb = jnp.moveaxis(lnb_p, 0, 2).reshape(depth, 3, 1, D)

    tab_t = jnp.pad(rel_bias.T, ((0, 0), (0, REL_PAD - N_REL)))
    vr = bias_vec(tab_t).reshape(HP, 2, VR_W)

    xf = x.reshape(M, D)
    xb = xf.astype(BF16)
    saved = []
    for i in range(depth):
        for j in range(3):
            gam, bet = lng[i, j], lnb[i, j]
            if j != 1:
                l = 2 * i + (0 if j == 0 else 1)
                g, u, h = ffn_up(xb, wg, wu, l)
                xo, xob, xhat, rstd = mm_ln(h, wd, l, xf, gam, bet, alpha, 0.5, True)
                saved.append(("ffn", l, xb, g, u, h, xhat, rstd, gam))
            elif i % 2 == 0:
                l = i // 2
                qkv = qkv_proj(xb, wqa, l)
                o = attn_a_fwd(qkv, vr, B, S)
                xo, xob, xhat, rstd = mm_ln(o, woa, l, xf, gam, bet, alpha, 1.0, False)
                saved.append(("a", l, xb, qkv, o, None, xhat, rstd, gam))
            else:
                l = i // 2
                qkv = qkv_proj(xb, wqb, l)
                o, ntot = attn_b_fwd(qkv, B, S)
                xo, xob, xhat, rstd = mm_ln(o, wob, l, xf, gam, bet, alpha, 1.0, False)
                saved.append(("b", l, xb, qkv, o, ntot, xhat, rstd, gam))
            xf, xb = xo, xob

    dy, loss_part = loss_head(xf, loss_target.reshape(M, D))
    loss = lax.psum(loss_part[0, 0], ("x", "y", "c"))

    la, lb = w_qkv_a.shape[0], w_qkv_b.shape[0]
    Cq, Ro = w_qkv_a.shape[-1], w_o_a.shape[1]
    gqa = jnp.zeros((la, N_CHIPS, D, Cq), BF16)
    goa = jnp.zeros((la, N_CHIPS, Ro, D), BF16)
    gqb = jnp.zeros((lb, N_CHIPS, D, Cq), BF16)
    gob = jnp.zeros((lb, N_CHIPS, Ro, D), BF16)
    ggate = jnp.zeros((n_ffn, N_CHIPS, D, Fs), BF16)
    gup = jnp.zeros((n_ffn, N_CHIPS, D, Fs), BF16)
    gdown = jnp.zeros((n_ffn, N_CHIPS, Fs, D), BF16)
    dgam, dbet, dvrs = [], [], []
    for rec in reversed(saved):
        kind, l, xb_in, t1, t2, t3, xhat, rstd, gam = rec
        scale = 0.5 if kind == "ffn" else 1.0
        dpre, dyb, dg_, db_ = ln_bwd(dy, xhat, rstd, gam, scale)
        dgam.append(dg_)
        dbet.append(db_)
        if kind == "ffn":
            g, u, h = t1, t2, t3
            dg, du = ffn_bwd_act(dyb, wd, l, g, u)
            gdown = dw_ffn_down(gdown, l, h, dyb)
            ggate = dw_ffn_up(ggate, l, xb_in, dg)
            gup = dw_ffn_up(gup, l, xb_in, du)
            dy = ffn_dx(dg, du, wg, wu, l, dpre, alpha)
        else:
            qkv, o = t1, t2
            wo, wq = (woa, wqa) if kind == "a" else (wob, wqb)
            do = o_proj_bwd(dyb, wo, l)
            if kind == "a":
                goa = dw_o(goa, l, o, dyb)
                dq, dk, dv, dvr = attn_a_bwd(qkv, vr, do, B, S)
                dvrs.append(dvr.reshape(B, H, VR_W))
            else:
                gob = dw_o(gob, l, o, dyb)
                dq, dk, dv = attn_b_bwd(qkv, do, t3, B, S)
            dqkv = jnp.concatenate([dq, dk, dv], axis=1)
            if kind == "a":
                gqa = dw_qkv(gqa, l, xb_in, dqkv)
            else:
                gqb = dw_qkv(gqb, l, xb_in, dqkv)
            dy = qkv_proj_bwd(dqkv, wq, l, dpre, alpha)
    grad_x = dy.reshape(B, S, D)

    def ln_family(parts):
        full = jnp.concatenate(parts[::-1], axis=0).reshape(depth, 3, N_CHIPS, D // N_CHIPS)
        return jnp.moveaxis(full, 2, 1)

    glng, glnb = ln_family(dgam), ln_family(dbet)

    cidx = lax.axis_index("c").astype(jnp.int32).reshape(1)
    fams = [gqa, goa, gqb, gob, ggate, gup, gdown, glng, glnb]
    from_sib = pair_exchange(fams)
    parts = [pair_add(cidx, g_, r_) for g_, r_ in zip(fams, from_sib)]
    lands = chip_exchange(parts)
    sums = [chip_sum(cidx, ld, g_.shape[0]) for ld, g_ in zip(lands, fams)]
    g_qa, g_oa, g_qb, g_ob, g_gate, g_up, g_down, g_lng, g_lnb = half_swap(sums)

    d_tab_t = bias_vec_bwd(jnp.concatenate(dvrs, axis=0))
    rows = -(-(H * REL_PAD) // (8 * 128)) * 8
    flat = jnp.pad(d_tab_t.reshape(-1), (0, rows * 128 - H * REL_PAD)).reshape(rows, 128)
    tot = all_sum_small(flat).reshape(-1)[:H * REL_PAD].reshape(H, REL_PAD)
    g_rel = tot[:, :N_REL].T

    def upd(w, g, m, v):
        shp = w.shape
        w3, m3, v3 = (a.reshape(g.shape) for a in (w, m, v))
        d, nm, nv = adamw(w3, g, m3, v3)
        return g.reshape(shp), d.reshape(shp), nm.reshape(shp), nv.reshape(shp)

    res = [
        upd(w_qkv_a, g_qa, m_w_qkv_a, v_w_qkv_a),
        upd(w_o_a, g_oa, m_w_o_a, v_w_o_a),
        upd(rel_bias, g_rel.reshape(1, N_REL, H), m_rel_bias, v_rel_bias),
        upd(w_qkv_b, g_qb, m_w_qkv_b, v_w_qkv_b),
        upd(w_o_b, g_ob, m_w_o_b, v_w_o_b),
        upd(ffn_w_gate, g_gate, m_ffn_w_gate, v_ffn_w_gate),
        upd(ffn_w_up, g_up, m_ffn_w_up, v_ffn_w_up),
        upd(ffn_w_down, g_down, m_ffn_w_down, v_ffn_w_down),
        upd(ln_g, g_lng, m_ln_g, v_ln_g),
        upd(ln_b, g_lnb, m_ln_b, v_ln_b),
    ]
    grads = [r[0] for r in res]
    deltas = [r[1] for r in res]
    new_m = [r[2] for r in res]
    new_v = [r[3] for r in res]
    return (loss, grad_x, *grads, *deltas, *new_m, *new_v)
```

```python
import functools
import math

import jax
import jax.numpy as jnp
from jax import lax
from jax.experimental import pallas as pl
from jax.experimental.pallas import tpu as pltpu

F32 = jnp.float32
BF16 = jnp.bfloat16
MESH = pl.DeviceIdType.MESH

N_CHIPS = 4
HEAD_DIM = 64
CHUNK = 64
LEFT_CHUNKS = 8
LOOKBACK = LEFT_CHUNKS * CHUNK
REL_CLIP = 128
N_REL = 2 * REL_CLIP + 1
REL_PAD = 384
SB_TILE = 256
QB_A = 256
KW_A = QB_A + LOOKBACK
VR_W = 1024
VR_C0 = KW_A - 1
LN_EPS = 1e-5
ADAM_LR, ADAM_B1, ADAM_B2, ADAM_EPS, ADAM_WD, ADAM_STEP = 0.001, 0.9, 0.999, 1e-08, 0.01, 10
NEG = -1e30
VMEM_LIMIT = 56 * 1024 * 1024

NT_DIMS = (((1,), (1,)), ((), ()))
TN_DIMS = (((0,), (0,)), ((), ()))
ANY = pl.BlockSpec(memory_space=pl.ANY)


def _params(sem=None):
    return pltpu.CompilerParams(dimension_semantics=sem, vmem_limit_bytes=VMEM_LIMIT)


def _tile(n, pref):
    t = min(n, pref)
    assert n % t == 0, (n, pref)
    return t


def _sigmoid(z):
    return 1.0 / (1.0 + jnp.exp(-z))


def _mm_call(name, operands, in_specs, out_shape, out_spec, grid, dims_list, acc_shape,
             add_coef=None, aliases=None):
    n_pairs = len(dims_list)
    nk = grid[-1]
    has_add = add_coef is not None
    n_alias = len(aliases) if aliases else 0

    def body(*refs):
        pair_refs = refs[:2 * n_pairs]
        pos = 2 * n_pairs
        add_ref = refs[pos] if has_add else None
        pos += (1 if has_add else 0) + n_alias
        o_ref = refs[pos]
        acc_ref = refs[pos + 1] if nk > 1 else None

        part = None
        for i, dims in enumerate(dims_list):
            d = lax.dot_general(pair_refs[2 * i][...], pair_refs[2 * i + 1][...], dims,
                                preferred_element_type=F32)
            part = d if part is None else part + d

        def finish(r):
            if has_add:
                r = r + add_coef * add_ref[...]
            o_ref[...] = r.astype(o_ref.dtype)

        if nk == 1:
            finish(part)
        else:
            k = pl.program_id(len(grid) - 1)

            @pl.when(k == 0)
            def _():
                acc_ref[...] = part

            @pl.when(k > 0)
            def _():
                acc_ref[...] += part

            @pl.when(k == nk - 1)
            def _():
                finish(acc_ref[...])

    sem = ("parallel",) * (len(grid) - 1) + ("arbitrary",)
    return pl.pallas_call(
        body, name=name, grid=grid, in_specs=in_specs, out_specs=out_spec, out_shape=out_shape,
        scratch_shapes=[pltpu.VMEM(acc_shape, F32)] if nk > 1 else [],
        input_output_aliases=aliases or {},
        compiler_params=_params(sem),
    )(*operands)


def qkv_proj(xb, w, l):
    M, D = xb.shape
    C = w.shape[-1]
    tm = _tile(M, 512)
    return _mm_call(
        "qkv_proj", (xb, w),
        [pl.BlockSpec((tm, D), lambda p, i, k: (i, 0)),
         pl.BlockSpec((None, None, D, C), lambda p, i, k: (p, l, 0, 0))],
        jax.ShapeDtypeStruct((M, N_CHIPS * C), BF16),
        pl.BlockSpec((tm, C), lambda p, i, k: (i, p)),
        (N_CHIPS, M // tm, 1), [(((1,), (0,)), ((), ()))], None)


def o_proj_bwd(dyb, w, l):
    M, D = dyb.shape
    R = w.shape[2]
    tm = _tile(M, 512)
    return _mm_call(
        "o_proj_bwd", (dyb, w),
        [pl.BlockSpec((tm, D), lambda p, i, k: (i, 0)),
         pl.BlockSpec((None, None, R, D), lambda p, i, k: (p, l, 0, 0))],
        jax.ShapeDtypeStruct((M, N_CHIPS * R), BF16),
        pl.BlockSpec((tm, R), lambda p, i, k: (i, p)),
        (N_CHIPS, M // tm, 1), [NT_DIMS], None)


def qkv_proj_bwd(dqkv, w, l, dpre, alpha):
    M = dqkv.shape[0]
    D, C = w.shape[2], w.shape[3]
    tm = _tile(M, 512)
    return _mm_call(
        "qkv_proj_bwd", (dqkv, w, dpre),
        [pl.BlockSpec((tm, C), lambda i, p: (i, p)),
         pl.BlockSpec((None, None, D, C), lambda i, p: (p, l, 0, 0)),
         pl.BlockSpec((tm, D), lambda i, p: (i, 0))],
        jax.ShapeDtypeStruct((M, D), F32),
        pl.BlockSpec((tm, D), lambda i, p: (i, 0)),
        (M // tm, N_CHIPS), [NT_DIMS], (tm, D), add_coef=alpha)


def ffn_dx(dg, du, wg, wu, l, dpre, alpha):
    _, M, Fs = dg.shape
    D = wg.shape[2]
    tm = _tile(M, 512)
    act = pl.BlockSpec((None, tm, Fs), lambda i, p: (p, i, 0))
    wsp = pl.BlockSpec((None, None, D, Fs), lambda i, p: (p, l, 0, 0))
    return _mm_call(
        "ffn_dx", (dg, wg, du, wu, dpre),
        [act, wsp, act, wsp, pl.BlockSpec((tm, D), lambda i, p: (i, 0))],
        jax.ShapeDtypeStruct((M, D), F32),
        pl.BlockSpec((tm, D), lambda i, p: (i, 0)),
        (M // tm, N_CHIPS), [NT_DIMS, NT_DIMS], (tm, D), add_coef=alpha)


def _dw_call(name, buf, l, a, b, a_spec, b_spec, M, tk):
    _, _, R, C = buf.shape
    return _mm_call(
        name, (a, b, buf),
        [a_spec, b_spec, ANY],
        jax.ShapeDtypeStruct(buf.shape, buf.dtype),
        pl.BlockSpec((None, None, R, C), lambda p, k: (l, p, 0, 0)),
        (N_CHIPS, M // tk), [TN_DIMS], (R, C), aliases={2: 0})


def dw_ffn_down(buf, l, h, dyb):
    _, M, Fs = h.shape
    D = dyb.shape[1]
    tk = _tile(M, 512)
    return _dw_call("dw_ffn_down", buf, l, h, dyb,
                    pl.BlockSpec((None, tk, Fs), lambda p, k: (p, k, 0)),
                    pl.BlockSpec((tk, D), lambda p, k: (k, 0)), M, tk)


def dw_ffn_up(buf, l, xb, dg):
    _, M, Fs = dg.shape
    D = xb.shape[1]
    tk = _tile(M, 512)
    return _dw_call("dw_ffn_up", buf, l, xb, dg,
                    pl.BlockSpec((tk, D), lambda p, k: (k, 0)),
                    pl.BlockSpec((None, tk, Fs), lambda p, k: (p, k, 0)), M, tk)


def dw_qkv(buf, l, xb, dqkv):
    M, D = xb.shape
    C = buf.shape[-1]
    tk = _tile(M, 512)
    return _dw_call("dw_qkv", buf, l, xb, dqkv,
                    pl.BlockSpec((tk, D), lambda p, k: (k, 0)),
                    pl.BlockSpec((tk, C), lambda p, k: (k, p)), M, tk)


def dw_o(buf, l, o, dyb):
    M, D = dyb.shape
    R = buf.shape[2]
    tk = _tile(M, 512)
    return _dw_call("dw_o", buf, l, o, dyb,
                    pl.BlockSpec((tk, R), lambda p, k: (k, p)),
                    pl.BlockSpec((tk, D), lambda p, k: (k, 0)), M, tk)


def ffn_up(xb, wg, wu, l):
    M, D = xb.shape
    Fs = wg.shape[-1]
    tm = _tile(M, 512)

    def body(x_ref, wg_ref, wu_ref, g_ref, u_ref, h_ref):
        x = x_ref[...]
        g = jnp.dot(x, wg_ref[...], preferred_element_type=F32)
        u = jnp.dot(x, wu_ref[...], preferred_element_type=F32)
        g_ref[...] = g.astype(BF16)
        u_ref[...] = u.astype(BF16)
        h_ref[...] = (g * _sigmoid(g) * u).astype(BF16)

    wsp = pl.BlockSpec((None, None, D, Fs), lambda p, i: (p, l, 0, 0))
    osp = pl.BlockSpec((None, tm, Fs), lambda p, i: (p, i, 0))
    osh = jax.ShapeDtypeStruct((N_CHIPS, M, Fs), BF16)
    return pl.pallas_call(
        body, name="ffn_up", grid=(N_CHIPS, M // tm),
        in_specs=[pl.BlockSpec((tm, D), lambda p, i: (i, 0)), wsp, wsp],
        out_specs=[osp, osp, osp], out_shape=[osh, osh, osh],
        compiler_params=_params(("parallel", "parallel")),
    )(xb, wg, wu)


def ffn_bwd_act(dyb, wd, l, g, u):
    M, D = dyb.shape
    Fs = wd.shape[2]
    tm = _tile(M, 512)

    def body(dy_ref, wd_ref, g_ref, u_ref, dg_ref, du_ref):
        dh = lax.dot_general(dy_ref[...], wd_ref[...], NT_DIMS, preferred_element_type=F32)
        gf = g_ref[...].astype(F32)
        uf = u_ref[...].astype(F32)
        sig = _sigmoid(gf)
        dg_ref[...] = (dh * uf * (sig * (1.0 + gf * (1.0 - sig)))).astype(BF16)
        du_ref[...] = (dh * (gf * sig)).astype(BF16)

    asp = pl.BlockSpec((None, tm, Fs), lambda p, i: (p, i, 0))
    osh = jax.ShapeDtypeStruct((N_CHIPS, M, Fs), BF16)
    return pl.pallas_call(
        body, name="ffn_bwd_act", grid=(N_CHIPS, M // tm),
        in_specs=[pl.BlockSpec((tm, D), lambda p, i: (i, 0)),
                  pl.BlockSpec((None, None, Fs, D), lambda p, i: (p, l, 0, 0)), asp, asp],
        out_specs=[asp, asp], out_shape=[osh, osh],
        compiler_params=_params(("parallel", "parallel")),
    )(dyb, wd, g, u)


def mm_ln(a, w, l, x, gam, bet, alpha, scale, a_piece_major):
    M, D = x.shape
    R = w.shape[2]
    tm = _tile(M, 512)
    if a_piece_major:
        a_spec = pl.BlockSpec((None, tm, R), lambda i, p: (p, i, 0))
    else:
        a_spec = pl.BlockSpec((tm, R), lambda i, p: (i, p))

    def body(a_ref, w_ref, x_ref, g_ref, b_ref, xo_ref, xb_ref, xh_ref, rs_ref, acc_ref):
        p = pl.program_id(1)
        part = jnp.dot(a_ref[...], w_ref[...], preferred_element_type=F32)

        @pl.when(p == 0)
        def _():
            acc_ref[...] = part

        @pl.when(p > 0)
        def _():
            acc_ref[...] += part

        @pl.when(p == N_CHIPS - 1)
        def _():
            pre = alpha * x_ref[...] + scale * acc_ref[...]
            mu = jnp.mean(pre, axis=-1, keepdims=True)
            cen = pre - mu
            var = jnp.mean(cen * cen, axis=-1, keepdims=True)
            rstd = lax.rsqrt(var + LN_EPS)
            xhat = cen * rstd
            out = xhat * g_ref[...] + b_ref[...]
            xo_ref[...] = out
            xb_ref[...] = out.astype(BF16)
            xh_ref[...] = xhat
            rs_ref[...] = rstd

    row = pl.BlockSpec((tm, D), lambda i, p: (i, 0))
    vec = pl.BlockSpec((1, D), lambda i, p: (0, 0))
    return pl.pallas_call(
        body, name="mm_ln", grid=(M // tm, N_CHIPS),
        in_specs=[a_spec, pl.BlockSpec((None, None, R, D), lambda i, p: (p, l, 0, 0)), row, vec, vec],
        out_specs=[row, row, row, pl.BlockSpec((tm, 1), lambda i, p: (i, 0))],
        out_shape=[jax.ShapeDtypeStruct((M, D), F32), jax.ShapeDtypeStruct((M, D), BF16),
                   jax.ShapeDtypeStruct((M, D), F32), jax.ShapeDtypeStruct((M, 1), F32)],
        scratch_shapes=[pltpu.VMEM((tm, D), F32)],
        compiler_params=_params(("parallel", "arbitrary")),
    )(a, w, x, gam, bet)


def ln_bwd(dy, xhat, rstd, gam, scale):
    M, D = dy.shape
    tm = _tile(M, 512)

    def body(dy_ref, xh_ref, rs_ref, g_ref, dp_ref, db16_ref, dg_ref, dbt_ref):
        i = pl.program_id(0)
        dy_v = dy_ref[...]
        xh = xh_ref[...]
        dxh = dy_v * g_ref[...]
        m1 = jnp.mean(dxh, axis=-1, keepdims=True)
        m2 = jnp.mean(dxh * xh, axis=-1, keepdims=True)
        dpre = rs_ref[...] * (dxh - m1 - xh * m2)
        dp_ref[...] = dpre
        db16_ref[...] = (scale * dpre).astype(BF16)
        dgp = jnp.sum(dy_v * xh, axis=0, keepdims=True)
        dbp = jnp.sum(dy_v, axis=0, keepdims=True)

        @pl.when(i == 0)
        def _():
            dg_ref[...] = dgp
            dbt_ref[...] = dbp

        @pl.when(i > 0)
        def _():
            dg_ref[...] += dgp
            dbt_ref[...] += dbp

    row = pl.BlockSpec((tm, D), lambda i: (i, 0))
    vec = pl.BlockSpec((1, D), lambda i: (0, 0))
    return pl.pallas_call(
        body, name="ln_bwd", grid=(M // tm,),
        in_specs=[row, row, pl.BlockSpec((tm, 1), lambda i: (i, 0)), vec],
        out_specs=[row, row, vec, vec],
        out_shape=[jax.ShapeDtypeStruct((M, D), F32), jax.ShapeDtypeStruct((M, D), BF16),
                   jax.ShapeDtypeStruct((1, D), F32), jax.ShapeDtypeStruct((1, D), F32)],
        compiler_params=_params(("arbitrary",)),
    )(dy, xhat, rstd, gam)


def loss_head(y, tgt):
    M, D = y.shape
    tm = _tile(M, 512)
    n = M // tm

    def body(y_ref, t_ref, dy_ref, l_ref, acc_ref):
        i = pl.program_id(0)
        e = y_ref[...] - t_ref[...]
        dy_ref[...] = e * (1.0 / D)
        part = jnp.sum(e * e, axis=0, keepdims=True)

        @pl.when(i == 0)
        def _():
            acc_ref[...] = part

        @pl.when(i > 0)
        def _():
            acc_ref[...] += part

        @pl.when(i == n - 1)
        def _():
            l_ref[...] = (0.5 / D) * jnp.sum(acc_ref[...], axis=1, keepdims=True)

    row = pl.BlockSpec((tm, D), lambda i: (i, 0))
    return pl.pallas_call(
        body, name="loss_head", grid=(n,),
        in_specs=[row, row],
        out_specs=[row, pl.BlockSpec((1, 1), lambda i: (0, 0))],
        out_shape=[jax.ShapeDtypeStruct((M, D), F32), jax.ShapeDtypeStruct((1, 1), F32)],
        scratch_shapes=[pltpu.VMEM((1, D), F32)],
        compiler_params=_params(("arbitrary",)),
    )(y, tgt)


def _rel_onehot_t():
    r = lax.broadcasted_iota(jnp.int32, (REL_PAD, VR_W), 0)
    n = lax.broadcasted_iota(jnp.int32, (REL_PAD, VR_W), 1)
    idx = jnp.clip(VR_C0 - n, -REL_CLIP, REL_CLIP) + REL_CLIP
    return (r == idx).astype(F32)


def bias_vec(tab_t):
    H = tab_t.shape[0]

    def body(t_ref, o_ref):
        o_ref[...] = jnp.dot(t_ref[...], _rel_onehot_t(), precision=lax.Precision.HIGHEST,
                             preferred_element_type=F32)

    return pl.pallas_call(
        body, name="bias_vec", out_shape=jax.ShapeDtypeStruct((H, VR_W), F32),
        compiler_params=_params(),
    )(tab_t)


def bias_vec_bwd(dvr):
    n, H, _ = dvr.shape

    def body(d_ref, o_ref):
        tot = d_ref[0]
        for i in range(1, n):
            tot = tot + d_ref[i]
        o_ref[...] = lax.dot_general(tot, _rel_onehot_t(), NT_DIMS, precision=lax.Precision.HIGHEST,
                                     preferred_element_type=F32)

    return pl.pallas_call(
        body, name="bias_vec_bwd", out_shape=jax.ShapeDtypeStruct((H, REL_PAD), F32),
        compiler_params=_params(),
    )(dvr)


def _a_bias_mask(vr_row):
    xb = jnp.broadcast_to(vr_row, (QB_A, VR_W))
    tile = pltpu.roll(xb, VR_W - (QB_A - 1), 1, stride=1, stride_axis=0)[:, :KW_A]
    qc = lax.broadcasted_iota(jnp.int32, (QB_A, KW_A), 0) // CHUNK
    kc = lax.broadcasted_iota(jnp.int32, (QB_A, KW_A), 1) // CHUNK
    valid = (kc >= qc) & (kc <= qc + LEFT_CHUNKS)
    return jnp.where(valid, tile, NEG)


def _a_diag_sums(db_acc, h):
    acc8 = None
    for a in range(QB_A // 8):
        grp = db_acc[h, 8 * a:8 * a + 8, :]
        shift = QB_A - 8 - 8 * a
        if shift:
            grp = pltpu.roll(grp, shift, 1)
        acc8 = grp if acc8 is None else acc8 + grp
    sub = lax.broadcasted_iota(jnp.int32, (8, VR_W), 0)
    tot = jnp.zeros((8, VR_W), F32)
    for b in range(8):
        moved = pltpu.roll(acc8, 7 - b, 1) if b < 7 else acc8
        tot = tot + jnp.where(sub == b, moved, 0.0)
    return jnp.sum(tot, axis=0, keepdims=True)


def _a_blocks(S):
    out = []
    for qi in range(S // QB_A):
        q0 = qi * QB_A
        ks = max(0, q0 - LOOKBACK)
        out.append((q0, ks, q0 + QB_A, ks - (q0 - LOOKBACK)))
    return out


def _head_specs(S, HP):
    q = pl.BlockSpec((S, 2 * HEAD_DIM), lambda b, hp: (b, hp))
    k = pl.BlockSpec((S, 2 * HEAD_DIM), lambda b, hp: (b, HP + hp))
    v = pl.BlockSpec((S, 2 * HEAD_DIM), lambda b, hp: (b, 2 * HP + hp))
    return q, k, v


def attn_a_fwd(qkv, vr, B, S):
    D = qkv.shape[1] // 3
    HP = D // (2 * HEAD_DIM)
    scale = HEAD_DIM ** -0.5
    blocks = _a_blocks(S)

    def body(q_ref, k_ref, v_ref, vr_ref, o_ref):
        for h in range(2):
            lo = h * HEAD_DIM
            bm = _a_bias_mask(vr_ref[h:h + 1, :])
            for (q0, ks, ke, joff) in blocks:
                q = q_ref[q0:q0 + QB_A, lo:lo + HEAD_DIM]
                k = k_ref[ks:ke, lo:lo + HEAD_DIM]
                v = v_ref[ks:ke, lo:lo + HEAD_DIM]
                s = lax.dot_general(q, k, NT_DIMS, preferred_element_type=F32) * scale + bm[:, joff:]
                m = jnp.max(s, axis=-1, keepdims=True)
                p = jnp.exp(s - m)
                den = jnp.sum(p, axis=-1, keepdims=True)
                o = jnp.dot(p.astype(BF16), v, preferred_element_type=F32) / den
                o_ref[q0:q0 + QB_A, lo:lo + HEAD_DIM] = o.astype(BF16)

    qs, ks_, vs = _head_specs(S, HP)
    return pl.pallas_call(
        body, name="attn_a_fwd", grid=(B, HP),
        in_specs=[qs, ks_, vs, pl.BlockSpec((None, 2, VR_W), lambda b, hp: (hp, 0, 0))],
        out_specs=pl.BlockSpec((S, 2 * HEAD_DIM), lambda b, hp: (b, hp)),
        out_shape=jax.ShapeDtypeStruct((B * S, D), BF16),
        compiler_params=_params(("parallel", "parallel")),
    )(qkv, qkv, qkv, vr)


def attn_a_bwd(qkv, vr, do, B, S):
    D = qkv.shape[1] // 3
    HP = D // (2 * HEAD_DIM)
    scale = HEAD_DIM ** -0.5
    blocks = _a_blocks(S)

    def body(q_ref, k_ref, v_ref, vr_ref, do_ref, dq_ref, dk_ref, dv_ref, dvr_ref,
             dk_acc, dv_acc, db_acc):
        dk_acc[...] = jnp.zeros_like(dk_acc)
        dv_acc[...] = jnp.zeros_like(dv_acc)
        db_acc[...] = jnp.zeros_like(db_acc)
        for h in range(2):
            lo = h * HEAD_DIM
            bm = _a_bias_mask(vr_ref[h:h + 1, :])
            for (q0, ks, ke, joff) in blocks:
                q = q_ref[q0:q0 + QB_A, lo:lo + HEAD_DIM]
                k = k_ref[ks:ke, lo:lo + HEAD_DIM]
                v = v_ref[ks:ke, lo:lo + HEAD_DIM]
                dob = do_ref[q0:q0 + QB_A, lo:lo + HEAD_DIM]
                s = lax.dot_general(q, k, NT_DIMS, preferred_element_type=F32) * scale + bm[:, joff:]
                m = jnp.max(s, axis=-1, keepdims=True)
                e = jnp.exp(s - m)
                p = e / jnp.sum(e, axis=-1, keepdims=True)
                dp = lax.dot_general(dob, v, NT_DIMS, preferred_element_type=F32)
                ds = p * (dp - jnp.sum(p * dp, axis=-1, keepdims=True))
                dsb = ds.astype(BF16)
                dq = jnp.dot(dsb, k, preferred_element_type=F32) * scale
                dq_ref[q0:q0 + QB_A, lo:lo + HEAD_DIM] = dq.astype(BF16)
                dk_acc[ks:ke, lo:lo + HEAD_DIM] += lax.dot_general(
                    dsb, q, TN_DIMS, preferred_element_type=F32) * scale
                dv_acc[ks:ke, lo:lo + HEAD_DIM] += lax.dot_general(
                    p.astype(BF16), dob, TN_DIMS, preferred_element_type=F32)
                db_acc[h, :, joff:KW_A] += ds
            dvr_ref[h:h + 1, :] = _a_diag_sums(db_acc, h)
        dk_ref[...] = dk_acc[...].astype(BF16)
        dv_ref[...] = dv_acc[...].astype(BF16)

    qs, ks_, vs = _head_specs(S, HP)
    hd = pl.BlockSpec((S, 2 * HEAD_DIM), lambda b, hp: (b, hp))
    osh = jax.ShapeDtypeStruct((B * S, D), BF16)
    return pl.pallas_call(
        body, name="attn_a_bwd", grid=(B, HP),
        in_specs=[qs, ks_, vs, pl.BlockSpec((None, 2, VR_W), lambda b, hp: (hp, 0, 0)), hd],
        out_specs=[hd, hd, hd, pl.BlockSpec((None, None, 2, VR_W), lambda b, hp: (b, hp, 0, 0))],
        out_shape=[osh, osh, osh, jax.ShapeDtypeStruct((B, HP, 2, VR_W), F32)],
        scratch_shapes=[pltpu.VMEM((S, 2 * HEAD_DIM), F32), pltpu.VMEM((S, 2 * HEAD_DIM), F32),
                        pltpu.VMEM((2, QB_A, VR_W), F32)],
        compiler_params=_params(("parallel", "parallel")),
    )(qkv, qkv, qkv, vr, do)


def _tri(cmp):
    j = lax.broadcasted_iota(jnp.int32, (SB_TILE, SB_TILE), 0)
    s = lax.broadcasted_iota(jnp.int32, (SB_TILE, SB_TILE), 1)
    return cmp(j, s).astype(BF16)


def _cumsum_mm(x, tri):
    hi = x.astype(BF16)
    mid = (x - hi.astype(F32)).astype(BF16)
    return jnp.dot(hi, tri, preferred_element_type=F32) + jnp.dot(mid, tri, preferred_element_type=F32)


def _sb_logs(q, k, scale, diagonal):
    z = lax.dot_general(q, k, NT_DIMS, preferred_element_type=F32) * scale
    log_b = jnp.minimum(z, 0.0) - jnp.log(1.0 + jnp.exp(-jnp.abs(z)))
    log_1mb = log_b - z
    if not diagonal:
        return log_b, log_1mb, None
    row = lax.broadcasted_iota(jnp.int32, (SB_TILE, SB_TILE), 0)
    col = lax.broadcasted_iota(jnp.int32, (SB_TILE, SB_TILE), 1)
    causal = col < row
    return log_b, jnp.where(causal, log_1mb, 0.0), causal


def attn_b_fwd(qkv, B, S):
    D = qkv.shape[1] // 3
    HP = D // (2 * HEAD_DIM)
    scale = HEAD_DIM ** -0.5
    nb = S // SB_TILE

    def body(q_ref, k_ref, v_ref, o_ref, nt_ref):
        tri = _tri(lambda j, s: j > s)
        for h in range(2):
            lo = h * HEAD_DIM

            def q_loop(qb, carry):
                q0 = pl.multiple_of(qb * SB_TILE, SB_TILE)
                q = q_ref[pl.ds(q0, SB_TILE), lo:lo + HEAD_DIM]

                def block(k0, diagonal, right, acc):
                    k = k_ref[pl.ds(k0, SB_TILE), lo:lo + HEAD_DIM]
                    v = v_ref[pl.ds(k0, SB_TILE), lo:lo + HEAD_DIM]
                    log_b, log_1mb, causal = _sb_logs(q, k, scale, diagonal)
                    a = jnp.exp(log_b + _cumsum_mm(log_1mb, tri) + right)
                    if diagonal:
                        a = jnp.where(causal, a, 0.0)
                    acc = acc + jnp.dot(a.astype(BF16), v, preferred_element_type=F32)
                    return right + jnp.sum(log_1mb, axis=-1, keepdims=True), acc

                def k_loop(t, kc):
                    k0 = pl.multiple_of((qb - 1 - t) * SB_TILE, SB_TILE)
                    return block(k0, False, *kc)

                kc = block(q0, True, jnp.zeros((SB_TILE, 1), F32), jnp.zeros((SB_TILE, HEAD_DIM), F32))
                right, acc = lax.fori_loop(0, qb, k_loop, kc)
                o_ref[pl.ds(q0, SB_TILE), lo:lo + HEAD_DIM] = acc.astype(BF16)
                nt_ref[pl.ds(q0, SB_TILE), lo:lo + HEAD_DIM] = jnp.broadcast_to(right, (SB_TILE, HEAD_DIM))
                return carry

            lax.fori_loop(0, nb, q_loop, 0)

    qs, ks_, vs = _head_specs(S, HP)
    hd = pl.BlockSpec((S, 2 * HEAD_DIM), lambda b, hp: (b, hp))
    return pl.pallas_call(
        body, name="attn_b_fwd", grid=(B, HP),
        in_specs=[qs, ks_, vs], out_specs=[hd, hd],
        out_shape=[jax.ShapeDtypeStruct((B * S, D), BF16), jax.ShapeDtypeStruct((B * S, D), F32)],
        compiler_params=_params(("parallel", "parallel")),
    )(qkv, qkv, qkv)


def attn_b_bwd(qkv, do, ntot, B, S):
    D = qkv.shape[1] // 3
    HP = D // (2 * HEAD_DIM)
    scale = HEAD_DIM ** -0.5
    nb = S // SB_TILE

    def body(q_ref, k_ref, v_ref, do_ref, nt_ref, dq_ref, dk_ref, dv_ref, dkt_acc, dvt_acc):
        tri_incl = _tri(lambda j, s: j <= s)
        tri_excl = _tri(lambda j, s: j < s)
        dkt_acc[...] = jnp.zeros_like(dkt_acc)
        dvt_acc[...] = jnp.zeros_like(dvt_acc)

        def q_loop(qb, carry):
            q0 = pl.multiple_of(qb * SB_TILE, SB_TILE)
            rows = pl.ds(q0, SB_TILE)
            qt_pair = q_ref[rows, :].astype(F32).T.astype(BF16)
            dot_pair = do_ref[rows, :].astype(F32).T.astype(BF16)
            for h in range(2):
                lo = h * HEAD_DIM
                q = q_ref[rows, lo:lo + HEAD_DIM]
                dob = do_ref[rows, lo:lo + HEAD_DIM]
                qt = qt_pair[lo:lo + HEAD_DIM, :]
                dot_ = dot_pair[lo:lo + HEAD_DIM, :]
                nt = nt_ref[rows, lo:lo + 1]

                def block(kb, diagonal, left_n, left_d, dq_acc):
                    k0 = pl.multiple_of(kb * SB_TILE, SB_TILE)
                    k = k_ref[pl.ds(k0, SB_TILE), lo:lo + HEAD_DIM]
                    v = v_ref[pl.ds(k0, SB_TILE), lo:lo + HEAD_DIM]
                    log_b, log_1mb, causal = _sb_logs(q, k, scale, diagonal)
                    a = jnp.exp(log_b + (nt - left_n) - _cumsum_mm(log_1mb, tri_incl))
                    if diagonal:
                        a = jnp.where(causal, a, 0.0)
                    dl = a * lax.dot_general(dob, v, NT_DIMS, preferred_element_type=F32)
                    dn = left_d + _cumsum_mm(dl, tri_excl)
                    dz = dl * jnp.exp(log_1mb) - dn * jnp.exp(log_b)
                    if diagonal:
                        dz = jnp.where(causal, dz, 0.0)
                    dzb = dz.astype(BF16)
                    dq_acc = dq_acc + jnp.dot(dzb, k, preferred_element_type=F32)
                    dkt_acc[kb, lo:lo + HEAD_DIM, :] += jnp.dot(qt, dzb, preferred_element_type=F32)
                    dvt_acc[kb, lo:lo + HEAD_DIM, :] += jnp.dot(dot_, a.astype(BF16),
                                                                preferred_element_type=F32)
                    return (left_n + jnp.sum(log_1mb, axis=-1, keepdims=True),
                            left_d + jnp.sum(dl, axis=-1, keepdims=True), dq_acc)

                zero1 = jnp.zeros((SB_TILE, 1), F32)
                kc = lax.fori_loop(0, qb, lambda kb, kc: block(kb, False, *kc),
                                   (zero1, zero1, jnp.zeros((SB_TILE, HEAD_DIM), F32)))
                _, _, dq_acc = block(qb, True, *kc)
                dq_ref[rows, lo:lo + HEAD_DIM] = (dq_acc * scale).astype(BF16)
            return carry

        lax.fori_loop(0, nb, q_loop, 0)
        for kb in range(nb):
            dk_ref[kb * SB_TILE:(kb + 1) * SB_TILE, :] = (dkt_acc[kb].T * scale).astype(BF16)
            dv_ref[kb * SB_TILE:(kb + 1) * SB_TILE, :] = dvt_acc[kb].T.astype(BF16)

    qs, ks_, vs = _head_specs(S, HP)
    hd = pl.BlockSpec((S, 2 * HEAD_DIM), lambda b, hp: (b, hp))
    osh = jax.ShapeDtypeStruct((B * S, D), BF16)
    acc = pltpu.VMEM((nb, 2 * HEAD_DIM, SB_TILE), F32)
    return pl.pallas_call(
        body, name="attn_b_bwd", grid=(B, HP),
        in_specs=[qs, ks_, vs, hd, hd], out_specs=[hd, hd, hd], out_shape=[osh, osh, osh],
        scratch_shapes=[acc, acc],
        compiler_params=_params(("parallel", "parallel")),
    )(qkv, qkv, qkv, do, ntot)


def _place():
    x, y, c = lax.axis_index("x"), lax.axis_index("y"), lax.axis_index("c")
    chips = [(1 - x, y), (x, 1 - y), (1 - x, 1 - y)]
    return x, y, c, 2 * x + y, chips


def _remote(src, dst, send_sem, recv_sem, dev):
    return pltpu.make_async_remote_copy(src_ref=src, dst_ref=dst, send_sem=send_sem, recv_sem=recv_sem,
                                        device_id=dev, device_id_type=MESH)


def gather_weights(shards):
    n = len(shards)

    def body(*refs):
        ins, outs = refs[:n], refs[n:2 * n]
        send1, recv1, send2, recv2, lsem = refs[2 * n:]
        x, y, c, me, chips = _place()
        local, first = [], []
        for f in range(n):
            hl = shards[f].shape[0] // 2
            cp = pltpu.make_async_copy(ins[f], outs[f].at[me], lsem.at[f])
            cp.start()
            local.append(cp)
            for j, (qx, qy) in enumerate(chips):
                half = pl.ds(c * hl, hl)
                cp = _remote(ins[f].at[half], outs[f].at[me, half],
                             send1.at[3 * f + j], recv1.at[3 * f + j], (qx, qy, c))
                cp.start()
                first.append(cp)
        passed = []
        for f in range(n):
            hl = shards[f].shape[0] // 2
            for j, (qx, qy) in enumerate(chips):
                slab = outs[f].at[2 * qx + qy, pl.ds(c * hl, hl)]
                _remote(slab, slab, send1.at[3 * f + j], recv1.at[3 * f + j], (x, y, c)).wait_recv()
                cp = _remote(slab, slab, send2.at[3 * f + j], recv2.at[3 * f + j], (x, y, 1 - c))
                cp.start()
                passed.append(cp)
        for f in range(n):
            hl = shards[f].shape[0] // 2
            for j, (qx, qy) in enumerate(chips):
                slab = outs[f].at[2 * qx + qy, pl.ds((1 - c) * hl, hl)]
                _remote(slab, slab, send2.at[3 * f + j], recv2.at[3 * f + j], (x, y, c)).wait_recv()
        for cp in first + passed:
            cp.wait_send()
        for cp in local:
            cp.wait()

    sems = pltpu.SemaphoreType.DMA((3 * n,))
    return pl.pallas_call(
        body, name="gather_weights",
        in_specs=[ANY] * n, out_specs=[ANY] * n,
        out_shape=[jax.ShapeDtypeStruct((N_CHIPS,) + s.shape, s.dtype) for s in shards],
        scratch_shapes=[sems, sems, sems, sems, pltpu.SemaphoreType.DMA((n,))],
        compiler_params=pltpu.CompilerParams(has_side_effects=True),
    )(*shards)


def pair_exchange(grads):
    n = len(grads)

    def body(*refs):
        ins, outs = refs[:n], refs[n:2 * n]
        send, recv = refs[2 * n:]
        x, y, c, _, _ = _place()
        cps = []
        for f in range(n):
            hl = grads[f].shape[0] // 2
            cp = _remote(ins[f].at[pl.ds((1 - c) * hl, hl)], outs[f], send.at[f], recv.at[f], (x, y, 1 - c))
            cp.start()
            cps.append(cp)
        for cp in cps:
            cp.wait()

    sems = pltpu.SemaphoreType.DMA((n,))
    return pl.pallas_call(
        body, name="pair_exchange",
        in_specs=[ANY] * n, out_specs=[ANY] * n,
        out_shape=[jax.ShapeDtypeStruct((g.shape[0] // 2,) + g.shape[1:], g.dtype) for g in grads],
        scratch_shapes=[sems, sems],
        compiler_params=pltpu.CompilerParams(has_side_effects=True),
    )(*grads)


def chip_exchange(parts):
    n = len(parts)

    def body(*refs):
        ins, outs = refs[:n], refs[n:2 * n]
        send, recv, lsem = refs[2 * n:]
        x, y, c, me, chips = _place()
        local, sent = [], []
        for f in range(n):
            hl = parts[f].shape[0]
            rows = pl.ds(0, hl)
            cp = pltpu.make_async_copy(ins[f].at[rows, me], outs[f].at[rows, me], lsem.at[f])
            cp.start()
            local.append(cp)
            for j, (qx, qy) in enumerate(chips):
                cp = _remote(ins[f].at[rows, 2 * qx + qy], outs[f].at[rows, me],
                             send.at[3 * f + j], recv.at[3 * f + j], (qx, qy, c))
                cp.start()
                sent.append(cp)
        for f in range(n):
            rows = pl.ds(0, parts[f].shape[0])
            for j, (qx, qy) in enumerate(chips):
                slab = outs[f].at[rows, 2 * qx + qy]
                _remote(slab, slab, send.at[3 * f + j], recv.at[3 * f + j], (x, y, c)).wait_recv()
        for cp in sent:
            cp.wait_send()
        for cp in local:
            cp.wait()

    sems = pltpu.SemaphoreType.DMA((3 * n,))
    return pl.pallas_call(
        body, name="chip_exchange",
        in_specs=[ANY] * n, out_specs=[ANY] * n,
        out_shape=[jax.ShapeDtypeStruct(s.shape, s.dtype) for s in parts],
        scratch_shapes=[sems, sems, pltpu.SemaphoreType.DMA((n,))],
        compiler_params=pltpu.CompilerParams(has_side_effects=True),
    )(*parts)


def half_swap(grads):
    n = len(grads)

    def body(*refs):
        ins, outs = refs[:n], refs[n:2 * n]
        send, recv = refs[2 * n:]
        x, y, c, _, _ = _place()
        cps = []
        for f in range(n):
            hl = grads[f].shape[0] // 2
            mine = pl.ds(c * hl, hl)
            cp = _remote(outs[f].at[mine], outs[f].at[mine], send.at[f], recv.at[f], (x, y, 1 - c))
            cp.start()
            cps.append(cp)
        for f in range(n):
            hl = grads[f].shape[0] // 2
            theirs = outs[f].at[pl.ds((1 - c) * hl, hl)]
            _remote(theirs, theirs, send.at[f], recv.at[f], (x, y, c)).wait_recv()
        for cp in cps:
            cp.wait_send()

    sems = pltpu.SemaphoreType.DMA((n,))
    return pl.pallas_call(
        body, name="half_swap",
        in_specs=[ANY] * n, out_specs=[ANY] * n,
        out_shape=[jax.ShapeDtypeStruct(g.shape, g.dtype) for g in grads],
        input_output_aliases={f: f for f in range(n)},
        scratch_shapes=[sems, sems],
        compiler_params=pltpu.CompilerParams(has_side_effects=True),
    )(*grads)


def all_sum_small(v):
    R = v.shape[0]

    def body(v_ref, o_ref, land, send, recv):
        x, y, c, _, _ = _place()
        me = 4 * x + 2 * y + c
        land[me] = v_ref[...]
        peers = [(px, py, pc) for px in range(2) for py in range(2) for pc in range(2)]
        cps = []
        for k in range(1, 8):
            dev = (x ^ (k >> 2), y ^ ((k >> 1) & 1), c ^ (k & 1))
            cp = _remote(v_ref, land.at[me], send.at[k - 1], recv.at[k - 1], dev)
            cp.start()
            cps.append(cp)
        for k in range(1, 8):
            src = 4 * (x ^ (k >> 2)) + 2 * (y ^ ((k >> 1) & 1)) + (c ^ (k & 1))
            _remote(v_ref, land.at[src], send.at[k - 1], recv.at[k - 1], (x, y, c)).wait_recv()
        for cp in cps:
            cp.wait_send()
        tot = land[0]
        for d in range(1, len(peers)):
            tot = tot + land[d]
        o_ref[...] = tot

    sems = pltpu.SemaphoreType.DMA((7,))
    vm = pl.BlockSpec(memory_space=pltpu.VMEM)
    return pl.pallas_call(
        body, name="all_sum_small", in_specs=[vm], out_specs=vm,
        out_shape=jax.ShapeDtypeStruct(v.shape, F32),
        scratch_shapes=[pltpu.VMEM((8, R, 128), F32), sems, sems],
        compiler_params=pltpu.CompilerParams(has_side_effects=True),
    )(v)


def _row_tile(R):
    for t in (512, 256, 128, 64, 32, 16):
        if R % t == 0:
            return t
    return R


def pair_add(cidx, grad, recv):
    hl, P, R, C = recv.shape
    tr = _row_tile(R)

    def body(c_ref, a_ref, b_ref, o_ref):
        o_ref[...] = (a_ref[...].astype(F32) + b_ref[...].astype(F32)).astype(o_ref.dtype)

    blk = (None, None, tr, C)
    return pl.pallas_call(
        body, name="pair_add",
        grid_spec=pltpu.PrefetchScalarGridSpec(
            num_scalar_prefetch=1, grid=(hl, P, R // tr),
            in_specs=[pl.BlockSpec(blk, lambda l, p, r, c: (c[0] * hl + l, p, r, 0)),
                      pl.BlockSpec(blk, lambda l, p, r, c: (l, p, r, 0))],
            out_specs=pl.BlockSpec(blk, lambda l, p, r, c: (l, p, r, 0))),
        out_shape=jax.ShapeDtypeStruct(recv.shape, recv.dtype),
        compiler_params=_params(("parallel", "parallel", "parallel")),
    )(cidx, grad, recv)


def chip_sum(cidx, land, L):
    hl, P, R, C = land.shape
    tr = _row_tile(R)

    def body(c_ref, a_ref, o_ref):
        tot = a_ref[0].astype(F32)
        for q in range(1, P):
            tot = tot + a_ref[q].astype(F32)
        o_ref[...] = tot

    return pl.pallas_call(
        body, name="chip_sum",
        grid_spec=pltpu.PrefetchScalarGridSpec(
            num_scalar_prefetch=1, grid=(hl, R // tr),
            in_specs=[pl.BlockSpec((None, P, tr, C), lambda l, r, c: (l, 0, r, 0))],
            out_specs=pl.BlockSpec((None, tr, C), lambda l, r, c: (c[0] * hl + l, r, 0))),
        out_shape=jax.ShapeDtypeStruct((L, R, C), F32),
        compiler_params=_params(("parallel", "parallel")),
    )(cidx, land)


def adamw(w, g, m, v):
    L, R, C = w.shape
    tr = _row_tile(R)
    c1 = 1.0 / (1.0 - ADAM_B1 ** ADAM_STEP)
    c2 = 1.0 / (1.0 - ADAM_B2 ** ADAM_STEP)

    def body(w_ref, g_ref, m_ref, v_ref, d_ref, nm_ref, nv_ref):
        gv = g_ref[...]
        nm = ADAM_B1 * m_ref[...] + (1.0 - ADAM_B1) * gv
        nv = ADAM_B2 * v_ref[...] + (1.0 - ADAM_B2) * (gv * gv)
        nm_ref[...] = nm
        nv_ref[...] = nv
        d_ref[...] = -ADAM_LR * ((nm * c1) / (jnp.sqrt(nv * c2) + ADAM_EPS) + ADAM_WD * w_ref[...])

    blk = pl.BlockSpec((None, tr, C), lambda l, r: (l, r, 0))
    osh = jax.ShapeDtypeStruct((L, R, C), F32)
    return pl.pallas_call(
        body, name="adamw", grid=(L, R // tr),
        in_specs=[blk, blk, blk, blk], out_specs=[blk, blk, blk], out_shape=[osh, osh, osh],
        compiler_params=_params(("parallel", "parallel")),
    )(w, g, m, v)


def kernel(x, w_qkv_a, w_o_a, rel_bias, w_qkv_b, w_o_b, ffn_w_gate, ffn_w_up, ffn_w_down, ln_g, ln_b, loss_target, m_w_qkv_a, m_w_o_a, m_rel_bias, m_w_qkv_b, m_w_o_b, m_ffn_w_gate, m_ffn_w_up, m_ffn_w_down, m_ln_g, m_ln_b, v_w_qkv_a, v_w_o_a, v_rel_bias, v_w_qkv_b, v_w_o_b, v_ffn_w_gate, v_ffn_w_up, v_ffn_w_down, v_ln_g, v_ln_b):
    B, S, D = x.shape
    M = B * S
    depth = ffn_w_gate.shape[0]
    n_ffn = 2 * depth
    H = D // HEAD_DIM
    HP = H // 2
    Fs = ffn_w_gate.shape[-1]
    alpha = (2.0 * depth) ** 0.25
    assert S % QB_A == 0 and S % SB_TILE == 0 and rel_bias.shape == (N_REL, H)

    def ffn3(a):
        return a.reshape((n_ffn,) + a.shape[2:])

    shards = [w_qkv_a.astype(BF16), w_o_a.astype(BF16), w_qkv_b.astype(BF16), w_o_b.astype(BF16),
              ffn3(ffn_w_gate).astype(BF16), ffn3(ffn_w_up).astype(BF16), ffn3(ffn_w_down).astype(BF16),
              ln_g, ln_b]
    wqa, woa, wqb, wob, wg, wu, wd, lng_p, lnb_p = gather_weights(shards)
    lng = jnp.moveaxis(lng_p, 0, 2).reshape(depth, 3, 1, D)
    lnb = jnp.moveaxis(lnb_p, 0, 2).reshape(depth, 3, 1, D)

    tab_t = jnp.pad(rel_bias.T, ((0, 0), (0, REL_PAD - N_REL)))
    vr = bias_vec(tab_t).reshape(HP, 2, VR_W)

    xf = x.reshape(M, D)
    xb = xf.astype(BF16)
    saved = []
    for i in range(depth):
        for j in range(3):
            gam, bet = lng[i, j], lnb[i, j]
            if j != 1:
                l = 2 * i + (0 if j == 0 else 1)
                g, u, h = ffn_up(xb, wg, wu, l)
                xo, xob, xhat, rstd = mm_ln(h, wd, l, xf, gam, bet, alpha, 0.5, True)
                saved.append(("ffn", l, xb, g, u, h, xhat, rstd, gam))
            elif i % 2 == 0:
                l = i // 2
                qkv = qkv_proj(xb, wqa, l)
                o = attn_a_fwd(qkv, vr, B, S)
                xo, xob, xhat, rstd = mm_ln(o, woa, l, xf, gam, bet, alpha, 1.0, False)
                saved.append(("a", l, xb, qkv, o, None, xhat, rstd, gam))
            else:
                l = i // 2
                qkv = qkv_proj(xb, wqb, l)
                o, ntot = attn_b_fwd(qkv, B, S)
                xo, xob, xhat, rstd = mm_ln(o, wob, l, xf, gam, bet, alpha, 1.0, False)
                saved.append(("b", l, xb, qkv, o, ntot, xhat, rstd, gam))
            xf, xb = xo, xob

    dy, loss_part = loss_head(xf, loss_target.reshape(M, D))
    loss = lax.psum(loss_part[0, 0], ("x", "y", "c"))

    la, lb = w_qkv_a.shape[0], w_qkv_b.shape[0]
    Cq, Ro = w_qkv_a.shape[-1], w_o_a.shape[1]
    gqa = jnp.zeros((la, N_CHIPS, D, Cq), BF16)
    goa = jnp.zeros((la, N_CHIPS, Ro, D), BF16)
    gqb = jnp.zeros((lb, N_CHIPS, D, Cq), BF16)
    gob = jnp.zeros((lb, N_CHIPS, Ro, D), BF16)
    ggate = jnp.zeros((n_ffn, N_CHIPS, D, Fs), BF16)
    gup = jnp.zeros((n_ffn, N_CHIPS, D, Fs), BF16)
    gdown = jnp.zeros((n_ffn, N_CHIPS, Fs, D), BF16)
    dgam, dbet, dvrs = [], [], []
    for rec in reversed(saved):
        kind, l, xb_in, t1, t2, t3, xhat, rstd, gam = rec
        scale = 0.5 if kind == "ffn" else 1.0
        dpre, dyb, dg_, db_ = ln_bwd(dy, xhat, rstd, gam, scale)
        dgam.append(dg_)
        dbet.append(db_)
        if kind == "ffn":
            g, u, h = t1, t2, t3
            dg, du = ffn_bwd_act(dyb, wd, l, g, u)
            gdown = dw_ffn_down(gdown, l, h, dyb)
            ggate = dw_ffn_up(ggate, l, xb_in, dg)
            gup = dw_ffn_up(gup, l, xb_in, du)
            dy = ffn_dx(dg, du, wg, wu, l, dpre, alpha)
        else:
            qkv, o = t1, t2
            wo, wq = (woa, wqa) if kind == "a" else (wob, wqb)
            do = o_proj_bwd(dyb, wo, l)
            if kind == "a":
                goa = dw_o(goa, l, o, dyb)
                dq, dk, dv, dvr = attn_a_bwd(qkv, vr, do, B, S)
                dvrs.append(dvr.reshape(B, H, VR_W))
            else:
                gob = dw_o(gob, l, o, dyb)
                dq, dk, dv = attn_b_bwd(qkv, do, t3, B, S)
            dqkv = jnp.concatenate([dq, dk, dv], axis=1)
            if kind == "a":
                gqa = dw_qkv(gqa, l, xb_in, dqkv)
            else:
                gqb = dw_qkv(gqb, l, xb_in, dqkv)
            dy = qkv_proj_bwd(dqkv, wq, l, dpre, alpha)
    grad_x = dy.reshape(B, S, D)

    def ln_family(parts):
        full = jnp.concatenate(parts[::-1], axis=0).reshape(depth, 3, N_CHIPS, D // N_CHIPS)
        return jnp.moveaxis(full, 2, 1)

    glng, glnb = ln_family(dgam), ln_family(dbet)

    cidx = lax.axis_index("c").astype(jnp.int32).reshape(1)
    fams = [gqa, goa, gqb, gob, ggate, gup, gdown, glng, glnb]
    from_sib = pair_exchange(fams)
    parts = [pair_add(cidx, g_, r_) for g_, r_ in zip(fams, from_sib)]
    lands = chip_exchange(parts)
    sums = [chip_sum(cidx, ld, g_.shape[0]) for ld, g_ in zip(lands, fams)]
    g_qa, g_oa, g_qb, g_ob, g_gate, g_up, g_down, g_lng, g_lnb = half_swap(sums)

    d_tab_t = bias_vec_bwd(jnp.concatenate(dvrs, axis=0))
    rows = -(-(H * REL_PAD) // (8 * 128)) * 8
    flat = jnp.pad(d_tab_t.reshape(-1), (0, rows * 128 - H * REL_PAD)).reshape(rows, 128)
    tot = all_sum_small(flat).reshape(-1)[:H * REL_PAD].reshape(H, REL_PAD)
    g_rel = tot[:, :N_REL].T

    def upd(w, g, m, v):
        shp = w.shape
        w3, m3, v3 = (a.reshape(g.shape) for a in (w, m, v))
        d, nm, nv = adamw(w3, g, m3, v3)
        return g.reshape(shp), d.reshape(shp), nm.reshape(shp), nv.reshape(shp)

    res = [
        upd(w_qkv_a, g_qa, m_w_qkv_a, v_w_qkv_a),
        upd(w_o_a, g_oa, m_w_o_a, v_w_o_a),
        upd(rel_bias, g_rel.reshape(1, N_REL, H), m_rel_bias, v_rel_bias),
        upd(w_qkv_b, g_qb, m_w_qkv_b, v_w_qkv_b),
        upd(w_o_b, g_ob, m_w_o_b, v_w_o_b),
        upd(ffn_w_gate, g_gate, m_ffn_w_gate, v_ffn_w_gate),
        upd(ffn_w_up, g_up, m_ffn_w_up, v_ffn_w_up),
        upd(ffn_w_down, g_down, m_ffn_w_down, v_ffn_w_down),
        upd(ln_g, g_lng, m_ln_g, v_ln_g),
        upd(ln_b, g_lnb, m_ln_b, v_ln_b),
    ]
    grads = [r[0] for r in res]
    deltas = [r[1] for r in res]
    new_m = [r[2] for r in res]
    new_v = [r[3] for r in res]
    return (loss, grad_x, *grads, *deltas, *new_m, *new_v)
```

```python
import functools
import math

import jax
import jax.numpy as jnp
from jax import lax
from jax.experimental import pallas as pl
from jax.experimental.pallas import tpu as pltpu

F32 = jnp.float32
BF16 = jnp.bfloat16
MESH = pl.DeviceIdType.MESH

N_CHIPS = 4
HEAD_DIM = 64
CHUNK = 64
LEFT_CHUNKS = 8
LOOKBACK = LEFT_CHUNKS * CHUNK
REL_CLIP = 128
N_REL = 2 * REL_CLIP + 1
REL_PAD = 384
SB_TILE = 256
QB_A = 256
KW_A = QB_A + LOOKBACK
VR_W = 1024
VR_C0 = KW_A - 1
LN_EPS = 1e-5
ADAM_LR, ADAM_B1, ADAM_B2, ADAM_EPS, ADAM_WD, ADAM_STEP = 0.001, 0.9, 0.999, 1e-08, 0.01, 10
NEG = -1e30
VMEM_LIMIT = 56 * 1024 * 1024

NT_DIMS = (((1,), (1,)), ((), ()))
TN_DIMS = (((0,), (0,)), ((), ()))
ANY = pl.BlockSpec(memory_space=pl.ANY)


def _params(sem=None):
    return pltpu.CompilerParams(dimension_semantics=sem, vmem_limit_bytes=VMEM_LIMIT)


def _tile(n, pref):
    t = min(n, pref)
    assert n % t == 0, (n, pref)
    return t


def _sigmoid(z):
    return 1.0 / (1.0 + jnp.exp(-z))


def _mm_call(name, operands, in_specs, out_shape, out_spec, grid, dims_list, acc_shape,
             add_coef=None, aliases=None):
    n_pairs = len(dims_list)
    nk = grid[-1]
    has_add = add_coef is not None
    n_alias = len(aliases) if aliases else 0

    def body(*refs):
        pair_refs = refs[:2 * n_pairs]
        pos = 2 * n_pairs
        add_ref = refs[pos] if has_add else None
        pos += (1 if has_add else 0) + n_alias
        o_ref = refs[pos]
        acc_ref = refs[pos + 1] if nk > 1 else None

        def product():
            part = None
            for i, dims in enumerate(dims_list):
                d = lax.dot_general(pair_refs[2 * i][...], pair_refs[2 * i + 1][...], dims,
                                    preferred_element_type=F32)
                part = d if part is None else part + d
            return part

        def finish(r):
            if has_add:
                r = r + add_coef * add_ref[...]
            o_ref[...] = r.astype(o_ref.dtype)

        if nk == 1:
            finish(product())
        else:
            k = pl.program_id(len(grid) - 1)

            @pl.when(k == 0)
            def _():
                acc_ref[...] = jnp.zeros_like(acc_ref)

            acc_ref[...] += product()

            @pl.when(k == nk - 1)
            def _():
                finish(acc_ref[...])

    sem = ("parallel",) * (len(grid) - 1) + ("arbitrary",)
    return pl.pallas_call(
        body, name=name, grid=grid, in_specs=in_specs, out_specs=out_spec, out_shape=out_shape,
        scratch_shapes=[pltpu.VMEM(acc_shape, F32)] if nk > 1 else [],
        input_output_aliases=aliases or {},
        compiler_params=_params(sem),
    )(*operands)


def qkv_proj(xb, w, l):
    M, D = xb.shape
    C = w.shape[-1]
    tm = _tile(M, 512)
    return _mm_call(
        "qkv_proj", (xb, w),
        [pl.BlockSpec((tm, D), lambda p, i, k: (i, 0)),
         pl.BlockSpec((None, None, D, C), lambda p, i, k: (p, l, 0, 0))],
        jax.ShapeDtypeStruct((M, N_CHIPS * C), BF16),
        pl.BlockSpec((tm, C), lambda p, i, k: (i, p)),
        (N_CHIPS, M // tm, 1), [(((1,), (0,)), ((), ()))], None)


def o_proj_bwd(dyb, w, l):
    M, D = dyb.shape
    R = w.shape[2]
    tm = _tile(M, 512)
    return _mm_call(
        "o_proj_bwd", (dyb, w),
        [pl.BlockSpec((tm, D), lambda p, i, k: (i, 0)),
         pl.BlockSpec((None, None, R, D), lambda p, i, k: (p, l, 0, 0))],
        jax.ShapeDtypeStruct((M, N_CHIPS * R), BF16),
        pl.BlockSpec((tm, R), lambda p, i, k: (i, p)),
        (N_CHIPS, M // tm, 1), [NT_DIMS], None)


def qkv_proj_bwd(dqkv, w, l, dpre, alpha):
    M = dqkv.shape[0]
    D, C = w.shape[2], w.shape[3]
    tm = _tile(M, 512)

    def body(a_ref, w_ref, add_ref, o_ref):
        acc = alpha * add_ref[...]
        for p in range(N_CHIPS):
            acc = acc + lax.dot_general(a_ref[:, p * C:(p + 1) * C], w_ref[p], NT_DIMS,
                                        preferred_element_type=F32)
        o_ref[...] = acc

    row = pl.BlockSpec((tm, D), lambda i: (i, 0))
    return pl.pallas_call(
        body, name="qkv_proj_bwd", grid=(M // tm,),
        in_specs=[pl.BlockSpec((tm, N_CHIPS * C), lambda i: (i, 0)),
                  pl.BlockSpec((N_CHIPS, None, D, C), lambda i: (0, l, 0, 0)), row],
        out_specs=row, out_shape=jax.ShapeDtypeStruct((M, D), F32),
        compiler_params=_params(("parallel",)),
    )(dqkv, w, dpre)


def ffn_dx(dg, du, wg, wu, l, dpre, alpha):
    _, M, Fs = dg.shape
    D = wg.shape[2]
    tm = _tile(M, 256)

    def body(dg_ref, wg_ref, du_ref, wu_ref, add_ref, o_ref):
        acc = alpha * add_ref[...]
        for p in range(N_CHIPS):
            acc = acc + lax.dot_general(dg_ref[p], wg_ref[p], NT_DIMS, preferred_element_type=F32)
            acc = acc + lax.dot_general(du_ref[p], wu_ref[p], NT_DIMS, preferred_element_type=F32)
        o_ref[...] = acc

    act = pl.BlockSpec((N_CHIPS, tm, Fs), lambda i: (0, i, 0))
    wsp = pl.BlockSpec((N_CHIPS, None, D, Fs), lambda i: (0, l, 0, 0))
    row = pl.BlockSpec((tm, D), lambda i: (i, 0))
    return pl.pallas_call(
        body, name="ffn_dx", grid=(M // tm,),
        in_specs=[act, wsp, act, wsp, row],
        out_specs=row, out_shape=jax.ShapeDtypeStruct((M, D), F32),
        compiler_params=_params(("parallel",)),
    )(dg, wg, du, wu, dpre)


def _dw_call(name, buf, l, a, b, a_spec, b_spec, M, tk):
    _, _, R, C = buf.shape
    return _mm_call(
        name, (a, b, buf),
        [a_spec, b_spec, ANY],
        jax.ShapeDtypeStruct(buf.shape, buf.dtype),
        pl.BlockSpec((None, None, R, C), lambda p, k: (l, p, 0, 0)),
        (N_CHIPS, M // tk), [TN_DIMS], (R, C), aliases={2: 0})


def ffn_bwd(gdown, ggate, gup, l, xb, dyb, wd, g, u, h):
    M, D = dyb.shape
    Fs = wd.shape[2]
    tm = _tile(M, 512)
    n = M // tm

    def body(x_ref, dy_ref, wd_ref, g_ref, u_ref, h_ref, _gd, _gg, _gu,
             dg_ref, du_ref, gd_ref, gg_ref, gu_ref, acc_d, acc_g, acc_u):
        i = pl.program_id(1)

        @pl.when(i == 0)
        def _():
            acc_d[...] = jnp.zeros_like(acc_d)
            acc_g[...] = jnp.zeros_like(acc_g)
            acc_u[...] = jnp.zeros_like(acc_u)

        dy = dy_ref[...]
        dh = lax.dot_general(dy, wd_ref[...], NT_DIMS, preferred_element_type=F32)
        gf = g_ref[...].astype(F32)
        sig = _sigmoid(gf)
        silu = gf * sig
        dg = (dh * u_ref[...].astype(F32) * (sig * (1.0 + gf - silu))).astype(BF16)
        du = (dh * silu).astype(BF16)
        dg_ref[...] = dg
        du_ref[...] = du
        x = x_ref[...]
        acc_d[...] += lax.dot_general(h_ref[...], dy, TN_DIMS, preferred_element_type=F32)
        acc_g[...] += lax.dot_general(x, dg, TN_DIMS, preferred_element_type=F32)
        acc_u[...] += lax.dot_general(x, du, TN_DIMS, preferred_element_type=F32)

        @pl.when(i == n - 1)
        def _():
            gd_ref[...] = acc_d[...].astype(gd_ref.dtype)
            gg_ref[...] = acc_g[...].astype(gg_ref.dtype)
            gu_ref[...] = acc_u[...].astype(gu_ref.dtype)

    row = pl.BlockSpec((tm, D), lambda p, i: (i, 0))
    act = pl.BlockSpec((None, tm, Fs), lambda p, i: (p, i, 0))
    ash = jax.ShapeDtypeStruct((N_CHIPS, M, Fs), BF16)
    down_blk = pl.BlockSpec((None, None, Fs, D), lambda p, i: (l, p, 0, 0))
    up_blk = pl.BlockSpec((None, None, D, Fs), lambda p, i: (l, p, 0, 0))
    return pl.pallas_call(
        body, name="ffn_bwd", grid=(N_CHIPS, n),
        in_specs=[row, row, pl.BlockSpec((None, None, Fs, D), lambda p, i: (p, l, 0, 0)),
                  act, act, act, ANY, ANY, ANY],
        out_specs=[act, act, down_blk, up_blk, up_blk],
        out_shape=[ash, ash] + [jax.ShapeDtypeStruct(b.shape, b.dtype) for b in (gdown, ggate, gup)],
        scratch_shapes=[pltpu.VMEM((Fs, D), F32), pltpu.VMEM((D, Fs), F32), pltpu.VMEM((D, Fs), F32)],
        input_output_aliases={6: 2, 7: 3, 8: 4},
        compiler_params=_params(("parallel", "arbitrary")),
    )(xb, dyb, wd, g, u, h, gdown, ggate, gup)


def dw_qkv(buf, l, xb, dqkv):
    M, D = xb.shape
    C = buf.shape[-1]
    tk = _tile(M, 1024)
    return _dw_call("dw_qkv", buf, l, xb, dqkv,
                    pl.BlockSpec((tk, D), lambda p, k: (k, 0)),
                    pl.BlockSpec((tk, C), lambda p, k: (k, p)), M, tk)


def dw_o(buf, l, o, dyb):
    M, D = dyb.shape
    R = buf.shape[2]
    tk = _tile(M, 1024)
    return _dw_call("dw_o", buf, l, o, dyb,
                    pl.BlockSpec((tk, R), lambda p, k: (k, p)),
                    pl.BlockSpec((tk, D), lambda p, k: (k, 0)), M, tk)


def ffn_up(xb, wg, wu, l):
    M, D = xb.shape
    Fs = wg.shape[-1]
    tm = _tile(M, 512)

    def body(x_ref, wg_ref, wu_ref, g_ref, u_ref, h_ref):
        x = x_ref[...]
        g = jnp.dot(x, wg_ref[...], preferred_element_type=F32)
        u = jnp.dot(x, wu_ref[...], preferred_element_type=F32)
        g_ref[...] = g.astype(BF16)
        u_ref[...] = u.astype(BF16)
        h_ref[...] = (g * _sigmoid(g) * u).astype(BF16)

    wsp = pl.BlockSpec((None, None, D, Fs), lambda p, i: (p, l, 0, 0))
    osp = pl.BlockSpec((None, tm, Fs), lambda p, i: (p, i, 0))
    osh = jax.ShapeDtypeStruct((N_CHIPS, M, Fs), BF16)
    return pl.pallas_call(
        body, name="ffn_up", grid=(N_CHIPS, M // tm),
        in_specs=[pl.BlockSpec((tm, D), lambda p, i: (i, 0)), wsp, wsp],
        out_specs=[osp, osp, osp], out_shape=[osh, osh, osh],
        compiler_params=_params(("parallel", "parallel")),
    )(xb, wg, wu)


def mm_ln(a, w, l, x, gam, bet, alpha, scale, a_piece_major):
    M, D = x.shape
    R = w.shape[2]
    tm = _tile(M, 256)
    if a_piece_major:
        a_spec = pl.BlockSpec((N_CHIPS, tm, R), lambda i: (0, i, 0))
    else:
        a_spec = pl.BlockSpec((tm, N_CHIPS * R), lambda i: (i, 0))

    def body(a_ref, w_ref, x_ref, g_ref, b_ref, xo_ref, xb_ref, xh_ref, rs_ref):
        y = None
        for p in range(N_CHIPS):
            a = a_ref[p] if a_piece_major else a_ref[:, p * R:(p + 1) * R]
            d = jnp.dot(a, w_ref[p], preferred_element_type=F32)
            y = d if y is None else y + d
        pre = alpha * x_ref[...] + scale * y
        mu = jnp.mean(pre, axis=-1, keepdims=True)
        cen = pre - mu
        var = jnp.mean(cen * cen, axis=-1, keepdims=True)
        rstd = lax.rsqrt(var + LN_EPS)
        xhat = cen * rstd
        out = xhat * g_ref[...] + b_ref[...]
        xo_ref[...] = out
        xb_ref[...] = out.astype(BF16)
        xh_ref[...] = xhat
        rs_ref[...] = rstd

    row = pl.BlockSpec((tm, D), lambda i: (i, 0))
    vec = pl.BlockSpec((1, D), lambda i: (0, 0))
    return pl.pallas_call(
        body, name="mm_ln", grid=(M // tm,),
        in_specs=[a_spec, pl.BlockSpec((N_CHIPS, None, R, D), lambda i: (0, l, 0, 0)), row, vec, vec],
        out_specs=[row, row, row, pl.BlockSpec((tm, 1), lambda i: (i, 0))],
        out_shape=[jax.ShapeDtypeStruct((M, D), F32), jax.ShapeDtypeStruct((M, D), BF16),
                   jax.ShapeDtypeStruct((M, D), F32), jax.ShapeDtypeStruct((M, 1), F32)],
        compiler_params=_params(("parallel",)),
    )(a, w, x, gam, bet)


def ln_bwd(dy, xhat, rstd, gam, scale):
    M, D = dy.shape
    tm = _tile(M, 512)

    def body(dy_ref, xh_ref, rs_ref, g_ref, dp_ref, db16_ref, dg_ref, dbt_ref):
        i = pl.program_id(0)
        dy_v = dy_ref[...]
        xh = xh_ref[...]
        dxh = dy_v * g_ref[...]
        m1 = jnp.mean(dxh, axis=-1, keepdims=True)
        m2 = jnp.mean(dxh * xh, axis=-1, keepdims=True)
        dpre = rs_ref[...] * (dxh - m1 - xh * m2)
        dp_ref[...] = dpre
        db16_ref[...] = (scale * dpre).astype(BF16)
        dgp = jnp.sum(dy_v * xh, axis=0, keepdims=True)
        dbp = jnp.sum(dy_v, axis=0, keepdims=True)

        @pl.when(i == 0)
        def _():
            dg_ref[...] = dgp
            dbt_ref[...] = dbp

        @pl.when(i > 0)
        def _():
            dg_ref[...] += dgp
            dbt_ref[...] += dbp

    row = pl.BlockSpec((tm, D), lambda i: (i, 0))
    vec = pl.BlockSpec((1, D), lambda i: (0, 0))
    return pl.pallas_call(
        body, name="ln_bwd", grid=(M // tm,),
        in_specs=[row, row, pl.BlockSpec((tm, 1), lambda i: (i, 0)), vec],
        out_specs=[row, row, vec, vec],
        out_shape=[jax.ShapeDtypeStruct((M, D), F32), jax.ShapeDtypeStruct((M, D), BF16),
                   jax.ShapeDtypeStruct((1, D), F32), jax.ShapeDtypeStruct((1, D), F32)],
        compiler_params=_params(("arbitrary",)),
    )(dy, xhat, rstd, gam)


def loss_head(y, tgt):
    M, D = y.shape
    tm = _tile(M, 512)
    n = M // tm

    def body(y_ref, t_ref, dy_ref, l_ref, acc_ref):
        i = pl.program_id(0)
        e = y_ref[...] - t_ref[...]
        dy_ref[...] = e * (1.0 / D)
        part = jnp.sum(e * e, axis=0, keepdims=True)

        @pl.when(i == 0)
        def _():
            acc_ref[...] = part

        @pl.when(i > 0)
        def _():
            acc_ref[...] += part

        @pl.when(i == n - 1)
        def _():
            l_ref[...] = (0.5 / D) * jnp.sum(acc_ref[...], axis=1, keepdims=True)

    row = pl.BlockSpec((tm, D), lambda i: (i, 0))
    return pl.pallas_call(
        body, name="loss_head", grid=(n,),
        in_specs=[row, row],
        out_specs=[row, pl.BlockSpec((1, 1), lambda i: (0, 0))],
        out_shape=[jax.ShapeDtypeStruct((M, D), F32), jax.ShapeDtypeStruct((1, 1), F32)],
        scratch_shapes=[pltpu.VMEM((1, D), F32)],
        compiler_params=_params(("arbitrary",)),
    )(y, tgt)


def _rel_onehot_t():
    r = lax.broadcasted_iota(jnp.int32, (REL_PAD, VR_W), 0)
    n = lax.broadcasted_iota(jnp.int32, (REL_PAD, VR_W), 1)
    idx = jnp.clip(VR_C0 - n, -REL_CLIP, REL_CLIP) + REL_CLIP
    return (r == idx).astype(F32)


def bias_vec(tab_t):
    H = tab_t.shape[0]

    def body(t_ref, o_ref):
        o_ref[...] = jnp.dot(t_ref[...], _rel_onehot_t(), precision=lax.Precision.HIGHEST,
                             preferred_element_type=F32)

    return pl.pallas_call(
        body, name="bias_vec", out_shape=jax.ShapeDtypeStruct((H, VR_W), F32),
        compiler_params=_params(),
    )(tab_t)


def bias_vec_bwd(dvr):
    n, H, _ = dvr.shape

    def body(d_ref, o_ref):
        tot = d_ref[0]
        for i in range(1, n):
            tot = tot + d_ref[i]
        o_ref[...] = lax.dot_general(tot, _rel_onehot_t(), NT_DIMS, precision=lax.Precision.HIGHEST,
                                     preferred_element_type=F32)

    return pl.pallas_call(
        body, name="bias_vec_bwd", out_shape=jax.ShapeDtypeStruct((H, REL_PAD), F32),
        compiler_params=_params(),
    )(dvr)


def _a_bias_mask(vr_row):
    xb = jnp.broadcast_to(vr_row, (QB_A, VR_W))
    tile = pltpu.roll(xb, VR_W - (QB_A - 1), 1, stride=1, stride_axis=0)[:, :KW_A]
    qc = lax.broadcasted_iota(jnp.int32, (QB_A, KW_A), 0) // CHUNK
    kc = lax.broadcasted_iota(jnp.int32, (QB_A, KW_A), 1) // CHUNK
    valid = (kc >= qc) & (kc <= qc + LEFT_CHUNKS)
    return jnp.where(valid, tile, NEG)


def _a_diag_sums(db_acc, h):
    acc8 = None
    for a in range(QB_A // 8):
        grp = db_acc[h, 8 * a:8 * a + 8, :]
        shift = QB_A - 8 - 8 * a
        if shift:
            grp = pltpu.roll(grp, shift, 1)
        acc8 = grp if acc8 is None else acc8 + grp
    sub = lax.broadcasted_iota(jnp.int32, (8, VR_W), 0)
    tot = jnp.zeros((8, VR_W), F32)
    for b in range(8):
        moved = pltpu.roll(acc8, 7 - b, 1) if b < 7 else acc8
        tot = tot + jnp.where(sub == b, moved, 0.0)
    return jnp.sum(tot, axis=0, keepdims=True)


def _a_blocks(S):
    out = []
    for qi in range(S // QB_A):
        q0 = qi * QB_A
        ks = max(0, q0 - LOOKBACK)
        out.append((q0, ks, q0 + QB_A, ks - (q0 - LOOKBACK)))
    return out


def _head_specs(S, HP):
    q = pl.BlockSpec((S, 2 * HEAD_DIM), lambda b, hp: (b, hp))
    k = pl.BlockSpec((S, 2 * HEAD_DIM), lambda b, hp: (b, HP + hp))
    v = pl.BlockSpec((S, 2 * HEAD_DIM), lambda b, hp: (b, 2 * HP + hp))
    return q, k, v


def attn_a_fwd(qkv, vr, B, S):
    D = qkv.shape[1] // 3
    HP = D // (2 * HEAD_DIM)
    scale = HEAD_DIM ** -0.5
    blocks = _a_blocks(S)

    def body(q_ref, k_ref, v_ref, vr_ref, o_ref):
        for h in range(2):
            lo = h * HEAD_DIM
            bm = _a_bias_mask(vr_ref[h:h + 1, :])
            for (q0, ks, ke, joff) in blocks:
                q = q_ref[q0:q0 + QB_A, lo:lo + HEAD_DIM]
                k = k_ref[ks:ke, lo:lo + HEAD_DIM]
                v = v_ref[ks:ke, lo:lo + HEAD_DIM]
                s = lax.dot_general(q, k, NT_DIMS, preferred_element_type=F32) * scale + bm[:, joff:]
                m = jnp.max(s, axis=-1, keepdims=True)
                p = jnp.exp(s - m)
                den = jnp.sum(p, axis=-1, keepdims=True)
                o = jnp.dot(p.astype(BF16), v, preferred_element_type=F32) / den
                o_ref[q0:q0 + QB_A, lo:lo + HEAD_DIM] = o.astype(BF16)

    qs, ks_, vs = _head_specs(S, HP)
    return pl.pallas_call(
        body, name="attn_a_fwd", grid=(B, HP),
        in_specs=[qs, ks_, vs, pl.BlockSpec((None, 2, VR_W), lambda b, hp: (hp, 0, 0))],
        out_specs=pl.BlockSpec((S, 2 * HEAD_DIM), lambda b, hp: (b, hp)),
        out_shape=jax.ShapeDtypeStruct((B * S, D), BF16),
        compiler_params=_params(("parallel", "parallel")),
    )(qkv, qkv, qkv, vr)


def attn_a_bwd(qkv, vr, do, B, S):
    D = qkv.shape[1] // 3
    HP = D // (2 * HEAD_DIM)
    scale = HEAD_DIM ** -0.5
    blocks = _a_blocks(S)

    def body(q_ref, k_ref, v_ref, vr_ref, do_ref, dq_ref, dk_ref, dv_ref, dvr_ref,
             dk_acc, dv_acc, db_acc):
        dk_acc[...] = jnp.zeros_like(dk_acc)
        dv_acc[...] = jnp.zeros_like(dv_acc)
        db_acc[...] = jnp.zeros_like(db_acc)
        for h in range(2):
            lo = h * HEAD_DIM
            bm = _a_bias_mask(vr_ref[h:h + 1, :])
            for (q0, ks, ke, joff) in blocks:
                q = q_ref[q0:q0 + QB_A, lo:lo + HEAD_DIM]
                k = k_ref[ks:ke, lo:lo + HEAD_DIM]
                v = v_ref[ks:ke, lo:lo + HEAD_DIM]
                dob = do_ref[q0:q0 + QB_A, lo:lo + HEAD_DIM]
                s = lax.dot_general(q, k, NT_DIMS, preferred_element_type=F32) * scale + bm[:, joff:]
                m = jnp.max(s, axis=-1, keepdims=True)
                e = jnp.exp(s - m)
                p = e / jnp.sum(e, axis=-1, keepdims=True)
                dp = lax.dot_general(dob, v, NT_DIMS, preferred_element_type=F32)
                ds = p * (dp - jnp.sum(p * dp, axis=-1, keepdims=True))
                dsb = ds.astype(BF16)
                dq = jnp.dot(dsb, k, preferred_element_type=F32) * scale
                dq_ref[q0:q0 + QB_A, lo:lo + HEAD_DIM] = dq.astype(BF16)
                dk_acc[ks:ke, lo:lo + HEAD_DIM] += lax.dot_general(
                    dsb, q, TN_DIMS, preferred_element_type=F32) * scale
                dv_acc[ks:ke, lo:lo + HEAD_DIM] += lax.dot_general(
                    p.astype(BF16), dob, TN_DIMS, preferred_element_type=F32)
                db_acc[h, :, joff:KW_A] += ds
            dvr_ref[h:h + 1, :] = _a_diag_sums(db_acc, h)
        dk_ref[...] = dk_acc[...].astype(BF16)
        dv_ref[...] = dv_acc[...].astype(BF16)

    qs, ks_, vs = _head_specs(S, HP)
    hd = pl.BlockSpec((S, 2 * HEAD_DIM), lambda b, hp: (b, hp))
    osh = jax.ShapeDtypeStruct((B * S, D), BF16)
    return pl.pallas_call(
        body, name="attn_a_bwd", grid=(B, HP),
        in_specs=[qs, ks_, vs, pl.BlockSpec((None, 2, VR_W), lambda b, hp: (hp, 0, 0)), hd],
        out_specs=[hd, hd, hd, pl.BlockSpec((None, None, 2, VR_W), lambda b, hp: (b, hp, 0, 0))],
        out_shape=[osh, osh, osh, jax.ShapeDtypeStruct((B, HP, 2, VR_W), F32)],
        scratch_shapes=[pltpu.VMEM((S, 2 * HEAD_DIM), F32), pltpu.VMEM((S, 2 * HEAD_DIM), F32),
                        pltpu.VMEM((2, QB_A, VR_W), F32)],
        compiler_params=_params(("parallel", "parallel")),
    )(qkv, qkv, qkv, vr, do)


def _tri(cmp):
    j = lax.broadcasted_iota(jnp.int32, (SB_TILE, SB_TILE), 0)
    s = lax.broadcasted_iota(jnp.int32, (SB_TILE, SB_TILE), 1)
    return cmp(j, s).astype(BF16)


def _cumsum_mm(x, tri):
    hi = x.astype(BF16)
    mid = (x - hi.astype(F32)).astype(BF16)
    return jnp.dot(hi, tri, preferred_element_type=F32) + jnp.dot(mid, tri, preferred_element_type=F32)


def _sb_logs(q, k, scale, diagonal):
    z = lax.dot_general(q, k, NT_DIMS, preferred_element_type=F32) * scale
    log_b = jnp.minimum(z, 0.0) - jnp.log(1.0 + jnp.exp(-jnp.abs(z)))
    log_1mb = log_b - z
    if not diagonal:
        return log_b, log_1mb, None
    row = lax.broadcasted_iota(jnp.int32, (SB_TILE, SB_TILE), 0)
    col = lax.broadcasted_iota(jnp.int32, (SB_TILE, SB_TILE), 1)
    causal = col < row
    return log_b, jnp.where(causal, log_1mb, 0.0), causal


def attn_b_fwd(qkv, B, S):
    D = qkv.shape[1] // 3
    HP = D // (2 * HEAD_DIM)
    scale = HEAD_DIM ** -0.5
    nb = S // SB_TILE

    def body(q_ref, k_ref, v_ref, o_ref, nt_ref):
        tri = _tri(lambda j, s: j > s)
        for h in range(2):
            lo = h * HEAD_DIM

            def q_loop(qb, carry):
                q0 = pl.multiple_of(qb * SB_TILE, SB_TILE)
                q = q_ref[pl.ds(q0, SB_TILE), lo:lo + HEAD_DIM]

                def block(k0, diagonal, right, acc):
                    k = k_ref[pl.ds(k0, SB_TILE), lo:lo + HEAD_DIM]
                    v = v_ref[pl.ds(k0, SB_TILE), lo:lo + HEAD_DIM]
                    log_b, log_1mb, causal = _sb_logs(q, k, scale, diagonal)
                    a = jnp.exp(log_b + _cumsum_mm(log_1mb, tri) + right)
                    if diagonal:
                        a = jnp.where(causal, a, 0.0)
                    acc = acc + jnp.dot(a.astype(BF16), v, preferred_element_type=F32)
                    return right + jnp.sum(log_1mb, axis=-1, keepdims=True), acc

                def k_loop(t, kc):
                    k0 = pl.multiple_of((qb - 1 - t) * SB_TILE, SB_TILE)
                    return block(k0, False, *kc)

                kc = block(q0, True, jnp.zeros((SB_TILE, 1), F32), jnp.zeros((SB_TILE, HEAD_DIM), F32))
                right, acc = lax.fori_loop(0, qb, k_loop, kc)
                o_ref[pl.ds(q0, SB_TILE), lo:lo + HEAD_DIM] = acc.astype(BF16)
                nt_ref[pl.ds(q0, SB_TILE), lo:lo + HEAD_DIM] = jnp.broadcast_to(right, (SB_TILE, HEAD_DIM))
                return carry

            lax.fori_loop(0, nb, q_loop, 0)

    qs, ks_, vs = _head_specs(S, HP)
    hd = pl.BlockSpec((S, 2 * HEAD_DIM), lambda b, hp: (b, hp))
    return pl.pallas_call(
        body, name="attn_b_fwd", grid=(B, HP),
        in_specs=[qs, ks_, vs], out_specs=[hd, hd],
        out_shape=[jax.ShapeDtypeStruct((B * S, D), BF16), jax.ShapeDtypeStruct((B * S, D), F32)],
        compiler_params=_params(("parallel", "parallel")),
    )(qkv, qkv, qkv)


def attn_b_bwd(qkv, do, ntot, B, S):
    D = qkv.shape[1] // 3
    HP = D // (2 * HEAD_DIM)
    scale = HEAD_DIM ** -0.5
    nb = S // SB_TILE

    def body(q_ref, k_ref, v_ref, do_ref, nt_ref, dq_ref, dk_ref, dv_ref, dkt_acc, dvt_acc):
        tri_incl = _tri(lambda j, s: j <= s)
        tri_excl = _tri(lambda j, s: j < s)
        dkt_acc[...] = jnp.zeros_like(dkt_acc)
        dvt_acc[...] = jnp.zeros_like(dvt_acc)

        def q_loop(qb, carry):
            q0 = pl.multiple_of(qb * SB_TILE, SB_TILE)
            rows = pl.ds(q0, SB_TILE)
            qt_pair = q_ref[rows, :].astype(F32).T.astype(BF16)
            dot_pair = do_ref[rows, :].astype(F32).T.astype(BF16)
            for h in range(2):
                lo = h * HEAD_DIM
                q = q_ref[rows, lo:lo + HEAD_DIM]
                dob = do_ref[rows, lo:lo + HEAD_DIM]
                qt = qt_pair[lo:lo + HEAD_DIM, :]
                dot_ = dot_pair[lo:lo + HEAD_DIM, :]
                nt = nt_ref[rows, lo:lo + 1]

                def block(kb, diagonal, left_n, left_d, dq_acc):
                    k0 = pl.multiple_of(kb * SB_TILE, SB_TILE)
                    k = k_ref[pl.ds(k0, SB_TILE), lo:lo + HEAD_DIM]
                    v = v_ref[pl.ds(k0, SB_TILE), lo:lo + HEAD_DIM]
                    log_b, log_1mb, causal = _sb_logs(q, k, scale, diagonal)
                    a = jnp.exp(log_b + (nt - left_n) - _cumsum_mm(log_1mb, tri_incl))
                    if diagonal:
                        a = jnp.where(causal, a, 0.0)
                    dl = a * lax.dot_general(dob, v, NT_DIMS, preferred_element_type=F32)
                    dn = left_d + _cumsum_mm(dl, tri_excl)
                    dz = dl * jnp.exp(log_1mb) - dn * jnp.exp(log_b)
                    if diagonal:
                        dz = jnp.where(causal, dz, 0.0)
                    dzb = dz.astype(BF16)
                    dq_acc = dq_acc + jnp.dot(dzb, k, preferred_element_type=F32)
                    dkt_acc[kb, lo:lo + HEAD_DIM, :] += jnp.dot(qt, dzb, preferred_element_type=F32)
                    dvt_acc[kb, lo:lo + HEAD_DIM, :] += jnp.dot(dot_, a.astype(BF16),
                                                                preferred_element_type=F32)
                    return (left_n + jnp.sum(log_1mb, axis=-1, keepdims=True),
                            left_d + jnp.sum(dl, axis=-1, keepdims=True), dq_acc)

                zero1 = jnp.zeros((SB_TILE, 1), F32)
                kc = lax.fori_loop(0, qb, lambda kb, kc: block(kb, False, *kc),
                                   (zero1, zero1, jnp.zeros((SB_TILE, HEAD_DIM), F32)))
                _, _, dq_acc = block(qb, True, *kc)
                dq_ref[rows, lo:lo + HEAD_DIM] = (dq_acc * scale).astype(BF16)
            return carry

        lax.fori_loop(0, nb, q_loop, 0)
        for kb in range(nb):
            dk_ref[kb * SB_TILE:(kb + 1) * SB_TILE, :] = (dkt_acc[kb].T * scale).astype(BF16)
            dv_ref[kb * SB_TILE:(kb + 1) * SB_TILE, :] = dvt_acc[kb].T.astype(BF16)

    qs, ks_, vs = _head_specs(S, HP)
    hd = pl.BlockSpec((S, 2 * HEAD_DIM), lambda b, hp: (b, hp))
    osh = jax.ShapeDtypeStruct((B * S, D), BF16)
    acc = pltpu.VMEM((nb, 2 * HEAD_DIM, SB_TILE), F32)
    return pl.pallas_call(
        body, name="attn_b_bwd", grid=(B, HP),
        in_specs=[qs, ks_, vs, hd, hd], out_specs=[hd, hd, hd], out_shape=[osh, osh, osh],
        scratch_shapes=[acc, acc],
        compiler_params=_params(("parallel", "parallel")),
    )(qkv, qkv, qkv, do, ntot)


def _place():
    x, y, c = lax.axis_index("x"), lax.axis_index("y"), lax.axis_index("c")
    chips = [(1 - x, y), (x, 1 - y), (1 - x, 1 - y)]
    return x, y, c, 2 * x + y, chips


def _remote(src, dst, send_sem, recv_sem, dev):
    return pltpu.make_async_remote_copy(src_ref=src, dst_ref=dst, send_sem=send_sem, recv_sem=recv_sem,
                                        device_id=dev, device_id_type=MESH)


def gather_weights(shards):
    n = len(shards)

    def body(*refs):
        ins, outs = refs[:n], refs[n:2 * n]
        send1, recv1, send2, recv2, lsem = refs[2 * n:]
        x, y, c, me, chips = _place()
        local, first = [], []
        for f in range(n):
            hl = shards[f].shape[0] // 2
            cp = pltpu.make_async_copy(ins[f], outs[f].at[me], lsem.at[f])
            cp.start()
            local.append(cp)
            for j, (qx, qy) in enumerate(chips):
                half = pl.ds(c * hl, hl)
                cp = _remote(ins[f].at[half], outs[f].at[me, half],
                             send1.at[3 * f + j], recv1.at[3 * f + j], (qx, qy, c))
                cp.start()
                first.append(cp)
        passed = []
        for f in range(n):
            hl = shards[f].shape[0] // 2
            for j, (qx, qy) in enumerate(chips):
                slab = outs[f].at[2 * qx + qy, pl.ds(c * hl, hl)]
                _remote(slab, slab, send1.at[3 * f + j], recv1.at[3 * f + j], (x, y, c)).wait_recv()
                cp = _remote(slab, slab, send2.at[3 * f + j], recv2.at[3 * f + j], (x, y, 1 - c))
                cp.start()
                passed.append(cp)
        for f in range(n):
            hl = shards[f].shape[0] // 2
            for j, (qx, qy) in enumerate(chips):
                slab = outs[f].at[2 * qx + qy, pl.ds((1 - c) * hl, hl)]
                _remote(slab, slab, send2.at[3 * f + j], recv2.at[3 * f + j], (x, y, c)).wait_recv()
        for cp in first + passed:
            cp.wait_send()
        for cp in local:
            cp.wait()

    sems = pltpu.SemaphoreType.DMA((3 * n,))
    return pl.pallas_call(
        body, name="gather_weights",
        in_specs=[ANY] * n, out_specs=[ANY] * n,
        out_shape=[jax.ShapeDtypeStruct((N_CHIPS,) + s.shape, s.dtype) for s in shards],
        scratch_shapes=[sems, sems, sems, sems, pltpu.SemaphoreType.DMA((n,))],
        compiler_params=pltpu.CompilerParams(has_side_effects=True),
    )(*shards)


def pair_exchange(grads):
    n = len(grads)

    def body(*refs):
        ins, outs = refs[:n], refs[n:2 * n]
        send, recv = refs[2 * n:]
        x, y, c, _, _ = _place()
        cps = []
        for f in range(n):
            hl = grads[f].shape[0] // 2
            cp = _remote(ins[f].at[pl.ds((1 - c) * hl, hl)], outs[f], send.at[f], recv.at[f], (x, y, 1 - c))
            cp.start()
            cps.append(cp)
        for cp in cps:
            cp.wait()

    sems = pltpu.SemaphoreType.DMA((n,))
    return pl.pallas_call(
        body, name="pair_exchange",
        in_specs=[ANY] * n, out_specs=[ANY] * n,
        out_shape=[jax.ShapeDtypeStruct((g.shape[0] // 2,) + g.shape[1:], g.dtype) for g in grads],
        scratch_shapes=[sems, sems],
        compiler_params=pltpu.CompilerParams(has_side_effects=True),
    )(*grads)


def chip_exchange(parts):
    n = len(parts)

    def body(*refs):
        ins, outs = refs[:n], refs[n:2 * n]
        send, recv, lsem = refs[2 * n:]
        x, y, c, me, chips = _place()
        local, sent = [], []
        for f in range(n):
            hl = parts[f].shape[0]
            rows = pl.ds(0, hl)
            cp = pltpu.make_async_copy(ins[f].at[rows, me], outs[f].at[rows, me], lsem.at[f])
            cp.start()
            local.append(cp)
            for j, (qx, qy) in enumerate(chips):
                cp = _remote(ins[f].at[rows, 2 * qx + qy], outs[f].at[rows, me],
                             send.at[3 * f + j], recv.at[3 * f + j], (qx, qy, c))
                cp.start()
                sent.append(cp)
        for f in range(n):
            rows = pl.ds(0, parts[f].shape[0])
            for j, (qx, qy) in enumerate(chips):
                slab = outs[f].at[rows, 2 * qx + qy]
                _remote(slab, slab, send.at[3 * f + j], recv.at[3 * f + j], (x, y, c)).wait_recv()
        for cp in sent:
            cp.wait_send()
        for cp in local:
            cp.wait()

    sems = pltpu.SemaphoreType.DMA((3 * n,))
    return pl.pallas_call(
        body, name="chip_exchange",
        in_specs=[ANY] * n, out_specs=[ANY] * n,
        out_shape=[jax.ShapeDtypeStruct(s.shape, s.dtype) for s in parts],
        scratch_shapes=[sems, sems, pltpu.SemaphoreType.DMA((n,))],
        compiler_params=pltpu.CompilerParams(has_side_effects=True),
    )(*parts)


def half_swap(grads):
    n = len(grads)

    def body(*refs):
        ins, outs = refs[:n], refs[n:2 * n]
        send, recv = refs[2 * n:]
        x, y, c, _, _ = _place()
        cps = []
        for f in range(n):
            hl = grads[f].shape[0] // 2
            mine = pl.ds(c * hl, hl)
            cp = _remote(outs[f].at[mine], outs[f].at[mine], send.at[f], recv.at[f], (x, y, 1 - c))
            cp.start()
            cps.append(cp)
        for f in range(n):
            hl = grads[f].shape[0] // 2
            theirs = outs[f].at[pl.ds((1 - c) * hl, hl)]
            _remote(theirs, theirs, send.at[f], recv.at[f], (x, y, c)).wait_recv()
        for cp in cps:
            cp.wait_send()

    sems = pltpu.SemaphoreType.DMA((n,))
    return pl.pallas_call(
        body, name="half_swap",
        in_specs=[ANY] * n, out_specs=[ANY] * n,
        out_shape=[jax.ShapeDtypeStruct(g.shape, g.dtype) for g in grads],
        input_output_aliases={f: f for f in range(n)},
        scratch_shapes=[sems, sems],
        compiler_params=pltpu.CompilerParams(has_side_effects=True),
    )(*grads)


def all_sum_small(v):
    R = v.shape[0]

    def body(v_ref, o_ref, land, send, recv):
        x, y, c, _, _ = _place()
        me = 4 * x + 2 * y + c
        land[me] = v_ref[...]
        peers = [(px, py, pc) for px in range(2) for py in range(2) for pc in range(2)]
        cps = []
        for k in range(1, 8):
            dev = (x ^ (k >> 2), y ^ ((k >> 1) & 1), c ^ (k & 1))
            cp = _remote(v_ref, land.at[me], send.at[k - 1], recv.at[k - 1], dev)
            cp.start()
            cps.append(cp)
        for k in range(1, 8):
            src = 4 * (x ^ (k >> 2)) + 2 * (y ^ ((k >> 1) & 1)) + (c ^ (k & 1))
            _remote(v_ref, land.at[src], send.at[k - 1], recv.at[k - 1], (x, y, c)).wait_recv()
        for cp in cps:
            cp.wait_send()
        tot = land[0]
        for d in range(1, len(peers)):
            tot = tot + land[d]
        o_ref[...] = tot

    sems = pltpu.SemaphoreType.DMA((7,))
    vm = pl.BlockSpec(memory_space=pltpu.VMEM)
    return pl.pallas_call(
        body, name="all_sum_small", in_specs=[vm], out_specs=vm,
        out_shape=jax.ShapeDtypeStruct(v.shape, F32),
        scratch_shapes=[pltpu.VMEM((8, R, 128), F32), sems, sems],
        compiler_params=pltpu.CompilerParams(has_side_effects=True),
    )(v)


def _row_tile(R):
    for t in (512, 256, 128, 64, 32, 16):
        if R % t == 0:
            return t
    return R


def pair_add(cidx, grad, recv):
    hl, P, R, C = recv.shape
    tr = _row_tile(R)

    def body(c_ref, a_ref, b_ref, o_ref):
        o_ref[...] = (a_ref[...].astype(F32) + b_ref[...].astype(F32)).astype(o_ref.dtype)

    blk = (None, None, tr, C)
    return pl.pallas_call(
        body, name="pair_add",
        grid_spec=pltpu.PrefetchScalarGridSpec(
            num_scalar_prefetch=1, grid=(hl, P, R // tr),
            in_specs=[pl.BlockSpec(blk, lambda l, p, r, c: (c[0] * hl + l, p, r, 0)),
                      pl.BlockSpec(blk, lambda l, p, r, c: (l, p, r, 0))],
            out_specs=pl.BlockSpec(blk, lambda l, p, r, c: (l, p, r, 0))),
        out_shape=jax.ShapeDtypeStruct(recv.shape, recv.dtype),
        compiler_params=_params(("parallel", "parallel", "parallel")),
    )(cidx, grad, recv)


def chip_sum(cidx, land, L):
    hl, P, R, C = land.shape
    tr = _row_tile(R)

    def body(c_ref, a_ref, o_ref):
        tot = a_ref[0].astype(F32)
        for q in range(1, P):
            tot = tot + a_ref[q].astype(F32)
        o_ref[...] = tot

    return pl.pallas_call(
        body, name="chip_sum",
        grid_spec=pltpu.PrefetchScalarGridSpec(
            num_scalar_prefetch=1, grid=(hl, R // tr),
            in_specs=[pl.BlockSpec((None, P, tr, C), lambda l, r, c: (l, 0, r, 0))],
            out_specs=pl.BlockSpec((None, tr, C), lambda l, r, c: (c[0] * hl + l, r, 0))),
        out_shape=jax.ShapeDtypeStruct((L, R, C), F32),
        compiler_params=_params(("parallel", "parallel")),
    )(cidx, land)


def adamw(w, g, m, v):
    L, R, C = w.shape
    tr = _row_tile(R)
    c1 = 1.0 / (1.0 - ADAM_B1 ** ADAM_STEP)
    c2 = 1.0 / (1.0 - ADAM_B2 ** ADAM_STEP)

    def body(w_ref, g_ref, m_ref, v_ref, d_ref, nm_ref, nv_ref):
        gv = g_ref[...]
        nm = ADAM_B1 * m_ref[...] + (1.0 - ADAM_B1) * gv
        nv = ADAM_B2 * v_ref[...] + (1.0 - ADAM_B2) * (gv * gv)
        nm_ref[...] = nm
        nv_ref[...] = nv
        d_ref[...] = -ADAM_LR * ((nm * c1) / (jnp.sqrt(nv * c2) + ADAM_EPS) + ADAM_WD * w_ref[...])

    blk = pl.BlockSpec((None, tr, C), lambda l, r: (l, r, 0))
    osh = jax.ShapeDtypeStruct((L, R, C), F32)
    return pl.pallas_call(
        body, name="adamw", grid=(L, R // tr),
        in_specs=[blk, blk, blk, blk], out_specs=[blk, blk, blk], out_shape=[osh, osh, osh],
        compiler_params=_params(("parallel", "parallel")),
    )(w, g, m, v)


def kernel(x, w_qkv_a, w_o_a, rel_bias, w_qkv_b, w_o_b, ffn_w_gate, ffn_w_up, ffn_w_down, ln_g, ln_b, loss_target, m_w_qkv_a, m_w_o_a, m_rel_bias, m_w_qkv_b, m_w_o_b, m_ffn_w_gate, m_ffn_w_up, m_ffn_w_down, m_ln_g, m_ln_b, v_w_qkv_a, v_w_o_a, v_rel_bias, v_w_qkv_b, v_w_o_b, v_ffn_w_gate, v_ffn_w_up, v_ffn_w_down, v_ln_g, v_ln_b):
    B, S, D = x.shape
    M = B * S
    depth = ffn_w_gate.shape[0]
    n_ffn = 2 * depth
    H = D // HEAD_DIM
    HP = H // 2
    Fs = ffn_w_gate.shape[-1]
    alpha = (2.0 * depth) ** 0.25
    assert S % QB_A == 0 and S % SB_TILE == 0 and rel_bias.shape == (N_REL, H)

    def ffn3(a):
        return a.reshape((n_ffn,) + a.shape[2:])

    shards = [w_qkv_a.astype(BF16), w_o_a.astype(BF16), w_qkv_b.astype(BF16), w_o_b.astype(BF16),
              ffn3(ffn_w_gate).astype(BF16), ffn3(ffn_w_up).astype(BF16), ffn3(ffn_w_down).astype(BF16),
              ln_g, ln_b]
    wqa, woa, wqb, wob, wg, wu, wd, lng_p, lnb_p = gather_weights(shards)
    lng = jnp.moveaxis(lng_p, 0, 2).reshape(depth, 3, 1, D)
    lnb = jnp.moveaxis(lnb_p, 0, 2).reshape(depth, 3, 1, D)

    tab_t = jnp.pad(rel_bias.T, ((0, 0), (0, REL_PAD - N_REL)))
    vr = bias_vec(tab_t).reshape(HP, 2, VR_W)

    xf = x.reshape(M, D)
    xb = xf.astype(BF16)
    saved = []
    for i in range(depth):
        for j in range(3):
            gam, bet = lng[i, j], lnb[i, j]
            if j != 1:
                l = 2 * i + (0 if j == 0 else 1)
                g, u, h = ffn_up(xb, wg, wu, l)
                xo, xob, xhat, rstd = mm_ln(h, wd, l, xf, gam, bet, alpha, 0.5, True)
                saved.append(("ffn", l, xb, g, u, h, xhat, rstd, gam))
            elif i % 2 == 0:
                l = i // 2
                qkv = qkv_proj(xb, wqa, l)
                o = attn_a_fwd(qkv, vr, B, S)
                xo, xob, xhat, rstd = mm_ln(o, woa, l, xf, gam, bet, alpha, 1.0, False)
                saved.append(("a", l, xb, qkv, o, None, xhat, rstd, gam))
            else:
                l = i // 2
                qkv = qkv_proj(xb, wqb, l)
                o, ntot = attn_b_fwd(qkv, B, S)
                xo, xob, xhat, rstd = mm_ln(o, wob, l, xf, gam, bet, alpha, 1.0, False)
                saved.append(("b", l, xb, qkv, o, ntot, xhat, rstd, gam))
            xf, xb = xo, xob

    dy, loss_part = loss_head(xf, loss_target.reshape(M, D))
    loss = lax.psum(loss_part[0, 0], ("x", "y", "c"))

    la, lb = w_qkv_a.shape[0], w_qkv_b.shape[0]
    Cq, Ro = w_qkv_a.shape[-1], w_o_a.shape[1]
    gqa = jnp.zeros((la, N_CHIPS, D, Cq), BF16)
    goa = jnp.zeros((la, N_CHIPS, Ro, D), BF16)
    gqb = jnp.zeros((lb, N_CHIPS, D, Cq), BF16)
    gob = jnp.zeros((lb, N_CHIPS, Ro, D), BF16)
    ggate = jnp.zeros((n_ffn, N_CHIPS, D, Fs), BF16)
    gup = jnp.zeros((n_ffn, N_CHIPS, D, Fs), BF16)
    gdown = jnp.zeros((n_ffn, N_CHIPS, Fs, D), BF16)
    dgam, dbet, dvrs = [], [], []
    for rec in reversed(saved):
        kind, l, xb_in, t1, t2, t3, xhat, rstd, gam = rec
        scale = 0.5 if kind == "ffn" else 1.0
        dpre, dyb, dg_, db_ = ln_bwd(dy, xhat, rstd, gam, scale)
        dgam.append(dg_)
        dbet.append(db_)
        if kind == "ffn":
            g, u, h = t1, t2, t3
            dg, du, gdown, ggate, gup = ffn_bwd(gdown, ggate, gup, l, xb_in, dyb, wd, g, u, h)
            dy = ffn_dx(dg, du, wg, wu, l, dpre, alpha)
        else:
            qkv, o = t1, t2
            wo, wq = (woa, wqa) if kind == "a" else (wob, wqb)
            do = o_proj_bwd(dyb, wo, l)
            if kind == "a":
                goa = dw_o(goa, l, o, dyb)
                dq, dk, dv, dvr = attn_a_bwd(qkv, vr, do, B, S)
                dvrs.append(dvr.reshape(B, H, VR_W))
            else:
                gob = dw_o(gob, l, o, dyb)
                dq, dk, dv = attn_b_bwd(qkv, do, t3, B, S)
            dqkv = jnp.concatenate([dq, dk, dv], axis=1)
            if kind == "a":
                gqa = dw_qkv(gqa, l, xb_in, dqkv)
            else:
                gqb = dw_qkv(gqb, l, xb_in, dqkv)
            dy = qkv_proj_bwd(dqkv, wq, l, dpre, alpha)
    grad_x = dy.reshape(B, S, D)

    def ln_family(parts):
        full = jnp.concatenate(parts[::-1], axis=0).reshape(depth, 3, N_CHIPS, D // N_CHIPS)
        return jnp.moveaxis(full, 2, 1)

    glng, glnb = ln_family(dgam), ln_family(dbet)

    cidx = lax.axis_index("c").astype(jnp.int32).reshape(1)
    fams = [gqa, goa, gqb, gob, ggate, gup, gdown, glng, glnb]
    from_sib = pair_exchange(fams)
    parts = [pair_add(cidx, g_, r_) for g_, r_ in zip(fams, from_sib)]
    lands = chip_exchange(parts)
    sums = [chip_sum(cidx, ld, g_.shape[0]) for ld, g_ in zip(lands, fams)]
    g_qa, g_oa, g_qb, g_ob, g_gate, g_up, g_down, g_lng, g_lnb = half_swap(sums)

    d_tab_t = bias_vec_bwd(jnp.concatenate(dvrs, axis=0))
    rows = -(-(H * REL_PAD) // (8 * 128)) * 8
    flat = jnp.pad(d_tab_t.reshape(-1), (0, rows * 128 - H * REL_PAD)).reshape(rows, 128)
    tot = all_sum_small(flat).reshape(-1)[:H * REL_PAD].reshape(H, REL_PAD)
    g_rel = tot[:, :N_REL].T

    def upd(w, g, m, v):
        shp = w.shape
        w3, m3, v3 = (a.reshape(g.shape) for a in (w, m, v))
        d, nm, nv = adamw(w3, g, m3, v3)
        return g.reshape(shp), d.reshape(shp), nm.reshape(shp), nv.reshape(shp)

    res = [
        upd(w_qkv_a, g_qa, m_w_qkv_a, v_w_qkv_a),
        upd(w_o_a, g_oa, m_w_o_a, v_w_o_a),
        upd(rel_bias, g_rel.reshape(1, N_REL, H), m_rel_bias, v_rel_bias),
        upd(w_qkv_b, g_qb, m_w_qkv_b, v_w_qkv_b),
        upd(w_o_b, g_ob, m_w_o_b, v_w_o_b),
        upd(ffn_w_gate, g_gate, m_ffn_w_gate, v_ffn_w_gate),
        upd(ffn_w_up, g_up, m_ffn_w_up, v_ffn_w_up),
        upd(ffn_w_down, g_down, m_ffn_w_down, v_ffn_w_down),
        upd(ln_g, g_lng, m_ln_g, v_ln_g),
        upd(ln_b, g_lnb, m_ln_b, v_ln_b),
    ]
    grads = [r[0] for r in res]
    deltas = [r[1] for r in res]
    new_m = [r[2] for r in res]
    new_v = [r[3] for r in res]
    return (loss, grad_x, *grads, *deltas, *new_m, *new_v)
```

```python
import functools
import math

import jax
import jax.numpy as jnp
from jax import lax
from jax.experimental import pallas as pl
from jax.experimental.pallas import tpu as pltpu

F32 = jnp.float32
BF16 = jnp.bfloat16
MESH = pl.DeviceIdType.MESH

N_CHIPS = 4
HEAD_DIM = 64
CHUNK = 64
LEFT_CHUNKS = 8
LOOKBACK = LEFT_CHUNKS * CHUNK
REL_CLIP = 128
N_REL = 2 * REL_CLIP + 1
REL_PAD = 384
SB_TILE = 256
QB_A = 256
KW_A = QB_A + LOOKBACK
VR_W = 1024
VR_C0 = KW_A - 1
LN_EPS = 1e-5
ADAM_LR, ADAM_B1, ADAM_B2, ADAM_EPS, ADAM_WD, ADAM_STEP = 0.001, 0.9, 0.999, 1e-08, 0.01, 10
NEG = -1e30
VMEM_LIMIT = 56 * 1024 * 1024

NT_DIMS = (((1,), (1,)), ((), ()))
TN_DIMS = (((0,), (0,)), ((), ()))
ANY = pl.BlockSpec(memory_space=pl.ANY)


def _params(sem=None):
    return pltpu.CompilerParams(dimension_semantics=sem, vmem_limit_bytes=VMEM_LIMIT)


def _tile(n, pref):
    t = min(n, pref)
    assert n % t == 0, (n, pref)
    return t


def _sigmoid(z):
    return 1.0 / (1.0 + jnp.exp(-z))


def _mm_call(name, operands, in_specs, out_shape, out_spec, grid, dims_list, acc_shape,
             add_coef=None, aliases=None):
    n_pairs = len(dims_list)
    nk = grid[-1]
    has_add = add_coef is not None
    n_alias = len(aliases) if aliases else 0

    def body(*refs):
        pair_refs = refs[:2 * n_pairs]
        pos = 2 * n_pairs
        add_ref = refs[pos] if has_add else None
        pos += (1 if has_add else 0) + n_alias
        o_ref = refs[pos]
        acc_ref = refs[pos + 1] if nk > 1 else None

        def product():
            part = None
            for i, dims in enumerate(dims_list):
                d = lax.dot_general(pair_refs[2 * i][...], pair_refs[2 * i + 1][...], dims,
                                    preferred_element_type=F32)
                part = d if part is None else part + d
            return part

        def finish(r):
            if has_add:
                r = r + add_coef * add_ref[...]
            o_ref[...] = r.astype(o_ref.dtype)

        if nk == 1:
            finish(product())
        else:
            k = pl.program_id(len(grid) - 1)

            @pl.when(k == 0)
            def _():
                acc_ref[...] = jnp.zeros_like(acc_ref)

            acc_ref[...] += product()

            @pl.when(k == nk - 1)
            def _():
                finish(acc_ref[...])

    sem = ("parallel",) * (len(grid) - 1) + ("arbitrary",)
    return pl.pallas_call(
        body, name=name, grid=grid, in_specs=in_specs, out_specs=out_spec, out_shape=out_shape,
        scratch_shapes=[pltpu.VMEM(acc_shape, F32)] if nk > 1 else [],
        input_output_aliases=aliases or {},
        compiler_params=_params(sem),
    )(*operands)


def qkv_proj(xb, w, l):
    M, D = xb.shape
    C = w.shape[-1]
    tm = _tile(M, 512)
    return _mm_call(
        "qkv_proj", (xb, w),
        [pl.BlockSpec((tm, D), lambda p, i, k: (i, 0)),
         pl.BlockSpec((None, None, D, C), lambda p, i, k: (p, l, 0, 0))],
        jax.ShapeDtypeStruct((M, N_CHIPS * C), BF16),
        pl.BlockSpec((tm, C), lambda p, i, k: (i, p)),
        (N_CHIPS, M // tm, 1), [(((1,), (0,)), ((), ()))], None)


def o_proj_bwd(dyb, w, l):
    M, D = dyb.shape
    R = w.shape[2]
    tm = _tile(M, 512)
    return _mm_call(
        "o_proj_bwd", (dyb, w),
        [pl.BlockSpec((tm, D), lambda p, i, k: (i, 0)),
         pl.BlockSpec((None, None, R, D), lambda p, i, k: (p, l, 0, 0))],
        jax.ShapeDtypeStruct((M, N_CHIPS * R), BF16),
        pl.BlockSpec((tm, R), lambda p, i, k: (i, p)),
        (N_CHIPS, M // tm, 1), [NT_DIMS], None)


def qkv_proj_bwd(dqkv, w, l, dpre, alpha):
    M = dqkv.shape[0]
    D, C = w.shape[2], w.shape[3]
    tm = _tile(M, 512)

    def body(a_ref, w_ref, add_ref, o_ref):
        acc = alpha * add_ref[...]
        for p in range(N_CHIPS):
            acc = acc + lax.dot_general(a_ref[:, p * C:(p + 1) * C], w_ref[p], NT_DIMS,
                                        preferred_element_type=F32)
        o_ref[...] = acc

    row = pl.BlockSpec((tm, D), lambda i: (i, 0))
    return pl.pallas_call(
        body, name="qkv_proj_bwd", grid=(M // tm,),
        in_specs=[pl.BlockSpec((tm, N_CHIPS * C), lambda i: (i, 0)),
                  pl.BlockSpec((N_CHIPS, None, D, C), lambda i: (0, l, 0, 0)), row],
        out_specs=row, out_shape=jax.ShapeDtypeStruct((M, D), F32),
        compiler_params=_params(("parallel",)),
    )(dqkv, w, dpre)


def ffn_dx(dg, du, wg, wu, l, dpre, alpha):
    _, M, Fs = dg.shape
    D = wg.shape[2]
    tm = _tile(M, 256)

    def body(dg_ref, wg_ref, du_ref, wu_ref, add_ref, o_ref):
        acc = alpha * add_ref[...]
        for p in range(N_CHIPS):
            acc = acc + lax.dot_general(dg_ref[p], wg_ref[p], NT_DIMS, preferred_element_type=F32)
            acc = acc + lax.dot_general(du_ref[p], wu_ref[p], NT_DIMS, preferred_element_type=F32)
        o_ref[...] = acc

    act = pl.BlockSpec((N_CHIPS, tm, Fs), lambda i: (0, i, 0))
    wsp = pl.BlockSpec((N_CHIPS, None, D, Fs), lambda i: (0, l, 0, 0))
    row = pl.BlockSpec((tm, D), lambda i: (i, 0))
    return pl.pallas_call(
        body, name="ffn_dx", grid=(M // tm,),
        in_specs=[act, wsp, act, wsp, row],
        out_specs=row, out_shape=jax.ShapeDtypeStruct((M, D), F32),
        compiler_params=_params(("parallel",)),
    )(dg, wg, du, wu, dpre)


def _dw_call(name, buf, l, a, b, a_spec, b_spec, M, tk):
    _, _, R, C = buf.shape
    return _mm_call(
        name, (a, b, buf),
        [a_spec, b_spec, ANY],
        jax.ShapeDtypeStruct(buf.shape, buf.dtype),
        pl.BlockSpec((None, None, R, C), lambda p, k: (l, p, 0, 0)),
        (N_CHIPS, M // tk), [TN_DIMS], (R, C), aliases={2: 0})


def ffn_bwd(gdown, ggate, gup, l, xb, dyb, wd, g, u, h):
    M, D = dyb.shape
    Fs = wd.shape[2]
    tm = _tile(M, 512)
    n = M // tm

    def body(x_ref, dy_ref, wd_ref, g_ref, u_ref, h_ref, _gd, _gg, _gu,
             dg_ref, du_ref, gd_ref, gg_ref, gu_ref, acc_d, acc_g, acc_u):
        i = pl.program_id(1)

        @pl.when(i == 0)
        def _():
            acc_d[...] = jnp.zeros_like(acc_d)
            acc_g[...] = jnp.zeros_like(acc_g)
            acc_u[...] = jnp.zeros_like(acc_u)

        dy = dy_ref[...]
        dh = lax.dot_general(dy, wd_ref[...], NT_DIMS, preferred_element_type=F32)
        gf = g_ref[...].astype(F32)
        sig = _sigmoid(gf)
        silu = gf * sig
        dg = (dh * u_ref[...].astype(F32) * (sig * (1.0 + gf - silu))).astype(BF16)
        du = (dh * silu).astype(BF16)
        dg_ref[...] = dg
        du_ref[...] = du
        x = x_ref[...]
        acc_d[...] += lax.dot_general(h_ref[...], dy, TN_DIMS, preferred_element_type=F32)
        acc_g[...] += lax.dot_general(x, dg, TN_DIMS, preferred_element_type=F32)
        acc_u[...] += lax.dot_general(x, du, TN_DIMS, preferred_element_type=F32)

        @pl.when(i == n - 1)
        def _():
            gd_ref[...] = acc_d[...].astype(gd_ref.dtype)
            gg_ref[...] = acc_g[...].astype(gg_ref.dtype)
            gu_ref[...] = acc_u[...].astype(gu_ref.dtype)

    row = pl.BlockSpec((tm, D), lambda p, i: (i, 0))
    act = pl.BlockSpec((None, tm, Fs), lambda p, i: (p, i, 0))
    ash = jax.ShapeDtypeStruct((N_CHIPS, M, Fs), BF16)
    down_blk = pl.BlockSpec((None, None, Fs, D), lambda p, i: (l, p, 0, 0))
    up_blk = pl.BlockSpec((None, None, D, Fs), lambda p, i: (l, p, 0, 0))
    return pl.pallas_call(
        body, name="ffn_bwd", grid=(N_CHIPS, n),
        in_specs=[row, row, pl.BlockSpec((None, None, Fs, D), lambda p, i: (p, l, 0, 0)),
                  act, act, act, ANY, ANY, ANY],
        out_specs=[act, act, down_blk, up_blk, up_blk],
        out_shape=[ash, ash] + [jax.ShapeDtypeStruct(b.shape, b.dtype) for b in (gdown, ggate, gup)],
        scratch_shapes=[pltpu.VMEM((Fs, D), F32), pltpu.VMEM((D, Fs), F32), pltpu.VMEM((D, Fs), F32)],
        input_output_aliases={6: 2, 7: 3, 8: 4},
        compiler_params=_params(("parallel", "arbitrary")),
    )(xb, dyb, wd, g, u, h, gdown, ggate, gup)


def dw_qkv(buf, l, xb, dqkv):
    M, D = xb.shape
    C = buf.shape[-1]
    tk = _tile(M, 1024)
    return _dw_call("dw_qkv", buf, l, xb, dqkv,
                    pl.BlockSpec((tk, D), lambda p, k: (k, 0)),
                    pl.BlockSpec((tk, C), lambda p, k: (k, p)), M, tk)


def dw_o(buf, l, o, dyb):
    M, D = dyb.shape
    R = buf.shape[2]
    tk = _tile(M, 1024)
    return _dw_call("dw_o", buf, l, o, dyb,
                    pl.BlockSpec((tk, R), lambda p, k: (k, p)),
                    pl.BlockSpec((tk, D), lambda p, k: (k, 0)), M, tk)


def ffn_up(xb, wg, wu, l):
    M, D = xb.shape
    Fs = wg.shape[-1]
    tm = _tile(M, 512)

    def body(x_ref, wg_ref, wu_ref, g_ref, u_ref, h_ref):
        x = x_ref[...]
        g = jnp.dot(x, wg_ref[...], preferred_element_type=F32)
        u = jnp.dot(x, wu_ref[...], preferred_element_type=F32)
        g_ref[...] = g.astype(BF16)
        u_ref[...] = u.astype(BF16)
        h_ref[...] = (g * _sigmoid(g) * u).astype(BF16)

    wsp = pl.BlockSpec((None, None, D, Fs), lambda p, i: (p, l, 0, 0))
    osp = pl.BlockSpec((None, tm, Fs), lambda p, i: (p, i, 0))
    osh = jax.ShapeDtypeStruct((N_CHIPS, M, Fs), BF16)
    return pl.pallas_call(
        body, name="ffn_up", grid=(N_CHIPS, M // tm),
        in_specs=[pl.BlockSpec((tm, D), lambda p, i: (i, 0)), wsp, wsp],
        out_specs=[osp, osp, osp], out_shape=[osh, osh, osh],
        compiler_params=_params(("parallel", "parallel")),
    )(xb, wg, wu)


def mm_ln(a, w, l, x, gam, bet, alpha, scale, a_piece_major):
    M, D = x.shape
    R = w.shape[2]
    tm = _tile(M, 256)
    if a_piece_major:
        a_spec = pl.BlockSpec((N_CHIPS, tm, R), lambda i: (0, i, 0))
    else:
        a_spec = pl.BlockSpec((tm, N_CHIPS * R), lambda i: (i, 0))

    def body(a_ref, w_ref, x_ref, g_ref, b_ref, xo_ref, xb_ref, xh_ref, rs_ref):
        y = None
        for p in range(N_CHIPS):
            a = a_ref[p] if a_piece_major else a_ref[:, p * R:(p + 1) * R]
            d = jnp.dot(a, w_ref[p], preferred_element_type=F32)
            y = d if y is None else y + d
        pre = alpha * x_ref[...] + scale * y
        mu = jnp.mean(pre, axis=-1, keepdims=True)
        cen = pre - mu
        var = jnp.mean(cen * cen, axis=-1, keepdims=True)
        rstd = lax.rsqrt(var + LN_EPS)
        xhat = cen * rstd
        out = xhat * g_ref[...] + b_ref[...]
        xo_ref[...] = out
        xb_ref[...] = out.astype(BF16)
        xh_ref[...] = xhat
        rs_ref[...] = rstd

    row = pl.BlockSpec((tm, D), lambda i: (i, 0))
    vec = pl.BlockSpec((1, D), lambda i: (0, 0))
    return pl.pallas_call(
        body, name="mm_ln", grid=(M // tm,),
        in_specs=[a_spec, pl.BlockSpec((N_CHIPS, None, R, D), lambda i: (0, l, 0, 0)), row, vec, vec],
        out_specs=[row, row, row, pl.BlockSpec((tm, 1), lambda i: (i, 0))],
        out_shape=[jax.ShapeDtypeStruct((M, D), F32), jax.ShapeDtypeStruct((M, D), BF16),
                   jax.ShapeDtypeStruct((M, D), F32), jax.ShapeDtypeStruct((M, 1), F32)],
        compiler_params=_params(("parallel",)),
    )(a, w, x, gam, bet)


def ln_bwd(dy, xhat, rstd, gam, scale):
    M, D = dy.shape
    tm = _tile(M, 512)

    def body(dy_ref, xh_ref, rs_ref, g_ref, dp_ref, db16_ref, dg_ref, dbt_ref):
        i = pl.program_id(0)
        dy_v = dy_ref[...]
        xh = xh_ref[...]
        dxh = dy_v * g_ref[...]
        m1 = jnp.mean(dxh, axis=-1, keepdims=True)
        m2 = jnp.mean(dxh * xh, axis=-1, keepdims=True)
        dpre = rs_ref[...] * (dxh - m1 - xh * m2)
        dp_ref[...] = dpre
        db16_ref[...] = (scale * dpre).astype(BF16)
        dgp = jnp.sum(dy_v * xh, axis=0, keepdims=True)
        dbp = jnp.sum(dy_v, axis=0, keepdims=True)

        @pl.when(i == 0)
        def _():
            dg_ref[...] = dgp
            dbt_ref[...] = dbp

        @pl.when(i > 0)
        def _():
            dg_ref[...] += dgp
            dbt_ref[...] += dbp

    row = pl.BlockSpec((tm, D), lambda i: (i, 0))
    vec = pl.BlockSpec((1, D), lambda i: (0, 0))
    return pl.pallas_call(
        body, name="ln_bwd", grid=(M // tm,),
        in_specs=[row, row, pl.BlockSpec((tm, 1), lambda i: (i, 0)), vec],
        out_specs=[row, row, vec, vec],
        out_shape=[jax.ShapeDtypeStruct((M, D), F32), jax.ShapeDtypeStruct((M, D), BF16),
                   jax.ShapeDtypeStruct((1, D), F32), jax.ShapeDtypeStruct((1, D), F32)],
        compiler_params=_params(("arbitrary",)),
    )(dy, xhat, rstd, gam)


def loss_head(y, tgt):
    M, D = y.shape
    tm = _tile(M, 512)
    n = M // tm

    def body(y_ref, t_ref, dy_ref, l_ref, acc_ref):
        i = pl.program_id(0)
        e = y_ref[...] - t_ref[...]
        dy_ref[...] = e * (1.0 / D)
        part = jnp.sum(e * e, axis=0, keepdims=True)

        @pl.when(i == 0)
        def _():
            acc_ref[...] = part

        @pl.when(i > 0)
        def _():
            acc_ref[...] += part

        @pl.when(i == n - 1)
        def _():
            l_ref[...] = (0.5 / D) * jnp.sum(acc_ref[...], axis=1, keepdims=True)

    row = pl.BlockSpec((tm, D), lambda i: (i, 0))
    return pl.pallas_call(
        body, name="loss_head", grid=(n,),
        in_specs=[row, row],
        out_specs=[row, pl.BlockSpec((1, 1), lambda i: (0, 0))],
        out_shape=[jax.ShapeDtypeStruct((M, D), F32), jax.ShapeDtypeStruct((1, 1), F32)],
        scratch_shapes=[pltpu.VMEM((1, D), F32)],
        compiler_params=_params(("arbitrary",)),
    )(y, tgt)


def _rel_onehot_t():
    r = lax.broadcasted_iota(jnp.int32, (REL_PAD, VR_W), 0)
    n = lax.broadcasted_iota(jnp.int32, (REL_PAD, VR_W), 1)
    idx = jnp.clip(VR_C0 - n, -REL_CLIP, REL_CLIP) + REL_CLIP
    return (r == idx).astype(F32)


def bias_vec(tab_t):
    H = tab_t.shape[0]

    def body(t_ref, o_ref):
        o_ref[...] = jnp.dot(t_ref[...], _rel_onehot_t(), precision=lax.Precision.HIGHEST,
                             preferred_element_type=F32)

    return pl.pallas_call(
        body, name="bias_vec", out_shape=jax.ShapeDtypeStruct((H, VR_W), F32),
        compiler_params=_params(),
    )(tab_t)


def bias_vec_bwd(dvr):
    n, H, _ = dvr.shape

    def body(d_ref, o_ref):
        tot = d_ref[0]
        for i in range(1, n):
            tot = tot + d_ref[i]
        o_ref[...] = lax.dot_general(tot, _rel_onehot_t(), NT_DIMS, precision=lax.Precision.HIGHEST,
                                     preferred_element_type=F32)

    return pl.pallas_call(
        body, name="bias_vec_bwd", out_shape=jax.ShapeDtypeStruct((H, REL_PAD), F32),
        compiler_params=_params(),
    )(dvr)


def _a_bias_mask(vr_row):
    xb = jnp.broadcast_to(vr_row, (QB_A, VR_W))
    tile = pltpu.roll(xb, VR_W - (QB_A - 1), 1, stride=1, stride_axis=0)[:, :KW_A]
    qc = lax.broadcasted_iota(jnp.int32, (QB_A, KW_A), 0) // CHUNK
    kc = lax.broadcasted_iota(jnp.int32, (QB_A, KW_A), 1) // CHUNK
    valid = (kc >= qc) & (kc <= qc + LEFT_CHUNKS)
    return jnp.where(valid, tile, NEG)


def _a_diag_sums(db_acc, h):
    acc8 = None
    for a in range(QB_A // 8):
        grp = db_acc[h, 8 * a:8 * a + 8, :]
        shift = QB_A - 8 - 8 * a
        if shift:
            grp = pltpu.roll(grp, shift, 1)
        acc8 = grp if acc8 is None else acc8 + grp
    sub = lax.broadcasted_iota(jnp.int32, (8, VR_W), 0)
    tot = jnp.zeros((8, VR_W), F32)
    for b in range(8):
        moved = pltpu.roll(acc8, 7 - b, 1) if b < 7 else acc8
        tot = tot + jnp.where(sub == b, moved, 0.0)
    return jnp.sum(tot, axis=0, keepdims=True)


def _a_blocks(S):
    out = []
    for qi in range(S // QB_A):
        q0 = qi * QB_A
        ks = max(0, q0 - LOOKBACK)
        out.append((q0, ks, q0 + QB_A, ks - (q0 - LOOKBACK)))
    return out


def _head_specs(S, HP):
    q = pl.BlockSpec((S, 2 * HEAD_DIM), lambda b, hp: (b, hp))
    k = pl.BlockSpec((S, 2 * HEAD_DIM), lambda b, hp: (b, HP + hp))
    v = pl.BlockSpec((S, 2 * HEAD_DIM), lambda b, hp: (b, 2 * HP + hp))
    return q, k, v


def attn_a_fwd(qkv, vr, B, S):
    D = qkv.shape[1] // 3
    HP = D // (2 * HEAD_DIM)
    scale = HEAD_DIM ** -0.5
    blocks = _a_blocks(S)

    def body(q_ref, k_ref, v_ref, vr_ref, o_ref):
        for h in range(2):
            lo = h * HEAD_DIM
            bm = _a_bias_mask(vr_ref[h:h + 1, :])
            for (q0, ks, ke, joff) in blocks:
                q = q_ref[q0:q0 + QB_A, lo:lo + HEAD_DIM]
                k = k_ref[ks:ke, lo:lo + HEAD_DIM]
                v = v_ref[ks:ke, lo:lo + HEAD_DIM]
                s = lax.dot_general(q, k, NT_DIMS, preferred_element_type=F32) * scale + bm[:, joff:]
                m = jnp.max(s, axis=-1, keepdims=True)
                p = jnp.exp(s - m)
                den = jnp.sum(p, axis=-1, keepdims=True)
                o = jnp.dot(p.astype(BF16), v, preferred_element_type=F32) / den
                o_ref[q0:q0 + QB_A, lo:lo + HEAD_DIM] = o.astype(BF16)

    qs, ks_, vs = _head_specs(S, HP)
    return pl.pallas_call(
        body, name="attn_a_fwd", grid=(B, HP),
        in_specs=[qs, ks_, vs, pl.BlockSpec((None, 2, VR_W), lambda b, hp: (hp, 0, 0))],
        out_specs=pl.BlockSpec((S, 2 * HEAD_DIM), lambda b, hp: (b, hp)),
        out_shape=jax.ShapeDtypeStruct((B * S, D), BF16),
        compiler_params=_params(("parallel", "parallel")),
    )(qkv, qkv, qkv, vr)


def attn_a_bwd(qkv, vr, do, B, S):
    D = qkv.shape[1] // 3
    HP = D // (2 * HEAD_DIM)
    scale = HEAD_DIM ** -0.5
    blocks = _a_blocks(S)

    def body(q_ref, k_ref, v_ref, vr_ref, do_ref, dq_ref, dk_ref, dv_ref, dvr_ref,
             dk_acc, dv_acc, db_acc):
        dk_acc[...] = jnp.zeros_like(dk_acc)
        dv_acc[...] = jnp.zeros_like(dv_acc)
        db_acc[...] = jnp.zeros_like(db_acc)
        for h in range(2):
            lo = h * HEAD_DIM
            bm = _a_bias_mask(vr_ref[h:h + 1, :])
            for (q0, ks, ke, joff) in blocks:
                q = q_ref[q0:q0 + QB_A, lo:lo + HEAD_DIM]
                k = k_ref[ks:ke, lo:lo + HEAD_DIM]
                v = v_ref[ks:ke, lo:lo + HEAD_DIM]
                dob = do_ref[q0:q0 + QB_A, lo:lo + HEAD_DIM]
                s = lax.dot_general(q, k, NT_DIMS, preferred_element_type=F32) * scale + bm[:, joff:]
                m = jnp.max(s, axis=-1, keepdims=True)
                e = jnp.exp(s - m)
                p = e / jnp.sum(e, axis=-1, keepdims=True)
                dp = lax.dot_general(dob, v, NT_DIMS, preferred_element_type=F32)
                ds = p * (dp - jnp.sum(p * dp, axis=-1, keepdims=True))
                dsb = ds.astype(BF16)
                dq = jnp.dot(dsb, k, preferred_element_type=F32) * scale
                dq_ref[q0:q0 + QB_A, lo:lo + HEAD_DIM] = dq.astype(BF16)
                dk_acc[ks:ke, lo:lo + HEAD_DIM] += lax.dot_general(
                    dsb, q, TN_DIMS, preferred_element_type=F32) * scale
                dv_acc[ks:ke, lo:lo + HEAD_DIM] += lax.dot_general(
                    p.astype(BF16), dob, TN_DIMS, preferred_element_type=F32)
                db_acc[h, :, joff:KW_A] += ds
            dvr_ref[h:h + 1, :] = _a_diag_sums(db_acc, h)
        dk_ref[...] = dk_acc[...].astype(BF16)
        dv_ref[...] = dv_acc[...].astype(BF16)

    qs, ks_, vs = _head_specs(S, HP)
    hd = pl.BlockSpec((S, 2 * HEAD_DIM), lambda b, hp: (b, hp))
    osh = jax.ShapeDtypeStruct((B * S, D), BF16)
    return pl.pallas_call(
        body, name="attn_a_bwd", grid=(B, HP),
        in_specs=[qs, ks_, vs, pl.BlockSpec((None, 2, VR_W), lambda b, hp: (hp, 0, 0)), hd],
        out_specs=[hd, hd, hd, pl.BlockSpec((None, None, 2, VR_W), lambda b, hp: (b, hp, 0, 0))],
        out_shape=[osh, osh, osh, jax.ShapeDtypeStruct((B, HP, 2, VR_W), F32)],
        scratch_shapes=[pltpu.VMEM((S, 2 * HEAD_DIM), F32), pltpu.VMEM((S, 2 * HEAD_DIM), F32),
                        pltpu.VMEM((2, QB_A, VR_W), F32)],
        compiler_params=_params(("parallel", "parallel")),
    )(qkv, qkv, qkv, vr, do)


def _tri(cmp):
    j = lax.broadcasted_iota(jnp.int32, (SB_TILE, SB_TILE), 0)
    s = lax.broadcasted_iota(jnp.int32, (SB_TILE, SB_TILE), 1)
    return cmp(j, s).astype(BF16)


def _cumsum_mm(x, tri):
    hi = x.astype(BF16)
    mid = (x - hi.astype(F32)).astype(BF16)
    return jnp.dot(hi, tri, preferred_element_type=F32) + jnp.dot(mid, tri, preferred_element_type=F32)


def _sb_logs(q, k, scale, diagonal):
    z = lax.dot_general(q, k, NT_DIMS, preferred_element_type=F32) * scale
    log_b = jnp.minimum(z, 0.0) - jnp.log(1.0 + jnp.exp(-jnp.abs(z)))
    log_1mb = log_b - z
    if not diagonal:
        return log_b, log_1mb, None
    row = lax.broadcasted_iota(jnp.int32, (SB_TILE, SB_TILE), 0)
    col = lax.broadcasted_iota(jnp.int32, (SB_TILE, SB_TILE), 1)
    causal = col < row
    return log_b, jnp.where(causal, log_1mb, 0.0), causal


def attn_b_fwd(qkv, B, S):
    D = qkv.shape[1] // 3
    HP = D // (2 * HEAD_DIM)
    scale = HEAD_DIM ** -0.5
    nb = S // SB_TILE

    def body(q_ref, k_ref, v_ref, o_ref, nt_ref):
        tri = _tri(lambda j, s: j > s)
        heads = [slice(h * HEAD_DIM, (h + 1) * HEAD_DIM) for h in range(2)]

        def q_block(qb, n_pairs, parity):
            q0 = pl.multiple_of(qb * SB_TILE, SB_TILE)
            rows = pl.ds(q0, SB_TILE)
            qs = [q_ref[rows, hs] for hs in heads]

            def step(blocks, state):
                chains = [(h, kb, diagonal, pl.ds(pl.multiple_of(kb * SB_TILE, SB_TILE), SB_TILE))
                          for h in range(2) for kb, diagonal in blocks]
                logs = [_sb_logs(qs[h], k_ref[keys, heads[h]], scale, diagonal)
                        for h, _, diagonal, keys in chains]
                sums = [_cumsum_mm(log_1mb, tri) for _, log_1mb, _ in logs]
                rights = [state[0][0], state[1][0]]
                accs = [state[0][1], state[1][1]]
                for (h, _, diagonal, keys), (log_b, log_1mb, causal), csum in zip(chains, logs, sums):
                    a = jnp.exp(log_b + csum + rights[h])
                    if diagonal:
                        a = jnp.where(causal, a, 0.0)
                    accs[h] = accs[h] + jnp.dot(a.astype(BF16), v_ref[keys, heads[h]],
                                                preferred_element_type=F32)
                    rights[h] = rights[h] + jnp.sum(log_1mb, axis=-1, keepdims=True)
                return ((rights[0], accs[0]), (rights[1], accs[1]))

            zero = (jnp.zeros((SB_TILE, 1), F32), jnp.zeros((SB_TILE, HEAD_DIM), F32))
            first = [(qb, True)] + ([(qb - 1, False)] if parity else [])
            top = qb - len(first)
            state = lax.fori_loop(
                0, n_pairs, lambda t, st: step([(top - 2 * t, False), (top - 2 * t - 1, False)], st),
                step(first, (zero, zero)))
            for hs, (right, acc) in zip(heads, state):
                o_ref[rows, hs] = acc.astype(BF16)
                nt_ref[rows, hs] = jnp.broadcast_to(right, (SB_TILE, HEAD_DIM))

        def q_pair_loop(j, carry):
            q_block(2 * j, j, 0)
            q_block(2 * j + 1, j, 1)
            return carry

        lax.fori_loop(0, nb // 2, q_pair_loop, 0)

    qs, ks_, vs = _head_specs(S, HP)
    hd = pl.BlockSpec((S, 2 * HEAD_DIM), lambda b, hp: (b, hp))
    return pl.pallas_call(
        body, name="attn_b_fwd", grid=(B, HP),
        in_specs=[qs, ks_, vs], out_specs=[hd, hd],
        out_shape=[jax.ShapeDtypeStruct((B * S, D), BF16), jax.ShapeDtypeStruct((B * S, D), F32)],
        compiler_params=_params(("parallel", "parallel")),
    )(qkv, qkv, qkv)


def attn_b_bwd(qkv, do, ntot, B, S):
    D = qkv.shape[1] // 3
    HP = D // (2 * HEAD_DIM)
    scale = HEAD_DIM ** -0.5
    nb = S // SB_TILE

    def body(q_ref, k_ref, v_ref, do_ref, nt_ref, dq_ref, dk_ref, dv_ref, dkt_acc, dvt_acc):
        tri_incl = _tri(lambda j, s: j <= s)
        tri_excl = _tri(lambda j, s: j < s)
        heads = [slice(h * HEAD_DIM, (h + 1) * HEAD_DIM) for h in range(2)]
        dkt_acc[...] = jnp.zeros_like(dkt_acc)
        dvt_acc[...] = jnp.zeros_like(dvt_acc)

        def q_block(qb, n_pairs, parity):
            q0 = pl.multiple_of(qb * SB_TILE, SB_TILE)
            rows = pl.ds(q0, SB_TILE)
            qt_pair = q_ref[rows, :].astype(F32).T.astype(BF16)
            dot_pair = do_ref[rows, :].astype(F32).T.astype(BF16)
            per_head = [(hs, q_ref[rows, hs], do_ref[rows, hs], qt_pair[hs, :], dot_pair[hs, :],
                         nt_ref[rows, hs.start:hs.start + 1]) for hs in heads]

            def step(blocks, state):
                chains = [(h, kb, diagonal, pl.ds(pl.multiple_of(kb * SB_TILE, SB_TILE), SB_TILE))
                          for h in range(2) for kb, diagonal in blocks]
                ks = [k_ref[keys, per_head[h][0]] for h, _, _, keys in chains]
                logs = [_sb_logs(per_head[h][1], k, scale, diagonal)
                        for (h, _, diagonal, _), k in zip(chains, ks)]
                das = [lax.dot_general(per_head[h][2], v_ref[keys, per_head[h][0]], NT_DIMS,
                                       preferred_element_type=F32) for h, _, _, keys in chains]
                sums = [_cumsum_mm(log_1mb, tri_incl) for _, log_1mb, _ in logs]
                left_n = [state[0][0], state[1][0]]
                left_d = [state[0][1], state[1][1]]
                dq_acc = [state[0][2], state[1][2]]
                a_s, dls = [], []
                for (h, _, diagonal, _), (log_b, log_1mb, causal), csum, da in zip(chains, logs, sums, das):
                    a = jnp.exp(log_b + (per_head[h][5] - left_n[h]) - csum)
                    if diagonal:
                        a = jnp.where(causal, a, 0.0)
                    a_s.append(a)
                    dls.append(a * da)
                    left_n[h] = left_n[h] + jnp.sum(log_1mb, axis=-1, keepdims=True)
                dsums = [_cumsum_mm(dl, tri_excl) for dl in dls]
                dzbs = []
                for (h, _, diagonal, _), (log_b, log_1mb, causal), dl, dsum in zip(chains, logs, dls, dsums):
                    dz = dl * jnp.exp(log_1mb) - (left_d[h] + dsum) * jnp.exp(log_b)
                    if diagonal:
                        dz = jnp.where(causal, dz, 0.0)
                    dzbs.append(dz.astype(BF16))
                    left_d[h] = left_d[h] + jnp.sum(dl, axis=-1, keepdims=True)
                for (h, kb, _, _), k, a, dzb in zip(chains, ks, a_s, dzbs):
                    hs, _, _, qt, dot_, _ = per_head[h]
                    dq_acc[h] = dq_acc[h] + jnp.dot(dzb, k, preferred_element_type=F32)
                    dkt_acc[kb, hs, :] += jnp.dot(qt, dzb, preferred_element_type=F32)
                    dvt_acc[kb, hs, :] += jnp.dot(dot_, a.astype(BF16), preferred_element_type=F32)
                return ((left_n[0], left_d[0], dq_acc[0]), (left_n[1], left_d[1], dq_acc[1]))

            zero1 = jnp.zeros((SB_TILE, 1), F32)
            zero = (zero1, zero1, jnp.zeros((SB_TILE, HEAD_DIM), F32))
            state = lax.fori_loop(
                0, n_pairs, lambda t, st: step([(2 * t, False), (2 * t + 1, False)], st), (zero, zero))
            last = ([(qb - 1, False)] if parity else []) + [(qb, True)]
            state = step(last, state)
            for hs, (_, _, dq_acc) in zip(heads, state):
                dq_ref[rows, hs] = (dq_acc * scale).astype(BF16)

        def q_pair_loop(j, carry):
            q_block(2 * j, j, 0)
            q_block(2 * j + 1, j, 1)
            return carry

        lax.fori_loop(0, nb // 2, q_pair_loop, 0)
        for kb in range(nb):
            dk_ref[kb * SB_TILE:(kb + 1) * SB_TILE, :] = (dkt_acc[kb].T * scale).astype(BF16)
            dv_ref[kb * SB_TILE:(kb + 1) * SB_TILE, :] = dvt_acc[kb].T.astype(BF16)

    qs, ks_, vs = _head_specs(S, HP)
    hd = pl.BlockSpec((S, 2 * HEAD_DIM), lambda b, hp: (b, hp))
    osh = jax.ShapeDtypeStruct((B * S, D), BF16)
    acc = pltpu.VMEM((nb, 2 * HEAD_DIM, SB_TILE), F32)
    return pl.pallas_call(
        body, name="attn_b_bwd", grid=(B, HP),
        in_specs=[qs, ks_, vs, hd, hd], out_specs=[hd, hd, hd], out_shape=[osh, osh, osh],
        scratch_shapes=[acc, acc],
        compiler_params=_params(("parallel", "parallel")),
    )(qkv, qkv, qkv, do, ntot)


def _place():
    x, y, c = lax.axis_index("x"), lax.axis_index("y"), lax.axis_index("c")
    chips = [(1 - x, y), (x, 1 - y), (1 - x, 1 - y)]
    return x, y, c, 2 * x + y, chips


def _remote(src, dst, send_sem, recv_sem, dev):
    return pltpu.make_async_remote_copy(src_ref=src, dst_ref=dst, send_sem=send_sem, recv_sem=recv_sem,
                                        device_id=dev, device_id_type=MESH)


def gather_weights(shards):
    n = len(shards)

    def body(*refs):
        ins, outs = refs[:n], refs[n:2 * n]
        send1, recv1, send2, recv2, lsem = refs[2 * n:]
        x, y, c, me, chips = _place()
        local, first = [], []
        for f in range(n):
            hl = shards[f].shape[0] // 2
            cp = pltpu.make_async_copy(ins[f], outs[f].at[me], lsem.at[f])
            cp.start()
            local.append(cp)
            for j, (qx, qy) in enumerate(chips):
                half = pl.ds(c * hl, hl)
                cp = _remote(ins[f].at[half], outs[f].at[me, half],
                             send1.at[3 * f + j], recv1.at[3 * f + j], (qx, qy, c))
                cp.start()
                first.append(cp)
        passed = []
        for f in range(n):
            hl = shards[f].shape[0] // 2
            for j, (qx, qy) in enumerate(chips):
                slab = outs[f].at[2 * qx + qy, pl.ds(c * hl, hl)]
                _remote(slab, slab, send1.at[3 * f + j], recv1.at[3 * f + j], (x, y, c)).wait_recv()
                cp = _remote(slab, slab, send2.at[3 * f + j], recv2.at[3 * f + j], (x, y, 1 - c))
                cp.start()
                passed.append(cp)
        for f in range(n):
            hl = shards[f].shape[0] // 2
            for j, (qx, qy) in enumerate(chips):
                slab = outs[f].at[2 * qx + qy, pl.ds((1 - c) * hl, hl)]
                _remote(slab, slab, send2.at[3 * f + j], recv2.at[3 * f + j], (x, y, c)).wait_recv()
        for cp in first + passed:
            cp.wait_send()
        for cp in local:
            cp.wait()

    sems = pltpu.SemaphoreType.DMA((3 * n,))
    return pl.pallas_call(
        body, name="gather_weights",
        in_specs=[ANY] * n, out_specs=[ANY] * n,
        out_shape=[jax.ShapeDtypeStruct((N_CHIPS,) + s.shape, s.dtype) for s in shards],
        scratch_shapes=[sems, sems, sems, sems, pltpu.SemaphoreType.DMA((n,))],
        compiler_params=pltpu.CompilerParams(has_side_effects=True),
    )(*shards)


def pair_exchange(grads):
    n = len(grads)

    def body(*refs):
        ins, outs = refs[:n], refs[n:2 * n]
        send, recv = refs[2 * n:]
        x, y, c, _, _ = _place()
        cps = []
        for f in range(n):
            hl = grads[f].shape[0] // 2
            cp = _remote(ins[f].at[pl.ds((1 - c) * hl, hl)], outs[f], send.at[f], recv.at[f], (x, y, 1 - c))
            cp.start()
            cps.append(cp)
        for cp in cps:
            cp.wait()

    sems = pltpu.SemaphoreType.DMA((n,))
    return pl.pallas_call(
        body, name="pair_exchange",
        in_specs=[ANY] * n, out_specs=[ANY] * n,
        out_shape=[jax.ShapeDtypeStruct((g.shape[0] // 2,) + g.shape[1:], g.dtype) for g in grads],
        scratch_shapes=[sems, sems],
        compiler_params=pltpu.CompilerParams(has_side_effects=True),
    )(*grads)


def chip_exchange(parts):
    n = len(parts)

    def body(*refs):
        ins, outs = refs[:n], refs[n:2 * n]
        send, recv, lsem = refs[2 * n:]
        x, y, c, me, chips = _place()
        local, sent = [], []
        for f in range(n):
            hl = parts[f].shape[0]
            rows = pl.ds(0, hl)
            cp = pltpu.make_async_copy(ins[f].at[rows, me], outs[f].at[rows, me], lsem.at[f])
            cp.start()
            local.append(cp)
            for j, (qx, qy) in enumerate(chips):
                cp = _remote(ins[f].at[rows, 2 * qx + qy], outs[f].at[rows, me],
                             send.at[3 * f + j], recv.at[3 * f + j], (qx, qy, c))
                cp.start()
                sent.append(cp)
        for f in range(n):
            rows = pl.ds(0, parts[f].shape[0])
            for j, (qx, qy) in enumerate(chips):
                slab = outs[f].at[rows, 2 * qx + qy]
                _remote(slab, slab, send.at[3 * f + j], recv.at[3 * f + j], (x, y, c)).wait_recv()
        for cp in sent:
            cp.wait_send()
        for cp in local:
            cp.wait()

    sems = pltpu.SemaphoreType.DMA((3 * n,))
    return pl.pallas_call(
        body, name="chip_exchange",
        in_specs=[ANY] * n, out_specs=[ANY] * n,
        out_shape=[jax.ShapeDtypeStruct(s.shape, s.dtype) for s in parts],
        scratch_shapes=[sems, sems, pltpu.SemaphoreType.DMA((n,))],
        compiler_params=pltpu.CompilerParams(has_side_effects=True),
    )(*parts)


def half_swap(grads):
    n = len(grads)

    def body(*refs):
        ins, outs = refs[:n], refs[n:2 * n]
        send, recv = refs[2 * n:]
        x, y, c, _, _ = _place()
        cps = []
        for f in range(n):
            hl = grads[f].shape[0] // 2
            mine = pl.ds(c * hl, hl)
            cp = _remote(outs[f].at[mine], outs[f].at[mine], send.at[f], recv.at[f], (x, y, 1 - c))
            cp.start()
            cps.append(cp)
        for f in range(n):
            hl = grads[f].shape[0] // 2
            theirs = outs[f].at[pl.ds((1 - c) * hl, hl)]
            _remote(theirs, theirs, send.at[f], recv.at[f], (x, y, c)).wait_recv()
        for cp in cps:
            cp.wait_send()

    sems = pltpu.SemaphoreType.DMA((n,))
    return pl.pallas_call(
        body, name="half_swap",
        in_specs=[ANY] * n, out_specs=[ANY] * n,
        out_shape=[jax.ShapeDtypeStruct(g.shape, g.dtype) for g in grads],
        input_output_aliases={f: f for f in range(n)},
        scratch_shapes=[sems, sems],
        compiler_params=pltpu.CompilerParams(has_side_effects=True),
    )(*grads)


def all_sum_small(v):
    R = v.shape[0]

    def body(v_ref, o_ref, land, send, recv):
        x, y, c, _, _ = _place()
        me = 4 * x + 2 * y + c
        land[me] = v_ref[...]
        peers = [(px, py, pc) for px in range(2) for py in range(2) for pc in range(2)]
        cps = []
        for k in range(1, 8):
            dev = (x ^ (k >> 2), y ^ ((k >> 1) & 1), c ^ (k & 1))
            cp = _remote(v_ref, land.at[me], send.at[k - 1], recv.at[k - 1], dev)
            cp.start()
            cps.append(cp)
        for k in range(1, 8):
            src = 4 * (x ^ (k >> 2)) + 2 * (y ^ ((k >> 1) & 1)) + (c ^ (k & 1))
            _remote(v_ref, land.at[src], send.at[k - 1], recv.at[k - 1], (x, y, c)).wait_recv()
        for cp in cps:
            cp.wait_send()
        tot = land[0]
        for d in range(1, len(peers)):
            tot = tot + land[d]
        o_ref[...] = tot

    sems = pltpu.SemaphoreType.DMA((7,))
    vm = pl.BlockSpec(memory_space=pltpu.VMEM)
    return pl.pallas_call(
        body, name="all_sum_small", in_specs=[vm], out_specs=vm,
        out_shape=jax.ShapeDtypeStruct(v.shape, F32),
        scratch_shapes=[pltpu.VMEM((8, R, 128), F32), sems, sems],
        compiler_params=pltpu.CompilerParams(has_side_effects=True),
    )(v)


def _row_tile(R):
    for t in (512, 256, 128, 64, 32, 16):
        if R % t == 0:
            return t
    return R


def pair_add(cidx, grad, recv):
    hl, P, R, C = recv.shape
    tr = _row_tile(R)

    def body(c_ref, a_ref, b_ref, o_ref):
        o_ref[...] = (a_ref[...].astype(F32) + b_ref[...].astype(F32)).astype(o_ref.dtype)

    blk = (None, None, tr, C)
    return pl.pallas_call(
        body, name="pair_add",
        grid_spec=pltpu.PrefetchScalarGridSpec(
            num_scalar_prefetch=1, grid=(hl, P, R // tr),
            in_specs=[pl.BlockSpec(blk, lambda l, p, r, c: (c[0] * hl + l, p, r, 0)),
                      pl.BlockSpec(blk, lambda l, p, r, c: (l, p, r, 0))],
            out_specs=pl.BlockSpec(blk, lambda l, p, r, c: (l, p, r, 0))),
        out_shape=jax.ShapeDtypeStruct(recv.shape, recv.dtype),
        compiler_params=_params(("parallel", "parallel", "parallel")),
    )(cidx, grad, recv)


def chip_sum(cidx, land, L):
    hl, P, R, C = land.shape
    tr = _row_tile(R)

    def body(c_ref, a_ref, o_ref):
        tot = a_ref[0].astype(F32)
        for q in range(1, P):
            tot = tot + a_ref[q].astype(F32)
        o_ref[...] = tot

    return pl.pallas_call(
        body, name="chip_sum",
        grid_spec=pltpu.PrefetchScalarGridSpec(
            num_scalar_prefetch=1, grid=(hl, R // tr),
            in_specs=[pl.BlockSpec((None, P, tr, C), lambda l, r, c: (l, 0, r, 0))],
            out_specs=pl.BlockSpec((None, tr, C), lambda l, r, c: (c[0] * hl + l, r, 0))),
        out_shape=jax.ShapeDtypeStruct((L, R, C), F32),
        compiler_params=_params(("parallel", "parallel")),
    )(cidx, land)


def adamw(w, g, m, v):
    L, R, C = w.shape
    tr = _row_tile(R)
    c1 = 1.0 / (1.0 - ADAM_B1 ** ADAM_STEP)
    c2 = 1.0 / (1.0 - ADAM_B2 ** ADAM_STEP)

    def body(w_ref, g_ref, m_ref, v_ref, d_ref, nm_ref, nv_ref):
        gv = g_ref[...]
        nm = ADAM_B1 * m_ref[...] + (1.0 - ADAM_B1) * gv
        nv = ADAM_B2 * v_ref[...] + (1.0 - ADAM_B2) * (gv * gv)
        nm_ref[...] = nm
        nv_ref[...] = nv
        d_ref[...] = -ADAM_LR * ((nm * c1) / (jnp.sqrt(nv * c2) + ADAM_EPS) + ADAM_WD * w_ref[...])

    blk = pl.BlockSpec((None, tr, C), lambda l, r: (l, r, 0))
    osh = jax.ShapeDtypeStruct((L, R, C), F32)
    return pl.pallas_call(
        body, name="adamw", grid=(L, R // tr),
        in_specs=[blk, blk, blk, blk], out_specs=[blk, blk, blk], out_shape=[osh, osh, osh],
        compiler_params=_params(("parallel", "parallel")),
    )(w, g, m, v)


def kernel(x, w_qkv_a, w_o_a, rel_bias, w_qkv_b, w_o_b, ffn_w_gate, ffn_w_up, ffn_w_down, ln_g, ln_b, loss_target, m_w_qkv_a, m_w_o_a, m_rel_bias, m_w_qkv_b, m_w_o_b, m_ffn_w_gate, m_ffn_w_up, m_ffn_w_down, m_ln_g, m_ln_b, v_w_qkv_a, v_w_o_a, v_rel_bias, v_w_qkv_b, v_w_o_b, v_ffn_w_gate, v_ffn_w_up, v_ffn_w_down, v_ln_g, v_ln_b):
    B, S, D = x.shape
    M = B * S
    depth = ffn_w_gate.shape[0]
    n_ffn = 2 * depth
    H = D // HEAD_DIM
    HP = H // 2
    Fs = ffn_w_gate.shape[-1]
    alpha = (2.0 * depth) ** 0.25
    assert S % QB_A == 0 and S % SB_TILE == 0 and rel_bias.shape == (N_REL, H)

    def ffn3(a):
        return a.reshape((n_ffn,) + a.shape[2:])

    shards = [w_qkv_a.astype(BF16), w_o_a.astype(BF16), w_qkv_b.astype(BF16), w_o_b.astype(BF16),
              ffn3(ffn_w_gate).astype(BF16), ffn3(ffn_w_up).astype(BF16), ffn3(ffn_w_down).astype(BF16),
              ln_g, ln_b]
    wqa, woa, wqb, wob, wg, wu, wd, lng_p, lnb_p = gather_weights(shards)
    lng = jnp.moveaxis(lng_p, 0, 2).reshape(depth, 3, 1, D)
    lnb = jnp.moveaxis(lnb_p, 0, 2).reshape(depth, 3, 1, D)

    tab_t = jnp.pad(rel_bias.T, ((0, 0), (0, REL_PAD - N_REL)))
    vr = bias_vec(tab_t).reshape(HP, 2, VR_W)

    xf = x.reshape(M, D)
    xb = xf.astype(BF16)
    saved = []
    for i in range(depth):
        for j in range(3):
            gam, bet = lng[i, j], lnb[i, j]
            if j != 1:
                l = 2 * i + (0 if j == 0 else 1)
                g, u, h = ffn_up(xb, wg, wu, l)
                xo, xob, xhat, rstd = mm_ln(h, wd, l, xf, gam, bet, alpha, 0.5, True)
                saved.append(("ffn", l, xb, g, u, h, xhat, rstd, gam))
            elif i % 2 == 0:
                l = i // 2
                qkv = qkv_proj(xb, wqa, l)
                o = attn_a_fwd(qkv, vr, B, S)
                xo, xob, xhat, rstd = mm_ln(o, woa, l, xf, gam, bet, alpha, 1.0, False)
                saved.append(("a", l, xb, qkv, o, None, xhat, rstd, gam))
            else:
                l = i // 2
                qkv = qkv_proj(xb, wqb, l)
                o, ntot = attn_b_fwd(qkv, B, S)
                xo, xob, xhat, rstd = mm_ln(o, wob, l, xf, gam, bet, alpha, 1.0, False)
                saved.append(("b", l, xb, qkv, o, ntot, xhat, rstd, gam))
            xf, xb = xo, xob

    dy, loss_part = loss_head(xf, loss_target.reshape(M, D))
    loss = lax.psum(loss_part[0, 0], ("x", "y", "c"))

    la, lb = w_qkv_a.shape[0], w_qkv_b.shape[0]
    Cq, Ro = w_qkv_a.shape[-1], w_o_a.shape[1]
    gqa = jnp.zeros((la, N_CHIPS, D, Cq), BF16)
    goa = jnp.zeros((la, N_CHIPS, Ro, D), BF16)
    gqb = jnp.zeros((lb, N_CHIPS, D, Cq), BF16)
    gob = jnp.zeros((lb, N_CHIPS, Ro, D), BF16)
    ggate = jnp.zeros((n_ffn, N_CHIPS, D, Fs), BF16)
    gup = jnp.zeros((n_ffn, N_CHIPS, D, Fs), BF16)
    gdown = jnp.zeros((n_ffn, N_CHIPS, Fs, D), BF16)
    dgam, dbet, dvrs = [], [], []
    for rec in reversed(saved):
        kind, l, xb_in, t1, t2, t3, xhat, rstd, gam = rec
        scale = 0.5 if kind == "ffn" else 1.0
        dpre, dyb, dg_, db_ = ln_bwd(dy, xhat, rstd, gam, scale)
        dgam.append(dg_)
        dbet.append(db_)
        if kind == "ffn":
            g, u, h = t1, t2, t3
            dg, du, gdown, ggate, gup = ffn_bwd(gdown, ggate, gup, l, xb_in, dyb, wd, g, u, h)
            dy = ffn_dx(dg, du, wg, wu, l, dpre, alpha)
        else:
            qkv, o = t1, t2
            wo, wq = (woa, wqa) if kind == "a" else (wob, wqb)
            do = o_proj_bwd(dyb, wo, l)
            if kind == "a":
                goa = dw_o(goa, l, o, dyb)
                dq, dk, dv, dvr = attn_a_bwd(qkv, vr, do, B, S)
                dvrs.append(dvr.reshape(B, H, VR_W))
            else:
                gob = dw_o(gob, l, o, dyb)
                dq, dk, dv = attn_b_bwd(qkv, do, t3, B, S)
            dqkv = jnp.concatenate([dq, dk, dv], axis=1)
            if kind == "a":
                gqa = dw_qkv(gqa, l, xb_in, dqkv)
            else:
                gqb = dw_qkv(gqb, l, xb_in, dqkv)
            dy = qkv_proj_bwd(dqkv, wq, l, dpre, alpha)
    grad_x = dy.reshape(B, S, D)

    def ln_family(parts):
        full = jnp.concatenate(parts[::-1], axis=0).reshape(depth, 3, N_CHIPS, D // N_CHIPS)
        return jnp.moveaxis(full, 2, 1)

    glng, glnb = ln_family(dgam), ln_family(dbet)

    cidx = lax.axis_index("c").astype(jnp.int32).reshape(1)
    fams = [gqa, goa, gqb, gob, ggate, gup, gdown, glng, glnb]
    from_sib = pair_exchange(fams)
    parts = [pair_add(cidx, g_, r_) for g_, r_ in zip(fams, from_sib)]
    lands = chip_exchange(parts)
    sums = [chip_sum(cidx, ld, g_.shape[0]) for ld, g_ in zip(lands, fams)]
    g_qa, g_oa, g_qb, g_ob, g_gate, g_up, g_down, g_lng, g_lnb = half_swap(sums)

    d_tab_t = bias_vec_bwd(jnp.concatenate(dvrs, axis=0))
    rows = -(-(H * REL_PAD) // (8 * 128)) * 8
    flat = jnp.pad(d_tab_t.reshape(-1), (0, rows * 128 - H * REL_PAD)).reshape(rows, 128)
    tot = all_sum_small(flat).reshape(-1)[:H * REL_PAD].reshape(H, REL_PAD)
    g_rel = tot[:, :N_REL].T

    def upd(w, g, m, v):
        shp = w.shape
        w3, m3, v3 = (a.reshape(g.shape) for a in (w, m, v))
        d, nm, nv = adamw(w3, g, m3, v3)
        return g.reshape(shp), d.reshape(shp), nm.reshape(shp), nv.reshape(shp)

    res = [
        upd(w_qkv_a, g_qa, m_w_qkv_a, v_w_qkv_a),
        upd(w_o_a, g_oa, m_w_o_a, v_w_o_a),
        upd(rel_bias, g_rel.reshape(1, N_REL, H), m_rel_bias, v_rel_bias),
        upd(w_qkv_b, g_qb, m_w_qkv_b, v_w_qkv_b),
        upd(w_o_b, g_ob, m_w_o_b, v_w_o_b),
        upd(ffn_w_gate, g_gate, m_ffn_w_gate, v_ffn_w_gate),
        upd(ffn_w_up, g_up, m_ffn_w_up, v_ffn_w_up),
        upd(ffn_w_down, g_down, m_ffn_w_down, v_ffn_w_down),
        upd(ln_g, g_lng, m_ln_g, v_ln_g),
        upd(ln_b, g_lnb, m_ln_b, v_ln_b),
    ]
    grads = [r[0] for r in res]
    deltas = [r[1] for r in res]
    new_m = [r[2] for r in res]
    new_v = [r[3] for r in res]
    return (loss, grad_x, *grads, *deltas, *new_m, *new_v)
```

```python
import functools
import math

import jax
import jax.numpy as jnp
from jax import lax
from jax.experimental import pallas as pl
from jax.experimental.pallas import tpu as pltpu

F32 = jnp.float32
BF16 = jnp.bfloat16
MESH = pl.DeviceIdType.MESH

N_CHIPS = 4
HEAD_DIM = 64
CHUNK = 64
LEFT_CHUNKS = 8
LOOKBACK = LEFT_CHUNKS * CHUNK
REL_CLIP = 128
N_REL = 2 * REL_CLIP + 1
REL_PAD = 384
SB_TILE = 256
QB_A = 256
KW_A = QB_A + LOOKBACK
VR_W = 1024
VR_C0 = KW_A - 1
LN_EPS = 1e-5
ADAM_LR, ADAM_B1, ADAM_B2, ADAM_EPS, ADAM_WD, ADAM_STEP = 0.001, 0.9, 0.999, 1e-08, 0.01, 10
NEG = -1e30
VMEM_LIMIT = 56 * 1024 * 1024

NT_DIMS = (((1,), (1,)), ((), ()))
TN_DIMS = (((0,), (0,)), ((), ()))
ANY = pl.BlockSpec(memory_space=pl.ANY)


def _params(sem=None):
    return pltpu.CompilerParams(dimension_semantics=sem, vmem_limit_bytes=VMEM_LIMIT)


def _tile(n, pref):
    t = min(n, pref)
    assert n % t == 0, (n, pref)
    return t


def _sigmoid(z):
    return 1.0 / (1.0 + jnp.exp(-z))


def _mm_call(name, operands, in_specs, out_shape, out_spec, grid, dims_list, acc_shape,
             add_coef=None, aliases=None):
    n_pairs = len(dims_list)
    nk = grid[-1]
    has_add = add_coef is not None
    n_alias = len(aliases) if aliases else 0

    def body(*refs):
        pair_refs = refs[:2 * n_pairs]
        pos = 2 * n_pairs
        add_ref = refs[pos] if has_add else None
        pos += (1 if has_add else 0) + n_alias
        o_ref = refs[pos]
        acc_ref = refs[pos + 1] if nk > 1 else None

        def product():
            part = None
            for i, dims in enumerate(dims_list):
                d = lax.dot_general(pair_refs[2 * i][...], pair_refs[2 * i + 1][...], dims,
                                    preferred_element_type=F32)
                part = d if part is None else part + d
            return part

        def finish(r):
            if has_add:
                r = r + add_coef * add_ref[...]
            o_ref[...] = r.astype(o_ref.dtype)

        if nk == 1:
            finish(product())
        else:
            k = pl.program_id(len(grid) - 1)

            @pl.when(k == 0)
            def _():
                acc_ref[...] = jnp.zeros_like(acc_ref)

            acc_ref[...] += product()

            @pl.when(k == nk - 1)
            def _():
                finish(acc_ref[...])

    sem = ("parallel",) * (len(grid) - 1) + ("arbitrary",)
    return pl.pallas_call(
        body, name=name, grid=grid, in_specs=in_specs, out_specs=out_spec, out_shape=out_shape,
        scratch_shapes=[pltpu.VMEM(acc_shape, F32)] if nk > 1 else [],
        input_output_aliases=aliases or {},
        compiler_params=_params(sem),
    )(*operands)


def qkv_proj(xb, w, l):
    M, D = xb.shape
    C = w.shape[-1]
    tm = _tile(M, 512)
    return _mm_call(
        "qkv_proj", (xb, w),
        [pl.BlockSpec((tm, D), lambda p, i, k: (i, 0)),
         pl.BlockSpec((None, None, D, C), lambda p, i, k: (p, l, 0, 0))],
        jax.ShapeDtypeStruct((M, N_CHIPS * C), BF16),
        pl.BlockSpec((tm, C), lambda p, i, k: (i, p)),
        (N_CHIPS, M // tm, 1), [(((1,), (0,)), ((), ()))], None)


def o_proj_bwd(dyb, w, l):
    M, D = dyb.shape
    R = w.shape[2]
    tm = _tile(M, 512)
    return _mm_call(
        "o_proj_bwd", (dyb, w),
        [pl.BlockSpec((tm, D), lambda p, i, k: (i, 0)),
         pl.BlockSpec((None, None, R, D), lambda p, i, k: (p, l, 0, 0))],
        jax.ShapeDtypeStruct((M, N_CHIPS * R), BF16),
        pl.BlockSpec((tm, R), lambda p, i, k: (i, p)),
        (N_CHIPS, M // tm, 1), [NT_DIMS], None)


def qkv_proj_bwd(dqkv, w, l, dpre, alpha):
    M = dqkv.shape[0]
    D, C = w.shape[2], w.shape[3]
    tm = _tile(M, 512)

    def body(a_ref, w_ref, add_ref, o_ref):
        acc = alpha * add_ref[...]
        for p in range(N_CHIPS):
            acc = acc + lax.dot_general(a_ref[:, p * C:(p + 1) * C], w_ref[p], NT_DIMS,
                                        preferred_element_type=F32)
        o_ref[...] = acc

    row = pl.BlockSpec((tm, D), lambda i: (i, 0))
    return pl.pallas_call(
        body, name="qkv_proj_bwd", grid=(M // tm,),
        in_specs=[pl.BlockSpec((tm, N_CHIPS * C), lambda i: (i, 0)),
                  pl.BlockSpec((N_CHIPS, None, D, C), lambda i: (0, l, 0, 0)), row],
        out_specs=row, out_shape=jax.ShapeDtypeStruct((M, D), F32),
        compiler_params=_params(("parallel",)),
    )(dqkv, w, dpre)


def ffn_dx(dg, du, wg, wu, l, dpre, alpha):
    _, M, Fs = dg.shape
    D = wg.shape[2]
    tm = _tile(M, 256)

    def body(dg_ref, wg_ref, du_ref, wu_ref, add_ref, o_ref):
        acc = alpha * add_ref[...]
        for p in range(N_CHIPS):
            acc = acc + lax.dot_general(dg_ref[p], wg_ref[p], NT_DIMS, preferred_element_type=F32)
            acc = acc + lax.dot_general(du_ref[p], wu_ref[p], NT_DIMS, preferred_element_type=F32)
        o_ref[...] = acc

    act = pl.BlockSpec((N_CHIPS, tm, Fs), lambda i: (0, i, 0))
    wsp = pl.BlockSpec((N_CHIPS, None, D, Fs), lambda i: (0, l, 0, 0))
    row = pl.BlockSpec((tm, D), lambda i: (i, 0))
    return pl.pallas_call(
        body, name="ffn_dx", grid=(M // tm,),
        in_specs=[act, wsp, act, wsp, row],
        out_specs=row, out_shape=jax.ShapeDtypeStruct((M, D), F32),
        compiler_params=_params(("parallel",)),
    )(dg, wg, du, wu, dpre)


def _dw_call(name, buf, l, a, b, a_spec, b_spec, M, tk):
    _, _, R, C = buf.shape
    return _mm_call(
        name, (a, b, buf),
        [a_spec, b_spec, ANY],
        jax.ShapeDtypeStruct(buf.shape, buf.dtype),
        pl.BlockSpec((None, None, R, C), lambda p, k: (l, p, 0, 0)),
        (N_CHIPS, M // tk), [TN_DIMS], (R, C), aliases={2: 0})


def ffn_bwd(gdown, ggate, gup, l, xb, dyb, wd, g, u, h):
    M, D = dyb.shape
    Fs = wd.shape[2]
    tm = _tile(M, 512)
    n = M // tm

    def body(x_ref, dy_ref, wd_ref, g_ref, u_ref, h_ref, _gd, _gg, _gu,
             dg_ref, du_ref, gd_ref, gg_ref, gu_ref, acc_d, acc_g, acc_u):
        i = pl.program_id(1)

        @pl.when(i == 0)
        def _():
            acc_d[...] = jnp.zeros_like(acc_d)
            acc_g[...] = jnp.zeros_like(acc_g)
            acc_u[...] = jnp.zeros_like(acc_u)

        dy = dy_ref[...]
        dh = lax.dot_general(dy, wd_ref[...], NT_DIMS, preferred_element_type=F32)
        gf = g_ref[...].astype(F32)
        sig = _sigmoid(gf)
        silu = gf * sig
        dg = (dh * u_ref[...].astype(F32) * (sig * (1.0 + gf - silu))).astype(BF16)
        du = (dh * silu).astype(BF16)
        dg_ref[...] = dg
        du_ref[...] = du
        x = x_ref[...]
        acc_d[...] += lax.dot_general(h_ref[...], dy, TN_DIMS, preferred_element_type=F32)
        acc_g[...] += lax.dot_general(x, dg, TN_DIMS, preferred_element_type=F32)
        acc_u[...] += lax.dot_general(x, du, TN_DIMS, preferred_element_type=F32)

        @pl.when(i == n - 1)
        def _():
            gd_ref[...] = acc_d[...].astype(gd_ref.dtype)
            gg_ref[...] = acc_g[...].astype(gg_ref.dtype)
            gu_ref[...] = acc_u[...].astype(gu_ref.dtype)

    row = pl.BlockSpec((tm, D), lambda p, i: (i, 0))
    act = pl.BlockSpec((None, tm, Fs), lambda p, i: (p, i, 0))
    ash = jax.ShapeDtypeStruct((N_CHIPS, M, Fs), BF16)
    down_blk = pl.BlockSpec((None, None, Fs, D), lambda p, i: (l, p, 0, 0))
    up_blk = pl.BlockSpec((None, None, D, Fs), lambda p, i: (l, p, 0, 0))
    return pl.pallas_call(
        body, name="ffn_bwd", grid=(N_CHIPS, n),
        in_specs=[row, row, pl.BlockSpec((None, None, Fs, D), lambda p, i: (p, l, 0, 0)),
                  act, act, act, ANY, ANY, ANY],
        out_specs=[act, act, down_blk, up_blk, up_blk],
        out_shape=[ash, ash] + [jax.ShapeDtypeStruct(b.shape, b.dtype) for b in (gdown, ggate, gup)],
        scratch_shapes=[pltpu.VMEM((Fs, D), F32), pltpu.VMEM((D, Fs), F32), pltpu.VMEM((D, Fs), F32)],
        input_output_aliases={6: 2, 7: 3, 8: 4},
        compiler_params=_params(("parallel", "arbitrary")),
    )(xb, dyb, wd, g, u, h, gdown, ggate, gup)


def dw_qkv(buf, l, xb, dqkv):
    M, D = xb.shape
    C = buf.shape[-1]
    tk = _tile(M, 1024)
    return _dw_call("dw_qkv", buf, l, xb, dqkv,
                    pl.BlockSpec((tk, D), lambda p, k: (k, 0)),
                    pl.BlockSpec((tk, C), lambda p, k: (k, p)), M, tk)


def dw_o(buf, l, o, dyb):
    M, D = dyb.shape
    R = buf.shape[2]
    tk = _tile(M, 1024)
    return _dw_call("dw_o", buf, l, o, dyb,
                    pl.BlockSpec((tk, R), lambda p, k: (k, p)),
                    pl.BlockSpec((tk, D), lambda p, k: (k, 0)), M, tk)


def ffn_up(xb, wg, wu, l):
    M, D = xb.shape
    Fs = wg.shape[-1]
    tm = _tile(M, 512)

    def body(x_ref, wg_ref, wu_ref, g_ref, u_ref, h_ref):
        x = x_ref[...]
        g = jnp.dot(x, wg_ref[...], preferred_element_type=F32)
        u = jnp.dot(x, wu_ref[...], preferred_element_type=F32)
        g_ref[...] = g.astype(BF16)
        u_ref[...] = u.astype(BF16)
        h_ref[...] = (g * _sigmoid(g) * u).astype(BF16)

    wsp = pl.BlockSpec((None, None, D, Fs), lambda p, i: (p, l, 0, 0))
    osp = pl.BlockSpec((None, tm, Fs), lambda p, i: (p, i, 0))
    osh = jax.ShapeDtypeStruct((N_CHIPS, M, Fs), BF16)
    return pl.pallas_call(
        body, name="ffn_up", grid=(N_CHIPS, M // tm),
        in_specs=[pl.BlockSpec((tm, D), lambda p, i: (i, 0)), wsp, wsp],
        out_specs=[osp, osp, osp], out_shape=[osh, osh, osh],
        compiler_params=_params(("parallel", "parallel")),
    )(xb, wg, wu)


def mm_ln(a, w, l, x, gam, bet, alpha, scale, a_piece_major):
    M, D = x.shape
    R = w.shape[2]
    tm = _tile(M, 256)
    if a_piece_major:
        a_spec = pl.BlockSpec((N_CHIPS, tm, R), lambda i: (0, i, 0))
    else:
        a_spec = pl.BlockSpec((tm, N_CHIPS * R), lambda i: (i, 0))

    def body(a_ref, w_ref, x_ref, g_ref, b_ref, xo_ref, xb_ref, xh_ref, rs_ref):
        y = None
        for p in range(N_CHIPS):
            a = a_ref[p] if a_piece_major else a_ref[:, p * R:(p + 1) * R]
            d = jnp.dot(a, w_ref[p], preferred_element_type=F32)
            y = d if y is None else y + d
        pre = alpha * x_ref[...] + scale * y
        mu = jnp.mean(pre, axis=-1, keepdims=True)
        cen = pre - mu
        var = jnp.mean(cen * cen, axis=-1, keepdims=True)
        rstd = lax.rsqrt(var + LN_EPS)
        xhat = cen * rstd
        out = xhat * g_ref[...] + b_ref[...]
        xo_ref[...] = out
        xb_ref[...] = out.astype(BF16)
        xh_ref[...] = xhat
        rs_ref[...] = rstd

    row = pl.BlockSpec((tm, D), lambda i: (i, 0))
    vec = pl.BlockSpec((1, D), lambda i: (0, 0))
    return pl.pallas_call(
        body, name="mm_ln", grid=(M // tm,),
        in_specs=[a_spec, pl.BlockSpec((N_CHIPS, None, R, D), lambda i: (0, l, 0, 0)), row, vec, vec],
        out_specs=[row, row, row, pl.BlockSpec((tm, 1), lambda i: (i, 0))],
        out_shape=[jax.ShapeDtypeStruct((M, D), F32), jax.ShapeDtypeStruct((M, D), BF16),
                   jax.ShapeDtypeStruct((M, D), F32), jax.ShapeDtypeStruct((M, 1), F32)],
        compiler_params=_params(("parallel",)),
    )(a, w, x, gam, bet)


def ln_bwd(dy, xhat, rstd, gam, scale):
    M, D = dy.shape
    tm = _tile(M, 512)

    def body(dy_ref, xh_ref, rs_ref, g_ref, dp_ref, db16_ref, dg_ref, dbt_ref):
        i = pl.program_id(0)
        dy_v = dy_ref[...]
        xh = xh_ref[...]
        dxh = dy_v * g_ref[...]
        m1 = jnp.mean(dxh, axis=-1, keepdims=True)
        m2 = jnp.mean(dxh * xh, axis=-1, keepdims=True)
        dpre = rs_ref[...] * (dxh - m1 - xh * m2)
        dp_ref[...] = dpre
        db16_ref[...] = (scale * dpre).astype(BF16)
        dgp = jnp.sum(dy_v * xh, axis=0, keepdims=True)
        dbp = jnp.sum(dy_v, axis=0, keepdims=True)

        @pl.when(i == 0)
        def _():
            dg_ref[...] = dgp
            dbt_ref[...] = dbp

        @pl.when(i > 0)
        def _():
            dg_ref[...] += dgp
            dbt_ref[...] += dbp

    row = pl.BlockSpec((tm, D), lambda i: (i, 0))
    vec = pl.BlockSpec((1, D), lambda i: (0, 0))
    return pl.pallas_call(
        body, name="ln_bwd", grid=(M // tm,),
        in_specs=[row, row, pl.BlockSpec((tm, 1), lambda i: (i, 0)), vec],
        out_specs=[row, row, vec, vec],
        out_shape=[jax.ShapeDtypeStruct((M, D), F32), jax.ShapeDtypeStruct((M, D), BF16),
                   jax.ShapeDtypeStruct((1, D), F32), jax.ShapeDtypeStruct((1, D), F32)],
        compiler_params=_params(("arbitrary",)),
    )(dy, xhat, rstd, gam)


def loss_head(y, tgt):
    M, D = y.shape
    tm = _tile(M, 512)
    n = M // tm

    def body(y_ref, t_ref, dy_ref, l_ref, acc_ref):
        i = pl.program_id(0)
        e = y_ref[...] - t_ref[...]
        dy_ref[...] = e * (1.0 / D)
        part = jnp.sum(e * e, axis=0, keepdims=True)

        @pl.when(i == 0)
        def _():
            acc_ref[...] = part

        @pl.when(i > 0)
        def _():
            acc_ref[...] += part

        @pl.when(i == n - 1)
        def _():
            l_ref[...] = (0.5 / D) * jnp.sum(acc_ref[...], axis=1, keepdims=True)

    row = pl.BlockSpec((tm, D), lambda i: (i, 0))
    return pl.pallas_call(
        body, name="loss_head", grid=(n,),
        in_specs=[row, row],
        out_specs=[row, pl.BlockSpec((1, 1), lambda i: (0, 0))],
        out_shape=[jax.ShapeDtypeStruct((M, D), F32), jax.ShapeDtypeStruct((1, 1), F32)],
        scratch_shapes=[pltpu.VMEM((1, D), F32)],
        compiler_params=_params(("arbitrary",)),
    )(y, tgt)


def _rel_onehot_t():
    r = lax.broadcasted_iota(jnp.int32, (REL_PAD, VR_W), 0)
    n = lax.broadcasted_iota(jnp.int32, (REL_PAD, VR_W), 1)
    idx = jnp.clip(VR_C0 - n, -REL_CLIP, REL_CLIP) + REL_CLIP
    return (r == idx).astype(F32)


def bias_vec(tab_t):
    H = tab_t.shape[0]

    def body(t_ref, o_ref):
        o_ref[...] = jnp.dot(t_ref[...], _rel_onehot_t(), precision=lax.Precision.HIGHEST,
                             preferred_element_type=F32)

    return pl.pallas_call(
        body, name="bias_vec", out_shape=jax.ShapeDtypeStruct((H, VR_W), F32),
        compiler_params=_params(),
    )(tab_t)


def bias_vec_bwd(dvr):
    n, H, _ = dvr.shape

    def body(d_ref, o_ref):
        tot = d_ref[0]
        for i in range(1, n):
            tot = tot + d_ref[i]
        o_ref[...] = lax.dot_general(tot, _rel_onehot_t(), NT_DIMS, precision=lax.Precision.HIGHEST,
                                     preferred_element_type=F32)

    return pl.pallas_call(
        body, name="bias_vec_bwd", out_shape=jax.ShapeDtypeStruct((H, REL_PAD), F32),
        compiler_params=_params(),
    )(dvr)


def _a_bias_mask(vr_row):
    xb = jnp.broadcast_to(vr_row, (QB_A, VR_W))
    tile = pltpu.roll(xb, VR_W - (QB_A - 1), 1, stride=1, stride_axis=0)[:, :KW_A]
    qc = lax.broadcasted_iota(jnp.int32, (QB_A, KW_A), 0) // CHUNK
    kc = lax.broadcasted_iota(jnp.int32, (QB_A, KW_A), 1) // CHUNK
    valid = (kc >= qc) & (kc <= qc + LEFT_CHUNKS)
    return jnp.where(valid, tile, NEG)


def _a_diag_sums(db_acc, h):
    acc8 = None
    for a in range(QB_A // 8):
        grp = db_acc[h, 8 * a:8 * a + 8, :]
        shift = QB_A - 8 - 8 * a
        if shift:
            grp = pltpu.roll(grp, shift, 1)
        acc8 = grp if acc8 is None else acc8 + grp
    sub = lax.broadcasted_iota(jnp.int32, (8, VR_W), 0)
    tot = jnp.zeros((8, VR_W), F32)
    for b in range(8):
        moved = pltpu.roll(acc8, 7 - b, 1) if b < 7 else acc8
        tot = tot + jnp.where(sub == b, moved, 0.0)
    return jnp.sum(tot, axis=0, keepdims=True)


def _a_blocks(S):
    out = []
    for qi in range(S // QB_A):
        q0 = qi * QB_A
        ks = max(0, q0 - LOOKBACK)
        out.append((q0, ks, q0 + QB_A, ks - (q0 - LOOKBACK)))
    return out


def _head_specs(S, HP):
    q = pl.BlockSpec((S, 2 * HEAD_DIM), lambda b, hp: (b, hp))
    k = pl.BlockSpec((S, 2 * HEAD_DIM), lambda b, hp: (b, HP + hp))
    v = pl.BlockSpec((S, 2 * HEAD_DIM), lambda b, hp: (b, 2 * HP + hp))
    return q, k, v


def attn_a_fwd(qkv, vr, B, S):
    D = qkv.shape[1] // 3
    HP = D // (2 * HEAD_DIM)
    scale = HEAD_DIM ** -0.5
    blocks = _a_blocks(S)

    def body(q_ref, k_ref, v_ref, vr_ref, o_ref):
        for h in range(2):
            lo = h * HEAD_DIM
            bm = _a_bias_mask(vr_ref[h:h + 1, :])
            for (q0, ks, ke, joff) in blocks:
                q = q_ref[q0:q0 + QB_A, lo:lo + HEAD_DIM]
                k = k_ref[ks:ke, lo:lo + HEAD_DIM]
                v = v_ref[ks:ke, lo:lo + HEAD_DIM]
                s = lax.dot_general(q, k, NT_DIMS, preferred_element_type=F32) * scale + bm[:, joff:]
                m = jnp.max(s, axis=-1, keepdims=True)
                p = jnp.exp(s - m)
                den = jnp.sum(p, axis=-1, keepdims=True)
                o = jnp.dot(p.astype(BF16), v, preferred_element_type=F32) / den
                o_ref[q0:q0 + QB_A, lo:lo + HEAD_DIM] = o.astype(BF16)

    qs, ks_, vs = _head_specs(S, HP)
    return pl.pallas_call(
        body, name="attn_a_fwd", grid=(B, HP),
        in_specs=[qs, ks_, vs, pl.BlockSpec((None, 2, VR_W), lambda b, hp: (hp, 0, 0))],
        out_specs=pl.BlockSpec((S, 2 * HEAD_DIM), lambda b, hp: (b, hp)),
        out_shape=jax.ShapeDtypeStruct((B * S, D), BF16),
        compiler_params=_params(("parallel", "parallel")),
    )(qkv, qkv, qkv, vr)


def attn_a_bwd(qkv, vr, do, B, S):
    D = qkv.shape[1] // 3
    HP = D // (2 * HEAD_DIM)
    scale = HEAD_DIM ** -0.5
    blocks = _a_blocks(S)

    def body(q_ref, k_ref, v_ref, vr_ref, do_ref, dq_ref, dk_ref, dv_ref, dvr_ref,
             dk_acc, dv_acc, db_acc):
        dk_acc[...] = jnp.zeros_like(dk_acc)
        dv_acc[...] = jnp.zeros_like(dv_acc)
        db_acc[...] = jnp.zeros_like(db_acc)
        for h in range(2):
            lo = h * HEAD_DIM
            bm = _a_bias_mask(vr_ref[h:h + 1, :])
            for (q0, ks, ke, joff) in blocks:
                q = q_ref[q0:q0 + QB_A, lo:lo + HEAD_DIM]
                k = k_ref[ks:ke, lo:lo + HEAD_DIM]
                v = v_ref[ks:ke, lo:lo + HEAD_DIM]
                dob = do_ref[q0:q0 + QB_A, lo:lo + HEAD_DIM]
                s = lax.dot_general(q, k, NT_DIMS, preferred_element_type=F32) * scale + bm[:, joff:]
                m = jnp.max(s, axis=-1, keepdims=True)
                e = jnp.exp(s - m)
                p = e / jnp.sum(e, axis=-1, keepdims=True)
                dp = lax.dot_general(dob, v, NT_DIMS, preferred_element_type=F32)
                ds = p * (dp - jnp.sum(p * dp, axis=-1, keepdims=True))
                dsb = ds.astype(BF16)
                dq = jnp.dot(dsb, k, preferred_element_type=F32) * scale
                dq_ref[q0:q0 + QB_A, lo:lo + HEAD_DIM] = dq.astype(BF16)
                dk_acc[ks:ke, lo:lo + HEAD_DIM] += lax.dot_general(
                    dsb, q, TN_DIMS, preferred_element_type=F32) * scale
                dv_acc[ks:ke, lo:lo + HEAD_DIM] += lax.dot_general(
                    p.astype(BF16), dob, TN_DIMS, preferred_element_type=F32)
                db_acc[h, :, joff:KW_A] += ds
            dvr_ref[h:h + 1, :] = _a_diag_sums(db_acc, h)
        dk_ref[...] = dk_acc[...].astype(BF16)
        dv_ref[...] = dv_acc[...].astype(BF16)

    qs, ks_, vs = _head_specs(S, HP)
    hd = pl.BlockSpec((S, 2 * HEAD_DIM), lambda b, hp: (b, hp))
    osh = jax.ShapeDtypeStruct((B * S, D), BF16)
    return pl.pallas_call(
        body, name="attn_a_bwd", grid=(B, HP),
        in_specs=[qs, ks_, vs, pl.BlockSpec((None, 2, VR_W), lambda b, hp: (hp, 0, 0)), hd],
        out_specs=[hd, hd, hd, pl.BlockSpec((None, None, 2, VR_W), lambda b, hp: (b, hp, 0, 0))],
        out_shape=[osh, osh, osh, jax.ShapeDtypeStruct((B, HP, 2, VR_W), F32)],
        scratch_shapes=[pltpu.VMEM((S, 2 * HEAD_DIM), F32), pltpu.VMEM((S, 2 * HEAD_DIM), F32),
                        pltpu.VMEM((2, QB_A, VR_W), F32)],
        compiler_params=_params(("parallel", "parallel")),
    )(qkv, qkv, qkv, vr, do)


def _tri(cmp):
    j = lax.broadcasted_iota(jnp.int32, (SB_TILE, SB_TILE), 0)
    s = lax.broadcasted_iota(jnp.int32, (SB_TILE, SB_TILE), 1)
    return cmp(j, s).astype(BF16)


def _cumsum_mm(x, tri):
    hi = x.astype(BF16)
    mid = (x - hi.astype(F32)).astype(BF16)
    return jnp.dot(hi, tri, preferred_element_type=F32) + jnp.dot(mid, tri, preferred_element_type=F32)


def _sb_logs(q, k, scale, diagonal):
    z = lax.dot_general(q, k, NT_DIMS, preferred_element_type=F32) * scale
    log_b = jnp.minimum(z, 0.0) - jnp.log(1.0 + jnp.exp(-jnp.abs(z)))
    log_1mb = log_b - z
    if not diagonal:
        return log_b, log_1mb, None
    row = lax.broadcasted_iota(jnp.int32, (SB_TILE, SB_TILE), 0)
    col = lax.broadcasted_iota(jnp.int32, (SB_TILE, SB_TILE), 1)
    causal = col < row
    return log_b, jnp.where(causal, log_1mb, 0.0), causal


def attn_b_fwd(qkv, B, S):
    D = qkv.shape[1] // 3
    HP = D // (2 * HEAD_DIM)
    scale = HEAD_DIM ** -0.5
    nb = S // SB_TILE

    def body(q_ref, k_ref, v_ref, o_ref, nt_ref):
        tri = _tri(lambda j, s: j > s)
        heads = [slice(h * HEAD_DIM, (h + 1) * HEAD_DIM) for h in range(2)]

        def q_block(qb, n_pairs, parity):
            q0 = pl.multiple_of(qb * SB_TILE, SB_TILE)
            rows = pl.ds(q0, SB_TILE)
            qs = [q_ref[rows, hs] for hs in heads]

            def step(blocks, state):
                chains = [(h, kb, diagonal, pl.ds(pl.multiple_of(kb * SB_TILE, SB_TILE), SB_TILE))
                          for h in range(2) for kb, diagonal in blocks]
                logs = [_sb_logs(qs[h], k_ref[keys, heads[h]], scale, diagonal)
                        for h, _, diagonal, keys in chains]
                sums = [_cumsum_mm(log_1mb, tri) for _, log_1mb, _ in logs]
                rights = [state[0][0], state[1][0]]
                accs = [state[0][1], state[1][1]]
                for (h, _, diagonal, keys), (log_b, log_1mb, causal), csum in zip(chains, logs, sums):
                    a = jnp.exp(log_b + csum + rights[h])
                    if diagonal:
                        a = jnp.where(causal, a, 0.0)
                    accs[h] = accs[h] + jnp.dot(a.astype(BF16), v_ref[keys, heads[h]],
                                                preferred_element_type=F32)
                    rights[h] = rights[h] + jnp.sum(log_1mb, axis=-1, keepdims=True)
                return ((rights[0], accs[0]), (rights[1], accs[1]))

            zero = (jnp.zeros((SB_TILE, 1), F32), jnp.zeros((SB_TILE, HEAD_DIM), F32))
            first = [(qb, True)] + ([(qb - 1, False)] if parity else [])
            top = qb - len(first)
            state = lax.fori_loop(
                0, n_pairs, lambda t, st: step([(top - 2 * t, False), (top - 2 * t - 1, False)], st),
                step(first, (zero, zero)))
            for hs, (right, acc) in zip(heads, state):
                o_ref[rows, hs] = acc.astype(BF16)
                nt_ref[rows, hs] = jnp.broadcast_to(right, (SB_TILE, HEAD_DIM))

        def q_pair_loop(j, carry):
            q_block(2 * j, j, 0)
            q_block(2 * j + 1, j, 1)
            return carry

        lax.fori_loop(0, nb // 2, q_pair_loop, 0)

    qs, ks_, vs = _head_specs(S, HP)
    hd = pl.BlockSpec((S, 2 * HEAD_DIM), lambda b, hp: (b, hp))
    return pl.pallas_call(
        body, name="attn_b_fwd", grid=(B, HP),
        in_specs=[qs, ks_, vs], out_specs=[hd, hd],
        out_shape=[jax.ShapeDtypeStruct((B * S, D), BF16), jax.ShapeDtypeStruct((B * S, D), F32)],
        compiler_params=_params(("parallel", "parallel")),
    )(qkv, qkv, qkv)


def attn_b_bwd(qkv, do, ntot, B, S):
    D = qkv.shape[1] // 3
    HP = D // (2 * HEAD_DIM)
    scale = HEAD_DIM ** -0.5
    nb = S // SB_TILE

    def body(q_ref, k_ref, v_ref, do_ref, nt_ref, dq_ref, dk_ref, dv_ref, dkt_acc, dvt_acc):
        tri_incl = _tri(lambda j, s: j <= s)
        tri_excl = _tri(lambda j, s: j < s)
        heads = [slice(h * HEAD_DIM, (h + 1) * HEAD_DIM) for h in range(2)]
        dkt_acc[...] = jnp.zeros_like(dkt_acc)
        dvt_acc[...] = jnp.zeros_like(dvt_acc)

        def q_block(qb, n_pairs, parity):
            q0 = pl.multiple_of(qb * SB_TILE, SB_TILE)
            rows = pl.ds(q0, SB_TILE)
            qt_pair = q_ref[rows, :].astype(F32).T.astype(BF16)
            dot_pair = do_ref[rows, :].astype(F32).T.astype(BF16)
            per_head = [(hs, q_ref[rows, hs], do_ref[rows, hs], qt_pair[hs, :], dot_pair[hs, :],
                         nt_ref[rows, hs.start:hs.start + 1]) for hs in heads]

            def step(blocks, state):
                chains = [(h, kb, diagonal, pl.ds(pl.multiple_of(kb * SB_TILE, SB_TILE), SB_TILE))
                          for h in range(2) for kb, diagonal in blocks]
                ks = [k_ref[keys, per_head[h][0]] for h, _, _, keys in chains]
                logs = [_sb_logs(per_head[h][1], k, scale, diagonal)
                        for (h, _, diagonal, _), k in zip(chains, ks)]
                das = [lax.dot_general(per_head[h][2], v_ref[keys, per_head[h][0]], NT_DIMS,
                                       preferred_element_type=F32) for h, _, _, keys in chains]
                sums = [_cumsum_mm(log_1mb, tri_incl) for _, log_1mb, _ in logs]
                left_n = [state[0][0], state[1][0]]
                left_d = [state[0][1], state[1][1]]
                dq_acc = [state[0][2], state[1][2]]
                a_s, dls = [], []
                for (h, _, diagonal, _), (log_b, log_1mb, causal), csum, da in zip(chains, logs, sums, das):
                    a = jnp.exp(log_b + (per_head[h][5] - left_n[h]) - csum)
                    if diagonal:
                        a = jnp.where(causal, a, 0.0)
                    a_s.append(a)
                    dls.append(a * da)
                    left_n[h] = left_n[h] + jnp.sum(log_1mb, axis=-1, keepdims=True)
                dsums = [_cumsum_mm(dl, tri_excl) for dl in dls]
                dzbs = []
                for (h, _, diagonal, _), (log_b, log_1mb, causal), dl, dsum in zip(chains, logs, dls, dsums):
                    dz = dl * jnp.exp(log_1mb) - (left_d[h] + dsum) * jnp.exp(log_b)
                    if diagonal:
                        dz = jnp.where(causal, dz, 0.0)
                    dzbs.append(dz.astype(BF16))
                    left_d[h] = left_d[h] + jnp.sum(dl, axis=-1, keepdims=True)
                for (h, kb, _, _), k, a, dzb in zip(chains, ks, a_s, dzbs):
                    hs, _, _, qt, dot_, _ = per_head[h]
                    dq_acc[h] = dq_acc[h] + jnp.dot(dzb, k, preferred_element_type=F32)
                    dkt_acc[kb, hs, :] += jnp.dot(qt, dzb, preferred_element_type=F32)
                    dvt_acc[kb, hs, :] += jnp.dot(dot_, a.astype(BF16), preferred_element_type=F32)
                return ((left_n[0], left_d[0], dq_acc[0]), (left_n[1], left_d[1], dq_acc[1]))

            zero1 = jnp.zeros((SB_TILE, 1), F32)
            zero = (zero1, zero1, jnp.zeros((SB_TILE, HEAD_DIM), F32))
            state = lax.fori_loop(
                0, n_pairs, lambda t, st: step([(2 * t, False), (2 * t + 1, False)], st), (zero, zero))
            last = ([(qb - 1, False)] if parity else []) + [(qb, True)]
            state = step(last, state)
            for hs, (_, _, dq_acc) in zip(heads, state):
                dq_ref[rows, hs] = (dq_acc * scale).astype(BF16)

        def q_pair_loop(j, carry):
            q_block(2 * j, j, 0)
            q_block(2 * j + 1, j, 1)
            return carry

        lax.fori_loop(0, nb // 2, q_pair_loop, 0)
        for kb in range(nb):
            dk_ref[kb * SB_TILE:(kb + 1) * SB_TILE, :] = (dkt_acc[kb].T * scale).astype(BF16)
            dv_ref[kb * SB_TILE:(kb + 1) * SB_TILE, :] = dvt_acc[kb].T.astype(BF16)

    qs, ks_, vs = _head_specs(S, HP)
    hd = pl.BlockSpec((S, 2 * HEAD_DIM), lambda b, hp: (b, hp))
    osh = jax.ShapeDtypeStruct((B * S, D), BF16)
    acc = pltpu.VMEM((nb, 2 * HEAD_DIM, SB_TILE), F32)
    return pl.pallas_call(
        body, name="attn_b_bwd", grid=(B, HP),
        in_specs=[qs, ks_, vs, hd, hd], out_specs=[hd, hd, hd], out_shape=[osh, osh, osh],
        scratch_shapes=[acc, acc],
        compiler_params=_params(("parallel", "parallel")),
    )(qkv, qkv, qkv, do, ntot)


def _place():
    x, y, c = lax.axis_index("x"), lax.axis_index("y"), lax.axis_index("c")
    chips = [(1 - x, y), (x, 1 - y), (1 - x, 1 - y)]
    return x, y, c, 2 * x + y, chips


def _remote(src, dst, send_sem, recv_sem, dev):
    return pltpu.make_async_remote_copy(src_ref=src, dst_ref=dst, send_sem=send_sem, recv_sem=recv_sem,
                                        device_id=dev, device_id_type=MESH)


def gather_weights(shards):
    n = len(shards)

    def body(*refs):
        ins, outs = refs[:n], refs[n:2 * n]
        send1, recv1, send2, recv2, lsem = refs[2 * n:]
        x, y, c, me, chips = _place()
        local, first = [], []
        for f in range(n):
            hl = shards[f].shape[0] // 2
            cp = pltpu.make_async_copy(ins[f], outs[f].at[me], lsem.at[f])
            cp.start()
            local.append(cp)
            for j, (qx, qy) in enumerate(chips):
                half = pl.ds(c * hl, hl)
                cp = _remote(ins[f].at[half], outs[f].at[me, half],
                             send1.at[3 * f + j], recv1.at[3 * f + j], (qx, qy, c))
                cp.start()
                first.append(cp)
        passed = []
        for f in range(n):
            hl = shards[f].shape[0] // 2
            for j, (qx, qy) in enumerate(chips):
                slab = outs[f].at[2 * qx + qy, pl.ds(c * hl, hl)]
                _remote(slab, slab, send1.at[3 * f + j], recv1.at[3 * f + j], (x, y, c)).wait_recv()
                cp = _remote(slab, slab, send2.at[3 * f + j], recv2.at[3 * f + j], (x, y, 1 - c))
                cp.start()
                passed.append(cp)
        for f in range(n):
            hl = shards[f].shape[0] // 2
            for j, (qx, qy) in enumerate(chips):
                slab = outs[f].at[2 * qx + qy, pl.ds((1 - c) * hl, hl)]
                _remote(slab, slab, send2.at[3 * f + j], recv2.at[3 * f + j], (x, y, c)).wait_recv()
        for cp in first + passed:
            cp.wait_send()
        for cp in local:
            cp.wait()

    sems = pltpu.SemaphoreType.DMA((3 * n,))
    return pl.pallas_call(
        body, name="gather_weights",
        in_specs=[ANY] * n, out_specs=[ANY] * n,
        out_shape=[jax.ShapeDtypeStruct((N_CHIPS,) + s.shape, s.dtype) for s in shards],
        scratch_shapes=[sems, sems, sems, sems, pltpu.SemaphoreType.DMA((n,))],
        compiler_params=pltpu.CompilerParams(has_side_effects=True),
    )(*shards)


HBM = pl.BlockSpec(memory_space=pltpu.HBM)
SEM = pl.BlockSpec(memory_space=pltpu.SEMAPHORE)
EFFECT = pltpu.SideEffectType.DATAFLOW_SIDE_EFFECTING


def _in_hbm(a):
    return pltpu.with_memory_space_constraint(a, pltpu.HBM)


def _row_half(ref, _unused, which):
    hr = ref.shape[-2] // 2
    idx = [pl.ds(0, d) for d in ref.shape[:-2]] + [pl.ds(which * hr, hr), pl.ds(0, ref.shape[-1])]
    return ref.at[tuple(idx)]


def gather_start(layers):
    flat = [s for lay in layers for s in lay]
    counts = [len(lay) for lay in layers]
    n, nl = len(flat), len(layers)

    def body(*refs):
        ins, lands = refs[:n], refs[n:2 * n]
        send, recv = refs[2 * n:2 * n + nl], refs[2 * n + nl:2 * n + 2 * nl]
        token = refs[-1]
        x, y, c, me, chips = _place()
        f = 0
        for li, cnt in enumerate(counts):
            for k in range(cnt):
                for j, (qx, qy) in enumerate(chips):
                    _remote(_row_half(ins[f], 0, c), _row_half(lands[f].at[me], 0, c),
                            send[li].at[3 * k + j], recv[li].at[3 * k + j], (qx, qy, c)).start()
                f += 1
        token[...] = jnp.zeros_like(token)

    sem_shapes = [pltpu.SemaphoreType.DMA((3 * cnt,)) for cnt in counts]
    land_shapes = [(N_CHIPS,) + s.shape for s in flat]
    res = pl.pallas_call(
        body, name="gather_start",
        out_shape=(*sem_shapes, *sem_shapes,
                   *[pltpu.HBM(s.shape, s.dtype) for s in flat],
                   *[pltpu.HBM(shp, s.dtype) for shp, s in zip(land_shapes, flat)],
                   jax.ShapeDtypeStruct((8, 128), F32)),
        in_specs=[HBM] * (2 * n),
        out_specs=(*[SEM] * (2 * nl), *[HBM] * (2 * n), pl.BlockSpec(memory_space=pltpu.VMEM)),
        input_output_aliases={k: 2 * nl + k for k in range(2 * n)},
        compiler_params=pltpu.CompilerParams(has_side_effects=EFFECT),
    )(*[_in_hbm(s) for s in flat],
      *[_in_hbm(lax.empty(shp, s.dtype)) for shp, s in zip(land_shapes, flat)])
    send, recv = res[:nl], res[nl:2 * nl]
    thru, lands, token = res[2 * nl:2 * nl + n], res[2 * nl + n:2 * nl + 2 * n], res[-1]
    out, f = [], 0
    for li, cnt in enumerate(counts):
        out.append((send[li], recv[li], list(thru[f:f + cnt]), list(lands[f:f + cnt])))
        f += cnt
    return out, token


def gather_wait(li, send, recv, shards, lands, after):
    m = len(shards)

    def body(*refs):
        ins, lnd = refs[:m], refs[m:2 * m]
        snd, rcv = refs[2 * m], refs[2 * m + 1]
        x, y, c, me, chips = _place()
        for k in range(m):
            for j, (qx, qy) in enumerate(chips):
                cp = _remote(_row_half(ins[k], 0, c), _row_half(lnd[k].at[2 * qx + qy], 0, c),
                             snd.at[3 * k + j], rcv.at[3 * k + j], (qx, qy, c))
                cp.wait_send()
                cp.wait_recv()

    res = pl.pallas_call(
        body, name=f"gather_wait_{li}",
        out_shape=(*[pltpu.HBM(s.shape, s.dtype) for s in shards],
                   *[pltpu.HBM(s.shape, s.dtype) for s in lands]),
        in_specs=[HBM] * (2 * m) + [SEM, SEM, ANY], out_specs=[HBM] * (2 * m),
        input_output_aliases={k: k for k in range(2 * m)},
        compiler_params=pltpu.CompilerParams(has_side_effects=EFFECT),
    )(*shards, *lands, send, recv, after)
    return list(res[:m]), list(res[m:])


def gather_forward(li, shards, lands):
    m = len(shards)

    def body(*refs):
        ins, outs = refs[:m], refs[2 * m:3 * m]
        send, recv, lsem = refs[3 * m:]
        x, y, c, me, chips = _place()
        local, passed = [], []
        for k in range(m):
            cp = pltpu.make_async_copy(ins[k], outs[k].at[me], lsem.at[k])
            cp.start()
            local.append(cp)
            for j, (qx, qy) in enumerate(chips):
                slab = _row_half(outs[k].at[2 * qx + qy], 0, c)
                cp = _remote(slab, slab, send.at[3 * k + j], recv.at[3 * k + j], (x, y, 1 - c))
                cp.start()
                passed.append(cp)
        for k in range(m):
            for j, (qx, qy) in enumerate(chips):
                slab = _row_half(outs[k].at[2 * qx + qy], 0, 1 - c)
                _remote(slab, slab, send.at[3 * k + j], recv.at[3 * k + j], (x, y, c)).wait_recv()
        for cp in passed:
            cp.wait_send()
        for cp in local:
            cp.wait()

    sems = pltpu.SemaphoreType.DMA((3 * m,))
    return pl.pallas_call(
        body, name=f"gather_forward_{li}",
        in_specs=[ANY] * (2 * m), out_specs=[ANY] * m,
        out_shape=[jax.ShapeDtypeStruct(s.shape, s.dtype) for s in lands],
        input_output_aliases={m + k: k for k in range(m)},
        scratch_shapes=[sems, sems, pltpu.SemaphoreType.DMA((m,))],
        compiler_params=pltpu.CompilerParams(has_side_effects=True),
    )(*shards, *lands)


def pair_exchange_rows(li, grads):
    n = len(grads)

    def body(*refs):
        ins, outs = refs[:n], refs[n:2 * n]
        send, recv = refs[2 * n:]
        x, y, c, _, _ = _place()
        cps = []
        for k in range(n):
            cp = _remote(_row_half(ins[k], 0, 1 - c), outs[k], send.at[k], recv.at[k], (x, y, 1 - c))
            cp.start()
            cps.append(cp)
        for cp in cps:
            cp.wait()

    sems = pltpu.SemaphoreType.DMA((n,))
    return pl.pallas_call(
        body, name=f"pair_exchange_{li}",
        in_specs=[ANY] * n, out_specs=[ANY] * n,
        out_shape=[jax.ShapeDtypeStruct(g.shape[:-2] + (g.shape[-2] // 2, g.shape[-1]), g.dtype)
                   for g in grads],
        scratch_shapes=[sems, sems],
        compiler_params=pltpu.CompilerParams(has_side_effects=True),
    )(*grads)


def pair_add_rows(ids, grad, recv):
    L, P, hr, C = recv.shape

    def body(ids_ref, a_ref, b_ref, o_ref):
        o_ref[...] = (a_ref[...].astype(F32) + b_ref[...].astype(F32)).astype(o_ref.dtype)

    blk = (None, None, hr, C)
    return pl.pallas_call(
        body, name="pair_add",
        grid_spec=pltpu.PrefetchScalarGridSpec(
            num_scalar_prefetch=1, grid=(L, P),
            in_specs=[pl.BlockSpec(blk, lambda l, p, ids: (l, p, ids[0], 0)),
                      pl.BlockSpec(blk, lambda l, p, ids: (l, p, 0, 0))],
            out_specs=pl.BlockSpec(blk, lambda l, p, ids: (l, p, 0, 0))),
        out_shape=jax.ShapeDtypeStruct(recv.shape, recv.dtype),
        compiler_params=_params(("parallel", "parallel")),
    )(ids, grad, recv)


def reduce_start(li, parts):
    m = len(parts)

    def body(*refs):
        ins, lands = refs[:m], refs[m:2 * m]
        send, recv = refs[2 * m], refs[2 * m + 1]
        token = refs[-1]
        x, y, c, me, chips = _place()
        for k in range(m):
            rows = pl.ds(0, parts[k].shape[0])
            for j, (qx, qy) in enumerate(chips):
                _remote(ins[k].at[rows, 2 * qx + qy], lands[k].at[rows, me],
                        send.at[3 * k + j], recv.at[3 * k + j], (qx, qy, c)).start()
        token[...] = jnp.zeros_like(token)

    sems = pltpu.SemaphoreType.DMA((3 * m,))
    res = pl.pallas_call(
        body, name=f"reduce_start_{li}",
        out_shape=(sems, sems, *[pltpu.HBM(s.shape, s.dtype) for s in parts],
                   *[pltpu.HBM(s.shape, s.dtype) for s in parts], jax.ShapeDtypeStruct((8, 128), F32)),
        in_specs=[HBM] * (2 * m),
        out_specs=(SEM, SEM, *[HBM] * (2 * m), pl.BlockSpec(memory_space=pltpu.VMEM)),
        input_output_aliases={k: 2 + k for k in range(2 * m)},
        compiler_params=pltpu.CompilerParams(has_side_effects=EFFECT),
    )(*[_in_hbm(s) for s in parts], *[_in_hbm(lax.empty(s.shape, s.dtype)) for s in parts])
    return res[0], res[1], list(res[2:2 + m]), list(res[2 + m:2 + 2 * m]), res[-1]


def reduce_wait(li, send, recv, parts, lands, after):
    m = len(parts)

    def body(*refs):
        ins, lnd = refs[:m], refs[m:2 * m]
        snd, rcv = refs[2 * m], refs[2 * m + 1]
        x, y, c, me, chips = _place()
        for k in range(m):
            rows = pl.ds(0, parts[k].shape[0])
            for j, (qx, qy) in enumerate(chips):
                cp = _remote(ins[k].at[rows, 2 * qx + qy], lnd[k].at[rows, 2 * qx + qy],
                             snd.at[3 * k + j], rcv.at[3 * k + j], (qx, qy, c))
                cp.wait_send()
                cp.wait_recv()

    res = pl.pallas_call(
        body, name=f"reduce_wait_{li}",
        out_shape=(*[pltpu.HBM(s.shape, s.dtype) for s in parts],
                   *[pltpu.HBM(s.shape, s.dtype) for s in lands]),
        in_specs=[HBM] * (2 * m) + [SEM, SEM, ANY], out_specs=[HBM] * (2 * m),
        input_output_aliases={k: k for k in range(2 * m)},
        compiler_params=pltpu.CompilerParams(has_side_effects=EFFECT),
    )(*parts, *lands, send, recv, after)
    return list(res[:m]), list(res[m:])


def chip_sum_rows(ids, land, part, gfull, l0):
    L, P, hr, C = land.shape

    def body(ids_ref, land_ref, part_ref, _g, o_ref):
        tot = None
        for q in range(P):
            term = jnp.where(ids_ref[1] == q, part_ref[...], land_ref[q]).astype(F32)
            tot = term if tot is None else tot + term
        o_ref[...] = tot

    return pl.pallas_call(
        body, name="chip_sum",
        grid_spec=pltpu.PrefetchScalarGridSpec(
            num_scalar_prefetch=1, grid=(L,),
            in_specs=[pl.BlockSpec((None, P, hr, C), lambda l, ids: (l, 0, 0, 0)),
                      pl.BlockSpec((None, None, hr, C), lambda l, ids: (l, ids[1], 0, 0)), ANY],
            out_specs=pl.BlockSpec((None, hr, C), lambda l, ids: (l0 + l, ids[0], 0))),
        out_shape=jax.ShapeDtypeStruct(gfull.shape, gfull.dtype),
        input_output_aliases={3: 0},
        compiler_params=_params(("arbitrary",)),
    )(ids, land, part, gfull)


def half_swap_rows(grads):
    n = len(grads)

    def body(*refs):
        outs = refs[n:2 * n]
        send, recv = refs[2 * n:]
        x, y, c, _, _ = _place()
        cps = []
        for k in range(n):
            mine = _row_half(outs[k], 0, c)
            cp = _remote(mine, mine, send.at[k], recv.at[k], (x, y, 1 - c))
            cp.start()
            cps.append(cp)
        for k in range(n):
            theirs = _row_half(outs[k], 0, 1 - c)
            _remote(theirs, theirs, send.at[k], recv.at[k], (x, y, c)).wait_recv()
        for cp in cps:
            cp.wait_send()

    sems = pltpu.SemaphoreType.DMA((n,))
    return pl.pallas_call(
        body, name="half_swap",
        in_specs=[ANY] * n, out_specs=[ANY] * n,
        out_shape=[jax.ShapeDtypeStruct(g.shape, g.dtype) for g in grads],
        input_output_aliases={k: k for k in range(n)},
        scratch_shapes=[sems, sems],
        compiler_params=pltpu.CompilerParams(has_side_effects=True),
    )(*grads)


def pair_exchange(grads):
    n = len(grads)

    def body(*refs):
        ins, outs = refs[:n], refs[n:2 * n]
        send, recv = refs[2 * n:]
        x, y, c, _, _ = _place()
        cps = []
        for f in range(n):
            hl = grads[f].shape[0] // 2
            cp = _remote(ins[f].at[pl.ds((1 - c) * hl, hl)], outs[f], send.at[f], recv.at[f], (x, y, 1 - c))
            cp.start()
            cps.append(cp)
        for cp in cps:
            cp.wait()

    sems = pltpu.SemaphoreType.DMA((n,))
    return pl.pallas_call(
        body, name="pair_exchange",
        in_specs=[ANY] * n, out_specs=[ANY] * n,
        out_shape=[jax.ShapeDtypeStruct((g.shape[0] // 2,) + g.shape[1:], g.dtype) for g in grads],
        scratch_shapes=[sems, sems],
        compiler_params=pltpu.CompilerParams(has_side_effects=True),
    )(*grads)


def chip_exchange(parts):
    n = len(parts)

    def body(*refs):
        ins, outs = refs[:n], refs[n:2 * n]
        send, recv, lsem = refs[2 * n:]
        x, y, c, me, chips = _place()
        local, sent = [], []
        for f in range(n):
            hl = parts[f].shape[0]
            rows = pl.ds(0, hl)
            cp = pltpu.make_async_copy(ins[f].at[rows, me], outs[f].at[rows, me], lsem.at[f])
            cp.start()
            local.append(cp)
            for j, (qx, qy) in enumerate(chips):
                cp = _remote(ins[f].at[rows, 2 * qx + qy], outs[f].at[rows, me],
                             send.at[3 * f + j], recv.at[3 * f + j], (qx, qy, c))
                cp.start()
                sent.append(cp)
        for f in range(n):
            rows = pl.ds(0, parts[f].shape[0])
            for j, (qx, qy) in enumerate(chips):
                slab = outs[f].at[rows, 2 * qx + qy]
                _remote(slab, slab, send.at[3 * f + j], recv.at[3 * f + j], (x, y, c)).wait_recv()
        for cp in sent:
            cp.wait_send()
        for cp in local:
            cp.wait()

    sems = pltpu.SemaphoreType.DMA((3 * n,))
    return pl.pallas_call(
        body, name="chip_exchange",
        in_specs=[ANY] * n, out_specs=[ANY] * n,
        out_shape=[jax.ShapeDtypeStruct(s.shape, s.dtype) for s in parts],
        scratch_shapes=[sems, sems, pltpu.SemaphoreType.DMA((n,))],
        compiler_params=pltpu.CompilerParams(has_side_effects=True),
    )(*parts)


def half_swap(grads):
    n = len(grads)

    def body(*refs):
        ins, outs = refs[:n], refs[n:2 * n]
        send, recv = refs[2 * n:]
        x, y, c, _, _ = _place()
        cps = []
        for f in range(n):
            hl = grads[f].shape[0] // 2
            mine = pl.ds(c * hl, hl)
            cp = _remote(outs[f].at[mine], outs[f].at[mine], send.at[f], recv.at[f], (x, y, 1 - c))
            cp.start()
            cps.append(cp)
        for f in range(n):
            hl = grads[f].shape[0] // 2
            theirs = outs[f].at[pl.ds((1 - c) * hl, hl)]
            _remote(theirs, theirs, send.at[f], recv.at[f], (x, y, c)).wait_recv()
        for cp in cps:
            cp.wait_send()

    sems = pltpu.SemaphoreType.DMA((n,))
    return pl.pallas_call(
        body, name="half_swap",
        in_specs=[ANY] * n, out_specs=[ANY] * n,
        out_shape=[jax.ShapeDtypeStruct(g.shape, g.dtype) for g in grads],
        input_output_aliases={f: f for f in range(n)},
        scratch_shapes=[sems, sems],
        compiler_params=pltpu.CompilerParams(has_side_effects=True),
    )(*grads)


def all_sum_small(v):
    R = v.shape[0]

    def body(v_ref, o_ref, land, send, recv):
        x, y, c, _, _ = _place()
        me = 4 * x + 2 * y + c
        land[me] = v_ref[...]
        peers = [(px, py, pc) for px in range(2) for py in range(2) for pc in range(2)]
        cps = []
        for k in range(1, 8):
            dev = (x ^ (k >> 2), y ^ ((k >> 1) & 1), c ^ (k & 1))
            cp = _remote(v_ref, land.at[me], send.at[k - 1], recv.at[k - 1], dev)
            cp.start()
            cps.append(cp)
        for k in range(1, 8):
            src = 4 * (x ^ (k >> 2)) + 2 * (y ^ ((k >> 1) & 1)) + (c ^ (k & 1))
            _remote(v_ref, land.at[src], send.at[k - 1], recv.at[k - 1], (x, y, c)).wait_recv()
        for cp in cps:
            cp.wait_send()
        tot = land[0]
        for d in range(1, len(peers)):
            tot = tot + land[d]
        o_ref[...] = tot

    sems = pltpu.SemaphoreType.DMA((7,))
    vm = pl.BlockSpec(memory_space=pltpu.VMEM)
    return pl.pallas_call(
        body, name="all_sum_small", in_specs=[vm], out_specs=vm,
        out_shape=jax.ShapeDtypeStruct(v.shape, F32),
        scratch_shapes=[pltpu.VMEM((8, R, 128), F32), sems, sems],
        compiler_params=pltpu.CompilerParams(has_side_effects=True),
    )(v)


def _row_tile(R):
    for t in (512, 256, 128, 64, 32, 16):
        if R % t == 0:
            return t
    return R


def pair_add(cidx, grad, recv):
    hl, P, R, C = recv.shape
    tr = _row_tile(R)

    def body(c_ref, a_ref, b_ref, o_ref):
        o_ref[...] = (a_ref[...].astype(F32) + b_ref[...].astype(F32)).astype(o_ref.dtype)

    blk = (None, None, tr, C)
    return pl.pallas_call(
        body, name="pair_add",
        grid_spec=pltpu.PrefetchScalarGridSpec(
            num_scalar_prefetch=1, grid=(hl, P, R // tr),
            in_specs=[pl.BlockSpec(blk, lambda l, p, r, c: (c[0] * hl + l, p, r, 0)),
                      pl.BlockSpec(blk, lambda l, p, r, c: (l, p, r, 0))],
            out_specs=pl.BlockSpec(blk, lambda l, p, r, c: (l, p, r, 0))),
        out_shape=jax.ShapeDtypeStruct(recv.shape, recv.dtype),
        compiler_params=_params(("parallel", "parallel", "parallel")),
    )(cidx, grad, recv)


def chip_sum(cidx, land, L):
    hl, P, R, C = land.shape
    tr = _row_tile(R)

    def body(c_ref, a_ref, o_ref):
        tot = a_ref[0].astype(F32)
        for q in range(1, P):
            tot = tot + a_ref[q].astype(F32)
        o_ref[...] = tot

    return pl.pallas_call(
        body, name="chip_sum",
        grid_spec=pltpu.PrefetchScalarGridSpec(
            num_scalar_prefetch=1, grid=(hl, R // tr),
            in_specs=[pl.BlockSpec((None, P, tr, C), lambda l, r, c: (l, 0, r, 0))],
            out_specs=pl.BlockSpec((None, tr, C), lambda l, r, c: (c[0] * hl + l, r, 0))),
        out_shape=jax.ShapeDtypeStruct((L, R, C), F32),
        compiler_params=_params(("parallel", "parallel")),
    )(cidx, land)


def adamw(w, g, m, v):
    L, R, C = w.shape
    tr = _row_tile(R)
    c1 = 1.0 / (1.0 - ADAM_B1 ** ADAM_STEP)
    c2 = 1.0 / (1.0 - ADAM_B2 ** ADAM_STEP)

    def body(w_ref, g_ref, m_ref, v_ref, d_ref, nm_ref, nv_ref):
        gv = g_ref[...]
        nm = ADAM_B1 * m_ref[...] + (1.0 - ADAM_B1) * gv
        nv = ADAM_B2 * v_ref[...] + (1.0 - ADAM_B2) * (gv * gv)
        nm_ref[...] = nm
        nv_ref[...] = nv
        d_ref[...] = -ADAM_LR * ((nm * c1) / (jnp.sqrt(nv * c2) + ADAM_EPS) + ADAM_WD * w_ref[...])

    blk = pl.BlockSpec((None, tr, C), lambda l, r: (l, r, 0))
    osh = jax.ShapeDtypeStruct((L, R, C), F32)
    return pl.pallas_call(
        body, name="adamw", grid=(L, R // tr),
        in_specs=[blk, blk, blk, blk], out_specs=[blk, blk, blk], out_shape=[osh, osh, osh],
        compiler_params=_params(("parallel", "parallel")),
    )(w, g, m, v)


def kernel(x, w_qkv_a, w_o_a, rel_bias, w_qkv_b, w_o_b, ffn_w_gate, ffn_w_up, ffn_w_down, ln_g, ln_b, loss_target, m_w_qkv_a, m_w_o_a, m_rel_bias, m_w_qkv_b, m_w_o_b, m_ffn_w_gate, m_ffn_w_up, m_ffn_w_down, m_ln_g, m_ln_b, v_w_qkv_a, v_w_o_a, v_rel_bias, v_w_qkv_b, v_w_o_b, v_ffn_w_gate, v_ffn_w_up, v_ffn_w_down, v_ln_g, v_ln_b):
    B, S, D = x.shape
    M = B * S
    depth = ffn_w_gate.shape[0]
    n_ffn = 2 * depth
    H = D // HEAD_DIM
    HP = H // 2
    Fs = ffn_w_gate.shape[-1]
    alpha = (2.0 * depth) ** 0.25
    assert S % QB_A == 0 and S % SB_TILE == 0 and rel_bias.shape == (N_REL, H)

    def mixer_weights(i):
        wq, wo = (w_qkv_a, w_o_a) if i % 2 == 0 else (w_qkv_b, w_o_b)
        return wq[i // 2:i // 2 + 1], wo[i // 2:i // 2 + 1]

    layer_shards = [[w.astype(BF16) for w in (*mixer_weights(i), ffn_w_gate[i], ffn_w_up[i], ffn_w_down[i])]
                    for i in range(depth)]
    in_flight, _ = gather_start(layer_shards)
    lng_p, lnb_p = gather_weights([ln_g, ln_b])
    lng = jnp.moveaxis(lng_p, 0, 2).reshape(depth, 3, 1, D)
    lnb = jnp.moveaxis(lnb_p, 0, 2).reshape(depth, 3, 1, D)

    tab_t = jnp.pad(rel_bias.T, ((0, 0), (0, REL_PAD - N_REL)))
    vr = bias_vec(tab_t).reshape(HP, 2, VR_W)

    xf = x.reshape(M, D)
    xb = xf.astype(BF16)
    saved, weights = [], []
    for i in range(depth):
        send, recv, thru, lands = in_flight[i]
        thru, lands = gather_wait(i, send, recv, thru, lands, xf)
        wq, wo, wg, wu, wd = gather_forward(i, thru, lands)
        weights.append((wq, wo, wg, wu, wd))
        for j in range(3):
            gam, bet = lng[i, j], lnb[i, j]
            if j != 1:
                l = 0 if j == 0 else 1
                g, u, h = ffn_up(xb, wg, wu, l)
                xo, xob, xhat, rstd = mm_ln(h, wd, l, xf, gam, bet, alpha, 0.5, True)
                saved.append(("ffn", i, l, xb, g, u, h, xhat, rstd, gam))
            elif i % 2 == 0:
                qkv = qkv_proj(xb, wq, 0)
                o = attn_a_fwd(qkv, vr, B, S)
                xo, xob, xhat, rstd = mm_ln(o, wo, 0, xf, gam, bet, alpha, 1.0, False)
                saved.append(("a", i, 0, xb, qkv, o, None, xhat, rstd, gam))
            else:
                qkv = qkv_proj(xb, wq, 0)
                o, ntot = attn_b_fwd(qkv, B, S)
                xo, xob, xhat, rstd = mm_ln(o, wo, 0, xf, gam, bet, alpha, 1.0, False)
                saved.append(("b", i, 0, xb, qkv, o, ntot, xhat, rstd, gam))
            xf, xb = xo, xob

    dy, loss_part = loss_head(xf, loss_target.reshape(M, D))
    loss = lax.psum(loss_part[0, 0], ("x", "y", "c"))

    Cq, Ro = w_qkv_a.shape[-1], w_o_a.shape[1]
    me = 2 * lax.axis_index("x") + lax.axis_index("y")
    ids = jnp.stack([lax.axis_index("c"), me]).astype(jnp.int32)
    dgam, dbet, dvrs, reducing = [], [], [], {}
    for i in reversed(range(depth)):
        wq, wo, wg, wu, wd = weights[i]
        gq = lax.empty((1, N_CHIPS, D, Cq), BF16)
        go = lax.empty((1, N_CHIPS, Ro, D), BF16)
        ggate = lax.empty((2, N_CHIPS, D, Fs), BF16)
        gup = lax.empty((2, N_CHIPS, D, Fs), BF16)
        gdown = lax.empty((2, N_CHIPS, Fs, D), BF16)
        for rec in reversed(saved[3 * i:3 * i + 3]):
            kind, _, l, xb_in, t1, t2, t3, xhat, rstd, gam = rec
            scale = 0.5 if kind == "ffn" else 1.0
            dpre, dyb, dg_, db_ = ln_bwd(dy, xhat, rstd, gam, scale)
            dgam.append(dg_)
            dbet.append(db_)
            if kind == "ffn":
                g, u, h = t1, t2, t3
                dg, du, gdown, ggate, gup = ffn_bwd(gdown, ggate, gup, l, xb_in, dyb, wd, g, u, h)
                dy = ffn_dx(dg, du, wg, wu, l, dpre, alpha)
            else:
                qkv, o = t1, t2
                do = o_proj_bwd(dyb, wo, l)
                go = dw_o(go, l, o, dyb)
                if kind == "a":
                    dq, dk, dv, dvr = attn_a_bwd(qkv, vr, do, B, S)
                    dvrs.append(dvr.reshape(B, H, VR_W))
                else:
                    dq, dk, dv = attn_b_bwd(qkv, do, t3, B, S)
                dqkv = jnp.concatenate([dq, dk, dv], axis=1)
                gq = dw_qkv(gq, l, xb_in, dqkv)
                dy = qkv_proj_bwd(dqkv, wq, l, dpre, alpha)
        fams = [gq, go, ggate, gup, gdown]
        from_sib = pair_exchange_rows(i, fams)
        parts = [pair_add_rows(ids, g_, r_) for g_, r_ in zip(fams, from_sib)]
        reducing[i] = reduce_start(i, parts)
    grad_x = dy.reshape(B, S, D)

    la, lb = w_qkv_a.shape[0], w_qkv_b.shape[0]
    full = {"qa": lax.empty((la, D, Cq), F32), "oa": lax.empty((la, Ro, D), F32),
            "qb": lax.empty((lb, D, Cq), F32), "ob": lax.empty((lb, Ro, D), F32),
            "gate": lax.empty((n_ffn, D, Fs), F32), "up": lax.empty((n_ffn, D, Fs), F32),
            "down": lax.empty((n_ffn, Fs, D), F32)}
    for i in reversed(range(depth)):
        send, recv, parts, lands, _ = reducing[i]
        parts, lands = reduce_wait(i, send, recv, parts, lands, dy)
        mix = "a" if i % 2 == 0 else "b"
        targets = [("q" + mix, i // 2), ("o" + mix, i // 2), ("gate", 2 * i), ("up", 2 * i), ("down", 2 * i)]
        for (name, l0), part, land in zip(targets, parts, lands):
            full[name] = chip_sum_rows(ids, land, part, full[name], l0)
    g_qa, g_oa, g_qb, g_ob, g_gate, g_up, g_down = half_swap_rows(
        [full[k] for k in ("qa", "oa", "qb", "ob", "gate", "up", "down")])

    d_tab_t = bias_vec_bwd(jnp.concatenate(dvrs, axis=0))
    small = jnp.concatenate([d_tab_t.reshape(-1)] + [p.reshape(-1) for p in dgam[::-1]]
                            + [p.reshape(-1) for p in dbet[::-1]])
    n_small = small.shape[0]
    rows = -(-n_small // (8 * 128)) * 8
    tot = all_sum_small(jnp.pad(small, (0, rows * 128 - n_small)).reshape(rows, 128)).reshape(-1)
    n_tab, n_ln = H * REL_PAD, 3 * depth * D
    g_rel = tot[:n_tab].reshape(H, REL_PAD)[:, :N_REL].T
    ln_cols = D // N_CHIPS

    def ln_shard(flat):
        return lax.dynamic_slice_in_dim(flat.reshape(depth, 3, D), me * ln_cols, ln_cols, axis=2)

    g_lng = ln_shard(tot[n_tab:n_tab + n_ln])
    g_lnb = ln_shard(tot[n_tab + n_ln:n_tab + 2 * n_ln])

    def upd(w, g, m, v):
        shp = w.shape
        w3, m3, v3 = (a.reshape(g.shape) for a in (w, m, v))
        d, nm, nv = adamw(w3, g, m3, v3)
        return g.reshape(shp), d.reshape(shp), nm.reshape(shp), nv.reshape(shp)

    res = [
        upd(w_qkv_a, g_qa, m_w_qkv_a, v_w_qkv_a),
        upd(w_o_a, g_oa, m_w_o_a, v_w_o_a),
        upd(rel_bias, g_rel.reshape(1, N_REL, H), m_rel_bias, v_rel_bias),
        upd(w_qkv_b, g_qb, m_w_qkv_b, v_w_qkv_b),
        upd(w_o_b, g_ob, m_w_o_b, v_w_o_b),
        upd(ffn_w_gate, g_gate, m_ffn_w_gate, v_ffn_w_gate),
        upd(ffn_w_up, g_up, m_ffn_w_up, v_ffn_w_up),
        upd(ffn_w_down, g_down, m_ffn_w_down, v_ffn_w_down),
        upd(ln_g, g_lng, m_ln_g, v_ln_g),
        upd(ln_b, g_lnb, m_ln_b, v_ln_b),
    ]
    grads = [r[0] for r in res]
    deltas = [r[1] for r in res]
    new_m = [r[2] for r in res]
    new_v = [r[3] for r in res]
    return (loss, grad_x, *grads, *deltas, *new_m, *new_v)
```

```python
import functools
import math

import jax
import jax.numpy as jnp
from jax import lax
from jax.experimental import pallas as pl
from jax.experimental.pallas import tpu as pltpu

F32 = jnp.float32
BF16 = jnp.bfloat16
MESH = pl.DeviceIdType.MESH

N_CHIPS = 4
HEAD_DIM = 64
CHUNK = 64
LEFT_CHUNKS = 8
LOOKBACK = LEFT_CHUNKS * CHUNK
REL_CLIP = 128
N_REL = 2 * REL_CLIP + 1
REL_PAD = 384
SB_TILE = 256
QB_A = 256
KW_A = QB_A + LOOKBACK
VR_W = 1024
VR_C0 = KW_A - 1
LN_EPS = 1e-5
ADAM_LR, ADAM_B1, ADAM_B2, ADAM_EPS, ADAM_WD, ADAM_STEP = 0.001, 0.9, 0.999, 1e-08, 0.01, 10
NEG = -1e30
VMEM_LIMIT = 56 * 1024 * 1024

NT_DIMS = (((1,), (1,)), ((), ()))
TN_DIMS = (((0,), (0,)), ((), ()))
ANY = pl.BlockSpec(memory_space=pl.ANY)


def _params(sem=None):
    return pltpu.CompilerParams(dimension_semantics=sem, vmem_limit_bytes=VMEM_LIMIT)


def _tile(n, pref):
    t = min(n, pref)
    assert n % t == 0, (n, pref)
    return t


def _sigmoid(z):
    return 1.0 / (1.0 + jnp.exp(-z))


def _mm_call(name, operands, in_specs, out_shape, out_spec, grid, dims_list, acc_shape,
             add_coef=None, aliases=None):
    n_pairs = len(dims_list)
    nk = grid[-1]
    has_add = add_coef is not None
    n_alias = len(aliases) if aliases else 0

    def body(*refs):
        pair_refs = refs[:2 * n_pairs]
        pos = 2 * n_pairs
        add_ref = refs[pos] if has_add else None
        pos += (1 if has_add else 0) + n_alias
        o_ref = refs[pos]
        acc_ref = refs[pos + 1] if nk > 1 else None

        def product():
            part = None
            for i, dims in enumerate(dims_list):
                d = lax.dot_general(pair_refs[2 * i][...], pair_refs[2 * i + 1][...], dims,
                                    preferred_element_type=F32)
                part = d if part is None else part + d
            return part

        def finish(r):
            if has_add:
                r = r + add_coef * add_ref[...]
            o_ref[...] = r.astype(o_ref.dtype)

        if nk == 1:
            finish(product())
        else:
            k = pl.program_id(len(grid) - 1)

            @pl.when(k == 0)
            def _():
                acc_ref[...] = jnp.zeros_like(acc_ref)

            acc_ref[...] += product()

            @pl.when(k == nk - 1)
            def _():
                finish(acc_ref[...])

    sem = ("parallel",) * (len(grid) - 1) + ("arbitrary",)
    return pl.pallas_call(
        body, name=name, grid=grid, in_specs=in_specs, out_specs=out_spec, out_shape=out_shape,
        scratch_shapes=[pltpu.VMEM(acc_shape, F32)] if nk > 1 else [],
        input_output_aliases=aliases or {},
        compiler_params=_params(sem),
    )(*operands)


def qkv_proj(xb, w, l):
    M, D = xb.shape
    C = w.shape[-1]
    tm = _tile(M, 512)
    return _mm_call(
        "qkv_proj", (xb, w),
        [pl.BlockSpec((tm, D), lambda p, i, k: (i, 0)),
         pl.BlockSpec((None, None, D, C), lambda p, i, k: (p, l, 0, 0))],
        jax.ShapeDtypeStruct((M, N_CHIPS * C), BF16),
        pl.BlockSpec((tm, C), lambda p, i, k: (i, p)),
        (N_CHIPS, M // tm, 1), [(((1,), (0,)), ((), ()))], None)


def o_proj_bwd(dyb, w, l):
    M, D = dyb.shape
    R = w.shape[2]
    tm = _tile(M, 512)
    return _mm_call(
        "o_proj_bwd", (dyb, w),
        [pl.BlockSpec((tm, D), lambda p, i, k: (i, 0)),
         pl.BlockSpec((None, None, R, D), lambda p, i, k: (p, l, 0, 0))],
        jax.ShapeDtypeStruct((M, N_CHIPS * R), BF16),
        pl.BlockSpec((tm, R), lambda p, i, k: (i, p)),
        (N_CHIPS, M // tm, 1), [NT_DIMS], None)


def qkv_proj_bwd(dqkv, w, l, dpre, alpha):
    M = dqkv.shape[0]
    D, C = w.shape[2], w.shape[3]
    tm = _tile(M, 512)

    def body(a_ref, w_ref, add_ref, o_ref):
        acc = alpha * add_ref[...]
        for p in range(N_CHIPS):
            acc = acc + lax.dot_general(a_ref[:, p * C:(p + 1) * C], w_ref[p], NT_DIMS,
                                        preferred_element_type=F32)
        o_ref[...] = acc

    row = pl.BlockSpec((tm, D), lambda i: (i, 0))
    return pl.pallas_call(
        body, name="qkv_proj_bwd", grid=(M // tm,),
        in_specs=[pl.BlockSpec((tm, N_CHIPS * C), lambda i: (i, 0)),
                  pl.BlockSpec((N_CHIPS, None, D, C), lambda i: (0, l, 0, 0)), row],
        out_specs=row, out_shape=jax.ShapeDtypeStruct((M, D), F32),
        compiler_params=_params(("parallel",)),
    )(dqkv, w, dpre)


def ffn_dx(dg, du, wg, wu, l, dpre, alpha):
    _, M, Fs = dg.shape
    D = wg.shape[2]
    tm = _tile(M, 256)

    def body(dg_ref, wg_ref, du_ref, wu_ref, add_ref, o_ref):
        acc = alpha * add_ref[...]
        for p in range(N_CHIPS):
            acc = acc + lax.dot_general(dg_ref[p], wg_ref[p], NT_DIMS, preferred_element_type=F32)
            acc = acc + lax.dot_general(du_ref[p], wu_ref[p], NT_DIMS, preferred_element_type=F32)
        o_ref[...] = acc

    act = pl.BlockSpec((N_CHIPS, tm, Fs), lambda i: (0, i, 0))
    wsp = pl.BlockSpec((N_CHIPS, None, D, Fs), lambda i: (0, l, 0, 0))
    row = pl.BlockSpec((tm, D), lambda i: (i, 0))
    return pl.pallas_call(
        body, name="ffn_dx", grid=(M // tm,),
        in_specs=[act, wsp, act, wsp, row],
        out_specs=row, out_shape=jax.ShapeDtypeStruct((M, D), F32),
        compiler_params=_params(("parallel",)),
    )(dg, wg, du, wu, dpre)


def _dw_call(name, buf, l, a, b, a_spec, b_spec, M, tk):
    _, _, R, C = buf.shape
    return _mm_call(
        name, (a, b, buf),
        [a_spec, b_spec, ANY],
        jax.ShapeDtypeStruct(buf.shape, buf.dtype),
        pl.BlockSpec((None, None, R, C), lambda p, k: (l, p, 0, 0)),
        (N_CHIPS, M // tk), [TN_DIMS], (R, C), aliases={2: 0})


def ffn_bwd(gdown, ggate, gup, l, xb, dyb, wd, g, u, h):
    M, D = dyb.shape
    Fs = wd.shape[2]
    tm = _tile(M, 512)
    n = M // tm

    def body(x_ref, dy_ref, wd_ref, g_ref, u_ref, h_ref, _gd, _gg, _gu,
             dg_ref, du_ref, gd_ref, gg_ref, gu_ref, acc_d, acc_g, acc_u):
        i = pl.program_id(1)

        @pl.when(i == 0)
        def _():
            acc_d[...] = jnp.zeros_like(acc_d)
            acc_g[...] = jnp.zeros_like(acc_g)
            acc_u[...] = jnp.zeros_like(acc_u)

        dy = dy_ref[...]
        dh = lax.dot_general(dy, wd_ref[...], NT_DIMS, preferred_element_type=F32)
        gf = g_ref[...].astype(F32)
        sig = _sigmoid(gf)
        silu = gf * sig
        dg = (dh * u_ref[...].astype(F32) * (sig * (1.0 + gf - silu))).astype(BF16)
        du = (dh * silu).astype(BF16)
        dg_ref[...] = dg
        du_ref[...] = du
        x = x_ref[...]
        acc_d[...] += lax.dot_general(h_ref[...], dy, TN_DIMS, preferred_element_type=F32)
        acc_g[...] += lax.dot_general(x, dg, TN_DIMS, preferred_element_type=F32)
        acc_u[...] += lax.dot_general(x, du, TN_DIMS, preferred_element_type=F32)

        @pl.when(i == n - 1)
        def _():
            gd_ref[...] = acc_d[...].astype(gd_ref.dtype)
            gg_ref[...] = acc_g[...].astype(gg_ref.dtype)
            gu_ref[...] = acc_u[...].astype(gu_ref.dtype)

    row = pl.BlockSpec((tm, D), lambda p, i: (i, 0))
    act = pl.BlockSpec((None, tm, Fs), lambda p, i: (p, i, 0))
    ash = jax.ShapeDtypeStruct((N_CHIPS, M, Fs), BF16)
    down_blk = pl.BlockSpec((None, None, Fs, D), lambda p, i: (l, p, 0, 0))
    up_blk = pl.BlockSpec((None, None, D, Fs), lambda p, i: (l, p, 0, 0))
    return pl.pallas_call(
        body, name="ffn_bwd", grid=(N_CHIPS, n),
        in_specs=[row, row, pl.BlockSpec((None, None, Fs, D), lambda p, i: (p, l, 0, 0)),
                  act, act, act, ANY, ANY, ANY],
        out_specs=[act, act, down_blk, up_blk, up_blk],
        out_shape=[ash, ash] + [jax.ShapeDtypeStruct(b.shape, b.dtype) for b in (gdown, ggate, gup)],
        scratch_shapes=[pltpu.VMEM((Fs, D), F32), pltpu.VMEM((D, Fs), F32), pltpu.VMEM((D, Fs), F32)],
        input_output_aliases={6: 2, 7: 3, 8: 4},
        compiler_params=_params(("parallel", "arbitrary")),
    )(xb, dyb, wd, g, u, h, gdown, ggate, gup)


def dw_qkv(buf, l, xb, dqkv):
    M, D = xb.shape
    C = buf.shape[-1]
    tk = _tile(M, 1024)
    return _dw_call("dw_qkv", buf, l, xb, dqkv,
                    pl.BlockSpec((tk, D), lambda p, k: (k, 0)),
                    pl.BlockSpec((tk, C), lambda p, k: (k, p)), M, tk)


def dw_o(buf, l, o, dyb):
    M, D = dyb.shape
    R = buf.shape[2]
    tk = _tile(M, 1024)
    return _dw_call("dw_o", buf, l, o, dyb,
                    pl.BlockSpec((tk, R), lambda p, k: (k, p)),
                    pl.BlockSpec((tk, D), lambda p, k: (k, 0)), M, tk)


def ffn_up(xb, wg, wu, l):
    M, D = xb.shape
    Fs = wg.shape[-1]
    tm = _tile(M, 512)

    def body(x_ref, wg_ref, wu_ref, g_ref, u_ref, h_ref):
        x = x_ref[...]
        g = jnp.dot(x, wg_ref[...], preferred_element_type=F32)
        u = jnp.dot(x, wu_ref[...], preferred_element_type=F32)
        g_ref[...] = g.astype(BF16)
        u_ref[...] = u.astype(BF16)
        h_ref[...] = (g * _sigmoid(g) * u).astype(BF16)

    wsp = pl.BlockSpec((None, None, D, Fs), lambda p, i: (p, l, 0, 0))
    osp = pl.BlockSpec((None, tm, Fs), lambda p, i: (p, i, 0))
    osh = jax.ShapeDtypeStruct((N_CHIPS, M, Fs), BF16)
    return pl.pallas_call(
        body, name="ffn_up", grid=(N_CHIPS, M // tm),
        in_specs=[pl.BlockSpec((tm, D), lambda p, i: (i, 0)), wsp, wsp],
        out_specs=[osp, osp, osp], out_shape=[osh, osh, osh],
        compiler_params=_params(("parallel", "parallel")),
    )(xb, wg, wu)


def mm_ln(a, w, l, x, gam, bet, alpha, scale, a_piece_major):
    M, D = x.shape
    R = w.shape[2]
    tm = _tile(M, 256)
    if a_piece_major:
        a_spec = pl.BlockSpec((N_CHIPS, tm, R), lambda i: (0, i, 0))
    else:
        a_spec = pl.BlockSpec((tm, N_CHIPS * R), lambda i: (i, 0))

    def body(a_ref, w_ref, x_ref, g_ref, b_ref, xo_ref, xb_ref, xh_ref, rs_ref):
        y = None
        for p in range(N_CHIPS):
            a = a_ref[p] if a_piece_major else a_ref[:, p * R:(p + 1) * R]
            d = jnp.dot(a, w_ref[p], preferred_element_type=F32)
            y = d if y is None else y + d
        pre = alpha * x_ref[...] + scale * y
        mu = jnp.mean(pre, axis=-1, keepdims=True)
        cen = pre - mu
        var = jnp.mean(cen * cen, axis=-1, keepdims=True)
        rstd = lax.rsqrt(var + LN_EPS)
        xhat = cen * rstd
        out = xhat * g_ref[...] + b_ref[...]
        xo_ref[...] = out
        xb_ref[...] = out.astype(BF16)
        xh_ref[...] = xhat
        rs_ref[...] = rstd

    row = pl.BlockSpec((tm, D), lambda i: (i, 0))
    vec = pl.BlockSpec((1, D), lambda i: (0, 0))
    return pl.pallas_call(
        body, name="mm_ln", grid=(M // tm,),
        in_specs=[a_spec, pl.BlockSpec((N_CHIPS, None, R, D), lambda i: (0, l, 0, 0)), row, vec, vec],
        out_specs=[row, row, row, pl.BlockSpec((tm, 1), lambda i: (i, 0))],
        out_shape=[jax.ShapeDtypeStruct((M, D), F32), jax.ShapeDtypeStruct((M, D), BF16),
                   jax.ShapeDtypeStruct((M, D), F32), jax.ShapeDtypeStruct((M, 1), F32)],
        compiler_params=_params(("parallel",)),
    )(a, w, x, gam, bet)


def ln_bwd(dy, xhat, rstd, gam, scale):
    M, D = dy.shape
    tm = _tile(M, 512)

    def body(dy_ref, xh_ref, rs_ref, g_ref, dp_ref, db16_ref, dg_ref, dbt_ref):
        i = pl.program_id(0)
        dy_v = dy_ref[...]
        xh = xh_ref[...]
        dxh = dy_v * g_ref[...]
        m1 = jnp.mean(dxh, axis=-1, keepdims=True)
        m2 = jnp.mean(dxh * xh, axis=-1, keepdims=True)
        dpre = rs_ref[...] * (dxh - m1 - xh * m2)
        dp_ref[...] = dpre
        db16_ref[...] = (scale * dpre).astype(BF16)
        dgp = jnp.sum(dy_v * xh, axis=0, keepdims=True)
        dbp = jnp.sum(dy_v, axis=0, keepdims=True)

        @pl.when(i == 0)
        def _():
            dg_ref[...] = dgp
            dbt_ref[...] = dbp

        @pl.when(i > 0)
        def _():
            dg_ref[...] += dgp
            dbt_ref[...] += dbp

    row = pl.BlockSpec((tm, D), lambda i: (i, 0))
    vec = pl.BlockSpec((1, D), lambda i: (0, 0))
    return pl.pallas_call(
        body, name="ln_bwd", grid=(M // tm,),
        in_specs=[row, row, pl.BlockSpec((tm, 1), lambda i: (i, 0)), vec],
        out_specs=[row, row, vec, vec],
        out_shape=[jax.ShapeDtypeStruct((M, D), F32), jax.ShapeDtypeStruct((M, D), BF16),
                   jax.ShapeDtypeStruct((1, D), F32), jax.ShapeDtypeStruct((1, D), F32)],
        compiler_params=_params(("arbitrary",)),
    )(dy, xhat, rstd, gam)


def loss_head(y, tgt):
    M, D = y.shape
    tm = _tile(M, 512)
    n = M // tm

    def body(y_ref, t_ref, dy_ref, l_ref, acc_ref):
        i = pl.program_id(0)
        e = y_ref[...] - t_ref[...]
        dy_ref[...] = e * (1.0 / D)
        part = jnp.sum(e * e, axis=0, keepdims=True)

        @pl.when(i == 0)
        def _():
            acc_ref[...] = part

        @pl.when(i > 0)
        def _():
            acc_ref[...] += part

        @pl.when(i == n - 1)
        def _():
            l_ref[...] = (0.5 / D) * jnp.sum(acc_ref[...], axis=1, keepdims=True)

    row = pl.BlockSpec((tm, D), lambda i: (i, 0))
    return pl.pallas_call(
        body, name="loss_head", grid=(n,),
        in_specs=[row, row],
        out_specs=[row, pl.BlockSpec((1, 1), lambda i: (0, 0))],
        out_shape=[jax.ShapeDtypeStruct((M, D), F32), jax.ShapeDtypeStruct((1, 1), F32)],
        scratch_shapes=[pltpu.VMEM((1, D), F32)],
        compiler_params=_params(("arbitrary",)),
    )(y, tgt)


def _rel_onehot_t():
    r = lax.broadcasted_iota(jnp.int32, (REL_PAD, VR_W), 0)
    n = lax.broadcasted_iota(jnp.int32, (REL_PAD, VR_W), 1)
    idx = jnp.clip(VR_C0 - n, -REL_CLIP, REL_CLIP) + REL_CLIP
    return (r == idx).astype(F32)


def bias_vec(tab_t):
    H = tab_t.shape[0]

    def body(t_ref, o_ref):
        o_ref[...] = jnp.dot(t_ref[...], _rel_onehot_t(), precision=lax.Precision.HIGHEST,
                             preferred_element_type=F32)

    return pl.pallas_call(
        body, name="bias_vec", out_shape=jax.ShapeDtypeStruct((H, VR_W), F32),
        compiler_params=_params(),
    )(tab_t)


def bias_vec_bwd(dvr):
    n, H, _ = dvr.shape

    def body(d_ref, o_ref):
        tot = d_ref[0]
        for i in range(1, n):
            tot = tot + d_ref[i]
        o_ref[...] = lax.dot_general(tot, _rel_onehot_t(), NT_DIMS, precision=lax.Precision.HIGHEST,
                                     preferred_element_type=F32)

    return pl.pallas_call(
        body, name="bias_vec_bwd", out_shape=jax.ShapeDtypeStruct((H, REL_PAD), F32),
        compiler_params=_params(),
    )(dvr)


def _a_bias_mask(vr_row):
    xb = jnp.broadcast_to(vr_row, (QB_A, VR_W))
    tile = pltpu.roll(xb, VR_W - (QB_A - 1), 1, stride=1, stride_axis=0)[:, :KW_A]
    qc = lax.broadcasted_iota(jnp.int32, (QB_A, KW_A), 0) // CHUNK
    kc = lax.broadcasted_iota(jnp.int32, (QB_A, KW_A), 1) // CHUNK
    valid = (kc >= qc) & (kc <= qc + LEFT_CHUNKS)
    return jnp.where(valid, tile, NEG)


def _a_diag_sums(db_acc, h):
    acc8 = None
    for a in range(QB_A // 8):
        grp = db_acc[h, 8 * a:8 * a + 8, :]
        shift = QB_A - 8 - 8 * a
        if shift:
            grp = pltpu.roll(grp, shift, 1)
        acc8 = grp if acc8 is None else acc8 + grp
    sub = lax.broadcasted_iota(jnp.int32, (8, VR_W), 0)
    tot = jnp.zeros((8, VR_W), F32)
    for b in range(8):
        moved = pltpu.roll(acc8, 7 - b, 1) if b < 7 else acc8
        tot = tot + jnp.where(sub == b, moved, 0.0)
    return jnp.sum(tot, axis=0, keepdims=True)


def _a_blocks(S):
    out = []
    for qi in range(S // QB_A):
        q0 = qi * QB_A
        ks = max(0, q0 - LOOKBACK)
        out.append((q0, ks, q0 + QB_A, ks - (q0 - LOOKBACK)))
    return out


def _head_specs(S, HP):
    q = pl.BlockSpec((S, 2 * HEAD_DIM), lambda b, hp: (b, hp))
    k = pl.BlockSpec((S, 2 * HEAD_DIM), lambda b, hp: (b, HP + hp))
    v = pl.BlockSpec((S, 2 * HEAD_DIM), lambda b, hp: (b, 2 * HP + hp))
    return q, k, v


def attn_a_fwd(qkv, vr, B, S):
    D = qkv.shape[1] // 3
    HP = D // (2 * HEAD_DIM)
    scale = HEAD_DIM ** -0.5
    blocks = _a_blocks(S)

    def body(q_ref, k_ref, v_ref, vr_ref, o_ref):
        for h in range(2):
            lo = h * HEAD_DIM
            bm = _a_bias_mask(vr_ref[h:h + 1, :])
            for (q0, ks, ke, joff) in blocks:
                q = q_ref[q0:q0 + QB_A, lo:lo + HEAD_DIM]
                k = k_ref[ks:ke, lo:lo + HEAD_DIM]
                v = v_ref[ks:ke, lo:lo + HEAD_DIM]
                s = lax.dot_general(q, k, NT_DIMS, preferred_element_type=F32) * scale + bm[:, joff:]
                m = jnp.max(s, axis=-1, keepdims=True)
                p = jnp.exp(s - m)
                den = jnp.sum(p, axis=-1, keepdims=True)
                o = jnp.dot(p.astype(BF16), v, preferred_element_type=F32) / den
                o_ref[q0:q0 + QB_A, lo:lo + HEAD_DIM] = o.astype(BF16)

    qs, ks_, vs = _head_specs(S, HP)
    return pl.pallas_call(
        body, name="attn_a_fwd", grid=(B, HP),
        in_specs=[qs, ks_, vs, pl.BlockSpec((None, 2, VR_W), lambda b, hp: (hp, 0, 0))],
        out_specs=pl.BlockSpec((S, 2 * HEAD_DIM), lambda b, hp: (b, hp)),
        out_shape=jax.ShapeDtypeStruct((B * S, D), BF16),
        compiler_params=_params(("parallel", "parallel")),
    )(qkv, qkv, qkv, vr)


def attn_a_bwd(qkv, vr, do, B, S):
    D = qkv.shape[1] // 3
    HP = D // (2 * HEAD_DIM)
    scale = HEAD_DIM ** -0.5
    blocks = _a_blocks(S)

    def body(q_ref, k_ref, v_ref, vr_ref, do_ref, dq_ref, dk_ref, dv_ref, dvr_ref,
             dk_acc, dv_acc, db_acc):
        dk_acc[...] = jnp.zeros_like(dk_acc)
        dv_acc[...] = jnp.zeros_like(dv_acc)
        db_acc[...] = jnp.zeros_like(db_acc)
        for h in range(2):
            lo = h * HEAD_DIM
            bm = _a_bias_mask(vr_ref[h:h + 1, :])
            for (q0, ks, ke, joff) in blocks:
                q = q_ref[q0:q0 + QB_A, lo:lo + HEAD_DIM]
                k = k_ref[ks:ke, lo:lo + HEAD_DIM]
                v = v_ref[ks:ke, lo:lo + HEAD_DIM]
                dob = do_ref[q0:q0 + QB_A, lo:lo + HEAD_DIM]
                s = lax.dot_general(q, k, NT_DIMS, preferred_element_type=F32) * scale + bm[:, joff:]
                m = jnp.max(s, axis=-1, keepdims=True)
                e = jnp.exp(s - m)
                p = e / jnp.sum(e, axis=-1, keepdims=True)
                dp = lax.dot_general(dob, v, NT_DIMS, preferred_element_type=F32)
                ds = p * (dp - jnp.sum(p * dp, axis=-1, keepdims=True))
                dsb = ds.astype(BF16)
                dq = jnp.dot(dsb, k, preferred_element_type=F32) * scale
                dq_ref[q0:q0 + QB_A, lo:lo + HEAD_DIM] = dq.astype(BF16)
                dk_acc[ks:ke, lo:lo + HEAD_DIM] += lax.dot_general(
                    dsb, q, TN_DIMS, preferred_element_type=F32) * scale
                dv_acc[ks:ke, lo:lo + HEAD_DIM] += lax.dot_general(
                    p.astype(BF16), dob, TN_DIMS, preferred_element_type=F32)
                db_acc[h, :, joff:KW_A] += ds
            dvr_ref[h:h + 1, :] = _a_diag_sums(db_acc, h)
        dk_ref[...] = dk_acc[...].astype(BF16)
        dv_ref[...] = dv_acc[...].astype(BF16)

    qs, ks_, vs = _head_specs(S, HP)
    hd = pl.BlockSpec((S, 2 * HEAD_DIM), lambda b, hp: (b, hp))
    osh = jax.ShapeDtypeStruct((B * S, D), BF16)
    return pl.pallas_call(
        body, name="attn_a_bwd", grid=(B, HP),
        in_specs=[qs, ks_, vs, pl.BlockSpec((None, 2, VR_W), lambda b, hp: (hp, 0, 0)), hd],
        out_specs=[hd, hd, hd, pl.BlockSpec((None, None, 2, VR_W), lambda b, hp: (b, hp, 0, 0))],
        out_shape=[osh, osh, osh, jax.ShapeDtypeStruct((B, HP, 2, VR_W), F32)],
        scratch_shapes=[pltpu.VMEM((S, 2 * HEAD_DIM), F32), pltpu.VMEM((S, 2 * HEAD_DIM), F32),
                        pltpu.VMEM((2, QB_A, VR_W), F32)],
        compiler_params=_params(("parallel", "parallel")),
    )(qkv, qkv, qkv, vr, do)


def _tri(cmp):
    j = lax.broadcasted_iota(jnp.int32, (SB_TILE, SB_TILE), 0)
    s = lax.broadcasted_iota(jnp.int32, (SB_TILE, SB_TILE), 1)
    return cmp(j, s).astype(BF16)


def _cumsum_mm(x, tri):
    hi = x.astype(BF16)
    mid = (x - hi.astype(F32)).astype(BF16)
    return jnp.dot(hi, tri, preferred_element_type=F32) + jnp.dot(mid, tri, preferred_element_type=F32)


def _sb_logs(q, k, scale, diagonal):
    z = lax.dot_general(q, k, NT_DIMS, preferred_element_type=F32) * scale
    log_b = jnp.minimum(z, 0.0) - jnp.log(1.0 + jnp.exp(-jnp.abs(z)))
    log_1mb = log_b - z
    if not diagonal:
        return log_b, log_1mb, None
    row = lax.broadcasted_iota(jnp.int32, (SB_TILE, SB_TILE), 0)
    col = lax.broadcasted_iota(jnp.int32, (SB_TILE, SB_TILE), 1)
    causal = col < row
    return log_b, jnp.where(causal, log_1mb, 0.0), causal


def attn_b_fwd(qkv, B, S):
    D = qkv.shape[1] // 3
    HP = D // (2 * HEAD_DIM)
    scale = HEAD_DIM ** -0.5
    nb = S // SB_TILE

    def body(q_ref, k_ref, v_ref, o_ref, nt_ref):
        tri = _tri(lambda j, s: j > s)
        heads = [slice(h * HEAD_DIM, (h + 1) * HEAD_DIM) for h in range(2)]

        def q_block(qb, n_pairs, parity):
            q0 = pl.multiple_of(qb * SB_TILE, SB_TILE)
            rows = pl.ds(q0, SB_TILE)
            qs = [q_ref[rows, hs] for hs in heads]

            def step(blocks, state):
                chains = [(h, kb, diagonal, pl.ds(pl.multiple_of(kb * SB_TILE, SB_TILE), SB_TILE))
                          for h in range(2) for kb, diagonal in blocks]
                logs = [_sb_logs(qs[h], k_ref[keys, heads[h]], scale, diagonal)
                        for h, _, diagonal, keys in chains]
                sums = [_cumsum_mm(log_1mb, tri) for _, log_1mb, _ in logs]
                rights = [state[0][0], state[1][0]]
                accs = [state[0][1], state[1][1]]
                for (h, _, diagonal, keys), (log_b, log_1mb, causal), csum in zip(chains, logs, sums):
                    a = jnp.exp(log_b + csum + rights[h])
                    if diagonal:
                        a = jnp.where(causal, a, 0.0)
                    accs[h] = accs[h] + jnp.dot(a.astype(BF16), v_ref[keys, heads[h]],
                                                preferred_element_type=F32)
                    rights[h] = rights[h] + jnp.sum(log_1mb, axis=-1, keepdims=True)
                return ((rights[0], accs[0]), (rights[1], accs[1]))

            zero = (jnp.zeros((SB_TILE, 1), F32), jnp.zeros((SB_TILE, HEAD_DIM), F32))
            first = [(qb, True)] + ([(qb - 1, False)] if parity else [])
            top = qb - len(first)
            state = lax.fori_loop(
                0, n_pairs, lambda t, st: step([(top - 2 * t, False), (top - 2 * t - 1, False)], st),
                step(first, (zero, zero)))
            for hs, (right, acc) in zip(heads, state):
                o_ref[rows, hs] = acc.astype(BF16)
                nt_ref[rows, hs] = jnp.broadcast_to(right, (SB_TILE, HEAD_DIM))

        def q_pair_loop(j, carry):
            q_block(2 * j, j, 0)
            q_block(2 * j + 1, j, 1)
            return carry

        lax.fori_loop(0, nb // 2, q_pair_loop, 0)

    qs, ks_, vs = _head_specs(S, HP)
    hd = pl.BlockSpec((S, 2 * HEAD_DIM), lambda b, hp: (b, hp))
    return pl.pallas_call(
        body, name="attn_b_fwd", grid=(B, HP),
        in_specs=[qs, ks_, vs], out_specs=[hd, hd],
        out_shape=[jax.ShapeDtypeStruct((B * S, D), BF16), jax.ShapeDtypeStruct((B * S, D), F32)],
        compiler_params=_params(("parallel", "parallel")),
    )(qkv, qkv, qkv)


def attn_b_bwd(qkv, do, ntot, B, S):
    D = qkv.shape[1] // 3
    HP = D // (2 * HEAD_DIM)
    scale = HEAD_DIM ** -0.5
    nb = S // SB_TILE

    def body(q_ref, k_ref, v_ref, do_ref, nt_ref, dq_ref, dk_ref, dv_ref, dkt_acc, dvt_acc):
        tri_incl = _tri(lambda j, s: j <= s)
        tri_excl = _tri(lambda j, s: j < s)
        heads = [slice(h * HEAD_DIM, (h + 1) * HEAD_DIM) for h in range(2)]
        dkt_acc[...] = jnp.zeros_like(dkt_acc)
        dvt_acc[...] = jnp.zeros_like(dvt_acc)

        def q_block(qb, n_pairs, parity):
            q0 = pl.multiple_of(qb * SB_TILE, SB_TILE)
            rows = pl.ds(q0, SB_TILE)
            qt_pair = q_ref[rows, :].astype(F32).T.astype(BF16)
            dot_pair = do_ref[rows, :].astype(F32).T.astype(BF16)
            per_head = [(hs, q_ref[rows, hs], do_ref[rows, hs], qt_pair[hs, :], dot_pair[hs, :],
                         nt_ref[rows, hs.start:hs.start + 1]) for hs in heads]

            def step(blocks, state):
                chains = [(h, kb, diagonal, pl.ds(pl.multiple_of(kb * SB_TILE, SB_TILE), SB_TILE))
                          for h in range(2) for kb, diagonal in blocks]
                ks = [k_ref[keys, per_head[h][0]] for h, _, _, keys in chains]
                logs = [_sb_logs(per_head[h][1], k, scale, diagonal)
                        for (h, _, diagonal, _), k in zip(chains, ks)]
                das = [lax.dot_general(per_head[h][2], v_ref[keys, per_head[h][0]], NT_DIMS,
                                       preferred_element_type=F32) for h, _, _, keys in chains]
                sums = [_cumsum_mm(log_1mb, tri_incl) for _, log_1mb, _ in logs]
                left_n = [state[0][0], state[1][0]]
                left_d = [state[0][1], state[1][1]]
                dq_acc = [state[0][2], state[1][2]]
                a_s, dls = [], []
                for (h, _, diagonal, _), (log_b, log_1mb, causal), csum, da in zip(chains, logs, sums, das):
                    a = jnp.exp(log_b + (per_head[h][5] - left_n[h]) - csum)
                    if diagonal:
                        a = jnp.where(causal, a, 0.0)
                    a_s.append(a)
                    dls.append(a * da)
                    left_n[h] = left_n[h] + jnp.sum(log_1mb, axis=-1, keepdims=True)
                dsums = [_cumsum_mm(dl, tri_excl) for dl in dls]
                dzbs = []
                for (h, _, diagonal, _), (log_b, log_1mb, causal), dl, dsum in zip(chains, logs, dls, dsums):
                    dz = dl * jnp.exp(log_1mb) - (left_d[h] + dsum) * jnp.exp(log_b)
                    if diagonal:
                        dz = jnp.where(causal, dz, 0.0)
                    dzbs.append(dz.astype(BF16))
                    left_d[h] = left_d[h] + jnp.sum(dl, axis=-1, keepdims=True)
                for (h, kb, _, _), k, a, dzb in zip(chains, ks, a_s, dzbs):
                    hs, _, _, qt, dot_, _ = per_head[h]
                    dq_acc[h] = dq_acc[h] + jnp.dot(dzb, k, preferred_element_type=F32)
                    dkt_acc[kb, hs, :] += jnp.dot(qt, dzb, preferred_element_type=F32)
                    dvt_acc[kb, hs, :] += jnp.dot(dot_, a.astype(BF16), preferred_element_type=F32)
                return ((left_n[0], left_d[0], dq_acc[0]), (left_n[1], left_d[1], dq_acc[1]))

            zero1 = jnp.zeros((SB_TILE, 1), F32)
            zero = (zero1, zero1, jnp.zeros((SB_TILE, HEAD_DIM), F32))
            state = lax.fori_loop(
                0, n_pairs, lambda t, st: step([(2 * t, False), (2 * t + 1, False)], st), (zero, zero))
            last = ([(qb - 1, False)] if parity else []) + [(qb, True)]
            state = step(last, state)
            for hs, (_, _, dq_acc) in zip(heads, state):
                dq_ref[rows, hs] = (dq_acc * scale).astype(BF16)

        def q_pair_loop(j, carry):
            q_block(2 * j, j, 0)
            q_block(2 * j + 1, j, 1)
            return carry

        lax.fori_loop(0, nb // 2, q_pair_loop, 0)
        for kb in range(nb):
            dk_ref[kb * SB_TILE:(kb + 1) * SB_TILE, :] = (dkt_acc[kb].T * scale).astype(BF16)
            dv_ref[kb * SB_TILE:(kb + 1) * SB_TILE, :] = dvt_acc[kb].T.astype(BF16)

    qs, ks_, vs = _head_specs(S, HP)
    hd = pl.BlockSpec((S, 2 * HEAD_DIM), lambda b, hp: (b, hp))
    osh = jax.ShapeDtypeStruct((B * S, D), BF16)
    acc = pltpu.VMEM((nb, 2 * HEAD_DIM, SB_TILE), F32)
    return pl.pallas_call(
        body, name="attn_b_bwd", grid=(B, HP),
        in_specs=[qs, ks_, vs, hd, hd], out_specs=[hd, hd, hd], out_shape=[osh, osh, osh],
        scratch_shapes=[acc, acc],
        compiler_params=_params(("parallel", "parallel")),
    )(qkv, qkv, qkv, do, ntot)


def _place():
    x, y, c = lax.axis_index("x"), lax.axis_index("y"), lax.axis_index("c")
    chips = [(1 - x, y), (x, 1 - y), (1 - x, 1 - y)]
    return x, y, c, 2 * x + y, chips


def _remote(src, dst, send_sem, recv_sem, dev):
    return pltpu.make_async_remote_copy(src_ref=src, dst_ref=dst, send_sem=send_sem, recv_sem=recv_sem,
                                        device_id=dev, device_id_type=MESH)


def gather_weights(shards):
    n = len(shards)

    def body(*refs):
        ins, outs = refs[:n], refs[n:2 * n]
        send1, recv1, send2, recv2, lsem = refs[2 * n:]
        x, y, c, me, chips = _place()
        local, first = [], []
        for f in range(n):
            hl = shards[f].shape[0] // 2
            cp = pltpu.make_async_copy(ins[f], outs[f].at[me], lsem.at[f])
            cp.start()
            local.append(cp)
            for j, (qx, qy) in enumerate(chips):
                half = pl.ds(c * hl, hl)
                cp = _remote(ins[f].at[half], outs[f].at[me, half],
                             send1.at[3 * f + j], recv1.at[3 * f + j], (qx, qy, c))
                cp.start()
                first.append(cp)
        passed = []
        for f in range(n):
            hl = shards[f].shape[0] // 2
            for j, (qx, qy) in enumerate(chips):
                slab = outs[f].at[2 * qx + qy, pl.ds(c * hl, hl)]
                _remote(slab, slab, send1.at[3 * f + j], recv1.at[3 * f + j], (x, y, c)).wait_recv()
                cp = _remote(slab, slab, send2.at[3 * f + j], recv2.at[3 * f + j], (x, y, 1 - c))
                cp.start()
                passed.append(cp)
        for f in range(n):
            hl = shards[f].shape[0] // 2
            for j, (qx, qy) in enumerate(chips):
                slab = outs[f].at[2 * qx + qy, pl.ds((1 - c) * hl, hl)]
                _remote(slab, slab, send2.at[3 * f + j], recv2.at[3 * f + j], (x, y, c)).wait_recv()
        for cp in first + passed:
            cp.wait_send()
        for cp in local:
            cp.wait()

    sems = pltpu.SemaphoreType.DMA((3 * n,))
    return pl.pallas_call(
        body, name="gather_weights",
        in_specs=[ANY] * n, out_specs=[ANY] * n,
        out_shape=[jax.ShapeDtypeStruct((N_CHIPS,) + s.shape, s.dtype) for s in shards],
        scratch_shapes=[sems, sems, sems, sems, pltpu.SemaphoreType.DMA((n,))],
        compiler_params=pltpu.CompilerParams(has_side_effects=True),
    )(*shards)


HBM = pl.BlockSpec(memory_space=pltpu.HBM)
SEM = pl.BlockSpec(memory_space=pltpu.SEMAPHORE)
EFFECT = pltpu.SideEffectType.DATAFLOW_SIDE_EFFECTING


def _in_hbm(a):
    return pltpu.with_memory_space_constraint(a, pltpu.HBM)


def _row_half(ref, _unused, which):
    hr = ref.shape[-2] // 2
    idx = [pl.ds(0, d) for d in ref.shape[:-2]] + [pl.ds(which * hr, hr), pl.ds(0, ref.shape[-1])]
    return ref.at[tuple(idx)]


def cast_place(ids, w):
    L, R, C = w.shape

    def body(ids_ref, w_ref, s_ref, land_ref):
        v = w_ref[...].astype(BF16)
        s_ref[...] = v
        land_ref[...] = v

    return pl.pallas_call(
        body, name="cast_place",
        grid_spec=pltpu.PrefetchScalarGridSpec(
            num_scalar_prefetch=1, grid=(L,),
            in_specs=[pl.BlockSpec((None, R, C), lambda l, ids: (l, 0, 0))],
            out_specs=[pl.BlockSpec((None, R, C), lambda l, ids: (l, 0, 0)),
                       pl.BlockSpec((None, None, R, C), lambda l, ids: (ids[1], l, 0, 0))]),
        out_shape=[jax.ShapeDtypeStruct((L, R, C), BF16), jax.ShapeDtypeStruct((N_CHIPS, L, R, C), BF16)],
        compiler_params=_params(("parallel",)),
    )(ids, w)


def gather_start(layers, zones):
    flat = [s for lay in layers for s in lay]
    flat_zones = [z for lay in zones for z in lay]
    counts = [len(lay) for lay in layers]
    n, nl = len(flat), len(layers)

    def body(*refs):
        ins, lands = refs[:n], refs[n:2 * n]
        send, recv = refs[2 * n:2 * n + nl], refs[2 * n + nl:2 * n + 2 * nl]
        token = refs[-1]
        x, y, c, me, chips = _place()
        f = 0
        for li, cnt in enumerate(counts):
            for k in range(cnt):
                for j, (qx, qy) in enumerate(chips):
                    _remote(_row_half(ins[f], 0, c), _row_half(lands[f].at[me], 0, c),
                            send[li].at[3 * k + j], recv[li].at[3 * k + j], (qx, qy, c)).start()
                f += 1
        token[...] = jnp.zeros_like(token)

    sem_shapes = [pltpu.SemaphoreType.DMA((3 * cnt,)) for cnt in counts]
    land_shapes = [(N_CHIPS,) + s.shape for s in flat]
    res = pl.pallas_call(
        body, name="gather_start",
        out_shape=(*sem_shapes, *sem_shapes,
                   *[pltpu.HBM(s.shape, s.dtype) for s in flat],
                   *[pltpu.HBM(shp, s.dtype) for shp, s in zip(land_shapes, flat)],
                   jax.ShapeDtypeStruct((8, 128), F32)),
        in_specs=[HBM] * (2 * n),
        out_specs=(*[SEM] * (2 * nl), *[HBM] * (2 * n), pl.BlockSpec(memory_space=pltpu.VMEM)),
        input_output_aliases={k: 2 * nl + k for k in range(2 * n)},
        compiler_params=pltpu.CompilerParams(has_side_effects=EFFECT),
    )(*[_in_hbm(s) for s in flat], *[_in_hbm(z) for z in flat_zones])
    send, recv = res[:nl], res[nl:2 * nl]
    thru, lands, token = res[2 * nl:2 * nl + n], res[2 * nl + n:2 * nl + 2 * n], res[-1]
    out, f = [], 0
    for li, cnt in enumerate(counts):
        out.append((send[li], recv[li], list(thru[f:f + cnt]), list(lands[f:f + cnt])))
        f += cnt
    return out, token


def gather_wait(li, send, recv, shards, lands, after):
    m = len(shards)

    def body(*refs):
        ins, lnd = refs[:m], refs[m:2 * m]
        snd, rcv = refs[2 * m], refs[2 * m + 1]
        x, y, c, me, chips = _place()
        for k in range(m):
            for j, (qx, qy) in enumerate(chips):
                cp = _remote(_row_half(ins[k], 0, c), _row_half(lnd[k].at[2 * qx + qy], 0, c),
                             snd.at[3 * k + j], rcv.at[3 * k + j], (qx, qy, c))
                cp.wait_send()
                cp.wait_recv()

    res = pl.pallas_call(
        body, name=f"gather_wait_{li}",
        out_shape=(*[pltpu.HBM(s.shape, s.dtype) for s in shards],
                   *[pltpu.HBM(s.shape, s.dtype) for s in lands]),
        in_specs=[HBM] * (2 * m) + [SEM, SEM, ANY], out_specs=[HBM] * (2 * m),
        input_output_aliases={k: k for k in range(2 * m)},
        compiler_params=pltpu.CompilerParams(has_side_effects=EFFECT),
    )(*shards, *lands, send, recv, after)
    return list(res[:m]), list(res[m:])


def gather_forward(li, lands):
    m = len(lands)

    def body(*refs):
        outs = refs[m:2 * m]
        send, recv = refs[2 * m:]
        x, y, c, me, chips = _place()
        passed = []
        for k in range(m):
            for j, (qx, qy) in enumerate(chips):
                slab = _row_half(outs[k].at[2 * qx + qy], 0, c)
                cp = _remote(slab, slab, send.at[3 * k + j], recv.at[3 * k + j], (x, y, 1 - c))
                cp.start()
                passed.append(cp)
        for k in range(m):
            for j, (qx, qy) in enumerate(chips):
                slab = _row_half(outs[k].at[2 * qx + qy], 0, 1 - c)
                _remote(slab, slab, send.at[3 * k + j], recv.at[3 * k + j], (x, y, c)).wait_recv()
        for cp in passed:
            cp.wait_send()

    sems = pltpu.SemaphoreType.DMA((3 * m,))
    return pl.pallas_call(
        body, name=f"gather_forward_{li}",
        in_specs=[ANY] * m, out_specs=[ANY] * m,
        out_shape=[jax.ShapeDtypeStruct(s.shape, s.dtype) for s in lands],
        input_output_aliases={k: k for k in range(m)},
        scratch_shapes=[sems, sems],
        compiler_params=pltpu.CompilerParams(has_side_effects=True),
    )(*lands)


def pair_exchange_rows(li, grads):
    n = len(grads)

    def body(*refs):
        ins, outs = refs[:n], refs[n:2 * n]
        send, recv = refs[2 * n:]
        x, y, c, _, _ = _place()
        cps = []
        for k in range(n):
            cp = _remote(_row_half(ins[k], 0, 1 - c), outs[k], send.at[k], recv.at[k], (x, y, 1 - c))
            cp.start()
            cps.append(cp)
        for cp in cps:
            cp.wait()

    sems = pltpu.SemaphoreType.DMA((n,))
    return pl.pallas_call(
        body, name=f"pair_exchange_{li}",
        in_specs=[ANY] * n, out_specs=[ANY] * n,
        out_shape=[jax.ShapeDtypeStruct(g.shape[:-2] + (g.shape[-2] // 2, g.shape[-1]), g.dtype)
                   for g in grads],
        scratch_shapes=[sems, sems],
        compiler_params=pltpu.CompilerParams(has_side_effects=True),
    )(*grads)


def pair_add_rows(ids, grad, recv):
    L, P, hr, C = recv.shape

    def body(ids_ref, a_ref, b_ref, o_ref):
        o_ref[...] = (a_ref[...].astype(F32) + b_ref[...].astype(F32)).astype(o_ref.dtype)

    blk = (None, None, hr, C)
    return pl.pallas_call(
        body, name="pair_add",
        grid_spec=pltpu.PrefetchScalarGridSpec(
            num_scalar_prefetch=1, grid=(L, P),
            in_specs=[pl.BlockSpec(blk, lambda l, p, ids: (l, p, ids[0], 0)),
                      pl.BlockSpec(blk, lambda l, p, ids: (l, p, 0, 0))],
            out_specs=pl.BlockSpec(blk, lambda l, p, ids: (l, p, 0, 0))),
        out_shape=jax.ShapeDtypeStruct(recv.shape, recv.dtype),
        compiler_params=_params(("parallel", "parallel")),
    )(ids, grad, recv)


def reduce_start(li, parts):
    m = len(parts)

    def body(*refs):
        ins, lands = refs[:m], refs[m:2 * m]
        send, recv = refs[2 * m], refs[2 * m + 1]
        token = refs[-1]
        x, y, c, me, chips = _place()
        for k in range(m):
            rows = pl.ds(0, parts[k].shape[0])
            for j, (qx, qy) in enumerate(chips):
                _remote(ins[k].at[rows, 2 * qx + qy], lands[k].at[rows, me],
                        send.at[3 * k + j], recv.at[3 * k + j], (qx, qy, c)).start()
        token[...] = jnp.zeros_like(token)

    sems = pltpu.SemaphoreType.DMA((3 * m,))
    res = pl.pallas_call(
        body, name=f"reduce_start_{li}",
        out_shape=(sems, sems, *[pltpu.HBM(s.shape, s.dtype) for s in parts],
                   *[pltpu.HBM(s.shape, s.dtype) for s in parts], jax.ShapeDtypeStruct((8, 128), F32)),
        in_specs=[HBM] * (2 * m),
        out_specs=(SEM, SEM, *[HBM] * (2 * m), pl.BlockSpec(memory_space=pltpu.VMEM)),
        input_output_aliases={k: 2 + k for k in range(2 * m)},
        compiler_params=pltpu.CompilerParams(has_side_effects=EFFECT),
    )(*[_in_hbm(s) for s in parts], *[_in_hbm(lax.empty(s.shape, s.dtype)) for s in parts])
    return res[0], res[1], list(res[2:2 + m]), list(res[2 + m:2 + 2 * m]), res[-1]


def reduce_wait(li, send, recv, parts, lands, after):
    m = len(parts)

    def body(*refs):
        ins, lnd = refs[:m], refs[m:2 * m]
        snd, rcv = refs[2 * m], refs[2 * m + 1]
        x, y, c, me, chips = _place()
        for k in range(m):
            rows = pl.ds(0, parts[k].shape[0])
            for j, (qx, qy) in enumerate(chips):
                cp = _remote(ins[k].at[rows, 2 * qx + qy], lnd[k].at[rows, 2 * qx + qy],
                             snd.at[3 * k + j], rcv.at[3 * k + j], (qx, qy, c))
                cp.wait_send()
                cp.wait_recv()

    res = pl.pallas_call(
        body, name=f"reduce_wait_{li}",
        out_shape=(*[pltpu.HBM(s.shape, s.dtype) for s in parts],
                   *[pltpu.HBM(s.shape, s.dtype) for s in lands]),
        in_specs=[HBM] * (2 * m) + [SEM, SEM, ANY], out_specs=[HBM] * (2 * m),
        input_output_aliases={k: k for k in range(2 * m)},
        compiler_params=pltpu.CompilerParams(has_side_effects=EFFECT),
    )(*parts, *lands, send, recv, after)
    return list(res[:m]), list(res[m:])


def chip_sum_rows(ids, land, part, gfull, l0):
    L, P, hr, C = land.shape

    def body(ids_ref, land_ref, part_ref, _g, o_ref):
        tot = None
        for q in range(P):
            term = jnp.where(ids_ref[1] == q, part_ref[...], land_ref[q]).astype(F32)
            tot = term if tot is None else tot + term
        o_ref[...] = tot

    return pl.pallas_call(
        body, name="chip_sum",
        grid_spec=pltpu.PrefetchScalarGridSpec(
            num_scalar_prefetch=1, grid=(L,),
            in_specs=[pl.BlockSpec((None, P, hr, C), lambda l, ids: (l, 0, 0, 0)),
                      pl.BlockSpec((None, None, hr, C), lambda l, ids: (l, ids[1], 0, 0)), ANY],
            out_specs=pl.BlockSpec((None, hr, C), lambda l, ids: (l0 + l, ids[0], 0))),
        out_shape=jax.ShapeDtypeStruct(gfull.shape, gfull.dtype),
        input_output_aliases={3: 0},
        compiler_params=_params(("arbitrary",)),
    )(ids, land, part, gfull)


def half_swap_rows(grads):
    n = len(grads)

    def body(*refs):
        outs = refs[n:2 * n]
        send, recv = refs[2 * n:]
        x, y, c, _, _ = _place()
        cps = []
        for k in range(n):
            mine = _row_half(outs[k], 0, c)
            cp = _remote(mine, mine, send.at[k], recv.at[k], (x, y, 1 - c))
            cp.start()
            cps.append(cp)
        for k in range(n):
            theirs = _row_half(outs[k], 0, 1 - c)
            _remote(theirs, theirs, send.at[k], recv.at[k], (x, y, c)).wait_recv()
        for cp in cps:
            cp.wait_send()

    sems = pltpu.SemaphoreType.DMA((n,))
    return pl.pallas_call(
        body, name="half_swap",
        in_specs=[ANY] * n, out_specs=[ANY] * n,
        out_shape=[jax.ShapeDtypeStruct(g.shape, g.dtype) for g in grads],
        input_output_aliases={k: k for k in range(n)},
        scratch_shapes=[sems, sems],
        compiler_params=pltpu.CompilerParams(has_side_effects=True),
    )(*grads)


def pair_exchange(grads):
    n = len(grads)

    def body(*refs):
        ins, outs = refs[:n], refs[n:2 * n]
        send, recv = refs[2 * n:]
        x, y, c, _, _ = _place()
        cps = []
        for f in range(n):
            hl = grads[f].shape[0] // 2
            cp = _remote(ins[f].at[pl.ds((1 - c) * hl, hl)], outs[f], send.at[f], recv.at[f], (x, y, 1 - c))
            cp.start()
            cps.append(cp)
        for cp in cps:
            cp.wait()

    sems = pltpu.SemaphoreType.DMA((n,))
    return pl.pallas_call(
        body, name="pair_exchange",
        in_specs=[ANY] * n, out_specs=[ANY] * n,
        out_shape=[jax.ShapeDtypeStruct((g.shape[0] // 2,) + g.shape[1:], g.dtype) for g in grads],
        scratch_shapes=[sems, sems],
        compiler_params=pltpu.CompilerParams(has_side_effects=True),
    )(*grads)


def chip_exchange(parts):
    n = len(parts)

    def body(*refs):
        ins, outs = refs[:n], refs[n:2 * n]
        send, recv, lsem = refs[2 * n:]
        x, y, c, me, chips = _place()
        local, sent = [], []
        for f in range(n):
            hl = parts[f].shape[0]
            rows = pl.ds(0, hl)
            cp = pltpu.make_async_copy(ins[f].at[rows, me], outs[f].at[rows, me], lsem.at[f])
            cp.start()
            local.append(cp)
            for j, (qx, qy) in enumerate(chips):
                cp = _remote(ins[f].at[rows, 2 * qx + qy], outs[f].at[rows, me],
                             send.at[3 * f + j], recv.at[3 * f + j], (qx, qy, c))
                cp.start()
                sent.append(cp)
        for f in range(n):
            rows = pl.ds(0, parts[f].shape[0])
            for j, (qx, qy) in enumerate(chips):
                slab = outs[f].at[rows, 2 * qx + qy]
                _remote(slab, slab, send.at[3 * f + j], recv.at[3 * f + j], (x, y, c)).wait_recv()
        for cp in sent:
            cp.wait_send()
        for cp in local:
            cp.wait()

    sems = pltpu.SemaphoreType.DMA((3 * n,))
    return pl.pallas_call(
        body, name="chip_exchange",
        in_specs=[ANY] * n, out_specs=[ANY] * n,
        out_shape=[jax.ShapeDtypeStruct(s.shape, s.dtype) for s in parts],
        scratch_shapes=[sems, sems, pltpu.SemaphoreType.DMA((n,))],
        compiler_params=pltpu.CompilerParams(has_side_effects=True),
    )(*parts)


def half_swap(grads):
    n = len(grads)

    def body(*refs):
        ins, outs = refs[:n], refs[n:2 * n]
        send, recv = refs[2 * n:]
        x, y, c, _, _ = _place()
        cps = []
        for f in range(n):
            hl = grads[f].shape[0] // 2
            mine = pl.ds(c * hl, hl)
            cp = _remote(outs[f].at[mine], outs[f].at[mine], send.at[f], recv.at[f], (x, y, 1 - c))
            cp.start()
            cps.append(cp)
        for f in range(n):
            hl = grads[f].shape[0] // 2
            theirs = outs[f].at[pl.ds((1 - c) * hl, hl)]
            _remote(theirs, theirs, send.at[f], recv.at[f], (x, y, c)).wait_recv()
        for cp in cps:
            cp.wait_send()

    sems = pltpu.SemaphoreType.DMA((n,))
    return pl.pallas_call(
        body, name="half_swap",
        in_specs=[ANY] * n, out_specs=[ANY] * n,
        out_shape=[jax.ShapeDtypeStruct(g.shape, g.dtype) for g in grads],
        input_output_aliases={f: f for f in range(n)},
        scratch_shapes=[sems, sems],
        compiler_params=pltpu.CompilerParams(has_side_effects=True),
    )(*grads)


def all_sum_small(v):
    R = v.shape[0]

    def body(v_ref, o_ref, land, send, recv):
        x, y, c, _, _ = _place()
        me = 4 * x + 2 * y + c
        land[me] = v_ref[...]
        peers = [(px, py, pc) for px in range(2) for py in range(2) for pc in range(2)]
        cps = []
        for k in range(1, 8):
            dev = (x ^ (k >> 2), y ^ ((k >> 1) & 1), c ^ (k & 1))
            cp = _remote(v_ref, land.at[me], send.at[k - 1], recv.at[k - 1], dev)
            cp.start()
            cps.append(cp)
        for k in range(1, 8):
            src = 4 * (x ^ (k >> 2)) + 2 * (y ^ ((k >> 1) & 1)) + (c ^ (k & 1))
            _remote(v_ref, land.at[src], send.at[k - 1], recv.at[k - 1], (x, y, c)).wait_recv()
        for cp in cps:
            cp.wait_send()
        tot = land[0]
        for d in range(1, len(peers)):
            tot = tot + land[d]
        o_ref[...] = tot

    sems = pltpu.SemaphoreType.DMA((7,))
    vm = pl.BlockSpec(memory_space=pltpu.VMEM)
    return pl.pallas_call(
        body, name="all_sum_small", in_specs=[vm], out_specs=vm,
        out_shape=jax.ShapeDtypeStruct(v.shape, F32),
        scratch_shapes=[pltpu.VMEM((8, R, 128), F32), sems, sems],
        compiler_params=pltpu.CompilerParams(has_side_effects=True),
    )(v)


def _row_tile(R):
    for t in (512, 256, 128, 64, 32, 16):
        if R % t == 0:
            return t
    return R


def pair_add(cidx, grad, recv):
    hl, P, R, C = recv.shape
    tr = _row_tile(R)

    def body(c_ref, a_ref, b_ref, o_ref):
        o_ref[...] = (a_ref[...].astype(F32) + b_ref[...].astype(F32)).astype(o_ref.dtype)

    blk = (None, None, tr, C)
    return pl.pallas_call(
        body, name="pair_add",
        grid_spec=pltpu.PrefetchScalarGridSpec(
            num_scalar_prefetch=1, grid=(hl, P, R // tr),
            in_specs=[pl.BlockSpec(blk, lambda l, p, r, c: (c[0] * hl + l, p, r, 0)),
                      pl.BlockSpec(blk, lambda l, p, r, c: (l, p, r, 0))],
            out_specs=pl.BlockSpec(blk, lambda l, p, r, c: (l, p, r, 0))),
        out_shape=jax.ShapeDtypeStruct(recv.shape, recv.dtype),
        compiler_params=_params(("parallel", "parallel", "parallel")),
    )(cidx, grad, recv)


def chip_sum(cidx, land, L):
    hl, P, R, C = land.shape
    tr = _row_tile(R)

    def body(c_ref, a_ref, o_ref):
        tot = a_ref[0].astype(F32)
        for q in range(1, P):
            tot = tot + a_ref[q].astype(F32)
        o_ref[...] = tot

    return pl.pallas_call(
        body, name="chip_sum",
        grid_spec=pltpu.PrefetchScalarGridSpec(
            num_scalar_prefetch=1, grid=(hl, R // tr),
            in_specs=[pl.BlockSpec((None, P, tr, C), lambda l, r, c: (l, 0, r, 0))],
            out_specs=pl.BlockSpec((None, tr, C), lambda l, r, c: (c[0] * hl + l, r, 0))),
        out_shape=jax.ShapeDtypeStruct((L, R, C), F32),
        compiler_params=_params(("parallel", "parallel")),
    )(cidx, land)


def adamw(w, g, m, v):
    L, R, C = w.shape
    tr = _row_tile(R)
    c1 = 1.0 / (1.0 - ADAM_B1 ** ADAM_STEP)
    c2 = 1.0 / (1.0 - ADAM_B2 ** ADAM_STEP)

    def body(w_ref, g_ref, m_ref, v_ref, d_ref, nm_ref, nv_ref):
        gv = g_ref[...]
        nm = ADAM_B1 * m_ref[...] + (1.0 - ADAM_B1) * gv
        nv = ADAM_B2 * v_ref[...] + (1.0 - ADAM_B2) * (gv * gv)
        nm_ref[...] = nm
        nv_ref[...] = nv
        d_ref[...] = -ADAM_LR * ((nm * c1) / (jnp.sqrt(nv * c2) + ADAM_EPS) + ADAM_WD * w_ref[...])

    blk = pl.BlockSpec((None, tr, C), lambda l, r: (l, r, 0))
    osh = jax.ShapeDtypeStruct((L, R, C), F32)
    return pl.pallas_call(
        body, name="adamw", grid=(L, R // tr),
        in_specs=[blk, blk, blk, blk], out_specs=[blk, blk, blk], out_shape=[osh, osh, osh],
        compiler_params=_params(("parallel", "parallel")),
    )(w, g, m, v)


def kernel(x, w_qkv_a, w_o_a, rel_bias, w_qkv_b, w_o_b, ffn_w_gate, ffn_w_up, ffn_w_down, ln_g, ln_b, loss_target, m_w_qkv_a, m_w_o_a, m_rel_bias, m_w_qkv_b, m_w_o_b, m_ffn_w_gate, m_ffn_w_up, m_ffn_w_down, m_ln_g, m_ln_b, v_w_qkv_a, v_w_o_a, v_rel_bias, v_w_qkv_b, v_w_o_b, v_ffn_w_gate, v_ffn_w_up, v_ffn_w_down, v_ln_g, v_ln_b):
    B, S, D = x.shape
    M = B * S
    depth = ffn_w_gate.shape[0]
    n_ffn = 2 * depth
    H = D // HEAD_DIM
    HP = H // 2
    Fs = ffn_w_gate.shape[-1]
    alpha = (2.0 * depth) ** 0.25
    assert S % QB_A == 0 and S % SB_TILE == 0 and rel_bias.shape == (N_REL, H)

    def mixer_weights(i):
        wq, wo = (w_qkv_a, w_o_a) if i % 2 == 0 else (w_qkv_b, w_o_b)
        return wq[i // 2:i // 2 + 1], wo[i // 2:i // 2 + 1]

    me = 2 * lax.axis_index("x") + lax.axis_index("y")
    ids = jnp.stack([lax.axis_index("c"), me]).astype(jnp.int32)
    placed = [[cast_place(ids, w) for w in (*mixer_weights(i), ffn_w_gate[i], ffn_w_up[i], ffn_w_down[i])]
              for i in range(depth)]
    in_flight, _ = gather_start([[s for s, _ in lay] for lay in placed], [[z for _, z in lay] for lay in placed])
    lng_p, lnb_p = gather_weights([ln_g, ln_b])
    lng = jnp.moveaxis(lng_p, 0, 2).reshape(depth, 3, 1, D)
    lnb = jnp.moveaxis(lnb_p, 0, 2).reshape(depth, 3, 1, D)

    tab_t = jnp.pad(rel_bias.T, ((0, 0), (0, REL_PAD - N_REL)))
    vr = bias_vec(tab_t).reshape(HP, 2, VR_W)

    xf = x.reshape(M, D)
    xb = xf.astype(BF16)
    saved, weights = [], []
    for i in range(depth):
        send, recv, thru, lands = in_flight[i]
        _, lands = gather_wait(i, send, recv, thru, lands, xf)
        wq, wo, wg, wu, wd = gather_forward(i, lands)
        weights.append((wq, wo, wg, wu, wd))
        for j in range(3):
            gam, bet = lng[i, j], lnb[i, j]
            if j != 1:
                l = 0 if j == 0 else 1
                g, u, h = ffn_up(xb, wg, wu, l)
                xo, xob, xhat, rstd = mm_ln(h, wd, l, xf, gam, bet, alpha, 0.5, True)
                saved.append(("ffn", i, l, xb, g, u, h, xhat, rstd, gam))
            elif i % 2 == 0:
                qkv = qkv_proj(xb, wq, 0)
                o = attn_a_fwd(qkv, vr, B, S)
                xo, xob, xhat, rstd = mm_ln(o, wo, 0, xf, gam, bet, alpha, 1.0, False)
                saved.append(("a", i, 0, xb, qkv, o, None, xhat, rstd, gam))
            else:
                qkv = qkv_proj(xb, wq, 0)
                o, ntot = attn_b_fwd(qkv, B, S)
                xo, xob, xhat, rstd = mm_ln(o, wo, 0, xf, gam, bet, alpha, 1.0, False)
                saved.append(("b", i, 0, xb, qkv, o, ntot, xhat, rstd, gam))
            xf, xb = xo, xob

    dy, loss_part = loss_head(xf, loss_target.reshape(M, D))
    loss = lax.psum(loss_part[0, 0], ("x", "y", "c"))

    Cq, Ro = w_qkv_a.shape[-1], w_o_a.shape[1]
    dgam, dbet, dvrs, reducing = [], [], [], {}
    started = None
    for i in reversed(range(depth)):
        wq, wo, wg, wu, wd = weights[i]
        gq = lax.empty((1, N_CHIPS, D, Cq), BF16)
        go = lax.empty((1, N_CHIPS, Ro, D), BF16)
        ggate = lax.empty((2, N_CHIPS, D, Fs), BF16)
        gup = lax.empty((2, N_CHIPS, D, Fs), BF16)
        gdown = lax.empty((2, N_CHIPS, Fs, D), BF16)
        for rec in reversed(saved[3 * i:3 * i + 3]):
            kind, _, l, xb_in, t1, t2, t3, xhat, rstd, gam = rec
            scale = 0.5 if kind == "ffn" else 1.0
            if started is not None:
                gam = gam + started[0, 0]
                started = None
            dpre, dyb, dg_, db_ = ln_bwd(dy, xhat, rstd, gam, scale)
            dgam.append(dg_)
            dbet.append(db_)
            if kind == "ffn":
                g, u, h = t1, t2, t3
                dg, du, gdown, ggate, gup = ffn_bwd(gdown, ggate, gup, l, xb_in, dyb, wd, g, u, h)
                dy = ffn_dx(dg, du, wg, wu, l, dpre, alpha)
            else:
                qkv, o = t1, t2
                do = o_proj_bwd(dyb, wo, l)
                go = dw_o(go, l, o, dyb)
                if kind == "a":
                    dq, dk, dv, dvr = attn_a_bwd(qkv, vr, do, B, S)
                    dvrs.append(dvr.reshape(B, H, VR_W))
                else:
                    dq, dk, dv = attn_b_bwd(qkv, do, t3, B, S)
                dqkv = jnp.concatenate([dq, dk, dv], axis=1)
                gq = dw_qkv(gq, l, xb_in, dqkv)
                dy = qkv_proj_bwd(dqkv, wq, l, dpre, alpha)
        fams = [gq, go, ggate, gup, gdown]
        from_sib = pair_exchange_rows(i, fams)
        parts = [pair_add_rows(ids, g_, r_) for g_, r_ in zip(fams, from_sib)]
        reducing[i] = reduce_start(i, parts)
        started = reducing[i][4]
    grad_x = dy.reshape(B, S, D)

    la, lb = w_qkv_a.shape[0], w_qkv_b.shape[0]
    full = {"qa": lax.empty((la, D, Cq), F32), "oa": lax.empty((la, Ro, D), F32),
            "qb": lax.empty((lb, D, Cq), F32), "ob": lax.empty((lb, Ro, D), F32),
            "gate": lax.empty((n_ffn, D, Fs), F32), "up": lax.empty((n_ffn, D, Fs), F32),
            "down": lax.empty((n_ffn, Fs, D), F32)}
    for i in reversed(range(depth)):
        send, recv, parts, lands, _ = reducing[i]
        parts, lands = reduce_wait(i, send, recv, parts, lands, dy)
        mix = "a" if i % 2 == 0 else "b"
        targets = [("q" + mix, i // 2), ("o" + mix, i // 2), ("gate", 2 * i), ("up", 2 * i), ("down", 2 * i)]
        for (name, l0), part, land in zip(targets, parts, lands):
            full[name] = chip_sum_rows(ids, land, part, full[name], l0)
    g_qa, g_oa, g_qb, g_ob, g_gate, g_up, g_down = half_swap_rows(
        [full[k] for k in ("qa", "oa", "qb", "ob", "gate", "up", "down")])

    d_tab_t = bias_vec_bwd(jnp.concatenate(dvrs, axis=0))
    small = jnp.concatenate([d_tab_t.reshape(-1)] + [p.reshape(-1) for p in dgam[::-1]]
                            + [p.reshape(-1) for p in dbet[::-1]])
    n_small = small.shape[0]
    rows = -(-n_small // (8 * 128)) * 8
    tot = all_sum_small(jnp.pad(small, (0, rows * 128 - n_small)).reshape(rows, 128)).reshape(-1)
    n_tab, n_ln = H * REL_PAD, 3 * depth * D
    g_rel = tot[:n_tab].reshape(H, REL_PAD)[:, :N_REL].T
    ln_cols = D // N_CHIPS

    def ln_shard(flat):
        return lax.dynamic_slice_in_dim(flat.reshape(depth, 3, D), me * ln_cols, ln_cols, axis=2)

    g_lng = ln_shard(tot[n_tab:n_tab + n_ln])
    g_lnb = ln_shard(tot[n_tab + n_ln:n_tab + 2 * n_ln])

    def upd(w, g, m, v):
        shp = w.shape
        w3, m3, v3 = (a.reshape(g.shape) for a in (w, m, v))
        d, nm, nv = adamw(w3, g, m3, v3)
        return g.reshape(shp), d.reshape(shp), nm.reshape(shp), nv.reshape(shp)

    res = [
        upd(w_qkv_a, g_qa, m_w_qkv_a, v_w_qkv_a),
        upd(w_o_a, g_oa, m_w_o_a, v_w_o_a),
        upd(rel_bias, g_rel.reshape(1, N_REL, H), m_rel_bias, v_rel_bias),
        upd(w_qkv_b, g_qb, m_w_qkv_b, v_w_qkv_b),
        upd(w_o_b, g_ob, m_w_o_b, v_w_o_b),
        upd(ffn_w_gate, g_gate, m_ffn_w_gate, v_ffn_w_gate),
        upd(ffn_w_up, g_up, m_ffn_w_up, v_ffn_w_up),
        upd(ffn_w_down, g_down, m_ffn_w_down, v_ffn_w_down),
        upd(ln_g, g_lng, m_ln_g, v_ln_g),
        upd(ln_b, g_lnb, m_ln_b, v_ln_b),
    ]
    grads = [r[0] for r in res]
    deltas = [r[1] for r in res]
    new_m = [r[2] for r in res]
    new_v = [r[3] for r in res]
    return (loss, grad_x, *grads, *deltas, *new_m, *new_v)
```

```python
import functools
import math

import jax
import jax.numpy as jnp
from jax import lax
from jax.experimental import pallas as pl
from jax.experimental.pallas import tpu as pltpu

F32 = jnp.float32
BF16 = jnp.bfloat16
MESH = pl.DeviceIdType.MESH

N_CHIPS = 4
HEAD_DIM = 64
CHUNK = 64
LEFT_CHUNKS = 8
LOOKBACK = LEFT_CHUNKS * CHUNK
REL_CLIP = 128
N_REL = 2 * REL_CLIP + 1
REL_PAD = 384
SB_TILE = 256
QB_A = 256
KW_A = QB_A + LOOKBACK
VR_W = 1024
VR_C0 = KW_A - 1
LN_EPS = 1e-5
ADAM_LR, ADAM_B1, ADAM_B2, ADAM_EPS, ADAM_WD, ADAM_STEP = 0.001, 0.9, 0.999, 1e-08, 0.01, 10
NEG = -1e30
VMEM_LIMIT = 56 * 1024 * 1024

NT_DIMS = (((1,), (1,)), ((), ()))
TN_DIMS = (((0,), (0,)), ((), ()))
ANY = pl.BlockSpec(memory_space=pl.ANY)


def _params(sem=None):
    return pltpu.CompilerParams(dimension_semantics=sem, vmem_limit_bytes=VMEM_LIMIT)


def _tile(n, pref):
    t = min(n, pref)
    assert n % t == 0, (n, pref)
    return t


def _sigmoid(z):
    return 1.0 / (1.0 + jnp.exp(-z))


def _mm_call(name, operands, in_specs, out_shape, out_spec, grid, dims_list, acc_shape,
             add_coef=None, aliases=None):
    n_pairs = len(dims_list)
    nk = grid[-1]
    has_add = add_coef is not None
    n_alias = len(aliases) if aliases else 0

    def body(*refs):
        pair_refs = refs[:2 * n_pairs]
        pos = 2 * n_pairs
        add_ref = refs[pos] if has_add else None
        pos += (1 if has_add else 0) + n_alias
        o_ref = refs[pos]
        acc_ref = refs[pos + 1] if nk > 1 else None

        def product():
            part = None
            for i, dims in enumerate(dims_list):
                d = lax.dot_general(pair_refs[2 * i][...], pair_refs[2 * i + 1][...], dims,
                                    preferred_element_type=F32)
                part = d if part is None else part + d
            return part

        def finish(r):
            if has_add:
                r = r + add_coef * add_ref[...]
            o_ref[...] = r.astype(o_ref.dtype)

        if nk == 1:
            finish(product())
        else:
            k = pl.program_id(len(grid) - 1)

            @pl.when(k == 0)
            def _():
                acc_ref[...] = jnp.zeros_like(acc_ref)

            acc_ref[...] += product()

            @pl.when(k == nk - 1)
            def _():
                finish(acc_ref[...])

    sem = ("parallel",) * (len(grid) - 1) + ("arbitrary",)
    return pl.pallas_call(
        body, name=name, grid=grid, in_specs=in_specs, out_specs=out_spec, out_shape=out_shape,
        scratch_shapes=[pltpu.VMEM(acc_shape, F32)] if nk > 1 else [],
        input_output_aliases=aliases or {},
        compiler_params=_params(sem),
    )(*operands)


def qkv_proj(xb, w, l):
    M, D = xb.shape
    C = w.shape[-1]
    tm = _tile(M, 512)
    return _mm_call(
        "qkv_proj", (xb, w),
        [pl.BlockSpec((tm, D), lambda p, i, k: (i, 0)),
         pl.BlockSpec((None, None, D, C), lambda p, i, k: (p, l, 0, 0))],
        jax.ShapeDtypeStruct((M, N_CHIPS * C), BF16),
        pl.BlockSpec((tm, C), lambda p, i, k: (i, p)),
        (N_CHIPS, M // tm, 1), [(((1,), (0,)), ((), ()))], None)


def o_proj_bwd(dyb, w, l):
    M, D = dyb.shape
    R = w.shape[2]
    tm = _tile(M, 512)
    return _mm_call(
        "o_proj_bwd", (dyb, w),
        [pl.BlockSpec((tm, D), lambda p, i, k: (i, 0)),
         pl.BlockSpec((None, None, R, D), lambda p, i, k: (p, l, 0, 0))],
        jax.ShapeDtypeStruct((M, N_CHIPS * R), BF16),
        pl.BlockSpec((tm, R), lambda p, i, k: (i, p)),
        (N_CHIPS, M // tm, 1), [NT_DIMS], None)


def qkv_proj_bwd(dqkv, w, l, dpre, alpha):
    M = dqkv.shape[0]
    D, C = w.shape[2], w.shape[3]
    tm = _tile(M, 512)

    def body(a_ref, w_ref, add_ref, o_ref):
        acc = alpha * add_ref[...]
        for p in range(N_CHIPS):
            acc = acc + lax.dot_general(a_ref[:, p * C:(p + 1) * C], w_ref[p], NT_DIMS,
                                        preferred_element_type=F32)
        o_ref[...] = acc

    row = pl.BlockSpec((tm, D), lambda i: (i, 0))
    return pl.pallas_call(
        body, name="qkv_proj_bwd", grid=(M // tm,),
        in_specs=[pl.BlockSpec((tm, N_CHIPS * C), lambda i: (i, 0)),
                  pl.BlockSpec((N_CHIPS, None, D, C), lambda i: (0, l, 0, 0)), row],
        out_specs=row, out_shape=jax.ShapeDtypeStruct((M, D), F32),
        compiler_params=_params(("parallel",)),
    )(dqkv, w, dpre)


def ffn_dx(dg, du, wg, wu, l, dpre, alpha):
    _, M, Fs = dg.shape
    D = wg.shape[2]
    tm = _tile(M, 256)

    def body(dg_ref, wg_ref, du_ref, wu_ref, add_ref, o_ref):
        acc = alpha * add_ref[...]
        for p in range(N_CHIPS):
            acc = acc + lax.dot_general(dg_ref[p], wg_ref[p], NT_DIMS, preferred_element_type=F32)
            acc = acc + lax.dot_general(du_ref[p], wu_ref[p], NT_DIMS, preferred_element_type=F32)
        o_ref[...] = acc

    act = pl.BlockSpec((N_CHIPS, tm, Fs), lambda i: (0, i, 0))
    wsp = pl.BlockSpec((N_CHIPS, None, D, Fs), lambda i: (0, l, 0, 0))
    row = pl.BlockSpec((tm, D), lambda i: (i, 0))
    return pl.pallas_call(
        body, name="ffn_dx", grid=(M // tm,),
        in_specs=[act, wsp, act, wsp, row],
        out_specs=row, out_shape=jax.ShapeDtypeStruct((M, D), F32),
        compiler_params=_params(("parallel",)),
    )(dg, wg, du, wu, dpre)


def _dw_call(name, buf, l, a, b, a_spec, b_spec, M, tk):
    _, _, R, C = buf.shape
    return _mm_call(
        name, (a, b, buf),
        [a_spec, b_spec, ANY],
        jax.ShapeDtypeStruct(buf.shape, buf.dtype),
        pl.BlockSpec((None, None, R, C), lambda p, k: (l, p, 0, 0)),
        (N_CHIPS, M // tk), [TN_DIMS], (R, C), aliases={2: 0})


def ffn_bwd(gdown, ggate, gup, l, xb, dyb, wd, g, u, h):
    M, D = dyb.shape
    Fs = wd.shape[2]
    tm = _tile(M, 512)
    n = M // tm

    def body(x_ref, dy_ref, wd_ref, g_ref, u_ref, h_ref, _gd, _gg, _gu,
             dg_ref, du_ref, gd_ref, gg_ref, gu_ref, acc_d, acc_g, acc_u):
        i = pl.program_id(1)

        @pl.when(i == 0)
        def _():
            acc_d[...] = jnp.zeros_like(acc_d)
            acc_g[...] = jnp.zeros_like(acc_g)
            acc_u[...] = jnp.zeros_like(acc_u)

        dy = dy_ref[...]
        dh = lax.dot_general(dy, wd_ref[...], NT_DIMS, preferred_element_type=F32)
        gf = g_ref[...].astype(F32)
        sig = _sigmoid(gf)
        silu = gf * sig
        dg = (dh * u_ref[...].astype(F32) * (sig * (1.0 + gf - silu))).astype(BF16)
        du = (dh * silu).astype(BF16)
        dg_ref[...] = dg
        du_ref[...] = du
        x = x_ref[...]
        acc_d[...] += lax.dot_general(h_ref[...], dy, TN_DIMS, preferred_element_type=F32)
        acc_g[...] += lax.dot_general(x, dg, TN_DIMS, preferred_element_type=F32)
        acc_u[...] += lax.dot_general(x, du, TN_DIMS, preferred_element_type=F32)

        @pl.when(i == n - 1)
        def _():
            gd_ref[...] = acc_d[...].astype(gd_ref.dtype)
            gg_ref[...] = acc_g[...].astype(gg_ref.dtype)
            gu_ref[...] = acc_u[...].astype(gu_ref.dtype)

    row = pl.BlockSpec((tm, D), lambda p, i: (i, 0))
    act = pl.BlockSpec((None, tm, Fs), lambda p, i: (p, i, 0))
    ash = jax.ShapeDtypeStruct((N_CHIPS, M, Fs), BF16)
    down_blk = pl.BlockSpec((None, None, Fs, D), lambda p, i: (l, p, 0, 0))
    up_blk = pl.BlockSpec((None, None, D, Fs), lambda p, i: (l, p, 0, 0))
    return pl.pallas_call(
        body, name="ffn_bwd", grid=(N_CHIPS, n),
        in_specs=[row, row, pl.BlockSpec((None, None, Fs, D), lambda p, i: (p, l, 0, 0)),
                  act, act, act, ANY, ANY, ANY],
        out_specs=[act, act, down_blk, up_blk, up_blk],
        out_shape=[ash, ash] + [jax.ShapeDtypeStruct(b.shape, b.dtype) for b in (gdown, ggate, gup)],
        scratch_shapes=[pltpu.VMEM((Fs, D), F32), pltpu.VMEM((D, Fs), F32), pltpu.VMEM((D, Fs), F32)],
        input_output_aliases={6: 2, 7: 3, 8: 4},
        compiler_params=_params(("parallel", "arbitrary")),
    )(xb, dyb, wd, g, u, h, gdown, ggate, gup)


def dw_qkv(buf, l, xb, dqkv):
    M, D = xb.shape
    C = buf.shape[-1]
    tk = _tile(M, 1024)
    return _dw_call("dw_qkv", buf, l, xb, dqkv,
                    pl.BlockSpec((tk, D), lambda p, k: (k, 0)),
                    pl.BlockSpec((tk, C), lambda p, k: (k, p)), M, tk)


def dw_o(buf, l, o, dyb):
    M, D = dyb.shape
    R = buf.shape[2]
    tk = _tile(M, 1024)
    return _dw_call("dw_o", buf, l, o, dyb,
                    pl.BlockSpec((tk, R), lambda p, k: (k, p)),
                    pl.BlockSpec((tk, D), lambda p, k: (k, 0)), M, tk)


def ffn_up(xb, wg, wu, l):
    M, D = xb.shape
    Fs = wg.shape[-1]
    tm = _tile(M, 512)

    def body(x_ref, wg_ref, wu_ref, g_ref, u_ref, h_ref):
        x = x_ref[...]
        g = jnp.dot(x, wg_ref[...], preferred_element_type=F32)
        u = jnp.dot(x, wu_ref[...], preferred_element_type=F32)
        g_ref[...] = g.astype(BF16)
        u_ref[...] = u.astype(BF16)
        h_ref[...] = (g * _sigmoid(g) * u).astype(BF16)

    wsp = pl.BlockSpec((None, None, D, Fs), lambda p, i: (p, l, 0, 0))
    osp = pl.BlockSpec((None, tm, Fs), lambda p, i: (p, i, 0))
    osh = jax.ShapeDtypeStruct((N_CHIPS, M, Fs), BF16)
    return pl.pallas_call(
        body, name="ffn_up", grid=(N_CHIPS, M // tm),
        in_specs=[pl.BlockSpec((tm, D), lambda p, i: (i, 0)), wsp, wsp],
        out_specs=[osp, osp, osp], out_shape=[osh, osh, osh],
        compiler_params=_params(("parallel", "parallel")),
    )(xb, wg, wu)


def mm_ln(a, w, l, x, gam, bet, alpha, scale, a_piece_major):
    M, D = x.shape
    R = w.shape[2]
    tm = _tile(M, 256)
    if a_piece_major:
        a_spec = pl.BlockSpec((N_CHIPS, tm, R), lambda i: (0, i, 0))
    else:
        a_spec = pl.BlockSpec((tm, N_CHIPS * R), lambda i: (i, 0))

    def body(a_ref, w_ref, x_ref, g_ref, b_ref, xo_ref, xb_ref, xh_ref, rs_ref):
        y = None
        for p in range(N_CHIPS):
            a = a_ref[p] if a_piece_major else a_ref[:, p * R:(p + 1) * R]
            d = jnp.dot(a, w_ref[p], preferred_element_type=F32)
            y = d if y is None else y + d
        pre = alpha * x_ref[...] + scale * y
        mu = jnp.mean(pre, axis=-1, keepdims=True)
        cen = pre - mu
        var = jnp.mean(cen * cen, axis=-1, keepdims=True)
        rstd = lax.rsqrt(var + LN_EPS)
        xhat = cen * rstd
        out = xhat * g_ref[...] + b_ref[...]
        xo_ref[...] = out
        xb_ref[...] = out.astype(BF16)
        xh_ref[...] = xhat
        rs_ref[...] = rstd

    row = pl.BlockSpec((tm, D), lambda i: (i, 0))
    vec = pl.BlockSpec((1, D), lambda i: (0, 0))
    return pl.pallas_call(
        body, name="mm_ln", grid=(M // tm,),
        in_specs=[a_spec, pl.BlockSpec((N_CHIPS, None, R, D), lambda i: (0, l, 0, 0)), row, vec, vec],
        out_specs=[row, row, row, pl.BlockSpec((tm, 1), lambda i: (i, 0))],
        out_shape=[jax.ShapeDtypeStruct((M, D), F32), jax.ShapeDtypeStruct((M, D), BF16),
                   jax.ShapeDtypeStruct((M, D), F32), jax.ShapeDtypeStruct((M, 1), F32)],
        compiler_params=_params(("parallel",)),
    )(a, w, x, gam, bet)


def ln_bwd(dy, xhat, rstd, gam, scale, dgam, dbet, row):
    M, D = dy.shape
    tm = _tile(M, 512)

    def body(dy_ref, xh_ref, rs_ref, g_ref, _dg, _db, dp_ref, db16_ref, dg_ref, dbt_ref):
        i = pl.program_id(0)
        dy_v = dy_ref[...]
        xh = xh_ref[...]
        dxh = dy_v * g_ref[...]
        m1 = jnp.mean(dxh, axis=-1, keepdims=True)
        m2 = jnp.mean(dxh * xh, axis=-1, keepdims=True)
        dpre = rs_ref[...] * (dxh - m1 - xh * m2)
        dp_ref[...] = dpre
        db16_ref[...] = (scale * dpre).astype(BF16)
        dgp = jnp.sum(dy_v * xh, axis=0, keepdims=True)
        dbp = jnp.sum(dy_v, axis=0, keepdims=True)

        @pl.when(i == 0)
        def _():
            dg_ref[...] = dgp
            dbt_ref[...] = dbp

        @pl.when(i > 0)
        def _():
            dg_ref[...] += dgp
            dbt_ref[...] += dbp

    tok = pl.BlockSpec((tm, D), lambda i: (i, 0))
    vec = pl.BlockSpec((1, D), lambda i: (0, 0))
    acc = pl.BlockSpec((None, 1, D), lambda i: (row, 0, 0))
    return pl.pallas_call(
        body, name="ln_bwd", grid=(M // tm,),
        in_specs=[tok, tok, pl.BlockSpec((tm, 1), lambda i: (i, 0)), vec, ANY, ANY],
        out_specs=[tok, tok, acc, acc],
        out_shape=[jax.ShapeDtypeStruct((M, D), F32), jax.ShapeDtypeStruct((M, D), BF16),
                   jax.ShapeDtypeStruct(dgam.shape, F32), jax.ShapeDtypeStruct(dbet.shape, F32)],
        input_output_aliases={4: 2, 5: 3},
        compiler_params=_params(("arbitrary",)),
    )(dy, xhat, rstd, gam, dgam, dbet)


def loss_head(y, tgt):
    M, D = y.shape
    tm = _tile(M, 512)
    n = M // tm

    def body(y_ref, t_ref, dy_ref, l_ref, acc_ref):
        i = pl.program_id(0)
        e = y_ref[...] - t_ref[...]
        dy_ref[...] = e * (1.0 / D)
        part = jnp.sum(e * e, axis=0, keepdims=True)

        @pl.when(i == 0)
        def _():
            acc_ref[...] = part

        @pl.when(i > 0)
        def _():
            acc_ref[...] += part

        @pl.when(i == n - 1)
        def _():
            l_ref[...] = (0.5 / D) * jnp.sum(acc_ref[...], axis=1, keepdims=True)

    row = pl.BlockSpec((tm, D), lambda i: (i, 0))
    return pl.pallas_call(
        body, name="loss_head", grid=(n,),
        in_specs=[row, row],
        out_specs=[row, pl.BlockSpec((1, 1), lambda i: (0, 0))],
        out_shape=[jax.ShapeDtypeStruct((M, D), F32), jax.ShapeDtypeStruct((1, 1), F32)],
        scratch_shapes=[pltpu.VMEM((1, D), F32)],
        compiler_params=_params(("arbitrary",)),
    )(y, tgt)


def _rel_onehot_t():
    r = lax.broadcasted_iota(jnp.int32, (REL_PAD, VR_W), 0)
    n = lax.broadcasted_iota(jnp.int32, (REL_PAD, VR_W), 1)
    idx = jnp.clip(VR_C0 - n, -REL_CLIP, REL_CLIP) + REL_CLIP
    return (r == idx).astype(F32)


def bias_vec(tab_t):
    H = tab_t.shape[0]

    def body(t_ref, o_ref):
        o_ref[...] = jnp.dot(t_ref[...], _rel_onehot_t(), precision=lax.Precision.HIGHEST,
                             preferred_element_type=F32)

    return pl.pallas_call(
        body, name="bias_vec", out_shape=jax.ShapeDtypeStruct((H, VR_W), F32),
        compiler_params=_params(),
    )(tab_t)


def bias_vec_bwd(dvr):
    n, H, _ = dvr.shape

    def body(d_ref, o_ref):
        tot = d_ref[0]
        for i in range(1, n):
            tot = tot + d_ref[i]
        o_ref[...] = lax.dot_general(tot, _rel_onehot_t(), NT_DIMS, precision=lax.Precision.HIGHEST,
                                     preferred_element_type=F32)

    return pl.pallas_call(
        body, name="bias_vec_bwd", out_shape=jax.ShapeDtypeStruct((H, REL_PAD), F32),
        compiler_params=_params(),
    )(dvr)


def _a_bias_mask(vr_row):
    xb = jnp.broadcast_to(vr_row, (QB_A, VR_W))
    tile = pltpu.roll(xb, VR_W - (QB_A - 1), 1, stride=1, stride_axis=0)[:, :KW_A]
    qc = lax.broadcasted_iota(jnp.int32, (QB_A, KW_A), 0) // CHUNK
    kc = lax.broadcasted_iota(jnp.int32, (QB_A, KW_A), 1) // CHUNK
    valid = (kc >= qc) & (kc <= qc + LEFT_CHUNKS)
    return jnp.where(valid, tile, NEG)


def _a_diag_sums(db_acc, h):
    acc8 = None
    for a in range(QB_A // 8):
        grp = db_acc[h, 8 * a:8 * a + 8, :]
        shift = QB_A - 8 - 8 * a
        if shift:
            grp = pltpu.roll(grp, shift, 1)
        acc8 = grp if acc8 is None else acc8 + grp
    sub = lax.broadcasted_iota(jnp.int32, (8, VR_W), 0)
    tot = jnp.zeros((8, VR_W), F32)
    for b in range(8):
        moved = pltpu.roll(acc8, 7 - b, 1) if b < 7 else acc8
        tot = tot + jnp.where(sub == b, moved, 0.0)
    return jnp.sum(tot, axis=0, keepdims=True)


def _a_blocks(S):
    out = []
    for qi in range(S // QB_A):
        q0 = qi * QB_A
        ks = max(0, q0 - LOOKBACK)
        out.append((q0, ks, q0 + QB_A, ks - (q0 - LOOKBACK)))
    return out


def _head_specs(S, HP):
    q = pl.BlockSpec((S, 2 * HEAD_DIM), lambda b, hp: (b, hp))
    k = pl.BlockSpec((S, 2 * HEAD_DIM), lambda b, hp: (b, HP + hp))
    v = pl.BlockSpec((S, 2 * HEAD_DIM), lambda b, hp: (b, 2 * HP + hp))
    return q, k, v


def attn_a_fwd(qkv, vr, B, S):
    D = qkv.shape[1] // 3
    HP = D // (2 * HEAD_DIM)
    scale = HEAD_DIM ** -0.5
    blocks = _a_blocks(S)

    def body(q_ref, k_ref, v_ref, vr_ref, o_ref):
        heads = [slice(h * HEAD_DIM, (h + 1) * HEAD_DIM) for h in range(2)]
        bms = [_a_bias_mask(vr_ref[h:h + 1, :]) for h in range(2)]
        for (q0, ks, ke, joff) in blocks:
            rows = slice(q0, q0 + QB_A)
            ss = [lax.dot_general(q_ref[rows, hs], k_ref[ks:ke, hs], NT_DIMS,
                                  preferred_element_type=F32) * scale + bm[:, joff:]
                  for hs, bm in zip(heads, bms)]
            ps = [jnp.exp(s - jnp.max(s, axis=-1, keepdims=True)) for s in ss]
            for hs, p in zip(heads, ps):
                den = jnp.sum(p, axis=-1, keepdims=True)
                o = jnp.dot(p.astype(BF16), v_ref[ks:ke, hs], preferred_element_type=F32) / den
                o_ref[rows, hs] = o.astype(BF16)

    qs, ks_, vs = _head_specs(S, HP)
    return pl.pallas_call(
        body, name="attn_a_fwd", grid=(B, HP),
        in_specs=[qs, ks_, vs, pl.BlockSpec((None, 2, VR_W), lambda b, hp: (hp, 0, 0))],
        out_specs=pl.BlockSpec((S, 2 * HEAD_DIM), lambda b, hp: (b, hp)),
        out_shape=jax.ShapeDtypeStruct((B * S, D), BF16),
        compiler_params=_params(("parallel", "parallel")),
    )(qkv, qkv, qkv, vr)


def attn_a_bwd(qkv, vr, do, B, S):
    D = qkv.shape[1] // 3
    HP = D // (2 * HEAD_DIM)
    scale = HEAD_DIM ** -0.5
    blocks = _a_blocks(S)

    def body(q_ref, k_ref, v_ref, vr_ref, do_ref, dq_ref, dk_ref, dv_ref, dvr_ref,
             dkt_acc, dvt_acc, db_acc):
        dkt_acc[...] = jnp.zeros_like(dkt_acc)
        dvt_acc[...] = jnp.zeros_like(dvt_acc)
        db_acc[...] = jnp.zeros_like(db_acc)
        heads = [slice(h * HEAD_DIM, (h + 1) * HEAD_DIM) for h in range(2)]
        bms = [_a_bias_mask(vr_ref[h:h + 1, :]) for h in range(2)]
        for (q0, ks, ke, joff) in blocks:
            rows = slice(q0, q0 + QB_A)
            qt_pair = q_ref[rows, :].astype(F32).T.astype(BF16)
            dot_pair = do_ref[rows, :].astype(F32).T.astype(BF16)
            ss = [lax.dot_general(q_ref[rows, hs], k_ref[ks:ke, hs], NT_DIMS,
                                  preferred_element_type=F32) * scale + bm[:, joff:]
                  for hs, bm in zip(heads, bms)]
            dps = [lax.dot_general(do_ref[rows, hs], v_ref[ks:ke, hs], NT_DIMS, preferred_element_type=F32)
                   for hs in heads]
            ps, dsbs = [], []
            for h, (s, dp) in enumerate(zip(ss, dps)):
                e = jnp.exp(s - jnp.max(s, axis=-1, keepdims=True))
                p = e / jnp.sum(e, axis=-1, keepdims=True)
                ds = p * (dp - jnp.sum(p * dp, axis=-1, keepdims=True))
                db_acc[h, :, joff:KW_A] += ds
                ps.append(p.astype(BF16))
                dsbs.append(ds.astype(BF16))
            for hs, p, dsb in zip(heads, ps, dsbs):
                dq = jnp.dot(dsb, k_ref[ks:ke, hs], preferred_element_type=F32) * scale
                dq_ref[rows, hs] = dq.astype(BF16)
                dkt_acc[hs, ks:ke] += jnp.dot(qt_pair[hs, :], dsb, preferred_element_type=F32)
                dvt_acc[hs, ks:ke] += jnp.dot(dot_pair[hs, :], p, preferred_element_type=F32)
        for h in range(2):
            dvr_ref[h:h + 1, :] = _a_diag_sums(db_acc, h)
        dk_ref[...] = (dkt_acc[...].T * scale).astype(BF16)
        dv_ref[...] = dvt_acc[...].T.astype(BF16)

    qs, ks_, vs = _head_specs(S, HP)
    hd = pl.BlockSpec((S, 2 * HEAD_DIM), lambda b, hp: (b, hp))
    osh = jax.ShapeDtypeStruct((B * S, D), BF16)
    return pl.pallas_call(
        body, name="attn_a_bwd", grid=(B, HP),
        in_specs=[qs, ks_, vs, pl.BlockSpec((None, 2, VR_W), lambda b, hp: (hp, 0, 0)), hd],
        out_specs=[hd, hd, hd, pl.BlockSpec((None, None, 2, VR_W), lambda b, hp: (b, hp, 0, 0))],
        out_shape=[osh, osh, osh, jax.ShapeDtypeStruct((B, HP, 2, VR_W), F32)],
        scratch_shapes=[pltpu.VMEM((2 * HEAD_DIM, S), F32), pltpu.VMEM((2 * HEAD_DIM, S), F32),
                        pltpu.VMEM((2, QB_A, VR_W), F32)],
        compiler_params=_params(("parallel", "parallel")),
    )(qkv, qkv, qkv, vr, do)


def _tri(cmp):
    j = lax.broadcasted_iota(jnp.int32, (SB_TILE, SB_TILE), 0)
    s = lax.broadcasted_iota(jnp.int32, (SB_TILE, SB_TILE), 1)
    return cmp(j, s).astype(BF16)


def _cumsum_mm(x, tri):
    hi = x.astype(BF16)
    mid = (x - hi.astype(F32)).astype(BF16)
    return jnp.dot(hi, tri, preferred_element_type=F32) + jnp.dot(mid, tri, preferred_element_type=F32)


def _sb_logs(q, k, scale, diagonal):
    z = lax.dot_general(q, k, NT_DIMS, preferred_element_type=F32) * scale
    log_b = jnp.minimum(z, 0.0) - jnp.log(1.0 + jnp.exp(-jnp.abs(z)))
    log_1mb = log_b - z
    if not diagonal:
        return log_b, log_1mb, None
    row = lax.broadcasted_iota(jnp.int32, (SB_TILE, SB_TILE), 0)
    col = lax.broadcasted_iota(jnp.int32, (SB_TILE, SB_TILE), 1)
    causal = col < row
    return log_b, jnp.where(causal, log_1mb, 0.0), causal


def attn_b_fwd(qkv, B, S):
    D = qkv.shape[1] // 3
    HP = D // (2 * HEAD_DIM)
    scale = HEAD_DIM ** -0.5
    nb = S // SB_TILE

    def body(q_ref, k_ref, v_ref, o_ref, nt_ref):
        tri = _tri(lambda j, s: j > s)
        heads = [slice(h * HEAD_DIM, (h + 1) * HEAD_DIM) for h in range(2)]

        def q_block(qb, n_pairs, parity):
            q0 = pl.multiple_of(qb * SB_TILE, SB_TILE)
            rows = pl.ds(q0, SB_TILE)
            qs = [q_ref[rows, hs] for hs in heads]

            def step(blocks, state):
                chains = [(h, kb, diagonal, pl.ds(pl.multiple_of(kb * SB_TILE, SB_TILE), SB_TILE))
                          for h in range(2) for kb, diagonal in blocks]
                logs = [_sb_logs(qs[h], k_ref[keys, heads[h]], scale, diagonal)
                        for h, _, diagonal, keys in chains]
                sums = [_cumsum_mm(log_1mb, tri) for _, log_1mb, _ in logs]
                rights = [state[0][0], state[1][0]]
                accs = [state[0][1], state[1][1]]
                for (h, _, diagonal, keys), (log_b, log_1mb, causal), csum in zip(chains, logs, sums):
                    a = jnp.exp(log_b + csum + rights[h])
                    if diagonal:
                        a = jnp.where(causal, a, 0.0)
                    accs[h] = accs[h] + jnp.dot(a.astype(BF16), v_ref[keys, heads[h]],
                                                preferred_element_type=F32)
                    rights[h] = rights[h] + jnp.sum(log_1mb, axis=-1, keepdims=True)
                return ((rights[0], accs[0]), (rights[1], accs[1]))

            zero = (jnp.zeros((SB_TILE, 1), F32), jnp.zeros((SB_TILE, HEAD_DIM), F32))
            first = [(qb, True)] + ([(qb - 1, False)] if parity else [])
            top = qb - len(first)
            state = lax.fori_loop(
                0, n_pairs, lambda t, st: step([(top - 2 * t, False), (top - 2 * t - 1, False)], st),
                step(first, (zero, zero)))
            for hs, (right, acc) in zip(heads, state):
                o_ref[rows, hs] = acc.astype(BF16)
                nt_ref[rows, hs] = jnp.broadcast_to(right, (SB_TILE, HEAD_DIM))

        def q_pair_loop(j, carry):
            q_block(2 * j, j, 0)
            q_block(2 * j + 1, j, 1)
            return carry

        lax.fori_loop(0, nb // 2, q_pair_loop, 0)

    qs, ks_, vs = _head_specs(S, HP)
    hd = pl.BlockSpec((S, 2 * HEAD_DIM), lambda b, hp: (b, hp))
    return pl.pallas_call(
        body, name="attn_b_fwd", grid=(B, HP),
        in_specs=[qs, ks_, vs], out_specs=[hd, hd],
        out_shape=[jax.ShapeDtypeStruct((B * S, D), BF16), jax.ShapeDtypeStruct((B * S, D), F32)],
        compiler_params=_params(("parallel", "parallel")),
    )(qkv, qkv, qkv)


def attn_b_bwd(qkv, do, ntot, B, S):
    D = qkv.shape[1] // 3
    HP = D // (2 * HEAD_DIM)
    scale = HEAD_DIM ** -0.5
    nb = S // SB_TILE

    def body(q_ref, k_ref, v_ref, do_ref, nt_ref, dq_ref, dk_ref, dv_ref, dkt_acc, dvt_acc):
        tri_incl = _tri(lambda j, s: j <= s)
        tri_excl = _tri(lambda j, s: j < s)
        heads = [slice(h * HEAD_DIM, (h + 1) * HEAD_DIM) for h in range(2)]
        dkt_acc[...] = jnp.zeros_like(dkt_acc)
        dvt_acc[...] = jnp.zeros_like(dvt_acc)

        def q_block(qb, n_pairs, parity):
            q0 = pl.multiple_of(qb * SB_TILE, SB_TILE)
            rows = pl.ds(q0, SB_TILE)
            qt_pair = q_ref[rows, :].astype(F32).T.astype(BF16)
            dot_pair = do_ref[rows, :].astype(F32).T.astype(BF16)
            per_head = [(hs, q_ref[rows, hs], do_ref[rows, hs], qt_pair[hs, :], dot_pair[hs, :],
                         nt_ref[rows, hs.start:hs.start + 1]) for hs in heads]

            def step(blocks, state):
                chains = [(h, kb, diagonal, pl.ds(pl.multiple_of(kb * SB_TILE, SB_TILE), SB_TILE))
                          for h in range(2) for kb, diagonal in blocks]
                ks = [k_ref[keys, per_head[h][0]] for h, _, _, keys in chains]
                logs = [_sb_logs(per_head[h][1], k, scale, diagonal)
                        for (h, _, diagonal, _), k in zip(chains, ks)]
                das = [lax.dot_general(per_head[h][2], v_ref[keys, per_head[h][0]], NT_DIMS,
                                       preferred_element_type=F32) for h, _, _, keys in chains]
                sums = [_cumsum_mm(log_1mb, tri_incl) for _, log_1mb, _ in logs]
                left_n = [state[0][0], state[1][0]]
                left_d = [state[0][1], state[1][1]]
                dq_acc = [state[0][2], state[1][2]]
                a_s, dls = [], []
                for (h, _, diagonal, _), (log_b, log_1mb, causal), csum, da in zip(chains, logs, sums, das):
                    a = jnp.exp(log_b + (per_head[h][5] - left_n[h]) - csum)
                    if diagonal:
                        a = jnp.where(causal, a, 0.0)
                    a_s.append(a)
                    dls.append(a * da)
                    left_n[h] = left_n[h] + jnp.sum(log_1mb, axis=-1, keepdims=True)
                dsums = [_cumsum_mm(dl, tri_excl) for dl in dls]
                dzbs = []
                for (h, _, diagonal, _), (log_b, log_1mb, causal), dl, dsum in zip(chains, logs, dls, dsums):
                    dz = dl * jnp.exp(log_1mb) - (left_d[h] + dsum) * jnp.exp(log_b)
                    if diagonal:
                        dz = jnp.where(causal, dz, 0.0)
                    dzbs.append(dz.astype(BF16))
                    left_d[h] = left_d[h] + jnp.sum(dl, axis=-1, keepdims=True)
                for (h, kb, _, _), k, a, dzb in zip(chains, ks, a_s, dzbs):
                    hs, _, _, qt, dot_, _ = per_head[h]
                    dq_acc[h] = dq_acc[h] + jnp.dot(dzb, k, preferred_element_type=F32)
                    dkt_acc[kb, hs, :] += jnp.dot(qt, dzb, preferred_element_type=F32)
                    dvt_acc[kb, hs, :] += jnp.dot(dot_, a.astype(BF16), preferred_element_type=F32)
                return ((left_n[0], left_d[0], dq_acc[0]), (left_n[1], left_d[1], dq_acc[1]))

            zero1 = jnp.zeros((SB_TILE, 1), F32)
            zero = (zero1, zero1, jnp.zeros((SB_TILE, HEAD_DIM), F32))
            state = lax.fori_loop(
                0, n_pairs, lambda t, st: step([(2 * t, False), (2 * t + 1, False)], st), (zero, zero))
            last = ([(qb - 1, False)] if parity else []) + [(qb, True)]
            state = step(last, state)
            for hs, (_, _, dq_acc) in zip(heads, state):
                dq_ref[rows, hs] = (dq_acc * scale).astype(BF16)

        def q_pair_loop(j, carry):
            q_block(2 * j, j, 0)
            q_block(2 * j + 1, j, 1)
            return carry

        lax.fori_loop(0, nb // 2, q_pair_loop, 0)
        for kb in range(nb):
            dk_ref[kb * SB_TILE:(kb + 1) * SB_TILE, :] = (dkt_acc[kb].T * scale).astype(BF16)
            dv_ref[kb * SB_TILE:(kb + 1) * SB_TILE, :] = dvt_acc[kb].T.astype(BF16)

    qs, ks_, vs = _head_specs(S, HP)
    hd = pl.BlockSpec((S, 2 * HEAD_DIM), lambda b, hp: (b, hp))
    osh = jax.ShapeDtypeStruct((B * S, D), BF16)
    acc = pltpu.VMEM((nb, 2 * HEAD_DIM, SB_TILE), F32)
    return pl.pallas_call(
        body, name="attn_b_bwd", grid=(B, HP),
        in_specs=[qs, ks_, vs, hd, hd], out_specs=[hd, hd, hd], out_shape=[osh, osh, osh],
        scratch_shapes=[acc, acc],
        compiler_params=_params(("parallel", "parallel")),
    )(qkv, qkv, qkv, do, ntot)


def _place():
    x, y, c = lax.axis_index("x"), lax.axis_index("y"), lax.axis_index("c")
    chips = [(1 - x, y), (x, 1 - y), (1 - x, 1 - y)]
    return x, y, c, 2 * x + y, chips


def _remote(src, dst, send_sem, recv_sem, dev):
    return pltpu.make_async_remote_copy(src_ref=src, dst_ref=dst, send_sem=send_sem, recv_sem=recv_sem,
                                        device_id=dev, device_id_type=MESH)


def gather_weights(shards):
    n = len(shards)

    def body(*refs):
        ins, outs = refs[:n], refs[n:2 * n]
        send1, recv1, send2, recv2, lsem = refs[2 * n:]
        x, y, c, me, chips = _place()
        local, first = [], []
        for f in range(n):
            hl = shards[f].shape[0] // 2
            cp = pltpu.make_async_copy(ins[f], outs[f].at[me], lsem.at[f])
            cp.start()
            local.append(cp)
            for j, (qx, qy) in enumerate(chips):
                half = pl.ds(c * hl, hl)
                cp = _remote(ins[f].at[half], outs[f].at[me, half],
                             send1.at[3 * f + j], recv1.at[3 * f + j], (qx, qy, c))
                cp.start()
                first.append(cp)
        passed = []
        for f in range(n):
            hl = shards[f].shape[0] // 2
            for j, (qx, qy) in enumerate(chips):
                slab = outs[f].at[2 * qx + qy, pl.ds(c * hl, hl)]
                _remote(slab, slab, send1.at[3 * f + j], recv1.at[3 * f + j], (x, y, c)).wait_recv()
                cp = _remote(slab, slab, send2.at[3 * f + j], recv2.at[3 * f + j], (x, y, 1 - c))
                cp.start()
                passed.append(cp)
        for f in range(n):
            hl = shards[f].shape[0] // 2
            for j, (qx, qy) in enumerate(chips):
                slab = outs[f].at[2 * qx + qy, pl.ds((1 - c) * hl, hl)]
                _remote(slab, slab, send2.at[3 * f + j], recv2.at[3 * f + j], (x, y, c)).wait_recv()
        for cp in first + passed:
            cp.wait_send()
        for cp in local:
            cp.wait()

    sems = pltpu.SemaphoreType.DMA((3 * n,))
    return pl.pallas_call(
        body, name="gather_weights",
        in_specs=[ANY] * n, out_specs=[ANY] * n,
        out_shape=[jax.ShapeDtypeStruct((N_CHIPS,) + s.shape, s.dtype) for s in shards],
        scratch_shapes=[sems, sems, sems, sems, pltpu.SemaphoreType.DMA((n,))],
        compiler_params=pltpu.CompilerParams(has_side_effects=True),
    )(*shards)


HBM = pl.BlockSpec(memory_space=pltpu.HBM)
SEM = pl.BlockSpec(memory_space=pltpu.SEMAPHORE)
EFFECT = pltpu.SideEffectType.DATAFLOW_SIDE_EFFECTING


def _in_hbm(a):
    return pltpu.with_memory_space_constraint(a, pltpu.HBM)


def _row_half(ref, _unused, which):
    hr = ref.shape[-2] // 2
    idx = [pl.ds(0, d) for d in ref.shape[:-2]] + [pl.ds(which * hr, hr), pl.ds(0, ref.shape[-1])]
    return ref.at[tuple(idx)]


def cast_place(ids, w):
    L, R, C = w.shape

    def body(ids_ref, w_ref, s_ref, land_ref):
        v = w_ref[...].astype(BF16)
        s_ref[...] = v
        land_ref[...] = v

    return pl.pallas_call(
        body, name="cast_place",
        grid_spec=pltpu.PrefetchScalarGridSpec(
            num_scalar_prefetch=1, grid=(L,),
            in_specs=[pl.BlockSpec((None, R, C), lambda l, ids: (l, 0, 0))],
            out_specs=[pl.BlockSpec((None, R, C), lambda l, ids: (l, 0, 0)),
                       pl.BlockSpec((None, None, R, C), lambda l, ids: (ids[1], l, 0, 0))]),
        out_shape=[jax.ShapeDtypeStruct((L, R, C), BF16), jax.ShapeDtypeStruct((N_CHIPS, L, R, C), BF16)],
        compiler_params=_params(("parallel",)),
    )(ids, w)


def gather_start(layers, zones):
    flat = [s for lay in layers for s in lay]
    flat_zones = [z for lay in zones for z in lay]
    counts = [len(lay) for lay in layers]
    n, nl = len(flat), len(layers)

    def body(*refs):
        ins, lands = refs[:n], refs[n:2 * n]
        send, recv = refs[2 * n:2 * n + nl], refs[2 * n + nl:2 * n + 2 * nl]
        token = refs[-1]
        x, y, c, me, chips = _place()
        f = 0
        for li, cnt in enumerate(counts):
            for k in range(cnt):
                for j, (qx, qy) in enumerate(chips):
                    _remote(_row_half(ins[f], 0, c), _row_half(lands[f].at[me], 0, c),
                            send[li].at[3 * k + j], recv[li].at[3 * k + j], (qx, qy, c)).start()
                f += 1
        token[...] = jnp.zeros_like(token)

    sem_shapes = [pltpu.SemaphoreType.DMA((3 * cnt,)) for cnt in counts]
    land_shapes = [(N_CHIPS,) + s.shape for s in flat]
    res = pl.pallas_call(
        body, name="gather_start",
        out_shape=(*sem_shapes, *sem_shapes,
                   *[pltpu.HBM(s.shape, s.dtype) for s in flat],
                   *[pltpu.HBM(shp, s.dtype) for shp, s in zip(land_shapes, flat)],
                   jax.ShapeDtypeStruct((8, 128), F32)),
        in_specs=[HBM] * (2 * n),
        out_specs=(*[SEM] * (2 * nl), *[HBM] * (2 * n), pl.BlockSpec(memory_space=pltpu.VMEM)),
        input_output_aliases={k: 2 * nl + k for k in range(2 * n)},
        compiler_params=pltpu.CompilerParams(has_side_effects=EFFECT),
    )(*[_in_hbm(s) for s in flat], *[_in_hbm(z) for z in flat_zones])
    send, recv = res[:nl], res[nl:2 * nl]
    thru, lands, token = res[2 * nl:2 * nl + n], res[2 * nl + n:2 * nl + 2 * n], res[-1]
    out, f = [], 0
    for li, cnt in enumerate(counts):
        out.append((send[li], recv[li], list(thru[f:f + cnt]), list(lands[f:f + cnt])))
        f += cnt
    return out, token


def gather_wait(li, send, recv, shards, lands, after):
    m = len(shards)

    def body(*refs):
        ins, lnd = refs[:m], refs[m:2 * m]
        snd, rcv = refs[2 * m], refs[2 * m + 1]
        x, y, c, me, chips = _place()
        for k in range(m):
            for j, (qx, qy) in enumerate(chips):
                cp = _remote(_row_half(ins[k], 0, c), _row_half(lnd[k].at[2 * qx + qy], 0, c),
                             snd.at[3 * k + j], rcv.at[3 * k + j], (qx, qy, c))
                cp.wait_send()
                cp.wait_recv()

    res = pl.pallas_call(
        body, name=f"gather_wait_{li}",
        out_shape=(*[pltpu.HBM(s.shape, s.dtype) for s in shards],
                   *[pltpu.HBM(s.shape, s.dtype) for s in lands]),
        in_specs=[HBM] * (2 * m) + [SEM, SEM, ANY], out_specs=[HBM] * (2 * m),
        input_output_aliases={k: k for k in range(2 * m)},
        compiler_params=pltpu.CompilerParams(has_side_effects=EFFECT),
    )(*shards, *lands, send, recv, after)
    return list(res[:m]), list(res[m:])


def gather_forward(li, lands):
    m = len(lands)

    def body(*refs):
        outs = refs[m:2 * m]
        send, recv = refs[2 * m:]
        x, y, c, me, chips = _place()
        passed = []
        for k in range(m):
            for j, (qx, qy) in enumerate(chips):
                slab = _row_half(outs[k].at[2 * qx + qy], 0, c)
                cp = _remote(slab, slab, send.at[3 * k + j], recv.at[3 * k + j], (x, y, 1 - c))
                cp.start()
                passed.append(cp)
        for k in range(m):
            for j, (qx, qy) in enumerate(chips):
                slab = _row_half(outs[k].at[2 * qx + qy], 0, 1 - c)
                _remote(slab, slab, send.at[3 * k + j], recv.at[3 * k + j], (x, y, c)).wait_recv()
        for cp in passed:
            cp.wait_send()

    sems = pltpu.SemaphoreType.DMA((3 * m,))
    return pl.pallas_call(
        body, name=f"gather_forward_{li}",
        in_specs=[ANY] * m, out_specs=[ANY] * m,
        out_shape=[jax.ShapeDtypeStruct(s.shape, s.dtype) for s in lands],
        input_output_aliases={k: k for k in range(m)},
        scratch_shapes=[sems, sems],
        compiler_params=pltpu.CompilerParams(has_side_effects=True),
    )(*lands)


def pair_exchange_rows(li, grads):
    n = len(grads)

    def body(*refs):
        ins, outs = refs[:n], refs[n:2 * n]
        send, recv = refs[2 * n:]
        x, y, c, _, _ = _place()
        cps = []
        for k in range(n):
            cp = _remote(_row_half(ins[k], 0, 1 - c), outs[k], send.at[k], recv.at[k], (x, y, 1 - c))
            cp.start()
            cps.append(cp)
        for cp in cps:
            cp.wait()

    sems = pltpu.SemaphoreType.DMA((n,))
    return pl.pallas_call(
        body, name=f"pair_exchange_{li}",
        in_specs=[ANY] * n, out_specs=[ANY] * n,
        out_shape=[jax.ShapeDtypeStruct(g.shape[:-2] + (g.shape[-2] // 2, g.shape[-1]), g.dtype)
                   for g in grads],
        scratch_shapes=[sems, sems],
        compiler_params=pltpu.CompilerParams(has_side_effects=True),
    )(*grads)


def pair_add_rows(ids, grad, recv):
    L, P, hr, C = recv.shape

    def body(ids_ref, a_ref, b_ref, o_ref):
        o_ref[...] = (a_ref[...].astype(F32) + b_ref[...].astype(F32)).astype(o_ref.dtype)

    blk = (None, None, hr, C)
    return pl.pallas_call(
        body, name="pair_add",
        grid_spec=pltpu.PrefetchScalarGridSpec(
            num_scalar_prefetch=1, grid=(L, P),
            in_specs=[pl.BlockSpec(blk, lambda l, p, ids: (l, p, ids[0], 0)),
                      pl.BlockSpec(blk, lambda l, p, ids: (l, p, 0, 0))],
            out_specs=pl.BlockSpec(blk, lambda l, p, ids: (l, p, 0, 0))),
        out_shape=jax.ShapeDtypeStruct(recv.shape, recv.dtype),
        compiler_params=_params(("parallel", "parallel")),
    )(ids, grad, recv)


def reduce_start(li, parts):
    m = len(parts)

    def body(*refs):
        ins, lands = refs[:m], refs[m:2 * m]
        send, recv = refs[2 * m], refs[2 * m + 1]
        token = refs[-1]
        x, y, c, me, chips = _place()
        for k in range(m):
            rows = pl.ds(0, parts[k].shape[0])
            for j, (qx, qy) in enumerate(chips):
                _remote(ins[k].at[rows, 2 * qx + qy], lands[k].at[rows, me],
                        send.at[3 * k + j], recv.at[3 * k + j], (qx, qy, c)).start()
        token[...] = jnp.zeros_like(token)

    sems = pltpu.SemaphoreType.DMA((3 * m,))
    res = pl.pallas_call(
        body, name=f"reduce_start_{li}",
        out_shape=(sems, sems, *[pltpu.HBM(s.shape, s.dtype) for s in parts],
                   *[pltpu.HBM(s.shape, s.dtype) for s in parts], jax.ShapeDtypeStruct((8, 128), F32)),
        in_specs=[HBM] * (2 * m),
        out_specs=(SEM, SEM, *[HBM] * (2 * m), pl.BlockSpec(memory_space=pltpu.VMEM)),
        input_output_aliases={k: 2 + k for k in range(2 * m)},
        compiler_params=pltpu.CompilerParams(has_side_effects=EFFECT),
    )(*[_in_hbm(s) for s in parts], *[_in_hbm(lax.empty(s.shape, s.dtype)) for s in parts])
    return res[0], res[1], list(res[2:2 + m]), list(res[2 + m:2 + 2 * m]), res[-1]


def reduce_wait(li, send, recv, parts, lands, after):
    m = len(parts)

    def body(*refs):
        ins, lnd = refs[:m], refs[m:2 * m]
        snd, rcv = refs[2 * m], refs[2 * m + 1]
        x, y, c, me, chips = _place()
        for k in range(m):
            rows = pl.ds(0, parts[k].shape[0])
            for j, (qx, qy) in enumerate(chips):
                cp = _remote(ins[k].at[rows, 2 * qx + qy], lnd[k].at[rows, 2 * qx + qy],
                             snd.at[3 * k + j], rcv.at[3 * k + j], (qx, qy, c))
                cp.wait_send()
                cp.wait_recv()

    res = pl.pallas_call(
        body, name=f"reduce_wait_{li}",
        out_shape=(*[pltpu.HBM(s.shape, s.dtype) for s in parts],
                   *[pltpu.HBM(s.shape, s.dtype) for s in lands]),
        in_specs=[HBM] * (2 * m) + [SEM, SEM, ANY], out_specs=[HBM] * (2 * m),
        input_output_aliases={k: k for k in range(2 * m)},
        compiler_params=pltpu.CompilerParams(has_side_effects=EFFECT),
    )(*parts, *lands, send, recv, after)
    return list(res[:m]), list(res[m:])


def chip_sum_rows(ids, land, part, gfull, l0):
    L, P, hr, C = land.shape

    def body(ids_ref, land_ref, part_ref, _g, o_ref):
        tot = None
        for q in range(P):
            term = jnp.where(ids_ref[1] == q, part_ref[...], land_ref[q]).astype(F32)
            tot = term if tot is None else tot + term
        o_ref[...] = tot

    return pl.pallas_call(
        body, name="chip_sum",
        grid_spec=pltpu.PrefetchScalarGridSpec(
            num_scalar_prefetch=1, grid=(L,),
            in_specs=[pl.BlockSpec((None, P, hr, C), lambda l, ids: (l, 0, 0, 0)),
                      pl.BlockSpec((None, None, hr, C), lambda l, ids: (l, ids[1], 0, 0)), ANY],
            out_specs=pl.BlockSpec((None, hr, C), lambda l, ids: (l0 + l, ids[0], 0))),
        out_shape=jax.ShapeDtypeStruct(gfull.shape, gfull.dtype),
        input_output_aliases={3: 0},
        compiler_params=_params(("arbitrary",)),
    )(ids, land, part, gfull)


def half_swap_rows(grads):
    n = len(grads)

    def body(*refs):
        outs = refs[n:2 * n]
        send, recv = refs[2 * n:]
        x, y, c, _, _ = _place()
        cps = []
        for k in range(n):
            mine = _row_half(outs[k], 0, c)
            cp = _remote(mine, mine, send.at[k], recv.at[k], (x, y, 1 - c))
            cp.start()
            cps.append(cp)
        for k in range(n):
            theirs = _row_half(outs[k], 0, 1 - c)
            _remote(theirs, theirs, send.at[k], recv.at[k], (x, y, c)).wait_recv()
        for cp in cps:
            cp.wait_send()

    sems = pltpu.SemaphoreType.DMA((n,))
    return pl.pallas_call(
        body, name="half_swap",
        in_specs=[ANY] * n, out_specs=[ANY] * n,
        out_shape=[jax.ShapeDtypeStruct(g.shape, g.dtype) for g in grads],
        input_output_aliases={k: k for k in range(n)},
        scratch_shapes=[sems, sems],
        compiler_params=pltpu.CompilerParams(has_side_effects=True),
    )(*grads)


def pair_exchange(grads):
    n = len(grads)

    def body(*refs):
        ins, outs = refs[:n], refs[n:2 * n]
        send, recv = refs[2 * n:]
        x, y, c, _, _ = _place()
        cps = []
        for f in range(n):
            hl = grads[f].shape[0] // 2
            cp = _remote(ins[f].at[pl.ds((1 - c) * hl, hl)], outs[f], send.at[f], recv.at[f], (x, y, 1 - c))
            cp.start()
            cps.append(cp)
        for cp in cps:
            cp.wait()

    sems = pltpu.SemaphoreType.DMA((n,))
    return pl.pallas_call(
        body, name="pair_exchange",
        in_specs=[ANY] * n, out_specs=[ANY] * n,
        out_shape=[jax.ShapeDtypeStruct((g.shape[0] // 2,) + g.shape[1:], g.dtype) for g in grads],
        scratch_shapes=[sems, sems],
        compiler_params=pltpu.CompilerParams(has_side_effects=True),
    )(*grads)


def chip_exchange(parts):
    n = len(parts)

    def body(*refs):
        ins, outs = refs[:n], refs[n:2 * n]
        send, recv, lsem = refs[2 * n:]
        x, y, c, me, chips = _place()
        local, sent = [], []
        for f in range(n):
            hl = parts[f].shape[0]
            rows = pl.ds(0, hl)
            cp = pltpu.make_async_copy(ins[f].at[rows, me], outs[f].at[rows, me], lsem.at[f])
            cp.start()
            local.append(cp)
            for j, (qx, qy) in enumerate(chips):
                cp = _remote(ins[f].at[rows, 2 * qx + qy], outs[f].at[rows, me],
                             send.at[3 * f + j], recv.at[3 * f + j], (qx, qy, c))
                cp.start()
                sent.append(cp)
        for f in range(n):
            rows = pl.ds(0, parts[f].shape[0])
            for j, (qx, qy) in enumerate(chips):
                slab = outs[f].at[rows, 2 * qx + qy]
                _remote(slab, slab, send.at[3 * f + j], recv.at[3 * f + j], (x, y, c)).wait_recv()
        for cp in sent:
            cp.wait_send()
        for cp in local:
            cp.wait()

    sems = pltpu.SemaphoreType.DMA((3 * n,))
    return pl.pallas_call(
        body, name="chip_exchange",
        in_specs=[ANY] * n, out_specs=[ANY] * n,
        out_shape=[jax.ShapeDtypeStruct(s.shape, s.dtype) for s in parts],
        scratch_shapes=[sems, sems, pltpu.SemaphoreType.DMA((n,))],
        compiler_params=pltpu.CompilerParams(has_side_effects=True),
    )(*parts)


def half_swap(grads):
    n = len(grads)

    def body(*refs):
        ins, outs = refs[:n], refs[n:2 * n]
        send, recv = refs[2 * n:]
        x, y, c, _, _ = _place()
        cps = []
        for f in range(n):
            hl = grads[f].shape[0] // 2
            mine = pl.ds(c * hl, hl)
            cp = _remote(outs[f].at[mine], outs[f].at[mine], send.at[f], recv.at[f], (x, y, 1 - c))
            cp.start()
            cps.append(cp)
        for f in range(n):
            hl = grads[f].shape[0] // 2
            theirs = outs[f].at[pl.ds((1 - c) * hl, hl)]
            _remote(theirs, theirs, send.at[f], recv.at[f], (x, y, c)).wait_recv()
        for cp in cps:
            cp.wait_send()

    sems = pltpu.SemaphoreType.DMA((n,))
    return pl.pallas_call(
        body, name="half_swap",
        in_specs=[ANY] * n, out_specs=[ANY] * n,
        out_shape=[jax.ShapeDtypeStruct(g.shape, g.dtype) for g in grads],
        input_output_aliases={f: f for f in range(n)},
        scratch_shapes=[sems, sems],
        compiler_params=pltpu.CompilerParams(has_side_effects=True),
    )(*grads)


def all_sum_small(v):
    R = v.shape[0]

    def body(v_ref, o_ref, land, send, recv):
        x, y, c, _, _ = _place()
        me = 4 * x + 2 * y + c
        land[me] = v_ref[...]
        peers = [(px, py, pc) for px in range(2) for py in range(2) for pc in range(2)]
        cps = []
        for k in range(1, 8):
            dev = (x ^ (k >> 2), y ^ ((k >> 1) & 1), c ^ (k & 1))
            cp = _remote(v_ref, land.at[me], send.at[k - 1], recv.at[k - 1], dev)
            cp.start()
            cps.append(cp)
        for k in range(1, 8):
            src = 4 * (x ^ (k >> 2)) + 2 * (y ^ ((k >> 1) & 1)) + (c ^ (k & 1))
            _remote(v_ref, land.at[src], send.at[k - 1], recv.at[k - 1], (x, y, c)).wait_recv()
        for cp in cps:
            cp.wait_send()
        tot = land[0]
        for d in range(1, len(peers)):
            tot = tot + land[d]
        o_ref[...] = tot

    sems = pltpu.SemaphoreType.DMA((7,))
    vm = pl.BlockSpec(memory_space=pltpu.VMEM)
    return pl.pallas_call(
        body, name="all_sum_small", in_specs=[vm], out_specs=vm,
        out_shape=jax.ShapeDtypeStruct(v.shape, F32),
        scratch_shapes=[pltpu.VMEM((8, R, 128), F32), sems, sems],
        compiler_params=pltpu.CompilerParams(has_side_effects=True),
    )(v)


def _row_tile(R):
    for t in (512, 256, 128, 64, 32, 16):
        if R % t == 0:
            return t
    return R


def pair_add(cidx, grad, recv):
    hl, P, R, C = recv.shape
    tr = _row_tile(R)

    def body(c_ref, a_ref, b_ref, o_ref):
        o_ref[...] = (a_ref[...].astype(F32) + b_ref[...].astype(F32)).astype(o_ref.dtype)

    blk = (None, None, tr, C)
    return pl.pallas_call(
        body, name="pair_add",
        grid_spec=pltpu.PrefetchScalarGridSpec(
            num_scalar_prefetch=1, grid=(hl, P, R // tr),
            in_specs=[pl.BlockSpec(blk, lambda l, p, r, c: (c[0] * hl + l, p, r, 0)),
                      pl.BlockSpec(blk, lambda l, p, r, c: (l, p, r, 0))],
            out_specs=pl.BlockSpec(blk, lambda l, p, r, c: (l, p, r, 0))),
        out_shape=jax.ShapeDtypeStruct(recv.shape, recv.dtype),
        compiler_params=_params(("parallel", "parallel", "parallel")),
    )(cidx, grad, recv)


def chip_sum(cidx, land, L):
    hl, P, R, C = land.shape
    tr = _row_tile(R)

    def body(c_ref, a_ref, o_ref):
        tot = a_ref[0].astype(F32)
        for q in range(1, P):
            tot = tot + a_ref[q].astype(F32)
        o_ref[...] = tot

    return pl.pallas_call(
        body, name="chip_sum",
        grid_spec=pltpu.PrefetchScalarGridSpec(
            num_scalar_prefetch=1, grid=(hl, R // tr),
            in_specs=[pl.BlockSpec((None, P, tr, C), lambda l, r, c: (l, 0, r, 0))],
            out_specs=pl.BlockSpec((None, tr, C), lambda l, r, c: (c[0] * hl + l, r, 0))),
        out_shape=jax.ShapeDtypeStruct((L, R, C), F32),
        compiler_params=_params(("parallel", "parallel")),
    )(cidx, land)


def adamw(w, g, m, v):
    L, R, C = w.shape
    tr = _row_tile(R)
    c1 = 1.0 / (1.0 - ADAM_B1 ** ADAM_STEP)
    c2 = 1.0 / (1.0 - ADAM_B2 ** ADAM_STEP)

    def body(w_ref, g_ref, m_ref, v_ref, go_ref, d_ref, nm_ref, nv_ref):
        gv = g_ref[...]
        go_ref[...] = gv
        nm = ADAM_B1 * m_ref[...] + (1.0 - ADAM_B1) * gv
        nv = ADAM_B2 * v_ref[...] + (1.0 - ADAM_B2) * (gv * gv)
        nm_ref[...] = nm
        nv_ref[...] = nv
        d_ref[...] = -ADAM_LR * ((nm * c1) / (jnp.sqrt(nv * c2) + ADAM_EPS) + ADAM_WD * w_ref[...])

    blk = pl.BlockSpec((None, tr, C), lambda l, r: (l, r, 0))
    osh = jax.ShapeDtypeStruct((L, R, C), F32)
    return pl.pallas_call(
        body, name="adamw", grid=(L, R // tr),
        in_specs=[blk, blk, blk, blk], out_specs=[blk, blk, blk, blk], out_shape=[osh, osh, osh, osh],
        compiler_params=_params(("parallel", "parallel")),
    )(w, g, m, v)


def kernel(x, w_qkv_a, w_o_a, rel_bias, w_qkv_b, w_o_b, ffn_w_gate, ffn_w_up, ffn_w_down, ln_g, ln_b, loss_target, m_w_qkv_a, m_w_o_a, m_rel_bias, m_w_qkv_b, m_w_o_b, m_ffn_w_gate, m_ffn_w_up, m_ffn_w_down, m_ln_g, m_ln_b, v_w_qkv_a, v_w_o_a, v_rel_bias, v_w_qkv_b, v_w_o_b, v_ffn_w_gate, v_ffn_w_up, v_ffn_w_down, v_ln_g, v_ln_b):
    B, S, D = x.shape
    M = B * S
    depth = ffn_w_gate.shape[0]
    n_ffn = 2 * depth
    H = D // HEAD_DIM
    HP = H // 2
    Fs = ffn_w_gate.shape[-1]
    alpha = (2.0 * depth) ** 0.25
    assert S % QB_A == 0 and S % SB_TILE == 0 and rel_bias.shape == (N_REL, H)

    def mixer_weights(i):
        wq, wo = (w_qkv_a, w_o_a) if i % 2 == 0 else (w_qkv_b, w_o_b)
        return wq[i // 2:i // 2 + 1], wo[i // 2:i // 2 + 1]

    me = 2 * lax.axis_index("x") + lax.axis_index("y")
    ids = jnp.stack([lax.axis_index("c"), me]).astype(jnp.int32)
    placed = [[cast_place(ids, w) for w in (*mixer_weights(i), ffn_w_gate[i], ffn_w_up[i], ffn_w_down[i])]
              for i in range(depth)]
    in_flight, _ = gather_start([[s for s, _ in lay] for lay in placed], [[z for _, z in lay] for lay in placed])
    lng_p, lnb_p = gather_weights([ln_g, ln_b])
    lng = jnp.moveaxis(lng_p, 0, 2).reshape(depth, 3, 1, D)
    lnb = jnp.moveaxis(lnb_p, 0, 2).reshape(depth, 3, 1, D)

    tab_t = jnp.pad(rel_bias.T, ((0, 0), (0, REL_PAD - N_REL)))
    vr = bias_vec(tab_t).reshape(HP, 2, VR_W)

    xf = x.reshape(M, D)
    xb = xf.astype(BF16)
    saved, weights = [], []
    for i in range(depth):
        send, recv, thru, lands = in_flight[i]
        _, lands = gather_wait(i, send, recv, thru, lands, xf)
        wq, wo, wg, wu, wd = gather_forward(i, lands)
        weights.append((wq, wo, wg, wu, wd))
        for j in range(3):
            gam, bet = lng[i, j], lnb[i, j]
            if j != 1:
                l = 0 if j == 0 else 1
                g, u, h = ffn_up(xb, wg, wu, l)
                xo, xob, xhat, rstd = mm_ln(h, wd, l, xf, gam, bet, alpha, 0.5, True)
                saved.append(("ffn", i, l, xb, g, u, h, xhat, rstd, gam))
            elif i % 2 == 0:
                qkv = qkv_proj(xb, wq, 0)
                o = attn_a_fwd(qkv, vr, B, S)
                xo, xob, xhat, rstd = mm_ln(o, wo, 0, xf, gam, bet, alpha, 1.0, False)
                saved.append(("a", i, 0, xb, qkv, o, None, xhat, rstd, gam))
            else:
                qkv = qkv_proj(xb, wq, 0)
                o, ntot = attn_b_fwd(qkv, B, S)
                xo, xob, xhat, rstd = mm_ln(o, wo, 0, xf, gam, bet, alpha, 1.0, False)
                saved.append(("b", i, 0, xb, qkv, o, ntot, xhat, rstd, gam))
            xf, xb = xo, xob

    dy, loss_part = loss_head(xf, loss_target.reshape(M, D))
    loss = lax.psum(loss_part[0, 0], ("x", "y", "c"))

    Cq, Ro = w_qkv_a.shape[-1], w_o_a.shape[1]
    dgam = lax.empty((3 * depth, 1, D), F32)
    dbet = lax.empty((3 * depth, 1, D), F32)
    dvrs, reducing = [], {}
    started = None
    for i in reversed(range(depth)):
        wq, wo, wg, wu, wd = weights[i]
        gq = lax.empty((1, N_CHIPS, D, Cq), BF16)
        go = lax.empty((1, N_CHIPS, Ro, D), BF16)
        ggate = lax.empty((2, N_CHIPS, D, Fs), BF16)
        gup = lax.empty((2, N_CHIPS, D, Fs), BF16)
        gdown = lax.empty((2, N_CHIPS, Fs, D), BF16)
        for sub in reversed(range(3 * i, 3 * i + 3)):
            kind, _, l, xb_in, t1, t2, t3, xhat, rstd, gam = saved[sub]
            scale = 0.5 if kind == "ffn" else 1.0
            if started is not None:
                gam = gam + started[0, 0]
                started = None
            dpre, dyb, dgam, dbet = ln_bwd(dy, xhat, rstd, gam, scale, dgam, dbet, sub)
            if kind == "ffn":
                g, u, h = t1, t2, t3
                dg, du, gdown, ggate, gup = ffn_bwd(gdown, ggate, gup, l, xb_in, dyb, wd, g, u, h)
                dy = ffn_dx(dg, du, wg, wu, l, dpre, alpha)
            else:
                qkv, o = t1, t2
                do = o_proj_bwd(dyb, wo, l)
                go = dw_o(go, l, o, dyb)
                if kind == "a":
                    dq, dk, dv, dvr = attn_a_bwd(qkv, vr, do, B, S)
                    dvrs.append(dvr.reshape(B, H, VR_W))
                else:
                    dq, dk, dv = attn_b_bwd(qkv, do, t3, B, S)
                dqkv = jnp.concatenate([dq, dk, dv], axis=1)
                gq = dw_qkv(gq, l, xb_in, dqkv)
                dy = qkv_proj_bwd(dqkv, wq, l, dpre, alpha)
        fams = [gq, go, ggate, gup, gdown]
        from_sib = pair_exchange_rows(i, fams)
        parts = [pair_add_rows(ids, g_, r_) for g_, r_ in zip(fams, from_sib)]
        reducing[i] = reduce_start(i, parts)
        started = reducing[i][4]
    grad_x = dy.reshape(B, S, D)

    la, lb = w_qkv_a.shape[0], w_qkv_b.shape[0]
    full = {"qa": lax.empty((la, D, Cq), F32), "oa": lax.empty((la, Ro, D), F32),
            "qb": lax.empty((lb, D, Cq), F32), "ob": lax.empty((lb, Ro, D), F32),
            "gate": lax.empty((n_ffn, D, Fs), F32), "up": lax.empty((n_ffn, D, Fs), F32),
            "down": lax.empty((n_ffn, Fs, D), F32)}
    for i in reversed(range(depth)):
        send, recv, parts, lands, _ = reducing[i]
        parts, lands = reduce_wait(i, send, recv, parts, lands, dy)
        mix = "a" if i % 2 == 0 else "b"
        targets = [("q" + mix, i // 2), ("o" + mix, i // 2), ("gate", 2 * i), ("up", 2 * i), ("down", 2 * i)]
        for (name, l0), part, land in zip(targets, parts, lands):
            full[name] = chip_sum_rows(ids, land, part, full[name], l0)
    g_qa, g_oa, g_qb, g_ob, g_gate, g_up, g_down = half_swap_rows(
        [full[k] for k in ("qa", "oa", "qb", "ob", "gate", "up", "down")])

    d_tab_t = bias_vec_bwd(jnp.concatenate(dvrs, axis=0))
    small = jnp.concatenate([d_tab_t.reshape(-1), dgam.reshape(-1), dbet.reshape(-1)])
    n_small = small.shape[0]
    rows = -(-n_small // (8 * 128)) * 8
    tot = all_sum_small(jnp.pad(small, (0, rows * 128 - n_small)).reshape(rows, 128)).reshape(-1)
    n_tab, n_ln = H * REL_PAD, 3 * depth * D
    g_rel = tot[:n_tab].reshape(H, REL_PAD)[:, :N_REL].T
    ln_cols = D // N_CHIPS

    def ln_shard(flat):
        return lax.dynamic_slice_in_dim(flat.reshape(depth, 3, D), me * ln_cols, ln_cols, axis=2)

    g_lng = ln_shard(tot[n_tab:n_tab + n_ln])
    g_lnb = ln_shard(tot[n_tab + n_ln:n_tab + 2 * n_ln])

    def upd(w, g, m, v):
        shp = w.shape
        w3, m3, v3 = (a.reshape(g.shape) for a in (w, m, v))
        g_out, d, nm, nv = adamw(w3, g, m3, v3)
        return g_out.reshape(shp), d.reshape(shp), nm.reshape(shp), nv.reshape(shp)

    res = [
        upd(w_qkv_a, g_qa, m_w_qkv_a, v_w_qkv_a),
        upd(w_o_a, g_oa, m_w_o_a, v_w_o_a),
        upd(rel_bias, g_rel.reshape(1, N_REL, H), m_rel_bias, v_rel_bias),
        upd(w_qkv_b, g_qb, m_w_qkv_b, v_w_qkv_b),
        upd(w_o_b, g_ob, m_w_o_b, v_w_o_b),
        upd(ffn_w_gate, g_gate, m_ffn_w_gate, v_ffn_w_gate),
        upd(ffn_w_up, g_up, m_ffn_w_up, v_ffn_w_up),
        upd(ffn_w_down, g_down, m_ffn_w_down, v_ffn_w_down),
        upd(ln_g, g_lng, m_ln_g, v_ln_g),
        upd(ln_b, g_lnb, m_ln_b, v_ln_b),
    ]
    grads = [r[0] for r in res]
    deltas = [r[1] for r in res]
    new_m = [r[2] for r in res]
    new_v = [r[3] for r in res]
    return (loss, grad_x, *grads, *deltas, *new_m, *new_v)
```

```python
import functools
import math

import jax
import jax.numpy as jnp
from jax import lax
from jax.experimental import pallas as pl
from jax.experimental.pallas import tpu as pltpu

F32 = jnp.float32
BF16 = jnp.bfloat16
MESH = pl.DeviceIdType.MESH

N_CHIPS = 4
HEAD_DIM = 64
CHUNK = 64
LEFT_CHUNKS = 8
LOOKBACK = LEFT_CHUNKS * CHUNK
REL_CLIP = 128
N_REL = 2 * REL_CLIP + 1
REL_PAD = 384
SB_TILE = 256
QB_A = 256
KW_A = QB_A + LOOKBACK
VR_W = 1024
VR_C0 = KW_A - 1
LN_EPS = 1e-5
ADAM_LR, ADAM_B1, ADAM_B2, ADAM_EPS, ADAM_WD, ADAM_STEP = 0.001, 0.9, 0.999, 1e-08, 0.01, 10
NEG = -1e30
VMEM_LIMIT = 56 * 1024 * 1024

NT_DIMS = (((1,), (1,)), ((), ()))
TN_DIMS = (((0,), (0,)), ((), ()))
ANY = pl.BlockSpec(memory_space=pl.ANY)


def _params(sem=None):
    return pltpu.CompilerParams(dimension_semantics=sem, vmem_limit_bytes=VMEM_LIMIT)


def _tile(n, pref):
    t = min(n, pref)
    assert n % t == 0, (n, pref)
    return t


def _sigmoid(z):
    return 1.0 / (1.0 + jnp.exp(-z))


def _mm_call(name, operands, in_specs, out_shape, out_spec, grid, dims_list, acc_shape,
             add_coef=None, aliases=None):
    n_pairs = len(dims_list)
    nk = grid[-1]
    has_add = add_coef is not None
    n_alias = len(aliases) if aliases else 0

    def body(*refs):
        pair_refs = refs[:2 * n_pairs]
        pos = 2 * n_pairs
        add_ref = refs[pos] if has_add else None
        pos += (1 if has_add else 0) + n_alias
        o_ref = refs[pos]
        acc_ref = refs[pos + 1] if nk > 1 else None

        def product():
            part = None
            for i, dims in enumerate(dims_list):
                d = lax.dot_general(pair_refs[2 * i][...], pair_refs[2 * i + 1][...], dims,
                                    preferred_element_type=F32)
                part = d if part is None else part + d
            return part

        def finish(r):
            if has_add:
                r = r + add_coef * add_ref[...]
            o_ref[...] = r.astype(o_ref.dtype)

        if nk == 1:
            finish(product())
        else:
            k = pl.program_id(len(grid) - 1)

            @pl.when(k == 0)
            def _():
                acc_ref[...] = jnp.zeros_like(acc_ref)

            acc_ref[...] += product()

            @pl.when(k == nk - 1)
            def _():
                finish(acc_ref[...])

    sem = ("parallel",) * (len(grid) - 1) + ("arbitrary",)
    return pl.pallas_call(
        body, name=name, grid=grid, in_specs=in_specs, out_specs=out_spec, out_shape=out_shape,
        scratch_shapes=[pltpu.VMEM(acc_shape, F32)] if nk > 1 else [],
        input_output_aliases=aliases or {},
        compiler_params=_params(sem),
    )(*operands)


def qkv_proj(xb, w, l):
    M, D = xb.shape
    C = w.shape[-1]
    tm = _tile(M, 512)
    return _mm_call(
        "qkv_proj", (xb, w),
        [pl.BlockSpec((tm, D), lambda p, i, k: (i, 0)),
         pl.BlockSpec((None, None, D, C), lambda p, i, k: (p, l, 0, 0))],
        jax.ShapeDtypeStruct((M, N_CHIPS * C), BF16),
        pl.BlockSpec((tm, C), lambda p, i, k: (i, p)),
        (N_CHIPS, M // tm, 1), [(((1,), (0,)), ((), ()))], None)


def o_proj_bwd(dyb, w, l):
    M, D = dyb.shape
    R = w.shape[2]
    tm = _tile(M, 512)
    return _mm_call(
        "o_proj_bwd", (dyb, w),
        [pl.BlockSpec((tm, D), lambda p, i, k: (i, 0)),
         pl.BlockSpec((None, None, R, D), lambda p, i, k: (p, l, 0, 0))],
        jax.ShapeDtypeStruct((M, N_CHIPS * R), BF16),
        pl.BlockSpec((tm, R), lambda p, i, k: (i, p)),
        (N_CHIPS, M // tm, 1), [NT_DIMS], None)


def qkv_proj_bwd(dqkv, w, l, dpre, alpha):
    _, M, D = dqkv.shape
    C = w.shape[3]
    tm = _tile(M, 512)
    T = math.gcd(D, C)

    def body(a_ref, w_ref, add_ref, o_ref):
        acc = alpha * add_ref[...]
        for t in range(3 * D // T):
            pa, ca = divmod(t * T, D)
            pw, cw = divmod(t * T, C)
            acc = acc + lax.dot_general(a_ref[pa, :, ca:ca + T], w_ref[pw, :, cw:cw + T], NT_DIMS,
                                        preferred_element_type=F32)
        o_ref[...] = acc

    row = pl.BlockSpec((tm, D), lambda i: (i, 0))
    return pl.pallas_call(
        body, name="qkv_proj_bwd", grid=(M // tm,),
        in_specs=[pl.BlockSpec((3, tm, D), lambda i: (0, i, 0)),
                  pl.BlockSpec((N_CHIPS, None, D, C), lambda i: (0, l, 0, 0)), row],
        out_specs=row, out_shape=jax.ShapeDtypeStruct((M, D), F32),
        compiler_params=_params(("parallel",)),
    )(dqkv, w, dpre)


def ffn_dx(dg, du, wgt, wut, l, dpre, alpha):
    _, M, Fs = dg.shape
    D = wgt.shape[3]
    tm = _tile(M, 256)

    def body(dg_ref, wg_ref, du_ref, wu_ref, add_ref, o_ref):
        acc = alpha * add_ref[...]
        for p in range(N_CHIPS):
            acc = acc + jnp.dot(dg_ref[p], wg_ref[p], preferred_element_type=F32)
            acc = acc + jnp.dot(du_ref[p], wu_ref[p], preferred_element_type=F32)
        o_ref[...] = acc

    act = pl.BlockSpec((N_CHIPS, tm, Fs), lambda i: (0, i, 0))
    wsp = pl.BlockSpec((N_CHIPS, None, Fs, D), lambda i: (0, l, 0, 0))
    row = pl.BlockSpec((tm, D), lambda i: (i, 0))
    return pl.pallas_call(
        body, name="ffn_dx", grid=(M // tm,),
        in_specs=[act, wsp, act, wsp, row],
        out_specs=row, out_shape=jax.ShapeDtypeStruct((M, D), F32),
        compiler_params=_params(("parallel",)),
    )(dg, wgt, du, wut, dpre)


def _dw_call(name, buf, l, a, b, a_spec, b_spec, M, tk):
    _, _, R, C = buf.shape
    return _mm_call(
        name, (a, b, buf),
        [a_spec, b_spec, ANY],
        jax.ShapeDtypeStruct(buf.shape, buf.dtype),
        pl.BlockSpec((None, None, R, C), lambda p, k: (l, p, 0, 0)),
        (N_CHIPS, M // tk), [TN_DIMS], (R, C), aliases={2: 0})


def ffn_bwd(gdown, ggate, gup, l, xb, dyb, wd, g, u, h):
    M, D = dyb.shape
    Fs = wd.shape[2]
    tm = _tile(M, 512)
    n = M // tm

    def body(x_ref, dy_ref, wd_ref, g_ref, u_ref, h_ref, _gd, _gg, _gu,
             dg_ref, du_ref, gd_ref, gg_ref, gu_ref, acc_d, acc_g, acc_u):
        i = pl.program_id(1)

        @pl.when(i == 0)
        def _():
            acc_d[...] = jnp.zeros_like(acc_d)
            acc_g[...] = jnp.zeros_like(acc_g)
            acc_u[...] = jnp.zeros_like(acc_u)

        dy = dy_ref[...]
        dh = lax.dot_general(dy, wd_ref[...], NT_DIMS, preferred_element_type=F32)
        gf = g_ref[...].astype(F32)
        sig = _sigmoid(gf)
        silu = gf * sig
        dg = (dh * u_ref[...].astype(F32) * (sig * (1.0 + gf - silu))).astype(BF16)
        du = (dh * silu).astype(BF16)
        dg_ref[...] = dg
        du_ref[...] = du
        x = x_ref[...]
        acc_d[...] += lax.dot_general(h_ref[...], dy, TN_DIMS, preferred_element_type=F32)
        acc_g[...] += lax.dot_general(dg, x, TN_DIMS, preferred_element_type=F32)
        acc_u[...] += lax.dot_general(du, x, TN_DIMS, preferred_element_type=F32)

        @pl.when(i == n - 1)
        def _():
            gd_ref[...] = acc_d[...].astype(gd_ref.dtype)
            gg_ref[...] = acc_g[...].astype(gg_ref.dtype)
            gu_ref[...] = acc_u[...].astype(gu_ref.dtype)

    row = pl.BlockSpec((tm, D), lambda p, i: (i, 0))
    act = pl.BlockSpec((None, tm, Fs), lambda p, i: (p, i, 0))
    ash = jax.ShapeDtypeStruct((N_CHIPS, M, Fs), BF16)
    w_blk = pl.BlockSpec((None, None, Fs, D), lambda p, i: (l, p, 0, 0))
    return pl.pallas_call(
        body, name="ffn_bwd", grid=(N_CHIPS, n),
        in_specs=[row, row, pl.BlockSpec((None, None, Fs, D), lambda p, i: (p, l, 0, 0)),
                  act, act, act, ANY, ANY, ANY],
        out_specs=[act, act, w_blk, w_blk, w_blk],
        out_shape=[ash, ash] + [jax.ShapeDtypeStruct(b.shape, b.dtype) for b in (gdown, ggate, gup)],
        scratch_shapes=[pltpu.VMEM((Fs, D), F32), pltpu.VMEM((Fs, D), F32), pltpu.VMEM((Fs, D), F32)],
        input_output_aliases={6: 2, 7: 3, 8: 4},
        compiler_params=_params(("parallel", "arbitrary")),
    )(xb, dyb, wd, g, u, h, gdown, ggate, gup)


def dw_qkv(buf, l, xb, dqkv):
    M, D = xb.shape
    C = buf.shape[-1]
    tk = _tile(M, 512)
    n = M // tk
    T = math.gcd(D, C)
    nt = 3 * D // T

    def body(x_ref, b_ref, _buf, o_ref, acc_ref):
        k = pl.program_id(0)

        @pl.when(k == 0)
        def _():
            acc_ref[...] = jnp.zeros_like(acc_ref)

        xt = x_ref[...].astype(F32).T.astype(BF16)
        for t in range(nt):
            pa, ca = divmod(t * T, D)
            acc_ref[t] += jnp.dot(xt, b_ref[pa, :, ca:ca + T], preferred_element_type=F32)

        @pl.when(k == n - 1)
        def _():
            for t in range(nt):
                pw, cw = divmod(t * T, C)
                o_ref[pw, :, cw:cw + T] = acc_ref[t].astype(o_ref.dtype)

    return pl.pallas_call(
        body, name="dw_qkv", grid=(n,),
        in_specs=[pl.BlockSpec((tk, D), lambda k: (k, 0)),
                  pl.BlockSpec((3, tk, D), lambda k: (0, k, 0)), ANY],
        out_specs=pl.BlockSpec((None, N_CHIPS, D, C), lambda k: (l, 0, 0, 0)),
        out_shape=jax.ShapeDtypeStruct(buf.shape, buf.dtype),
        scratch_shapes=[pltpu.VMEM((nt, D, T), F32)],
        input_output_aliases={2: 0},
        compiler_params=_params(("arbitrary",)),
    )(xb, dqkv, buf)


def dw_o(buf, l, o, dyb):
    M, D = dyb.shape
    R = buf.shape[2]
    tk = _tile(M, 1024)
    return _dw_call("dw_o", buf, l, o, dyb,
                    pl.BlockSpec((tk, R), lambda p, k: (k, p)),
                    pl.BlockSpec((tk, D), lambda p, k: (k, 0)), M, tk)


def ffn_up(xb, wgt, wut, l):
    M, D = xb.shape
    Fs = wgt.shape[2]
    tm = _tile(M, 512)

    def body(x_ref, wg_ref, wu_ref, g_ref, u_ref, h_ref):
        x = x_ref[...]
        g = lax.dot_general(x, wg_ref[...], NT_DIMS, preferred_element_type=F32)
        u = lax.dot_general(x, wu_ref[...], NT_DIMS, preferred_element_type=F32)
        g_ref[...] = g.astype(BF16)
        u_ref[...] = u.astype(BF16)
        h_ref[...] = (g * _sigmoid(g) * u).astype(BF16)

    wsp = pl.BlockSpec((None, None, Fs, D), lambda p, i: (p, l, 0, 0))
    osp = pl.BlockSpec((None, tm, Fs), lambda p, i: (p, i, 0))
    osh = jax.ShapeDtypeStruct((N_CHIPS, M, Fs), BF16)
    return pl.pallas_call(
        body, name="ffn_up", grid=(N_CHIPS, M // tm),
        in_specs=[pl.BlockSpec((tm, D), lambda p, i: (i, 0)), wsp, wsp],
        out_specs=[osp, osp, osp], out_shape=[osh, osh, osh],
        compiler_params=_params(("parallel", "parallel")),
    )(xb, wgt, wut)


def mm_ln(a, w, l, x, gam, bet, alpha, scale, a_piece_major):
    M, D = x.shape
    R = w.shape[2]
    tm = _tile(M, 256)
    if a_piece_major:
        a_spec = pl.BlockSpec((N_CHIPS, tm, R), lambda i: (0, i, 0))
    else:
        a_spec = pl.BlockSpec((tm, N_CHIPS * R), lambda i: (i, 0))

    def body(a_ref, w_ref, x_ref, g_ref, b_ref, xo_ref, xb_ref, xh_ref, rs_ref):
        y = None
        for p in range(N_CHIPS):
            a = a_ref[p] if a_piece_major else a_ref[:, p * R:(p + 1) * R]
            d = jnp.dot(a, w_ref[p], preferred_element_type=F32)
            y = d if y is None else y + d
        pre = alpha * x_ref[...] + scale * y
        mu = jnp.mean(pre, axis=-1, keepdims=True)
        cen = pre - mu
        var = jnp.mean(cen * cen, axis=-1, keepdims=True)
        rstd = lax.rsqrt(var + LN_EPS)
        xhat = cen * rstd
        out = xhat * g_ref[...] + b_ref[...]
        xo_ref[...] = out
        xb_ref[...] = out.astype(BF16)
        xh_ref[...] = xhat
        rs_ref[...] = rstd

    row = pl.BlockSpec((tm, D), lambda i: (i, 0))
    vec = pl.BlockSpec((1, D), lambda i: (0, 0))
    return pl.pallas_call(
        body, name="mm_ln", grid=(M // tm,),
        in_specs=[a_spec, pl.BlockSpec((N_CHIPS, None, R, D), lambda i: (0, l, 0, 0)), row, vec, vec],
        out_specs=[row, row, row, pl.BlockSpec((tm, 1), lambda i: (i, 0))],
        out_shape=[jax.ShapeDtypeStruct((M, D), F32), jax.ShapeDtypeStruct((M, D), BF16),
                   jax.ShapeDtypeStruct((M, D), F32), jax.ShapeDtypeStruct((M, 1), F32)],
        compiler_params=_params(("parallel",)),
    )(a, w, x, gam, bet)


def ln_bwd(dy, xhat, rstd, gam, scale, dgam, dbet, row):
    M, D = dy.shape
    tm = _tile(M, 512)

    def body(dy_ref, xh_ref, rs_ref, g_ref, _dg, _db, dp_ref, db16_ref, dg_ref, dbt_ref):
        i = pl.program_id(0)
        dy_v = dy_ref[...]
        xh = xh_ref[...]
        dxh = dy_v * g_ref[...]
        m1 = jnp.mean(dxh, axis=-1, keepdims=True)
        m2 = jnp.mean(dxh * xh, axis=-1, keepdims=True)
        dpre = rs_ref[...] * (dxh - m1 - xh * m2)
        dp_ref[...] = dpre
        db16_ref[...] = (scale * dpre).astype(BF16)
        dgp = jnp.sum(dy_v * xh, axis=0, keepdims=True)
        dbp = jnp.sum(dy_v, axis=0, keepdims=True)

        @pl.when(i == 0)
        def _():
            dg_ref[...] = dgp
            dbt_ref[...] = dbp

        @pl.when(i > 0)
        def _():
            dg_ref[...] += dgp
            dbt_ref[...] += dbp

    tok = pl.BlockSpec((tm, D), lambda i: (i, 0))
    vec = pl.BlockSpec((1, D), lambda i: (0, 0))
    acc = pl.BlockSpec((None, 1, D), lambda i: (row, 0, 0))
    return pl.pallas_call(
        body, name="ln_bwd", grid=(M // tm,),
        in_specs=[tok, tok, pl.BlockSpec((tm, 1), lambda i: (i, 0)), vec, ANY, ANY],
        out_specs=[tok, tok, acc, acc],
        out_shape=[jax.ShapeDtypeStruct((M, D), F32), jax.ShapeDtypeStruct((M, D), BF16),
                   jax.ShapeDtypeStruct(dgam.shape, F32), jax.ShapeDtypeStruct(dbet.shape, F32)],
        input_output_aliases={4: 2, 5: 3},
        compiler_params=_params(("arbitrary",)),
    )(dy, xhat, rstd, gam, dgam, dbet)


def loss_head(y, tgt):
    M, D = y.shape
    tm = _tile(M, 512)
    n = M // tm

    def body(y_ref, t_ref, dy_ref, l_ref, acc_ref):
        i = pl.program_id(0)
        e = y_ref[...] - t_ref[...]
        dy_ref[...] = e * (1.0 / D)
        part = jnp.sum(e * e, axis=0, keepdims=True)

        @pl.when(i == 0)
        def _():
            acc_ref[...] = part

        @pl.when(i > 0)
        def _():
            acc_ref[...] += part

        @pl.when(i == n - 1)
        def _():
            l_ref[...] = (0.5 / D) * jnp.sum(acc_ref[...], axis=1, keepdims=True)

    row = pl.BlockSpec((tm, D), lambda i: (i, 0))
    return pl.pallas_call(
        body, name="loss_head", grid=(n,),
        in_specs=[row, row],
        out_specs=[row, pl.BlockSpec((1, 1), lambda i: (0, 0))],
        out_shape=[jax.ShapeDtypeStruct((M, D), F32), jax.ShapeDtypeStruct((1, 1), F32)],
        scratch_shapes=[pltpu.VMEM((1, D), F32)],
        compiler_params=_params(("arbitrary",)),
    )(y, tgt)


def _rel_onehot_t():
    r = lax.broadcasted_iota(jnp.int32, (REL_PAD, VR_W), 0)
    n = lax.broadcasted_iota(jnp.int32, (REL_PAD, VR_W), 1)
    idx = jnp.clip(VR_C0 - n, -REL_CLIP, REL_CLIP) + REL_CLIP
    return (r == idx).astype(F32)


def bias_vec(tab_t):
    H = tab_t.shape[0]

    def body(t_ref, o_ref):
        o_ref[...] = jnp.dot(t_ref[...], _rel_onehot_t(), precision=lax.Precision.HIGHEST,
                             preferred_element_type=F32)

    return pl.pallas_call(
        body, name="bias_vec", out_shape=jax.ShapeDtypeStruct((H, VR_W), F32),
        compiler_params=_params(),
    )(tab_t)


def bias_vec_bwd(dvr):
    n, H, _ = dvr.shape

    def body(d_ref, o_ref):
        tot = d_ref[0]
        for i in range(1, n):
            tot = tot + d_ref[i]
        o_ref[...] = lax.dot_general(tot, _rel_onehot_t(), NT_DIMS, precision=lax.Precision.HIGHEST,
                                     preferred_element_type=F32)

    return pl.pallas_call(
        body, name="bias_vec_bwd", out_shape=jax.ShapeDtypeStruct((H, REL_PAD), F32),
        compiler_params=_params(),
    )(dvr)


def _a_bias_mask(vr_row):
    xb = jnp.broadcast_to(vr_row, (QB_A, VR_W))
    tile = pltpu.roll(xb, VR_W - (QB_A - 1), 1, stride=1, stride_axis=0)[:, :KW_A]
    qc = lax.broadcasted_iota(jnp.int32, (QB_A, KW_A), 0) // CHUNK
    kc = lax.broadcasted_iota(jnp.int32, (QB_A, KW_A), 1) // CHUNK
    valid = (kc >= qc) & (kc <= qc + LEFT_CHUNKS)
    return jnp.where(valid, tile, NEG)


def _a_diag_sums(db_acc, h):
    acc8 = None
    for a in range(QB_A // 8):
        grp = db_acc[h, 8 * a:8 * a + 8, :]
        shift = QB_A - 8 - 8 * a
        if shift:
            grp = pltpu.roll(grp, shift, 1)
        acc8 = grp if acc8 is None else acc8 + grp
    sub = lax.broadcasted_iota(jnp.int32, (8, VR_W), 0)
    tot = jnp.zeros((8, VR_W), F32)
    for b in range(8):
        moved = pltpu.roll(acc8, 7 - b, 1) if b < 7 else acc8
        tot = tot + jnp.where(sub == b, moved, 0.0)
    return jnp.sum(tot, axis=0, keepdims=True)


def _a_blocks(S):
    out = []
    for qi in range(S // QB_A):
        q0 = qi * QB_A
        ks = max(0, q0 - LOOKBACK)
        out.append((q0, ks, q0 + QB_A, ks - (q0 - LOOKBACK)))
    return out


def _head_specs(S, HP):
    q = pl.BlockSpec((S, 2 * HEAD_DIM), lambda b, hp: (b, hp))
    k = pl.BlockSpec((S, 2 * HEAD_DIM), lambda b, hp: (b, HP + hp))
    v = pl.BlockSpec((S, 2 * HEAD_DIM), lambda b, hp: (b, 2 * HP + hp))
    return q, k, v


def attn_a_fwd(qkv, vr, B, S):
    D = qkv.shape[1] // 3
    HP = D // (2 * HEAD_DIM)
    scale = HEAD_DIM ** -0.5
    blocks = _a_blocks(S)

    def body(q_ref, k_ref, v_ref, vr_ref, o_ref):
        heads = [slice(h * HEAD_DIM, (h + 1) * HEAD_DIM) for h in range(2)]
        bms = [_a_bias_mask(vr_ref[h:h + 1, :]) for h in range(2)]
        for (q0, ks, ke, joff) in blocks:
            rows = slice(q0, q0 + QB_A)
            ss = [lax.dot_general(q_ref[rows, hs], k_ref[ks:ke, hs], NT_DIMS,
                                  preferred_element_type=F32) * scale + bm[:, joff:]
                  for hs, bm in zip(heads, bms)]
            ps = [jnp.exp(s - jnp.max(s, axis=-1, keepdims=True)) for s in ss]
            for hs, p in zip(heads, ps):
                den = jnp.sum(p, axis=-1, keepdims=True)
                o = jnp.dot(p.astype(BF16), v_ref[ks:ke, hs], preferred_element_type=F32) / den
                o_ref[rows, hs] = o.astype(BF16)

    qs, ks_, vs = _head_specs(S, HP)
    return pl.pallas_call(
        body, name="attn_a_fwd", grid=(B, HP),
        in_specs=[qs, ks_, vs, pl.BlockSpec((None, 2, VR_W), lambda b, hp: (hp, 0, 0))],
        out_specs=pl.BlockSpec((S, 2 * HEAD_DIM), lambda b, hp: (b, hp)),
        out_shape=jax.ShapeDtypeStruct((B * S, D), BF16),
        compiler_params=_params(("parallel", "parallel")),
    )(qkv, qkv, qkv, vr)


def attn_a_bwd(qkv, vr, do, B, S):
    D = qkv.shape[1] // 3
    HP = D // (2 * HEAD_DIM)
    scale = HEAD_DIM ** -0.5
    blocks = _a_blocks(S)

    def body(q_ref, k_ref, v_ref, vr_ref, do_ref, dqkv_ref, dvr_ref, dkt_acc, dvt_acc, db_acc):
        dkt_acc[...] = jnp.zeros_like(dkt_acc)
        dvt_acc[...] = jnp.zeros_like(dvt_acc)
        db_acc[...] = jnp.zeros_like(db_acc)
        heads = [slice(h * HEAD_DIM, (h + 1) * HEAD_DIM) for h in range(2)]
        bms = [_a_bias_mask(vr_ref[h:h + 1, :]) for h in range(2)]
        for (q0, ks, ke, joff) in blocks:
            rows = slice(q0, q0 + QB_A)
            qt_pair = q_ref[rows, :].astype(F32).T.astype(BF16)
            dot_pair = do_ref[rows, :].astype(F32).T.astype(BF16)
            ss = [lax.dot_general(q_ref[rows, hs], k_ref[ks:ke, hs], NT_DIMS,
                                  preferred_element_type=F32) * scale + bm[:, joff:]
                  for hs, bm in zip(heads, bms)]
            dps = [lax.dot_general(do_ref[rows, hs], v_ref[ks:ke, hs], NT_DIMS, preferred_element_type=F32)
                   for hs in heads]
            ps, dsbs = [], []
            for h, (s, dp) in enumerate(zip(ss, dps)):
                e = jnp.exp(s - jnp.max(s, axis=-1, keepdims=True))
                p = e / jnp.sum(e, axis=-1, keepdims=True)
                ds = p * (dp - jnp.sum(p * dp, axis=-1, keepdims=True))
                db_acc[h, :, joff:KW_A] += ds
                ps.append(p.astype(BF16))
                dsbs.append(ds.astype(BF16))
            for hs, p, dsb in zip(heads, ps, dsbs):
                dq = jnp.dot(dsb, k_ref[ks:ke, hs], preferred_element_type=F32) * scale
                dqkv_ref[0, rows, hs] = dq.astype(BF16)
                dkt_acc[hs, ks:ke] += jnp.dot(qt_pair[hs, :], dsb, preferred_element_type=F32)
                dvt_acc[hs, ks:ke] += jnp.dot(dot_pair[hs, :], p, preferred_element_type=F32)
        for h in range(2):
            dvr_ref[h:h + 1, :] = _a_diag_sums(db_acc, h)
        dqkv_ref[1] = (dkt_acc[...].T * scale).astype(BF16)
        dqkv_ref[2] = dvt_acc[...].T.astype(BF16)

    qs, ks_, vs = _head_specs(S, HP)
    hd = pl.BlockSpec((S, 2 * HEAD_DIM), lambda b, hp: (b, hp))
    return pl.pallas_call(
        body, name="attn_a_bwd", grid=(B, HP),
        in_specs=[qs, ks_, vs, pl.BlockSpec((None, 2, VR_W), lambda b, hp: (hp, 0, 0)), hd],
        out_specs=[pl.BlockSpec((3, S, 2 * HEAD_DIM), lambda b, hp: (0, b, hp)),
                   pl.BlockSpec((None, None, 2, VR_W), lambda b, hp: (b, hp, 0, 0))],
        out_shape=[jax.ShapeDtypeStruct((3, B * S, D), BF16), jax.ShapeDtypeStruct((B, HP, 2, VR_W), F32)],
        scratch_shapes=[pltpu.VMEM((2 * HEAD_DIM, S), F32), pltpu.VMEM((2 * HEAD_DIM, S), F32),
                        pltpu.VMEM((2, QB_A, VR_W), F32)],
        compiler_params=_params(("parallel", "parallel")),
    )(qkv, qkv, qkv, vr, do)


def _tri(cmp):
    j = lax.broadcasted_iota(jnp.int32, (SB_TILE, SB_TILE), 0)
    s = lax.broadcasted_iota(jnp.int32, (SB_TILE, SB_TILE), 1)
    return cmp(j, s).astype(BF16)


def _cumsum_mm(x, tri):
    hi = x.astype(BF16)
    mid = (x - hi.astype(F32)).astype(BF16)
    return jnp.dot(hi, tri, preferred_element_type=F32) + jnp.dot(mid, tri, preferred_element_type=F32)


def _sb_logs(q, k, scale, diagonal):
    z = lax.dot_general(q, k, NT_DIMS, preferred_element_type=F32) * scale
    log_b = jnp.minimum(z, 0.0) - jnp.log(1.0 + jnp.exp(-jnp.abs(z)))
    log_1mb = log_b - z
    if not diagonal:
        return log_b, log_1mb, None
    row = lax.broadcasted_iota(jnp.int32, (SB_TILE, SB_TILE), 0)
    col = lax.broadcasted_iota(jnp.int32, (SB_TILE, SB_TILE), 1)
    causal = col < row
    return log_b, jnp.where(causal, log_1mb, 0.0), causal


def attn_b_fwd(qkv, B, S):
    D = qkv.shape[1] // 3
    HP = D // (2 * HEAD_DIM)
    scale = HEAD_DIM ** -0.5
    nb = S // SB_TILE

    def body(q_ref, k_ref, v_ref, o_ref, nt_ref):
        tri = _tri(lambda j, s: j > s)
        heads = [slice(h * HEAD_DIM, (h + 1) * HEAD_DIM) for h in range(2)]

        def q_block(qb, n_pairs, parity):
            q0 = pl.multiple_of(qb * SB_TILE, SB_TILE)
            rows = pl.ds(q0, SB_TILE)
            qs = [q_ref[rows, hs] for hs in heads]

            def step(blocks, state):
                chains = [(h, kb, diagonal, pl.ds(pl.multiple_of(kb * SB_TILE, SB_TILE), SB_TILE))
                          for h in range(2) for kb, diagonal in blocks]
                logs = [_sb_logs(qs[h], k_ref[keys, heads[h]], scale, diagonal)
                        for h, _, diagonal, keys in chains]
                sums = [_cumsum_mm(log_1mb, tri) for _, log_1mb, _ in logs]
                rights = [state[0][0], state[1][0]]
                accs = [state[0][1], state[1][1]]
                for (h, _, diagonal, keys), (log_b, log_1mb, causal), csum in zip(chains, logs, sums):
                    a = jnp.exp(log_b + csum + rights[h])
                    if diagonal:
                        a = jnp.where(causal, a, 0.0)
                    accs[h] = accs[h] + jnp.dot(a.astype(BF16), v_ref[keys, heads[h]],
                                                preferred_element_type=F32)
                    rights[h] = rights[h] + jnp.sum(log_1mb, axis=-1, keepdims=True)
                return ((rights[0], accs[0]), (rights[1], accs[1]))

            zero = (jnp.zeros((SB_TILE, 1), F32), jnp.zeros((SB_TILE, HEAD_DIM), F32))
            first = [(qb, True)] + ([(qb - 1, False)] if parity else [])
            top = qb - len(first)
            state = lax.fori_loop(
                0, n_pairs, lambda t, st: step([(top - 2 * t, False), (top - 2 * t - 1, False)], st),
                step(first, (zero, zero)))
            for hs, (right, acc) in zip(heads, state):
                o_ref[rows, hs] = acc.astype(BF16)
                nt_ref[rows, hs] = jnp.broadcast_to(right, (SB_TILE, HEAD_DIM))

        def q_pair_loop(j, carry):
            q_block(2 * j, j, 0)
            q_block(2 * j + 1, j, 1)
            return carry

        lax.fori_loop(0, nb // 2, q_pair_loop, 0)

    qs, ks_, vs = _head_specs(S, HP)
    hd = pl.BlockSpec((S, 2 * HEAD_DIM), lambda b, hp: (b, hp))
    return pl.pallas_call(
        body, name="attn_b_fwd", grid=(B, HP),
        in_specs=[qs, ks_, vs], out_specs=[hd, hd],
        out_shape=[jax.ShapeDtypeStruct((B * S, D), BF16), jax.ShapeDtypeStruct((B * S, D), F32)],
        compiler_params=_params(("parallel", "parallel")),
    )(qkv, qkv, qkv)


def attn_b_bwd(qkv, do, ntot, B, S):
    D = qkv.shape[1] // 3
    HP = D // (2 * HEAD_DIM)
    scale = HEAD_DIM ** -0.5
    nb = S // SB_TILE

    def body(q_ref, k_ref, v_ref, do_ref, nt_ref, dqkv_ref, dkt_acc, dvt_acc):
        tri_incl = _tri(lambda j, s: j <= s)
        tri_excl = _tri(lambda j, s: j < s)
        heads = [slice(h * HEAD_DIM, (h + 1) * HEAD_DIM) for h in range(2)]
        dkt_acc[...] = jnp.zeros_like(dkt_acc)
        dvt_acc[...] = jnp.zeros_like(dvt_acc)

        def q_block(qb, n_pairs, parity):
            q0 = pl.multiple_of(qb * SB_TILE, SB_TILE)
            rows = pl.ds(q0, SB_TILE)
            qt_pair = q_ref[rows, :].astype(F32).T.astype(BF16)
            dot_pair = do_ref[rows, :].astype(F32).T.astype(BF16)
            per_head = [(hs, q_ref[rows, hs], do_ref[rows, hs], qt_pair[hs, :], dot_pair[hs, :],
                         nt_ref[rows, hs.start:hs.start + 1]) for hs in heads]

            def step(blocks, state):
                chains = [(h, kb, diagonal, pl.ds(pl.multiple_of(kb * SB_TILE, SB_TILE), SB_TILE))
                          for h in range(2) for kb, diagonal in blocks]
                ks = [k_ref[keys, per_head[h][0]] for h, _, _, keys in chains]
                logs = [_sb_logs(per_head[h][1], k, scale, diagonal)
                        for (h, _, diagonal, _), k in zip(chains, ks)]
                das = [lax.dot_general(per_head[h][2], v_ref[keys, per_head[h][0]], NT_DIMS,
                                       preferred_element_type=F32) for h, _, _, keys in chains]
                sums = [_cumsum_mm(log_1mb, tri_incl) for _, log_1mb, _ in logs]
                left_n = [state[0][0], state[1][0]]
                left_d = [state[0][1], state[1][1]]
                dq_acc = [state[0][2], state[1][2]]
                a_s, dls = [], []
                for (h, _, diagonal, _), (log_b, log_1mb, causal), csum, da in zip(chains, logs, sums, das):
                    a = jnp.exp(log_b + (per_head[h][5] - left_n[h]) - csum)
                    if diagonal:
                        a = jnp.where(causal, a, 0.0)
                    a_s.append(a)
                    dls.append(a * da)
                    left_n[h] = left_n[h] + jnp.sum(log_1mb, axis=-1, keepdims=True)
                dsums = [_cumsum_mm(dl, tri_excl) for dl in dls]
                dzbs = []
                for (h, _, diagonal, _), (log_b, log_1mb, causal), dl, dsum in zip(chains, logs, dls, dsums):
                    dz = dl * jnp.exp(log_1mb) - (left_d[h] + dsum) * jnp.exp(log_b)
                    if diagonal:
                        dz = jnp.where(causal, dz, 0.0)
                    dzbs.append(dz.astype(BF16))
                    left_d[h] = left_d[h] + jnp.sum(dl, axis=-1, keepdims=True)
                for (h, kb, _, _), k, a, dzb in zip(chains, ks, a_s, dzbs):
                    hs, _, _, qt, dot_, _ = per_head[h]
                    dq_acc[h] = dq_acc[h] + jnp.dot(dzb, k, preferred_element_type=F32)
                    dkt_acc[kb, hs, :] += jnp.dot(qt, dzb, preferred_element_type=F32)
                    dvt_acc[kb, hs, :] += jnp.dot(dot_, a.astype(BF16), preferred_element_type=F32)
                return ((left_n[0], left_d[0], dq_acc[0]), (left_n[1], left_d[1], dq_acc[1]))

            zero1 = jnp.zeros((SB_TILE, 1), F32)
            zero = (zero1, zero1, jnp.zeros((SB_TILE, HEAD_DIM), F32))
            state = lax.fori_loop(
                0, n_pairs, lambda t, st: step([(2 * t, False), (2 * t + 1, False)], st), (zero, zero))
            last = ([(qb - 1, False)] if parity else []) + [(qb, True)]
            state = step(last, state)
            for hs, (_, _, dq_acc) in zip(heads, state):
                dqkv_ref[0, rows, hs] = (dq_acc * scale).astype(BF16)

        def q_pair_loop(j, carry):
            q_block(2 * j, j, 0)
            q_block(2 * j + 1, j, 1)
            return carry

        lax.fori_loop(0, nb // 2, q_pair_loop, 0)
        for kb in range(nb):
            dqkv_ref[1, kb * SB_TILE:(kb + 1) * SB_TILE, :] = (dkt_acc[kb].T * scale).astype(BF16)
            dqkv_ref[2, kb * SB_TILE:(kb + 1) * SB_TILE, :] = dvt_acc[kb].T.astype(BF16)

    qs, ks_, vs = _head_specs(S, HP)
    hd = pl.BlockSpec((S, 2 * HEAD_DIM), lambda b, hp: (b, hp))
    acc = pltpu.VMEM((nb, 2 * HEAD_DIM, SB_TILE), F32)
    return pl.pallas_call(
        body, name="attn_b_bwd", grid=(B, HP),
        in_specs=[qs, ks_, vs, hd, hd],
        out_specs=pl.BlockSpec((3, S, 2 * HEAD_DIM), lambda b, hp: (0, b, hp)),
        out_shape=jax.ShapeDtypeStruct((3, B * S, D), BF16),
        scratch_shapes=[acc, acc],
        compiler_params=_params(("parallel", "parallel")),
    )(qkv, qkv, qkv, do, ntot)


def _place():
    x, y, c = lax.axis_index("x"), lax.axis_index("y"), lax.axis_index("c")
    chips = [(1 - x, y), (x, 1 - y), (1 - x, 1 - y)]
    return x, y, c, 2 * x + y, chips


def _remote(src, dst, send_sem, recv_sem, dev):
    return pltpu.make_async_remote_copy(src_ref=src, dst_ref=dst, send_sem=send_sem, recv_sem=recv_sem,
                                        device_id=dev, device_id_type=MESH)


def gather_weights(shards):
    n = len(shards)

    def body(*refs):
        ins, outs = refs[:n], refs[n:2 * n]
        send1, recv1, send2, recv2, lsem = refs[2 * n:]
        x, y, c, me, chips = _place()
        local, first = [], []
        for f in range(n):
            hl = shards[f].shape[0] // 2
            cp = pltpu.make_async_copy(ins[f], outs[f].at[me], lsem.at[f])
            cp.start()
            local.append(cp)
            for j, (qx, qy) in enumerate(chips):
                half = pl.ds(c * hl, hl)
                cp = _remote(ins[f].at[half], outs[f].at[me, half],
                             send1.at[3 * f + j], recv1.at[3 * f + j], (qx, qy, c))
                cp.start()
                first.append(cp)
        passed = []
        for f in range(n):
            hl = shards[f].shape[0] // 2
            for j, (qx, qy) in enumerate(chips):
                slab = outs[f].at[2 * qx + qy, pl.ds(c * hl, hl)]
                _remote(slab, slab, send1.at[3 * f + j], recv1.at[3 * f + j], (x, y, c)).wait_recv()
                cp = _remote(slab, slab, send2.at[3 * f + j], recv2.at[3 * f + j], (x, y, 1 - c))
                cp.start()
                passed.append(cp)
        for f in range(n):
            hl = shards[f].shape[0] // 2
            for j, (qx, qy) in enumerate(chips):
                slab = outs[f].at[2 * qx + qy, pl.ds((1 - c) * hl, hl)]
                _remote(slab, slab, send2.at[3 * f + j], recv2.at[3 * f + j], (x, y, c)).wait_recv()
        for cp in first + passed:
            cp.wait_send()
        for cp in local:
            cp.wait()

    sems = pltpu.SemaphoreType.DMA((3 * n,))
    return pl.pallas_call(
        body, name="gather_weights",
        in_specs=[ANY] * n, out_specs=[ANY] * n,
        out_shape=[jax.ShapeDtypeStruct((N_CHIPS,) + s.shape, s.dtype) for s in shards],
        scratch_shapes=[sems, sems, sems, sems, pltpu.SemaphoreType.DMA((n,))],
        compiler_params=pltpu.CompilerParams(has_side_effects=True),
    )(*shards)


HBM = pl.BlockSpec(memory_space=pltpu.HBM)
SEM = pl.BlockSpec(memory_space=pltpu.SEMAPHORE)
EFFECT = pltpu.SideEffectType.DATAFLOW_SIDE_EFFECTING


def _in_hbm(a):
    return pltpu.with_memory_space_constraint(a, pltpu.HBM)


def _row_half(ref, _unused, which):
    hr = ref.shape[-2] // 2
    idx = [pl.ds(0, d) for d in ref.shape[:-2]] + [pl.ds(which * hr, hr), pl.ds(0, ref.shape[-1])]
    return ref.at[tuple(idx)]


def cast_place(ids, w, lead, L):
    R, C = w.shape[-2:]

    def body(ids_ref, w_ref, s_ref, land_ref):
        v = w_ref[...].astype(BF16)
        s_ref[...] = v
        land_ref[...] = v

    return pl.pallas_call(
        body, name="cast_place",
        grid_spec=pltpu.PrefetchScalarGridSpec(
            num_scalar_prefetch=1, grid=(L,),
            in_specs=[pl.BlockSpec((None,) * (w.ndim - 2) + (R, C), lambda l, ids: (*lead(l), 0, 0))],
            out_specs=[pl.BlockSpec((None, R, C), lambda l, ids: (l, 0, 0)),
                       pl.BlockSpec((None, None, R, C), lambda l, ids: (ids[1], l, 0, 0))]),
        out_shape=[jax.ShapeDtypeStruct((L, R, C), BF16), jax.ShapeDtypeStruct((N_CHIPS, L, R, C), BF16)],
        compiler_params=_params(("parallel",)),
    )(ids, w)


def gather_start(layers, zones):
    flat = [s for lay in layers for s in lay]
    flat_zones = [z for lay in zones for z in lay]
    counts = [len(lay) for lay in layers]
    n, nl = len(flat), len(layers)

    def body(*refs):
        ins, lands = refs[:n], refs[n:2 * n]
        send, recv = refs[2 * n:2 * n + nl], refs[2 * n + nl:2 * n + 2 * nl]
        token = refs[-1]
        x, y, c, me, chips = _place()
        f = 0
        for li, cnt in enumerate(counts):
            for k in range(cnt):
                for j, (qx, qy) in enumerate(chips):
                    _remote(_row_half(ins[f], 0, c), _row_half(lands[f].at[me], 0, c),
                            send[li].at[3 * k + j], recv[li].at[3 * k + j], (qx, qy, c)).start()
                f += 1
        token[...] = jnp.zeros_like(token)

    sem_shapes = [pltpu.SemaphoreType.DMA((3 * cnt,)) for cnt in counts]
    land_shapes = [(N_CHIPS,) + s.shape for s in flat]
    res = pl.pallas_call(
        body, name="gather_start",
        out_shape=(*sem_shapes, *sem_shapes,
                   *[pltpu.HBM(s.shape, s.dtype) for s in flat],
                   *[pltpu.HBM(shp, s.dtype) for shp, s in zip(land_shapes, flat)],
                   jax.ShapeDtypeStruct((8, 128), F32)),
        in_specs=[HBM] * (2 * n),
        out_specs=(*[SEM] * (2 * nl), *[HBM] * (2 * n), pl.BlockSpec(memory_space=pltpu.VMEM)),
        input_output_aliases={k: 2 * nl + k for k in range(2 * n)},
        compiler_params=pltpu.CompilerParams(has_side_effects=EFFECT),
    )(*[_in_hbm(s) for s in flat], *[_in_hbm(z) for z in flat_zones])
    send, recv = res[:nl], res[nl:2 * nl]
    thru, lands, token = res[2 * nl:2 * nl + n], res[2 * nl + n:2 * nl + 2 * n], res[-1]
    out, f = [], 0
    for li, cnt in enumerate(counts):
        out.append((send[li], recv[li], list(thru[f:f + cnt]), list(lands[f:f + cnt])))
        f += cnt
    return out, token


def gather_wait(li, send, recv, shards, lands, after):
    m = len(shards)

    def body(*refs):
        ins, lnd = refs[:m], refs[m:2 * m]
        snd, rcv = refs[2 * m], refs[2 * m + 1]
        x, y, c, me, chips = _place()
        for k in range(m):
            for j, (qx, qy) in enumerate(chips):
                cp = _remote(_row_half(ins[k], 0, c), _row_half(lnd[k].at[2 * qx + qy], 0, c),
                             snd.at[3 * k + j], rcv.at[3 * k + j], (qx, qy, c))
                cp.wait_send()
                cp.wait_recv()

    res = pl.pallas_call(
        body, name=f"gather_wait_{li}",
        out_shape=(*[pltpu.HBM(s.shape, s.dtype) for s in shards],
                   *[pltpu.HBM(s.shape, s.dtype) for s in lands]),
        in_specs=[HBM] * (2 * m) + [SEM, SEM, ANY], out_specs=[HBM] * (2 * m),
        input_output_aliases={k: k for k in range(2 * m)},
        compiler_params=pltpu.CompilerParams(has_side_effects=EFFECT),
    )(*shards, *lands, send, recv, after)
    return list(res[:m]), list(res[m:])


def gather_forward(li, lands):
    m = len(lands)

    def body(*refs):
        outs = refs[m:2 * m]
        send, recv = refs[2 * m:]
        x, y, c, me, chips = _place()
        passed = []
        for k in range(m):
            for j, (qx, qy) in enumerate(chips):
                slab = _row_half(outs[k].at[2 * qx + qy], 0, c)
                cp = _remote(slab, slab, send.at[3 * k + j], recv.at[3 * k + j], (x, y, 1 - c))
                cp.start()
                passed.append(cp)
        for k in range(m):
            for j, (qx, qy) in enumerate(chips):
                slab = _row_half(outs[k].at[2 * qx + qy], 0, 1 - c)
                _remote(slab, slab, send.at[3 * k + j], recv.at[3 * k + j], (x, y, c)).wait_recv()
        for cp in passed:
            cp.wait_send()

    sems = pltpu.SemaphoreType.DMA((3 * m,))
    return pl.pallas_call(
        body, name=f"gather_forward_{li}",
        in_specs=[ANY] * m, out_specs=[ANY] * m,
        out_shape=[jax.ShapeDtypeStruct(s.shape, s.dtype) for s in lands],
        input_output_aliases={k: k for k in range(m)},
        scratch_shapes=[sems, sems],
        compiler_params=pltpu.CompilerParams(has_side_effects=True),
    )(*lands)


def pair_exchange_rows(li, grads):
    n = len(grads)

    def body(*refs):
        ins, outs = refs[:n], refs[n:2 * n]
        send, recv = refs[2 * n:]
        x, y, c, _, _ = _place()
        cps = []
        for k in range(n):
            cp = _remote(_row_half(ins[k], 0, 1 - c), outs[k], send.at[k], recv.at[k], (x, y, 1 - c))
            cp.start()
            cps.append(cp)
        for cp in cps:
            cp.wait()

    sems = pltpu.SemaphoreType.DMA((n,))
    return pl.pallas_call(
        body, name=f"pair_exchange_{li}",
        in_specs=[ANY] * n, out_specs=[ANY] * n,
        out_shape=[jax.ShapeDtypeStruct(g.shape[:-2] + (g.shape[-2] // 2, g.shape[-1]), g.dtype)
                   for g in grads],
        scratch_shapes=[sems, sems],
        compiler_params=pltpu.CompilerParams(has_side_effects=True),
    )(*grads)


def pair_add_rows(ids, grad, recv):
    L, P, hr, C = recv.shape

    def body(ids_ref, a_ref, b_ref, o_ref):
        o_ref[...] = (a_ref[...].astype(F32) + b_ref[...].astype(F32)).astype(o_ref.dtype)

    blk = (None, None, hr, C)
    return pl.pallas_call(
        body, name="pair_add",
        grid_spec=pltpu.PrefetchScalarGridSpec(
            num_scalar_prefetch=1, grid=(L, P),
            in_specs=[pl.BlockSpec(blk, lambda l, p, ids: (l, p, ids[0], 0)),
                      pl.BlockSpec(blk, lambda l, p, ids: (l, p, 0, 0))],
            out_specs=pl.BlockSpec(blk, lambda l, p, ids: (l, p, 0, 0))),
        out_shape=jax.ShapeDtypeStruct(recv.shape, recv.dtype),
        compiler_params=_params(("parallel", "parallel")),
    )(ids, grad, recv)


def reduce_start(li, parts):
    m = len(parts)

    def body(*refs):
        ins, lands = refs[:m], refs[m:2 * m]
        send, recv = refs[2 * m], refs[2 * m + 1]
        token = refs[-1]
        x, y, c, me, chips = _place()
        for k in range(m):
            rows = pl.ds(0, parts[k].shape[0])
            for j, (qx, qy) in enumerate(chips):
                _remote(ins[k].at[rows, 2 * qx + qy], lands[k].at[rows, me],
                        send.at[3 * k + j], recv.at[3 * k + j], (qx, qy, c)).start()
        token[...] = jnp.zeros_like(token)

    sems = pltpu.SemaphoreType.DMA((3 * m,))
    res = pl.pallas_call(
        body, name=f"reduce_start_{li}",
        out_shape=(sems, sems, *[pltpu.HBM(s.shape, s.dtype) for s in parts],
                   *[pltpu.HBM(s.shape, s.dtype) for s in parts], jax.ShapeDtypeStruct((8, 128), F32)),
        in_specs=[HBM] * (2 * m),
        out_specs=(SEM, SEM, *[HBM] * (2 * m), pl.BlockSpec(memory_space=pltpu.VMEM)),
        input_output_aliases={k: 2 + k for k in range(2 * m)},
        compiler_params=pltpu.CompilerParams(has_side_effects=EFFECT),
    )(*[_in_hbm(s) for s in parts], *[_in_hbm(lax.empty(s.shape, s.dtype)) for s in parts])
    return res[0], res[1], list(res[2:2 + m]), list(res[2 + m:2 + 2 * m]), res[-1]


def reduce_wait(li, send, recv, parts, lands, after):
    m = len(parts)

    def body(*refs):
        ins, lnd = refs[:m], refs[m:2 * m]
        snd, rcv = refs[2 * m], refs[2 * m + 1]
        x, y, c, me, chips = _place()
        for k in range(m):
            rows = pl.ds(0, parts[k].shape[0])
            for j, (qx, qy) in enumerate(chips):
                cp = _remote(ins[k].at[rows, 2 * qx + qy], lnd[k].at[rows, 2 * qx + qy],
                             snd.at[3 * k + j], rcv.at[3 * k + j], (qx, qy, c))
                cp.wait_send()
                cp.wait_recv()

    res = pl.pallas_call(
        body, name=f"reduce_wait_{li}",
        out_shape=(*[pltpu.HBM(s.shape, s.dtype) for s in parts],
                   *[pltpu.HBM(s.shape, s.dtype) for s in lands]),
        in_specs=[HBM] * (2 * m) + [SEM, SEM, ANY], out_specs=[HBM] * (2 * m),
        input_output_aliases={k: k for k in range(2 * m)},
        compiler_params=pltpu.CompilerParams(has_side_effects=EFFECT),
    )(*parts, *lands, send, recv, after)
    return list(res[:m]), list(res[m:])


def chip_sum_rows(ids, land, part, gfull, l0):
    L, P, hr, C = land.shape

    def body(ids_ref, land_ref, part_ref, _g, o_ref):
        tot = None
        for q in range(P):
            term = jnp.where(ids_ref[1] == q, part_ref[...], land_ref[q]).astype(F32)
            tot = term if tot is None else tot + term
        o_ref[...] = tot

    return pl.pallas_call(
        body, name="chip_sum",
        grid_spec=pltpu.PrefetchScalarGridSpec(
            num_scalar_prefetch=1, grid=(L,),
            in_specs=[pl.BlockSpec((None, P, hr, C), lambda l, ids: (l, 0, 0, 0)),
                      pl.BlockSpec((None, None, hr, C), lambda l, ids: (l, ids[1], 0, 0)), ANY],
            out_specs=pl.BlockSpec((None, hr, C), lambda l, ids: (l0 + l, ids[0], 0))),
        out_shape=jax.ShapeDtypeStruct(gfull.shape, gfull.dtype),
        input_output_aliases={3: 0},
        compiler_params=_params(("arbitrary",)),
    )(ids, land, part, gfull)


def half_swap_rows(grads):
    n = len(grads)

    def body(*refs):
        outs = refs[n:2 * n]
        send, recv = refs[2 * n:]
        x, y, c, _, _ = _place()
        cps = []
        for k in range(n):
            mine = _row_half(outs[k], 0, c)
            cp = _remote(mine, mine, send.at[k], recv.at[k], (x, y, 1 - c))
            cp.start()
            cps.append(cp)
        for k in range(n):
            theirs = _row_half(outs[k], 0, 1 - c)
            _remote(theirs, theirs, send.at[k], recv.at[k], (x, y, c)).wait_recv()
        for cp in cps:
            cp.wait_send()

    sems = pltpu.SemaphoreType.DMA((n,))
    return pl.pallas_call(
        body, name="half_swap",
        in_specs=[ANY] * n, out_specs=[ANY] * n,
        out_shape=[jax.ShapeDtypeStruct(g.shape, g.dtype) for g in grads],
        input_output_aliases={k: k for k in range(n)},
        scratch_shapes=[sems, sems],
        compiler_params=pltpu.CompilerParams(has_side_effects=True),
    )(*grads)


def pair_exchange(grads):
    n = len(grads)

    def body(*refs):
        ins, outs = refs[:n], refs[n:2 * n]
        send, recv = refs[2 * n:]
        x, y, c, _, _ = _place()
        cps = []
        for f in range(n):
            hl = grads[f].shape[0] // 2
            cp = _remote(ins[f].at[pl.ds((1 - c) * hl, hl)], outs[f], send.at[f], recv.at[f], (x, y, 1 - c))
            cp.start()
            cps.append(cp)
        for cp in cps:
            cp.wait()

    sems = pltpu.SemaphoreType.DMA((n,))
    return pl.pallas_call(
        body, name="pair_exchange",
        in_specs=[ANY] * n, out_specs=[ANY] * n,
        out_shape=[jax.ShapeDtypeStruct((g.shape[0] // 2,) + g.shape[1:], g.dtype) for g in grads],
        scratch_shapes=[sems, sems],
        compiler_params=pltpu.CompilerParams(has_side_effects=True),
    )(*grads)


def chip_exchange(parts):
    n = len(parts)

    def body(*refs):
        ins, outs = refs[:n], refs[n:2 * n]
        send, recv, lsem = refs[2 * n:]
        x, y, c, me, chips = _place()
        local, sent = [], []
        for f in range(n):
            hl = parts[f].shape[0]
            rows = pl.ds(0, hl)
            cp = pltpu.make_async_copy(ins[f].at[rows, me], outs[f].at[rows, me], lsem.at[f])
            cp.start()
            local.append(cp)
            for j, (qx, qy) in enumerate(chips):
                cp = _remote(ins[f].at[rows, 2 * qx + qy], outs[f].at[rows, me],
                             send.at[3 * f + j], recv.at[3 * f + j], (qx, qy, c))
                cp.start()
                sent.append(cp)
        for f in range(n):
            rows = pl.ds(0, parts[f].shape[0])
            for j, (qx, qy) in enumerate(chips):
                slab = outs[f].at[rows, 2 * qx + qy]
                _remote(slab, slab, send.at[3 * f + j], recv.at[3 * f + j], (x, y, c)).wait_recv()
        for cp in sent:
            cp.wait_send()
        for cp in local:
            cp.wait()

    sems = pltpu.SemaphoreType.DMA((3 * n,))
    return pl.pallas_call(
        body, name="chip_exchange",
        in_specs=[ANY] * n, out_specs=[ANY] * n,
        out_shape=[jax.ShapeDtypeStruct(s.shape, s.dtype) for s in parts],
        scratch_shapes=[sems, sems, pltpu.SemaphoreType.DMA((n,))],
        compiler_params=pltpu.CompilerParams(has_side_effects=True),
    )(*parts)


def half_swap(grads):
    n = len(grads)

    def body(*refs):
        ins, outs = refs[:n], refs[n:2 * n]
        send, recv = refs[2 * n:]
        x, y, c, _, _ = _place()
        cps = []
        for f in range(n):
            hl = grads[f].shape[0] // 2
            mine = pl.ds(c * hl, hl)
            cp = _remote(outs[f].at[mine], outs[f].at[mine], send.at[f], recv.at[f], (x, y, 1 - c))
            cp.start()
            cps.append(cp)
        for f in range(n):
            hl = grads[f].shape[0] // 2
            theirs = outs[f].at[pl.ds((1 - c) * hl, hl)]
            _remote(theirs, theirs, send.at[f], recv.at[f], (x, y, c)).wait_recv()
        for cp in cps:
            cp.wait_send()

    sems = pltpu.SemaphoreType.DMA((n,))
    return pl.pallas_call(
        body, name="half_swap",
        in_specs=[ANY] * n, out_specs=[ANY] * n,
        out_shape=[jax.ShapeDtypeStruct(g.shape, g.dtype) for g in grads],
        input_output_aliases={f: f for f in range(n)},
        scratch_shapes=[sems, sems],
        compiler_params=pltpu.CompilerParams(has_side_effects=True),
    )(*grads)


def all_sum_small(v):
    R = v.shape[0]

    def body(v_ref, o_ref, land, send, recv):
        x, y, c, _, _ = _place()
        me = 4 * x + 2 * y + c
        land[me] = v_ref[...]
        peers = [(px, py, pc) for px in range(2) for py in range(2) for pc in range(2)]
        cps = []
        for k in range(1, 8):
            dev = (x ^ (k >> 2), y ^ ((k >> 1) & 1), c ^ (k & 1))
            cp = _remote(v_ref, land.at[me], send.at[k - 1], recv.at[k - 1], dev)
            cp.start()
            cps.append(cp)
        for k in range(1, 8):
            src = 4 * (x ^ (k >> 2)) + 2 * (y ^ ((k >> 1) & 1)) + (c ^ (k & 1))
            _remote(v_ref, land.at[src], send.at[k - 1], recv.at[k - 1], (x, y, c)).wait_recv()
        for cp in cps:
            cp.wait_send()
        tot = land[0]
        for d in range(1, len(peers)):
            tot = tot + land[d]
        o_ref[...] = tot

    sems = pltpu.SemaphoreType.DMA((7,))
    vm = pl.BlockSpec(memory_space=pltpu.VMEM)
    return pl.pallas_call(
        body, name="all_sum_small", in_specs=[vm], out_specs=vm,
        out_shape=jax.ShapeDtypeStruct(v.shape, F32),
        scratch_shapes=[pltpu.VMEM((8, R, 128), F32), sems, sems],
        compiler_params=pltpu.CompilerParams(has_side_effects=True),
    )(v)


def _row_tile(R):
    for t in range(min(R, 512) // 8 * 8, 7, -8):
        if R % t == 0:
            return t
    return R


def pair_add(cidx, grad, recv):
    hl, P, R, C = recv.shape
    tr = _row_tile(R)

    def body(c_ref, a_ref, b_ref, o_ref):
        o_ref[...] = (a_ref[...].astype(F32) + b_ref[...].astype(F32)).astype(o_ref.dtype)

    blk = (None, None, tr, C)
    return pl.pallas_call(
        body, name="pair_add",
        grid_spec=pltpu.PrefetchScalarGridSpec(
            num_scalar_prefetch=1, grid=(hl, P, R // tr),
            in_specs=[pl.BlockSpec(blk, lambda l, p, r, c: (c[0] * hl + l, p, r, 0)),
                      pl.BlockSpec(blk, lambda l, p, r, c: (l, p, r, 0))],
            out_specs=pl.BlockSpec(blk, lambda l, p, r, c: (l, p, r, 0))),
        out_shape=jax.ShapeDtypeStruct(recv.shape, recv.dtype),
        compiler_params=_params(("parallel", "parallel", "parallel")),
    )(cidx, grad, recv)


def chip_sum(cidx, land, L):
    hl, P, R, C = land.shape
    tr = _row_tile(R)

    def body(c_ref, a_ref, o_ref):
        tot = a_ref[0].astype(F32)
        for q in range(1, P):
            tot = tot + a_ref[q].astype(F32)
        o_ref[...] = tot

    return pl.pallas_call(
        body, name="chip_sum",
        grid_spec=pltpu.PrefetchScalarGridSpec(
            num_scalar_prefetch=1, grid=(hl, R // tr),
            in_specs=[pl.BlockSpec((None, P, tr, C), lambda l, r, c: (l, 0, r, 0))],
            out_specs=pl.BlockSpec((None, tr, C), lambda l, r, c: (c[0] * hl + l, r, 0))),
        out_shape=jax.ShapeDtypeStruct((L, R, C), F32),
        compiler_params=_params(("parallel", "parallel")),
    )(cidx, land)


def adamw(w, g, m, v):
    lead, (R, C) = w.shape[:-2], w.shape[-2:]
    tr = _row_tile(R)
    c1 = 1.0 / (1.0 - ADAM_B1 ** ADAM_STEP)
    c2 = 1.0 / (1.0 - ADAM_B2 ** ADAM_STEP)

    def body(w_ref, g_ref, m_ref, v_ref, go_ref, d_ref, nm_ref, nv_ref):
        gv = g_ref[...]
        go_ref[...] = gv
        nm = ADAM_B1 * m_ref[...] + (1.0 - ADAM_B1) * gv
        nv = ADAM_B2 * v_ref[...] + (1.0 - ADAM_B2) * (gv * gv)
        nm_ref[...] = nm
        nv_ref[...] = nv
        d_ref[...] = -ADAM_LR * ((nm * c1) / (jnp.sqrt(nv * c2) + ADAM_EPS) + ADAM_WD * w_ref[...])

    def flat(idx):
        l = 0
        for i, n in zip(idx, lead):
            l = l * n + i
        return l

    blk = pl.BlockSpec((None,) * len(lead) + (tr, C), lambda *ix: (*ix, 0))
    gblk = pl.BlockSpec((None, tr, C), lambda *ix: (flat(ix[:-1]), ix[-1], 0))
    osh = jax.ShapeDtypeStruct(w.shape, F32)
    return pl.pallas_call(
        body, name="adamw", grid=(*lead, R // tr),
        in_specs=[blk, gblk, blk, blk], out_specs=[blk, blk, blk, blk], out_shape=[osh, osh, osh, osh],
        compiler_params=_params(("parallel",) * (len(lead) + 1)),
    )(w, g, m, v)


def kernel(x, w_qkv_a, w_o_a, rel_bias, w_qkv_b, w_o_b, ffn_w_gate, ffn_w_up, ffn_w_down, ln_g, ln_b, loss_target, m_w_qkv_a, m_w_o_a, m_rel_bias, m_w_qkv_b, m_w_o_b, m_ffn_w_gate, m_ffn_w_up, m_ffn_w_down, m_ln_g, m_ln_b, v_w_qkv_a, v_w_o_a, v_rel_bias, v_w_qkv_b, v_w_o_b, v_ffn_w_gate, v_ffn_w_up, v_ffn_w_down, v_ln_g, v_ln_b):
    B, S, D = x.shape
    M = B * S
    depth = ffn_w_gate.shape[0]
    n_ffn = 2 * depth
    H = D // HEAD_DIM
    HP = H // 2
    Fs = ffn_w_gate.shape[-1]
    alpha = (2.0 * depth) ** 0.25
    assert S % QB_A == 0 and S % SB_TILE == 0 and rel_bias.shape == (N_REL, H)

    me = 2 * lax.axis_index("x") + lax.axis_index("y")
    ids = jnp.stack([lax.axis_index("c"), me]).astype(jnp.int32)

    def tr(a):
        return jnp.swapaxes(a, -1, -2)

    gate_t, up_t = tr(ffn_w_gate), tr(ffn_w_up)

    def place_layer(i):
        wq, wo = (w_qkv_a, w_o_a) if i % 2 == 0 else (w_qkv_b, w_o_b)
        mixer = [cast_place(ids, w, lambda l: (i // 2,), 1) for w in (wq, wo)]
        return mixer + [cast_place(ids, w, lambda l: (i, l), 2) for w in (gate_t, up_t, ffn_w_down)]

    placed = [place_layer(i) for i in range(depth)]
    in_flight, _ = gather_start([[s for s, _ in lay] for lay in placed], [[z for _, z in lay] for lay in placed])
    lng_p, lnb_p = gather_weights([ln_g, ln_b])
    lng = jnp.moveaxis(lng_p, 0, 2).reshape(depth, 3, 1, D)
    lnb = jnp.moveaxis(lnb_p, 0, 2).reshape(depth, 3, 1, D)

    tab_t = jnp.pad(rel_bias.T, ((0, 0), (0, REL_PAD - N_REL)))
    vr = bias_vec(tab_t).reshape(HP, 2, VR_W)

    xf = x.reshape(M, D)
    xb = xf.astype(BF16)
    saved, weights = [], []
    for i in range(depth):
        send, recv, thru, lands = in_flight[i]
        _, lands = gather_wait(i, send, recv, thru, lands, xf)
        wq, wo, wg, wu, wd = gather_forward(i, lands)
        weights.append((wq, wo, wg, wu, wd))
        for j in range(3):
            gam, bet = lng[i, j], lnb[i, j]
            if j != 1:
                l = 0 if j == 0 else 1
                g, u, h = ffn_up(xb, wg, wu, l)
                xo, xob, xhat, rstd = mm_ln(h, wd, l, xf, gam, bet, alpha, 0.5, True)
                saved.append(("ffn", i, l, xb, g, u, h, xhat, rstd, gam))
            elif i % 2 == 0:
                qkv = qkv_proj(xb, wq, 0)
                o = attn_a_fwd(qkv, vr, B, S)
                xo, xob, xhat, rstd = mm_ln(o, wo, 0, xf, gam, bet, alpha, 1.0, False)
                saved.append(("a", i, 0, xb, qkv, o, None, xhat, rstd, gam))
            else:
                qkv = qkv_proj(xb, wq, 0)
                o, ntot = attn_b_fwd(qkv, B, S)
                xo, xob, xhat, rstd = mm_ln(o, wo, 0, xf, gam, bet, alpha, 1.0, False)
                saved.append(("b", i, 0, xb, qkv, o, ntot, xhat, rstd, gam))
            xf, xb = xo, xob

    dy, loss_part = loss_head(xf, loss_target.reshape(M, D))
    loss = lax.psum(loss_part[0, 0], ("x", "y", "c"))

    Cq, Ro = w_qkv_a.shape[-1], w_o_a.shape[1]
    dgam = lax.empty((3 * depth, 1, D), F32)
    dbet = lax.empty((3 * depth, 1, D), F32)
    dvrs, reducing = [], {}
    started = None
    for i in reversed(range(depth)):
        wq, wo, wg, wu, wd = weights[i]
        gq = lax.empty((1, N_CHIPS, D, Cq), BF16)
        go = lax.empty((1, N_CHIPS, Ro, D), BF16)
        ggate = lax.empty((2, N_CHIPS, Fs, D), BF16)
        gup = lax.empty((2, N_CHIPS, Fs, D), BF16)
        gdown = lax.empty((2, N_CHIPS, Fs, D), BF16)
        for sub in reversed(range(3 * i, 3 * i + 3)):
            kind, _, l, xb_in, t1, t2, t3, xhat, rstd, gam = saved[sub]
            scale = 0.5 if kind == "ffn" else 1.0
            if started is not None:
                gam = gam + started[0, 0]
                started = None
            dpre, dyb, dgam, dbet = ln_bwd(dy, xhat, rstd, gam, scale, dgam, dbet, sub)
            if kind == "ffn":
                g, u, h = t1, t2, t3
                dg, du, gdown, ggate, gup = ffn_bwd(gdown, ggate, gup, l, xb_in, dyb, wd, g, u, h)
                dy = ffn_dx(dg, du, wg, wu, l, dpre, alpha)
            else:
                qkv, o = t1, t2
                do = o_proj_bwd(dyb, wo, l)
                go = dw_o(go, l, o, dyb)
                if kind == "a":
                    dqkv, dvr = attn_a_bwd(qkv, vr, do, B, S)
                    dvrs.append(dvr.reshape(B, H, VR_W))
                else:
                    dqkv = attn_b_bwd(qkv, do, t3, B, S)
                gq = dw_qkv(gq, l, xb_in, dqkv)
                dy = qkv_proj_bwd(dqkv, wq, l, dpre, alpha)
        fams = [gq, go, ggate, gup, gdown]
        from_sib = pair_exchange_rows(i, fams)
        parts = [pair_add_rows(ids, g_, r_) for g_, r_ in zip(fams, from_sib)]
        reducing[i] = reduce_start(i, parts)
        started = reducing[i][4]
    grad_x = dy.reshape(B, S, D)

    la, lb = w_qkv_a.shape[0], w_qkv_b.shape[0]
    full = {"qa": lax.empty((la, D, Cq), F32), "oa": lax.empty((la, Ro, D), F32),
            "qb": lax.empty((lb, D, Cq), F32), "ob": lax.empty((lb, Ro, D), F32),
            "gate": lax.empty((n_ffn, Fs, D), F32), "up": lax.empty((n_ffn, Fs, D), F32),
            "down": lax.empty((n_ffn, Fs, D), F32)}
    for i in reversed(range(depth)):
        send, recv, parts, lands, _ = reducing[i]
        parts, lands = reduce_wait(i, send, recv, parts, lands, dy)
        mix = "a" if i % 2 == 0 else "b"
        targets = [("q" + mix, i // 2), ("o" + mix, i // 2), ("gate", 2 * i), ("up", 2 * i), ("down", 2 * i)]
        for (name, l0), part, land in zip(targets, parts, lands):
            full[name] = chip_sum_rows(ids, land, part, full[name], l0)
    g_qa, g_oa, g_qb, g_ob, g_gate, g_up, g_down = half_swap_rows(
        [full[k] for k in ("qa", "oa", "qb", "ob", "gate", "up", "down")])

    d_tab_t = bias_vec_bwd(jnp.concatenate(dvrs, axis=0))
    small = jnp.concatenate([d_tab_t.reshape(-1), dgam.reshape(-1), dbet.reshape(-1)])
    n_small = small.shape[0]
    rows = -(-n_small // (8 * 128)) * 8
    tot = all_sum_small(jnp.pad(small, (0, rows * 128 - n_small)).reshape(rows, 128)).reshape(-1)
    n_tab, n_ln = H * REL_PAD, 3 * depth * D
    g_rel = tot[:n_tab].reshape(H, REL_PAD)[:, :N_REL].T
    ln_cols = D // N_CHIPS

    def ln_shard(flat):
        return lax.dynamic_slice_in_dim(flat.reshape(depth, 3, D), me * ln_cols, ln_cols, axis=2)

    g_lng = ln_shard(tot[n_tab:n_tab + n_ln])
    g_lnb = ln_shard(tot[n_tab + n_ln:n_tab + 2 * n_ln])

    def upd(w, g, m, v):
        if w.ndim == 2:
            return tuple(a[0] for a in adamw(w[None], g, m[None], v[None]))
        return adamw(w, g, m, v)

    res = [
        upd(w_qkv_a, g_qa, m_w_qkv_a, v_w_qkv_a),
        upd(w_o_a, g_oa, m_w_o_a, v_w_o_a),
        upd(rel_bias, g_rel.reshape(1, N_REL, H), m_rel_bias, v_rel_bias),
        upd(w_qkv_b, g_qb, m_w_qkv_b, v_w_qkv_b),
        upd(w_o_b, g_ob, m_w_o_b, v_w_o_b),
        tuple(tr(a) for a in upd(gate_t, g_gate, tr(m_ffn_w_gate), tr(v_ffn_w_gate))),
        tuple(tr(a) for a in upd(up_t, g_up, tr(m_ffn_w_up), tr(v_ffn_w_up))),
        upd(ffn_w_down, g_down, m_ffn_w_down, v_ffn_w_down),
        upd(ln_g, g_lng, m_ln_g, v_ln_g),
        upd(ln_b, g_lnb, m_ln_b, v_ln_b),
    ]
    grads = [r[0] for r in res]
    deltas = [r[1] for r in res]
    new_m = [r[2] for r in res]
    new_v = [r[3] for r in res]
    return (loss, grad_x, *grads, *deltas, *new_m, *new_v)
```

```python
import functools
import math

import jax
import jax.numpy as jnp
from jax import lax
from jax.experimental import pallas as pl
from jax.experimental.pallas import tpu as pltpu

F32 = jnp.float32
BF16 = jnp.bfloat16
MESH = pl.DeviceIdType.MESH

N_CHIPS = 4
HEAD_DIM = 64
CHUNK = 64
LEFT_CHUNKS = 8
LOOKBACK = LEFT_CHUNKS * CHUNK
REL_CLIP = 128
N_REL = 2 * REL_CLIP + 1
REL_PAD = 384
SB_TILE = 256
QB_A = 256
KW_A = QB_A + LOOKBACK
VR_W = 1024
VR_C0 = KW_A - 1
assert math.frexp(HEAD_DIM ** -0.5)[0] == 0.5
LN_EPS = 1e-5
ADAM_LR, ADAM_B1, ADAM_B2, ADAM_EPS, ADAM_WD, ADAM_STEP = 0.001, 0.9, 0.999, 1e-08, 0.01, 10
NEG = -1e30
VMEM_LIMIT = 56 * 1024 * 1024

NT_DIMS = (((1,), (1,)), ((), ()))
TN_DIMS = (((0,), (0,)), ((), ()))
ANY = pl.BlockSpec(memory_space=pl.ANY)


def _params(sem=None):
    return pltpu.CompilerParams(dimension_semantics=sem, vmem_limit_bytes=VMEM_LIMIT)


def _tile(n, pref):
    t = min(n, pref)
    assert n % t == 0, (n, pref)
    return t


def _sigmoid(z):
    return 1.0 / (1.0 + jnp.exp(-z))


def _mm_call(name, operands, in_specs, out_shape, out_spec, grid, dims_list, acc_shape,
             add_coef=None, aliases=None):
    n_pairs = len(dims_list)
    nk = grid[-1]
    has_add = add_coef is not None
    n_alias = len(aliases) if aliases else 0

    def body(*refs):
        pair_refs = refs[:2 * n_pairs]
        pos = 2 * n_pairs
        add_ref = refs[pos] if has_add else None
        pos += (1 if has_add else 0) + n_alias
        o_ref = refs[pos]
        acc_ref = refs[pos + 1] if nk > 1 else None

        def product():
            part = None
            for i, dims in enumerate(dims_list):
                d = lax.dot_general(pair_refs[2 * i][...], pair_refs[2 * i + 1][...], dims,
                                    preferred_element_type=F32)
                part = d if part is None else part + d
            return part

        def finish(r):
            if has_add:
                r = r + add_coef * add_ref[...]
            o_ref[...] = r.astype(o_ref.dtype)

        if nk == 1:
            finish(product())
        else:
            k = pl.program_id(len(grid) - 1)

            @pl.when(k == 0)
            def _():
                acc_ref[...] = jnp.zeros_like(acc_ref)

            acc_ref[...] += product()

            @pl.when(k == nk - 1)
            def _():
                finish(acc_ref[...])

    sem = ("parallel",) * (len(grid) - 1) + ("arbitrary",)
    return pl.pallas_call(
        body, name=name, grid=grid, in_specs=in_specs, out_specs=out_spec, out_shape=out_shape,
        scratch_shapes=[pltpu.VMEM(acc_shape, F32)] if nk > 1 else [],
        input_output_aliases=aliases or {},
        compiler_params=_params(sem),
    )(*operands)


def qkv_proj(xb, w, l):
    M, D = xb.shape
    C = w.shape[-1]
    tm = _tile(M, 512)
    return _mm_call(
        "qkv_proj", (xb, w),
        [pl.BlockSpec((tm, D), lambda p, i, k: (i, 0)),
         pl.BlockSpec((None, None, D, C), lambda p, i, k: (p, l, 0, 0))],
        jax.ShapeDtypeStruct((M, N_CHIPS * C), BF16),
        pl.BlockSpec((tm, C), lambda p, i, k: (i, p)),
        (N_CHIPS, M // tm, 1), [(((1,), (0,)), ((), ()))], None)


def o_proj_bwd(dyb, w, l):
    M, D = dyb.shape
    R = w.shape[2]
    tm = _tile(M, 512)
    return _mm_call(
        "o_proj_bwd", (dyb, w),
        [pl.BlockSpec((tm, D), lambda p, i, k: (i, 0)),
         pl.BlockSpec((None, None, R, D), lambda p, i, k: (p, l, 0, 0))],
        jax.ShapeDtypeStruct((M, N_CHIPS * R), BF16),
        pl.BlockSpec((tm, R), lambda p, i, k: (i, p)),
        (N_CHIPS, M // tm, 1), [NT_DIMS], None)


def qkv_proj_bwd(dqkv, w, l, dpre, alpha):
    _, M, D = dqkv.shape
    C = w.shape[3]
    tm = _tile(M, 512)
    T = math.gcd(D, C)

    def body(a_ref, w_ref, add_ref, o_ref):
        acc = alpha * add_ref[...]
        for t in range(3 * D // T):
            pa, ca = divmod(t * T, D)
            pw, cw = divmod(t * T, C)
            acc = acc + lax.dot_general(a_ref[pa, :, ca:ca + T], w_ref[pw, :, cw:cw + T], NT_DIMS,
                                        preferred_element_type=F32)
        o_ref[...] = acc

    row = pl.BlockSpec((tm, D), lambda i: (i, 0))
    return pl.pallas_call(
        body, name="qkv_proj_bwd", grid=(M // tm,),
        in_specs=[pl.BlockSpec((3, tm, D), lambda i: (0, i, 0)),
                  pl.BlockSpec((N_CHIPS, None, D, C), lambda i: (0, l, 0, 0)), row],
        out_specs=row, out_shape=jax.ShapeDtypeStruct((M, D), F32),
        compiler_params=_params(("parallel",)),
    )(dqkv, w, dpre)


def ffn_dx(dg, du, wgt, wut, l, dpre, alpha):
    _, M, Fs = dg.shape
    D = wgt.shape[3]
    tm = _tile(M, 256)

    def body(dg_ref, wg_ref, du_ref, wu_ref, add_ref, o_ref):
        acc = alpha * add_ref[...]
        for p in range(N_CHIPS):
            acc = acc + jnp.dot(dg_ref[p], wg_ref[p], preferred_element_type=F32)
            acc = acc + jnp.dot(du_ref[p], wu_ref[p], preferred_element_type=F32)
        o_ref[...] = acc

    act = pl.BlockSpec((N_CHIPS, tm, Fs), lambda i: (0, i, 0))
    wsp = pl.BlockSpec((N_CHIPS, None, Fs, D), lambda i: (0, l, 0, 0))
    row = pl.BlockSpec((tm, D), lambda i: (i, 0))
    return pl.pallas_call(
        body, name="ffn_dx", grid=(M // tm,),
        in_specs=[act, wsp, act, wsp, row],
        out_specs=row, out_shape=jax.ShapeDtypeStruct((M, D), F32),
        compiler_params=_params(("parallel",)),
    )(dg, wgt, du, wut, dpre)


def _dw_call(name, buf, l, a, b, a_spec, b_spec, M, tk):
    _, _, R, C = buf.shape
    return _mm_call(
        name, (a, b, buf),
        [a_spec, b_spec, ANY],
        jax.ShapeDtypeStruct(buf.shape, buf.dtype),
        pl.BlockSpec((None, None, R, C), lambda p, k: (l, p, 0, 0)),
        (N_CHIPS, M // tk), [TN_DIMS], (R, C), aliases={2: 0})


def ffn_bwd(gdown, ggate, gup, l, xb, dyb, wd, g, u, h):
    M, D = dyb.shape
    Fs = wd.shape[2]
    tm = _tile(M, 512)
    n = M // tm

    def body(x_ref, dy_ref, wd_ref, g_ref, u_ref, h_ref, _gd, _gg, _gu,
             dg_ref, du_ref, gd_ref, gg_ref, gu_ref, acc_d, acc_g, acc_u):
        i = pl.program_id(1)

        @pl.when(i == 0)
        def _():
            acc_d[...] = jnp.zeros_like(acc_d)
            acc_g[...] = jnp.zeros_like(acc_g)
            acc_u[...] = jnp.zeros_like(acc_u)

        dy = dy_ref[...]
        dh = lax.dot_general(dy, wd_ref[...], NT_DIMS, preferred_element_type=F32)
        gf = g_ref[...].astype(F32)
        sig = _sigmoid(gf)
        silu = gf * sig
        dg = (dh * u_ref[...].astype(F32) * (sig * (1.0 + gf - silu))).astype(BF16)
        du = (dh * silu).astype(BF16)
        dg_ref[...] = dg
        du_ref[...] = du
        x = x_ref[...]
        acc_d[...] += lax.dot_general(h_ref[...], dy, TN_DIMS, preferred_element_type=F32)
        acc_g[...] += lax.dot_general(dg, x, TN_DIMS, preferred_element_type=F32)
        acc_u[...] += lax.dot_general(du, x, TN_DIMS, preferred_element_type=F32)

        @pl.when(i == n - 1)
        def _():
            gd_ref[...] = acc_d[...].astype(gd_ref.dtype)
            gg_ref[...] = acc_g[...].astype(gg_ref.dtype)
            gu_ref[...] = acc_u[...].astype(gu_ref.dtype)

    row = pl.BlockSpec((tm, D), lambda p, i: (i, 0))
    act = pl.BlockSpec((None, tm, Fs), lambda p, i: (p, i, 0))
    ash = jax.ShapeDtypeStruct((N_CHIPS, M, Fs), BF16)
    w_blk = pl.BlockSpec((None, None, Fs, D), lambda p, i: (l, p, 0, 0))
    return pl.pallas_call(
        body, name="ffn_bwd", grid=(N_CHIPS, n),
        in_specs=[row, row, pl.BlockSpec((None, None, Fs, D), lambda p, i: (p, l, 0, 0)),
                  act, act, act, ANY, ANY, ANY],
        out_specs=[act, act, w_blk, w_blk, w_blk],
        out_shape=[ash, ash] + [jax.ShapeDtypeStruct(b.shape, b.dtype) for b in (gdown, ggate, gup)],
        scratch_shapes=[pltpu.VMEM((Fs, D), F32), pltpu.VMEM((Fs, D), F32), pltpu.VMEM((Fs, D), F32)],
        input_output_aliases={6: 2, 7: 3, 8: 4},
        compiler_params=_params(("parallel", "arbitrary")),
    )(xb, dyb, wd, g, u, h, gdown, ggate, gup)


def dw_qkv(buf, l, xb, dqkv):
    M, D = xb.shape
    C = buf.shape[-1]
    tk = _tile(M, 512)
    n = M // tk
    T = math.gcd(D, C)
    nt = 3 * D // T

    def body(x_ref, b_ref, _buf, o_ref, acc_ref):
        k = pl.program_id(0)

        @pl.when(k == 0)
        def _():
            acc_ref[...] = jnp.zeros_like(acc_ref)

        xt = x_ref[...].astype(F32).T.astype(BF16)
        for t in range(nt):
            pa, ca = divmod(t * T, D)
            acc_ref[t] += jnp.dot(xt, b_ref[pa, :, ca:ca + T], preferred_element_type=F32)

        @pl.when(k == n - 1)
        def _():
            for t in range(nt):
                pw, cw = divmod(t * T, C)
                o_ref[pw, :, cw:cw + T] = acc_ref[t].astype(o_ref.dtype)

    return pl.pallas_call(
        body, name="dw_qkv", grid=(n,),
        in_specs=[pl.BlockSpec((tk, D), lambda k: (k, 0)),
                  pl.BlockSpec((3, tk, D), lambda k: (0, k, 0)), ANY],
        out_specs=pl.BlockSpec((None, N_CHIPS, D, C), lambda k: (l, 0, 0, 0)),
        out_shape=jax.ShapeDtypeStruct(buf.shape, buf.dtype),
        scratch_shapes=[pltpu.VMEM((nt, D, T), F32)],
        input_output_aliases={2: 0},
        compiler_params=_params(("arbitrary",)),
    )(xb, dqkv, buf)


def dw_o(buf, l, o, dyb):
    M, D = dyb.shape
    R = buf.shape[2]
    tk = _tile(M, 1024)
    return _dw_call("dw_o", buf, l, o, dyb,
                    pl.BlockSpec((tk, R), lambda p, k: (k, p)),
                    pl.BlockSpec((tk, D), lambda p, k: (k, 0)), M, tk)


def ffn_up(xb, wgt, wut, l):
    M, D = xb.shape
    Fs = wgt.shape[2]
    tm = _tile(M, 512)

    def body(x_ref, wg_ref, wu_ref, g_ref, u_ref, h_ref):
        x = x_ref[...]
        g = lax.dot_general(x, wg_ref[...], NT_DIMS, preferred_element_type=F32)
        u = lax.dot_general(x, wu_ref[...], NT_DIMS, preferred_element_type=F32)
        g_ref[...] = g.astype(BF16)
        u_ref[...] = u.astype(BF16)
        h_ref[...] = (g * _sigmoid(g) * u).astype(BF16)

    wsp = pl.BlockSpec((None, None, Fs, D), lambda p, i: (p, l, 0, 0))
    osp = pl.BlockSpec((None, tm, Fs), lambda p, i: (p, i, 0))
    osh = jax.ShapeDtypeStruct((N_CHIPS, M, Fs), BF16)
    return pl.pallas_call(
        body, name="ffn_up", grid=(N_CHIPS, M // tm),
        in_specs=[pl.BlockSpec((tm, D), lambda p, i: (i, 0)), wsp, wsp],
        out_specs=[osp, osp, osp], out_shape=[osh, osh, osh],
        compiler_params=_params(("parallel", "parallel")),
    )(xb, wgt, wut)


def mm_ln(a, w, l, x, gam, bet, alpha, scale, a_piece_major):
    M, D = x.shape
    R = w.shape[2]
    tm = _tile(M, 256)
    if a_piece_major:
        a_spec = pl.BlockSpec((N_CHIPS, tm, R), lambda i: (0, i, 0))
    else:
        a_spec = pl.BlockSpec((tm, N_CHIPS * R), lambda i: (i, 0))

    def body(a_ref, w_ref, x_ref, g_ref, b_ref, xo_ref, xb_ref, xh_ref, rs_ref):
        y = None
        for p in range(N_CHIPS):
            a = a_ref[p] if a_piece_major else a_ref[:, p * R:(p + 1) * R]
            d = jnp.dot(a, w_ref[p], preferred_element_type=F32)
            y = d if y is None else y + d
        pre = alpha * x_ref[...] + scale * y
        mu = jnp.mean(pre, axis=-1, keepdims=True)
        cen = pre - mu
        var = jnp.mean(cen * cen, axis=-1, keepdims=True)
        rstd = lax.rsqrt(var + LN_EPS)
        xhat = cen * rstd
        out = xhat * g_ref[...] + b_ref[...]
        xo_ref[...] = out
        xb_ref[...] = out.astype(BF16)
        xh_ref[...] = xhat
        rs_ref[...] = rstd

    row = pl.BlockSpec((tm, D), lambda i: (i, 0))
    vec = pl.BlockSpec((1, D), lambda i: (0, 0))
    return pl.pallas_call(
        body, name="mm_ln", grid=(M // tm,),
        in_specs=[a_spec, pl.BlockSpec((N_CHIPS, None, R, D), lambda i: (0, l, 0, 0)), row, vec, vec],
        out_specs=[row, row, row, pl.BlockSpec((tm, 1), lambda i: (i, 0))],
        out_shape=[jax.ShapeDtypeStruct((M, D), F32), jax.ShapeDtypeStruct((M, D), BF16),
                   jax.ShapeDtypeStruct((M, D), F32), jax.ShapeDtypeStruct((M, 1), F32)],
        compiler_params=_params(("parallel",)),
    )(a, w, x, gam, bet)


def ln_bwd(dy, xhat, rstd, gam, scale, dgam, dbet, row):
    M, D = dy.shape
    tm = _tile(M, 512)

    def body(dy_ref, xh_ref, rs_ref, g_ref, _dg, _db, dp_ref, db16_ref, dg_ref, dbt_ref):
        i = pl.program_id(0)
        dy_v = dy_ref[...]
        xh = xh_ref[...]
        dxh = dy_v * g_ref[...]
        m1 = jnp.mean(dxh, axis=-1, keepdims=True)
        m2 = jnp.mean(dxh * xh, axis=-1, keepdims=True)
        dpre = rs_ref[...] * (dxh - m1 - xh * m2)
        dp_ref[...] = dpre
        db16_ref[...] = (scale * dpre).astype(BF16)
        dgp = jnp.sum(dy_v * xh, axis=0, keepdims=True)
        dbp = jnp.sum(dy_v, axis=0, keepdims=True)

        @pl.when(i == 0)
        def _():
            dg_ref[...] = dgp
            dbt_ref[...] = dbp

        @pl.when(i > 0)
        def _():
            dg_ref[...] += dgp
            dbt_ref[...] += dbp

    tok = pl.BlockSpec((tm, D), lambda i: (i, 0))
    vec = pl.BlockSpec((1, D), lambda i: (0, 0))
    acc = pl.BlockSpec((None, 1, D), lambda i: (row, 0, 0))
    return pl.pallas_call(
        body, name="ln_bwd", grid=(M // tm,),
        in_specs=[tok, tok, pl.BlockSpec((tm, 1), lambda i: (i, 0)), vec, ANY, ANY],
        out_specs=[tok, tok, acc, acc],
        out_shape=[jax.ShapeDtypeStruct((M, D), F32), jax.ShapeDtypeStruct((M, D), BF16),
                   jax.ShapeDtypeStruct(dgam.shape, F32), jax.ShapeDtypeStruct(dbet.shape, F32)],
        input_output_aliases={4: 2, 5: 3},
        compiler_params=_params(("arbitrary",)),
    )(dy, xhat, rstd, gam, dgam, dbet)


def loss_head(y, tgt):
    M, D = y.shape
    tm = _tile(M, 512)
    n = M // tm

    def body(y_ref, t_ref, dy_ref, l_ref, acc_ref):
        i = pl.program_id(0)
        e = y_ref[...] - t_ref[...]
        dy_ref[...] = e * (1.0 / D)
        part = jnp.sum(e * e, axis=0, keepdims=True)

        @pl.when(i == 0)
        def _():
            acc_ref[...] = part

        @pl.when(i > 0)
        def _():
            acc_ref[...] += part

        @pl.when(i == n - 1)
        def _():
            l_ref[...] = (0.5 / D) * jnp.sum(acc_ref[...], axis=1, keepdims=True)

    row = pl.BlockSpec((tm, D), lambda i: (i, 0))
    return pl.pallas_call(
        body, name="loss_head", grid=(n,),
        in_specs=[row, row],
        out_specs=[row, pl.BlockSpec((1, 1), lambda i: (0, 0))],
        out_shape=[jax.ShapeDtypeStruct((M, D), F32), jax.ShapeDtypeStruct((1, 1), F32)],
        scratch_shapes=[pltpu.VMEM((1, D), F32)],
        compiler_params=_params(("arbitrary",)),
    )(y, tgt)


def _rel_onehot_t():
    r = lax.broadcasted_iota(jnp.int32, (REL_PAD, VR_W), 0)
    n = lax.broadcasted_iota(jnp.int32, (REL_PAD, VR_W), 1)
    idx = jnp.clip(VR_C0 - n, -REL_CLIP, REL_CLIP) + REL_CLIP
    return (r == idx).astype(F32)


def bias_vec(tab_t):
    H = tab_t.shape[0]

    def body(t_ref, o_ref):
        o_ref[...] = jnp.dot(t_ref[...], _rel_onehot_t(), precision=lax.Precision.HIGHEST,
                             preferred_element_type=F32)

    return pl.pallas_call(
        body, name="bias_vec", out_shape=jax.ShapeDtypeStruct((H, VR_W), F32),
        compiler_params=_params(),
    )(tab_t)


def bias_vec_bwd(dvr):
    n, H, _ = dvr.shape

    def body(d_ref, o_ref):
        tot = d_ref[0]
        for i in range(1, n):
            tot = tot + d_ref[i]
        o_ref[...] = lax.dot_general(tot, _rel_onehot_t(), NT_DIMS, precision=lax.Precision.HIGHEST,
                                     preferred_element_type=F32)

    return pl.pallas_call(
        body, name="bias_vec_bwd", out_shape=jax.ShapeDtypeStruct((H, REL_PAD), F32),
        compiler_params=_params(),
    )(dvr)


def _a_bias_mask(vr_row):
    xb = jnp.broadcast_to(vr_row, (QB_A, VR_W))
    tile = pltpu.roll(xb, VR_W - (QB_A - 1), 1, stride=1, stride_axis=0)[:, :KW_A]
    qc = lax.broadcasted_iota(jnp.int32, (QB_A, KW_A), 0) // CHUNK
    kc = lax.broadcasted_iota(jnp.int32, (QB_A, KW_A), 1) // CHUNK
    valid = (kc >= qc) & (kc <= qc + LEFT_CHUNKS)
    return jnp.where(valid, tile, NEG)


def _a_diag_sums(db_acc, h):
    acc8 = None
    for a in range(QB_A // 8):
        grp = db_acc[h, 8 * a:8 * a + 8, :]
        shift = QB_A - 8 - 8 * a
        if shift:
            grp = pltpu.roll(grp, shift, 1)
        acc8 = grp if acc8 is None else acc8 + grp
    sub = lax.broadcasted_iota(jnp.int32, (8, VR_W), 0)
    tot = jnp.zeros((8, VR_W), F32)
    for b in range(8):
        moved = pltpu.roll(acc8, 7 - b, 1) if b < 7 else acc8
        tot = tot + jnp.where(sub == b, moved, 0.0)
    return jnp.sum(tot, axis=0, keepdims=True)


def _a_blocks(S):
    out = []
    for qi in range(S // QB_A):
        q0 = qi * QB_A
        ks = max(0, q0 - LOOKBACK)
        out.append((q0, ks, q0 + QB_A, ks - (q0 - LOOKBACK)))
    return out


def _head_specs(S, HP):
    q = pl.BlockSpec((S, 2 * HEAD_DIM), lambda b, hp: (b, hp))
    k = pl.BlockSpec((S, 2 * HEAD_DIM), lambda b, hp: (b, HP + hp))
    v = pl.BlockSpec((S, 2 * HEAD_DIM), lambda b, hp: (b, 2 * HP + hp))
    return q, k, v


def attn_a_fwd(qkv, vr, B, S):
    D = qkv.shape[1] // 3
    HP = D // (2 * HEAD_DIM)
    scale = HEAD_DIM ** -0.5
    blocks = _a_blocks(S)

    def body(q_ref, k_ref, v_ref, vr_ref, o_ref):
        heads = [slice(h * HEAD_DIM, (h + 1) * HEAD_DIM) for h in range(2)]
        bms = [_a_bias_mask(vr_ref[h:h + 1, :]) for h in range(2)]
        for (q0, ks, ke, joff) in blocks:
            rows = slice(q0, q0 + QB_A)
            ss = [lax.dot_general(q_ref[rows, hs] * scale, k_ref[ks:ke, hs], NT_DIMS,
                                  preferred_element_type=F32) + bm[:, joff:]
                  for hs, bm in zip(heads, bms)]
            ps = [jnp.exp(s - jnp.max(s, axis=-1, keepdims=True)) for s in ss]
            for hs, p in zip(heads, ps):
                den = jnp.sum(p, axis=-1, keepdims=True)
                o = jnp.dot(p.astype(BF16), v_ref[ks:ke, hs], preferred_element_type=F32) / den
                o_ref[rows, hs] = o.astype(BF16)

    qs, ks_, vs = _head_specs(S, HP)
    return pl.pallas_call(
        body, name="attn_a_fwd", grid=(B, HP),
        in_specs=[qs, ks_, vs, pl.BlockSpec((None, 2, VR_W), lambda b, hp: (hp, 0, 0))],
        out_specs=pl.BlockSpec((S, 2 * HEAD_DIM), lambda b, hp: (b, hp)),
        out_shape=jax.ShapeDtypeStruct((B * S, D), BF16),
        compiler_params=_params(("parallel", "parallel")),
    )(qkv, qkv, qkv, vr)


def attn_a_bwd(qkv, vr, do, B, S):
    D = qkv.shape[1] // 3
    HP = D // (2 * HEAD_DIM)
    scale = HEAD_DIM ** -0.5
    blocks = _a_blocks(S)

    def body(q_ref, k_ref, v_ref, vr_ref, do_ref, dqkv_ref, dvr_ref, dkt_acc, dvt_acc, db_acc):
        dkt_acc[...] = jnp.zeros_like(dkt_acc)
        dvt_acc[...] = jnp.zeros_like(dvt_acc)
        db_acc[...] = jnp.zeros_like(db_acc)
        heads = [slice(h * HEAD_DIM, (h + 1) * HEAD_DIM) for h in range(2)]
        bms = [_a_bias_mask(vr_ref[h:h + 1, :]) for h in range(2)]
        for (q0, ks, ke, joff) in blocks:
            rows = slice(q0, q0 + QB_A)
            qt_pair = (q_ref[rows, :] * scale).astype(F32).T.astype(BF16)
            dot_pair = do_ref[rows, :].astype(F32).T.astype(BF16)
            ss = [lax.dot_general(q_ref[rows, hs] * scale, k_ref[ks:ke, hs], NT_DIMS,
                                  preferred_element_type=F32) + bm[:, joff:]
                  for hs, bm in zip(heads, bms)]
            dps = [lax.dot_general(do_ref[rows, hs], v_ref[ks:ke, hs], NT_DIMS, preferred_element_type=F32)
                   for hs in heads]
            ps, dsbs = [], []
            for h, (s, dp) in enumerate(zip(ss, dps)):
                e = jnp.exp(s - jnp.max(s, axis=-1, keepdims=True))
                p = e / jnp.sum(e, axis=-1, keepdims=True)
                ds = p * (dp - jnp.sum(p * dp, axis=-1, keepdims=True))
                db_acc[h, :, joff:KW_A] += ds
                ps.append(p.astype(BF16))
                dsbs.append(ds.astype(BF16))
            for hs, p, dsb in zip(heads, ps, dsbs):
                dq = jnp.dot(dsb, k_ref[ks:ke, hs], preferred_element_type=F32) * scale
                dqkv_ref[0, rows, hs] = dq.astype(BF16)
                dkt_acc[hs, ks:ke] += jnp.dot(qt_pair[hs, :], dsb, preferred_element_type=F32)
                dvt_acc[hs, ks:ke] += jnp.dot(dot_pair[hs, :], p, preferred_element_type=F32)
        for h in range(2):
            dvr_ref[h:h + 1, :] = _a_diag_sums(db_acc, h)
        dqkv_ref[1] = dkt_acc[...].T.astype(BF16)
        dqkv_ref[2] = dvt_acc[...].T.astype(BF16)

    qs, ks_, vs = _head_specs(S, HP)
    hd = pl.BlockSpec((S, 2 * HEAD_DIM), lambda b, hp: (b, hp))
    return pl.pallas_call(
        body, name="attn_a_bwd", grid=(B, HP),
        in_specs=[qs, ks_, vs, pl.BlockSpec((None, 2, VR_W), lambda b, hp: (hp, 0, 0)), hd],
        out_specs=[pl.BlockSpec((3, S, 2 * HEAD_DIM), lambda b, hp: (0, b, hp)),
                   pl.BlockSpec((None, None, 2, VR_W), lambda b, hp: (b, hp, 0, 0))],
        out_shape=[jax.ShapeDtypeStruct((3, B * S, D), BF16), jax.ShapeDtypeStruct((B, HP, 2, VR_W), F32)],
        scratch_shapes=[pltpu.VMEM((2 * HEAD_DIM, S), F32), pltpu.VMEM((2 * HEAD_DIM, S), F32),
                        pltpu.VMEM((2, QB_A, VR_W), F32)],
        compiler_params=_params(("parallel", "parallel")),
    )(qkv, qkv, qkv, vr, do)


def _tri(cmp):
    j = lax.broadcasted_iota(jnp.int32, (SB_TILE, SB_TILE), 0)
    s = lax.broadcasted_iota(jnp.int32, (SB_TILE, SB_TILE), 1)
    return cmp(j, s).astype(BF16)


def _cumsum_mm(x, tri):
    hi = x.astype(BF16)
    mid = (x - hi.astype(F32)).astype(BF16)
    return jnp.dot(hi, tri, preferred_element_type=F32) + jnp.dot(mid, tri, preferred_element_type=F32)


def _sb_logs(q, k, diagonal):
    z = lax.dot_general(q, k, NT_DIMS, preferred_element_type=F32)
    log_b = jnp.minimum(z, 0.0) - jnp.log(1.0 + jnp.exp(-jnp.abs(z)))
    log_1mb = log_b - z
    if not diagonal:
        return log_b, log_1mb, None
    row = lax.broadcasted_iota(jnp.int32, (SB_TILE, SB_TILE), 0)
    col = lax.broadcasted_iota(jnp.int32, (SB_TILE, SB_TILE), 1)
    causal = col < row
    return log_b, jnp.where(causal, log_1mb, 0.0), causal


def attn_b_fwd(qkv, B, S):
    D = qkv.shape[1] // 3
    HP = D // (2 * HEAD_DIM)
    scale = HEAD_DIM ** -0.5
    nb = S // SB_TILE

    def body(q_ref, k_ref, v_ref, o_ref, nt_ref):
        tri = _tri(lambda j, s: j > s)
        heads = [slice(h * HEAD_DIM, (h + 1) * HEAD_DIM) for h in range(2)]

        def q_block(qb, n_pairs, parity):
            q0 = pl.multiple_of(qb * SB_TILE, SB_TILE)
            rows = pl.ds(q0, SB_TILE)
            qs = [q_ref[rows, hs] * scale for hs in heads]

            def step(blocks, state):
                chains = [(h, kb, diagonal, pl.ds(pl.multiple_of(kb * SB_TILE, SB_TILE), SB_TILE))
                          for h in range(2) for kb, diagonal in blocks]
                logs = [_sb_logs(qs[h], k_ref[keys, heads[h]], diagonal)
                        for h, _, diagonal, keys in chains]
                sums = [_cumsum_mm(log_1mb, tri) for _, log_1mb, _ in logs]
                rights = [state[0][0], state[1][0]]
                accs = [state[0][1], state[1][1]]
                for (h, _, diagonal, keys), (log_b, log_1mb, causal), csum in zip(chains, logs, sums):
                    a = jnp.exp(log_b + csum + rights[h])
                    if diagonal:
                        a = jnp.where(causal, a, 0.0)
                    accs[h] = accs[h] + jnp.dot(a.astype(BF16), v_ref[keys, heads[h]],
                                                preferred_element_type=F32)
                    rights[h] = rights[h] + jnp.sum(log_1mb, axis=-1, keepdims=True)
                return ((rights[0], accs[0]), (rights[1], accs[1]))

            zero = (jnp.zeros((SB_TILE, 1), F32), jnp.zeros((SB_TILE, HEAD_DIM), F32))
            first = [(qb, True)] + ([(qb - 1, False)] if parity else [])
            top = qb - len(first)
            state = lax.fori_loop(
                0, n_pairs, lambda t, st: step([(top - 2 * t, False), (top - 2 * t - 1, False)], st),
                step(first, (zero, zero)))
            for hs, (right, acc) in zip(heads, state):
                o_ref[rows, hs] = acc.astype(BF16)
                nt_ref[rows, hs] = jnp.broadcast_to(right, (SB_TILE, HEAD_DIM))

        def q_pair_loop(j, carry):
            q_block(2 * j, j, 0)
            q_block(2 * j + 1, j, 1)
            return carry

        lax.fori_loop(0, nb // 2, q_pair_loop, 0)

    qs, ks_, vs = _head_specs(S, HP)
    hd = pl.BlockSpec((S, 2 * HEAD_DIM), lambda b, hp: (b, hp))
    return pl.pallas_call(
        body, name="attn_b_fwd", grid=(B, HP),
        in_specs=[qs, ks_, vs], out_specs=[hd, hd],
        out_shape=[jax.ShapeDtypeStruct((B * S, D), BF16), jax.ShapeDtypeStruct((B * S, D), F32)],
        compiler_params=_params(("parallel", "parallel")),
    )(qkv, qkv, qkv)


def attn_b_bwd(qkv, do, ntot, B, S):
    D = qkv.shape[1] // 3
    HP = D // (2 * HEAD_DIM)
    scale = HEAD_DIM ** -0.5
    nb = S // SB_TILE

    def body(q_ref, k_ref, v_ref, do_ref, nt_ref, dqkv_ref, dkt_acc, dvt_acc):
        tri_incl = _tri(lambda j, s: j <= s)
        tri_excl = _tri(lambda j, s: j < s)
        heads = [slice(h * HEAD_DIM, (h + 1) * HEAD_DIM) for h in range(2)]
        dkt_acc[...] = jnp.zeros_like(dkt_acc)
        dvt_acc[...] = jnp.zeros_like(dvt_acc)

        def q_block(qb, n_pairs, parity):
            q0 = pl.multiple_of(qb * SB_TILE, SB_TILE)
            rows = pl.ds(q0, SB_TILE)
            qt_pair = (q_ref[rows, :] * scale).astype(F32).T.astype(BF16)
            dot_pair = do_ref[rows, :].astype(F32).T.astype(BF16)
            per_head = [(hs, q_ref[rows, hs] * scale, do_ref[rows, hs], qt_pair[hs, :], dot_pair[hs, :],
                         nt_ref[rows, hs.start:hs.start + 1]) for hs in heads]

            def step(blocks, state):
                chains = [(h, kb, diagonal, pl.ds(pl.multiple_of(kb * SB_TILE, SB_TILE), SB_TILE))
                          for h in range(2) for kb, diagonal in blocks]
                ks = [k_ref[keys, per_head[h][0]] for h, _, _, keys in chains]
                logs = [_sb_logs(per_head[h][1], k, diagonal)
                        for (h, _, diagonal, _), k in zip(chains, ks)]
                das = [lax.dot_general(per_head[h][2], v_ref[keys, per_head[h][0]], NT_DIMS,
                                       preferred_element_type=F32) for h, _, _, keys in chains]
                sums = [_cumsum_mm(log_1mb, tri_incl) for _, log_1mb, _ in logs]
                left_n = [state[0][0], state[1][0]]
                left_d = [state[0][1], state[1][1]]
                dq_acc = [state[0][2], state[1][2]]
                a_s, dls = [], []
                for (h, _, diagonal, _), (log_b, log_1mb, causal), csum, da in zip(chains, logs, sums, das):
                    a = jnp.exp(log_b + (per_head[h][5] - left_n[h]) - csum)
                    if diagonal:
                        a = jnp.where(causal, a, 0.0)
                    a_s.append(a)
                    dls.append(a * da)
                    left_n[h] = left_n[h] + jnp.sum(log_1mb, axis=-1, keepdims=True)
                dsums = [_cumsum_mm(dl, tri_excl) for dl in dls]
                dzbs = []
                for (h, _, diagonal, _), (log_b, log_1mb, causal), dl, dsum in zip(chains, logs, dls, dsums):
                    dz = dl * jnp.exp(log_1mb) - (left_d[h] + dsum) * jnp.exp(log_b)
                    if diagonal:
                        dz = jnp.where(causal, dz, 0.0)
                    dzbs.append(dz.astype(BF16))
                    left_d[h] = left_d[h] + jnp.sum(dl, axis=-1, keepdims=True)
                for (h, kb, _, _), k, a, dzb in zip(chains, ks, a_s, dzbs):
                    hs, _, _, qt, dot_, _ = per_head[h]
                    dq_acc[h] = dq_acc[h] + jnp.dot(dzb, k, preferred_element_type=F32)
                    dkt_acc[kb, hs, :] += jnp.dot(qt, dzb, preferred_element_type=F32)
                    dvt_acc[kb, hs, :] += jnp.dot(dot_, a.astype(BF16), preferred_element_type=F32)
                return ((left_n[0], left_d[0], dq_acc[0]), (left_n[1], left_d[1], dq_acc[1]))

            zero1 = jnp.zeros((SB_TILE, 1), F32)
            zero = (zero1, zero1, jnp.zeros((SB_TILE, HEAD_DIM), F32))
            state = lax.fori_loop(
                0, n_pairs, lambda t, st: step([(2 * t, False), (2 * t + 1, False)], st), (zero, zero))
            last = ([(qb - 1, False)] if parity else []) + [(qb, True)]
            state = step(last, state)
            for hs, (_, _, dq_acc) in zip(heads, state):
                dqkv_ref[0, rows, hs] = (dq_acc * scale).astype(BF16)

        def q_pair_loop(j, carry):
            q_block(2 * j, j, 0)
            q_block(2 * j + 1, j, 1)
            return carry

        lax.fori_loop(0, nb // 2, q_pair_loop, 0)
        for kb in range(nb):
            dqkv_ref[1, kb * SB_TILE:(kb + 1) * SB_TILE, :] = dkt_acc[kb].T.astype(BF16)
            dqkv_ref[2, kb * SB_TILE:(kb + 1) * SB_TILE, :] = dvt_acc[kb].T.astype(BF16)

    qs, ks_, vs = _head_specs(S, HP)
    hd = pl.BlockSpec((S, 2 * HEAD_DIM), lambda b, hp: (b, hp))
    acc = pltpu.VMEM((nb, 2 * HEAD_DIM, SB_TILE), F32)
    return pl.pallas_call(
        body, name="attn_b_bwd", grid=(B, HP),
        in_specs=[qs, ks_, vs, hd, hd],
        out_specs=pl.BlockSpec((3, S, 2 * HEAD_DIM), lambda b, hp: (0, b, hp)),
        out_shape=jax.ShapeDtypeStruct((3, B * S, D), BF16),
        scratch_shapes=[acc, acc],
        compiler_params=_params(("parallel", "parallel")),
    )(qkv, qkv, qkv, do, ntot)


def _place():
    x, y, c = lax.axis_index("x"), lax.axis_index("y"), lax.axis_index("c")
    chips = [(1 - x, y), (x, 1 - y), (1 - x, 1 - y)]
    return x, y, c, 2 * x + y, chips


def _remote(src, dst, send_sem, recv_sem, dev):
    return pltpu.make_async_remote_copy(src_ref=src, dst_ref=dst, send_sem=send_sem, recv_sem=recv_sem,
                                        device_id=dev, device_id_type=MESH)


def gather_weights(shards):
    n = len(shards)

    def body(*refs):
        ins, outs = refs[:n], refs[n:2 * n]
        send1, recv1, send2, recv2, lsem = refs[2 * n:]
        x, y, c, me, chips = _place()
        local, first = [], []
        for f in range(n):
            hl = shards[f].shape[0] // 2
            cp = pltpu.make_async_copy(ins[f], outs[f].at[me], lsem.at[f])
            cp.start()
            local.append(cp)
            for j, (qx, qy) in enumerate(chips):
                half = pl.ds(c * hl, hl)
                cp = _remote(ins[f].at[half], outs[f].at[me, half],
                             send1.at[3 * f + j], recv1.at[3 * f + j], (qx, qy, c))
                cp.start()
                first.append(cp)
        passed = []
        for f in range(n):
            hl = shards[f].shape[0] // 2
            for j, (qx, qy) in enumerate(chips):
                slab = outs[f].at[2 * qx + qy, pl.ds(c * hl, hl)]
                _remote(slab, slab, send1.at[3 * f + j], recv1.at[3 * f + j], (x, y, c)).wait_recv()
                cp = _remote(slab, slab, send2.at[3 * f + j], recv2.at[3 * f + j], (x, y, 1 - c))
                cp.start()
                passed.append(cp)
        for f in range(n):
            hl = shards[f].shape[0] // 2
            for j, (qx, qy) in enumerate(chips):
                slab = outs[f].at[2 * qx + qy, pl.ds((1 - c) * hl, hl)]
                _remote(slab, slab, send2.at[3 * f + j], recv2.at[3 * f + j], (x, y, c)).wait_recv()
        for cp in first + passed:
            cp.wait_send()
        for cp in local:
            cp.wait()

    sems = pltpu.SemaphoreType.DMA((3 * n,))
    return pl.pallas_call(
        body, name="gather_weights",
        in_specs=[ANY] * n, out_specs=[ANY] * n,
        out_shape=[jax.ShapeDtypeStruct((N_CHIPS,) + s.shape, s.dtype) for s in shards],
        scratch_shapes=[sems, sems, sems, sems, pltpu.SemaphoreType.DMA((n,))],
        compiler_params=pltpu.CompilerParams(has_side_effects=True),
    )(*shards)


HBM = pl.BlockSpec(memory_space=pltpu.HBM)
SEM = pl.BlockSpec(memory_space=pltpu.SEMAPHORE)
EFFECT = pltpu.SideEffectType.DATAFLOW_SIDE_EFFECTING


def _in_hbm(a):
    return pltpu.with_memory_space_constraint(a, pltpu.HBM)


def _row_half(ref, _unused, which):
    hr = ref.shape[-2] // 2
    idx = [pl.ds(0, d) for d in ref.shape[:-2]] + [pl.ds(which * hr, hr), pl.ds(0, ref.shape[-1])]
    return ref.at[tuple(idx)]


def cast_place(ids, w, lead, L):
    R, C = w.shape[-2:]

    def body(ids_ref, w_ref, s_ref, land_ref):
        v = w_ref[...].astype(BF16)
        s_ref[...] = v
        land_ref[...] = v

    return pl.pallas_call(
        body, name="cast_place",
        grid_spec=pltpu.PrefetchScalarGridSpec(
            num_scalar_prefetch=1, grid=(L,),
            in_specs=[pl.BlockSpec((None,) * (w.ndim - 2) + (R, C), lambda l, ids: (*lead(l), 0, 0))],
            out_specs=[pl.BlockSpec((None, R, C), lambda l, ids: (l, 0, 0)),
                       pl.BlockSpec((None, None, R, C), lambda l, ids: (ids[1], l, 0, 0))]),
        out_shape=[jax.ShapeDtypeStruct((L, R, C), BF16), jax.ShapeDtypeStruct((N_CHIPS, L, R, C), BF16)],
        compiler_params=_params(("parallel",)),
    )(ids, w)


def gather_start(layers, zones, after):
    flat = [s for lay in layers for s in lay]
    flat_zones = [z for lay in zones for z in lay]
    counts = [len(lay) for lay in layers]
    n, nl = len(flat), len(layers)

    def body(*refs):
        ins, lands = refs[:n], refs[n:2 * n]
        send, recv = refs[2 * n + 1:2 * n + 1 + nl], refs[2 * n + 1 + nl:2 * n + 1 + 2 * nl]
        token = refs[-1]
        x, y, c, me, chips = _place()
        f = 0
        for li, cnt in enumerate(counts):
            for k in range(cnt):
                for j, (qx, qy) in enumerate(chips):
                    _remote(_row_half(ins[f], 0, c), _row_half(lands[f].at[me], 0, c),
                            send[li].at[3 * k + j], recv[li].at[3 * k + j], (qx, qy, c)).start()
                f += 1
        token[...] = jnp.zeros_like(token)

    sem_shapes = [pltpu.SemaphoreType.DMA((3 * cnt,)) for cnt in counts]
    land_shapes = [(N_CHIPS,) + s.shape for s in flat]
    res = pl.pallas_call(
        body, name="gather_start",
        out_shape=(*sem_shapes, *sem_shapes,
                   *[pltpu.HBM(s.shape, s.dtype) for s in flat],
                   *[pltpu.HBM(shp, s.dtype) for shp, s in zip(land_shapes, flat)],
                   jax.ShapeDtypeStruct((8, 128), F32)),
        in_specs=[HBM] * (2 * n) + [ANY],
        out_specs=(*[SEM] * (2 * nl), *[HBM] * (2 * n), pl.BlockSpec(memory_space=pltpu.VMEM)),
        input_output_aliases={k: 2 * nl + k for k in range(2 * n)},
        compiler_params=pltpu.CompilerParams(has_side_effects=EFFECT),
    )(*[_in_hbm(s) for s in flat], *[_in_hbm(z) for z in flat_zones], after)
    send, recv = res[:nl], res[nl:2 * nl]
    thru, lands, token = res[2 * nl:2 * nl + n], res[2 * nl + n:2 * nl + 2 * n], res[-1]
    out, f = [], 0
    for li, cnt in enumerate(counts):
        out.append((send[li], recv[li], list(thru[f:f + cnt]), list(lands[f:f + cnt])))
        f += cnt
    return out, token


def gather_wait(li, send, recv, shards, lands, after):
    m = len(shards)

    def body(*refs):
        ins, lnd = refs[:m], refs[m:2 * m]
        snd, rcv = refs[2 * m], refs[2 * m + 1]
        x, y, c, me, chips = _place()
        for k in range(m):
            for j, (qx, qy) in enumerate(chips):
                cp = _remote(_row_half(ins[k], 0, c), _row_half(lnd[k].at[2 * qx + qy], 0, c),
                             snd.at[3 * k + j], rcv.at[3 * k + j], (qx, qy, c))
                cp.wait_send()
                cp.wait_recv()

    res = pl.pallas_call(
        body, name=f"gather_wait_{li}",
        out_shape=(*[pltpu.HBM(s.shape, s.dtype) for s in shards],
                   *[pltpu.HBM(s.shape, s.dtype) for s in lands]),
        in_specs=[HBM] * (2 * m) + [SEM, SEM, ANY], out_specs=[HBM] * (2 * m),
        input_output_aliases={k: k for k in range(2 * m)},
        compiler_params=pltpu.CompilerParams(has_side_effects=EFFECT),
    )(*shards, *lands, send, recv, after)
    return list(res[:m]), list(res[m:])


def gather_forward(li, lands):
    m = len(lands)

    def body(*refs):
        outs = refs[m:2 * m]
        send, recv = refs[2 * m:]
        x, y, c, me, chips = _place()
        passed = []
        for k in range(m):
            for j, (qx, qy) in enumerate(chips):
                slab = _row_half(outs[k].at[2 * qx + qy], 0, c)
                cp = _remote(slab, slab, send.at[3 * k + j], recv.at[3 * k + j], (x, y, 1 - c))
                cp.start()
                passed.append(cp)
        for k in range(m):
            for j, (qx, qy) in enumerate(chips):
                slab = _row_half(outs[k].at[2 * qx + qy], 0, 1 - c)
                _remote(slab, slab, send.at[3 * k + j], recv.at[3 * k + j], (x, y, c)).wait_recv()
        for cp in passed:
            cp.wait_send()

    sems = pltpu.SemaphoreType.DMA((3 * m,))
    return pl.pallas_call(
        body, name=f"gather_forward_{li}",
        in_specs=[ANY] * m, out_specs=[ANY] * m,
        out_shape=[jax.ShapeDtypeStruct(s.shape, s.dtype) for s in lands],
        input_output_aliases={k: k for k in range(m)},
        scratch_shapes=[sems, sems],
        compiler_params=pltpu.CompilerParams(has_side_effects=True),
    )(*lands)


def pair_exchange_rows(li, grads):
    n = len(grads)

    def body(*refs):
        ins, outs = refs[:n], refs[n:2 * n]
        send, recv = refs[2 * n:]
        x, y, c, _, _ = _place()
        cps = []
        for k in range(n):
            cp = _remote(_row_half(ins[k], 0, 1 - c), outs[k], send.at[k], recv.at[k], (x, y, 1 - c))
            cp.start()
            cps.append(cp)
        for cp in cps:
            cp.wait()

    sems = pltpu.SemaphoreType.DMA((n,))
    return pl.pallas_call(
        body, name=f"pair_exchange_{li}",
        in_specs=[ANY] * n, out_specs=[ANY] * n,
        out_shape=[jax.ShapeDtypeStruct(g.shape[:-2] + (g.shape[-2] // 2, g.shape[-1]), g.dtype)
                   for g in grads],
        scratch_shapes=[sems, sems],
        compiler_params=pltpu.CompilerParams(has_side_effects=True),
    )(*grads)


def pair_add_rows(ids, grad, recv):
    L, P, hr, C = recv.shape

    def body(ids_ref, a_ref, b_ref, o_ref):
        o_ref[...] = (a_ref[...].astype(F32) + b_ref[...].astype(F32)).astype(o_ref.dtype)

    blk = (None, None, hr, C)
    return pl.pallas_call(
        body, name="pair_add",
        grid_spec=pltpu.PrefetchScalarGridSpec(
            num_scalar_prefetch=1, grid=(L, P),
            in_specs=[pl.BlockSpec(blk, lambda l, p, ids: (l, p, ids[0], 0)),
                      pl.BlockSpec(blk, lambda l, p, ids: (l, p, 0, 0))],
            out_specs=pl.BlockSpec(blk, lambda l, p, ids: (l, p, 0, 0))),
        out_shape=jax.ShapeDtypeStruct(recv.shape, recv.dtype),
        compiler_params=_params(("parallel", "parallel")),
    )(ids, grad, recv)


def reduce_start(li, parts):
    m = len(parts)

    def body(*refs):
        ins, lands = refs[:m], refs[m:2 * m]
        send, recv = refs[2 * m], refs[2 * m + 1]
        token = refs[-1]
        x, y, c, me, chips = _place()
        for k in range(m):
            rows = pl.ds(0, parts[k].shape[0])
            for j, (qx, qy) in enumerate(chips):
                _remote(ins[k].at[rows, 2 * qx + qy], lands[k].at[rows, me],
                        send.at[3 * k + j], recv.at[3 * k + j], (qx, qy, c)).start()
        token[...] = jnp.zeros_like(token)

    sems = pltpu.SemaphoreType.DMA((3 * m,))
    res = pl.pallas_call(
        body, name=f"reduce_start_{li}",
        out_shape=(sems, sems, *[pltpu.HBM(s.shape, s.dtype) for s in parts],
                   *[pltpu.HBM(s.shape, s.dtype) for s in parts], jax.ShapeDtypeStruct((8, 128), F32)),
        in_specs=[HBM] * (2 * m),
        out_specs=(SEM, SEM, *[HBM] * (2 * m), pl.BlockSpec(memory_space=pltpu.VMEM)),
        input_output_aliases={k: 2 + k for k in range(2 * m)},
        compiler_params=pltpu.CompilerParams(has_side_effects=EFFECT),
    )(*[_in_hbm(s) for s in parts], *[_in_hbm(lax.empty(s.shape, s.dtype)) for s in parts])
    return res[0], res[1], list(res[2:2 + m]), list(res[2 + m:2 + 2 * m]), res[-1]


def reduce_wait(li, send, recv, parts, lands, after):
    m = len(parts)

    def body(*refs):
        ins, lnd = refs[:m], refs[m:2 * m]
        snd, rcv = refs[2 * m], refs[2 * m + 1]
        x, y, c, me, chips = _place()
        for k in range(m):
            rows = pl.ds(0, parts[k].shape[0])
            for j, (qx, qy) in enumerate(chips):
                cp = _remote(ins[k].at[rows, 2 * qx + qy], lnd[k].at[rows, 2 * qx + qy],
                             snd.at[3 * k + j], rcv.at[3 * k + j], (qx, qy, c))
                cp.wait_send()
                cp.wait_recv()

    res = pl.pallas_call(
        body, name=f"reduce_wait_{li}",
        out_shape=(*[pltpu.HBM(s.shape, s.dtype) for s in parts],
                   *[pltpu.HBM(s.shape, s.dtype) for s in lands]),
        in_specs=[HBM] * (2 * m) + [SEM, SEM, ANY], out_specs=[HBM] * (2 * m),
        input_output_aliases={k: k for k in range(2 * m)},
        compiler_params=pltpu.CompilerParams(has_side_effects=EFFECT),
    )(*parts, *lands, send, recv, after)
    return list(res[:m]), list(res[m:])


def chip_sum_rows(ids, land, part, gfull, l0):
    L, P, hr, C = land.shape

    def body(ids_ref, land_ref, part_ref, _g, o_ref):
        tot = None
        for q in range(P):
            term = jnp.where(ids_ref[1] == q, part_ref[...], land_ref[q]).astype(F32)
            tot = term if tot is None else tot + term
        o_ref[...] = tot

    return pl.pallas_call(
        body, name="chip_sum",
        grid_spec=pltpu.PrefetchScalarGridSpec(
            num_scalar_prefetch=1, grid=(L,),
            in_specs=[pl.BlockSpec((None, P, hr, C), lambda l, ids: (l, 0, 0, 0)),
                      pl.BlockSpec((None, None, hr, C), lambda l, ids: (l, ids[1], 0, 0)), ANY],
            out_specs=pl.BlockSpec((None, hr, C), lambda l, ids: (l0 + l, ids[0], 0))),
        out_shape=jax.ShapeDtypeStruct(gfull.shape, gfull.dtype),
        input_output_aliases={3: 0},
        compiler_params=_params(("arbitrary",)),
    )(ids, land, part, gfull)


def half_swap_rows(grads):
    n = len(grads)

    def body(*refs):
        outs = refs[n:2 * n]
        send, recv = refs[2 * n:]
        x, y, c, _, _ = _place()
        cps = []
        for k in range(n):
            mine = _row_half(outs[k], 0, c)
            cp = _remote(mine, mine, send.at[k], recv.at[k], (x, y, 1 - c))
            cp.start()
            cps.append(cp)
        for k in range(n):
            theirs = _row_half(outs[k], 0, 1 - c)
            _remote(theirs, theirs, send.at[k], recv.at[k], (x, y, c)).wait_recv()
        for cp in cps:
            cp.wait_send()

    sems = pltpu.SemaphoreType.DMA((n,))
    return pl.pallas_call(
        body, name="half_swap",
        in_specs=[ANY] * n, out_specs=[ANY] * n,
        out_shape=[jax.ShapeDtypeStruct(g.shape, g.dtype) for g in grads],
        input_output_aliases={k: k for k in range(n)},
        scratch_shapes=[sems, sems],
        compiler_params=pltpu.CompilerParams(has_side_effects=True),
    )(*grads)


def pair_exchange(grads):
    n = len(grads)

    def body(*refs):
        ins, outs = refs[:n], refs[n:2 * n]
        send, recv = refs[2 * n:]
        x, y, c, _, _ = _place()
        cps = []
        for f in range(n):
            hl = grads[f].shape[0] // 2
            cp = _remote(ins[f].at[pl.ds((1 - c) * hl, hl)], outs[f], send.at[f], recv.at[f], (x, y, 1 - c))
            cp.start()
            cps.append(cp)
        for cp in cps:
            cp.wait()

    sems = pltpu.SemaphoreType.DMA((n,))
    return pl.pallas_call(
        body, name="pair_exchange",
        in_specs=[ANY] * n, out_specs=[ANY] * n,
        out_shape=[jax.ShapeDtypeStruct((g.shape[0] // 2,) + g.shape[1:], g.dtype) for g in grads],
        scratch_shapes=[sems, sems],
        compiler_params=pltpu.CompilerParams(has_side_effects=True),
    )(*grads)


def chip_exchange(parts):
    n = len(parts)

    def body(*refs):
        ins, outs = refs[:n], refs[n:2 * n]
        send, recv, lsem = refs[2 * n:]
        x, y, c, me, chips = _place()
        local, sent = [], []
        for f in range(n):
            hl = parts[f].shape[0]
            rows = pl.ds(0, hl)
            cp = pltpu.make_async_copy(ins[f].at[rows, me], outs[f].at[rows, me], lsem.at[f])
            cp.start()
            local.append(cp)
            for j, (qx, qy) in enumerate(chips):
                cp = _remote(ins[f].at[rows, 2 * qx + qy], outs[f].at[rows, me],
                             send.at[3 * f + j], recv.at[3 * f + j], (qx, qy, c))
                cp.start()
                sent.append(cp)
        for f in range(n):
            rows = pl.ds(0, parts[f].shape[0])
            for j, (qx, qy) in enumerate(chips):
                slab = outs[f].at[rows, 2 * qx + qy]
                _remote(slab, slab, send.at[3 * f + j], recv.at[3 * f + j], (x, y, c)).wait_recv()
        for cp in sent:
            cp.wait_send()
        for cp in local:
            cp.wait()

    sems = pltpu.SemaphoreType.DMA((3 * n,))
    return pl.pallas_call(
        body, name="chip_exchange",
        in_specs=[ANY] * n, out_specs=[ANY] * n,
        out_shape=[jax.ShapeDtypeStruct(s.shape, s.dtype) for s in parts],
        scratch_shapes=[sems, sems, pltpu.SemaphoreType.DMA((n,))],
        compiler_params=pltpu.CompilerParams(has_side_effects=True),
    )(*parts)


def half_swap(grads):
    n = len(grads)

    def body(*refs):
        ins, outs = refs[:n], refs[n:2 * n]
        send, recv = refs[2 * n:]
        x, y, c, _, _ = _place()
        cps = []
        for f in range(n):
            hl = grads[f].shape[0] // 2
            mine = pl.ds(c * hl, hl)
            cp = _remote(outs[f].at[mine], outs[f].at[mine], send.at[f], recv.at[f], (x, y, 1 - c))
            cp.start()
            cps.append(cp)
        for f in range(n):
            hl = grads[f].shape[0] // 2
            theirs = outs[f].at[pl.ds((1 - c) * hl, hl)]
            _remote(theirs, theirs, send.at[f], recv.at[f], (x, y, c)).wait_recv()
        for cp in cps:
            cp.wait_send()

    sems = pltpu.SemaphoreType.DMA((n,))
    return pl.pallas_call(
        body, name="half_swap",
        in_specs=[ANY] * n, out_specs=[ANY] * n,
        out_shape=[jax.ShapeDtypeStruct(g.shape, g.dtype) for g in grads],
        input_output_aliases={f: f for f in range(n)},
        scratch_shapes=[sems, sems],
        compiler_params=pltpu.CompilerParams(has_side_effects=True),
    )(*grads)


def all_sum_small(v):
    R = v.shape[0]

    def body(v_ref, o_ref, land, send, recv):
        x, y, c, _, _ = _place()
        me = 4 * x + 2 * y + c
        land[me] = v_ref[...]
        peers = [(px, py, pc) for px in range(2) for py in range(2) for pc in range(2)]
        cps = []
        for k in range(1, 8):
            dev = (x ^ (k >> 2), y ^ ((k >> 1) & 1), c ^ (k & 1))
            cp = _remote(v_ref, land.at[me], send.at[k - 1], recv.at[k - 1], dev)
            cp.start()
            cps.append(cp)
        for k in range(1, 8):
            src = 4 * (x ^ (k >> 2)) + 2 * (y ^ ((k >> 1) & 1)) + (c ^ (k & 1))
            _remote(v_ref, land.at[src], send.at[k - 1], recv.at[k - 1], (x, y, c)).wait_recv()
        for cp in cps:
            cp.wait_send()
        tot = land[0]
        for d in range(1, len(peers)):
            tot = tot + land[d]
        o_ref[...] = tot

    sems = pltpu.SemaphoreType.DMA((7,))
    vm = pl.BlockSpec(memory_space=pltpu.VMEM)
    return pl.pallas_call(
        body, name="all_sum_small", in_specs=[vm], out_specs=vm,
        out_shape=jax.ShapeDtypeStruct(v.shape, F32),
        scratch_shapes=[pltpu.VMEM((8, R, 128), F32), sems, sems],
        compiler_params=pltpu.CompilerParams(has_side_effects=True),
    )(v)


def _row_tile(R):
    for t in range(min(R, 512) // 8 * 8, 7, -8):
        if R % t == 0:
            return t
    return R


def pair_add(cidx, grad, recv):
    hl, P, R, C = recv.shape
    tr = _row_tile(R)

    def body(c_ref, a_ref, b_ref, o_ref):
        o_ref[...] = (a_ref[...].astype(F32) + b_ref[...].astype(F32)).astype(o_ref.dtype)

    blk = (None, None, tr, C)
    return pl.pallas_call(
        body, name="pair_add",
        grid_spec=pltpu.PrefetchScalarGridSpec(
            num_scalar_prefetch=1, grid=(hl, P, R // tr),
            in_specs=[pl.BlockSpec(blk, lambda l, p, r, c: (c[0] * hl + l, p, r, 0)),
                      pl.BlockSpec(blk, lambda l, p, r, c: (l, p, r, 0))],
            out_specs=pl.BlockSpec(blk, lambda l, p, r, c: (l, p, r, 0))),
        out_shape=jax.ShapeDtypeStruct(recv.shape, recv.dtype),
        compiler_params=_params(("parallel", "parallel", "parallel")),
    )(cidx, grad, recv)


def chip_sum(cidx, land, L):
    hl, P, R, C = land.shape
    tr = _row_tile(R)

    def body(c_ref, a_ref, o_ref):
        tot = a_ref[0].astype(F32)
        for q in range(1, P):
            tot = tot + a_ref[q].astype(F32)
        o_ref[...] = tot

    return pl.pallas_call(
        body, name="chip_sum",
        grid_spec=pltpu.PrefetchScalarGridSpec(
            num_scalar_prefetch=1, grid=(hl, R // tr),
            in_specs=[pl.BlockSpec((None, P, tr, C), lambda l, r, c: (l, 0, r, 0))],
            out_specs=pl.BlockSpec((None, tr, C), lambda l, r, c: (c[0] * hl + l, r, 0))),
        out_shape=jax.ShapeDtypeStruct((L, R, C), F32),
        compiler_params=_params(("parallel", "parallel")),
    )(cidx, land)


def adamw(w, g, m, v):
    lead, (R, C) = w.shape[:-2], w.shape[-2:]
    tr = _row_tile(R)
    c1 = 1.0 / (1.0 - ADAM_B1 ** ADAM_STEP)
    c2 = 1.0 / (1.0 - ADAM_B2 ** ADAM_STEP)

    def body(w_ref, g_ref, m_ref, v_ref, go_ref, d_ref, nm_ref, nv_ref):
        gv = g_ref[...]
        go_ref[...] = gv
        nm = ADAM_B1 * m_ref[...] + (1.0 - ADAM_B1) * gv
        nv = ADAM_B2 * v_ref[...] + (1.0 - ADAM_B2) * (gv * gv)
        nm_ref[...] = nm
        nv_ref[...] = nv
        d_ref[...] = -ADAM_LR * ((nm * c1) / (jnp.sqrt(nv * c2) + ADAM_EPS) + ADAM_WD * w_ref[...])

    def flat(idx):
        l = 0
        for i, n in zip(idx, lead):
            l = l * n + i
        return l

    blk = pl.BlockSpec((None,) * len(lead) + (tr, C), lambda *ix: (*ix, 0))
    gblk = pl.BlockSpec((None, tr, C), lambda *ix: (flat(ix[:-1]), ix[-1], 0))
    osh = jax.ShapeDtypeStruct(w.shape, F32)
    return pl.pallas_call(
        body, name="adamw", grid=(*lead, R // tr),
        in_specs=[blk, gblk, blk, blk], out_specs=[blk, blk, blk, blk], out_shape=[osh, osh, osh, osh],
        compiler_params=_params(("parallel",) * (len(lead) + 1)),
    )(w, g, m, v)


def kernel(x, w_qkv_a, w_o_a, rel_bias, w_qkv_b, w_o_b, ffn_w_gate, ffn_w_up, ffn_w_down, ln_g, ln_b, loss_target, m_w_qkv_a, m_w_o_a, m_rel_bias, m_w_qkv_b, m_w_o_b, m_ffn_w_gate, m_ffn_w_up, m_ffn_w_down, m_ln_g, m_ln_b, v_w_qkv_a, v_w_o_a, v_rel_bias, v_w_qkv_b, v_w_o_b, v_ffn_w_gate, v_ffn_w_up, v_ffn_w_down, v_ln_g, v_ln_b):
    B, S, D = x.shape
    M = B * S
    depth = ffn_w_gate.shape[0]
    n_ffn = 2 * depth
    H = D // HEAD_DIM
    HP = H // 2
    Fs = ffn_w_gate.shape[-1]
    alpha = (2.0 * depth) ** 0.25
    assert S % QB_A == 0 and S % SB_TILE == 0 and rel_bias.shape == (N_REL, H)

    me = 2 * lax.axis_index("x") + lax.axis_index("y")
    ids = jnp.stack([lax.axis_index("c"), me]).astype(jnp.int32)

    def tr(a):
        return jnp.swapaxes(a, -1, -2)

    gate_t, up_t = tr(ffn_w_gate), tr(ffn_w_up)

    def mixer_slots(i):
        wq, wo, mix = (w_qkv_a, w_o_a, "a") if i % 2 == 0 else (w_qkv_b, w_o_b, "b")
        return [(w, (lambda l, i=i: (i // 2,)), 1, (role + mix, i // 2), [((i, 1), role, 0)])
                for role, w in (("q", wq), ("o", wo))]

    def ffn_slots(i, subs):
        first = 0 if subs[0] == 0 else 1
        return [(w, (lambda l, i=i, first=first: (i, first + l)), len(subs), (role, 2 * i + first),
                 [((i, j), role, k) for k, j in enumerate(subs)])
                for role, w in (("gate", gate_t), ("up", up_t), ("down", ffn_w_down))]

    groups = [ffn_slots(0, [0]), mixer_slots(0) + ffn_slots(0, [2])]
    groups += [mixer_slots(i) + ffn_slots(i, [0, 2]) for i in range(1, depth)]

    def group_of(i, j):
        return i + 1 if i else (0 if j == 0 else 1)

    lng_p, lnb_p = gather_weights([ln_g, ln_b])
    lng = jnp.moveaxis(lng_p, 0, 2).reshape(depth, 3, 1, D)
    lnb = jnp.moveaxis(lnb_p, 0, 2).reshape(depth, 3, 1, D)
    placed = [[cast_place(ids, src, lead, L) for src, lead, L, _, _ in grp] for grp in groups]
    in_flight, _ = gather_start([[s for s, _ in grp] for grp in placed],
                                [[z for _, z in grp] for grp in placed], lng_p)

    tab_t = jnp.pad(rel_bias.T, ((0, 0), (0, REL_PAD - N_REL)))
    vr = bias_vec(tab_t).reshape(HP, 2, VR_W)

    xf = x.reshape(M, D)
    xb = xf.astype(BF16)
    saved, weights, landed = [], {}, set()
    for i in range(depth):
        for j in range(3):
            gi = group_of(i, j)
            if gi not in landed:
                landed.add(gi)
                send, recv, thru, lands = in_flight[gi]
                _, lands = gather_wait(gi, send, recv, thru, lands, xf)
                for slot, arr in zip(groups[gi], gather_forward(gi, lands)):
                    for unit, role, l in slot[4]:
                        weights[unit, role] = (arr, l)
            gam, bet = lng[i, j], lnb[i, j]
            if j != 1:
                (wg, l), (wu, _), (wd, _) = (weights[(i, j), r] for r in ("gate", "up", "down"))
                g, u, h = ffn_up(xb, wg, wu, l)
                xo, xob, xhat, rstd = mm_ln(h, wd, l, xf, gam, bet, alpha, 0.5, True)
                saved.append(("ffn", i, l, xb, g, u, h, xhat, rstd, gam))
            else:
                wq, wo = weights[(i, 1), "q"][0], weights[(i, 1), "o"][0]
                qkv = qkv_proj(xb, wq, 0)
                if i % 2 == 0:
                    o, ntot = attn_a_fwd(qkv, vr, B, S), None
                else:
                    o, ntot = attn_b_fwd(qkv, B, S)
                xo, xob, xhat, rstd = mm_ln(o, wo, 0, xf, gam, bet, alpha, 1.0, False)
                saved.append(("a" if i % 2 == 0 else "b", i, 0, xb, qkv, o, ntot, xhat, rstd, gam))
            xf, xb = xo, xob

    dy, loss_part = loss_head(xf, loss_target.reshape(M, D))
    loss = lax.psum(loss_part[0, 0], ("x", "y", "c"))

    Cq, Ro = w_qkv_a.shape[-1], w_o_a.shape[1]
    dgam = lax.empty((3 * depth, 1, D), F32)
    dbet = lax.empty((3 * depth, 1, D), F32)
    dvrs, reducing = [], {}
    started = None
    gbufs = [[lax.empty((L, N_CHIPS) + src.shape[-2:], BF16) for src, _, L, _, _ in grp] for grp in groups]
    grad_of = {(unit, role): (gi, si, l) for gi, grp in enumerate(groups) for si, slot in enumerate(grp)
               for unit, role, l in slot[4]}
    for sub in reversed(range(3 * depth)):
        i, j = divmod(sub, 3)
        kind, _, l, xb_in, t1, t2, t3, xhat, rstd, gam = saved[sub]
        scale = 0.5 if kind == "ffn" else 1.0
        if started is not None:
            gam = gam + started[0, 0]
            started = None
        dpre, dyb, dgam, dbet = ln_bwd(dy, xhat, rstd, gam, scale, dgam, dbet, sub)
        if kind == "ffn":
            g, u, h = t1, t2, t3
            (gi, sg, _), (_, su, _), (_, sd, _) = (grad_of[(i, j), r] for r in ("gate", "up", "down"))
            wg, wu, wd = (weights[(i, j), r][0] for r in ("gate", "up", "down"))
            dg, du, gbufs[gi][sd], gbufs[gi][sg], gbufs[gi][su] = ffn_bwd(
                gbufs[gi][sd], gbufs[gi][sg], gbufs[gi][su], l, xb_in, dyb, wd, g, u, h)
            dy = ffn_dx(dg, du, wg, wu, l, dpre, alpha)
        else:
            qkv, o = t1, t2
            (gi, sq, _), (_, so, _) = grad_of[(i, j), "q"], grad_of[(i, j), "o"]
            wq, wo = weights[(i, j), "q"][0], weights[(i, j), "o"][0]
            do = o_proj_bwd(dyb, wo, l)
            gbufs[gi][so] = dw_o(gbufs[gi][so], l, o, dyb)
            if kind == "a":
                dqkv, dvr = attn_a_bwd(qkv, vr, do, B, S)
                dvrs.append(dvr.reshape(B, H, VR_W))
            else:
                dqkv = attn_b_bwd(qkv, do, t3, B, S)
            gbufs[gi][sq] = dw_qkv(gbufs[gi][sq], l, xb_in, dqkv)
            dy = qkv_proj_bwd(dqkv, wq, l, dpre, alpha)
        gi = group_of(i, j)
        if sub == 0 or group_of(*divmod(sub - 1, 3)) != gi:
            from_sib = pair_exchange_rows(gi, gbufs[gi])
            parts = [pair_add_rows(ids, g_, r_) for g_, r_ in zip(gbufs[gi], from_sib)]
            reducing[gi] = reduce_start(gi, parts)
            started = reducing[gi][4]
    grad_x = dy.reshape(B, S, D)

    la, lb = w_qkv_a.shape[0], w_qkv_b.shape[0]
    full = {"qa": lax.empty((la, D, Cq), F32), "oa": lax.empty((la, Ro, D), F32),
            "qb": lax.empty((lb, D, Cq), F32), "ob": lax.empty((lb, Ro, D), F32),
            "gate": lax.empty((n_ffn, Fs, D), F32), "up": lax.empty((n_ffn, Fs, D), F32),
            "down": lax.empty((n_ffn, Fs, D), F32)}
    for gi in reversed(range(len(groups))):
        send, recv, parts, lands, _ = reducing[gi]
        parts, lands = reduce_wait(gi, send, recv, parts, lands, dy)
        for (_, _, _, (name, l0), _), part, land in zip(groups[gi], parts, lands):
            full[name] = chip_sum_rows(ids, land, part, full[name], l0)
    g_qa, g_oa, g_qb, g_ob, g_gate, g_up, g_down = half_swap_rows(
        [full[k] for k in ("qa", "oa", "qb", "ob", "gate", "up", "down")])

    d_tab_t = bias_vec_bwd(jnp.concatenate(dvrs, axis=0))
    small = jnp.concatenate([d_tab_t.reshape(-1), dgam.reshape(-1), dbet.reshape(-1)])
    n_small = small.shape[0]
    rows = -(-n_small // (8 * 128)) * 8
    tot = all_sum_small(jnp.pad(small, (0, rows * 128 - n_small)).reshape(rows, 128)).reshape(-1)
    n_tab, n_ln = H * REL_PAD, 3 * depth * D
    g_rel = tot[:n_tab].reshape(H, REL_PAD)[:, :N_REL].T
    ln_cols = D // N_CHIPS

    def ln_shard(flat):
        return lax.dynamic_slice_in_dim(flat.reshape(depth, 3, D), me * ln_cols, ln_cols, axis=2)

    g_lng = ln_shard(tot[n_tab:n_tab + n_ln])
    g_lnb = ln_shard(tot[n_tab + n_ln:n_tab + 2 * n_ln])

    def upd(w, g, m, v):
        if w.ndim == 2:
            return tuple(a[0] for a in adamw(w[None], g, m[None], v[None]))
        return adamw(w, g, m, v)

    res = [
        upd(w_qkv_a, g_qa, m_w_qkv_a, v_w_qkv_a),
        upd(w_o_a, g_oa, m_w_o_a, v_w_o_a),
        upd(rel_bias, g_rel.reshape(1, N_REL, H), m_rel_bias, v_rel_bias),
        upd(w_qkv_b, g_qb, m_w_qkv_b, v_w_qkv_b),
        upd(w_o_b, g_ob, m_w_o_b, v_w_o_b),
        tuple(tr(a) for a in upd(gate_t, g_gate, tr(m_ffn_w_gate), tr(v_ffn_w_gate))),
        tuple(tr(a) for a in upd(up_t, g_up, tr(m_ffn_w_up), tr(v_ffn_w_up))),
        upd(ffn_w_down, g_down, m_ffn_w_down, v_ffn_w_down),
        upd(ln_g, g_lng, m_ln_g, v_ln_g),
        upd(ln_b, g_lnb, m_ln_b, v_ln_b),
    ]
    grads = [r[0] for r in res]
    deltas = [r[1] for r in res]
    new_m = [r[2] for r in res]
    new_v = [r[3] for r in res]
    return (loss, grad_x, *grads, *deltas, *new_m, *new_v)
```

```python
import functools
import math

import jax
import jax.numpy as jnp
from jax import lax
from jax.experimental import pallas as pl
from jax.experimental.pallas import tpu as pltpu

F32 = jnp.float32
BF16 = jnp.bfloat16
MESH = pl.DeviceIdType.MESH

N_CHIPS = 4
HEAD_DIM = 64
CHUNK = 64
LEFT_CHUNKS = 8
LOOKBACK = LEFT_CHUNKS * CHUNK
REL_CLIP = 128
N_REL = 2 * REL_CLIP + 1
REL_PAD = 384
SB_TILE = 256
QB_A = 256
KW_A = QB_A + LOOKBACK
VR_W = 1024
VR_C0 = KW_A - 1
assert math.frexp(HEAD_DIM ** -0.5)[0] == 0.5
LN_EPS = 1e-5
ADAM_LR, ADAM_B1, ADAM_B2, ADAM_EPS, ADAM_WD, ADAM_STEP = 0.001, 0.9, 0.999, 1e-08, 0.01, 10
NEG = -1e30
VMEM_LIMIT = 56 * 1024 * 1024

NT_DIMS = (((1,), (1,)), ((), ()))
TN_DIMS = (((0,), (0,)), ((), ()))
ANY = pl.BlockSpec(memory_space=pl.ANY)


def _params(sem=None):
    return pltpu.CompilerParams(dimension_semantics=sem, vmem_limit_bytes=VMEM_LIMIT)


def _tile(n, pref):
    t = min(n, pref)
    assert n % t == 0, (n, pref)
    return t


def _sigmoid(z):
    return 1.0 / (1.0 + jnp.exp(-z))


def _mm_call(name, operands, in_specs, out_shape, out_spec, grid, dims_list, acc_shape,
             add_coef=None, aliases=None):
    n_pairs = len(dims_list)
    nk = grid[-1]
    has_add = add_coef is not None
    n_alias = len(aliases) if aliases else 0

    def body(*refs):
        pair_refs = refs[:2 * n_pairs]
        pos = 2 * n_pairs
        add_ref = refs[pos] if has_add else None
        pos += (1 if has_add else 0) + n_alias
        o_ref = refs[pos]
        acc_ref = refs[pos + 1] if nk > 1 else None

        def product():
            part = None
            for i, dims in enumerate(dims_list):
                d = lax.dot_general(pair_refs[2 * i][...], pair_refs[2 * i + 1][...], dims,
                                    preferred_element_type=F32)
                part = d if part is None else part + d
            return part

        def finish(r):
            if has_add:
                r = r + add_coef * add_ref[...]
            o_ref[...] = r.astype(o_ref.dtype)

        if nk == 1:
            finish(product())
        else:
            k = pl.program_id(len(grid) - 1)

            @pl.when(k == 0)
            def _():
                acc_ref[...] = jnp.zeros_like(acc_ref)

            acc_ref[...] += product()

            @pl.when(k == nk - 1)
            def _():
                finish(acc_ref[...])

    sem = ("parallel",) * (len(grid) - 1) + ("arbitrary",)
    return pl.pallas_call(
        body, name=name, grid=grid, in_specs=in_specs, out_specs=out_spec, out_shape=out_shape,
        scratch_shapes=[pltpu.VMEM(acc_shape, F32)] if nk > 1 else [],
        input_output_aliases=aliases or {},
        compiler_params=_params(sem),
    )(*operands)


def qkv_proj(xb, w, l):
    M, D = xb.shape
    C = w.shape[-1]
    tm = _tile(M, 512)
    return _mm_call(
        "qkv_proj", (xb, w),
        [pl.BlockSpec((tm, D), lambda p, i, k: (i, 0)),
         pl.BlockSpec((None, None, D, C), lambda p, i, k: (p, l, 0, 0))],
        jax.ShapeDtypeStruct((M, N_CHIPS * C), BF16),
        pl.BlockSpec((tm, C), lambda p, i, k: (i, p)),
        (N_CHIPS, M // tm, 1), [(((1,), (0,)), ((), ()))], None)


def o_proj_bwd(dyb, w, l, after):
    M, D = dyb.shape
    R = w.shape[2]
    tm = _tile(M, 512)

    def body(a_ref, w_ref, _after, o_ref):
        o_ref[...] = lax.dot_general(a_ref[...], w_ref[...], NT_DIMS,
                                     preferred_element_type=F32).astype(o_ref.dtype)

    return pl.pallas_call(
        body, name="o_proj_bwd", grid=(N_CHIPS, M // tm),
        in_specs=[pl.BlockSpec((tm, D), lambda p, i: (i, 0)),
                  pl.BlockSpec((None, None, R, D), lambda p, i: (p, l, 0, 0)),
                  pl.BlockSpec(after.shape, lambda p, i: (0, 0))],
        out_specs=pl.BlockSpec((tm, R), lambda p, i: (i, p)),
        out_shape=jax.ShapeDtypeStruct((M, N_CHIPS * R), BF16),
        compiler_params=_params(("parallel", "parallel")),
    )(dyb, w, after)


def qkv_proj_bwd(dqkv, w, l, dpre, alpha, ln=None):
    _, M, D = dqkv.shape
    C = w.shape[3]
    tm = _tile(M, 512)
    T = math.gcd(D, C)

    def body(a_ref, w_ref, add_ref, *rest):
        acc = alpha * add_ref[...]
        for t in range(3 * D // T):
            pa, ca = divmod(t * T, D)
            pw, cw = divmod(t * T, C)
            acc = acc + lax.dot_general(a_ref[pa, :, ca:ca + T], w_ref[pw, :, cw:cw + T], NT_DIMS,
                                        preferred_element_type=F32)
        if ln is None:
            rest[0][...] = acc
        else:
            xh_ref, rs_ref, g_ref, _dg, _db, dp_ref, db16_ref, dg_out, db_out = rest
            _ln_bwd_tile(acc, xh_ref, rs_ref, g_ref, ln[3], dp_ref, db16_ref, dg_out, db_out)

    row = pl.BlockSpec((tm, D), lambda i: (i, 0))
    ins = [pl.BlockSpec((3, tm, D), lambda i: (0, i, 0)),
           pl.BlockSpec((N_CHIPS, None, D, C), lambda i: (0, l, 0, 0)), row]
    if ln is None:
        return pl.pallas_call(
            body, name="qkv_proj_bwd", grid=(M // tm,), in_specs=ins,
            out_specs=row, out_shape=jax.ShapeDtypeStruct((M, D), F32),
            compiler_params=_params(("parallel",)),
        )(dqkv, w, dpre)
    xhat, rstd, gam, _, dgam, dbet, ln_row = ln
    specs = _ln_bwd_specs(M, D, tm, ln_row, dgam, dbet)
    return pl.pallas_call(
        body, name="qkv_proj_bwd_ln", grid=(M // tm,), in_specs=ins + specs["in"],
        out_specs=specs["out"], out_shape=specs["out_shape"],
        input_output_aliases={6: 2, 7: 3},
        compiler_params=_params(("arbitrary",)),
    )(dqkv, w, dpre, xhat, rstd, gam, dgam, dbet)


def ffn_dx(dg, du, wgt, wut, l, dpre, alpha, ln=None):
    _, M, Fs = dg.shape
    D = wgt.shape[3]
    tm = _tile(M, 256)

    def body(dg_ref, wg_ref, du_ref, wu_ref, add_ref, *rest):
        acc = alpha * add_ref[...]
        for p in range(N_CHIPS):
            acc = acc + jnp.dot(dg_ref[p], wg_ref[p], preferred_element_type=F32)
            acc = acc + jnp.dot(du_ref[p], wu_ref[p], preferred_element_type=F32)
        if ln is None:
            rest[0][...] = acc
        else:
            xh_ref, rs_ref, g_ref, _dg, _db, dp_ref, db16_ref, dg_out, db_out = rest
            _ln_bwd_tile(acc, xh_ref, rs_ref, g_ref, ln[3], dp_ref, db16_ref, dg_out, db_out)

    act = pl.BlockSpec((N_CHIPS, tm, Fs), lambda i: (0, i, 0))
    wsp = pl.BlockSpec((N_CHIPS, None, Fs, D), lambda i: (0, l, 0, 0))
    row = pl.BlockSpec((tm, D), lambda i: (i, 0))
    if ln is None:
        return pl.pallas_call(
            body, name="ffn_dx", grid=(M // tm,),
            in_specs=[act, wsp, act, wsp, row],
            out_specs=row, out_shape=jax.ShapeDtypeStruct((M, D), F32),
            compiler_params=_params(("parallel",)),
        )(dg, wgt, du, wut, dpre)
    xhat, rstd, gam, _, dgam, dbet, ln_row = ln
    specs = _ln_bwd_specs(M, D, tm, ln_row, dgam, dbet)
    return pl.pallas_call(
        body, name="ffn_dx_ln", grid=(M // tm,),
        in_specs=[act, wsp, act, wsp, row] + specs["in"],
        out_specs=specs["out"], out_shape=specs["out_shape"],
        input_output_aliases={8: 2, 9: 3},
        compiler_params=_params(("arbitrary",)),
    )(dg, wgt, du, wut, dpre, xhat, rstd, gam, dgam, dbet)


def _dw_call(name, buf, l, a, b, a_spec, b_spec, M, tk):
    _, _, R, C = buf.shape
    return _mm_call(
        name, (a, b, buf),
        [a_spec, b_spec, ANY],
        jax.ShapeDtypeStruct(buf.shape, buf.dtype),
        pl.BlockSpec((None, None, R, C), lambda p, k: (l, p, 0, 0)),
        (N_CHIPS, M // tk), [TN_DIMS], (R, C), aliases={2: 0})


def ffn_bwd(gdown, ggate, gup, l, xb, dyb, wd, g, u, h, after):
    M, D = dyb.shape
    Fs = wd.shape[2]
    tm = _tile(M, 512)
    n = M // tm

    def body(x_ref, dy_ref, wd_ref, g_ref, u_ref, h_ref, _gd, _gg, _gu, _after,
             dg_ref, du_ref, gd_ref, gg_ref, gu_ref, acc_d, acc_g, acc_u):
        i = pl.program_id(1)

        @pl.when(i == 0)
        def _():
            acc_d[...] = jnp.zeros_like(acc_d)
            acc_g[...] = jnp.zeros_like(acc_g)
            acc_u[...] = jnp.zeros_like(acc_u)

        dy = dy_ref[...]
        dh = lax.dot_general(dy, wd_ref[...], NT_DIMS, preferred_element_type=F32)
        gf = g_ref[...].astype(F32)
        sig = _sigmoid(gf)
        silu = gf * sig
        dg = (dh * u_ref[...].astype(F32) * (sig * (1.0 + gf - silu))).astype(BF16)
        du = (dh * silu).astype(BF16)
        dg_ref[...] = dg
        du_ref[...] = du
        x = x_ref[...]
        acc_d[...] += lax.dot_general(h_ref[...], dy, TN_DIMS, preferred_element_type=F32)
        acc_g[...] += lax.dot_general(dg, x, TN_DIMS, preferred_element_type=F32)
        acc_u[...] += lax.dot_general(du, x, TN_DIMS, preferred_element_type=F32)

        @pl.when(i == n - 1)
        def _():
            gd_ref[...] = acc_d[...].astype(gd_ref.dtype)
            gg_ref[...] = acc_g[...].astype(gg_ref.dtype)
            gu_ref[...] = acc_u[...].astype(gu_ref.dtype)

    row = pl.BlockSpec((tm, D), lambda p, i: (i, 0))
    act = pl.BlockSpec((None, tm, Fs), lambda p, i: (p, i, 0))
    ash = jax.ShapeDtypeStruct((N_CHIPS, M, Fs), BF16)
    w_blk = pl.BlockSpec((None, None, Fs, D), lambda p, i: (l, p, 0, 0))
    return pl.pallas_call(
        body, name="ffn_bwd", grid=(N_CHIPS, n),
        in_specs=[row, row, pl.BlockSpec((None, None, Fs, D), lambda p, i: (p, l, 0, 0)),
                  act, act, act, ANY, ANY, ANY, pl.BlockSpec(after.shape, lambda p, i: (0, 0))],
        out_specs=[act, act, w_blk, w_blk, w_blk],
        out_shape=[ash, ash] + [jax.ShapeDtypeStruct(b.shape, b.dtype) for b in (gdown, ggate, gup)],
        scratch_shapes=[pltpu.VMEM((Fs, D), F32), pltpu.VMEM((Fs, D), F32), pltpu.VMEM((Fs, D), F32)],
        input_output_aliases={6: 2, 7: 3, 8: 4},
        compiler_params=_params(("parallel", "arbitrary")),
    )(xb, dyb, wd, g, u, h, gdown, ggate, gup, after)


def dw_qkv(buf, l, xb, dqkv):
    M, D = xb.shape
    C = buf.shape[-1]
    tk = _tile(M, 512)
    n = M // tk
    T = math.gcd(D, C)
    nt = 3 * D // T

    def body(x_ref, b_ref, _buf, o_ref, acc_ref):
        k = pl.program_id(0)

        @pl.when(k == 0)
        def _():
            acc_ref[...] = jnp.zeros_like(acc_ref)

        xt = x_ref[...].astype(F32).T.astype(BF16)
        for t in range(nt):
            pa, ca = divmod(t * T, D)
            acc_ref[t] += jnp.dot(xt, b_ref[pa, :, ca:ca + T], preferred_element_type=F32)

        @pl.when(k == n - 1)
        def _():
            for t in range(nt):
                pw, cw = divmod(t * T, C)
                o_ref[pw, :, cw:cw + T] = acc_ref[t].astype(o_ref.dtype)

    return pl.pallas_call(
        body, name="dw_qkv", grid=(n,),
        in_specs=[pl.BlockSpec((tk, D), lambda k: (k, 0)),
                  pl.BlockSpec((3, tk, D), lambda k: (0, k, 0)), ANY],
        out_specs=pl.BlockSpec((None, N_CHIPS, D, C), lambda k: (l, 0, 0, 0)),
        out_shape=jax.ShapeDtypeStruct(buf.shape, buf.dtype),
        scratch_shapes=[pltpu.VMEM((nt, D, T), F32)],
        input_output_aliases={2: 0},
        compiler_params=_params(("arbitrary",)),
    )(xb, dqkv, buf)


def dw_o(buf, l, o, dyb):
    M, D = dyb.shape
    R = buf.shape[2]
    tk = _tile(M, 1024)
    return _dw_call("dw_o", buf, l, o, dyb,
                    pl.BlockSpec((tk, R), lambda p, k: (k, p)),
                    pl.BlockSpec((tk, D), lambda p, k: (k, 0)), M, tk)


def ffn_up(xb, wgt, wut, l):
    M, D = xb.shape
    Fs = wgt.shape[2]
    tm = _tile(M, 512)

    def body(x_ref, wg_ref, wu_ref, g_ref, u_ref, h_ref):
        x = x_ref[...]
        g = lax.dot_general(x, wg_ref[...], NT_DIMS, preferred_element_type=F32)
        u = lax.dot_general(x, wu_ref[...], NT_DIMS, preferred_element_type=F32)
        g_ref[...] = g.astype(BF16)
        u_ref[...] = u.astype(BF16)
        h_ref[...] = (g * _sigmoid(g) * u).astype(BF16)

    wsp = pl.BlockSpec((None, None, Fs, D), lambda p, i: (p, l, 0, 0))
    osp = pl.BlockSpec((None, tm, Fs), lambda p, i: (p, i, 0))
    osh = jax.ShapeDtypeStruct((N_CHIPS, M, Fs), BF16)
    return pl.pallas_call(
        body, name="ffn_up", grid=(N_CHIPS, M // tm),
        in_specs=[pl.BlockSpec((tm, D), lambda p, i: (i, 0)), wsp, wsp],
        out_specs=[osp, osp, osp], out_shape=[osh, osh, osh],
        compiler_params=_params(("parallel", "parallel")),
    )(xb, wgt, wut)


def mm_ln(a, w, l, x, gam, bet, alpha, scale, a_piece_major):
    M, D = x.shape
    R = w.shape[2]
    tm = _tile(M, 256)
    if a_piece_major:
        a_spec = pl.BlockSpec((N_CHIPS, tm, R), lambda i: (0, i, 0))
    else:
        a_spec = pl.BlockSpec((tm, N_CHIPS * R), lambda i: (i, 0))

    def body(a_ref, w_ref, x_ref, g_ref, b_ref, xo_ref, xb_ref, xh_ref, rs_ref):
        y = None
        for p in range(N_CHIPS):
            a = a_ref[p] if a_piece_major else a_ref[:, p * R:(p + 1) * R]
            d = jnp.dot(a, w_ref[p], preferred_element_type=F32)
            y = d if y is None else y + d
        pre = alpha * x_ref[...] + scale * y
        mu = jnp.mean(pre, axis=-1, keepdims=True)
        cen = pre - mu
        var = jnp.mean(cen * cen, axis=-1, keepdims=True)
        rstd = lax.rsqrt(var + LN_EPS)
        xhat = cen * rstd
        out = xhat * g_ref[...] + b_ref[...]
        xo_ref[...] = out
        xb_ref[...] = out.astype(BF16)
        xh_ref[...] = xhat
        rs_ref[...] = rstd

    row = pl.BlockSpec((tm, D), lambda i: (i, 0))
    vec = pl.BlockSpec((1, D), lambda i: (0, 0))
    return pl.pallas_call(
        body, name="mm_ln", grid=(M // tm,),
        in_specs=[a_spec, pl.BlockSpec((N_CHIPS, None, R, D), lambda i: (0, l, 0, 0)), row, vec, vec],
        out_specs=[row, row, row, pl.BlockSpec((tm, 1), lambda i: (i, 0))],
        out_shape=[jax.ShapeDtypeStruct((M, D), F32), jax.ShapeDtypeStruct((M, D), BF16),
                   jax.ShapeDtypeStruct((M, D), F32), jax.ShapeDtypeStruct((M, 1), F32)],
        compiler_params=_params(("parallel",)),
    )(a, w, x, gam, bet)


def ln_bwd(dy, xhat, rstd, gam, scale, dgam, dbet, row):
    M, D = dy.shape
    tm = _tile(M, 512)

    def body(dy_ref, xh_ref, rs_ref, g_ref, _dg, _db, dp_ref, db16_ref, dg_ref, dbt_ref):
        _ln_bwd_tile(dy_ref[...], xh_ref, rs_ref, g_ref, scale, dp_ref, db16_ref, dg_ref, dbt_ref)

    specs = _ln_bwd_specs(M, D, tm, row, dgam, dbet)
    return pl.pallas_call(
        body, name="ln_bwd", grid=(M // tm,),
        in_specs=[pl.BlockSpec((tm, D), lambda i: (i, 0))] + specs["in"],
        out_specs=specs["out"], out_shape=specs["out_shape"],
        input_output_aliases={4: 2, 5: 3},
        compiler_params=_params(("arbitrary",)),
    )(dy, xhat, rstd, gam, dgam, dbet)


def _ln_bwd_tile(dy_v, xh_ref, rs_ref, g_ref, scale, dp_ref, db16_ref, dg_ref, dbt_ref):
    i = pl.program_id(0)
    xh = xh_ref[...]
    dxh = dy_v * g_ref[...]
    m1 = jnp.mean(dxh, axis=-1, keepdims=True)
    m2 = jnp.mean(dxh * xh, axis=-1, keepdims=True)
    dpre = rs_ref[...] * (dxh - m1 - xh * m2)
    dp_ref[...] = dpre
    db16_ref[...] = (scale * dpre).astype(BF16)
    dgp = jnp.sum(dy_v * xh, axis=0, keepdims=True)
    dbp = jnp.sum(dy_v, axis=0, keepdims=True)

    @pl.when(i == 0)
    def _():
        dg_ref[...] = dgp
        dbt_ref[...] = dbp

    @pl.when(i > 0)
    def _():
        dg_ref[...] += dgp
        dbt_ref[...] += dbp


def _ln_bwd_specs(M, D, tm, row, dgam, dbet):
    tok = pl.BlockSpec((tm, D), lambda i: (i, 0))
    vec = pl.BlockSpec((1, D), lambda i: (0, 0))
    acc = pl.BlockSpec((None, 1, D), lambda i: (row, 0, 0))
    return {"in": [tok, pl.BlockSpec((tm, 1), lambda i: (i, 0)), vec, ANY, ANY],
            "out": [tok, tok, acc, acc],
            "out_shape": [jax.ShapeDtypeStruct((M, D), F32), jax.ShapeDtypeStruct((M, D), BF16),
                          jax.ShapeDtypeStruct(dgam.shape, F32), jax.ShapeDtypeStruct(dbet.shape, F32)]}


def loss_head(y, tgt):
    M, D = y.shape
    tm = _tile(M, 512)
    n = M // tm

    def body(y_ref, t_ref, dy_ref, l_ref, acc_ref):
        i = pl.program_id(0)
        e = y_ref[...] - t_ref[...]
        dy_ref[...] = e * (1.0 / D)
        part = jnp.sum(e * e, axis=0, keepdims=True)

        @pl.when(i == 0)
        def _():
            acc_ref[...] = part

        @pl.when(i > 0)
        def _():
            acc_ref[...] += part

        @pl.when(i == n - 1)
        def _():
            l_ref[...] = (0.5 / D) * jnp.sum(acc_ref[...], axis=1, keepdims=True)

    row = pl.BlockSpec((tm, D), lambda i: (i, 0))
    return pl.pallas_call(
        body, name="loss_head", grid=(n,),
        in_specs=[row, row],
        out_specs=[row, pl.BlockSpec((1, 1), lambda i: (0, 0))],
        out_shape=[jax.ShapeDtypeStruct((M, D), F32), jax.ShapeDtypeStruct((1, 1), F32)],
        scratch_shapes=[pltpu.VMEM((1, D), F32)],
        compiler_params=_params(("arbitrary",)),
    )(y, tgt)


def _rel_onehot_t():
    r = lax.broadcasted_iota(jnp.int32, (REL_PAD, VR_W), 0)
    n = lax.broadcasted_iota(jnp.int32, (REL_PAD, VR_W), 1)
    idx = jnp.clip(VR_C0 - n, -REL_CLIP, REL_CLIP) + REL_CLIP
    return (r == idx).astype(F32)


def bias_vec(tab_t):
    H = tab_t.shape[0]

    def body(t_ref, o_ref):
        o_ref[...] = jnp.dot(t_ref[...], _rel_onehot_t(), precision=lax.Precision.HIGHEST,
                             preferred_element_type=F32)

    return pl.pallas_call(
        body, name="bias_vec", out_shape=jax.ShapeDtypeStruct((H, VR_W), F32),
        compiler_params=_params(),
    )(tab_t)


def bias_vec_bwd(dvr):
    n, H, _ = dvr.shape

    def body(d_ref, o_ref):
        tot = d_ref[0]
        for i in range(1, n):
            tot = tot + d_ref[i]
        o_ref[...] = lax.dot_general(tot, _rel_onehot_t(), NT_DIMS, precision=lax.Precision.HIGHEST,
                                     preferred_element_type=F32)

    return pl.pallas_call(
        body, name="bias_vec_bwd", out_shape=jax.ShapeDtypeStruct((H, REL_PAD), F32),
        compiler_params=_params(),
    )(dvr)


def _a_bias_mask(vr_row):
    xb = jnp.broadcast_to(vr_row, (QB_A, VR_W))
    tile = pltpu.roll(xb, VR_W - (QB_A - 1), 1, stride=1, stride_axis=0)[:, :KW_A]
    qc = lax.broadcasted_iota(jnp.int32, (QB_A, KW_A), 0) // CHUNK
    kc = lax.broadcasted_iota(jnp.int32, (QB_A, KW_A), 1) // CHUNK
    valid = (kc >= qc) & (kc <= qc + LEFT_CHUNKS)
    return jnp.where(valid, tile, NEG)


def _a_diag_sums(db_acc, h):
    acc8 = None
    for a in range(QB_A // 8):
        grp = db_acc[h, 8 * a:8 * a + 8, :]
        shift = QB_A - 8 - 8 * a
        if shift:
            grp = pltpu.roll(grp, shift, 1)
        acc8 = grp if acc8 is None else acc8 + grp
    sub = lax.broadcasted_iota(jnp.int32, (8, VR_W), 0)
    tot = jnp.zeros((8, VR_W), F32)
    for b in range(8):
        moved = pltpu.roll(acc8, 7 - b, 1) if b < 7 else acc8
        tot = tot + jnp.where(sub == b, moved, 0.0)
    return jnp.sum(tot, axis=0, keepdims=True)


def _a_blocks(S):
    out = []
    for qi in range(S // QB_A):
        q0 = qi * QB_A
        ks = max(0, q0 - LOOKBACK)
        out.append((q0, ks, q0 + QB_A, ks - (q0 - LOOKBACK)))
    return out


def _head_specs(S, HP):
    q = pl.BlockSpec((S, 2 * HEAD_DIM), lambda b, hp: (b, hp))
    k = pl.BlockSpec((S, 2 * HEAD_DIM), lambda b, hp: (b, HP + hp))
    v = pl.BlockSpec((S, 2 * HEAD_DIM), lambda b, hp: (b, 2 * HP + hp))
    return q, k, v


def attn_a_fwd(qkv, vr, B, S):
    D = qkv.shape[1] // 3
    HP = D // (2 * HEAD_DIM)
    scale = HEAD_DIM ** -0.5
    blocks = _a_blocks(S)

    def body(q_ref, k_ref, v_ref, vr_ref, o_ref):
        heads = [slice(h * HEAD_DIM, (h + 1) * HEAD_DIM) for h in range(2)]
        bms = [_a_bias_mask(vr_ref[h:h + 1, :]) for h in range(2)]
        for pair in range(0, len(blocks), 2):
            chains = [(slice(q0, q0 + QB_A), ks, ke, joff, hs, bm)
                      for (q0, ks, ke, joff) in blocks[pair:pair + 2] for hs, bm in zip(heads, bms)]
            ss = [lax.dot_general(q_ref[rows, hs] * scale, k_ref[ks:ke, hs], NT_DIMS,
                                  preferred_element_type=F32) + bm[:, joff:]
                  for rows, ks, ke, joff, hs, bm in chains]
            ps = [jnp.exp(s - jnp.max(s, axis=-1, keepdims=True)) for s in ss]
            for (rows, ks, ke, _, hs, _), p in zip(chains, ps):
                den = jnp.sum(p, axis=-1, keepdims=True)
                o = jnp.dot(p.astype(BF16), v_ref[ks:ke, hs], preferred_element_type=F32) / den
                o_ref[rows, hs] = o.astype(BF16)

    qs, ks_, vs = _head_specs(S, HP)
    return pl.pallas_call(
        body, name="attn_a_fwd", grid=(B, HP),
        in_specs=[qs, ks_, vs, pl.BlockSpec((None, 2, VR_W), lambda b, hp: (hp, 0, 0))],
        out_specs=pl.BlockSpec((S, 2 * HEAD_DIM), lambda b, hp: (b, hp)),
        out_shape=jax.ShapeDtypeStruct((B * S, D), BF16),
        compiler_params=_params(("parallel", "parallel")),
    )(qkv, qkv, qkv, vr)


def attn_a_bwd(qkv, vr, do, B, S):
    D = qkv.shape[1] // 3
    HP = D // (2 * HEAD_DIM)
    scale = HEAD_DIM ** -0.5
    blocks = _a_blocks(S)

    def body(q_ref, k_ref, v_ref, vr_ref, do_ref, dqkv_ref, dvr_ref, dkt_acc, dvt_acc, db_acc):
        dkt_acc[...] = jnp.zeros_like(dkt_acc)
        dvt_acc[...] = jnp.zeros_like(dvt_acc)
        db_acc[...] = jnp.zeros_like(db_acc)
        heads = [slice(h * HEAD_DIM, (h + 1) * HEAD_DIM) for h in range(2)]
        bms = [_a_bias_mask(vr_ref[h:h + 1, :]) for h in range(2)]
        for pair in range(0, len(blocks), 2):
            chains, qts, dots = [], [], []
            for (q0, ks, ke, joff) in blocks[pair:pair + 2]:
                rows = slice(q0, q0 + QB_A)
                qt_pair = (q_ref[rows, :] * scale).astype(F32).T.astype(BF16)
                dot_pair = do_ref[rows, :].astype(F32).T.astype(BF16)
                for h, hs in enumerate(heads):
                    chains.append((rows, ks, ke, joff, h, hs))
                    qts.append(qt_pair[hs, :])
                    dots.append(dot_pair[hs, :])
            ss = [lax.dot_general(q_ref[rows, hs] * scale, k_ref[ks:ke, hs], NT_DIMS,
                                  preferred_element_type=F32) + bms[h][:, joff:]
                  for rows, ks, ke, joff, h, hs in chains]
            dps = [lax.dot_general(do_ref[rows, hs], v_ref[ks:ke, hs], NT_DIMS, preferred_element_type=F32)
                   for rows, ks, ke, _, _, hs in chains]
            ps, dsbs = [], []
            for (_, _, _, joff, h, _), s, dp in zip(chains, ss, dps):
                e = jnp.exp(s - jnp.max(s, axis=-1, keepdims=True))
                p = e / jnp.sum(e, axis=-1, keepdims=True)
                ds = p * (dp - jnp.sum(p * dp, axis=-1, keepdims=True))
                db_acc[h, :, joff:KW_A] += ds
                ps.append(p.astype(BF16))
                dsbs.append(ds.astype(BF16))
            for (rows, ks, ke, _, _, hs), p, dsb, qt, dot_ in zip(chains, ps, dsbs, qts, dots):
                dq = jnp.dot(dsb, k_ref[ks:ke, hs], preferred_element_type=F32) * scale
                dqkv_ref[0, rows, hs] = dq.astype(BF16)
                dkt_acc[hs, ks:ke] += jnp.dot(qt, dsb, preferred_element_type=F32)
                dvt_acc[hs, ks:ke] += jnp.dot(dot_, p, preferred_element_type=F32)
        for h in range(2):
            dvr_ref[h:h + 1, :] = _a_diag_sums(db_acc, h)
        dqkv_ref[1] = dkt_acc[...].T.astype(BF16)
        dqkv_ref[2] = dvt_acc[...].T.astype(BF16)

    qs, ks_, vs = _head_specs(S, HP)
    hd = pl.BlockSpec((S, 2 * HEAD_DIM), lambda b, hp: (b, hp))
    return pl.pallas_call(
        body, name="attn_a_bwd", grid=(B, HP),
        in_specs=[qs, ks_, vs, pl.BlockSpec((None, 2, VR_W), lambda b, hp: (hp, 0, 0)), hd],
        out_specs=[pl.BlockSpec((3, S, 2 * HEAD_DIM), lambda b, hp: (0, b, hp)),
                   pl.BlockSpec((None, None, 2, VR_W), lambda b, hp: (b, hp, 0, 0))],
        out_shape=[jax.ShapeDtypeStruct((3, B * S, D), BF16), jax.ShapeDtypeStruct((B, HP, 2, VR_W), F32)],
        scratch_shapes=[pltpu.VMEM((2 * HEAD_DIM, S), F32), pltpu.VMEM((2 * HEAD_DIM, S), F32),
                        pltpu.VMEM((2, QB_A, VR_W), F32)],
        compiler_params=_params(("parallel", "parallel")),
    )(qkv, qkv, qkv, vr, do)


def _tri(cmp):
    j = lax.broadcasted_iota(jnp.int32, (SB_TILE, SB_TILE), 0)
    s = lax.broadcasted_iota(jnp.int32, (SB_TILE, SB_TILE), 1)
    return cmp(j, s).astype(BF16)


def _cumsum_mm(x, tri):
    hi = x.astype(BF16)
    mid = (x - hi.astype(F32)).astype(BF16)
    return jnp.dot(hi, tri, preferred_element_type=F32) + jnp.dot(mid, tri, preferred_element_type=F32)


def _sb_logs(q, k, diagonal):
    z = lax.dot_general(q, k, NT_DIMS, preferred_element_type=F32)
    log_b = jnp.minimum(z, 0.0) - jnp.log(1.0 + jnp.exp(-jnp.abs(z)))
    log_1mb = log_b - z
    if not diagonal:
        return log_b, log_1mb, None
    row = lax.broadcasted_iota(jnp.int32, (SB_TILE, SB_TILE), 0)
    col = lax.broadcasted_iota(jnp.int32, (SB_TILE, SB_TILE), 1)
    causal = col < row
    return log_b, jnp.where(causal, log_1mb, 0.0), causal


def attn_b_fwd(qkv, B, S):
    D = qkv.shape[1] // 3
    HP = D // (2 * HEAD_DIM)
    scale = HEAD_DIM ** -0.5
    nb = S // SB_TILE

    def body(q_ref, k_ref, v_ref, o_ref, nt_ref):
        tri = _tri(lambda j, s: j > s)
        heads = [slice(h * HEAD_DIM, (h + 1) * HEAD_DIM) for h in range(2)]

        def q_block(qb, n_pairs, parity):
            q0 = pl.multiple_of(qb * SB_TILE, SB_TILE)
            rows = pl.ds(q0, SB_TILE)
            qs = [q_ref[rows, hs] * scale for hs in heads]

            def step(blocks, state):
                chains = [(h, kb, diagonal, pl.ds(pl.multiple_of(kb * SB_TILE, SB_TILE), SB_TILE))
                          for h in range(2) for kb, diagonal in blocks]
                logs = [_sb_logs(qs[h], k_ref[keys, heads[h]], diagonal)
                        for h, _, diagonal, keys in chains]
                sums = [_cumsum_mm(log_1mb, tri) for _, log_1mb, _ in logs]
                rights = [state[0][0], state[1][0]]
                accs = [state[0][1], state[1][1]]
                for (h, _, diagonal, keys), (log_b, log_1mb, causal), csum in zip(chains, logs, sums):
                    a = jnp.exp(log_b + csum + rights[h])
                    if diagonal:
                        a = jnp.where(causal, a, 0.0)
                    accs[h] = accs[h] + jnp.dot(a.astype(BF16), v_ref[keys, heads[h]],
                                                preferred_element_type=F32)
                    rights[h] = rights[h] + jnp.sum(log_1mb, axis=-1, keepdims=True)
                return ((rights[0], accs[0]), (rights[1], accs[1]))

            zero = (jnp.zeros((SB_TILE, 1), F32), jnp.zeros((SB_TILE, HEAD_DIM), F32))
            first = [(qb, True)] + ([(qb - 1, False)] if parity else [])
            top = qb - len(first)
            state = lax.fori_loop(
                0, n_pairs, lambda t, st: step([(top - 2 * t, False), (top - 2 * t - 1, False)], st),
                step(first, (zero, zero)))
            for hs, (right, acc) in zip(heads, state):
                o_ref[rows, hs] = acc.astype(BF16)
                nt_ref[rows, hs] = jnp.broadcast_to(right, (SB_TILE, HEAD_DIM))

        def q_pair_loop(j, carry):
            q_block(2 * j, j, 0)
            q_block(2 * j + 1, j, 1)
            return carry

        lax.fori_loop(0, nb // 2, q_pair_loop, 0)

    qs, ks_, vs = _head_specs(S, HP)
    hd = pl.BlockSpec((S, 2 * HEAD_DIM), lambda b, hp: (b, hp))
    return pl.pallas_call(
        body, name="attn_b_fwd", grid=(B, HP),
        in_specs=[qs, ks_, vs], out_specs=[hd, hd],
        out_shape=[jax.ShapeDtypeStruct((B * S, D), BF16), jax.ShapeDtypeStruct((B * S, D), F32)],
        compiler_params=_params(("parallel", "parallel")),
    )(qkv, qkv, qkv)


def attn_b_bwd(qkv, do, ntot, B, S):
    D = qkv.shape[1] // 3
    HP = D // (2 * HEAD_DIM)
    scale = HEAD_DIM ** -0.5
    nb = S // SB_TILE

    def body(q_ref, k_ref, v_ref, do_ref, nt_ref, dqkv_ref, dkt_acc, dvt_acc):
        tri_incl = _tri(lambda j, s: j <= s)
        tri_excl = _tri(lambda j, s: j < s)
        heads = [slice(h * HEAD_DIM, (h + 1) * HEAD_DIM) for h in range(2)]
        dkt_acc[...] = jnp.zeros_like(dkt_acc)
        dvt_acc[...] = jnp.zeros_like(dvt_acc)

        def q_block(qb, n_pairs, parity):
            q0 = pl.multiple_of(qb * SB_TILE, SB_TILE)
            rows = pl.ds(q0, SB_TILE)
            qt_pair = (q_ref[rows, :] * scale).astype(F32).T.astype(BF16)
            dot_pair = do_ref[rows, :].astype(F32).T.astype(BF16)
            per_head = [(hs, q_ref[rows, hs] * scale, do_ref[rows, hs], qt_pair[hs, :], dot_pair[hs, :],
                         nt_ref[rows, hs.start:hs.start + 1]) for hs in heads]

            def step(blocks, state):
                chains = [(h, kb, diagonal, pl.ds(pl.multiple_of(kb * SB_TILE, SB_TILE), SB_TILE))
                          for h in range(2) for kb, diagonal in blocks]
                ks = [k_ref[keys, per_head[h][0]] for h, _, _, keys in chains]
                logs = [_sb_logs(per_head[h][1], k, diagonal)
                        for (h, _, diagonal, _), k in zip(chains, ks)]
                das = [lax.dot_general(per_head[h][2], v_ref[keys, per_head[h][0]], NT_DIMS,
                                       preferred_element_type=F32) for h, _, _, keys in chains]
                sums = [_cumsum_mm(log_1mb, tri_incl) for _, log_1mb, _ in logs]
                left_n = [state[0][0], state[1][0]]
                left_d = [state[0][1], state[1][1]]
                dq_acc = [state[0][2], state[1][2]]
                a_s, dls = [], []
                for (h, _, diagonal, _), (log_b, log_1mb, causal), csum, da in zip(chains, logs, sums, das):
                    a = jnp.exp(log_b + (per_head[h][5] - left_n[h]) - csum)
                    if diagonal:
                        a = jnp.where(causal, a, 0.0)
                    a_s.append(a)
                    dls.append(a * da)
                    left_n[h] = left_n[h] + jnp.sum(log_1mb, axis=-1, keepdims=True)
                dsums = [_cumsum_mm(dl, tri_excl) for dl in dls]
                dzbs = []
                for (h, _, diagonal, _), (log_b, log_1mb, causal), dl, dsum in zip(chains, logs, dls, dsums):
                    dz = dl * jnp.exp(log_1mb) - (left_d[h] + dsum) * jnp.exp(log_b)
                    if diagonal:
                        dz = jnp.where(causal, dz, 0.0)
                    dzbs.append(dz.astype(BF16))
                    left_d[h] = left_d[h] + jnp.sum(dl, axis=-1, keepdims=True)
                for (h, kb, _, _), k, a, dzb in zip(chains, ks, a_s, dzbs):
                    hs, _, _, qt, dot_, _ = per_head[h]
                    dq_acc[h] = dq_acc[h] + jnp.dot(dzb, k, preferred_element_type=F32)
                    dkt_acc[kb, hs, :] += jnp.dot(qt, dzb, preferred_element_type=F32)
                    dvt_acc[kb, hs, :] += jnp.dot(dot_, a.astype(BF16), preferred_element_type=F32)
                return ((left_n[0], left_d[0], dq_acc[0]), (left_n[1], left_d[1], dq_acc[1]))

            zero1 = jnp.zeros((SB_TILE, 1), F32)
            zero = (zero1, zero1, jnp.zeros((SB_TILE, HEAD_DIM), F32))
            state = lax.fori_loop(
                0, n_pairs, lambda t, st: step([(2 * t, False), (2 * t + 1, False)], st), (zero, zero))
            last = ([(qb - 1, False)] if parity else []) + [(qb, True)]
            state = step(last, state)
            for hs, (_, _, dq_acc) in zip(heads, state):
                dqkv_ref[0, rows, hs] = (dq_acc * scale).astype(BF16)

        def q_pair_loop(j, carry):
            q_block(2 * j, j, 0)
            q_block(2 * j + 1, j, 1)
            return carry

        lax.fori_loop(0, nb // 2, q_pair_loop, 0)
        for kb in range(nb):
            dqkv_ref[1, kb * SB_TILE:(kb + 1) * SB_TILE, :] = dkt_acc[kb].T.astype(BF16)
            dqkv_ref[2, kb * SB_TILE:(kb + 1) * SB_TILE, :] = dvt_acc[kb].T.astype(BF16)

    qs, ks_, vs = _head_specs(S, HP)
    hd = pl.BlockSpec((S, 2 * HEAD_DIM), lambda b, hp: (b, hp))
    acc = pltpu.VMEM((nb, 2 * HEAD_DIM, SB_TILE), F32)
    return pl.pallas_call(
        body, name="attn_b_bwd", grid=(B, HP),
        in_specs=[qs, ks_, vs, hd, hd],
        out_specs=pl.BlockSpec((3, S, 2 * HEAD_DIM), lambda b, hp: (0, b, hp)),
        out_shape=jax.ShapeDtypeStruct((3, B * S, D), BF16),
        scratch_shapes=[acc, acc],
        compiler_params=_params(("parallel", "parallel")),
    )(qkv, qkv, qkv, do, ntot)


def _place():
    x, y, c = lax.axis_index("x"), lax.axis_index("y"), lax.axis_index("c")
    chips = [(1 - x, y), (x, 1 - y), (1 - x, 1 - y)]
    return x, y, c, 2 * x + y, chips


def _remote(src, dst, send_sem, recv_sem, dev):
    return pltpu.make_async_remote_copy(src_ref=src, dst_ref=dst, send_sem=send_sem, recv_sem=recv_sem,
                                        device_id=dev, device_id_type=MESH)


def gather_weights(shards):
    n = len(shards)

    def body(*refs):
        ins, outs = refs[:n], refs[n:2 * n]
        send1, recv1, send2, recv2, lsem = refs[2 * n:]
        x, y, c, me, chips = _place()
        local, first = [], []
        for f in range(n):
            hl = shards[f].shape[0] // 2
            cp = pltpu.make_async_copy(ins[f], outs[f].at[me], lsem.at[f])
            cp.start()
            local.append(cp)
            for j, (qx, qy) in enumerate(chips):
                half = pl.ds(c * hl, hl)
                cp = _remote(ins[f].at[half], outs[f].at[me, half],
                             send1.at[3 * f + j], recv1.at[3 * f + j], (qx, qy, c))
                cp.start()
                first.append(cp)
        passed = []
        for f in range(n):
            hl = shards[f].shape[0] // 2
            for j, (qx, qy) in enumerate(chips):
                slab = outs[f].at[2 * qx + qy, pl.ds(c * hl, hl)]
                _remote(slab, slab, send1.at[3 * f + j], recv1.at[3 * f + j], (x, y, c)).wait_recv()
                cp = _remote(slab, slab, send2.at[3 * f + j], recv2.at[3 * f + j], (x, y, 1 - c))
                cp.start()
                passed.append(cp)
        for f in range(n):
            hl = shards[f].shape[0] // 2
            for j, (qx, qy) in enumerate(chips):
                slab = outs[f].at[2 * qx + qy, pl.ds((1 - c) * hl, hl)]
                _remote(slab, slab, send2.at[3 * f + j], recv2.at[3 * f + j], (x, y, c)).wait_recv()
        for cp in first + passed:
            cp.wait_send()
        for cp in local:
            cp.wait()

    sems = pltpu.SemaphoreType.DMA((3 * n,))
    return pl.pallas_call(
        body, name="gather_weights",
        in_specs=[ANY] * n, out_specs=[ANY] * n,
        out_shape=[jax.ShapeDtypeStruct((N_CHIPS,) + s.shape, s.dtype) for s in shards],
        scratch_shapes=[sems, sems, sems, sems, pltpu.SemaphoreType.DMA((n,))],
        compiler_params=pltpu.CompilerParams(has_side_effects=True),
    )(*shards)


HBM = pl.BlockSpec(memory_space=pltpu.HBM)
SEM = pl.BlockSpec(memory_space=pltpu.SEMAPHORE)
EFFECT = pltpu.SideEffectType.DATAFLOW_SIDE_EFFECTING


def _in_hbm(a):
    return pltpu.with_memory_space_constraint(a, pltpu.HBM)


def _row_half(ref, _unused, which):
    hr = ref.shape[-2] // 2
    idx = [pl.ds(0, d) for d in ref.shape[:-2]] + [pl.ds(which * hr, hr), pl.ds(0, ref.shape[-1])]
    return ref.at[tuple(idx)]


def cast_place(ids, w, lead, L):
    R, C = w.shape[-2:]

    def body(ids_ref, w_ref, s_ref, land_ref):
        v = w_ref[...].astype(BF16)
        s_ref[...] = v
        land_ref[...] = v

    return pl.pallas_call(
        body, name="cast_place",
        grid_spec=pltpu.PrefetchScalarGridSpec(
            num_scalar_prefetch=1, grid=(L,),
            in_specs=[pl.BlockSpec((None,) * (w.ndim - 2) + (R, C), lambda l, ids: (*lead(l), 0, 0))],
            out_specs=[pl.BlockSpec((None, R, C), lambda l, ids: (l, 0, 0)),
                       pl.BlockSpec((None, None, R, C), lambda l, ids: (ids[1], l, 0, 0))]),
        out_shape=[jax.ShapeDtypeStruct((L, R, C), BF16), jax.ShapeDtypeStruct((N_CHIPS, L, R, C), BF16)],
        compiler_params=_params(("parallel",)),
    )(ids, w)


def gather_start(layers, zones, after):
    flat = [s for lay in layers for s in lay]
    flat_zones = [z for lay in zones for z in lay]
    counts = [len(lay) for lay in layers]
    n, nl = len(flat), len(layers)

    def body(*refs):
        ins, lands = refs[:n], refs[n:2 * n]
        send, recv = refs[2 * n + 1:2 * n + 1 + nl], refs[2 * n + 1 + nl:2 * n + 1 + 2 * nl]
        token = refs[-1]
        x, y, c, me, chips = _place()
        f = 0
        for li, cnt in enumerate(counts):
            for k in range(cnt):
                for j, (qx, qy) in enumerate(chips):
                    _remote(_row_half(ins[f], 0, c), _row_half(lands[f].at[me], 0, c),
                            send[li].at[3 * k + j], recv[li].at[3 * k + j], (qx, qy, c)).start()
                f += 1
        token[...] = jnp.zeros_like(token)

    sem_shapes = [pltpu.SemaphoreType.DMA((3 * cnt,)) for cnt in counts]
    land_shapes = [(N_CHIPS,) + s.shape for s in flat]
    res = pl.pallas_call(
        body, name="gather_start",
        out_shape=(*sem_shapes, *sem_shapes,
                   *[pltpu.HBM(s.shape, s.dtype) for s in flat],
                   *[pltpu.HBM(shp, s.dtype) for shp, s in zip(land_shapes, flat)],
                   jax.ShapeDtypeStruct((8, 128), F32)),
        in_specs=[HBM] * (2 * n) + [ANY],
        out_specs=(*[SEM] * (2 * nl), *[HBM] * (2 * n), pl.BlockSpec(memory_space=pltpu.VMEM)),
        input_output_aliases={k: 2 * nl + k for k in range(2 * n)},
        compiler_params=pltpu.CompilerParams(has_side_effects=EFFECT),
    )(*[_in_hbm(s) for s in flat], *[_in_hbm(z) for z in flat_zones], after)
    send, recv = res[:nl], res[nl:2 * nl]
    thru, lands, token = res[2 * nl:2 * nl + n], res[2 * nl + n:2 * nl + 2 * n], res[-1]
    out, f = [], 0
    for li, cnt in enumerate(counts):
        out.append((send[li], recv[li], list(thru[f:f + cnt]), list(lands[f:f + cnt])))
        f += cnt
    return out, token


def gather_wait(li, send, recv, shards, lands, after):
    m = len(shards)

    def body(*refs):
        ins, lnd = refs[:m], refs[m:2 * m]
        snd, rcv = refs[2 * m], refs[2 * m + 1]
        x, y, c, me, chips = _place()
        for k in range(m):
            for j, (qx, qy) in enumerate(chips):
                cp = _remote(_row_half(ins[k], 0, c), _row_half(lnd[k].at[2 * qx + qy], 0, c),
                             snd.at[3 * k + j], rcv.at[3 * k + j], (qx, qy, c))
                cp.wait_send()
                cp.wait_recv()

    res = pl.pallas_call(
        body, name=f"gather_wait_{li}",
        out_shape=(*[pltpu.HBM(s.shape, s.dtype) for s in shards],
                   *[pltpu.HBM(s.shape, s.dtype) for s in lands]),
        in_specs=[HBM] * (2 * m) + [SEM, SEM, ANY], out_specs=[HBM] * (2 * m),
        input_output_aliases={k: k for k in range(2 * m)},
        compiler_params=pltpu.CompilerParams(has_side_effects=EFFECT),
    )(*shards, *lands, send, recv, after)
    return list(res[:m]), list(res[m:])


def gather_forward(li, lands):
    m = len(lands)

    def body(*refs):
        outs = refs[m:2 * m]
        send, recv = refs[2 * m:]
        x, y, c, me, chips = _place()
        passed = []
        for k in range(m):
            for j, (qx, qy) in enumerate(chips):
                slab = _row_half(outs[k].at[2 * qx + qy], 0, c)
                cp = _remote(slab, slab, send.at[3 * k + j], recv.at[3 * k + j], (x, y, 1 - c))
                cp.start()
                passed.append(cp)
        for k in range(m):
            for j, (qx, qy) in enumerate(chips):
                slab = _row_half(outs[k].at[2 * qx + qy], 0, 1 - c)
                _remote(slab, slab, send.at[3 * k + j], recv.at[3 * k + j], (x, y, c)).wait_recv()
        for cp in passed:
            cp.wait_send()

    sems = pltpu.SemaphoreType.DMA((3 * m,))
    return pl.pallas_call(
        body, name=f"gather_forward_{li}",
        in_specs=[ANY] * m, out_specs=[ANY] * m,
        out_shape=[jax.ShapeDtypeStruct(s.shape, s.dtype) for s in lands],
        input_output_aliases={k: k for k in range(m)},
        scratch_shapes=[sems, sems],
        compiler_params=pltpu.CompilerParams(has_side_effects=True),
    )(*lands)


def pair_exchange_rows(li, grads):
    n = len(grads)

    def body(*refs):
        ins, outs = refs[:n], refs[n:2 * n]
        send, recv = refs[2 * n:]
        x, y, c, _, _ = _place()
        cps = []
        for k in range(n):
            cp = _remote(_row_half(ins[k], 0, 1 - c), outs[k], send.at[k], recv.at[k], (x, y, 1 - c))
            cp.start()
            cps.append(cp)
        for cp in cps:
            cp.wait()

    sems = pltpu.SemaphoreType.DMA((n,))
    return pl.pallas_call(
        body, name=f"pair_exchange_{li}",
        in_specs=[ANY] * n, out_specs=[ANY] * n,
        out_shape=[jax.ShapeDtypeStruct(g.shape[:-2] + (g.shape[-2] // 2, g.shape[-1]), g.dtype)
                   for g in grads],
        scratch_shapes=[sems, sems],
        compiler_params=pltpu.CompilerParams(has_side_effects=True),
    )(*grads)


def pair_add_rows(ids, grad, recv):
    L, P, hr, C = recv.shape

    def body(ids_ref, a_ref, b_ref, o_ref):
        o_ref[...] = (a_ref[...].astype(F32) + b_ref[...].astype(F32)).astype(o_ref.dtype)

    blk = (None, None, hr, C)
    return pl.pallas_call(
        body, name="pair_add",
        grid_spec=pltpu.PrefetchScalarGridSpec(
            num_scalar_prefetch=1, grid=(L, P),
            in_specs=[pl.BlockSpec(blk, lambda l, p, ids: (l, p, ids[0], 0)),
                      pl.BlockSpec(blk, lambda l, p, ids: (l, p, 0, 0))],
            out_specs=pl.BlockSpec(blk, lambda l, p, ids: (l, p, 0, 0))),
        out_shape=jax.ShapeDtypeStruct(recv.shape, recv.dtype),
        compiler_params=_params(("parallel", "parallel")),
    )(ids, grad, recv)


def reduce_start(li, parts):
    m = len(parts)

    def body(*refs):
        ins, lands = refs[:m], refs[m:2 * m]
        send, recv = refs[2 * m], refs[2 * m + 1]
        token = refs[-1]
        x, y, c, me, chips = _place()
        for k in range(m):
            rows = pl.ds(0, parts[k].shape[0])
            for j, (qx, qy) in enumerate(chips):
                _remote(ins[k].at[rows, 2 * qx + qy], lands[k].at[rows, me],
                        send.at[3 * k + j], recv.at[3 * k + j], (qx, qy, c)).start()
        token[...] = jnp.zeros_like(token)

    sems = pltpu.SemaphoreType.DMA((3 * m,))
    res = pl.pallas_call(
        body, name=f"reduce_start_{li}",
        out_shape=(sems, sems, *[pltpu.HBM(s.shape, s.dtype) for s in parts],
                   *[pltpu.HBM(s.shape, s.dtype) for s in parts], jax.ShapeDtypeStruct((8, 128), F32)),
        in_specs=[HBM] * (2 * m),
        out_specs=(SEM, SEM, *[HBM] * (2 * m), pl.BlockSpec(memory_space=pltpu.VMEM)),
        input_output_aliases={k: 2 + k for k in range(2 * m)},
        compiler_params=pltpu.CompilerParams(has_side_effects=EFFECT),
    )(*[_in_hbm(s) for s in parts], *[_in_hbm(lax.empty(s.shape, s.dtype)) for s in parts])
    return res[0], res[1], list(res[2:2 + m]), list(res[2 + m:2 + 2 * m]), res[-1]


def reduce_wait(li, send, recv, parts, lands, after):
    m = len(parts)

    def body(*refs):
        ins, lnd = refs[:m], refs[m:2 * m]
        snd, rcv = refs[2 * m], refs[2 * m + 1]
        x, y, c, me, chips = _place()
        for k in range(m):
            rows = pl.ds(0, parts[k].shape[0])
            for j, (qx, qy) in enumerate(chips):
                cp = _remote(ins[k].at[rows, 2 * qx + qy], lnd[k].at[rows, 2 * qx + qy],
                             snd.at[3 * k + j], rcv.at[3 * k + j], (qx, qy, c))
                cp.wait_send()
                cp.wait_recv()

    res = pl.pallas_call(
        body, name=f"reduce_wait_{li}",
        out_shape=(*[pltpu.HBM(s.shape, s.dtype) for s in parts],
                   *[pltpu.HBM(s.shape, s.dtype) for s in lands]),
        in_specs=[HBM] * (2 * m) + [SEM, SEM, ANY], out_specs=[HBM] * (2 * m),
        input_output_aliases={k: k for k in range(2 * m)},
        compiler_params=pltpu.CompilerParams(has_side_effects=EFFECT),
    )(*parts, *lands, send, recv, after)
    return list(res[:m]), list(res[m:])


def chip_sum_rows(ids, land, part, gfull, l0):
    L, P, hr, C = land.shape

    def body(ids_ref, land_ref, part_ref, _g, o_ref):
        tot = None
        for q in range(P):
            term = jnp.where(ids_ref[1] == q, part_ref[...], land_ref[q]).astype(F32)
            tot = term if tot is None else tot + term
        o_ref[...] = tot

    return pl.pallas_call(
        body, name="chip_sum",
        grid_spec=pltpu.PrefetchScalarGridSpec(
            num_scalar_prefetch=1, grid=(L,),
            in_specs=[pl.BlockSpec((None, P, hr, C), lambda l, ids: (l, 0, 0, 0)),
                      pl.BlockSpec((None, None, hr, C), lambda l, ids: (l, ids[1], 0, 0)), ANY],
            out_specs=pl.BlockSpec((None, hr, C), lambda l, ids: (l0 + l, ids[0], 0))),
        out_shape=jax.ShapeDtypeStruct(gfull.shape, gfull.dtype),
        input_output_aliases={3: 0},
        compiler_params=_params(("arbitrary",)),
    )(ids, land, part, gfull)


def half_swap_rows(grads):
    n = len(grads)

    def body(*refs):
        outs = refs[n:2 * n]
        send, recv = refs[2 * n:]
        x, y, c, _, _ = _place()
        cps = []
        for k in range(n):
            mine = _row_half(outs[k], 0, c)
            cp = _remote(mine, mine, send.at[k], recv.at[k], (x, y, 1 - c))
            cp.start()
            cps.append(cp)
        for k in range(n):
            theirs = _row_half(outs[k], 0, 1 - c)
            _remote(theirs, theirs, send.at[k], recv.at[k], (x, y, c)).wait_recv()
        for cp in cps:
            cp.wait_send()

    sems = pltpu.SemaphoreType.DMA((n,))
    return pl.pallas_call(
        body, name="half_swap",
        in_specs=[ANY] * n, out_specs=[ANY] * n,
        out_shape=[jax.ShapeDtypeStruct(g.shape, g.dtype) for g in grads],
        input_output_aliases={k: k for k in range(n)},
        scratch_shapes=[sems, sems],
        compiler_params=pltpu.CompilerParams(has_side_effects=True),
    )(*grads)


def pair_exchange(grads):
    n = len(grads)

    def body(*refs):
        ins, outs = refs[:n], refs[n:2 * n]
        send, recv = refs[2 * n:]
        x, y, c, _, _ = _place()
        cps = []
        for f in range(n):
            hl = grads[f].shape[0] // 2
            cp = _remote(ins[f].at[pl.ds((1 - c) * hl, hl)], outs[f], send.at[f], recv.at[f], (x, y, 1 - c))
            cp.start()
            cps.append(cp)
        for cp in cps:
            cp.wait()

    sems = pltpu.SemaphoreType.DMA((n,))
    return pl.pallas_call(
        body, name="pair_exchange",
        in_specs=[ANY] * n, out_specs=[ANY] * n,
        out_shape=[jax.ShapeDtypeStruct((g.shape[0] // 2,) + g.shape[1:], g.dtype) for g in grads],
        scratch_shapes=[sems, sems],
        compiler_params=pltpu.CompilerParams(has_side_effects=True),
    )(*grads)


def chip_exchange(parts):
    n = len(parts)

    def body(*refs):
        ins, outs = refs[:n], refs[n:2 * n]
        send, recv, lsem = refs[2 * n:]
        x, y, c, me, chips = _place()
        local, sent = [], []
        for f in range(n):
            hl = parts[f].shape[0]
            rows = pl.ds(0, hl)
            cp = pltpu.make_async_copy(ins[f].at[rows, me], outs[f].at[rows, me], lsem.at[f])
            cp.start()
            local.append(cp)
            for j, (qx, qy) in enumerate(chips):
                cp = _remote(ins[f].at[rows, 2 * qx + qy], outs[f].at[rows, me],
                             send.at[3 * f + j], recv.at[3 * f + j], (qx, qy, c))
                cp.start()
                sent.append(cp)
        for f in range(n):
            rows = pl.ds(0, parts[f].shape[0])
            for j, (qx, qy) in enumerate(chips):
                slab = outs[f].at[rows, 2 * qx + qy]
                _remote(slab, slab, send.at[3 * f + j], recv.at[3 * f + j], (x, y, c)).wait_recv()
        for cp in sent:
            cp.wait_send()
        for cp in local:
            cp.wait()

    sems = pltpu.SemaphoreType.DMA((3 * n,))
    return pl.pallas_call(
        body, name="chip_exchange",
        in_specs=[ANY] * n, out_specs=[ANY] * n,
        out_shape=[jax.ShapeDtypeStruct(s.shape, s.dtype) for s in parts],
        scratch_shapes=[sems, sems, pltpu.SemaphoreType.DMA((n,))],
        compiler_params=pltpu.CompilerParams(has_side_effects=True),
    )(*parts)


def half_swap(grads):
    n = len(grads)

    def body(*refs):
        ins, outs = refs[:n], refs[n:2 * n]
        send, recv = refs[2 * n:]
        x, y, c, _, _ = _place()
        cps = []
        for f in range(n):
            hl = grads[f].shape[0] // 2
            mine = pl.ds(c * hl, hl)
            cp = _remote(outs[f].at[mine], outs[f].at[mine], send.at[f], recv.at[f], (x, y, 1 - c))
            cp.start()
            cps.append(cp)
        for f in range(n):
            hl = grads[f].shape[0] // 2
            theirs = outs[f].at[pl.ds((1 - c) * hl, hl)]
            _remote(theirs, theirs, send.at[f], recv.at[f], (x, y, c)).wait_recv()
        for cp in cps:
            cp.wait_send()

    sems = pltpu.SemaphoreType.DMA((n,))
    return pl.pallas_call(
        body, name="half_swap",
        in_specs=[ANY] * n, out_specs=[ANY] * n,
        out_shape=[jax.ShapeDtypeStruct(g.shape, g.dtype) for g in grads],
        input_output_aliases={f: f for f in range(n)},
        scratch_shapes=[sems, sems],
        compiler_params=pltpu.CompilerParams(has_side_effects=True),
    )(*grads)


def all_sum_small(v):
    R = v.shape[0]

    def body(v_ref, o_ref, land, send, recv):
        x, y, c, _, _ = _place()
        me = 4 * x + 2 * y + c
        land[me] = v_ref[...]
        peers = [(px, py, pc) for px in range(2) for py in range(2) for pc in range(2)]
        cps = []
        for k in range(1, 8):
            dev = (x ^ (k >> 2), y ^ ((k >> 1) & 1), c ^ (k & 1))
            cp = _remote(v_ref, land.at[me], send.at[k - 1], recv.at[k - 1], dev)
            cp.start()
            cps.append(cp)
        for k in range(1, 8):
            src = 4 * (x ^ (k >> 2)) + 2 * (y ^ ((k >> 1) & 1)) + (c ^ (k & 1))
            _remote(v_ref, land.at[src], send.at[k - 1], recv.at[k - 1], (x, y, c)).wait_recv()
        for cp in cps:
            cp.wait_send()
        tot = land[0]
        for d in range(1, len(peers)):
            tot = tot + land[d]
        o_ref[...] = tot

    sems = pltpu.SemaphoreType.DMA((7,))
    vm = pl.BlockSpec(memory_space=pltpu.VMEM)
    return pl.pallas_call(
        body, name="all_sum_small", in_specs=[vm], out_specs=vm,
        out_shape=jax.ShapeDtypeStruct(v.shape, F32),
        scratch_shapes=[pltpu.VMEM((8, R, 128), F32), sems, sems],
        compiler_params=pltpu.CompilerParams(has_side_effects=True),
    )(v)


def _row_tile(R):
    for t in range(min(R, 512) // 8 * 8, 7, -8):
        if R % t == 0:
            return t
    return R


def pair_add(cidx, grad, recv):
    hl, P, R, C = recv.shape
    tr = _row_tile(R)

    def body(c_ref, a_ref, b_ref, o_ref):
        o_ref[...] = (a_ref[...].astype(F32) + b_ref[...].astype(F32)).astype(o_ref.dtype)

    blk = (None, None, tr, C)
    return pl.pallas_call(
        body, name="pair_add",
        grid_spec=pltpu.PrefetchScalarGridSpec(
            num_scalar_prefetch=1, grid=(hl, P, R // tr),
            in_specs=[pl.BlockSpec(blk, lambda l, p, r, c: (c[0] * hl + l, p, r, 0)),
                      pl.BlockSpec(blk, lambda l, p, r, c: (l, p, r, 0))],
            out_specs=pl.BlockSpec(blk, lambda l, p, r, c: (l, p, r, 0))),
        out_shape=jax.ShapeDtypeStruct(recv.shape, recv.dtype),
        compiler_params=_params(("parallel", "parallel", "parallel")),
    )(cidx, grad, recv)


def chip_sum(cidx, land, L):
    hl, P, R, C = land.shape
    tr = _row_tile(R)

    def body(c_ref, a_ref, o_ref):
        tot = a_ref[0].astype(F32)
        for q in range(1, P):
            tot = tot + a_ref[q].astype(F32)
        o_ref[...] = tot

    return pl.pallas_call(
        body, name="chip_sum",
        grid_spec=pltpu.PrefetchScalarGridSpec(
            num_scalar_prefetch=1, grid=(hl, R // tr),
            in_specs=[pl.BlockSpec((None, P, tr, C), lambda l, r, c: (l, 0, r, 0))],
            out_specs=pl.BlockSpec((None, tr, C), lambda l, r, c: (c[0] * hl + l, r, 0))),
        out_shape=jax.ShapeDtypeStruct((L, R, C), F32),
        compiler_params=_params(("parallel", "parallel")),
    )(cidx, land)


def adamw(w, g, m, v):
    lead, (R, C) = w.shape[:-2], w.shape[-2:]
    tr = _row_tile(R)
    c1 = 1.0 / (1.0 - ADAM_B1 ** ADAM_STEP)
    c2 = 1.0 / (1.0 - ADAM_B2 ** ADAM_STEP)

    def body(w_ref, g_ref, m_ref, v_ref, go_ref, d_ref, nm_ref, nv_ref):
        gv = g_ref[...]
        go_ref[...] = gv
        nm = ADAM_B1 * m_ref[...] + (1.0 - ADAM_B1) * gv
        nv = ADAM_B2 * v_ref[...] + (1.0 - ADAM_B2) * (gv * gv)
        nm_ref[...] = nm
        nv_ref[...] = nv
        d_ref[...] = -ADAM_LR * ((nm * c1) / (jnp.sqrt(nv * c2) + ADAM_EPS) + ADAM_WD * w_ref[...])

    def flat(idx):
        l = 0
        for i, n in zip(idx, lead):
            l = l * n + i
        return l

    blk = pl.BlockSpec((None,) * len(lead) + (tr, C), lambda *ix: (*ix, 0))
    gblk = pl.BlockSpec((None, tr, C), lambda *ix: (flat(ix[:-1]), ix[-1], 0))
    osh = jax.ShapeDtypeStruct(w.shape, F32)
    return pl.pallas_call(
        body, name="adamw", grid=(*lead, R // tr),
        in_specs=[blk, gblk, blk, blk], out_specs=[blk, blk, blk, blk], out_shape=[osh, osh, osh, osh],
        compiler_params=_params(("parallel",) * (len(lead) + 1)),
    )(w, g, m, v)


def kernel(x, w_qkv_a, w_o_a, rel_bias, w_qkv_b, w_o_b, ffn_w_gate, ffn_w_up, ffn_w_down, ln_g, ln_b, loss_target, m_w_qkv_a, m_w_o_a, m_rel_bias, m_w_qkv_b, m_w_o_b, m_ffn_w_gate, m_ffn_w_up, m_ffn_w_down, m_ln_g, m_ln_b, v_w_qkv_a, v_w_o_a, v_rel_bias, v_w_qkv_b, v_w_o_b, v_ffn_w_gate, v_ffn_w_up, v_ffn_w_down, v_ln_g, v_ln_b):
    B, S, D = x.shape
    M = B * S
    depth = ffn_w_gate.shape[0]
    n_ffn = 2 * depth
    H = D // HEAD_DIM
    HP = H // 2
    Fs = ffn_w_gate.shape[-1]
    alpha = (2.0 * depth) ** 0.25
    assert S % QB_A == 0 and S % SB_TILE == 0 and rel_bias.shape == (N_REL, H)

    me = 2 * lax.axis_index("x") + lax.axis_index("y")
    ids = jnp.stack([lax.axis_index("c"), me]).astype(jnp.int32)

    def tr(a):
        return jnp.swapaxes(a, -1, -2)

    gate_t, up_t = tr(ffn_w_gate), tr(ffn_w_up)

    def mixer_slots(i):
        wq, wo, mix = (w_qkv_a, w_o_a, "a") if i % 2 == 0 else (w_qkv_b, w_o_b, "b")
        return [(w, (lambda l, i=i: (i // 2,)), 1, (role + mix, i // 2), [((i, 1), role, 0)])
                for role, w in (("q", wq), ("o", wo))]

    def ffn_slots(i, subs):
        first = 0 if subs[0] == 0 else 1
        return [(w, (lambda l, i=i, first=first: (i, first + l)), len(subs), (role, 2 * i + first),
                 [((i, j), role, k) for k, j in enumerate(subs)])
                for role, w in (("gate", gate_t), ("up", up_t), ("down", ffn_w_down))]

    groups = [ffn_slots(0, [0]), mixer_slots(0) + ffn_slots(0, [2])]
    groups += [mixer_slots(i) + ffn_slots(i, [0, 2]) for i in range(1, depth)]

    def group_of(i, j):
        return i + 1 if i else (0 if j == 0 else 1)

    lng_p, lnb_p = gather_weights([ln_g, ln_b])
    lng = jnp.moveaxis(lng_p, 0, 2).reshape(depth, 3, 1, D)
    lnb = jnp.moveaxis(lnb_p, 0, 2).reshape(depth, 3, 1, D)
    placed = [[cast_place(ids, src, lead, L) for src, lead, L, _, _ in grp] for grp in groups]
    in_flight, _ = gather_start([[s for s, _ in grp] for grp in placed],
                                [[z for _, z in grp] for grp in placed], lng_p)

    tab_t = jnp.pad(rel_bias.T, ((0, 0), (0, REL_PAD - N_REL)))
    vr = bias_vec(tab_t).reshape(HP, 2, VR_W)

    xf = x.reshape(M, D)
    xb = xf.astype(BF16)
    saved, weights, landed = [], {}, set()
    for i in range(depth):
        for j in range(3):
            gi = group_of(i, j)
            if gi not in landed:
                landed.add(gi)
                send, recv, thru, lands = in_flight[gi]
                _, lands = gather_wait(gi, send, recv, thru, lands, xf)
                for slot, arr in zip(groups[gi], gather_forward(gi, lands)):
                    for unit, role, l in slot[4]:
                        weights[unit, role] = (arr, l)
            gam, bet = lng[i, j], lnb[i, j]
            if j != 1:
                (wg, l), (wu, _), (wd, _) = (weights[(i, j), r] for r in ("gate", "up", "down"))
                g, u, h = ffn_up(xb, wg, wu, l)
                xo, xob, xhat, rstd = mm_ln(h, wd, l, xf, gam, bet, alpha, 0.5, True)
                saved.append(("ffn", i, l, xb, g, u, h, xhat, rstd, gam))
            else:
                wq, wo = weights[(i, 1), "q"][0], weights[(i, 1), "o"][0]
                qkv = qkv_proj(xb, wq, 0)
                if i % 2 == 0:
                    o, ntot = attn_a_fwd(qkv, vr, B, S), None
                else:
                    o, ntot = attn_b_fwd(qkv, B, S)
                xo, xob, xhat, rstd = mm_ln(o, wo, 0, xf, gam, bet, alpha, 1.0, False)
                saved.append(("a" if i % 2 == 0 else "b", i, 0, xb, qkv, o, ntot, xhat, rstd, gam))
            xf, xb = xo, xob

    dy, loss_part = loss_head(xf, loss_target.reshape(M, D))
    loss = lax.psum(loss_part[0, 0], ("x", "y", "c"))

    Cq, Ro = w_qkv_a.shape[-1], w_o_a.shape[1]
    dgam = lax.empty((3 * depth, 1, D), F32)
    dbet = lax.empty((3 * depth, 1, D), F32)
    dvrs, reducing = [], {}
    gbufs = [[lax.empty((L, N_CHIPS) + src.shape[-2:], BF16) for src, _, L, _, _ in grp] for grp in groups]
    grad_of = {(unit, role): (gi, si, l) for gi, grp in enumerate(groups) for si, slot in enumerate(grp)
               for unit, role, l in slot[4]}

    def ln_of(sub):
        rec = saved[sub]
        return rec[7], rec[8], rec[9], (0.5 if rec[0] == "ffn" else 1.0)

    no_token = jnp.zeros((8, 128), F32)
    token = no_token
    dpre, dyb, dgam, dbet = ln_bwd(dy, *ln_of(3 * depth - 1), dgam, dbet, 3 * depth - 1)
    for sub in reversed(range(3 * depth)):
        i, j = divmod(sub, 3)
        kind, _, l, xb_in, t1, t2, t3, _, _, _ = saved[sub]
        ln = (*ln_of(sub - 1), dgam, dbet, sub - 1) if sub else None
        if kind == "ffn":
            g, u, h = t1, t2, t3
            (gi, sg, _), (_, su, _), (_, sd, _) = (grad_of[(i, j), r] for r in ("gate", "up", "down"))
            wg, wu, wd = (weights[(i, j), r][0] for r in ("gate", "up", "down"))
            dg, du, gbufs[gi][sd], gbufs[gi][sg], gbufs[gi][su] = ffn_bwd(
                gbufs[gi][sd], gbufs[gi][sg], gbufs[gi][su], l, xb_in, dyb, wd, g, u, h, token)
            out = ffn_dx(dg, du, wg, wu, l, dpre, alpha, ln)
        else:
            qkv, o = t1, t2
            (gi, sq, _), (_, so, _) = grad_of[(i, j), "q"], grad_of[(i, j), "o"]
            wq, wo = weights[(i, j), "q"][0], weights[(i, j), "o"][0]
            do = o_proj_bwd(dyb, wo, l, token)
            gbufs[gi][so] = dw_o(gbufs[gi][so], l, o, dyb)
            if kind == "a":
                dqkv, dvr = attn_a_bwd(qkv, vr, do, B, S)
                dvrs.append(dvr.reshape(B, H, VR_W))
            else:
                dqkv = attn_b_bwd(qkv, do, t3, B, S)
            gbufs[gi][sq] = dw_qkv(gbufs[gi][sq], l, xb_in, dqkv)
            out = qkv_proj_bwd(dqkv, wq, l, dpre, alpha, ln)
        if ln is None:
            dy = out
        else:
            dpre, dyb, dgam, dbet = out
        token = no_token
        gi = group_of(i, j)
        if sub == 0 or group_of(*divmod(sub - 1, 3)) != gi:
            from_sib = pair_exchange_rows(gi, gbufs[gi])
            parts = [pair_add_rows(ids, g_, r_) for g_, r_ in zip(gbufs[gi], from_sib)]
            reducing[gi] = reduce_start(gi, parts)
            token = reducing[gi][4]
    grad_x = dy.reshape(B, S, D)

    la, lb = w_qkv_a.shape[0], w_qkv_b.shape[0]
    full = {"qa": lax.empty((la, D, Cq), F32), "oa": lax.empty((la, Ro, D), F32),
            "qb": lax.empty((lb, D, Cq), F32), "ob": lax.empty((lb, Ro, D), F32),
            "gate": lax.empty((n_ffn, Fs, D), F32), "up": lax.empty((n_ffn, Fs, D), F32),
            "down": lax.empty((n_ffn, Fs, D), F32)}
    for gi in reversed(range(len(groups))):
        send, recv, parts, lands, _ = reducing[gi]
        parts, lands = reduce_wait(gi, send, recv, parts, lands, dy)
        for (_, _, _, (name, l0), _), part, land in zip(groups[gi], parts, lands):
            full[name] = chip_sum_rows(ids, land, part, full[name], l0)
    g_qa, g_oa, g_qb, g_ob, g_gate, g_up, g_down = half_swap_rows(
        [full[k] for k in ("qa", "oa", "qb", "ob", "gate", "up", "down")])

    d_tab_t = bias_vec_bwd(jnp.concatenate(dvrs, axis=0))
    small = jnp.concatenate([d_tab_t.reshape(-1), dgam.reshape(-1), dbet.reshape(-1)])
    n_small = small.shape[0]
    rows = -(-n_small // (8 * 128)) * 8
    tot = all_sum_small(jnp.pad(small, (0, rows * 128 - n_small)).reshape(rows, 128)).reshape(-1)
    n_tab, n_ln = H * REL_PAD, 3 * depth * D
    g_rel = tot[:n_tab].reshape(H, REL_PAD)[:, :N_REL].T
    ln_cols = D // N_CHIPS

    def ln_shard(flat):
        return lax.dynamic_slice_in_dim(flat.reshape(depth, 3, D), me * ln_cols, ln_cols, axis=2)

    g_lng = ln_shard(tot[n_tab:n_tab + n_ln])
    g_lnb = ln_shard(tot[n_tab + n_ln:n_tab + 2 * n_ln])

    def upd(w, g, m, v):
        if w.ndim == 2:
            return tuple(a[0] for a in adamw(w[None], g, m[None], v[None]))
        return adamw(w, g, m, v)

    res = [
        upd(w_qkv_a, g_qa, m_w_qkv_a, v_w_qkv_a),
        upd(w_o_a, g_oa, m_w_o_a, v_w_o_a),
        upd(rel_bias, g_rel.reshape(1, N_REL, H), m_rel_bias, v_rel_bias),
        upd(w_qkv_b, g_qb, m_w_qkv_b, v_w_qkv_b),
        upd(w_o_b, g_ob, m_w_o_b, v_w_o_b),
        tuple(tr(a) for a in upd(gate_t, g_gate, tr(m_ffn_w_gate), tr(v_ffn_w_gate))),
        tuple(tr(a) for a in upd(up_t, g_up, tr(m_ffn_w_up), tr(v_ffn_w_up))),
        upd(ffn_w_down, g_down, m_ffn_w_down, v_ffn_w_down),
        upd(ln_g, g_lng, m_ln_g, v_ln_g),
        upd(ln_b, g_lnb, m_ln_b, v_ln_b),
    ]
    grads = [r[0] for r in res]
    deltas = [r[1] for r in res]
    new_m = [r[2] for r in res]
    new_v = [r[3] for r in res]
    return (loss, grad_x, *grads, *deltas, *new_m, *new_v)
```

```python
import math

import jax
import jax.numpy as jnp
from jax import lax
from jax.experimental import pallas as pl
from jax.experimental.pallas import tpu as pltpu

F32 = jnp.float32
BF16 = jnp.bfloat16
MESH = pl.DeviceIdType.MESH

N_CHIPS = 4
HEAD_DIM = 64
CHUNK = 64
LEFT_CHUNKS = 8
LOOKBACK = LEFT_CHUNKS * CHUNK
REL_CLIP = 128
N_REL = 2 * REL_CLIP + 1
REL_PAD = 384
SB_TILE = 256
QB_A = 256
KW_A = QB_A + LOOKBACK
VR_W = 1024
VR_C0 = KW_A - 1
assert math.frexp(HEAD_DIM ** -0.5)[0] == 0.5
LN_EPS = 1e-5
ADAM_LR, ADAM_B1, ADAM_B2, ADAM_EPS, ADAM_WD, ADAM_STEP = 0.001, 0.9, 0.999, 1e-08, 0.01, 10
NEG = -1e30
VMEM_LIMIT = 56 * 1024 * 1024

NT_DIMS = (((1,), (1,)), ((), ()))
TN_DIMS = (((0,), (0,)), ((), ()))
ANY = pl.BlockSpec(memory_space=pl.ANY)


def _params(sem=None):
    return pltpu.CompilerParams(dimension_semantics=sem, vmem_limit_bytes=VMEM_LIMIT)


def _tile(n, pref):
    t = min(n, pref)
    assert n % t == 0, (n, pref)
    return t


def _sigmoid(z):
    return 1.0 / (1.0 + jnp.exp(-z))


def _mm_call(name, operands, in_specs, out_shape, out_spec, grid, dims_list, acc_shape,
             add_coef=None, aliases=None):
    n_pairs = len(dims_list)
    nk = grid[-1]
    has_add = add_coef is not None
    n_alias = len(aliases) if aliases else 0

    def body(*refs):
        pair_refs = refs[:2 * n_pairs]
        pos = 2 * n_pairs
        add_ref = refs[pos] if has_add else None
        pos += (1 if has_add else 0) + n_alias
        o_ref = refs[pos]
        acc_ref = refs[pos + 1] if nk > 1 else None

        def product():
            part = None
            for i, dims in enumerate(dims_list):
                d = lax.dot_general(pair_refs[2 * i][...], pair_refs[2 * i + 1][...], dims,
                                    preferred_element_type=F32)
                part = d if part is None else part + d
            return part

        def finish(r):
            if has_add:
                r = r + add_coef * add_ref[...]
            o_ref[...] = r.astype(o_ref.dtype)

        if nk == 1:
            finish(product())
        else:
            k = pl.program_id(len(grid) - 1)

            @pl.when(k == 0)
            def _():
                acc_ref[...] = jnp.zeros_like(acc_ref)

            acc_ref[...] += product()

            @pl.when(k == nk - 1)
            def _():
                finish(acc_ref[...])

    sem = ("parallel",) * (len(grid) - 1) + ("arbitrary",)
    return pl.pallas_call(
        body, name=name, grid=grid, in_specs=in_specs, out_specs=out_spec, out_shape=out_shape,
        scratch_shapes=[pltpu.VMEM(acc_shape, F32)] if nk > 1 else [],
        input_output_aliases=aliases or {},
        compiler_params=_params(sem),
    )(*operands)


def qkv_proj(xb, w, l):
    M, D = xb.shape
    C = w.shape[-1]
    tm = _tile(M, 1024)
    return _mm_call(
        "qkv_proj", (xb, w),
        [pl.BlockSpec((tm, D), lambda p, i, k: (i, 0)),
         pl.BlockSpec((None, None, D, C), lambda p, i, k: (p, l, 0, 0))],
        jax.ShapeDtypeStruct((M, N_CHIPS * C), BF16),
        pl.BlockSpec((tm, C), lambda p, i, k: (i, p)),
        (N_CHIPS, M // tm, 1), [(((1,), (0,)), ((), ()))], None)


def o_proj_bwd(dyb, w, l, after):
    M, D = dyb.shape
    R = w.shape[2]
    tm = _tile(M, 512)

    def body(a_ref, w_ref, _after, o_ref):
        o_ref[...] = lax.dot_general(a_ref[...], w_ref[...], NT_DIMS,
                                     preferred_element_type=F32).astype(o_ref.dtype)

    return pl.pallas_call(
        body, name="o_proj_bwd", grid=(N_CHIPS, M // tm),
        in_specs=[pl.BlockSpec((tm, D), lambda p, i: (i, 0)),
                  pl.BlockSpec((None, None, R, D), lambda p, i: (p, l, 0, 0)),
                  pl.BlockSpec(after.shape, lambda p, i: (0, 0))],
        out_specs=pl.BlockSpec((tm, R), lambda p, i: (i, p)),
        out_shape=jax.ShapeDtypeStruct((M, N_CHIPS * R), BF16),
        compiler_params=_params(("parallel", "parallel")),
    )(dyb, w, after)


def qkv_proj_bwd(dqkv, w, l, dpre, alpha, ln=None):
    _, M, D = dqkv.shape
    C = w.shape[3]
    tm = _tile(M, 512)
    T = math.gcd(D, C)

    def body(a_ref, w_ref, add_ref, *rest):
        acc = alpha * add_ref[...]
        for t in range(3 * D // T):
            pa, ca = divmod(t * T, D)
            pw, cw = divmod(t * T, C)
            acc = acc + lax.dot_general(a_ref[pa, :, ca:ca + T], w_ref[pw, :, cw:cw + T], NT_DIMS,
                                        preferred_element_type=F32)
        if ln is None:
            rest[0][...] = acc
        else:
            xh_ref, rs_ref, g_ref, _dg, _db, dp_ref, db16_ref, dg_out, db_out = rest
            _ln_bwd_tile(acc, xh_ref, rs_ref, g_ref, ln[3], dp_ref, db16_ref, dg_out, db_out)

    row = pl.BlockSpec((tm, D), lambda i: (i, 0))
    ins = [pl.BlockSpec((3, tm, D), lambda i: (0, i, 0)),
           pl.BlockSpec((N_CHIPS, None, D, C), lambda i: (0, l, 0, 0)), row]
    if ln is None:
        return pl.pallas_call(
            body, name="qkv_proj_bwd", grid=(M // tm,), in_specs=ins,
            out_specs=row, out_shape=jax.ShapeDtypeStruct((M, D), F32),
            compiler_params=_params(("parallel",)),
        )(dqkv, w, dpre)
    xhat, rstd, gam, _, dgam, dbet, ln_row = ln
    specs = _ln_bwd_specs(M, D, tm, ln_row, dgam, dbet)
    return pl.pallas_call(
        body, name="qkv_proj_bwd_ln", grid=(M // tm,), in_specs=ins + specs["in"],
        out_specs=specs["out"], out_shape=specs["out_shape"],
        input_output_aliases={6: 2, 7: 3},
        compiler_params=_params(("arbitrary",)),
    )(dqkv, w, dpre, xhat, rstd, gam, dgam, dbet)


def ffn_dx(dg, du, wgt, wut, l, dpre, alpha, ln=None):
    _, M, Fs = dg.shape
    D = wgt.shape[3]
    tm = _tile(M, 256)

    def body(dg_ref, wg_ref, du_ref, wu_ref, add_ref, *rest):
        acc = alpha * add_ref[...]
        for p in range(N_CHIPS):
            acc = acc + jnp.dot(dg_ref[p], wg_ref[p], preferred_element_type=F32)
            acc = acc + jnp.dot(du_ref[p], wu_ref[p], preferred_element_type=F32)
        if ln is None:
            rest[0][...] = acc
        else:
            xh_ref, rs_ref, g_ref, _dg, _db, dp_ref, db16_ref, dg_out, db_out = rest
            _ln_bwd_tile(acc, xh_ref, rs_ref, g_ref, ln[3], dp_ref, db16_ref, dg_out, db_out)

    act = pl.BlockSpec((N_CHIPS, tm, Fs), lambda i: (0, i, 0))
    wsp = pl.BlockSpec((N_CHIPS, None, Fs, D), lambda i: (0, l, 0, 0))
    row = pl.BlockSpec((tm, D), lambda i: (i, 0))
    if ln is None:
        return pl.pallas_call(
            body, name="ffn_dx", grid=(M // tm,),
            in_specs=[act, wsp, act, wsp, row],
            out_specs=row, out_shape=jax.ShapeDtypeStruct((M, D), F32),
            compiler_params=_params(("parallel",)),
        )(dg, wgt, du, wut, dpre)
    xhat, rstd, gam, _, dgam, dbet, ln_row = ln
    specs = _ln_bwd_specs(M, D, tm, ln_row, dgam, dbet)
    return pl.pallas_call(
        body, name="ffn_dx_ln", grid=(M // tm,),
        in_specs=[act, wsp, act, wsp, row] + specs["in"],
        out_specs=specs["out"], out_shape=specs["out_shape"],
        input_output_aliases={8: 2, 9: 3},
        compiler_params=_params(("arbitrary",)),
    )(dg, wgt, du, wut, dpre, xhat, rstd, gam, dgam, dbet)


def _dw_call(name, buf, l, a, b, a_spec, b_spec, M, tk):
    _, _, R, C = buf.shape
    return _mm_call(
        name, (a, b, buf),
        [a_spec, b_spec, ANY],
        jax.ShapeDtypeStruct(buf.shape, buf.dtype),
        pl.BlockSpec((None, None, R, C), lambda p, k: (l, p, 0, 0)),
        (N_CHIPS, M // tk), [TN_DIMS], (R, C), aliases={2: 0})


def ffn_bwd(gdown, ggate, gup, l, xb, dyb, wd, g, u, h, after):
    M, D = dyb.shape
    Fs = wd.shape[2]
    tm = _tile(M, 512)
    n = M // tm

    def body(x_ref, dy_ref, wd_ref, g_ref, u_ref, h_ref, _gd, _gg, _gu, _after,
             dg_ref, du_ref, gd_ref, gg_ref, gu_ref, acc_d, acc_g, acc_u):
        i = pl.program_id(1)

        @pl.when(i == 0)
        def _():
            acc_d[...] = jnp.zeros_like(acc_d)
            acc_g[...] = jnp.zeros_like(acc_g)
            acc_u[...] = jnp.zeros_like(acc_u)

        dy = dy_ref[...]
        dh = lax.dot_general(dy, wd_ref[...], NT_DIMS, preferred_element_type=F32)
        gf = g_ref[...].astype(F32)
        sig = _sigmoid(gf)
        silu = gf * sig
        dg = (dh * u_ref[...].astype(F32) * (sig * (1.0 + gf - silu))).astype(BF16)
        du = (dh * silu).astype(BF16)
        dg_ref[...] = dg
        du_ref[...] = du
        x = x_ref[...]
        acc_d[...] += lax.dot_general(h_ref[...], dy, TN_DIMS, preferred_element_type=F32)
        acc_g[...] += lax.dot_general(dg, x, TN_DIMS, preferred_element_type=F32)
        acc_u[...] += lax.dot_general(du, x, TN_DIMS, preferred_element_type=F32)

        @pl.when(i == n - 1)
        def _():
            gd_ref[...] = acc_d[...].astype(gd_ref.dtype)
            gg_ref[...] = acc_g[...].astype(gg_ref.dtype)
            gu_ref[...] = acc_u[...].astype(gu_ref.dtype)

    row = pl.BlockSpec((tm, D), lambda p, i: (i, 0))
    act = pl.BlockSpec((None, tm, Fs), lambda p, i: (p, i, 0))
    ash = jax.ShapeDtypeStruct((N_CHIPS, M, Fs), BF16)
    w_blk = pl.BlockSpec((None, None, Fs, D), lambda p, i: (l, p, 0, 0))
    return pl.pallas_call(
        body, name="ffn_bwd", grid=(N_CHIPS, n),
        in_specs=[row, row, pl.BlockSpec((None, None, Fs, D), lambda p, i: (p, l, 0, 0)),
                  act, act, act, ANY, ANY, ANY, pl.BlockSpec(after.shape, lambda p, i: (0, 0))],
        out_specs=[act, act, w_blk, w_blk, w_blk],
        out_shape=[ash, ash] + [jax.ShapeDtypeStruct(b.shape, b.dtype) for b in (gdown, ggate, gup)],
        scratch_shapes=[pltpu.VMEM((Fs, D), F32), pltpu.VMEM((Fs, D), F32), pltpu.VMEM((Fs, D), F32)],
        input_output_aliases={6: 2, 7: 3, 8: 4},
        compiler_params=_params(("parallel", "arbitrary")),
    )(xb, dyb, wd, g, u, h, gdown, ggate, gup, after)


def dw_qkv(buf, l, xb, dqkv):
    M, D = xb.shape
    C = buf.shape[-1]
    tk = _tile(M, 512)
    n = M // tk
    T = math.gcd(D, C)
    nt = 3 * D // T

    def body(x_ref, b_ref, _buf, o_ref, acc_ref):
        k = pl.program_id(0)

        @pl.when(k == 0)
        def _():
            acc_ref[...] = jnp.zeros_like(acc_ref)

        xt = x_ref[...].astype(F32).T.astype(BF16)
        for t in range(nt):
            pa, ca = divmod(t * T, D)
            acc_ref[t] += jnp.dot(xt, b_ref[pa, :, ca:ca + T], preferred_element_type=F32)

        @pl.when(k == n - 1)
        def _():
            for t in range(nt):
                pw, cw = divmod(t * T, C)
                o_ref[pw, :, cw:cw + T] = acc_ref[t].astype(o_ref.dtype)

    return pl.pallas_call(
        body, name="dw_qkv", grid=(n,),
        in_specs=[pl.BlockSpec((tk, D), lambda k: (k, 0)),
                  pl.BlockSpec((3, tk, D), lambda k: (0, k, 0)), ANY],
        out_specs=pl.BlockSpec((None, N_CHIPS, D, C), lambda k: (l, 0, 0, 0)),
        out_shape=jax.ShapeDtypeStruct(buf.shape, buf.dtype),
        scratch_shapes=[pltpu.VMEM((nt, D, T), F32)],
        input_output_aliases={2: 0},
        compiler_params=_params(("arbitrary",)),
    )(xb, dqkv, buf)


def dw_o(buf, l, o, dyb):
    M, D = dyb.shape
    R = buf.shape[2]
    tk = _tile(M, 1024)
    return _dw_call("dw_o", buf, l, o, dyb,
                    pl.BlockSpec((tk, R), lambda p, k: (k, p)),
                    pl.BlockSpec((tk, D), lambda p, k: (k, 0)), M, tk)


def ffn_up(xb, wgt, wut, l):
    M, D = xb.shape
    Fs = wgt.shape[2]
    tm = _tile(M, 1024)

    def body(x_ref, wg_ref, wu_ref, g_ref, u_ref, h_ref):
        x = x_ref[...]
        g = lax.dot_general(x, wg_ref[...], NT_DIMS, preferred_element_type=F32)
        u = lax.dot_general(x, wu_ref[...], NT_DIMS, preferred_element_type=F32)
        g_ref[...] = g.astype(BF16)
        u_ref[...] = u.astype(BF16)
        h_ref[...] = (g * _sigmoid(g) * u).astype(BF16)

    wsp = pl.BlockSpec((None, None, Fs, D), lambda p, i: (p, l, 0, 0))
    osp = pl.BlockSpec((None, tm, Fs), lambda p, i: (p, i, 0))
    osh = jax.ShapeDtypeStruct((N_CHIPS, M, Fs), BF16)
    return pl.pallas_call(
        body, name="ffn_up", grid=(N_CHIPS, M // tm),
        in_specs=[pl.BlockSpec((tm, D), lambda p, i: (i, 0)), wsp, wsp],
        out_specs=[osp, osp, osp], out_shape=[osh, osh, osh],
        compiler_params=_params(("parallel", "parallel")),
    )(xb, wgt, wut)


def mm_ln(a, w, l, x, gam, bet, alpha, scale, a_piece_major):
    M, D = x.shape
    R = w.shape[2]
    tm = _tile(M, 512)
    if a_piece_major:
        a_spec = pl.BlockSpec((N_CHIPS, tm, R), lambda i: (0, i, 0))
    else:
        a_spec = pl.BlockSpec((tm, N_CHIPS * R), lambda i: (i, 0))

    def body(a_ref, w_ref, x_ref, g_ref, b_ref, xo_ref, xb_ref, xh_ref, rs_ref):
        y = None
        for p in range(N_CHIPS):
            a = a_ref[p] if a_piece_major else a_ref[:, p * R:(p + 1) * R]
            d = jnp.dot(a, w_ref[p], preferred_element_type=F32)
            y = d if y is None else y + d
        pre = alpha * x_ref[...] + scale * y
        mu = jnp.mean(pre, axis=-1, keepdims=True)
        cen = pre - mu
        var = jnp.mean(cen * cen, axis=-1, keepdims=True)
        rstd = lax.rsqrt(var + LN_EPS)
        xhat = cen * rstd
        out = xhat * g_ref[...] + b_ref[...]
        xo_ref[...] = out
        xb_ref[...] = out.astype(BF16)
        xh_ref[...] = xhat
        rs_ref[...] = rstd

    row = pl.BlockSpec((tm, D), lambda i: (i, 0))
    vec = pl.BlockSpec((1, D), lambda i: (0, 0))
    return pl.pallas_call(
        body, name="mm_ln", grid=(M // tm,),
        in_specs=[a_spec, pl.BlockSpec((N_CHIPS, None, R, D), lambda i: (0, l, 0, 0)), row, vec, vec],
        out_specs=[row, row, row, pl.BlockSpec((tm, 1), lambda i: (i, 0))],
        out_shape=[jax.ShapeDtypeStruct((M, D), F32), jax.ShapeDtypeStruct((M, D), BF16),
                   jax.ShapeDtypeStruct((M, D), F32), jax.ShapeDtypeStruct((M, 1), F32)],
        compiler_params=_params(("parallel",)),
    )(a, w, x, gam, bet)


def ln_bwd(dy, xhat, rstd, gam, scale, dgam, dbet, row):
    M, D = dy.shape
    tm = _tile(M, 512)

    def body(dy_ref, xh_ref, rs_ref, g_ref, _dg, _db, dp_ref, db16_ref, dg_ref, dbt_ref):
        _ln_bwd_tile(dy_ref[...], xh_ref, rs_ref, g_ref, scale, dp_ref, db16_ref, dg_ref, dbt_ref)

    specs = _ln_bwd_specs(M, D, tm, row, dgam, dbet)
    return pl.pallas_call(
        body, name="ln_bwd", grid=(M // tm,),
        in_specs=[pl.BlockSpec((tm, D), lambda i: (i, 0))] + specs["in"],
        out_specs=specs["out"], out_shape=specs["out_shape"],
        input_output_aliases={4: 2, 5: 3},
        compiler_params=_params(("arbitrary",)),
    )(dy, xhat, rstd, gam, dgam, dbet)


def _ln_bwd_tile(dy_v, xh_ref, rs_ref, g_ref, scale, dp_ref, db16_ref, dg_ref, dbt_ref):
    i = pl.program_id(0)
    xh = xh_ref[...]
    dxh = dy_v * g_ref[...]
    m1 = jnp.mean(dxh, axis=-1, keepdims=True)
    m2 = jnp.mean(dxh * xh, axis=-1, keepdims=True)
    dpre = rs_ref[...] * (dxh - m1 - xh * m2)
    dp_ref[...] = dpre
    db16_ref[...] = (scale * dpre).astype(BF16)
    dgp = jnp.sum(dy_v * xh, axis=0, keepdims=True)
    dbp = jnp.sum(dy_v, axis=0, keepdims=True)

    @pl.when(i == 0)
    def _():
        dg_ref[...] = dgp
        dbt_ref[...] = dbp

    @pl.when(i > 0)
    def _():
        dg_ref[...] += dgp
        dbt_ref[...] += dbp


def _ln_bwd_specs(M, D, tm, row, dgam, dbet):
    tok = pl.BlockSpec((tm, D), lambda i: (i, 0))
    vec = pl.BlockSpec((1, D), lambda i: (0, 0))
    acc = pl.BlockSpec((None, 1, D), lambda i: (row, 0, 0))
    return {"in": [tok, pl.BlockSpec((tm, 1), lambda i: (i, 0)), vec, ANY, ANY],
            "out": [tok, tok, acc, acc],
            "out_shape": [jax.ShapeDtypeStruct((M, D), F32), jax.ShapeDtypeStruct((M, D), BF16),
                          jax.ShapeDtypeStruct(dgam.shape, F32), jax.ShapeDtypeStruct(dbet.shape, F32)]}


def loss_head(y, tgt):
    M, D = y.shape
    tm = _tile(M, 512)
    n = M // tm

    def body(y_ref, t_ref, dy_ref, l_ref, acc_ref):
        i = pl.program_id(0)
        e = y_ref[...] - t_ref[...]
        dy_ref[...] = e * (1.0 / D)
        part = jnp.sum(e * e, axis=0, keepdims=True)

        @pl.when(i == 0)
        def _():
            acc_ref[...] = part

        @pl.when(i > 0)
        def _():
            acc_ref[...] += part

        @pl.when(i == n - 1)
        def _():
            l_ref[...] = (0.5 / D) * jnp.sum(acc_ref[...], axis=1, keepdims=True)

    row = pl.BlockSpec((tm, D), lambda i: (i, 0))
    return pl.pallas_call(
        body, name="loss_head", grid=(n,),
        in_specs=[row, row],
        out_specs=[row, pl.BlockSpec((1, 1), lambda i: (0, 0))],
        out_shape=[jax.ShapeDtypeStruct((M, D), F32), jax.ShapeDtypeStruct((1, 1), F32)],
        scratch_shapes=[pltpu.VMEM((1, D), F32)],
        compiler_params=_params(("arbitrary",)),
    )(y, tgt)


def _rel_onehot_t():
    r = lax.broadcasted_iota(jnp.int32, (REL_PAD, VR_W), 0)
    n = lax.broadcasted_iota(jnp.int32, (REL_PAD, VR_W), 1)
    idx = jnp.clip(VR_C0 - n, -REL_CLIP, REL_CLIP) + REL_CLIP
    return (r == idx).astype(F32)


def bias_vec(tab_t):
    H = tab_t.shape[0]

    def body(t_ref, o_ref):
        o_ref[...] = jnp.dot(t_ref[...], _rel_onehot_t(), precision=lax.Precision.HIGHEST,
                             preferred_element_type=F32)

    return pl.pallas_call(
        body, name="bias_vec", out_shape=jax.ShapeDtypeStruct((H, VR_W), F32),
        compiler_params=_params(),
    )(tab_t)


def bias_vec_bwd(dvr):
    n, H, _ = dvr.shape

    def body(d_ref, o_ref):
        tot = d_ref[0]
        for i in range(1, n):
            tot = tot + d_ref[i]
        o_ref[...] = lax.dot_general(tot, _rel_onehot_t(), NT_DIMS, precision=lax.Precision.HIGHEST,
                                     preferred_element_type=F32)

    return pl.pallas_call(
        body, name="bias_vec_bwd", out_shape=jax.ShapeDtypeStruct((H, REL_PAD), F32),
        compiler_params=_params(),
    )(dvr)


def _a_bias_mask(vr_row):
    xb = jnp.broadcast_to(vr_row, (QB_A, VR_W))
    tile = pltpu.roll(xb, VR_W - (QB_A - 1), 1, stride=1, stride_axis=0)[:, :KW_A]
    qc = lax.broadcasted_iota(jnp.int32, (QB_A, KW_A), 0) // CHUNK
    kc = lax.broadcasted_iota(jnp.int32, (QB_A, KW_A), 1) // CHUNK
    valid = (kc >= qc) & (kc <= qc + LEFT_CHUNKS)
    return jnp.where(valid, tile, NEG)


def _a_diag_sums(db_acc, h):
    acc8 = None
    for a in range(QB_A // 8):
        grp = db_acc[h, 8 * a:8 * a + 8, :]
        shift = QB_A - 8 - 8 * a
        if shift:
            grp = pltpu.roll(grp, shift, 1)
        acc8 = grp if acc8 is None else acc8 + grp
    sub = lax.broadcasted_iota(jnp.int32, (8, VR_W), 0)
    tot = jnp.zeros((8, VR_W), F32)
    for b in range(8):
        moved = pltpu.roll(acc8, 7 - b, 1) if b < 7 else acc8
        tot = tot + jnp.where(sub == b, moved, 0.0)
    return jnp.sum(tot, axis=0, keepdims=True)


def _a_blocks(S):
    out = []
    for qi in range(S // QB_A):
        q0 = qi * QB_A
        ks = max(0, q0 - LOOKBACK)
        out.append((q0, ks, q0 + QB_A, ks - (q0 - LOOKBACK)))
    return out


def _head_specs(S, HP):
    q = pl.BlockSpec((S, 2 * HEAD_DIM), lambda b, hp: (b, hp))
    k = pl.BlockSpec((S, 2 * HEAD_DIM), lambda b, hp: (b, HP + hp))
    v = pl.BlockSpec((S, 2 * HEAD_DIM), lambda b, hp: (b, 2 * HP + hp))
    return q, k, v


def attn_a_fwd(qkv, vr, B, S):
    D = qkv.shape[1] // 3
    HP = D // (2 * HEAD_DIM)
    scale = HEAD_DIM ** -0.5
    blocks = _a_blocks(S)

    def body(q_ref, k_ref, v_ref, vr_ref, o_ref):
        heads = [slice(h * HEAD_DIM, (h + 1) * HEAD_DIM) for h in range(2)]
        bms = [_a_bias_mask(vr_ref[h:h + 1, :]) for h in range(2)]
        for pair in range(0, len(blocks), 2):
            chains = [(slice(q0, q0 + QB_A), ks, ke, joff, hs, bm)
                      for (q0, ks, ke, joff) in blocks[pair:pair + 2] for hs, bm in zip(heads, bms)]
            ss = [lax.dot_general(q_ref[rows, hs] * scale, k_ref[ks:ke, hs], NT_DIMS,
                                  preferred_element_type=F32) + bm[:, joff:]
                  for rows, ks, ke, joff, hs, bm in chains]
            ps = [jnp.exp(s - jnp.max(s, axis=-1, keepdims=True)) for s in ss]
            for (rows, ks, ke, _, hs, _), p in zip(chains, ps):
                den = jnp.sum(p, axis=-1, keepdims=True)
                o = jnp.dot(p.astype(BF16), v_ref[ks:ke, hs], preferred_element_type=F32) / den
                o_ref[rows, hs] = o.astype(BF16)

    qs, ks_, vs = _head_specs(S, HP)
    return pl.pallas_call(
        body, name="attn_a_fwd", grid=(B, HP),
        in_specs=[qs, ks_, vs, pl.BlockSpec((None, 2, VR_W), lambda b, hp: (hp, 0, 0))],
        out_specs=pl.BlockSpec((S, 2 * HEAD_DIM), lambda b, hp: (b, hp)),
        out_shape=jax.ShapeDtypeStruct((B * S, D), BF16),
        compiler_params=_params(("parallel", "parallel")),
    )(qkv, qkv, qkv, vr)


def attn_a_bwd(qkv, vr, do, B, S):
    D = qkv.shape[1] // 3
    HP = D // (2 * HEAD_DIM)
    scale = HEAD_DIM ** -0.5
    blocks = _a_blocks(S)

    def body(q_ref, k_ref, v_ref, vr_ref, do_ref, dqkv_ref, dvr_ref, dkt_acc, dvt_acc, db_acc):
        dkt_acc[...] = jnp.zeros_like(dkt_acc)
        dvt_acc[...] = jnp.zeros_like(dvt_acc)
        db_acc[...] = jnp.zeros_like(db_acc)
        heads = [slice(h * HEAD_DIM, (h + 1) * HEAD_DIM) for h in range(2)]
        bms = [_a_bias_mask(vr_ref[h:h + 1, :]) for h in range(2)]
        for pair in range(0, len(blocks), 2):
            chains, qts, dots = [], [], []
            for (q0, ks, ke, joff) in blocks[pair:pair + 2]:
                rows = slice(q0, q0 + QB_A)
                qt_pair = (q_ref[rows, :] * scale).astype(F32).T.astype(BF16)
                dot_pair = do_ref[rows, :].astype(F32).T.astype(BF16)
                for h, hs in enumerate(heads):
                    chains.append((rows, ks, ke, joff, h, hs))
                    qts.append(qt_pair[hs, :])
                    dots.append(dot_pair[hs, :])
            ss = [lax.dot_general(q_ref[rows, hs] * scale, k_ref[ks:ke, hs], NT_DIMS,
                                  preferred_element_type=F32) + bms[h][:, joff:]
                  for rows, ks, ke, joff, h, hs in chains]
            dps = [lax.dot_general(do_ref[rows, hs], v_ref[ks:ke, hs], NT_DIMS, preferred_element_type=F32)
                   for rows, ks, ke, _, _, hs in chains]
            ps, dsbs = [], []
            for (_, _, _, joff, h, _), s, dp in zip(chains, ss, dps):
                e = jnp.exp(s - jnp.max(s, axis=-1, keepdims=True))
                p = e / jnp.sum(e, axis=-1, keepdims=True)
                ds = p * (dp - jnp.sum(p * dp, axis=-1, keepdims=True))
                db_acc[h, :, joff:KW_A] += ds
                ps.append(p.astype(BF16))
                dsbs.append(ds.astype(BF16))
            for (rows, ks, ke, _, _, hs), p, dsb, qt, dot_ in zip(chains, ps, dsbs, qts, dots):
                dq = jnp.dot(dsb, k_ref[ks:ke, hs], preferred_element_type=F32) * scale
                dqkv_ref[0, rows, hs] = dq.astype(BF16)
                dkt_acc[hs, ks:ke] += jnp.dot(qt, dsb, preferred_element_type=F32)
                dvt_acc[hs, ks:ke] += jnp.dot(dot_, p, preferred_element_type=F32)
        for h in range(2):
            dvr_ref[h:h + 1, :] = _a_diag_sums(db_acc, h)
        dqkv_ref[1] = dkt_acc[...].T.astype(BF16)
        dqkv_ref[2] = dvt_acc[...].T.astype(BF16)

    qs, ks_, vs = _head_specs(S, HP)
    hd = pl.BlockSpec((S, 2 * HEAD_DIM), lambda b, hp: (b, hp))
    return pl.pallas_call(
        body, name="attn_a_bwd", grid=(B, HP),
        in_specs=[qs, ks_, vs, pl.BlockSpec((None, 2, VR_W), lambda b, hp: (hp, 0, 0)), hd],
        out_specs=[pl.BlockSpec((3, S, 2 * HEAD_DIM), lambda b, hp: (0, b, hp)),
                   pl.BlockSpec((None, None, 2, VR_W), lambda b, hp: (b, hp, 0, 0))],
        out_shape=[jax.ShapeDtypeStruct((3, B * S, D), BF16), jax.ShapeDtypeStruct((B, HP, 2, VR_W), F32)],
        scratch_shapes=[pltpu.VMEM((2 * HEAD_DIM, S), F32), pltpu.VMEM((2 * HEAD_DIM, S), F32),
                        pltpu.VMEM((2, QB_A, VR_W), F32)],
        compiler_params=_params(("parallel", "parallel")),
    )(qkv, qkv, qkv, vr, do)


def _tri(cmp):
    j = lax.broadcasted_iota(jnp.int32, (SB_TILE, SB_TILE), 0)
    s = lax.broadcasted_iota(jnp.int32, (SB_TILE, SB_TILE), 1)
    return cmp(j, s).astype(BF16)


def _cumsum_mm(x, tri):
    hi = x.astype(BF16)
    mid = (x - hi.astype(F32)).astype(BF16)
    return jnp.dot(hi, tri, preferred_element_type=F32) + jnp.dot(mid, tri, preferred_element_type=F32)


def _sb_logs(q, k, diagonal):
    z = lax.dot_general(q, k, NT_DIMS, preferred_element_type=F32)
    log_b = jnp.minimum(z, 0.0) - jnp.log(1.0 + jnp.exp(-jnp.abs(z)))
    log_1mb = log_b - z
    if not diagonal:
        return log_b, log_1mb, None
    row = lax.broadcasted_iota(jnp.int32, (SB_TILE, SB_TILE), 0)
    col = lax.broadcasted_iota(jnp.int32, (SB_TILE, SB_TILE), 1)
    causal = col < row
    return log_b, jnp.where(causal, log_1mb, 0.0), causal


def attn_b_fwd(qkv, B, S):
    D = qkv.shape[1] // 3
    HP = D // (2 * HEAD_DIM)
    scale = HEAD_DIM ** -0.5
    nb = S // SB_TILE

    def body(q_ref, k_ref, v_ref, o_ref, nt_ref):
        tri = _tri(lambda j, s: j > s)
        heads = [slice(h * HEAD_DIM, (h + 1) * HEAD_DIM) for h in range(2)]

        def q_block(qb, n_pairs, parity):
            q0 = pl.multiple_of(qb * SB_TILE, SB_TILE)
            rows = pl.ds(q0, SB_TILE)
            qs = [q_ref[rows, hs] * scale for hs in heads]

            def step(blocks, state):
                chains = [(h, kb, diagonal, pl.ds(pl.multiple_of(kb * SB_TILE, SB_TILE), SB_TILE))
                          for h in range(2) for kb, diagonal in blocks]
                logs = [_sb_logs(qs[h], k_ref[keys, heads[h]], diagonal)
                        for h, _, diagonal, keys in chains]
                sums = [_cumsum_mm(log_1mb, tri) for _, log_1mb, _ in logs]
                rights = [state[0][0], state[1][0]]
                accs = [state[0][1], state[1][1]]
                weights_ = []
                for (h, _, diagonal, _), (log_b, log_1mb, causal), csum in zip(chains, logs, sums):
                    a = jnp.exp(log_b + csum + rights[h])
                    if diagonal:
                        a = jnp.where(causal, a, 0.0)
                    weights_.append(a.astype(BF16))
                    rights[h] = rights[h] + jnp.sum(log_1mb, axis=-1, keepdims=True)
                for (h, _, _, keys), a in zip(chains, weights_):
                    accs[h] = accs[h] + jnp.dot(a, v_ref[keys, heads[h]], preferred_element_type=F32)
                return ((rights[0], accs[0]), (rights[1], accs[1]))

            zero = (jnp.zeros((SB_TILE, 1), F32), jnp.zeros((SB_TILE, HEAD_DIM), F32))
            first = [(qb, True)] + ([(qb - 1, False)] if parity else [])
            top = qb - len(first)
            state = lax.fori_loop(
                0, n_pairs, lambda t, st: step([(top - 2 * t, False), (top - 2 * t - 1, False)], st),
                step(first, (zero, zero)))
            for hs, (right, acc) in zip(heads, state):
                o_ref[rows, hs] = acc.astype(BF16)
                nt_ref[rows, hs] = jnp.broadcast_to(right, (SB_TILE, HEAD_DIM))

        def q_pair_loop(j, carry):
            q_block(2 * j, j, 0)
            q_block(2 * j + 1, j, 1)
            return carry

        lax.fori_loop(0, nb // 2, q_pair_loop, 0)

    qs, ks_, vs = _head_specs(S, HP)
    hd = pl.BlockSpec((S, 2 * HEAD_DIM), lambda b, hp: (b, hp))
    return pl.pallas_call(
        body, name="attn_b_fwd", grid=(B, HP),
        in_specs=[qs, ks_, vs], out_specs=[hd, hd],
        out_shape=[jax.ShapeDtypeStruct((B * S, D), BF16), jax.ShapeDtypeStruct((B * S, D), F32)],
        compiler_params=_params(("parallel", "parallel")),
    )(qkv, qkv, qkv)


def attn_b_bwd(qkv, do, ntot, B, S):
    D = qkv.shape[1] // 3
    HP = D // (2 * HEAD_DIM)
    scale = HEAD_DIM ** -0.5
    nb = S // SB_TILE

    def body(q_ref, k_ref, v_ref, do_ref, nt_ref, dqkv_ref, dkt_acc, dvt_acc):
        tri_incl = _tri(lambda j, s: j <= s)
        tri_excl = _tri(lambda j, s: j < s)
        heads = [slice(h * HEAD_DIM, (h + 1) * HEAD_DIM) for h in range(2)]
        dkt_acc[...] = jnp.zeros_like(dkt_acc)
        dvt_acc[...] = jnp.zeros_like(dvt_acc)

        def q_block(qb, n_pairs, parity):
            q0 = pl.multiple_of(qb * SB_TILE, SB_TILE)
            rows = pl.ds(q0, SB_TILE)
            qt_pair = (q_ref[rows, :] * scale).astype(F32).T.astype(BF16)
            dot_pair = do_ref[rows, :].astype(F32).T.astype(BF16)
            per_head = [(hs, q_ref[rows, hs] * scale, do_ref[rows, hs], qt_pair[hs, :], dot_pair[hs, :],
                         nt_ref[rows, hs.start:hs.start + 1]) for hs in heads]

            def step(blocks, state):
                chains = [(h, kb, diagonal, pl.ds(pl.multiple_of(kb * SB_TILE, SB_TILE), SB_TILE))
                          for h in range(2) for kb, diagonal in blocks]
                ks = [k_ref[keys, per_head[h][0]] for h, _, _, keys in chains]
                logs = [_sb_logs(per_head[h][1], k, diagonal)
                        for (h, _, diagonal, _), k in zip(chains, ks)]
                das = [lax.dot_general(per_head[h][2], v_ref[keys, per_head[h][0]], NT_DIMS,
                                       preferred_element_type=F32) for h, _, _, keys in chains]
                sums = [_cumsum_mm(log_1mb, tri_incl) for _, log_1mb, _ in logs]
                left_n = [state[0][0], state[1][0]]
                left_d = [state[0][1], state[1][1]]
                dq_acc = [state[0][2], state[1][2]]
                a_s, dls = [], []
                for (h, _, diagonal, _), (log_b, log_1mb, causal), csum, da in zip(chains, logs, sums, das):
                    a = jnp.exp(log_b + (per_head[h][5] - left_n[h]) - csum)
                    if diagonal:
                        a = jnp.where(causal, a, 0.0)
                    a_s.append(a)
                    dls.append(a * da)
                    left_n[h] = left_n[h] + jnp.sum(log_1mb, axis=-1, keepdims=True)
                dsums = [_cumsum_mm(dl, tri_excl) for dl in dls]
                dzbs = []
                for (h, _, diagonal, _), (log_b, log_1mb, causal), dl, dsum in zip(chains, logs, dls, dsums):
                    dz = dl * jnp.exp(log_1mb) - (left_d[h] + dsum) * jnp.exp(log_b)
                    if diagonal:
                        dz = jnp.where(causal, dz, 0.0)
                    dzbs.append(dz.astype(BF16))
                    left_d[h] = left_d[h] + jnp.sum(dl, axis=-1, keepdims=True)
                for (h, kb, _, _), k, a, dzb in zip(chains, ks, a_s, dzbs):
                    hs, _, _, qt, dot_, _ = per_head[h]
                    dq_acc[h] = dq_acc[h] + jnp.dot(dzb, k, preferred_element_type=F32)
                    dkt_acc[kb, hs, :] += jnp.dot(qt, dzb, preferred_element_type=F32)
                    dvt_acc[kb, hs, :] += jnp.dot(dot_, a.astype(BF16), preferred_element_type=F32)
                return ((left_n[0], left_d[0], dq_acc[0]), (left_n[1], left_d[1], dq_acc[1]))

            zero1 = jnp.zeros((SB_TILE, 1), F32)
            zero = (zero1, zero1, jnp.zeros((SB_TILE, HEAD_DIM), F32))
            state = lax.fori_loop(
                0, n_pairs, lambda t, st: step([(2 * t, False), (2 * t + 1, False)], st), (zero, zero))
            last = ([(qb - 1, False)] if parity else []) + [(qb, True)]
            state = step(last, state)
            for hs, (_, _, dq_acc) in zip(heads, state):
                dqkv_ref[0, rows, hs] = (dq_acc * scale).astype(BF16)

        def q_pair_loop(j, carry):
            q_block(2 * j, j, 0)
            q_block(2 * j + 1, j, 1)
            return carry

        lax.fori_loop(0, nb // 2, q_pair_loop, 0)
        for kb in range(nb):
            dqkv_ref[1, kb * SB_TILE:(kb + 1) * SB_TILE, :] = dkt_acc[kb].T.astype(BF16)
            dqkv_ref[2, kb * SB_TILE:(kb + 1) * SB_TILE, :] = dvt_acc[kb].T.astype(BF16)

    qs, ks_, vs = _head_specs(S, HP)
    hd = pl.BlockSpec((S, 2 * HEAD_DIM), lambda b, hp: (b, hp))
    acc = pltpu.VMEM((nb, 2 * HEAD_DIM, SB_TILE), F32)
    return pl.pallas_call(
        body, name="attn_b_bwd", grid=(B, HP),
        in_specs=[qs, ks_, vs, hd, hd],
        out_specs=pl.BlockSpec((3, S, 2 * HEAD_DIM), lambda b, hp: (0, b, hp)),
        out_shape=jax.ShapeDtypeStruct((3, B * S, D), BF16),
        scratch_shapes=[acc, acc],
        compiler_params=_params(("parallel", "parallel")),
    )(qkv, qkv, qkv, do, ntot)


def _place():
    x, y, c = lax.axis_index("x"), lax.axis_index("y"), lax.axis_index("c")
    chips = [(1 - x, y), (x, 1 - y), (1 - x, 1 - y)]
    return x, y, c, 2 * x + y, chips


def _remote(src, dst, send_sem, recv_sem, dev):
    return pltpu.make_async_remote_copy(src_ref=src, dst_ref=dst, send_sem=send_sem, recv_sem=recv_sem,
                                        device_id=dev, device_id_type=MESH)


def gather_weights(shards):
    n = len(shards)

    def body(*refs):
        ins, outs = refs[:n], refs[n:2 * n]
        send1, recv1, send2, recv2, lsem = refs[2 * n:]
        x, y, c, me, chips = _place()
        local, first = [], []
        for f in range(n):
            hl = shards[f].shape[0] // 2
            cp = pltpu.make_async_copy(ins[f], outs[f].at[me], lsem.at[f])
            cp.start()
            local.append(cp)
            for j, (qx, qy) in enumerate(chips):
                half = pl.ds(c * hl, hl)
                cp = _remote(ins[f].at[half], outs[f].at[me, half],
                             send1.at[3 * f + j], recv1.at[3 * f + j], (qx, qy, c))
                cp.start()
                first.append(cp)
        passed = []
        for f in range(n):
            hl = shards[f].shape[0] // 2
            for j, (qx, qy) in enumerate(chips):
                slab = outs[f].at[2 * qx + qy, pl.ds(c * hl, hl)]
                _remote(slab, slab, send1.at[3 * f + j], recv1.at[3 * f + j], (x, y, c)).wait_recv()
                cp = _remote(slab, slab, send2.at[3 * f + j], recv2.at[3 * f + j], (x, y, 1 - c))
                cp.start()
                passed.append(cp)
        for f in range(n):
            hl = shards[f].shape[0] // 2
            for j, (qx, qy) in enumerate(chips):
                slab = outs[f].at[2 * qx + qy, pl.ds((1 - c) * hl, hl)]
                _remote(slab, slab, send2.at[3 * f + j], recv2.at[3 * f + j], (x, y, c)).wait_recv()
        for cp in first + passed:
            cp.wait_send()
        for cp in local:
            cp.wait()

    sems = pltpu.SemaphoreType.DMA((3 * n,))
    return pl.pallas_call(
        body, name="gather_weights",
        in_specs=[ANY] * n, out_specs=[ANY] * n,
        out_shape=[jax.ShapeDtypeStruct((N_CHIPS,) + s.shape, s.dtype) for s in shards],
        scratch_shapes=[sems, sems, sems, sems, pltpu.SemaphoreType.DMA((n,))],
        compiler_params=pltpu.CompilerParams(has_side_effects=True),
    )(*shards)


HBM = pl.BlockSpec(memory_space=pltpu.HBM)
SEM = pl.BlockSpec(memory_space=pltpu.SEMAPHORE)
EFFECT = pltpu.SideEffectType.DATAFLOW_SIDE_EFFECTING


def _in_hbm(a):
    return pltpu.with_memory_space_constraint(a, pltpu.HBM)


def _row_half(ref, which):
    hr = ref.shape[-2] // 2
    idx = [pl.ds(0, d) for d in ref.shape[:-2]] + [pl.ds(which * hr, hr), pl.ds(0, ref.shape[-1])]
    return ref.at[tuple(idx)]


def cast_place(ids, w, lead, L):
    R, C = w.shape[-2:]

    def body(ids_ref, w_ref, s_ref, land_ref):
        v = w_ref[...].astype(BF16)
        s_ref[...] = v
        land_ref[...] = v

    return pl.pallas_call(
        body, name="cast_place",
        grid_spec=pltpu.PrefetchScalarGridSpec(
            num_scalar_prefetch=1, grid=(L,),
            in_specs=[pl.BlockSpec((None,) * (w.ndim - 2) + (R, C), lambda l, ids: (*lead(l), 0, 0))],
            out_specs=[pl.BlockSpec((None, R, C), lambda l, ids: (l, 0, 0)),
                       pl.BlockSpec((None, None, R, C), lambda l, ids: (ids[1], l, 0, 0))]),
        out_shape=[jax.ShapeDtypeStruct((L, R, C), BF16), jax.ShapeDtypeStruct((N_CHIPS, L, R, C), BF16)],
        compiler_params=_params(("parallel",)),
    )(ids, w)


def gather_start(layers, zones, after):
    flat = [s for lay in layers for s in lay]
    flat_zones = [z for lay in zones for z in lay]
    counts = [len(lay) for lay in layers]
    n, nl = len(flat), len(layers)

    def body(*refs):
        ins, lands = refs[:n], refs[n:2 * n]
        send, recv = refs[2 * n + 1:2 * n + 1 + nl], refs[2 * n + 1 + nl:2 * n + 1 + 2 * nl]
        token = refs[-1]
        x, y, c, me, chips = _place()
        f = 0
        for li, cnt in enumerate(counts):
            for k in range(cnt):
                for j, (qx, qy) in enumerate(chips):
                    _remote(_row_half(ins[f], c), _row_half(lands[f].at[me], c),
                            send[li].at[3 * k + j], recv[li].at[3 * k + j], (qx, qy, c)).start()
                f += 1
        token[...] = jnp.zeros_like(token)

    sem_shapes = [pltpu.SemaphoreType.DMA((3 * cnt,)) for cnt in counts]
    land_shapes = [(N_CHIPS,) + s.shape for s in flat]
    res = pl.pallas_call(
        body, name="gather_start",
        out_shape=(*sem_shapes, *sem_shapes,
                   *[pltpu.HBM(s.shape, s.dtype) for s in flat],
                   *[pltpu.HBM(shp, s.dtype) for shp, s in zip(land_shapes, flat)],
                   jax.ShapeDtypeStruct((8, 128), F32)),
        in_specs=[HBM] * (2 * n) + [ANY],
        out_specs=(*[SEM] * (2 * nl), *[HBM] * (2 * n), pl.BlockSpec(memory_space=pltpu.VMEM)),
        input_output_aliases={k: 2 * nl + k for k in range(2 * n)},
        compiler_params=pltpu.CompilerParams(has_side_effects=EFFECT),
    )(*[_in_hbm(s) for s in flat], *[_in_hbm(z) for z in flat_zones], after)
    send, recv = res[:nl], res[nl:2 * nl]
    thru, lands, token = res[2 * nl:2 * nl + n], res[2 * nl + n:2 * nl + 2 * n], res[-1]
    out, f = [], 0
    for li, cnt in enumerate(counts):
        out.append((send[li], recv[li], list(thru[f:f + cnt]), list(lands[f:f + cnt])))
        f += cnt
    return out, token


def gather_wait(li, send, recv, shards, lands, after):
    m = len(shards)

    def body(*refs):
        ins, lnd = refs[:m], refs[m:2 * m]
        snd, rcv = refs[2 * m], refs[2 * m + 1]
        x, y, c, me, chips = _place()
        for k in range(m):
            for j, (qx, qy) in enumerate(chips):
                cp = _remote(_row_half(ins[k], c), _row_half(lnd[k].at[2 * qx + qy], c),
                             snd.at[3 * k + j], rcv.at[3 * k + j], (qx, qy, c))
                cp.wait_send()
                cp.wait_recv()

    res = pl.pallas_call(
        body, name=f"gather_wait_{li}",
        out_shape=(*[pltpu.HBM(s.shape, s.dtype) for s in shards],
                   *[pltpu.HBM(s.shape, s.dtype) for s in lands]),
        in_specs=[HBM] * (2 * m) + [SEM, SEM, ANY], out_specs=[HBM] * (2 * m),
        input_output_aliases={k: k for k in range(2 * m)},
        compiler_params=pltpu.CompilerParams(has_side_effects=EFFECT),
    )(*shards, *lands, send, recv, after)
    return list(res[:m]), list(res[m:])


def gather_forward(li, lands):
    m = len(lands)

    def body(*refs):
        outs = refs[m:2 * m]
        send, recv = refs[2 * m:]
        x, y, c, me, chips = _place()
        passed = []
        for k in range(m):
            for j, (qx, qy) in enumerate(chips):
                slab = _row_half(outs[k].at[2 * qx + qy], c)
                cp = _remote(slab, slab, send.at[3 * k + j], recv.at[3 * k + j], (x, y, 1 - c))
                cp.start()
                passed.append(cp)
        for k in range(m):
            for j, (qx, qy) in enumerate(chips):
                slab = _row_half(outs[k].at[2 * qx + qy], 1 - c)
                _remote(slab, slab, send.at[3 * k + j], recv.at[3 * k + j], (x, y, c)).wait_recv()
        for cp in passed:
            cp.wait_send()

    sems = pltpu.SemaphoreType.DMA((3 * m,))
    return pl.pallas_call(
        body, name=f"gather_forward_{li}",
        in_specs=[ANY] * m, out_specs=[ANY] * m,
        out_shape=[jax.ShapeDtypeStruct(s.shape, s.dtype) for s in lands],
        input_output_aliases={k: k for k in range(m)},
        scratch_shapes=[sems, sems],
        compiler_params=pltpu.CompilerParams(has_side_effects=True),
    )(*lands)


def pair_exchange_rows(li, grads):
    n = len(grads)

    def body(*refs):
        ins, outs = refs[:n], refs[n:2 * n]
        send, recv = refs[2 * n:]
        x, y, c, _, _ = _place()
        cps = []
        for k in range(n):
            cp = _remote(_row_half(ins[k], 1 - c), outs[k], send.at[k], recv.at[k], (x, y, 1 - c))
            cp.start()
            cps.append(cp)
        for cp in cps:
            cp.wait()

    sems = pltpu.SemaphoreType.DMA((n,))
    return pl.pallas_call(
        body, name=f"pair_exchange_{li}",
        in_specs=[ANY] * n, out_specs=[ANY] * n,
        out_shape=[jax.ShapeDtypeStruct(g.shape[:-2] + (g.shape[-2] // 2, g.shape[-1]), g.dtype)
                   for g in grads],
        scratch_shapes=[sems, sems],
        compiler_params=pltpu.CompilerParams(has_side_effects=True),
    )(*grads)


def pair_add_rows(ids, grad, recv):
    L, P, hr, C = recv.shape

    def body(ids_ref, a_ref, b_ref, o_ref):
        o_ref[...] = (a_ref[...].astype(F32) + b_ref[...].astype(F32)).astype(o_ref.dtype)

    blk = (None, None, hr, C)
    return pl.pallas_call(
        body, name="pair_add",
        grid_spec=pltpu.PrefetchScalarGridSpec(
            num_scalar_prefetch=1, grid=(L, P),
            in_specs=[pl.BlockSpec(blk, lambda l, p, ids: (l, p, ids[0], 0)),
                      pl.BlockSpec(blk, lambda l, p, ids: (l, p, 0, 0))],
            out_specs=pl.BlockSpec(blk, lambda l, p, ids: (l, p, 0, 0))),
        out_shape=jax.ShapeDtypeStruct(recv.shape, recv.dtype),
        compiler_params=_params(("parallel", "parallel")),
    )(ids, grad, recv)


def reduce_start(li, parts):
    m = len(parts)

    def body(*refs):
        ins, lands = refs[:m], refs[m:2 * m]
        send, recv = refs[2 * m], refs[2 * m + 1]
        token = refs[-1]
        x, y, c, me, chips = _place()
        for k in range(m):
            rows = pl.ds(0, parts[k].shape[0])
            for j, (qx, qy) in enumerate(chips):
                _remote(ins[k].at[rows, 2 * qx + qy], lands[k].at[rows, me],
                        send.at[3 * k + j], recv.at[3 * k + j], (qx, qy, c)).start()
        token[...] = jnp.zeros_like(token)

    sems = pltpu.SemaphoreType.DMA((3 * m,))
    res = pl.pallas_call(
        body, name=f"reduce_start_{li}",
        out_shape=(sems, sems, *[pltpu.HBM(s.shape, s.dtype) for s in parts],
                   *[pltpu.HBM(s.shape, s.dtype) for s in parts], jax.ShapeDtypeStruct((8, 128), F32)),
        in_specs=[HBM] * (2 * m),
        out_specs=(SEM, SEM, *[HBM] * (2 * m), pl.BlockSpec(memory_space=pltpu.VMEM)),
        input_output_aliases={k: 2 + k for k in range(2 * m)},
        compiler_params=pltpu.CompilerParams(has_side_effects=EFFECT),
    )(*[_in_hbm(s) for s in parts], *[_in_hbm(lax.empty(s.shape, s.dtype)) for s in parts])
    return res[0], res[1], list(res[2:2 + m]), list(res[2 + m:2 + 2 * m]), res[-1]


def reduce_wait(li, send, recv, parts, lands, after):
    m = len(parts)

    def body(*refs):
        ins, lnd = refs[:m], refs[m:2 * m]
        snd, rcv = refs[2 * m], refs[2 * m + 1]
        x, y, c, me, chips = _place()
        for k in range(m):
            rows = pl.ds(0, parts[k].shape[0])
            for j, (qx, qy) in enumerate(chips):
                cp = _remote(ins[k].at[rows, 2 * qx + qy], lnd[k].at[rows, 2 * qx + qy],
                             snd.at[3 * k + j], rcv.at[3 * k + j], (qx, qy, c))
                cp.wait_send()
                cp.wait_recv()

    res = pl.pallas_call(
        body, name=f"reduce_wait_{li}",
        out_shape=(*[pltpu.HBM(s.shape, s.dtype) for s in parts],
                   *[pltpu.HBM(s.shape, s.dtype) for s in lands]),
        in_specs=[HBM] * (2 * m) + [SEM, SEM, ANY], out_specs=[HBM] * (2 * m),
        input_output_aliases={k: k for k in range(2 * m)},
        compiler_params=pltpu.CompilerParams(has_side_effects=EFFECT),
    )(*parts, *lands, send, recv, after)
    return list(res[:m]), list(res[m:])


def chip_sum_rows(ids, land, part, gfull, l0):
    L, P, hr, C = land.shape

    def body(ids_ref, land_ref, part_ref, _g, o_ref):
        tot = None
        for q in range(P):
            term = jnp.where(ids_ref[1] == q, part_ref[...], land_ref[q]).astype(F32)
            tot = term if tot is None else tot + term
        o_ref[...] = tot

    return pl.pallas_call(
        body, name="chip_sum",
        grid_spec=pltpu.PrefetchScalarGridSpec(
            num_scalar_prefetch=1, grid=(L,),
            in_specs=[pl.BlockSpec((None, P, hr, C), lambda l, ids: (l, 0, 0, 0)),
                      pl.BlockSpec((None, None, hr, C), lambda l, ids: (l, ids[1], 0, 0)), ANY],
            out_specs=pl.BlockSpec((None, hr, C), lambda l, ids: (l0 + l, ids[0], 0))),
        out_shape=jax.ShapeDtypeStruct(gfull.shape, gfull.dtype),
        input_output_aliases={3: 0},
        compiler_params=_params(("arbitrary",)),
    )(ids, land, part, gfull)


def half_swap_rows(grads):
    n = len(grads)

    def body(*refs):
        outs = refs[n:2 * n]
        send, recv = refs[2 * n:]
        x, y, c, _, _ = _place()
        cps = []
        for k in range(n):
            mine = _row_half(outs[k], c)
            cp = _remote(mine, mine, send.at[k], recv.at[k], (x, y, 1 - c))
            cp.start()
            cps.append(cp)
        for k in range(n):
            theirs = _row_half(outs[k], 1 - c)
            _remote(theirs, theirs, send.at[k], recv.at[k], (x, y, c)).wait_recv()
        for cp in cps:
            cp.wait_send()

    sems = pltpu.SemaphoreType.DMA((n,))
    return pl.pallas_call(
        body, name="half_swap",
        in_specs=[ANY] * n, out_specs=[ANY] * n,
        out_shape=[jax.ShapeDtypeStruct(g.shape, g.dtype) for g in grads],
        input_output_aliases={k: k for k in range(n)},
        scratch_shapes=[sems, sems],
        compiler_params=pltpu.CompilerParams(has_side_effects=True),
    )(*grads)


def all_sum_small(v):
    R = v.shape[0]

    def body(v_ref, o_ref, land, send, recv):
        x, y, c, _, _ = _place()
        me = 4 * x + 2 * y + c
        land[me] = v_ref[...]
        peers = [(px, py, pc) for px in range(2) for py in range(2) for pc in range(2)]
        cps = []
        for k in range(1, 8):
            dev = (x ^ (k >> 2), y ^ ((k >> 1) & 1), c ^ (k & 1))
            cp = _remote(v_ref, land.at[me], send.at[k - 1], recv.at[k - 1], dev)
            cp.start()
            cps.append(cp)
        for k in range(1, 8):
            src = 4 * (x ^ (k >> 2)) + 2 * (y ^ ((k >> 1) & 1)) + (c ^ (k & 1))
            _remote(v_ref, land.at[src], send.at[k - 1], recv.at[k - 1], (x, y, c)).wait_recv()
        for cp in cps:
            cp.wait_send()
        tot = land[0]
        for d in range(1, len(peers)):
            tot = tot + land[d]
        o_ref[...] = tot

    sems = pltpu.SemaphoreType.DMA((7,))
    vm = pl.BlockSpec(memory_space=pltpu.VMEM)
    return pl.pallas_call(
        body, name="all_sum_small", in_specs=[vm], out_specs=vm,
        out_shape=jax.ShapeDtypeStruct(v.shape, F32),
        scratch_shapes=[pltpu.VMEM((8, R, 128), F32), sems, sems],
        compiler_params=pltpu.CompilerParams(has_side_effects=True),
    )(v)


def _row_tile(R):
    for t in range(min(R, 512) // 8 * 8, 7, -8):
        if R % t == 0:
            return t
    return R


def adamw(w, g, m, v):
    lead, (R, C) = w.shape[:-2], w.shape[-2:]
    tr = _row_tile(R)
    c1 = 1.0 / (1.0 - ADAM_B1 ** ADAM_STEP)
    c2 = 1.0 / (1.0 - ADAM_B2 ** ADAM_STEP)

    def body(w_ref, g_ref, m_ref, v_ref, go_ref, d_ref, nm_ref, nv_ref):
        gv = g_ref[...]
        go_ref[...] = gv
        nm = ADAM_B1 * m_ref[...] + (1.0 - ADAM_B1) * gv
        nv = ADAM_B2 * v_ref[...] + (1.0 - ADAM_B2) * (gv * gv)
        nm_ref[...] = nm
        nv_ref[...] = nv
        d_ref[...] = -ADAM_LR * ((nm * c1) / (jnp.sqrt(nv * c2) + ADAM_EPS) + ADAM_WD * w_ref[...])

    def flat(idx):
        l = 0
        for i, n in zip(idx, lead):
            l = l * n + i
        return l

    blk = pl.BlockSpec((None,) * len(lead) + (tr, C), lambda *ix: (*ix, 0))
    gblk = pl.BlockSpec((None, tr, C), lambda *ix: (flat(ix[:-1]), ix[-1], 0))
    osh = jax.ShapeDtypeStruct(w.shape, F32)
    return pl.pallas_call(
        body, name="adamw", grid=(*lead, R // tr),
        in_specs=[blk, gblk, blk, blk], out_specs=[blk, blk, blk, blk], out_shape=[osh, osh, osh, osh],
        compiler_params=_params(("parallel",) * (len(lead) + 1)),
    )(w, g, m, v)


def kernel(x, w_qkv_a, w_o_a, rel_bias, w_qkv_b, w_o_b, ffn_w_gate, ffn_w_up, ffn_w_down, ln_g, ln_b, loss_target, m_w_qkv_a, m_w_o_a, m_rel_bias, m_w_qkv_b, m_w_o_b, m_ffn_w_gate, m_ffn_w_up, m_ffn_w_down, m_ln_g, m_ln_b, v_w_qkv_a, v_w_o_a, v_rel_bias, v_w_qkv_b, v_w_o_b, v_ffn_w_gate, v_ffn_w_up, v_ffn_w_down, v_ln_g, v_ln_b):
    B, S, D = x.shape
    M = B * S
    depth = ffn_w_gate.shape[0]
    n_ffn = 2 * depth
    H = D // HEAD_DIM
    HP = H // 2
    Fs = ffn_w_gate.shape[-1]
    alpha = (2.0 * depth) ** 0.25
    assert S % QB_A == 0 and S % SB_TILE == 0 and rel_bias.shape == (N_REL, H)

    me = 2 * lax.axis_index("x") + lax.axis_index("y")
    ids = jnp.stack([lax.axis_index("c"), me]).astype(jnp.int32)

    def tr(a):
        return jnp.swapaxes(a, -1, -2)

    gate_t, up_t = tr(ffn_w_gate), tr(ffn_w_up)

    def mixer_slots(i):
        wq, wo, mix = (w_qkv_a, w_o_a, "a") if i % 2 == 0 else (w_qkv_b, w_o_b, "b")
        return [(w, (lambda l, i=i: (i // 2,)), 1, (role + mix, i // 2), [((i, 1), role, 0)])
                for role, w in (("q", wq), ("o", wo))]

    def ffn_slots(i, subs):
        first = 0 if subs[0] == 0 else 1
        return [(w, (lambda l, i=i, first=first: (i, first + l)), len(subs), (role, 2 * i + first),
                 [((i, j), role, k) for k, j in enumerate(subs)])
                for role, w in (("gate", gate_t), ("up", up_t), ("down", ffn_w_down))]

    groups = [ffn_slots(0, [0]), mixer_slots(0) + ffn_slots(0, [2])]
    groups += [mixer_slots(i) + ffn_slots(i, [0, 2]) for i in range(1, depth)]

    def group_of(i, j):
        return i + 1 if i else (0 if j == 0 else 1)

    lng_p, lnb_p = gather_weights([ln_g, ln_b])
    lng = jnp.moveaxis(lng_p, 0, 2).reshape(depth, 3, 1, D)
    lnb = jnp.moveaxis(lnb_p, 0, 2).reshape(depth, 3, 1, D)
    placed = [[cast_place(ids, src, lead, L) for src, lead, L, _, _ in grp] for grp in groups]
    in_flight, _ = gather_start([[s for s, _ in grp] for grp in placed],
                                [[z for _, z in grp] for grp in placed], lng_p)

    tab_t = jnp.pad(rel_bias.T, ((0, 0), (0, REL_PAD - N_REL)))
    vr = bias_vec(tab_t).reshape(HP, 2, VR_W)

    xf = x.reshape(M, D)
    xb = xf.astype(BF16)
    saved, weights, landed = [], {}, set()
    for i in range(depth):
        for j in range(3):
            gi = group_of(i, j)
            if gi not in landed:
                landed.add(gi)
                send, recv, thru, lands = in_flight[gi]
                _, lands = gather_wait(gi, send, recv, thru, lands, xf)
                for slot, arr in zip(groups[gi], gather_forward(gi, lands)):
                    for unit, role, l in slot[4]:
                        weights[unit, role] = (arr, l)
            gam, bet = lng[i, j], lnb[i, j]
            if j != 1:
                (wg, l), (wu, _), (wd, _) = (weights[(i, j), r] for r in ("gate", "up", "down"))
                g, u, h = ffn_up(xb, wg, wu, l)
                xo, xob, xhat, rstd = mm_ln(h, wd, l, xf, gam, bet, alpha, 0.5, True)
                saved.append(("ffn", i, l, xb, g, u, h, xhat, rstd, gam))
            else:
                wq, wo = weights[(i, 1), "q"][0], weights[(i, 1), "o"][0]
                qkv = qkv_proj(xb, wq, 0)
                if i % 2 == 0:
                    o, ntot = attn_a_fwd(qkv, vr, B, S), None
                else:
                    o, ntot = attn_b_fwd(qkv, B, S)
                xo, xob, xhat, rstd = mm_ln(o, wo, 0, xf, gam, bet, alpha, 1.0, False)
                saved.append(("a" if i % 2 == 0 else "b", i, 0, xb, qkv, o, ntot, xhat, rstd, gam))
            xf, xb = xo, xob

    dy, loss_part = loss_head(xf, loss_target.reshape(M, D))
    loss = lax.psum(loss_part[0, 0], ("x", "y", "c"))

    Cq, Ro = w_qkv_a.shape[-1], w_o_a.shape[1]
    dgam = lax.empty((3 * depth, 1, D), F32)
    dbet = lax.empty((3 * depth, 1, D), F32)
    dvrs, reducing = [], {}
    gbufs = [[lax.empty((L, N_CHIPS) + src.shape[-2:], BF16) for src, _, L, _, _ in grp] for grp in groups]
    grad_of = {(unit, role): (gi, si, l) for gi, grp in enumerate(groups) for si, slot in enumerate(grp)
               for unit, role, l in slot[4]}

    def ln_of(sub):
        rec = saved[sub]
        return rec[7], rec[8], rec[9], (0.5 if rec[0] == "ffn" else 1.0)

    no_token = jnp.zeros((8, 128), F32)
    token = no_token
    dpre, dyb, dgam, dbet = ln_bwd(dy, *ln_of(3 * depth - 1), dgam, dbet, 3 * depth - 1)
    for sub in reversed(range(3 * depth)):
        i, j = divmod(sub, 3)
        kind, _, l, xb_in, t1, t2, t3, _, _, _ = saved[sub]
        ln = (*ln_of(sub - 1), dgam, dbet, sub - 1) if sub else None
        if kind == "ffn":
            g, u, h = t1, t2, t3
            (gi, sg, _), (_, su, _), (_, sd, _) = (grad_of[(i, j), r] for r in ("gate", "up", "down"))
            wg, wu, wd = (weights[(i, j), r][0] for r in ("gate", "up", "down"))
            dg, du, gbufs[gi][sd], gbufs[gi][sg], gbufs[gi][su] = ffn_bwd(
                gbufs[gi][sd], gbufs[gi][sg], gbufs[gi][su], l, xb_in, dyb, wd, g, u, h, token)
            out = ffn_dx(dg, du, wg, wu, l, dpre, alpha, ln)
        else:
            qkv, o = t1, t2
            (gi, sq, _), (_, so, _) = grad_of[(i, j), "q"], grad_of[(i, j), "o"]
            wq, wo = weights[(i, j), "q"][0], weights[(i, j), "o"][0]
            do = o_proj_bwd(dyb, wo, l, token)
            gbufs[gi][so] = dw_o(gbufs[gi][so], l, o, dyb)
            if kind == "a":
                dqkv, dvr = attn_a_bwd(qkv, vr, do, B, S)
                dvrs.append(dvr.reshape(B, H, VR_W))
            else:
                dqkv = attn_b_bwd(qkv, do, t3, B, S)
            gbufs[gi][sq] = dw_qkv(gbufs[gi][sq], l, xb_in, dqkv)
            out = qkv_proj_bwd(dqkv, wq, l, dpre, alpha, ln)
        if ln is None:
            dy = out
        else:
            dpre, dyb, dgam, dbet = out
        token = no_token
        gi = group_of(i, j)
        if sub == 0 or group_of(*divmod(sub - 1, 3)) != gi:
            from_sib = pair_exchange_rows(gi, gbufs[gi])
            parts = [pair_add_rows(ids, g_, r_) for g_, r_ in zip(gbufs[gi], from_sib)]
            reducing[gi] = reduce_start(gi, parts)
            token = reducing[gi][4]
    grad_x = dy.reshape(B, S, D)

    la, lb = w_qkv_a.shape[0], w_qkv_b.shape[0]
    full = {"qa": lax.empty((la, D, Cq), F32), "oa": lax.empty((la, Ro, D), F32),
            "qb": lax.empty((lb, D, Cq), F32), "ob": lax.empty((lb, Ro, D), F32),
            "gate": lax.empty((n_ffn, Fs, D), F32), "up": lax.empty((n_ffn, Fs, D), F32),
            "down": lax.empty((n_ffn, Fs, D), F32)}
    for gi in reversed(range(len(groups))):
        send, recv, parts, lands, _ = reducing[gi]
        parts, lands = reduce_wait(gi, send, recv, parts, lands, dy)
        for (_, _, _, (name, l0), _), part, land in zip(groups[gi], parts, lands):
            full[name] = chip_sum_rows(ids, land, part, full[name], l0)
    g_qa, g_oa, g_qb, g_ob, g_gate, g_up, g_down = half_swap_rows(
        [full[k] for k in ("qa", "oa", "qb", "ob", "gate", "up", "down")])

    d_tab_t = bias_vec_bwd(jnp.concatenate(dvrs, axis=0))
    small = jnp.concatenate([d_tab_t.reshape(-1), dgam.reshape(-1), dbet.reshape(-1)])
    n_small = small.shape[0]
    rows = -(-n_small // (8 * 128)) * 8
    tot = all_sum_small(jnp.pad(small, (0, rows * 128 - n_small)).reshape(rows, 128)).reshape(-1)
    n_tab, n_ln = H * REL_PAD, 3 * depth * D
    g_rel = tot[:n_tab].reshape(H, REL_PAD)[:, :N_REL].T
    ln_cols = D // N_CHIPS

    def ln_shard(flat):
        return lax.dynamic_slice_in_dim(flat.reshape(depth, 3, D), me * ln_cols, ln_cols, axis=2)

    g_lng = ln_shard(tot[n_tab:n_tab + n_ln])
    g_lnb = ln_shard(tot[n_tab + n_ln:n_tab + 2 * n_ln])

    def upd(w, g, m, v):
        if w.ndim == 2:
            return tuple(a[0] for a in adamw(w[None], g, m[None], v[None]))
        return adamw(w, g, m, v)

    res = [
        upd(w_qkv_a, g_qa, m_w_qkv_a, v_w_qkv_a),
        upd(w_o_a, g_oa, m_w_o_a, v_w_o_a),
        upd(rel_bias, g_rel.reshape(1, N_REL, H), m_rel_bias, v_rel_bias),
        upd(w_qkv_b, g_qb, m_w_qkv_b, v_w_qkv_b),
        upd(w_o_b, g_ob, m_w_o_b, v_w_o_b),
        tuple(tr(a) for a in upd(gate_t, g_gate, tr(m_ffn_w_gate), tr(v_ffn_w_gate))),
        tuple(tr(a) for a in upd(up_t, g_up, tr(m_ffn_w_up), tr(v_ffn_w_up))),
        upd(ffn_w_down, g_down, m_ffn_w_down, v_ffn_w_down),
        upd(ln_g, g_lng, m_ln_g, v_ln_g),
        upd(ln_b, g_lnb, m_ln_b, v_ln_b),
    ]
    grads = [r[0] for r in res]
    deltas = [r[1] for r in res]
    new_m = [r[2] for r in res]
    new_v = [r[3] for r in res]
    return (loss, grad_x, *grads, *deltas, *new_m, *new_v)
```

```python
import math

import jax
import jax.numpy as jnp
from jax import lax
from jax.experimental import pallas as pl
from jax.experimental.pallas import tpu as pltpu

F32 = jnp.float32
BF16 = jnp.bfloat16
MESH = pl.DeviceIdType.MESH

N_CHIPS = 4
HEAD_DIM = 64
CHUNK = 64
LEFT_CHUNKS = 8
LOOKBACK = LEFT_CHUNKS * CHUNK
REL_CLIP = 128
N_REL = 2 * REL_CLIP + 1
REL_PAD = 384
SB_TILE = 256
QB_A = 256
KW_A = QB_A + LOOKBACK
VR_W = 1024
VR_C0 = KW_A - 1
assert math.frexp(HEAD_DIM ** -0.5)[0] == 0.5
LN_EPS = 1e-5
ADAM_LR, ADAM_B1, ADAM_B2, ADAM_EPS, ADAM_WD, ADAM_STEP = 0.001, 0.9, 0.999, 1e-08, 0.01, 10
NEG = -1e30
VMEM_LIMIT = 56 * 1024 * 1024

NT_DIMS = (((1,), (1,)), ((), ()))
TN_DIMS = (((0,), (0,)), ((), ()))
ANY = pl.BlockSpec(memory_space=pl.ANY)


def _params(sem=None):
    return pltpu.CompilerParams(dimension_semantics=sem, vmem_limit_bytes=VMEM_LIMIT)


def _tile(n, pref):
    t = min(n, pref)
    assert n % t == 0, (n, pref)
    return t


def _sigmoid(z):
    return 1.0 / (1.0 + jnp.exp(-z))


def _mm_call(name, operands, in_specs, out_shape, out_spec, grid, dims_list, acc_shape,
             add_coef=None, aliases=None):
    n_pairs = len(dims_list)
    nk = grid[-1]
    has_add = add_coef is not None
    n_alias = len(aliases) if aliases else 0

    def body(*refs):
        pair_refs = refs[:2 * n_pairs]
        pos = 2 * n_pairs
        add_ref = refs[pos] if has_add else None
        pos += (1 if has_add else 0) + n_alias
        o_ref = refs[pos]
        acc_ref = refs[pos + 1] if nk > 1 else None

        def product():
            part = None
            for i, dims in enumerate(dims_list):
                d = lax.dot_general(pair_refs[2 * i][...], pair_refs[2 * i + 1][...], dims,
                                    preferred_element_type=F32)
                part = d if part is None else part + d
            return part

        def finish(r):
            if has_add:
                r = r + add_coef * add_ref[...]
            o_ref[...] = r.astype(o_ref.dtype)

        if nk == 1:
            finish(product())
        else:
            k = pl.program_id(len(grid) - 1)

            @pl.when(k == 0)
            def _():
                acc_ref[...] = jnp.zeros_like(acc_ref)

            acc_ref[...] += product()

            @pl.when(k == nk - 1)
            def _():
                finish(acc_ref[...])

    sem = ("parallel",) * (len(grid) - 1) + ("arbitrary",)
    return pl.pallas_call(
        body, name=name, grid=grid, in_specs=in_specs, out_specs=out_spec, out_shape=out_shape,
        scratch_shapes=[pltpu.VMEM(acc_shape, F32)] if nk > 1 else [],
        input_output_aliases=aliases or {},
        compiler_params=_params(sem),
    )(*operands)


def qkv_proj(xb, w, l):
    M, D = xb.shape
    C = w.shape[-1]
    tm = _tile(M, 1024)
    return _mm_call(
        "qkv_proj", (xb, w),
        [pl.BlockSpec((tm, D), lambda p, i, k: (i, 0)),
         pl.BlockSpec((None, None, D, C), lambda p, i, k: (p, l, 0, 0))],
        jax.ShapeDtypeStruct((M, N_CHIPS * C), BF16),
        pl.BlockSpec((tm, C), lambda p, i, k: (i, p)),
        (N_CHIPS, M // tm, 1), [(((1,), (0,)), ((), ()))], None)


def o_proj_bwd(dyb, w, l, after):
    M, D = dyb.shape
    R = w.shape[2]
    tm = _tile(M, 512)

    def body(a_ref, w_ref, _after, o_ref):
        o_ref[...] = lax.dot_general(a_ref[...], w_ref[...], NT_DIMS,
                                     preferred_element_type=F32).astype(o_ref.dtype)

    return pl.pallas_call(
        body, name="o_proj_bwd", grid=(N_CHIPS, M // tm),
        in_specs=[pl.BlockSpec((tm, D), lambda p, i: (i, 0)),
                  pl.BlockSpec((None, None, R, D), lambda p, i: (p, l, 0, 0)),
                  pl.BlockSpec(after.shape, lambda p, i: (0, 0))],
        out_specs=pl.BlockSpec((tm, R), lambda p, i: (i, p)),
        out_shape=jax.ShapeDtypeStruct((M, N_CHIPS * R), BF16),
        compiler_params=_params(("parallel", "parallel")),
    )(dyb, w, after)


def qkv_proj_bwd(dqkv, w, l, dpre, alpha, ln=None):
    _, M, D = dqkv.shape
    C = w.shape[3]
    tm = _tile(M, 512)
    T = math.gcd(D, C)

    def body(a_ref, w_ref, add_ref, *rest):
        acc = alpha * add_ref[...]
        for t in range(3 * D // T):
            pa, ca = divmod(t * T, D)
            pw, cw = divmod(t * T, C)
            acc = acc + lax.dot_general(a_ref[pa, :, ca:ca + T], w_ref[pw, :, cw:cw + T], NT_DIMS,
                                        preferred_element_type=F32)
        if ln is None:
            rest[0][...] = acc
        else:
            xh_ref, rs_ref, g_ref, _dg, _db, dp_ref, db16_ref, dg_out, db_out = rest
            _ln_bwd_tile(acc, xh_ref, rs_ref, g_ref, ln[3], dp_ref, db16_ref, dg_out, db_out)

    row = pl.BlockSpec((tm, D), lambda i: (i, 0))
    ins = [pl.BlockSpec((3, tm, D), lambda i: (0, i, 0)),
           pl.BlockSpec((N_CHIPS, None, D, C), lambda i: (0, l, 0, 0)), row]
    if ln is None:
        return pl.pallas_call(
            body, name="qkv_proj_bwd", grid=(M // tm,), in_specs=ins,
            out_specs=row, out_shape=jax.ShapeDtypeStruct((M, D), F32),
            compiler_params=_params(("parallel",)),
        )(dqkv, w, dpre)
    xhat, rstd, gam, _, dgam, dbet, ln_row = ln
    specs = _ln_bwd_specs(M, D, tm, ln_row, dgam, dbet)
    return pl.pallas_call(
        body, name="qkv_proj_bwd_ln", grid=(M // tm,), in_specs=ins + specs["in"],
        out_specs=specs["out"], out_shape=specs["out_shape"],
        input_output_aliases={6: 2, 7: 3},
        compiler_params=_params(("arbitrary",)),
    )(dqkv, w, dpre, xhat, rstd, gam, dgam, dbet)


def ffn_dx(dg, du, wgt, wut, l, dpre, alpha, ln=None):
    _, M, Fs = dg.shape
    D = wgt.shape[3]
    tm = _tile(M, 256)

    def body(dg_ref, wg_ref, du_ref, wu_ref, add_ref, *rest):
        acc = alpha * add_ref[...]
        for p in range(N_CHIPS):
            acc = acc + jnp.dot(dg_ref[p], wg_ref[p], preferred_element_type=F32)
            acc = acc + jnp.dot(du_ref[p], wu_ref[p], preferred_element_type=F32)
        if ln is None:
            rest[0][...] = acc
        else:
            xh_ref, rs_ref, g_ref, _dg, _db, dp_ref, db16_ref, dg_out, db_out = rest
            _ln_bwd_tile(acc, xh_ref, rs_ref, g_ref, ln[3], dp_ref, db16_ref, dg_out, db_out)

    act = pl.BlockSpec((N_CHIPS, tm, Fs), lambda i: (0, i, 0))
    wsp = pl.BlockSpec((N_CHIPS, None, Fs, D), lambda i: (0, l, 0, 0))
    row = pl.BlockSpec((tm, D), lambda i: (i, 0))
    if ln is None:
        return pl.pallas_call(
            body, name="ffn_dx", grid=(M // tm,),
            in_specs=[act, wsp, act, wsp, row],
            out_specs=row, out_shape=jax.ShapeDtypeStruct((M, D), F32),
            compiler_params=_params(("parallel",)),
        )(dg, wgt, du, wut, dpre)
    xhat, rstd, gam, _, dgam, dbet, ln_row = ln
    specs = _ln_bwd_specs(M, D, tm, ln_row, dgam, dbet)
    return pl.pallas_call(
        body, name="ffn_dx_ln", grid=(M // tm,),
        in_specs=[act, wsp, act, wsp, row] + specs["in"],
        out_specs=specs["out"], out_shape=specs["out_shape"],
        input_output_aliases={8: 2, 9: 3},
        compiler_params=_params(("arbitrary",)),
    )(dg, wgt, du, wut, dpre, xhat, rstd, gam, dgam, dbet)


def _dw_call(name, buf, l, a, b, a_spec, b_spec, M, tk):
    _, _, R, C = buf.shape
    return _mm_call(
        name, (a, b, buf),
        [a_spec, b_spec, ANY],
        jax.ShapeDtypeStruct(buf.shape, buf.dtype),
        pl.BlockSpec((None, None, R, C), lambda p, k: (l, p, 0, 0)),
        (N_CHIPS, M // tk), [TN_DIMS], (R, C), aliases={2: 0})


def ffn_bwd(gdown, ggate, gup, l, xb, dyb, wd, g, u, h, after):
    M, D = dyb.shape
    Fs = wd.shape[2]
    tm = _tile(M, 512)
    n = M // tm

    def body(x_ref, dy_ref, wd_ref, g_ref, u_ref, h_ref, _gd, _gg, _gu, _after,
             dg_ref, du_ref, gd_ref, gg_ref, gu_ref, acc_d, acc_g, acc_u):
        i = pl.program_id(1)

        @pl.when(i == 0)
        def _():
            acc_d[...] = jnp.zeros_like(acc_d)
            acc_g[...] = jnp.zeros_like(acc_g)
            acc_u[...] = jnp.zeros_like(acc_u)

        dy = dy_ref[...]
        dh = lax.dot_general(dy, wd_ref[...], NT_DIMS, preferred_element_type=F32)
        gf = g_ref[...].astype(F32)
        sig = _sigmoid(gf)
        silu = gf * sig
        dg = (dh * u_ref[...].astype(F32) * (sig * (1.0 + gf - silu))).astype(BF16)
        du = (dh * silu).astype(BF16)
        dg_ref[...] = dg
        du_ref[...] = du
        x = x_ref[...]
        acc_d[...] += lax.dot_general(h_ref[...], dy, TN_DIMS, preferred_element_type=F32)
        acc_g[...] += lax.dot_general(dg, x, TN_DIMS, preferred_element_type=F32)
        acc_u[...] += lax.dot_general(du, x, TN_DIMS, preferred_element_type=F32)

        @pl.when(i == n - 1)
        def _():
            gd_ref[...] = acc_d[...].astype(gd_ref.dtype)
            gg_ref[...] = acc_g[...].astype(gg_ref.dtype)
            gu_ref[...] = acc_u[...].astype(gu_ref.dtype)

    row = pl.BlockSpec((tm, D), lambda p, i: (i, 0))
    act = pl.BlockSpec((None, tm, Fs), lambda p, i: (p, i, 0))
    ash = jax.ShapeDtypeStruct((N_CHIPS, M, Fs), BF16)
    w_blk = pl.BlockSpec((None, None, Fs, D), lambda p, i: (l, p, 0, 0))
    return pl.pallas_call(
        body, name="ffn_bwd", grid=(N_CHIPS, n),
        in_specs=[row, row, pl.BlockSpec((None, None, Fs, D), lambda p, i: (p, l, 0, 0)),
                  act, act, act, ANY, ANY, ANY, pl.BlockSpec(after.shape, lambda p, i: (0, 0))],
        out_specs=[act, act, w_blk, w_blk, w_blk],
        out_shape=[ash, ash] + [jax.ShapeDtypeStruct(b.shape, b.dtype) for b in (gdown, ggate, gup)],
        scratch_shapes=[pltpu.VMEM((Fs, D), F32), pltpu.VMEM((Fs, D), F32), pltpu.VMEM((Fs, D), F32)],
        input_output_aliases={6: 2, 7: 3, 8: 4},
        compiler_params=_params(("parallel", "arbitrary")),
    )(xb, dyb, wd, g, u, h, gdown, ggate, gup, after)


def dw_qkv(buf, l, xb, dqkv):
    M, D = xb.shape
    C = buf.shape[-1]
    tk = _tile(M, 512)
    n = M // tk
    T = math.gcd(D, C)
    nt = 3 * D // T

    def body(x_ref, b_ref, _buf, o_ref, acc_ref):
        k = pl.program_id(0)

        @pl.when(k == 0)
        def _():
            acc_ref[...] = jnp.zeros_like(acc_ref)

        xt = x_ref[...].astype(F32).T.astype(BF16)
        for t in range(nt):
            pa, ca = divmod(t * T, D)
            acc_ref[t] += jnp.dot(xt, b_ref[pa, :, ca:ca + T], preferred_element_type=F32)

        @pl.when(k == n - 1)
        def _():
            for t in range(nt):
                pw, cw = divmod(t * T, C)
                o_ref[pw, :, cw:cw + T] = acc_ref[t].astype(o_ref.dtype)

    return pl.pallas_call(
        body, name="dw_qkv", grid=(n,),
        in_specs=[pl.BlockSpec((tk, D), lambda k: (k, 0)),
                  pl.BlockSpec((3, tk, D), lambda k: (0, k, 0)), ANY],
        out_specs=pl.BlockSpec((None, N_CHIPS, D, C), lambda k: (l, 0, 0, 0)),
        out_shape=jax.ShapeDtypeStruct(buf.shape, buf.dtype),
        scratch_shapes=[pltpu.VMEM((nt, D, T), F32)],
        input_output_aliases={2: 0},
        compiler_params=_params(("arbitrary",)),
    )(xb, dqkv, buf)


def dw_o(buf, l, o, dyb):
    M, D = dyb.shape
    R = buf.shape[2]
    tk = _tile(M, 1024)
    return _dw_call("dw_o", buf, l, o, dyb,
                    pl.BlockSpec((tk, R), lambda p, k: (k, p)),
                    pl.BlockSpec((tk, D), lambda p, k: (k, 0)), M, tk)


def ffn_up(xb, wgt, wut, l):
    M, D = xb.shape
    Fs = wgt.shape[2]
    tm = _tile(M, 1024)

    def body(x_ref, wg_ref, wu_ref, g_ref, u_ref, h_ref):
        x = x_ref[...]
        g = lax.dot_general(x, wg_ref[...], NT_DIMS, preferred_element_type=F32)
        u = lax.dot_general(x, wu_ref[...], NT_DIMS, preferred_element_type=F32)
        g_ref[...] = g.astype(BF16)
        u_ref[...] = u.astype(BF16)
        h_ref[...] = (g * _sigmoid(g) * u).astype(BF16)

    wsp = pl.BlockSpec((None, None, Fs, D), lambda p, i: (p, l, 0, 0))
    osp = pl.BlockSpec((None, tm, Fs), lambda p, i: (p, i, 0))
    osh = jax.ShapeDtypeStruct((N_CHIPS, M, Fs), BF16)
    return pl.pallas_call(
        body, name="ffn_up", grid=(N_CHIPS, M // tm),
        in_specs=[pl.BlockSpec((tm, D), lambda p, i: (i, 0)), wsp, wsp],
        out_specs=[osp, osp, osp], out_shape=[osh, osh, osh],
        compiler_params=_params(("parallel", "parallel")),
    )(xb, wgt, wut)


def mm_ln(a, w, l, x, gam, bet, alpha, scale, a_piece_major):
    M, D = x.shape
    R = w.shape[2]
    tm = _tile(M, 512)
    if a_piece_major:
        a_spec = pl.BlockSpec((N_CHIPS, tm, R), lambda i: (0, i, 0))
    else:
        a_spec = pl.BlockSpec((tm, N_CHIPS * R), lambda i: (i, 0))

    def body(a_ref, w_ref, x_ref, g_ref, b_ref, xo_ref, xb_ref, xh_ref, rs_ref):
        y = None
        for p in range(N_CHIPS):
            a = a_ref[p] if a_piece_major else a_ref[:, p * R:(p + 1) * R]
            d = jnp.dot(a, w_ref[p], preferred_element_type=F32)
            y = d if y is None else y + d
        pre = alpha * x_ref[...] + scale * y
        mu = jnp.mean(pre, axis=-1, keepdims=True)
        cen = pre - mu
        var = jnp.mean(cen * cen, axis=-1, keepdims=True)
        rstd = lax.rsqrt(var + LN_EPS)
        xhat = cen * rstd
        out = xhat * g_ref[...] + b_ref[...]
        xo_ref[...] = out
        xb_ref[...] = out.astype(BF16)
        xh_ref[...] = xhat
        rs_ref[...] = rstd

    row = pl.BlockSpec((tm, D), lambda i: (i, 0))
    vec = pl.BlockSpec((1, D), lambda i: (0, 0))
    return pl.pallas_call(
        body, name="mm_ln", grid=(M // tm,),
        in_specs=[a_spec, pl.BlockSpec((N_CHIPS, None, R, D), lambda i: (0, l, 0, 0)), row, vec, vec],
        out_specs=[row, row, row, pl.BlockSpec((tm, 1), lambda i: (i, 0))],
        out_shape=[jax.ShapeDtypeStruct((M, D), F32), jax.ShapeDtypeStruct((M, D), BF16),
                   jax.ShapeDtypeStruct((M, D), F32), jax.ShapeDtypeStruct((M, 1), F32)],
        compiler_params=_params(("parallel",)),
    )(a, w, x, gam, bet)


def ln_bwd(dy, xhat, rstd, gam, scale, dgam, dbet, row):
    M, D = dy.shape
    tm = _tile(M, 512)

    def body(dy_ref, xh_ref, rs_ref, g_ref, _dg, _db, dp_ref, db16_ref, dg_ref, dbt_ref):
        _ln_bwd_tile(dy_ref[...], xh_ref, rs_ref, g_ref, scale, dp_ref, db16_ref, dg_ref, dbt_ref)

    specs = _ln_bwd_specs(M, D, tm, row, dgam, dbet)
    return pl.pallas_call(
        body, name="ln_bwd", grid=(M // tm,),
        in_specs=[pl.BlockSpec((tm, D), lambda i: (i, 0))] + specs["in"],
        out_specs=specs["out"], out_shape=specs["out_shape"],
        input_output_aliases={4: 2, 5: 3},
        compiler_params=_params(("arbitrary",)),
    )(dy, xhat, rstd, gam, dgam, dbet)


def _ln_bwd_tile(dy_v, xh_ref, rs_ref, g_ref, scale, dp_ref, db16_ref, dg_ref, dbt_ref):
    i = pl.program_id(0)
    xh = xh_ref[...]
    dxh = dy_v * g_ref[...]
    m1 = jnp.mean(dxh, axis=-1, keepdims=True)
    m2 = jnp.mean(dxh * xh, axis=-1, keepdims=True)
    dpre = rs_ref[...] * (dxh - m1 - xh * m2)
    dp_ref[...] = dpre
    db16_ref[...] = (scale * dpre).astype(BF16)
    dgp = jnp.sum(dy_v * xh, axis=0, keepdims=True)
    dbp = jnp.sum(dy_v, axis=0, keepdims=True)

    @pl.when(i == 0)
    def _():
        dg_ref[...] = dgp
        dbt_ref[...] = dbp

    @pl.when(i > 0)
    def _():
        dg_ref[...] += dgp
        dbt_ref[...] += dbp


def _ln_bwd_specs(M, D, tm, row, dgam, dbet):
    tok = pl.BlockSpec((tm, D), lambda i: (i, 0))
    vec = pl.BlockSpec((1, D), lambda i: (0, 0))
    acc = pl.BlockSpec((None, 1, D), lambda i: (row, 0, 0))
    return {"in": [tok, pl.BlockSpec((tm, 1), lambda i: (i, 0)), vec, ANY, ANY],
            "out": [tok, tok, acc, acc],
            "out_shape": [jax.ShapeDtypeStruct((M, D), F32), jax.ShapeDtypeStruct((M, D), BF16),
                          jax.ShapeDtypeStruct(dgam.shape, F32), jax.ShapeDtypeStruct(dbet.shape, F32)]}


def loss_head(y, tgt):
    M, D = y.shape
    tm = _tile(M, 512)
    n = M // tm

    def body(y_ref, t_ref, dy_ref, l_ref, acc_ref):
        i = pl.program_id(0)
        e = y_ref[...] - t_ref[...]
        dy_ref[...] = e * (1.0 / D)
        part = jnp.sum(e * e, axis=0, keepdims=True)

        @pl.when(i == 0)
        def _():
            acc_ref[...] = part

        @pl.when(i > 0)
        def _():
            acc_ref[...] += part

        @pl.when(i == n - 1)
        def _():
            l_ref[...] = (0.5 / D) * jnp.sum(acc_ref[...], axis=1, keepdims=True)

    row = pl.BlockSpec((tm, D), lambda i: (i, 0))
    return pl.pallas_call(
        body, name="loss_head", grid=(n,),
        in_specs=[row, row],
        out_specs=[row, pl.BlockSpec((1, 1), lambda i: (0, 0))],
        out_shape=[jax.ShapeDtypeStruct((M, D), F32), jax.ShapeDtypeStruct((1, 1), F32)],
        scratch_shapes=[pltpu.VMEM((1, D), F32)],
        compiler_params=_params(("arbitrary",)),
    )(y, tgt)


def _rel_onehot_t():
    r = lax.broadcasted_iota(jnp.int32, (REL_PAD, VR_W), 0)
    n = lax.broadcasted_iota(jnp.int32, (REL_PAD, VR_W), 1)
    idx = jnp.clip(VR_C0 - n, -REL_CLIP, REL_CLIP) + REL_CLIP
    return (r == idx).astype(F32)


def bias_vec(tab_t):
    H = tab_t.shape[0]

    def body(t_ref, o_ref):
        o_ref[...] = jnp.dot(t_ref[...], _rel_onehot_t(), precision=lax.Precision.HIGHEST,
                             preferred_element_type=F32)

    return pl.pallas_call(
        body, name="bias_vec", out_shape=jax.ShapeDtypeStruct((H, VR_W), F32),
        compiler_params=_params(),
    )(tab_t)


def bias_vec_bwd(dvr):
    n, H, _ = dvr.shape

    def body(d_ref, o_ref):
        tot = d_ref[0]
        for i in range(1, n):
            tot = tot + d_ref[i]
        o_ref[...] = lax.dot_general(tot, _rel_onehot_t(), NT_DIMS, precision=lax.Precision.HIGHEST,
                                     preferred_element_type=F32)

    return pl.pallas_call(
        body, name="bias_vec_bwd", out_shape=jax.ShapeDtypeStruct((H, REL_PAD), F32),
        compiler_params=_params(),
    )(dvr)


def _a_bias_mask(vr_row):
    xb = jnp.broadcast_to(vr_row, (QB_A, VR_W))
    tile = pltpu.roll(xb, VR_W - (QB_A - 1), 1, stride=1, stride_axis=0)[:, :KW_A]
    qc = lax.broadcasted_iota(jnp.int32, (QB_A, KW_A), 0) // CHUNK
    kc = lax.broadcasted_iota(jnp.int32, (QB_A, KW_A), 1) // CHUNK
    valid = (kc >= qc) & (kc <= qc + LEFT_CHUNKS)
    return jnp.where(valid, tile, NEG)


def _a_diag_sums(db_acc, h):
    acc8 = None
    for a in range(QB_A // 8):
        grp = db_acc[h, 8 * a:8 * a + 8, :]
        shift = QB_A - 8 - 8 * a
        if shift:
            grp = pltpu.roll(grp, shift, 1)
        acc8 = grp if acc8 is None else acc8 + grp
    sub = lax.broadcasted_iota(jnp.int32, (8, VR_W), 0)
    tot = jnp.zeros((8, VR_W), F32)
    for b in range(8):
        moved = pltpu.roll(acc8, 7 - b, 1) if b < 7 else acc8
        tot = tot + jnp.where(sub == b, moved, 0.0)
    return jnp.sum(tot, axis=0, keepdims=True)


def _a_blocks(S):
    out = []
    for qi in range(S // QB_A):
        q0 = qi * QB_A
        ks = max(0, q0 - LOOKBACK)
        out.append((q0, ks, q0 + QB_A, ks - (q0 - LOOKBACK)))
    return out


def _head_specs(S, HP):
    q = pl.BlockSpec((S, 2 * HEAD_DIM), lambda b, hp: (b, hp))
    k = pl.BlockSpec((S, 2 * HEAD_DIM), lambda b, hp: (b, HP + hp))
    v = pl.BlockSpec((S, 2 * HEAD_DIM), lambda b, hp: (b, 2 * HP + hp))
    return q, k, v


def attn_a_fwd(qkv, vr, B, S):
    D = qkv.shape[1] // 3
    HP = D // (2 * HEAD_DIM)
    scale = HEAD_DIM ** -0.5
    blocks = _a_blocks(S)

    def body(q_ref, k_ref, v_ref, vr_ref, o_ref):
        heads = [slice(h * HEAD_DIM, (h + 1) * HEAD_DIM) for h in range(2)]
        bms = [_a_bias_mask(vr_ref[h:h + 1, :]) for h in range(2)]
        for pair in range(0, len(blocks), 2):
            chains = [(slice(q0, q0 + QB_A), ks, ke, joff, hs, bm)
                      for (q0, ks, ke, joff) in blocks[pair:pair + 2] for hs, bm in zip(heads, bms)]
            ss = [lax.dot_general(q_ref[rows, hs] * scale, k_ref[ks:ke, hs], NT_DIMS,
                                  preferred_element_type=F32) + bm[:, joff:]
                  for rows, ks, ke, joff, hs, bm in chains]
            ps = [jnp.exp(s - jnp.max(s, axis=-1, keepdims=True)) for s in ss]
            for (rows, ks, ke, _, hs, _), p in zip(chains, ps):
                den = jnp.sum(p, axis=-1, keepdims=True)
                o = jnp.dot(p.astype(BF16), v_ref[ks:ke, hs], preferred_element_type=F32) / den
                o_ref[rows, hs] = o.astype(BF16)

    qs, ks_, vs = _head_specs(S, HP)
    return pl.pallas_call(
        body, name="attn_a_fwd", grid=(B, HP),
        in_specs=[qs, ks_, vs, pl.BlockSpec((None, 2, VR_W), lambda b, hp: (hp, 0, 0))],
        out_specs=pl.BlockSpec((S, 2 * HEAD_DIM), lambda b, hp: (b, hp)),
        out_shape=jax.ShapeDtypeStruct((B * S, D), BF16),
        compiler_params=_params(("parallel", "parallel")),
    )(qkv, qkv, qkv, vr)


def attn_a_bwd(qkv, vr, do, B, S):
    D = qkv.shape[1] // 3
    HP = D // (2 * HEAD_DIM)
    scale = HEAD_DIM ** -0.5
    blocks = _a_blocks(S)

    def body(q_ref, k_ref, v_ref, vr_ref, do_ref, dqkv_ref, dvr_ref, dkt_acc, dvt_acc, db_acc):
        dkt_acc[...] = jnp.zeros_like(dkt_acc)
        dvt_acc[...] = jnp.zeros_like(dvt_acc)
        db_acc[...] = jnp.zeros_like(db_acc)
        heads = [slice(h * HEAD_DIM, (h + 1) * HEAD_DIM) for h in range(2)]
        bms = [_a_bias_mask(vr_ref[h:h + 1, :]) for h in range(2)]
        for pair in range(0, len(blocks), 2):
            chains, qts, dots = [], [], []
            for (q0, ks, ke, joff) in blocks[pair:pair + 2]:
                rows = slice(q0, q0 + QB_A)
                qt_pair = (q_ref[rows, :] * scale).astype(F32).T.astype(BF16)
                dot_pair = do_ref[rows, :].astype(F32).T.astype(BF16)
                for h, hs in enumerate(heads):
                    chains.append((rows, ks, ke, joff, h, hs))
                    qts.append(qt_pair[hs, :])
                    dots.append(dot_pair[hs, :])
            ss = [lax.dot_general(q_ref[rows, hs] * scale, k_ref[ks:ke, hs], NT_DIMS,
                                  preferred_element_type=F32) + bms[h][:, joff:]
                  for rows, ks, ke, joff, h, hs in chains]
            dps = [lax.dot_general(do_ref[rows, hs], v_ref[ks:ke, hs], NT_DIMS, preferred_element_type=F32)
                   for rows, ks, ke, _, _, hs in chains]
            ps, dsbs = [], []
            for (_, _, _, joff, h, _), s, dp in zip(chains, ss, dps):
                e = jnp.exp(s - jnp.max(s, axis=-1, keepdims=True))
                p = e / jnp.sum(e, axis=-1, keepdims=True)
                ds = p * (dp - jnp.sum(p * dp, axis=-1, keepdims=True))
                db_acc[h, :, joff:KW_A] += ds
                ps.append(p.astype(BF16))
                dsbs.append(ds.astype(BF16))
            for (rows, ks, ke, _, _, hs), p, dsb, qt, dot_ in zip(chains, ps, dsbs, qts, dots):
                dq = jnp.dot(dsb, k_ref[ks:ke, hs], preferred_element_type=F32) * scale
                dqkv_ref[0, rows, hs] = dq.astype(BF16)
                dkt_acc[hs, ks:ke] += jnp.dot(qt, dsb, preferred_element_type=F32)
                dvt_acc[hs, ks:ke] += jnp.dot(dot_, p, preferred_element_type=F32)
        for h in range(2):
            dvr_ref[h:h + 1, :] = _a_diag_sums(db_acc, h)
        dqkv_ref[1] = dkt_acc[...].T.astype(BF16)
        dqkv_ref[2] = dvt_acc[...].T.astype(BF16)

    qs, ks_, vs = _head_specs(S, HP)
    hd = pl.BlockSpec((S, 2 * HEAD_DIM), lambda b, hp: (b, hp))
    return pl.pallas_call(
        body, name="attn_a_bwd", grid=(B, HP),
        in_specs=[qs, ks_, vs, pl.BlockSpec((None, 2, VR_W), lambda b, hp: (hp, 0, 0)), hd],
        out_specs=[pl.BlockSpec((3, S, 2 * HEAD_DIM), lambda b, hp: (0, b, hp)),
                   pl.BlockSpec((None, None, 2, VR_W), lambda b, hp: (b, hp, 0, 0))],
        out_shape=[jax.ShapeDtypeStruct((3, B * S, D), BF16), jax.ShapeDtypeStruct((B, HP, 2, VR_W), F32)],
        scratch_shapes=[pltpu.VMEM((2 * HEAD_DIM, S), F32), pltpu.VMEM((2 * HEAD_DIM, S), F32),
                        pltpu.VMEM((2, QB_A, VR_W), F32)],
        compiler_params=_params(("parallel", "parallel")),
    )(qkv, qkv, qkv, vr, do)


def _tri(cmp):
    j = lax.broadcasted_iota(jnp.int32, (SB_TILE, SB_TILE), 0)
    s = lax.broadcasted_iota(jnp.int32, (SB_TILE, SB_TILE), 1)
    return cmp(j, s).astype(BF16)


def _cumsum_mm(x, tri):
    hi = x.astype(BF16)
    mid = (x - hi.astype(F32)).astype(BF16)
    return jnp.dot(hi, tri, preferred_element_type=F32) + jnp.dot(mid, tri, preferred_element_type=F32)


def _sb_logs(q, k, diagonal):
    z = lax.dot_general(q, k, NT_DIMS, preferred_element_type=F32)
    log_b = jnp.minimum(z, 0.0) - jnp.log(1.0 + jnp.exp(-jnp.abs(z)))
    log_1mb = log_b - z
    if not diagonal:
        return log_b, log_1mb, None
    row = lax.broadcasted_iota(jnp.int32, (SB_TILE, SB_TILE), 0)
    col = lax.broadcasted_iota(jnp.int32, (SB_TILE, SB_TILE), 1)
    causal = col < row
    return log_b, jnp.where(causal, log_1mb, 0.0), causal


def attn_b_fwd(qkv, B, S):
    D = qkv.shape[1] // 3
    HP = D // (2 * HEAD_DIM)
    scale = HEAD_DIM ** -0.5
    nb = S // SB_TILE

    def body(q_ref, k_ref, v_ref, o_ref, nt_ref):
        tri = _tri(lambda j, s: j > s)
        heads = [slice(h * HEAD_DIM, (h + 1) * HEAD_DIM) for h in range(2)]

        def q_block(qb, n_pairs, parity):
            q0 = pl.multiple_of(qb * SB_TILE, SB_TILE)
            rows = pl.ds(q0, SB_TILE)
            qs = [q_ref[rows, hs] * scale for hs in heads]

            def step(blocks, state):
                chains = [(h, kb, diagonal, pl.ds(pl.multiple_of(kb * SB_TILE, SB_TILE), SB_TILE))
                          for h in range(2) for kb, diagonal in blocks]
                logs = [_sb_logs(qs[h], k_ref[keys, heads[h]], diagonal)
                        for h, _, diagonal, keys in chains]
                sums = [_cumsum_mm(log_1mb, tri) for _, log_1mb, _ in logs]
                rights = [state[0][0], state[1][0]]
                accs = [state[0][1], state[1][1]]
                weights_ = []
                for (h, _, diagonal, _), (log_b, log_1mb, causal), csum in zip(chains, logs, sums):
                    a = jnp.exp(log_b + csum + rights[h])
                    if diagonal:
                        a = jnp.where(causal, a, 0.0)
                    weights_.append(a.astype(BF16))
                    rights[h] = rights[h] + jnp.sum(log_1mb, axis=-1, keepdims=True)
                for (h, _, _, keys), a in zip(chains, weights_):
                    accs[h] = accs[h] + jnp.dot(a, v_ref[keys, heads[h]], preferred_element_type=F32)
                return ((rights[0], accs[0]), (rights[1], accs[1]))

            zero = (jnp.zeros((SB_TILE, 1), F32), jnp.zeros((SB_TILE, HEAD_DIM), F32))
            first = [(qb, True)] + ([(qb - 1, False)] if parity else [])
            top = qb - len(first)
            state = lax.fori_loop(
                0, n_pairs, lambda t, st: step([(top - 2 * t, False), (top - 2 * t - 1, False)], st),
                step(first, (zero, zero)))
            for hs, (right, acc) in zip(heads, state):
                o_ref[rows, hs] = acc.astype(BF16)
                nt_ref[rows, hs] = jnp.broadcast_to(right, (SB_TILE, HEAD_DIM))

        def q_pair_loop(j, carry):
            q_block(2 * j, j, 0)
            q_block(2 * j + 1, j, 1)
            return carry

        lax.fori_loop(0, nb // 2, q_pair_loop, 0)

    qs, ks_, vs = _head_specs(S, HP)
    hd = pl.BlockSpec((S, 2 * HEAD_DIM), lambda b, hp: (b, hp))
    return pl.pallas_call(
        body, name="attn_b_fwd", grid=(B, HP),
        in_specs=[qs, ks_, vs], out_specs=[hd, hd],
        out_shape=[jax.ShapeDtypeStruct((B * S, D), BF16), jax.ShapeDtypeStruct((B * S, D), F32)],
        compiler_params=_params(("parallel", "parallel")),
    )(qkv, qkv, qkv)


def attn_b_bwd(qkv, do, ntot, B, S):
    D = qkv.shape[1] // 3
    HP = D // (2 * HEAD_DIM)
    scale = HEAD_DIM ** -0.5
    nb = S // SB_TILE

    def body(q_ref, k_ref, v_ref, do_ref, nt_ref, dqkv_ref, dkt_acc, dvt_acc):
        tri_incl = _tri(lambda j, s: j <= s)
        tri_excl = _tri(lambda j, s: j < s)
        heads = [slice(h * HEAD_DIM, (h + 1) * HEAD_DIM) for h in range(2)]
        dkt_acc[...] = jnp.zeros_like(dkt_acc)
        dvt_acc[...] = jnp.zeros_like(dvt_acc)

        def q_block(qb, n_pairs, parity):
            q0 = pl.multiple_of(qb * SB_TILE, SB_TILE)
            rows = pl.ds(q0, SB_TILE)
            qt_pair = (q_ref[rows, :] * scale).astype(F32).T.astype(BF16)
            dot_pair = do_ref[rows, :].astype(F32).T.astype(BF16)
            per_head = [(hs, q_ref[rows, hs] * scale, do_ref[rows, hs], qt_pair[hs, :], dot_pair[hs, :],
                         nt_ref[rows, hs.start:hs.start + 1]) for hs in heads]

            def step(blocks, state):
                chains = [(h, kb, diagonal, pl.ds(pl.multiple_of(kb * SB_TILE, SB_TILE), SB_TILE))
                          for h in range(2) for kb, diagonal in blocks]
                ks = [k_ref[keys, per_head[h][0]] for h, _, _, keys in chains]
                logs = [_sb_logs(per_head[h][1], k, diagonal)
                        for (h, _, diagonal, _), k in zip(chains, ks)]
                das = [lax.dot_general(per_head[h][2], v_ref[keys, per_head[h][0]], NT_DIMS,
                                       preferred_element_type=F32) for h, _, _, keys in chains]
                sums = [_cumsum_mm(log_1mb, tri_incl) for _, log_1mb, _ in logs]
                left_n = [state[0][0], state[1][0]]
                left_d = [state[0][1], state[1][1]]
                dq_acc = [state[0][2], state[1][2]]
                a_s, dls = [], []
                for (h, _, diagonal, _), (log_b, log_1mb, causal), csum, da in zip(chains, logs, sums, das):
                    a = jnp.exp(log_b + (per_head[h][5] - left_n[h]) - csum)
                    if diagonal:
                        a = jnp.where(causal, a, 0.0)
                    a_s.append(a)
                    dls.append(a * da)
                    left_n[h] = left_n[h] + jnp.sum(log_1mb, axis=-1, keepdims=True)
                dsums = [_cumsum_mm(dl, tri_excl) for dl in dls]
                dzbs = []
                for (h, _, diagonal, _), (log_b, log_1mb, causal), dl, dsum in zip(chains, logs, dls, dsums):
                    dz = dl * jnp.exp(log_1mb) - (left_d[h] + dsum) * jnp.exp(log_b)
                    if diagonal:
                        dz = jnp.where(causal, dz, 0.0)
                    dzbs.append(dz.astype(BF16))
                    left_d[h] = left_d[h] + jnp.sum(dl, axis=-1, keepdims=True)
                for (h, kb, _, _), k, a, dzb in zip(chains, ks, a_s, dzbs):
                    hs, _, _, qt, dot_, _ = per_head[h]
                    dq_acc[h] = dq_acc[h] + jnp.dot(dzb, k, preferred_element_type=F32)
                    dkt_acc[kb, hs, :] += jnp.dot(qt, dzb, preferred_element_type=F32)
                    dvt_acc[kb, hs, :] += jnp.dot(dot_, a.astype(BF16), preferred_element_type=F32)
                return ((left_n[0], left_d[0], dq_acc[0]), (left_n[1], left_d[1], dq_acc[1]))

            zero1 = jnp.zeros((SB_TILE, 1), F32)
            zero = (zero1, zero1, jnp.zeros((SB_TILE, HEAD_DIM), F32))
            state = lax.fori_loop(
                0, n_pairs, lambda t, st: step([(2 * t, False), (2 * t + 1, False)], st), (zero, zero))
            last = ([(qb - 1, False)] if parity else []) + [(qb, True)]
            state = step(last, state)
            for hs, (_, _, dq_acc) in zip(heads, state):
                dqkv_ref[0, rows, hs] = (dq_acc * scale).astype(BF16)

        def q_pair_loop(j, carry):
            q_block(2 * j, j, 0)
            q_block(2 * j + 1, j, 1)
            return carry

        lax.fori_loop(0, nb // 2, q_pair_loop, 0)
        for kb in range(nb):
            dqkv_ref[1, kb * SB_TILE:(kb + 1) * SB_TILE, :] = dkt_acc[kb].T.astype(BF16)
            dqkv_ref[2, kb * SB_TILE:(kb + 1) * SB_TILE, :] = dvt_acc[kb].T.astype(BF16)

    qs, ks_, vs = _head_specs(S, HP)
    hd = pl.BlockSpec((S, 2 * HEAD_DIM), lambda b, hp: (b, hp))
    acc = pltpu.VMEM((nb, 2 * HEAD_DIM, SB_TILE), F32)
    return pl.pallas_call(
        body, name="attn_b_bwd", grid=(B, HP),
        in_specs=[qs, ks_, vs, hd, hd],
        out_specs=pl.BlockSpec((3, S, 2 * HEAD_DIM), lambda b, hp: (0, b, hp)),
        out_shape=jax.ShapeDtypeStruct((3, B * S, D), BF16),
        scratch_shapes=[acc, acc],
        compiler_params=_params(("parallel", "parallel")),
    )(qkv, qkv, qkv, do, ntot)


def _place():
    x, y, c = lax.axis_index("x"), lax.axis_index("y"), lax.axis_index("c")
    chips = [(1 - x, y), (x, 1 - y), (1 - x, 1 - y)]
    return x, y, c, 2 * x + y, chips


def _remote(src, dst, send_sem, recv_sem, dev):
    return pltpu.make_async_remote_copy(src_ref=src, dst_ref=dst, send_sem=send_sem, recv_sem=recv_sem,
                                        device_id=dev, device_id_type=MESH)


def gather_weights(shards):
    n = len(shards)

    def body(*refs):
        ins, outs = refs[:n], refs[n:2 * n]
        send1, recv1, send2, recv2, lsem = refs[2 * n:]
        x, y, c, me, chips = _place()
        local, first = [], []
        for f in range(n):
            hl = shards[f].shape[0] // 2
            cp = pltpu.make_async_copy(ins[f], outs[f].at[me], lsem.at[f])
            cp.start()
            local.append(cp)
            for j, (qx, qy) in enumerate(chips):
                half = pl.ds(c * hl, hl)
                cp = _remote(ins[f].at[half], outs[f].at[me, half],
                             send1.at[3 * f + j], recv1.at[3 * f + j], (qx, qy, c))
                cp.start()
                first.append(cp)
        passed = []
        for f in range(n):
            hl = shards[f].shape[0] // 2
            for j, (qx, qy) in enumerate(chips):
                slab = outs[f].at[2 * qx + qy, pl.ds(c * hl, hl)]
                _remote(slab, slab, send1.at[3 * f + j], recv1.at[3 * f + j], (x, y, c)).wait_recv()
                cp = _remote(slab, slab, send2.at[3 * f + j], recv2.at[3 * f + j], (x, y, 1 - c))
                cp.start()
                passed.append(cp)
        for f in range(n):
            hl = shards[f].shape[0] // 2
            for j, (qx, qy) in enumerate(chips):
                slab = outs[f].at[2 * qx + qy, pl.ds((1 - c) * hl, hl)]
                _remote(slab, slab, send2.at[3 * f + j], recv2.at[3 * f + j], (x, y, c)).wait_recv()
        for cp in first + passed:
            cp.wait_send()
        for cp in local:
            cp.wait()

    sems = pltpu.SemaphoreType.DMA((3 * n,))
    return pl.pallas_call(
        body, name="gather_weights",
        in_specs=[ANY] * n, out_specs=[ANY] * n,
        out_shape=[jax.ShapeDtypeStruct((N_CHIPS,) + s.shape, s.dtype) for s in shards],
        scratch_shapes=[sems, sems, sems, sems, pltpu.SemaphoreType.DMA((n,))],
        compiler_params=pltpu.CompilerParams(has_side_effects=True),
    )(*shards)


HBM = pl.BlockSpec(memory_space=pltpu.HBM)
SEM = pl.BlockSpec(memory_space=pltpu.SEMAPHORE)
EFFECT = pltpu.SideEffectType.DATAFLOW_SIDE_EFFECTING


def _in_hbm(a):
    return pltpu.with_memory_space_constraint(a, pltpu.HBM)


def _row_half(ref, which):
    hr = ref.shape[-2] // 2
    idx = [pl.ds(0, d) for d in ref.shape[:-2]] + [pl.ds(which * hr, hr), pl.ds(0, ref.shape[-1])]
    return ref.at[tuple(idx)]


def cast_place(ids, w, lead, L):
    R, C = w.shape[-2:]

    def body(ids_ref, w_ref, s_ref, land_ref):
        v = w_ref[...].astype(BF16)
        s_ref[...] = v
        land_ref[...] = v

    return pl.pallas_call(
        body, name="cast_place",
        grid_spec=pltpu.PrefetchScalarGridSpec(
            num_scalar_prefetch=1, grid=(L,),
            in_specs=[pl.BlockSpec((None,) * (w.ndim - 2) + (R, C), lambda l, ids: (*lead(l), 0, 0))],
            out_specs=[pl.BlockSpec((None, R, C), lambda l, ids: (l, 0, 0)),
                       pl.BlockSpec((None, None, R, C), lambda l, ids: (ids[1], l, 0, 0))]),
        out_shape=[jax.ShapeDtypeStruct((L, R, C), BF16), jax.ShapeDtypeStruct((N_CHIPS, L, R, C), BF16)],
        compiler_params=_params(("parallel",)),
    )(ids, w)


def gather_start(layers, zones, after):
    flat = [s for lay in layers for s in lay]
    flat_zones = [z for lay in zones for z in lay]
    counts = [len(lay) for lay in layers]
    n, nl = len(flat), len(layers)

    def body(*refs):
        ins, lands = refs[:n], refs[n:2 * n]
        send, recv = refs[2 * n + 1:2 * n + 1 + nl], refs[2 * n + 1 + nl:2 * n + 1 + 2 * nl]
        token = refs[-1]
        x, y, c, me, chips = _place()
        f = 0
        for li, cnt in enumerate(counts):
            for k in range(cnt):
                for j, (qx, qy) in enumerate(chips):
                    _remote(_row_half(ins[f], c), _row_half(lands[f].at[me], c),
                            send[li].at[3 * k + j], recv[li].at[3 * k + j], (qx, qy, c)).start()
                f += 1
        token[...] = jnp.zeros_like(token)

    sem_shapes = [pltpu.SemaphoreType.DMA((3 * cnt,)) for cnt in counts]
    land_shapes = [(N_CHIPS,) + s.shape for s in flat]
    res = pl.pallas_call(
        body, name="gather_start",
        out_shape=(*sem_shapes, *sem_shapes,
                   *[pltpu.HBM(s.shape, s.dtype) for s in flat],
                   *[pltpu.HBM(shp, s.dtype) for shp, s in zip(land_shapes, flat)],
                   jax.ShapeDtypeStruct((8, 128), F32)),
        in_specs=[HBM] * (2 * n) + [ANY],
        out_specs=(*[SEM] * (2 * nl), *[HBM] * (2 * n), pl.BlockSpec(memory_space=pltpu.VMEM)),
        input_output_aliases={k: 2 * nl + k for k in range(2 * n)},
        compiler_params=pltpu.CompilerParams(has_side_effects=EFFECT),
    )(*[_in_hbm(s) for s in flat], *[_in_hbm(z) for z in flat_zones], after)
    send, recv = res[:nl], res[nl:2 * nl]
    thru, lands, token = res[2 * nl:2 * nl + n], res[2 * nl + n:2 * nl + 2 * n], res[-1]
    out, f = [], 0
    for li, cnt in enumerate(counts):
        out.append((send[li], recv[li], list(thru[f:f + cnt]), list(lands[f:f + cnt])))
        f += cnt
    return out, token


def gather_wait(li, send, recv, shards, lands, after):
    m = len(shards)

    def body(*refs):
        ins, lnd = refs[:m], refs[m:2 * m]
        snd, rcv = refs[2 * m], refs[2 * m + 1]
        x, y, c, me, chips = _place()
        for k in range(m):
            for j, (qx, qy) in enumerate(chips):
                cp = _remote(_row_half(ins[k], c), _row_half(lnd[k].at[2 * qx + qy], c),
                             snd.at[3 * k + j], rcv.at[3 * k + j], (qx, qy, c))
                cp.wait_send()
                cp.wait_recv()

    res = pl.pallas_call(
        body, name=f"gather_wait_{li}",
        out_shape=(*[pltpu.HBM(s.shape, s.dtype) for s in shards],
                   *[pltpu.HBM(s.shape, s.dtype) for s in lands]),
        in_specs=[HBM] * (2 * m) + [SEM, SEM, ANY], out_specs=[HBM] * (2 * m),
        input_output_aliases={k: k for k in range(2 * m)},
        compiler_params=pltpu.CompilerParams(has_side_effects=EFFECT),
    )(*shards, *lands, send, recv, after)
    return list(res[:m]), list(res[m:])


def gather_forward(li, lands):
    m = len(lands)

    def body(*refs):
        outs = refs[m:2 * m]
        send, recv = refs[2 * m:]
        x, y, c, me, chips = _place()
        passed = []
        for k in range(m):
            for j, (qx, qy) in enumerate(chips):
                slab = _row_half(outs[k].at[2 * qx + qy], c)
                cp = _remote(slab, slab, send.at[3 * k + j], recv.at[3 * k + j], (x, y, 1 - c))
                cp.start()
                passed.append(cp)
        for k in range(m):
            for j, (qx, qy) in enumerate(chips):
                slab = _row_half(outs[k].at[2 * qx + qy], 1 - c)
                _remote(slab, slab, send.at[3 * k + j], recv.at[3 * k + j], (x, y, c)).wait_recv()
        for cp in passed:
            cp.wait_send()

    sems = pltpu.SemaphoreType.DMA((3 * m,))
    return pl.pallas_call(
        body, name=f"gather_forward_{li}",
        in_specs=[ANY] * m, out_specs=[ANY] * m,
        out_shape=[jax.ShapeDtypeStruct(s.shape, s.dtype) for s in lands],
        input_output_aliases={k: k for k in range(m)},
        scratch_shapes=[sems, sems],
        compiler_params=pltpu.CompilerParams(has_side_effects=True),
    )(*lands)


def pair_exchange_rows(li, grads):
    n = len(grads)

    def body(*refs):
        ins, outs = refs[:n], refs[n:2 * n]
        send, recv = refs[2 * n:]
        x, y, c, _, _ = _place()
        cps = []
        for k in range(n):
            cp = _remote(_row_half(ins[k], 1 - c), outs[k], send.at[k], recv.at[k], (x, y, 1 - c))
            cp.start()
            cps.append(cp)
        for cp in cps:
            cp.wait()

    sems = pltpu.SemaphoreType.DMA((n,))
    return pl.pallas_call(
        body, name=f"pair_exchange_{li}",
        in_specs=[ANY] * n, out_specs=[ANY] * n,
        out_shape=[jax.ShapeDtypeStruct(g.shape[:-2] + (g.shape[-2] // 2, g.shape[-1]), g.dtype)
                   for g in grads],
        scratch_shapes=[sems, sems],
        compiler_params=pltpu.CompilerParams(has_side_effects=True),
    )(*grads)


def pair_add_rows(ids, grad, recv):
    L, P, hr, C = recv.shape

    def body(ids_ref, a_ref, b_ref, o_ref):
        o_ref[...] = (a_ref[...].astype(F32) + b_ref[...].astype(F32)).astype(o_ref.dtype)

    blk = (None, P, hr, C)
    return pl.pallas_call(
        body, name="pair_add",
        grid_spec=pltpu.PrefetchScalarGridSpec(
            num_scalar_prefetch=1, grid=(L,),
            in_specs=[pl.BlockSpec(blk, lambda l, ids: (l, 0, ids[0], 0)),
                      pl.BlockSpec(blk, lambda l, ids: (l, 0, 0, 0))],
            out_specs=pl.BlockSpec(blk, lambda l, ids: (l, 0, 0, 0))),
        out_shape=jax.ShapeDtypeStruct(recv.shape, recv.dtype),
        compiler_params=_params(("parallel",)),
    )(ids, grad, recv)


def reduce_start(li, parts):
    m = len(parts)

    def body(*refs):
        ins, lands = refs[:m], refs[m:2 * m]
        send, recv = refs[2 * m], refs[2 * m + 1]
        token = refs[-1]
        x, y, c, me, chips = _place()
        for k in range(m):
            rows = pl.ds(0, parts[k].shape[0])
            for j, (qx, qy) in enumerate(chips):
                _remote(ins[k].at[rows, 2 * qx + qy], lands[k].at[rows, me],
                        send.at[3 * k + j], recv.at[3 * k + j], (qx, qy, c)).start()
        token[...] = jnp.zeros_like(token)

    sems = pltpu.SemaphoreType.DMA((3 * m,))
    res = pl.pallas_call(
        body, name=f"reduce_start_{li}",
        out_shape=(sems, sems, *[pltpu.HBM(s.shape, s.dtype) for s in parts],
                   *[pltpu.HBM(s.shape, s.dtype) for s in parts], jax.ShapeDtypeStruct((8, 128), F32)),
        in_specs=[HBM] * (2 * m),
        out_specs=(SEM, SEM, *[HBM] * (2 * m), pl.BlockSpec(memory_space=pltpu.VMEM)),
        input_output_aliases={k: 2 + k for k in range(2 * m)},
        compiler_params=pltpu.CompilerParams(has_side_effects=EFFECT),
    )(*[_in_hbm(s) for s in parts], *[_in_hbm(lax.empty(s.shape, s.dtype)) for s in parts])
    return res[0], res[1], list(res[2:2 + m]), list(res[2 + m:2 + 2 * m]), res[-1]


def reduce_wait(li, send, recv, parts, lands, after):
    m = len(parts)

    def body(*refs):
        ins, lnd = refs[:m], refs[m:2 * m]
        snd, rcv = refs[2 * m], refs[2 * m + 1]
        x, y, c, me, chips = _place()
        for k in range(m):
            rows = pl.ds(0, parts[k].shape[0])
            for j, (qx, qy) in enumerate(chips):
                cp = _remote(ins[k].at[rows, 2 * qx + qy], lnd[k].at[rows, 2 * qx + qy],
                             snd.at[3 * k + j], rcv.at[3 * k + j], (qx, qy, c))
                cp.wait_send()
                cp.wait_recv()

    res = pl.pallas_call(
        body, name=f"reduce_wait_{li}",
        out_shape=(*[pltpu.HBM(s.shape, s.dtype) for s in parts],
                   *[pltpu.HBM(s.shape, s.dtype) for s in lands]),
        in_specs=[HBM] * (2 * m) + [SEM, SEM, ANY], out_specs=[HBM] * (2 * m),
        input_output_aliases={k: k for k in range(2 * m)},
        compiler_params=pltpu.CompilerParams(has_side_effects=EFFECT),
    )(*parts, *lands, send, recv, after)
    return list(res[:m]), list(res[m:])


def chip_sum_rows(ids, land, part, gfull, l0):
    L, P, hr, C = land.shape

    def body(ids_ref, land_ref, part_ref, _g, o_ref):
        tot = None
        for q in range(P):
            term = jnp.where(ids_ref[1] == q, part_ref[...], land_ref[q]).astype(F32)
            tot = term if tot is None else tot + term
        o_ref[...] = tot

    return pl.pallas_call(
        body, name="chip_sum",
        grid_spec=pltpu.PrefetchScalarGridSpec(
            num_scalar_prefetch=1, grid=(L,),
            in_specs=[pl.BlockSpec((None, P, hr, C), lambda l, ids: (l, 0, 0, 0)),
                      pl.BlockSpec((None, None, hr, C), lambda l, ids: (l, ids[1], 0, 0)), ANY],
            out_specs=pl.BlockSpec((None, hr, C), lambda l, ids: (l0 + l, ids[0], 0))),
        out_shape=jax.ShapeDtypeStruct(gfull.shape, gfull.dtype),
        input_output_aliases={3: 0},
        compiler_params=_params(("arbitrary",)),
    )(ids, land, part, gfull)


def half_swap_rows(grads):
    n = len(grads)

    def body(*refs):
        outs = refs[n:2 * n]
        send, recv = refs[2 * n:]
        x, y, c, _, _ = _place()
        cps = []
        for k in range(n):
            mine = _row_half(outs[k], c)
            cp = _remote(mine, mine, send.at[k], recv.at[k], (x, y, 1 - c))
            cp.start()
            cps.append(cp)
        for k in range(n):
            theirs = _row_half(outs[k], 1 - c)
            _remote(theirs, theirs, send.at[k], recv.at[k], (x, y, c)).wait_recv()
        for cp in cps:
            cp.wait_send()

    sems = pltpu.SemaphoreType.DMA((n,))
    return pl.pallas_call(
        body, name="half_swap",
        in_specs=[ANY] * n, out_specs=[ANY] * n,
        out_shape=[jax.ShapeDtypeStruct(g.shape, g.dtype) for g in grads],
        input_output_aliases={k: k for k in range(n)},
        scratch_shapes=[sems, sems],
        compiler_params=pltpu.CompilerParams(has_side_effects=True),
    )(*grads)


def all_sum_small(v):
    R = v.shape[0]

    def body(v_ref, o_ref, land, send, recv):
        x, y, c, _, _ = _place()
        me = 4 * x + 2 * y + c
        land[me] = v_ref[...]
        peers = [(px, py, pc) for px in range(2) for py in range(2) for pc in range(2)]
        cps = []
        for k in range(1, 8):
            dev = (x ^ (k >> 2), y ^ ((k >> 1) & 1), c ^ (k & 1))
            cp = _remote(v_ref, land.at[me], send.at[k - 1], recv.at[k - 1], dev)
            cp.start()
            cps.append(cp)
        for k in range(1, 8):
            src = 4 * (x ^ (k >> 2)) + 2 * (y ^ ((k >> 1) & 1)) + (c ^ (k & 1))
            _remote(v_ref, land.at[src], send.at[k - 1], recv.at[k - 1], (x, y, c)).wait_recv()
        for cp in cps:
            cp.wait_send()
        tot = land[0]
        for d in range(1, len(peers)):
            tot = tot + land[d]
        o_ref[...] = tot

    sems = pltpu.SemaphoreType.DMA((7,))
    vm = pl.BlockSpec(memory_space=pltpu.VMEM)
    return pl.pallas_call(
        body, name="all_sum_small", in_specs=[vm], out_specs=vm,
        out_shape=jax.ShapeDtypeStruct(v.shape, F32),
        scratch_shapes=[pltpu.VMEM((8, R, 128), F32), sems, sems],
        compiler_params=pltpu.CompilerParams(has_side_effects=True),
    )(v)


def _row_tile(R):
    for t in range(min(R, 512) // 8 * 8, 7, -8):
        if R % t == 0:
            return t
    return R


def adamw(w, g, m, v):
    lead, (R, C) = w.shape[:-2], w.shape[-2:]
    tr = _row_tile(R)
    c1 = 1.0 / (1.0 - ADAM_B1 ** ADAM_STEP)
    c2 = 1.0 / (1.0 - ADAM_B2 ** ADAM_STEP)

    def body(w_ref, g_ref, m_ref, v_ref, go_ref, d_ref, nm_ref, nv_ref):
        gv = g_ref[...]
        go_ref[...] = gv
        nm = ADAM_B1 * m_ref[...] + (1.0 - ADAM_B1) * gv
        nv = ADAM_B2 * v_ref[...] + (1.0 - ADAM_B2) * (gv * gv)
        nm_ref[...] = nm
        nv_ref[...] = nv
        d_ref[...] = -ADAM_LR * ((nm * c1) / (jnp.sqrt(nv * c2) + ADAM_EPS) + ADAM_WD * w_ref[...])

    def flat(idx):
        l = 0
        for i, n in zip(idx, lead):
            l = l * n + i
        return l

    blk = pl.BlockSpec((None,) * len(lead) + (tr, C), lambda *ix: (*ix, 0))
    gblk = pl.BlockSpec((None, tr, C), lambda *ix: (flat(ix[:-1]), ix[-1], 0))
    osh = jax.ShapeDtypeStruct(w.shape, F32)
    return pl.pallas_call(
        body, name="adamw", grid=(*lead, R // tr),
        in_specs=[blk, gblk, blk, blk], out_specs=[blk, blk, blk, blk], out_shape=[osh, osh, osh, osh],
        compiler_params=_params(("parallel",) * (len(lead) + 1)),
    )(w, g, m, v)


def kernel(x, w_qkv_a, w_o_a, rel_bias, w_qkv_b, w_o_b, ffn_w_gate, ffn_w_up, ffn_w_down, ln_g, ln_b, loss_target, m_w_qkv_a, m_w_o_a, m_rel_bias, m_w_qkv_b, m_w_o_b, m_ffn_w_gate, m_ffn_w_up, m_ffn_w_down, m_ln_g, m_ln_b, v_w_qkv_a, v_w_o_a, v_rel_bias, v_w_qkv_b, v_w_o_b, v_ffn_w_gate, v_ffn_w_up, v_ffn_w_down, v_ln_g, v_ln_b):
    B, S, D = x.shape
    M = B * S
    depth = ffn_w_gate.shape[0]
    n_ffn = 2 * depth
    H = D // HEAD_DIM
    HP = H // 2
    Fs = ffn_w_gate.shape[-1]
    alpha = (2.0 * depth) ** 0.25
    assert S % QB_A == 0 and S % SB_TILE == 0 and rel_bias.shape == (N_REL, H)

    me = 2 * lax.axis_index("x") + lax.axis_index("y")
    ids = jnp.stack([lax.axis_index("c"), me]).astype(jnp.int32)

    def tr(a):
        return jnp.swapaxes(a, -1, -2)

    gate_t, up_t = tr(ffn_w_gate), tr(ffn_w_up)

    def mixer_slots(i):
        wq, wo, mix = (w_qkv_a, w_o_a, "a") if i % 2 == 0 else (w_qkv_b, w_o_b, "b")
        return [(w, (lambda l, i=i: (i // 2,)), 1, (role + mix, i // 2), [((i, 1), role, 0)])
                for role, w in (("q", wq), ("o", wo))]

    def ffn_slots(i, subs):
        first = 0 if subs[0] == 0 else 1
        return [(w, (lambda l, i=i, first=first: (i, first + l)), len(subs), (role, 2 * i + first),
                 [((i, j), role, k) for k, j in enumerate(subs)])
                for role, w in (("gate", gate_t), ("up", up_t), ("down", ffn_w_down))]

    groups = [ffn_slots(0, [0]), mixer_slots(0) + ffn_slots(0, [2])]
    groups += [mixer_slots(i) + ffn_slots(i, [0, 2]) for i in range(1, depth)]

    def group_of(i, j):
        return i + 1 if i else (0 if j == 0 else 1)

    lng_p, lnb_p = gather_weights([ln_g, ln_b])
    lng = jnp.moveaxis(lng_p, 0, 2).reshape(depth, 3, 1, D)
    lnb = jnp.moveaxis(lnb_p, 0, 2).reshape(depth, 3, 1, D)
    placed = [[cast_place(ids, src, lead, L) for src, lead, L, _, _ in grp] for grp in groups]
    in_flight, _ = gather_start([[s for s, _ in grp] for grp in placed],
                                [[z for _, z in grp] for grp in placed], lng_p)

    tab_t = jnp.pad(rel_bias.T, ((0, 0), (0, REL_PAD - N_REL)))
    vr = bias_vec(tab_t).reshape(HP, 2, VR_W)

    xf = x.reshape(M, D)
    xb = xf.astype(BF16)
    saved, weights, landed = [], {}, set()
    for i in range(depth):
        for j in range(3):
            gi = group_of(i, j)
            if gi not in landed:
                landed.add(gi)
                send, recv, thru, lands = in_flight[gi]
                _, lands = gather_wait(gi, send, recv, thru, lands, xf)
                for slot, arr in zip(groups[gi], gather_forward(gi, lands)):
                    for unit, role, l in slot[4]:
                        weights[unit, role] = (arr, l)
            gam, bet = lng[i, j], lnb[i, j]
            if j != 1:
                (wg, l), (wu, _), (wd, _) = (weights[(i, j), r] for r in ("gate", "up", "down"))
                g, u, h = ffn_up(xb, wg, wu, l)
                xo, xob, xhat, rstd = mm_ln(h, wd, l, xf, gam, bet, alpha, 0.5, True)
                saved.append(("ffn", i, l, xb, g, u, h, xhat, rstd, gam))
            else:
                wq, wo = weights[(i, 1), "q"][0], weights[(i, 1), "o"][0]
                qkv = qkv_proj(xb, wq, 0)
                if i % 2 == 0:
                    o, ntot = attn_a_fwd(qkv, vr, B, S), None
                else:
                    o, ntot = attn_b_fwd(qkv, B, S)
                xo, xob, xhat, rstd = mm_ln(o, wo, 0, xf, gam, bet, alpha, 1.0, False)
                saved.append(("a" if i % 2 == 0 else "b", i, 0, xb, qkv, o, ntot, xhat, rstd, gam))
            xf, xb = xo, xob

    dy, loss_part = loss_head(xf, loss_target.reshape(M, D))
    loss = lax.psum(loss_part[0, 0], ("x", "y", "c"))

    Cq, Ro = w_qkv_a.shape[-1], w_o_a.shape[1]
    dgam = lax.empty((3 * depth, 1, D), F32)
    dbet = lax.empty((3 * depth, 1, D), F32)
    dvrs, reducing = [], {}
    gbufs = [[lax.empty((L, N_CHIPS) + src.shape[-2:], BF16) for src, _, L, _, _ in grp] for grp in groups]
    grad_of = {(unit, role): (gi, si, l) for gi, grp in enumerate(groups) for si, slot in enumerate(grp)
               for unit, role, l in slot[4]}

    def ln_of(sub):
        rec = saved[sub]
        return rec[7], rec[8], rec[9], (0.5 if rec[0] == "ffn" else 1.0)

    no_token = jnp.zeros((8, 128), F32)
    token = no_token
    dpre, dyb, dgam, dbet = ln_bwd(dy, *ln_of(3 * depth - 1), dgam, dbet, 3 * depth - 1)
    for sub in reversed(range(3 * depth)):
        i, j = divmod(sub, 3)
        kind, _, l, xb_in, t1, t2, t3, _, _, _ = saved[sub]
        ln = (*ln_of(sub - 1), dgam, dbet, sub - 1) if sub else None
        if kind == "ffn":
            g, u, h = t1, t2, t3
            (gi, sg, _), (_, su, _), (_, sd, _) = (grad_of[(i, j), r] for r in ("gate", "up", "down"))
            wg, wu, wd = (weights[(i, j), r][0] for r in ("gate", "up", "down"))
            dg, du, gbufs[gi][sd], gbufs[gi][sg], gbufs[gi][su] = ffn_bwd(
                gbufs[gi][sd], gbufs[gi][sg], gbufs[gi][su], l, xb_in, dyb, wd, g, u, h, token)
            out = ffn_dx(dg, du, wg, wu, l, dpre, alpha, ln)
        else:
            qkv, o = t1, t2
            (gi, sq, _), (_, so, _) = grad_of[(i, j), "q"], grad_of[(i, j), "o"]
            wq, wo = weights[(i, j), "q"][0], weights[(i, j), "o"][0]
            do = o_proj_bwd(dyb, wo, l, token)
            gbufs[gi][so] = dw_o(gbufs[gi][so], l, o, dyb)
            if kind == "a":
                dqkv, dvr = attn_a_bwd(qkv, vr, do, B, S)
                dvrs.append(dvr.reshape(B, H, VR_W))
            else:
                dqkv = attn_b_bwd(qkv, do, t3, B, S)
            gbufs[gi][sq] = dw_qkv(gbufs[gi][sq], l, xb_in, dqkv)
            out = qkv_proj_bwd(dqkv, wq, l, dpre, alpha, ln)
        if ln is None:
            dy = out
        else:
            dpre, dyb, dgam, dbet = out
        token = no_token
        gi = group_of(i, j)
        if sub == 0 or group_of(*divmod(sub - 1, 3)) != gi:
            from_sib = pair_exchange_rows(gi, gbufs[gi])
            parts = [pair_add_rows(ids, g_, r_) for g_, r_ in zip(gbufs[gi], from_sib)]
            reducing[gi] = reduce_start(gi, parts)
            token = reducing[gi][4]
    grad_x = dy.reshape(B, S, D)

    la, lb = w_qkv_a.shape[0], w_qkv_b.shape[0]
    full = {"qa": lax.empty((la, D, Cq), F32), "oa": lax.empty((la, Ro, D), F32),
            "qb": lax.empty((lb, D, Cq), F32), "ob": lax.empty((lb, Ro, D), F32),
            "gate": lax.empty((n_ffn, Fs, D), F32), "up": lax.empty((n_ffn, Fs, D), F32),
            "down": lax.empty((n_ffn, Fs, D), F32)}
    for gi in reversed(range(len(groups))):
        send, recv, parts, lands, _ = reducing[gi]
        parts, lands = reduce_wait(gi, send, recv, parts, lands, dy)
        for (_, _, _, (name, l0), _), part, land in zip(groups[gi], parts, lands):
            full[name] = chip_sum_rows(ids, land, part, full[name], l0)
    g_qa, g_oa, g_qb, g_ob, g_gate, g_up, g_down = half_swap_rows(
        [full[k] for k in ("qa", "oa", "qb", "ob", "gate", "up", "down")])

    d_tab_t = bias_vec_bwd(jnp.concatenate(dvrs, axis=0))
    small = jnp.concatenate([d_tab_t.reshape(-1), dgam.reshape(-1), dbet.reshape(-1)])
    n_small = small.shape[0]
    rows = -(-n_small // (8 * 128)) * 8
    tot = all_sum_small(jnp.pad(small, (0, rows * 128 - n_small)).reshape(rows, 128)).reshape(-1)
    n_tab, n_ln = H * REL_PAD, 3 * depth * D
    g_rel = tot[:n_tab].reshape(H, REL_PAD)[:, :N_REL].T
    ln_cols = D // N_CHIPS

    def ln_shard(flat):
        return lax.dynamic_slice_in_dim(flat.reshape(depth, 3, D), me * ln_cols, ln_cols, axis=2)

    g_lng = ln_shard(tot[n_tab:n_tab + n_ln])
    g_lnb = ln_shard(tot[n_tab + n_ln:n_tab + 2 * n_ln])

    def upd(w, g, m, v):
        if w.ndim == 2:
            return tuple(a[0] for a in adamw(w[None], g, m[None], v[None]))
        return adamw(w, g, m, v)

    res = [
        upd(w_qkv_a, g_qa, m_w_qkv_a, v_w_qkv_a),
        upd(w_o_a, g_oa, m_w_o_a, v_w_o_a),
        upd(rel_bias, g_rel.reshape(1, N_REL, H), m_rel_bias, v_rel_bias),
        upd(w_qkv_b, g_qb, m_w_qkv_b, v_w_qkv_b),
        upd(w_o_b, g_ob, m_w_o_b, v_w_o_b),
        tuple(tr(a) for a in upd(gate_t, g_gate, tr(m_ffn_w_gate), tr(v_ffn_w_gate))),
        tuple(tr(a) for a in upd(up_t, g_up, tr(m_ffn_w_up), tr(v_ffn_w_up))),
        upd(ffn_w_down, g_down, m_ffn_w_down, v_ffn_w_down),
        upd(ln_g, g_lng, m_ln_g, v_ln_g),
        upd(ln_b, g_lnb, m_ln_b, v_ln_b),
    ]
    grads = [r[0] for r in res]
    deltas = [r[1] for r in res]
    new_m = [r[2] for r in res]
    new_v = [r[3] for r in res]
    return (loss, grad_x, *grads, *deltas, *new_m, *new_v)
```

```python
import math

import jax
import jax.numpy as jnp
from jax import lax
from jax.experimental import pallas as pl
from jax.experimental.pallas import tpu as pltpu

F32 = jnp.float32
BF16 = jnp.bfloat16
MESH = pl.DeviceIdType.MESH

N_CHIPS = 4
HEAD_DIM = 64
CHUNK = 64
LEFT_CHUNKS = 8
LOOKBACK = LEFT_CHUNKS * CHUNK
REL_CLIP = 128
N_REL = 2 * REL_CLIP + 1
REL_PAD = 384
SB_TILE = 256
QB_A = 256
KW_A = QB_A + LOOKBACK
VR_W = 1024
VR_C0 = KW_A - 1
assert math.frexp(HEAD_DIM ** -0.5)[0] == 0.5
LN_EPS = 1e-5
ADAM_LR, ADAM_B1, ADAM_B2, ADAM_EPS, ADAM_WD, ADAM_STEP = 0.001, 0.9, 0.999, 1e-08, 0.01, 10
NEG = -1e30
VMEM_LIMIT = 56 * 1024 * 1024

NT_DIMS = (((1,), (1,)), ((), ()))
TN_DIMS = (((0,), (0,)), ((), ()))
ANY = pl.BlockSpec(memory_space=pl.ANY)


def _params(sem=None):
    return pltpu.CompilerParams(dimension_semantics=sem, vmem_limit_bytes=VMEM_LIMIT)


def _tile(n, pref):
    t = min(n, pref)
    assert n % t == 0, (n, pref)
    return t


def _sigmoid(z):
    return 1.0 / (1.0 + jnp.exp(-z))


def _mm_call(name, operands, in_specs, out_shape, out_spec, grid, dims_list, acc_shape,
             add_coef=None, aliases=None):
    n_pairs = len(dims_list)
    nk = grid[-1]
    has_add = add_coef is not None
    n_alias = len(aliases) if aliases else 0

    def body(*refs):
        pair_refs = refs[:2 * n_pairs]
        pos = 2 * n_pairs
        add_ref = refs[pos] if has_add else None
        pos += (1 if has_add else 0) + n_alias
        o_ref = refs[pos]
        acc_ref = refs[pos + 1] if nk > 1 else None

        def product():
            part = None
            for i, dims in enumerate(dims_list):
                d = lax.dot_general(pair_refs[2 * i][...], pair_refs[2 * i + 1][...], dims,
                                    preferred_element_type=F32)
                part = d if part is None else part + d
            return part

        def finish(r):
            if has_add:
                r = r + add_coef * add_ref[...]
            o_ref[...] = r.astype(o_ref.dtype)

        if nk == 1:
            finish(product())
        else:
            k = pl.program_id(len(grid) - 1)

            @pl.when(k == 0)
            def _():
                acc_ref[...] = jnp.zeros_like(acc_ref)

            acc_ref[...] += product()

            @pl.when(k == nk - 1)
            def _():
                finish(acc_ref[...])

    sem = ("parallel",) * (len(grid) - 1) + ("arbitrary",)
    return pl.pallas_call(
        body, name=name, grid=grid, in_specs=in_specs, out_specs=out_spec, out_shape=out_shape,
        scratch_shapes=[pltpu.VMEM(acc_shape, F32)] if nk > 1 else [],
        input_output_aliases=aliases or {},
        compiler_params=_params(sem),
    )(*operands)


def qkv_proj(xb, w, l):
    M, D = xb.shape
    C = w.shape[-1]
    tm = _tile(M, 1024)
    return _mm_call(
        "qkv_proj", (xb, w),
        [pl.BlockSpec((tm, D), lambda p, i, k: (i, 0)),
         pl.BlockSpec((None, None, D, C), lambda p, i, k: (p, l, 0, 0))],
        jax.ShapeDtypeStruct((M, N_CHIPS * C), BF16),
        pl.BlockSpec((tm, C), lambda p, i, k: (i, p)),
        (N_CHIPS, M // tm, 1), [(((1,), (0,)), ((), ()))], None)


def o_proj_bwd(dyb, w, l, after):
    M, D = dyb.shape
    R = w.shape[2]
    tm = _tile(M, 512)

    def body(a_ref, w_ref, _after, o_ref):
        o_ref[...] = lax.dot_general(a_ref[...], w_ref[...], NT_DIMS,
                                     preferred_element_type=F32).astype(o_ref.dtype)

    return pl.pallas_call(
        body, name="o_proj_bwd", grid=(N_CHIPS, M // tm),
        in_specs=[pl.BlockSpec((tm, D), lambda p, i: (i, 0)),
                  pl.BlockSpec((None, None, R, D), lambda p, i: (p, l, 0, 0)),
                  pl.BlockSpec(after.shape, lambda p, i: (0, 0))],
        out_specs=pl.BlockSpec((tm, R), lambda p, i: (i, p)),
        out_shape=jax.ShapeDtypeStruct((M, N_CHIPS * R), BF16),
        compiler_params=_params(("parallel", "parallel")),
    )(dyb, w, after)


def qkv_proj_bwd(dqkv, w, l, dpre, alpha, ln=None):
    _, M, D = dqkv.shape
    C = w.shape[3]
    tm = _tile(M, 512)
    T = math.gcd(D, C)

    def body(a_ref, w_ref, add_ref, *rest):
        acc = alpha * add_ref[...]
        for t in range(3 * D // T):
            pa, ca = divmod(t * T, D)
            pw, cw = divmod(t * T, C)
            acc = acc + lax.dot_general(a_ref[pa, :, ca:ca + T], w_ref[pw, :, cw:cw + T], NT_DIMS,
                                        preferred_element_type=F32)
        if ln is None:
            rest[0][...] = acc
        else:
            xh_ref, rs_ref, g_ref, _dg, _db, dp_ref, db16_ref, dg_out, db_out = rest
            _ln_bwd_tile(acc, xh_ref, rs_ref, g_ref, ln[3], dp_ref, db16_ref, dg_out, db_out)

    row = pl.BlockSpec((tm, D), lambda i: (i, 0))
    ins = [pl.BlockSpec((3, tm, D), lambda i: (0, i, 0)),
           pl.BlockSpec((N_CHIPS, None, D, C), lambda i: (0, l, 0, 0)), row]
    if ln is None:
        return pl.pallas_call(
            body, name="qkv_proj_bwd", grid=(M // tm,), in_specs=ins,
            out_specs=row, out_shape=jax.ShapeDtypeStruct((M, D), F32),
            compiler_params=_params(("parallel",)),
        )(dqkv, w, dpre)
    xhat, rstd, gam, _, dgam, dbet, ln_row = ln
    specs = _ln_bwd_specs(M, D, tm, ln_row, dgam, dbet)
    return pl.pallas_call(
        body, name="qkv_proj_bwd_ln", grid=(M // tm,), in_specs=ins + specs["in"],
        out_specs=specs["out"], out_shape=specs["out_shape"],
        input_output_aliases={6: 2, 7: 3},
        compiler_params=_params(("arbitrary",)),
    )(dqkv, w, dpre, xhat, rstd, gam, dgam, dbet)


def ffn_dx(dg, du, wgt, wut, l, dpre, alpha, ln=None):
    _, M, Fs = dg.shape
    D = wgt.shape[3]
    tm = _tile(M, 256)

    def body(dg_ref, wg_ref, du_ref, wu_ref, add_ref, *rest):
        acc = alpha * add_ref[...]
        for p in range(N_CHIPS):
            acc = acc + jnp.dot(dg_ref[p], wg_ref[p], preferred_element_type=F32)
            acc = acc + jnp.dot(du_ref[p], wu_ref[p], preferred_element_type=F32)
        if ln is None:
            rest[0][...] = acc
        else:
            xh_ref, rs_ref, g_ref, _dg, _db, dp_ref, db16_ref, dg_out, db_out = rest
            _ln_bwd_tile(acc, xh_ref, rs_ref, g_ref, ln[3], dp_ref, db16_ref, dg_out, db_out)

    act = pl.BlockSpec((N_CHIPS, tm, Fs), lambda i: (0, i, 0))
    wsp = pl.BlockSpec((N_CHIPS, None, Fs, D), lambda i: (0, l, 0, 0))
    row = pl.BlockSpec((tm, D), lambda i: (i, 0))
    if ln is None:
        return pl.pallas_call(
            body, name="ffn_dx", grid=(M // tm,),
            in_specs=[act, wsp, act, wsp, row],
            out_specs=row, out_shape=jax.ShapeDtypeStruct((M, D), F32),
            compiler_params=_params(("parallel",)),
        )(dg, wgt, du, wut, dpre)
    xhat, rstd, gam, _, dgam, dbet, ln_row = ln
    specs = _ln_bwd_specs(M, D, tm, ln_row, dgam, dbet)
    return pl.pallas_call(
        body, name="ffn_dx_ln", grid=(M // tm,),
        in_specs=[act, wsp, act, wsp, row] + specs["in"],
        out_specs=specs["out"], out_shape=specs["out_shape"],
        input_output_aliases={8: 2, 9: 3},
        compiler_params=_params(("arbitrary",)),
    )(dg, wgt, du, wut, dpre, xhat, rstd, gam, dgam, dbet)


def _dw_call(name, buf, l, a, b, a_spec, b_spec, M, tk):
    _, _, R, C = buf.shape
    return _mm_call(
        name, (a, b, buf),
        [a_spec, b_spec, ANY],
        jax.ShapeDtypeStruct(buf.shape, buf.dtype),
        pl.BlockSpec((None, None, R, C), lambda p, k: (l, p, 0, 0)),
        (N_CHIPS, M // tk), [TN_DIMS], (R, C), aliases={2: 0})


def ffn_bwd(gdown, ggate, gup, l, xb, dyb, wd, g, u, h, after):
    M, D = dyb.shape
    Fs = wd.shape[2]
    tm = _tile(M, 1024)
    n = M // tm

    def body(x_ref, dy_ref, wd_ref, g_ref, u_ref, h_ref, _gd, _gg, _gu, _after,
             dg_ref, du_ref, gd_ref, gg_ref, gu_ref, acc_d, acc_g, acc_u):
        i = pl.program_id(1)

        @pl.when(i == 0)
        def _():
            acc_d[...] = jnp.zeros_like(acc_d)
            acc_g[...] = jnp.zeros_like(acc_g)
            acc_u[...] = jnp.zeros_like(acc_u)

        dy = dy_ref[...]
        dh = lax.dot_general(dy, wd_ref[...], NT_DIMS, preferred_element_type=F32)
        gf = g_ref[...].astype(F32)
        sig = _sigmoid(gf)
        silu = gf * sig
        dg = (dh * u_ref[...].astype(F32) * (sig * (1.0 + gf - silu))).astype(BF16)
        du = (dh * silu).astype(BF16)
        dg_ref[...] = dg
        du_ref[...] = du
        x = x_ref[...]
        acc_d[...] += lax.dot_general(h_ref[...], dy, TN_DIMS, preferred_element_type=F32)
        acc_g[...] += lax.dot_general(dg, x, TN_DIMS, preferred_element_type=F32)
        acc_u[...] += lax.dot_general(du, x, TN_DIMS, preferred_element_type=F32)

        @pl.when(i == n - 1)
        def _():
            gd_ref[...] = acc_d[...].astype(gd_ref.dtype)
            gg_ref[...] = acc_g[...].astype(gg_ref.dtype)
            gu_ref[...] = acc_u[...].astype(gu_ref.dtype)

    row = pl.BlockSpec((tm, D), lambda p, i: (i, 0))
    act = pl.BlockSpec((None, tm, Fs), lambda p, i: (p, i, 0))
    ash = jax.ShapeDtypeStruct((N_CHIPS, M, Fs), BF16)
    w_blk = pl.BlockSpec((None, None, Fs, D), lambda p, i: (l, p, 0, 0))
    return pl.pallas_call(
        body, name="ffn_bwd", grid=(N_CHIPS, n),
        in_specs=[row, row, pl.BlockSpec((None, None, Fs, D), lambda p, i: (p, l, 0, 0)),
                  act, act, act, ANY, ANY, ANY, pl.BlockSpec(after.shape, lambda p, i: (0, 0))],
        out_specs=[act, act, w_blk, w_blk, w_blk],
        out_shape=[ash, ash] + [jax.ShapeDtypeStruct(b.shape, b.dtype) for b in (gdown, ggate, gup)],
        scratch_shapes=[pltpu.VMEM((Fs, D), F32), pltpu.VMEM((Fs, D), F32), pltpu.VMEM((Fs, D), F32)],
        input_output_aliases={6: 2, 7: 3, 8: 4},
        compiler_params=_params(("parallel", "arbitrary")),
    )(xb, dyb, wd, g, u, h, gdown, ggate, gup, after)


def dw_qkv(buf, l, xb, dqkv):
    M, D = xb.shape
    C = buf.shape[-1]
    tk = _tile(M, 512)
    n = M // tk
    T = math.gcd(D, C)
    nt = 3 * D // T

    def body(x_ref, b_ref, _buf, o_ref, acc_ref):
        k = pl.program_id(0)

        @pl.when(k == 0)
        def _():
            acc_ref[...] = jnp.zeros_like(acc_ref)

        xt = x_ref[...].astype(F32).T.astype(BF16)
        for t in range(nt):
            pa, ca = divmod(t * T, D)
            acc_ref[t] += jnp.dot(xt, b_ref[pa, :, ca:ca + T], preferred_element_type=F32)

        @pl.when(k == n - 1)
        def _():
            for t in range(nt):
                pw, cw = divmod(t * T, C)
                o_ref[pw, :, cw:cw + T] = acc_ref[t].astype(o_ref.dtype)

    return pl.pallas_call(
        body, name="dw_qkv", grid=(n,),
        in_specs=[pl.BlockSpec((tk, D), lambda k: (k, 0)),
                  pl.BlockSpec((3, tk, D), lambda k: (0, k, 0)), ANY],
        out_specs=pl.BlockSpec((None, N_CHIPS, D, C), lambda k: (l, 0, 0, 0)),
        out_shape=jax.ShapeDtypeStruct(buf.shape, buf.dtype),
        scratch_shapes=[pltpu.VMEM((nt, D, T), F32)],
        input_output_aliases={2: 0},
        compiler_params=_params(("arbitrary",)),
    )(xb, dqkv, buf)


def dw_o(buf, l, o, dyb):
    M, D = dyb.shape
    R = buf.shape[2]
    tk = _tile(M, 1024)
    return _dw_call("dw_o", buf, l, o, dyb,
                    pl.BlockSpec((tk, R), lambda p, k: (k, p)),
                    pl.BlockSpec((tk, D), lambda p, k: (k, 0)), M, tk)


def ffn_up(xb, wgt, wut, l):
    M, D = xb.shape
    Fs = wgt.shape[2]
    tm = _tile(M, 1024)

    def body(x_ref, wg_ref, wu_ref, g_ref, u_ref, h_ref):
        x = x_ref[...]
        g = lax.dot_general(x, wg_ref[...], NT_DIMS, preferred_element_type=F32)
        u = lax.dot_general(x, wu_ref[...], NT_DIMS, preferred_element_type=F32)
        g_ref[...] = g.astype(BF16)
        u_ref[...] = u.astype(BF16)
        h_ref[...] = (g * _sigmoid(g) * u).astype(BF16)

    wsp = pl.BlockSpec((None, None, Fs, D), lambda p, i: (p, l, 0, 0))
    osp = pl.BlockSpec((None, tm, Fs), lambda p, i: (p, i, 0))
    osh = jax.ShapeDtypeStruct((N_CHIPS, M, Fs), BF16)
    return pl.pallas_call(
        body, name="ffn_up", grid=(N_CHIPS, M // tm),
        in_specs=[pl.BlockSpec((tm, D), lambda p, i: (i, 0)), wsp, wsp],
        out_specs=[osp, osp, osp], out_shape=[osh, osh, osh],
        compiler_params=_params(("parallel", "parallel")),
    )(xb, wgt, wut)


def mm_ln(a, w, l, x, gam, bet, alpha, scale, a_piece_major):
    M, D = x.shape
    R = w.shape[2]
    tm = _tile(M, 512)
    if a_piece_major:
        a_spec = pl.BlockSpec((N_CHIPS, tm, R), lambda i: (0, i, 0))
    else:
        a_spec = pl.BlockSpec((tm, N_CHIPS * R), lambda i: (i, 0))

    def body(a_ref, w_ref, x_ref, g_ref, b_ref, xo_ref, xb_ref, xh_ref, rs_ref):
        y = None
        for p in range(N_CHIPS):
            a = a_ref[p] if a_piece_major else a_ref[:, p * R:(p + 1) * R]
            d = jnp.dot(a, w_ref[p], preferred_element_type=F32)
            y = d if y is None else y + d
        pre = alpha * x_ref[...] + scale * y
        mu = jnp.mean(pre, axis=-1, keepdims=True)
        cen = pre - mu
        var = jnp.mean(cen * cen, axis=-1, keepdims=True)
        rstd = lax.rsqrt(var + LN_EPS)
        xhat = cen * rstd
        out = xhat * g_ref[...] + b_ref[...]
        xo_ref[...] = out
        xb_ref[...] = out.astype(BF16)
        xh_ref[...] = xhat
        rs_ref[...] = rstd

    row = pl.BlockSpec((tm, D), lambda i: (i, 0))
    vec = pl.BlockSpec((1, D), lambda i: (0, 0))
    return pl.pallas_call(
        body, name="mm_ln", grid=(M // tm,),
        in_specs=[a_spec, pl.BlockSpec((N_CHIPS, None, R, D), lambda i: (0, l, 0, 0)), row, vec, vec],
        out_specs=[row, row, row, pl.BlockSpec((tm, 1), lambda i: (i, 0))],
        out_shape=[jax.ShapeDtypeStruct((M, D), F32), jax.ShapeDtypeStruct((M, D), BF16),
                   jax.ShapeDtypeStruct((M, D), F32), jax.ShapeDtypeStruct((M, 1), F32)],
        compiler_params=_params(("parallel",)),
    )(a, w, x, gam, bet)


def ln_bwd(dy, xhat, rstd, gam, scale, dgam, dbet, row):
    M, D = dy.shape
    tm = _tile(M, 512)

    def body(dy_ref, xh_ref, rs_ref, g_ref, _dg, _db, dp_ref, db16_ref, dg_ref, dbt_ref):
        _ln_bwd_tile(dy_ref[...], xh_ref, rs_ref, g_ref, scale, dp_ref, db16_ref, dg_ref, dbt_ref)

    specs = _ln_bwd_specs(M, D, tm, row, dgam, dbet)
    return pl.pallas_call(
        body, name="ln_bwd", grid=(M // tm,),
        in_specs=[pl.BlockSpec((tm, D), lambda i: (i, 0))] + specs["in"],
        out_specs=specs["out"], out_shape=specs["out_shape"],
        input_output_aliases={4: 2, 5: 3},
        compiler_params=_params(("arbitrary",)),
    )(dy, xhat, rstd, gam, dgam, dbet)


def _ln_bwd_tile(dy_v, xh_ref, rs_ref, g_ref, scale, dp_ref, db16_ref, dg_ref, dbt_ref):
    i = pl.program_id(0)
    xh = xh_ref[...]
    dxh = dy_v * g_ref[...]
    m1 = jnp.mean(dxh, axis=-1, keepdims=True)
    m2 = jnp.mean(dxh * xh, axis=-1, keepdims=True)
    dpre = rs_ref[...] * (dxh - m1 - xh * m2)
    dp_ref[...] = dpre
    db16_ref[...] = (scale * dpre).astype(BF16)
    dgp = jnp.sum(dy_v * xh, axis=0, keepdims=True)
    dbp = jnp.sum(dy_v, axis=0, keepdims=True)

    @pl.when(i == 0)
    def _():
        dg_ref[...] = dgp
        dbt_ref[...] = dbp

    @pl.when(i > 0)
    def _():
        dg_ref[...] += dgp
        dbt_ref[...] += dbp


def _ln_bwd_specs(M, D, tm, row, dgam, dbet):
    tok = pl.BlockSpec((tm, D), lambda i: (i, 0))
    vec = pl.BlockSpec((1, D), lambda i: (0, 0))
    acc = pl.BlockSpec((None, 1, D), lambda i: (row, 0, 0))
    return {"in": [tok, pl.BlockSpec((tm, 1), lambda i: (i, 0)), vec, ANY, ANY],
            "out": [tok, tok, acc, acc],
            "out_shape": [jax.ShapeDtypeStruct((M, D), F32), jax.ShapeDtypeStruct((M, D), BF16),
                          jax.ShapeDtypeStruct(dgam.shape, F32), jax.ShapeDtypeStruct(dbet.shape, F32)]}


def loss_head(y, tgt):
    M, D = y.shape
    tm = _tile(M, 512)
    n = M // tm

    def body(y_ref, t_ref, dy_ref, l_ref, acc_ref):
        i = pl.program_id(0)
        e = y_ref[...] - t_ref[...]
        dy_ref[...] = e * (1.0 / D)
        part = jnp.sum(e * e, axis=0, keepdims=True)

        @pl.when(i == 0)
        def _():
            acc_ref[...] = part

        @pl.when(i > 0)
        def _():
            acc_ref[...] += part

        @pl.when(i == n - 1)
        def _():
            l_ref[...] = (0.5 / D) * jnp.sum(acc_ref[...], axis=1, keepdims=True)

    row = pl.BlockSpec((tm, D), lambda i: (i, 0))
    return pl.pallas_call(
        body, name="loss_head", grid=(n,),
        in_specs=[row, row],
        out_specs=[row, pl.BlockSpec((1, 1), lambda i: (0, 0))],
        out_shape=[jax.ShapeDtypeStruct((M, D), F32), jax.ShapeDtypeStruct((1, 1), F32)],
        scratch_shapes=[pltpu.VMEM((1, D), F32)],
        compiler_params=_params(("arbitrary",)),
    )(y, tgt)


def _rel_onehot_t():
    r = lax.broadcasted_iota(jnp.int32, (REL_PAD, VR_W), 0)
    n = lax.broadcasted_iota(jnp.int32, (REL_PAD, VR_W), 1)
    idx = jnp.clip(VR_C0 - n, -REL_CLIP, REL_CLIP) + REL_CLIP
    return (r == idx).astype(F32)


def bias_vec(tab_t):
    H = tab_t.shape[0]

    def body(t_ref, o_ref):
        o_ref[...] = jnp.dot(t_ref[...], _rel_onehot_t(), precision=lax.Precision.HIGHEST,
                             preferred_element_type=F32)

    return pl.pallas_call(
        body, name="bias_vec", out_shape=jax.ShapeDtypeStruct((H, VR_W), F32),
        compiler_params=_params(),
    )(tab_t)


def bias_vec_bwd(dvr):
    n, H, _ = dvr.shape

    def body(d_ref, o_ref):
        tot = d_ref[0]
        for i in range(1, n):
            tot = tot + d_ref[i]
        o_ref[...] = lax.dot_general(tot, _rel_onehot_t(), NT_DIMS, precision=lax.Precision.HIGHEST,
                                     preferred_element_type=F32)

    return pl.pallas_call(
        body, name="bias_vec_bwd", out_shape=jax.ShapeDtypeStruct((H, REL_PAD), F32),
        compiler_params=_params(),
    )(dvr)


def _a_bias_mask(vr_row):
    xb = jnp.broadcast_to(vr_row, (QB_A, VR_W))
    tile = pltpu.roll(xb, VR_W - (QB_A - 1), 1, stride=1, stride_axis=0)[:, :KW_A]
    qc = lax.broadcasted_iota(jnp.int32, (QB_A, KW_A), 0) // CHUNK
    kc = lax.broadcasted_iota(jnp.int32, (QB_A, KW_A), 1) // CHUNK
    valid = (kc >= qc) & (kc <= qc + LEFT_CHUNKS)
    return jnp.where(valid, tile, NEG)


def _a_diag_sums(db_acc, h):
    acc8 = None
    for a in range(QB_A // 8):
        grp = db_acc[h, 8 * a:8 * a + 8, :]
        shift = QB_A - 8 - 8 * a
        if shift:
            grp = pltpu.roll(grp, shift, 1)
        acc8 = grp if acc8 is None else acc8 + grp
    sub = lax.broadcasted_iota(jnp.int32, (8, VR_W), 0)
    tot = jnp.zeros((8, VR_W), F32)
    for b in range(8):
        moved = pltpu.roll(acc8, 7 - b, 1) if b < 7 else acc8
        tot = tot + jnp.where(sub == b, moved, 0.0)
    return jnp.sum(tot, axis=0, keepdims=True)


def _a_blocks(S):
    out = []
    for qi in range(S // QB_A):
        q0 = qi * QB_A
        ks = max(0, q0 - LOOKBACK)
        out.append((q0, ks, q0 + QB_A, ks - (q0 - LOOKBACK)))
    return out


def _head_specs(S, HP):
    q = pl.BlockSpec((S, 2 * HEAD_DIM), lambda b, hp: (b, hp))
    k = pl.BlockSpec((S, 2 * HEAD_DIM), lambda b, hp: (b, HP + hp))
    v = pl.BlockSpec((S, 2 * HEAD_DIM), lambda b, hp: (b, 2 * HP + hp))
    return q, k, v


def attn_a_fwd(qkv, vr, B, S):
    D = qkv.shape[1] // 3
    HP = D // (2 * HEAD_DIM)
    scale = HEAD_DIM ** -0.5
    blocks = _a_blocks(S)

    def body(q_ref, k_ref, v_ref, vr_ref, o_ref):
        heads = [slice(h * HEAD_DIM, (h + 1) * HEAD_DIM) for h in range(2)]
        bms = [_a_bias_mask(vr_ref[h:h + 1, :]) for h in range(2)]
        for pair in range(0, len(blocks), 2):
            chains = [(slice(q0, q0 + QB_A), ks, ke, joff, hs, bm)
                      for (q0, ks, ke, joff) in blocks[pair:pair + 2] for hs, bm in zip(heads, bms)]
            ss = [lax.dot_general(q_ref[rows, hs] * scale, k_ref[ks:ke, hs], NT_DIMS,
                                  preferred_element_type=F32) + bm[:, joff:]
                  for rows, ks, ke, joff, hs, bm in chains]
            ps = [jnp.exp(s - jnp.max(s, axis=-1, keepdims=True)) for s in ss]
            for (rows, ks, ke, _, hs, _), p in zip(chains, ps):
                den = jnp.sum(p, axis=-1, keepdims=True)
                o = jnp.dot(p.astype(BF16), v_ref[ks:ke, hs], preferred_element_type=F32) / den
                o_ref[rows, hs] = o.astype(BF16)

    qs, ks_, vs = _head_specs(S, HP)
    return pl.pallas_call(
        body, name="attn_a_fwd", grid=(B, HP),
        in_specs=[qs, ks_, vs, pl.BlockSpec((None, 2, VR_W), lambda b, hp: (hp, 0, 0))],
        out_specs=pl.BlockSpec((S, 2 * HEAD_DIM), lambda b, hp: (b, hp)),
        out_shape=jax.ShapeDtypeStruct((B * S, D), BF16),
        compiler_params=_params(("parallel", "parallel")),
    )(qkv, qkv, qkv, vr)


def attn_a_bwd(qkv, vr, do, B, S):
    D = qkv.shape[1] // 3
    HP = D // (2 * HEAD_DIM)
    scale = HEAD_DIM ** -0.5
    blocks = _a_blocks(S)

    def body(q_ref, k_ref, v_ref, vr_ref, do_ref, dqkv_ref, dvr_ref, dkt_acc, dvt_acc, db_acc):
        dkt_acc[...] = jnp.zeros_like(dkt_acc)
        dvt_acc[...] = jnp.zeros_like(dvt_acc)
        db_acc[...] = jnp.zeros_like(db_acc)
        heads = [slice(h * HEAD_DIM, (h + 1) * HEAD_DIM) for h in range(2)]
        bms = [_a_bias_mask(vr_ref[h:h + 1, :]) for h in range(2)]
        for pair in range(0, len(blocks), 2):
            chains, qts, dots = [], [], []
            for (q0, ks, ke, joff) in blocks[pair:pair + 2]:
                rows = slice(q0, q0 + QB_A)
                qt_pair = (q_ref[rows, :] * scale).astype(F32).T.astype(BF16)
                dot_pair = do_ref[rows, :].astype(F32).T.astype(BF16)
                for h, hs in enumerate(heads):
                    chains.append((rows, ks, ke, joff, h, hs))
                    qts.append(qt_pair[hs, :])
                    dots.append(dot_pair[hs, :])
            ss = [lax.dot_general(q_ref[rows, hs] * scale, k_ref[ks:ke, hs], NT_DIMS,
                                  preferred_element_type=F32) + bms[h][:, joff:]
                  for rows, ks, ke, joff, h, hs in chains]
            dps = [lax.dot_general(do_ref[rows, hs], v_ref[ks:ke, hs], NT_DIMS, preferred_element_type=F32)
                   for rows, ks, ke, _, _, hs in chains]
            ps, dsbs = [], []
            for (_, _, _, joff, h, _), s, dp in zip(chains, ss, dps):
                e = jnp.exp(s - jnp.max(s, axis=-1, keepdims=True))
                p = e / jnp.sum(e, axis=-1, keepdims=True)
                ds = p * (dp - jnp.sum(p * dp, axis=-1, keepdims=True))
                db_acc[h, :, joff:KW_A] += ds
                ps.append(p.astype(BF16))
                dsbs.append(ds.astype(BF16))
            for (rows, ks, ke, _, _, hs), p, dsb, qt, dot_ in zip(chains, ps, dsbs, qts, dots):
                dq = jnp.dot(dsb, k_ref[ks:ke, hs], preferred_element_type=F32) * scale
                dqkv_ref[0, rows, hs] = dq.astype(BF16)
                dkt_acc[hs, ks:ke] += jnp.dot(qt, dsb, preferred_element_type=F32)
                dvt_acc[hs, ks:ke] += jnp.dot(dot_, p, preferred_element_type=F32)
        for h in range(2):
            dvr_ref[h:h + 1, :] = _a_diag_sums(db_acc, h)
        dqkv_ref[1] = dkt_acc[...].T.astype(BF16)
        dqkv_ref[2] = dvt_acc[...].T.astype(BF16)

    qs, ks_, vs = _head_specs(S, HP)
    hd = pl.BlockSpec((S, 2 * HEAD_DIM), lambda b, hp: (b, hp))
    return pl.pallas_call(
        body, name="attn_a_bwd", grid=(B, HP),
        in_specs=[qs, ks_, vs, pl.BlockSpec((None, 2, VR_W), lambda b, hp: (hp, 0, 0)), hd],
        out_specs=[pl.BlockSpec((3, S, 2 * HEAD_DIM), lambda b, hp: (0, b, hp)),
                   pl.BlockSpec((None, None, 2, VR_W), lambda b, hp: (b, hp, 0, 0))],
        out_shape=[jax.ShapeDtypeStruct((3, B * S, D), BF16), jax.ShapeDtypeStruct((B, HP, 2, VR_W), F32)],
        scratch_shapes=[pltpu.VMEM((2 * HEAD_DIM, S), F32), pltpu.VMEM((2 * HEAD_DIM, S), F32),
                        pltpu.VMEM((2, QB_A, VR_W), F32)],
        compiler_params=_params(("parallel", "parallel")),
    )(qkv, qkv, qkv, vr, do)


def _tri(cmp):
    j = lax.broadcasted_iota(jnp.int32, (SB_TILE, SB_TILE), 0)
    s = lax.broadcasted_iota(jnp.int32, (SB_TILE, SB_TILE), 1)
    return cmp(j, s).astype(BF16)


def _cumsum_mm(x, tri):
    hi = x.astype(BF16)
    mid = (x - hi.astype(F32)).astype(BF16)
    return jnp.dot(hi, tri, preferred_element_type=F32) + jnp.dot(mid, tri, preferred_element_type=F32)


def _sb_logs(q, k, diagonal):
    z = lax.dot_general(q, k, NT_DIMS, preferred_element_type=F32)
    log_b = jnp.minimum(z, 0.0) - jnp.log(1.0 + jnp.exp(-jnp.abs(z)))
    log_1mb = log_b - z
    if not diagonal:
        return log_b, log_1mb, None
    row = lax.broadcasted_iota(jnp.int32, (SB_TILE, SB_TILE), 0)
    col = lax.broadcasted_iota(jnp.int32, (SB_TILE, SB_TILE), 1)
    causal = col < row
    return log_b, jnp.where(causal, log_1mb, 0.0), causal


def attn_b_fwd(qkv, B, S):
    D = qkv.shape[1] // 3
    HP = D // (2 * HEAD_DIM)
    scale = HEAD_DIM ** -0.5
    nb = S // SB_TILE

    def body(q_ref, k_ref, v_ref, o_ref, nt_ref):
        tri = _tri(lambda j, s: j > s)
        heads = [slice(h * HEAD_DIM, (h + 1) * HEAD_DIM) for h in range(2)]

        def q_block(qb, n_pairs, parity):
            q0 = pl.multiple_of(qb * SB_TILE, SB_TILE)
            rows = pl.ds(q0, SB_TILE)
            qs = [q_ref[rows, hs] * scale for hs in heads]

            def step(blocks, state):
                chains = [(h, kb, diagonal, pl.ds(pl.multiple_of(kb * SB_TILE, SB_TILE), SB_TILE))
                          for h in range(2) for kb, diagonal in blocks]
                logs = [_sb_logs(qs[h], k_ref[keys, heads[h]], diagonal)
                        for h, _, diagonal, keys in chains]
                sums = [_cumsum_mm(log_1mb, tri) for _, log_1mb, _ in logs]
                rights = [state[0][0], state[1][0]]
                accs = [state[0][1], state[1][1]]
                weights_ = []
                for (h, _, diagonal, _), (log_b, log_1mb, causal), csum in zip(chains, logs, sums):
                    a = jnp.exp(log_b + csum + rights[h])
                    if diagonal:
                        a = jnp.where(causal, a, 0.0)
                    weights_.append(a.astype(BF16))
                    rights[h] = rights[h] + jnp.sum(log_1mb, axis=-1, keepdims=True)
                for (h, _, _, keys), a in zip(chains, weights_):
                    accs[h] = accs[h] + jnp.dot(a, v_ref[keys, heads[h]], preferred_element_type=F32)
                return ((rights[0], accs[0]), (rights[1], accs[1]))

            zero = (jnp.zeros((SB_TILE, 1), F32), jnp.zeros((SB_TILE, HEAD_DIM), F32))
            first = [(qb, True)] + ([(qb - 1, False)] if parity else [])
            top = qb - len(first)
            state = lax.fori_loop(
                0, n_pairs, lambda t, st: step([(top - 2 * t, False), (top - 2 * t - 1, False)], st),
                step(first, (zero, zero)))
            for hs, (right, acc) in zip(heads, state):
                o_ref[rows, hs] = acc.astype(BF16)
                nt_ref[rows, hs] = jnp.broadcast_to(right, (SB_TILE, HEAD_DIM))

        def q_pair_loop(j, carry):
            q_block(2 * j, j, 0)
            q_block(2 * j + 1, j, 1)
            return carry

        lax.fori_loop(0, nb // 2, q_pair_loop, 0)

    qs, ks_, vs = _head_specs(S, HP)
    hd = pl.BlockSpec((S, 2 * HEAD_DIM), lambda b, hp: (b, hp))
    return pl.pallas_call(
        body, name="attn_b_fwd", grid=(B, HP),
        in_specs=[qs, ks_, vs], out_specs=[hd, hd],
        out_shape=[jax.ShapeDtypeStruct((B * S, D), BF16), jax.ShapeDtypeStruct((B * S, D), F32)],
        compiler_params=_params(("parallel", "parallel")),
    )(qkv, qkv, qkv)


def attn_b_bwd(qkv, do, ntot, B, S):
    D = qkv.shape[1] // 3
    HP = D // (2 * HEAD_DIM)
    scale = HEAD_DIM ** -0.5
    nb = S // SB_TILE

    def body(q_ref, k_ref, v_ref, do_ref, nt_ref, dqkv_ref, dkt_acc, dvt_acc):
        tri_incl = _tri(lambda j, s: j <= s)
        tri_excl = _tri(lambda j, s: j < s)
        heads = [slice(h * HEAD_DIM, (h + 1) * HEAD_DIM) for h in range(2)]
        dkt_acc[...] = jnp.zeros_like(dkt_acc)
        dvt_acc[...] = jnp.zeros_like(dvt_acc)

        def q_block(qb, n_pairs, parity):
            q0 = pl.multiple_of(qb * SB_TILE, SB_TILE)
            rows = pl.ds(q0, SB_TILE)
            qt_pair = (q_ref[rows, :] * scale).astype(F32).T.astype(BF16)
            dot_pair = do_ref[rows, :].astype(F32).T.astype(BF16)
            per_head = [(hs, q_ref[rows, hs] * scale, do_ref[rows, hs], qt_pair[hs, :], dot_pair[hs, :],
                         nt_ref[rows, hs.start:hs.start + 1]) for hs in heads]

            def step(blocks, state):
                chains = [(h, kb, diagonal, pl.ds(pl.multiple_of(kb * SB_TILE, SB_TILE), SB_TILE))
                          for h in range(2) for kb, diagonal in blocks]
                ks = [k_ref[keys, per_head[h][0]] for h, _, _, keys in chains]
                logs = [_sb_logs(per_head[h][1], k, diagonal)
                        for (h, _, diagonal, _), k in zip(chains, ks)]
                das = [lax.dot_general(per_head[h][2], v_ref[keys, per_head[h][0]], NT_DIMS,
                                       preferred_element_type=F32) for h, _, _, keys in chains]
                sums = [_cumsum_mm(log_1mb, tri_incl) for _, log_1mb, _ in logs]
                left_n = [state[0][0], state[1][0]]
                left_d = [state[0][1], state[1][1]]
                dq_acc = [state[0][2], state[1][2]]
                a_s, dls = [], []
                for (h, _, diagonal, _), (log_b, log_1mb, causal), csum, da in zip(chains, logs, sums, das):
                    a = jnp.exp(log_b + (per_head[h][5] - left_n[h]) - csum)
                    if diagonal:
                        a = jnp.where(causal, a, 0.0)
                    a_s.append(a)
                    dls.append(a * da)
                    left_n[h] = left_n[h] + jnp.sum(log_1mb, axis=-1, keepdims=True)
                dsums = [_cumsum_mm(dl, tri_excl) for dl in dls]
                dzbs = []
                for (h, _, diagonal, _), (log_b, log_1mb, causal), dl, dsum in zip(chains, logs, dls, dsums):
                    dz = dl * jnp.exp(log_1mb) - (left_d[h] + dsum) * jnp.exp(log_b)
                    if diagonal:
                        dz = jnp.where(causal, dz, 0.0)
                    dzbs.append(dz.astype(BF16))
                    left_d[h] = left_d[h] + jnp.sum(dl, axis=-1, keepdims=True)
                for (h, kb, _, _), k, a, dzb in zip(chains, ks, a_s, dzbs):
                    hs, _, _, qt, dot_, _ = per_head[h]
                    dq_acc[h] = dq_acc[h] + jnp.dot(dzb, k, preferred_element_type=F32)
                    dkt_acc[kb, hs, :] += jnp.dot(qt, dzb, preferred_element_type=F32)
                    dvt_acc[kb, hs, :] += jnp.dot(dot_, a.astype(BF16), preferred_element_type=F32)
                return ((left_n[0], left_d[0], dq_acc[0]), (left_n[1], left_d[1], dq_acc[1]))

            zero1 = jnp.zeros((SB_TILE, 1), F32)
            zero = (zero1, zero1, jnp.zeros((SB_TILE, HEAD_DIM), F32))
            state = lax.fori_loop(
                0, n_pairs, lambda t, st: step([(2 * t, False), (2 * t + 1, False)], st), (zero, zero))
            last = ([(qb - 1, False)] if parity else []) + [(qb, True)]
            state = step(last, state)
            for hs, (_, _, dq_acc) in zip(heads, state):
                dqkv_ref[0, rows, hs] = (dq_acc * scale).astype(BF16)

        def q_pair_loop(j, carry):
            q_block(2 * j, j, 0)
            q_block(2 * j + 1, j, 1)
            return carry

        lax.fori_loop(0, nb // 2, q_pair_loop, 0)
        for kb in range(nb):
            dqkv_ref[1, kb * SB_TILE:(kb + 1) * SB_TILE, :] = dkt_acc[kb].T.astype(BF16)
            dqkv_ref[2, kb * SB_TILE:(kb + 1) * SB_TILE, :] = dvt_acc[kb].T.astype(BF16)

    qs, ks_, vs = _head_specs(S, HP)
    hd = pl.BlockSpec((S, 2 * HEAD_DIM), lambda b, hp: (b, hp))
    acc = pltpu.VMEM((nb, 2 * HEAD_DIM, SB_TILE), F32)
    return pl.pallas_call(
        body, name="attn_b_bwd", grid=(B, HP),
        in_specs=[qs, ks_, vs, hd, hd],
        out_specs=pl.BlockSpec((3, S, 2 * HEAD_DIM), lambda b, hp: (0, b, hp)),
        out_shape=jax.ShapeDtypeStruct((3, B * S, D), BF16),
        scratch_shapes=[acc, acc],
        compiler_params=_params(("parallel", "parallel")),
    )(qkv, qkv, qkv, do, ntot)


def _place():
    x, y, c = lax.axis_index("x"), lax.axis_index("y"), lax.axis_index("c")
    chips = [(1 - x, y), (x, 1 - y), (1 - x, 1 - y)]
    return x, y, c, 2 * x + y, chips


def _remote(src, dst, send_sem, recv_sem, dev):
    return pltpu.make_async_remote_copy(src_ref=src, dst_ref=dst, send_sem=send_sem, recv_sem=recv_sem,
                                        device_id=dev, device_id_type=MESH)


def gather_weights(shards):
    n = len(shards)

    def body(*refs):
        ins, outs = refs[:n], refs[n:2 * n]
        send1, recv1, send2, recv2, lsem = refs[2 * n:]
        x, y, c, me, chips = _place()
        local, first = [], []
        for f in range(n):
            hl = shards[f].shape[0] // 2
            cp = pltpu.make_async_copy(ins[f], outs[f].at[me], lsem.at[f])
            cp.start()
            local.append(cp)
            for j, (qx, qy) in enumerate(chips):
                half = pl.ds(c * hl, hl)
                cp = _remote(ins[f].at[half], outs[f].at[me, half],
                             send1.at[3 * f + j], recv1.at[3 * f + j], (qx, qy, c))
                cp.start()
                first.append(cp)
        passed = []
        for f in range(n):
            hl = shards[f].shape[0] // 2
            for j, (qx, qy) in enumerate(chips):
                slab = outs[f].at[2 * qx + qy, pl.ds(c * hl, hl)]
                _remote(slab, slab, send1.at[3 * f + j], recv1.at[3 * f + j], (x, y, c)).wait_recv()
                cp = _remote(slab, slab, send2.at[3 * f + j], recv2.at[3 * f + j], (x, y, 1 - c))
                cp.start()
                passed.append(cp)
        for f in range(n):
            hl = shards[f].shape[0] // 2
            for j, (qx, qy) in enumerate(chips):
                slab = outs[f].at[2 * qx + qy, pl.ds((1 - c) * hl, hl)]
                _remote(slab, slab, send2.at[3 * f + j], recv2.at[3 * f + j], (x, y, c)).wait_recv()
        for cp in first + passed:
            cp.wait_send()
        for cp in local:
            cp.wait()

    sems = pltpu.SemaphoreType.DMA((3 * n,))
    return pl.pallas_call(
        body, name="gather_weights",
        in_specs=[ANY] * n, out_specs=[ANY] * n,
        out_shape=[jax.ShapeDtypeStruct((N_CHIPS,) + s.shape, s.dtype) for s in shards],
        scratch_shapes=[sems, sems, sems, sems, pltpu.SemaphoreType.DMA((n,))],
        compiler_params=pltpu.CompilerParams(has_side_effects=True),
    )(*shards)


HBM = pl.BlockSpec(memory_space=pltpu.HBM)
SEM = pl.BlockSpec(memory_space=pltpu.SEMAPHORE)
EFFECT = pltpu.SideEffectType.DATAFLOW_SIDE_EFFECTING


def _in_hbm(a):
    return pltpu.with_memory_space_constraint(a, pltpu.HBM)


def _row_half(ref, which):
    hr = ref.shape[-2] // 2
    idx = [pl.ds(0, d) for d in ref.shape[:-2]] + [pl.ds(which * hr, hr), pl.ds(0, ref.shape[-1])]
    return ref.at[tuple(idx)]


def cast_place(ids, w, lead, L):
    R, C = w.shape[-2:]

    def body(ids_ref, w_ref, s_ref, land_ref):
        v = w_ref[...].astype(BF16)
        s_ref[...] = v
        land_ref[...] = v

    return pl.pallas_call(
        body, name="cast_place",
        grid_spec=pltpu.PrefetchScalarGridSpec(
            num_scalar_prefetch=1, grid=(L,),
            in_specs=[pl.BlockSpec((None,) * (w.ndim - 2) + (R, C), lambda l, ids: (*lead(l), 0, 0))],
            out_specs=[pl.BlockSpec((None, R, C), lambda l, ids: (l, 0, 0)),
                       pl.BlockSpec((None, None, R, C), lambda l, ids: (ids[1], l, 0, 0))]),
        out_shape=[jax.ShapeDtypeStruct((L, R, C), BF16), jax.ShapeDtypeStruct((N_CHIPS, L, R, C), BF16)],
        compiler_params=_params(("parallel",)),
    )(ids, w)


def gather_start(layers, zones, after):
    flat = [s for lay in layers for s in lay]
    flat_zones = [z for lay in zones for z in lay]
    counts = [len(lay) for lay in layers]
    n, nl = len(flat), len(layers)

    def body(*refs):
        ins, lands = refs[:n], refs[n:2 * n]
        send, recv = refs[2 * n + 1:2 * n + 1 + nl], refs[2 * n + 1 + nl:2 * n + 1 + 2 * nl]
        token = refs[-1]
        x, y, c, me, chips = _place()
        f = 0
        for li, cnt in enumerate(counts):
            for k in range(cnt):
                for j, (qx, qy) in enumerate(chips):
                    _remote(_row_half(ins[f], c), _row_half(lands[f].at[me], c),
                            send[li].at[3 * k + j], recv[li].at[3 * k + j], (qx, qy, c)).start()
                f += 1
        token[...] = jnp.zeros_like(token)

    sem_shapes = [pltpu.SemaphoreType.DMA((3 * cnt,)) for cnt in counts]
    land_shapes = [(N_CHIPS,) + s.shape for s in flat]
    res = pl.pallas_call(
        body, name="gather_start",
        out_shape=(*sem_shapes, *sem_shapes,
                   *[pltpu.HBM(s.shape, s.dtype) for s in flat],
                   *[pltpu.HBM(shp, s.dtype) for shp, s in zip(land_shapes, flat)],
                   jax.ShapeDtypeStruct((8, 128), F32)),
        in_specs=[HBM] * (2 * n) + [ANY],
        out_specs=(*[SEM] * (2 * nl), *[HBM] * (2 * n), pl.BlockSpec(memory_space=pltpu.VMEM)),
        input_output_aliases={k: 2 * nl + k for k in range(2 * n)},
        compiler_params=pltpu.CompilerParams(has_side_effects=EFFECT),
    )(*[_in_hbm(s) for s in flat], *[_in_hbm(z) for z in flat_zones], after)
    send, recv = res[:nl], res[nl:2 * nl]
    thru, lands, token = res[2 * nl:2 * nl + n], res[2 * nl + n:2 * nl + 2 * n], res[-1]
    out, f = [], 0
    for li, cnt in enumerate(counts):
        out.append((send[li], recv[li], list(thru[f:f + cnt]), list(lands[f:f + cnt])))
        f += cnt
    return out, token


def gather_wait(li, send, recv, shards, lands, after):
    m = len(shards)

    def body(*refs):
        ins, lnd = refs[:m], refs[m:2 * m]
        snd, rcv = refs[2 * m], refs[2 * m + 1]
        x, y, c, me, chips = _place()
        for k in range(m):
            for j, (qx, qy) in enumerate(chips):
                cp = _remote(_row_half(ins[k], c), _row_half(lnd[k].at[2 * qx + qy], c),
                             snd.at[3 * k + j], rcv.at[3 * k + j], (qx, qy, c))
                cp.wait_send()
                cp.wait_recv()

    res = pl.pallas_call(
        body, name=f"gather_wait_{li}",
        out_shape=(*[pltpu.HBM(s.shape, s.dtype) for s in shards],
                   *[pltpu.HBM(s.shape, s.dtype) for s in lands]),
        in_specs=[HBM] * (2 * m) + [SEM, SEM, ANY], out_specs=[HBM] * (2 * m),
        input_output_aliases={k: k for k in range(2 * m)},
        compiler_params=pltpu.CompilerParams(has_side_effects=EFFECT),
    )(*shards, *lands, send, recv, after)
    return list(res[:m]), list(res[m:])


def gather_forward(li, lands):
    m = len(lands)

    def body(*refs):
        outs = refs[m:2 * m]
        send, recv = refs[2 * m:]
        x, y, c, me, chips = _place()
        passed = []
        for k in range(m):
            for j, (qx, qy) in enumerate(chips):
                slab = _row_half(outs[k].at[2 * qx + qy], c)
                cp = _remote(slab, slab, send.at[3 * k + j], recv.at[3 * k + j], (x, y, 1 - c))
                cp.start()
                passed.append(cp)
        for k in range(m):
            for j, (qx, qy) in enumerate(chips):
                slab = _row_half(outs[k].at[2 * qx + qy], 1 - c)
                _remote(slab, slab, send.at[3 * k + j], recv.at[3 * k + j], (x, y, c)).wait_recv()
        for cp in passed:
            cp.wait_send()

    sems = pltpu.SemaphoreType.DMA((3 * m,))
    return pl.pallas_call(
        body, name=f"gather_forward_{li}",
        in_specs=[ANY] * m, out_specs=[ANY] * m,
        out_shape=[jax.ShapeDtypeStruct(s.shape, s.dtype) for s in lands],
        input_output_aliases={k: k for k in range(m)},
        scratch_shapes=[sems, sems],
        compiler_params=pltpu.CompilerParams(has_side_effects=True),
    )(*lands)


def pair_exchange_rows(li, grads):
    n = len(grads)

    def body(*refs):
        ins, outs = refs[:n], refs[n:2 * n]
        send, recv = refs[2 * n:]
        x, y, c, _, _ = _place()
        cps = []
        for k in range(n):
            cp = _remote(_row_half(ins[k], 1 - c), outs[k], send.at[k], recv.at[k], (x, y, 1 - c))
            cp.start()
            cps.append(cp)
        for cp in cps:
            cp.wait()

    sems = pltpu.SemaphoreType.DMA((n,))
    return pl.pallas_call(
        body, name=f"pair_exchange_{li}",
        in_specs=[ANY] * n, out_specs=[ANY] * n,
        out_shape=[jax.ShapeDtypeStruct(g.shape[:-2] + (g.shape[-2] // 2, g.shape[-1]), g.dtype)
                   for g in grads],
        scratch_shapes=[sems, sems],
        compiler_params=pltpu.CompilerParams(has_side_effects=True),
    )(*grads)


def pair_add_rows(ids, grad, recv):
    L, P, hr, C = recv.shape

    def body(ids_ref, a_ref, b_ref, o_ref):
        o_ref[...] = (a_ref[...].astype(F32) + b_ref[...].astype(F32)).astype(o_ref.dtype)

    blk = (None, P, hr, C)
    return pl.pallas_call(
        body, name="pair_add",
        grid_spec=pltpu.PrefetchScalarGridSpec(
            num_scalar_prefetch=1, grid=(L,),
            in_specs=[pl.BlockSpec(blk, lambda l, ids: (l, 0, ids[0], 0)),
                      pl.BlockSpec(blk, lambda l, ids: (l, 0, 0, 0))],
            out_specs=pl.BlockSpec(blk, lambda l, ids: (l, 0, 0, 0))),
        out_shape=jax.ShapeDtypeStruct(recv.shape, recv.dtype),
        compiler_params=_params(("parallel",)),
    )(ids, grad, recv)


def reduce_start(li, parts):
    m = len(parts)

    def body(*refs):
        ins, lands = refs[:m], refs[m:2 * m]
        send, recv = refs[2 * m], refs[2 * m + 1]
        token = refs[-1]
        x, y, c, me, chips = _place()
        for k in range(m):
            rows = pl.ds(0, parts[k].shape[0])
            for j, (qx, qy) in enumerate(chips):
                _remote(ins[k].at[rows, 2 * qx + qy], lands[k].at[rows, me],
                        send.at[3 * k + j], recv.at[3 * k + j], (qx, qy, c)).start()
        token[...] = jnp.zeros_like(token)

    sems = pltpu.SemaphoreType.DMA((3 * m,))
    res = pl.pallas_call(
        body, name=f"reduce_start_{li}",
        out_shape=(sems, sems, *[pltpu.HBM(s.shape, s.dtype) for s in parts],
                   *[pltpu.HBM(s.shape, s.dtype) for s in parts], jax.ShapeDtypeStruct((8, 128), F32)),
        in_specs=[HBM] * (2 * m),
        out_specs=(SEM, SEM, *[HBM] * (2 * m), pl.BlockSpec(memory_space=pltpu.VMEM)),
        input_output_aliases={k: 2 + k for k in range(2 * m)},
        compiler_params=pltpu.CompilerParams(has_side_effects=EFFECT),
    )(*[_in_hbm(s) for s in parts], *[_in_hbm(lax.empty(s.shape, s.dtype)) for s in parts])
    return res[0], res[1], list(res[2:2 + m]), list(res[2 + m:2 + 2 * m]), res[-1]


def reduce_wait(li, send, recv, parts, lands, after):
    m = len(parts)

    def body(*refs):
        ins, lnd = refs[:m], refs[m:2 * m]
        snd, rcv = refs[2 * m], refs[2 * m + 1]
        x, y, c, me, chips = _place()
        for k in range(m):
            rows = pl.ds(0, parts[k].shape[0])
            for j, (qx, qy) in enumerate(chips):
                cp = _remote(ins[k].at[rows, 2 * qx + qy], lnd[k].at[rows, 2 * qx + qy],
                             snd.at[3 * k + j], rcv.at[3 * k + j], (qx, qy, c))
                cp.wait_send()
                cp.wait_recv()

    res = pl.pallas_call(
        body, name=f"reduce_wait_{li}",
        out_shape=(*[pltpu.HBM(s.shape, s.dtype) for s in parts],
                   *[pltpu.HBM(s.shape, s.dtype) for s in lands]),
        in_specs=[HBM] * (2 * m) + [SEM, SEM, ANY], out_specs=[HBM] * (2 * m),
        input_output_aliases={k: k for k in range(2 * m)},
        compiler_params=pltpu.CompilerParams(has_side_effects=EFFECT),
    )(*parts, *lands, send, recv, after)
    return list(res[:m]), list(res[m:])


def chip_sum_rows(ids, land, part, gfull, l0):
    L, P, hr, C = land.shape

    def body(ids_ref, land_ref, part_ref, _g, o_ref):
        tot = None
        for q in range(P):
            term = jnp.where(ids_ref[1] == q, part_ref[...], land_ref[q]).astype(F32)
            tot = term if tot is None else tot + term
        o_ref[...] = tot

    return pl.pallas_call(
        body, name="chip_sum",
        grid_spec=pltpu.PrefetchScalarGridSpec(
            num_scalar_prefetch=1, grid=(L,),
            in_specs=[pl.BlockSpec((None, P, hr, C), lambda l, ids: (l, 0, 0, 0)),
                      pl.BlockSpec((None, None, hr, C), lambda l, ids: (l, ids[1], 0, 0)), ANY],
            out_specs=pl.BlockSpec((None, hr, C), lambda l, ids: (l0 + l, ids[0], 0))),
        out_shape=jax.ShapeDtypeStruct(gfull.shape, gfull.dtype),
        input_output_aliases={3: 0},
        compiler_params=_params(("arbitrary",)),
    )(ids, land, part, gfull)


def half_swap_rows(grads):
    n = len(grads)

    def body(*refs):
        outs = refs[n:2 * n]
        send, recv = refs[2 * n:]
        x, y, c, _, _ = _place()
        cps = []
        for k in range(n):
            mine = _row_half(outs[k], c)
            cp = _remote(mine, mine, send.at[k], recv.at[k], (x, y, 1 - c))
            cp.start()
            cps.append(cp)
        for k in range(n):
            theirs = _row_half(outs[k], 1 - c)
            _remote(theirs, theirs, send.at[k], recv.at[k], (x, y, c)).wait_recv()
        for cp in cps:
            cp.wait_send()

    sems = pltpu.SemaphoreType.DMA((n,))
    return pl.pallas_call(
        body, name="half_swap",
        in_specs=[ANY] * n, out_specs=[ANY] * n,
        out_shape=[jax.ShapeDtypeStruct(g.shape, g.dtype) for g in grads],
        input_output_aliases={k: k for k in range(n)},
        scratch_shapes=[sems, sems],
        compiler_params=pltpu.CompilerParams(has_side_effects=True),
    )(*grads)


def all_sum_small(v):
    R = v.shape[0]

    def body(v_ref, o_ref, land, send, recv):
        x, y, c, _, _ = _place()
        me = 4 * x + 2 * y + c
        land[me] = v_ref[...]
        peers = [(px, py, pc) for px in range(2) for py in range(2) for pc in range(2)]
        cps = []
        for k in range(1, 8):
            dev = (x ^ (k >> 2), y ^ ((k >> 1) & 1), c ^ (k & 1))
            cp = _remote(v_ref, land.at[me], send.at[k - 1], recv.at[k - 1], dev)
            cp.start()
            cps.append(cp)
        for k in range(1, 8):
            src = 4 * (x ^ (k >> 2)) + 2 * (y ^ ((k >> 1) & 1)) + (c ^ (k & 1))
            _remote(v_ref, land.at[src], send.at[k - 1], recv.at[k - 1], (x, y, c)).wait_recv()
        for cp in cps:
            cp.wait_send()
        tot = land[0]
        for d in range(1, len(peers)):
            tot = tot + land[d]
        o_ref[...] = tot

    sems = pltpu.SemaphoreType.DMA((7,))
    vm = pl.BlockSpec(memory_space=pltpu.VMEM)
    return pl.pallas_call(
        body, name="all_sum_small", in_specs=[vm], out_specs=vm,
        out_shape=jax.ShapeDtypeStruct(v.shape, F32),
        scratch_shapes=[pltpu.VMEM((8, R, 128), F32), sems, sems],
        compiler_params=pltpu.CompilerParams(has_side_effects=True),
    )(v)


def _row_tile(R):
    for t in range(min(R, 512) // 8 * 8, 7, -8):
        if R % t == 0:
            return t
    return R


def adamw(w, g, m, v):
    lead, (R, C) = w.shape[:-2], w.shape[-2:]
    tr = _row_tile(R)
    c1 = 1.0 / (1.0 - ADAM_B1 ** ADAM_STEP)
    c2 = 1.0 / (1.0 - ADAM_B2 ** ADAM_STEP)

    def body(w_ref, g_ref, m_ref, v_ref, go_ref, d_ref, nm_ref, nv_ref):
        gv = g_ref[...]
        go_ref[...] = gv
        nm = ADAM_B1 * m_ref[...] + (1.0 - ADAM_B1) * gv
        nv = ADAM_B2 * v_ref[...] + (1.0 - ADAM_B2) * (gv * gv)
        nm_ref[...] = nm
        nv_ref[...] = nv
        d_ref[...] = -ADAM_LR * ((nm * c1) / (jnp.sqrt(nv * c2) + ADAM_EPS) + ADAM_WD * w_ref[...])

    def flat(idx):
        l = 0
        for i, n in zip(idx, lead):
            l = l * n + i
        return l

    blk = pl.BlockSpec((None,) * len(lead) + (tr, C), lambda *ix: (*ix, 0))
    gblk = pl.BlockSpec((None, tr, C), lambda *ix: (flat(ix[:-1]), ix[-1], 0))
    osh = jax.ShapeDtypeStruct(w.shape, F32)
    return pl.pallas_call(
        body, name="adamw", grid=(*lead, R // tr),
        in_specs=[blk, gblk, blk, blk], out_specs=[blk, blk, blk, blk], out_shape=[osh, osh, osh, osh],
        compiler_params=_params(("parallel",) * (len(lead) + 1)),
    )(w, g, m, v)


def kernel(x, w_qkv_a, w_o_a, rel_bias, w_qkv_b, w_o_b, ffn_w_gate, ffn_w_up, ffn_w_down, ln_g, ln_b, loss_target, m_w_qkv_a, m_w_o_a, m_rel_bias, m_w_qkv_b, m_w_o_b, m_ffn_w_gate, m_ffn_w_up, m_ffn_w_down, m_ln_g, m_ln_b, v_w_qkv_a, v_w_o_a, v_rel_bias, v_w_qkv_b, v_w_o_b, v_ffn_w_gate, v_ffn_w_up, v_ffn_w_down, v_ln_g, v_ln_b):
    B, S, D = x.shape
    M = B * S
    depth = ffn_w_gate.shape[0]
    n_ffn = 2 * depth
    H = D // HEAD_DIM
    HP = H // 2
    Fs = ffn_w_gate.shape[-1]
    alpha = (2.0 * depth) ** 0.25
    assert S % QB_A == 0 and S % SB_TILE == 0 and rel_bias.shape == (N_REL, H)

    me = 2 * lax.axis_index("x") + lax.axis_index("y")
    ids = jnp.stack([lax.axis_index("c"), me]).astype(jnp.int32)

    def tr(a):
        return jnp.swapaxes(a, -1, -2)

    gate_t, up_t = tr(ffn_w_gate), tr(ffn_w_up)

    def mixer_slots(i):
        wq, wo, mix = (w_qkv_a, w_o_a, "a") if i % 2 == 0 else (w_qkv_b, w_o_b, "b")
        return [(w, (lambda l, i=i: (i // 2,)), 1, (role + mix, i // 2), [((i, 1), role, 0)])
                for role, w in (("q", wq), ("o", wo))]

    def ffn_slots(i, subs):
        first = 0 if subs[0] == 0 else 1
        return [(w, (lambda l, i=i, first=first: (i, first + l)), len(subs), (role, 2 * i + first),
                 [((i, j), role, k) for k, j in enumerate(subs)])
                for role, w in (("gate", gate_t), ("up", up_t), ("down", ffn_w_down))]

    groups = [ffn_slots(0, [0]), mixer_slots(0) + ffn_slots(0, [2])]
    groups += [mixer_slots(i) + ffn_slots(i, [0, 2]) for i in range(1, depth)]

    def group_of(i, j):
        return i + 1 if i else (0 if j == 0 else 1)

    lng_p, lnb_p = gather_weights([ln_g, ln_b])
    lng = jnp.moveaxis(lng_p, 0, 2).reshape(depth, 3, 1, D)
    lnb = jnp.moveaxis(lnb_p, 0, 2).reshape(depth, 3, 1, D)
    placed = [[cast_place(ids, src, lead, L) for src, lead, L, _, _ in grp] for grp in groups]
    in_flight, _ = gather_start([[s for s, _ in grp] for grp in placed],
                                [[z for _, z in grp] for grp in placed], lng_p)

    tab_t = jnp.pad(rel_bias.T, ((0, 0), (0, REL_PAD - N_REL)))
    vr = bias_vec(tab_t).reshape(HP, 2, VR_W)

    xf = x.reshape(M, D)
    xb = xf.astype(BF16)
    saved, weights, landed = [], {}, set()
    for i in range(depth):
        for j in range(3):
            gi = group_of(i, j)
            if gi not in landed:
                landed.add(gi)
                send, recv, thru, lands = in_flight[gi]
                _, lands = gather_wait(gi, send, recv, thru, lands, xf)
                for slot, arr in zip(groups[gi], gather_forward(gi, lands)):
                    for unit, role, l in slot[4]:
                        weights[unit, role] = (arr, l)
            gam, bet = lng[i, j], lnb[i, j]
            if j != 1:
                (wg, l), (wu, _), (wd, _) = (weights[(i, j), r] for r in ("gate", "up", "down"))
                g, u, h = ffn_up(xb, wg, wu, l)
                xo, xob, xhat, rstd = mm_ln(h, wd, l, xf, gam, bet, alpha, 0.5, True)
                saved.append(("ffn", i, l, xb, g, u, h, xhat, rstd, gam))
            else:
                wq, wo = weights[(i, 1), "q"][0], weights[(i, 1), "o"][0]
                qkv = qkv_proj(xb, wq, 0)
                if i % 2 == 0:
                    o, ntot = attn_a_fwd(qkv, vr, B, S), None
                else:
                    o, ntot = attn_b_fwd(qkv, B, S)
                xo, xob, xhat, rstd = mm_ln(o, wo, 0, xf, gam, bet, alpha, 1.0, False)
                saved.append(("a" if i % 2 == 0 else "b", i, 0, xb, qkv, o, ntot, xhat, rstd, gam))
            xf, xb = xo, xob

    dy, loss_part = loss_head(xf, loss_target.reshape(M, D))
    loss = lax.psum(loss_part[0, 0], ("x", "y", "c"))

    Cq, Ro = w_qkv_a.shape[-1], w_o_a.shape[1]
    dgam = lax.empty((3 * depth, 1, D), F32)
    dbet = lax.empty((3 * depth, 1, D), F32)
    dvrs, reducing = [], {}
    gbufs = [[lax.empty((L, N_CHIPS) + src.shape[-2:], BF16) for src, _, L, _, _ in grp] for grp in groups]
    grad_of = {(unit, role): (gi, si, l) for gi, grp in enumerate(groups) for si, slot in enumerate(grp)
               for unit, role, l in slot[4]}

    def ln_of(sub):
        rec = saved[sub]
        return rec[7], rec[8], rec[9], (0.5 if rec[0] == "ffn" else 1.0)

    no_token = jnp.zeros((8, 128), F32)
    token = no_token
    dpre, dyb, dgam, dbet = ln_bwd(dy, *ln_of(3 * depth - 1), dgam, dbet, 3 * depth - 1)
    for sub in reversed(range(3 * depth)):
        i, j = divmod(sub, 3)
        kind, _, l, xb_in, t1, t2, t3, _, _, _ = saved[sub]
        ln = (*ln_of(sub - 1), dgam, dbet, sub - 1) if sub else None
        if kind == "ffn":
            g, u, h = t1, t2, t3
            (gi, sg, _), (_, su, _), (_, sd, _) = (grad_of[(i, j), r] for r in ("gate", "up", "down"))
            wg, wu, wd = (weights[(i, j), r][0] for r in ("gate", "up", "down"))
            dg, du, gbufs[gi][sd], gbufs[gi][sg], gbufs[gi][su] = ffn_bwd(
                gbufs[gi][sd], gbufs[gi][sg], gbufs[gi][su], l, xb_in, dyb, wd, g, u, h, token)
            out = ffn_dx(dg, du, wg, wu, l, dpre, alpha, ln)
        else:
            qkv, o = t1, t2
            (gi, sq, _), (_, so, _) = grad_of[(i, j), "q"], grad_of[(i, j), "o"]
            wq, wo = weights[(i, j), "q"][0], weights[(i, j), "o"][0]
            do = o_proj_bwd(dyb, wo, l, token)
            gbufs[gi][so] = dw_o(gbufs[gi][so], l, o, dyb)
            if kind == "a":
                dqkv, dvr = attn_a_bwd(qkv, vr, do, B, S)
                dvrs.append(dvr.reshape(B, H, VR_W))
            else:
                dqkv = attn_b_bwd(qkv, do, t3, B, S)
            gbufs[gi][sq] = dw_qkv(gbufs[gi][sq], l, xb_in, dqkv)
            out = qkv_proj_bwd(dqkv, wq, l, dpre, alpha, ln)
        if ln is None:
            dy = out
        else:
            dpre, dyb, dgam, dbet = out
        token = no_token
        gi = group_of(i, j)
        if sub == 0 or group_of(*divmod(sub - 1, 3)) != gi:
            from_sib = pair_exchange_rows(gi, gbufs[gi])
            parts = [pair_add_rows(ids, g_, r_) for g_, r_ in zip(gbufs[gi], from_sib)]
            reducing[gi] = reduce_start(gi, parts)
            token = reducing[gi][4]
    grad_x = dy.reshape(B, S, D)

    la, lb = w_qkv_a.shape[0], w_qkv_b.shape[0]
    full = {"qa": lax.empty((la, D, Cq), F32), "oa": lax.empty((la, Ro, D), F32),
            "qb": lax.empty((lb, D, Cq), F32), "ob": lax.empty((lb, Ro, D), F32),
            "gate": lax.empty((n_ffn, Fs, D), F32), "up": lax.empty((n_ffn, Fs, D), F32),
            "down": lax.empty((n_ffn, Fs, D), F32)}
    for gi in reversed(range(len(groups))):
        send, recv, parts, lands, _ = reducing[gi]
        parts, lands = reduce_wait(gi, send, recv, parts, lands, dy)
        for (_, _, _, (name, l0), _), part, land in zip(groups[gi], parts, lands):
            full[name] = chip_sum_rows(ids, land, part, full[name], l0)
    g_qa, g_oa, g_qb, g_ob, g_gate, g_up, g_down = half_swap_rows(
        [full[k] for k in ("qa", "oa", "qb", "ob", "gate", "up", "down")])

    d_tab_t = bias_vec_bwd(jnp.concatenate(dvrs, axis=0))
    small = jnp.concatenate([d_tab_t.reshape(-1), dgam.reshape(-1), dbet.reshape(-1)])
    n_small = small.shape[0]
    rows = -(-n_small // (8 * 128)) * 8
    tot = all_sum_small(jnp.pad(small, (0, rows * 128 - n_small)).reshape(rows, 128)).reshape(-1)
    n_tab, n_ln = H * REL_PAD, 3 * depth * D
    g_rel = tot[:n_tab].reshape(H, REL_PAD)[:, :N_REL].T
    ln_cols = D // N_CHIPS

    def ln_shard(flat):
        return lax.dynamic_slice_in_dim(flat.reshape(depth, 3, D), me * ln_cols, ln_cols, axis=2)

    g_lng = ln_shard(tot[n_tab:n_tab + n_ln])
    g_lnb = ln_shard(tot[n_tab + n_ln:n_tab + 2 * n_ln])

    def upd(w, g, m, v):
        if w.ndim == 2:
            return tuple(a[0] for a in adamw(w[None], g, m[None], v[None]))
        return adamw(w, g, m, v)

    res = [
        upd(w_qkv_a, g_qa, m_w_qkv_a, v_w_qkv_a),
        upd(w_o_a, g_oa, m_w_o_a, v_w_o_a),
        upd(rel_bias, g_rel.reshape(1, N_REL, H), m_rel_bias, v_rel_bias),
        upd(w_qkv_b, g_qb, m_w_qkv_b, v_w_qkv_b),
        upd(w_o_b, g_ob, m_w_o_b, v_w_o_b),
        tuple(tr(a) for a in upd(gate_t, g_gate, tr(m_ffn_w_gate), tr(v_ffn_w_gate))),
        tuple(tr(a) for a in upd(up_t, g_up, tr(m_ffn_w_up), tr(v_ffn_w_up))),
        upd(ffn_w_down, g_down, m_ffn_w_down, v_ffn_w_down),
        upd(ln_g, g_lng, m_ln_g, v_ln_g),
        upd(ln_b, g_lnb, m_ln_b, v_ln_b),
    ]
    grads = [r[0] for r in res]
    deltas = [r[1] for r in res]
    new_m = [r[2] for r in res]
    new_v = [r[3] for r in res]
    return (loss, grad_x, *grads, *deltas, *new_m, *new_v)
```
